```python
import jax, jax.numpy as jnp
from jax import lax
import numpy as np

D_MODEL = 1024
BATCH = 8
SEQ = 8192
DEPTH = 2

GRID_W = 64
CTX_LEN = 256
N_MIXERS = 2
EXPAND = 2
D_INNER = EXPAND * D_MODEL
LRU_BLOCKS = 16
LRU_BLOCK = D_INNER // LRU_BLOCKS
CONV_W = 4
CONV_LEFT = 2
LRU_C = 8.0
POOL_WINDOWS = (2, 4, 8, 16)
N_POOL_GROUPS = len(POOL_WINDOWS)
POOL_GROUP = D_INNER // N_POOL_GROUPS
N_A_LAYERS = (DEPTH + 1) // 2
N_B_LAYERS = DEPTH // 2
ALPHA = float((2 * DEPTH) ** 0.25)
BETA = float((8 * DEPTH) ** -0.25)
LN_EPS = 1e-5

kernel_name = "hybrid_rglru_pool_deepnorm_prefix"


def _layer_norm(v, g, b):
    vf = v.astype(jnp.float32)
    mu = jnp.mean(vf, axis=-1, keepdims=True)
    var = jnp.mean(jnp.square(vf - mu), axis=-1, keepdims=True)
    y = (vf - mu) * lax.rsqrt(var + LN_EPS) * g.astype(jnp.float32) + b.astype(jnp.float32)
    return y.astype(v.dtype)


def _adaln(cvec, w_mod, b_mod):
    m = jax.nn.silu(cvec) @ w_mod + b_mod
    shift, scale, gate = jnp.split(m, 3, axis=-1)
    return shift, scale, gate


def _centred_dwconv(u, w, b):
    L = u.shape[1]
    up = jnp.pad(u, ((0, 0), (CONV_LEFT, CONV_W - 1 - CONV_LEFT), (0, 0)))
    out = up[:, 0:L] * w[0]
    for k in range(1, CONV_W):
        out = out + up[:, k:k + L] * w[k]
    return out + b


def _lru_coeffs(uf, wa, ba, wx, bx, lam):
    bn, L, _ = uf.shape
    ub = uf.reshape(bn, L, LRU_BLOCKS, LRU_BLOCK)
    r = jax.nn.sigmoid(jnp.einsum('blnh,nhk->blnk', ub, wa.astype(jnp.float32)).reshape(bn, L, D_INNER) + ba.astype(jnp.float32))
    i = jax.nn.sigmoid(jnp.einsum('blnh,nhk->blnk', ub, wx.astype(jnp.float32)).reshape(bn, L, D_INNER) + bx.astype(jnp.float32))
    log_a = LRU_C * r * jax.nn.log_sigmoid(lam.astype(jnp.float32))
    a = jnp.exp(log_a)
    drive = jnp.sqrt(-jnp.expm1(2.0 * log_a)) * (i * uf)
    return a, drive


def _linear_scan(a, b, h0, reverse):
    if h0 is not None:
        if reverse:
            b = b.at[:, -1].add(a[:, -1] * h0)
        else:
            b = b.at[:, 0].add(a[:, 0] * h0)

    def combine(left, right):
        a1, b1 = left
        a2, b2 = right
        return a1 * a2, a2 * b1 + b2

    _, h = lax.associative_scan(combine, (a, b), reverse=reverse, axis=1)
    return h


def _rglru(u, wa, ba, wx, bx, lam, h0_f, h0_b):
    uf = u.astype(jnp.float32)
    af, df = _lru_coeffs(uf, wa[0], ba[0], wx[0], bx[0], lam[0])
    ab, db = _lru_coeffs(uf, wa[1], ba[1], wx[1], bx[1], lam[1])
    h_fwd = _linear_scan(af, df, h0_f, False)
    h_bwd = _linear_scan(ab, db, h0_b, True)
    return h_fwd, h_bwd


def _window_mean(v, w, axis):
    n = v.shape[axis]
    cs = jnp.cumsum(v.astype(jnp.float32), axis=axis)
    pad = [(0, 0)] * v.ndim
    pad[axis] = (1, 0)
    cs = jnp.pad(cs, pad)
    t = np.arange(n)
    lo = np.clip(t - w // 2, 0, n)
    hi = np.clip(t + w // 2, 0, n)
    s = jnp.take(cs, hi, axis=axis) - jnp.take(cs, lo, axis=axis)
    cnt_shape = [1] * v.ndim
    cnt_shape[axis] = n
    cnt = jnp.asarray((hi - lo).astype(np.float32)).reshape(cnt_shape)
    return s / cnt


def _pool_grid(u, w_p, scale, rows):
    bn = u.shape[0]
    ug = u.reshape(bn, rows, GRID_W, D_INNER)
    outs = []
    for k, w in enumerate(POOL_WINDOWS):
        seg = ug[..., k * POOL_GROUP:(k + 1) * POOL_GROUP]
        m = _window_mean(_window_mean(seg, w, 2), w, 1)
        d = (m - seg.astype(jnp.float32)).astype(u.dtype)
        outs.append(jnp.einsum('brcg,gh->brch', d, w_p[k]))
    y = jnp.concatenate(outs, axis=-1).reshape(bn, rows * GRID_W, D_INNER)
    return y * scale


def _pool_seq(u, w_p, scale):
    outs = []
    for k, w in enumerate(POOL_WINDOWS):
        seg = u[..., k * POOL_GROUP:(k + 1) * POOL_GROUP]
        d = (_window_mean(seg, w, 1) - seg.astype(jnp.float32)).astype(u.dtype)
        outs.append(jnp.einsum('blg,gh->blh', d, w_p[k]))
    return jnp.concatenate(outs, axis=-1) * scale


def _ctx_needed_after(i):
    return any(j % N_MIXERS == 0 for j in range(i + 1, DEPTH))


def _fwd_setup_inputs(seed: int = 0) -> dict:
    key = jax.random.key(seed)
    ks = jax.random.split(key, 20)
    f32 = jnp.float32
    x = jax.random.normal(ks[0], (BATCH, SEQ, D_MODEL), f32)
    c = jax.random.normal(ks[1], (BATCH, D_MODEL), f32)
    ctx = jax.random.normal(ks[2], (BATCH, CTX_LEN, D_MODEL), f32)
    c_ctx = jax.random.normal(ks[3], (D_MODEL,), f32)
    w_mod = jax.random.normal(ks[4], (DEPTH, D_MODEL, 3 * D_MODEL), f32) * (0.5 * D_MODEL ** -0.5)
    b_mod = jax.random.normal(ks[5], (DEPTH, 3 * D_MODEL), f32) * 0.02
    w_in = jax.random.normal(ks[6], (DEPTH, D_MODEL, 2 * D_INNER), f32) * D_MODEL ** -0.5
    w_out = jax.random.normal(ks[7], (DEPTH, D_INNER, D_MODEL), f32) * (D_INNER ** -0.5 * BETA)
    ln_g = 1.0 + 0.02 * jax.random.normal(ks[8], (DEPTH, D_MODEL), f32)
    ln_b = 0.02 * jax.random.normal(ks[9], (DEPTH, D_MODEL), f32)
    conv_w = jax.random.normal(ks[10], (N_A_LAYERS, CONV_W, D_INNER), f32) * CONV_W ** -0.5
    conv_b = 0.02 * jax.random.normal(ks[11], (N_A_LAYERS, D_INNER), f32)
    lru_wa = jax.random.normal(ks[12], (N_A_LAYERS, 2, LRU_BLOCKS, LRU_BLOCK, LRU_BLOCK), f32) * LRU_BLOCK ** -0.5
    lru_ba = 0.02 * jax.random.normal(ks[13], (N_A_LAYERS, 2, D_INNER), f32)
    lru_wx = jax.random.normal(ks[14], (N_A_LAYERS, 2, LRU_BLOCKS, LRU_BLOCK, LRU_BLOCK), f32) * LRU_BLOCK ** -0.5
    lru_bx = 0.02 * jax.random.normal(ks[15], (N_A_LAYERS, 2, D_INNER), f32)
    a_pow_c = jax.random.uniform(ks[16], (N_A_LAYERS, 2, D_INNER), f32, minval=0.9, maxval=0.999)
    a0 = a_pow_c ** (1.0 / LRU_C)
    lru_lam = jnp.log(a0) - jnp.log1p(-a0)
    pool_w = jax.random.normal(ks[17], (N_B_LAYERS, N_POOL_GROUPS, POOL_GROUP, POOL_GROUP), f32) * POOL_GROUP ** -0.5
    pool_scale = 1.0 + 0.02 * jax.random.normal(ks[18], (N_B_LAYERS, D_INNER), f32)
    return {"x": x, "c": c, "ctx": ctx, "c_ctx": c_ctx, "w_mod": w_mod, "b_mod": b_mod,
            "w_in": w_in, "w_out": w_out, "ln_g": ln_g, "ln_b": ln_b,
            "conv_w": conv_w, "conv_b": conv_b, "lru_wa": lru_wa, "lru_ba": lru_ba,
            "lru_wx": lru_wx, "lru_bx": lru_bx, "lru_lam": lru_lam,
            "pool_w": pool_w, "pool_scale": pool_scale}


def _fwd_reference(x, c, ctx, c_ctx, w_mod, b_mod, w_in, w_out, ln_g, ln_b,
              conv_w, conv_b, lru_wa, lru_ba, lru_wx, lru_bx, lru_lam,
              pool_w, pool_scale):
    rows = x.shape[1] // GRID_W
    xc = ctx
    for i in range(DEPTH):
        kind = i % N_MIXERS
        j = i // N_MIXERS
        ctx_out = _ctx_needed_after(i)
        sh, sc, gt = _adaln(c, w_mod[i], b_mod[i])
        h = x * (1.0 + sc[:, None]) + sh[:, None]
        u, g = jnp.split(h @ w_in[i], 2, axis=-1)
        if kind == 0 or ctx_out:
            shc, scc, gtc = _adaln(c_ctx, w_mod[i], b_mod[i])
            hc = xc * (1.0 + scc) + shc
            if ctx_out:
                uc, gc = jnp.split(hc @ w_in[i], 2, axis=-1)
            else:
                uc = hc @ w_in[i][:, :D_INNER]
        if kind == 0:
            uc = _centred_dwconv(uc, conv_w[j], conv_b[j])
            hcf, hcb = _rglru(uc, lru_wa[j], lru_ba[j], lru_wx[j], lru_bx[j], lru_lam[j], None, None)
            u = _centred_dwconv(u, conv_w[j], conv_b[j])
            hf, hb = _rglru(u, lru_wa[j], lru_ba[j], lru_wx[j], lru_bx[j], lru_lam[j],
                            hcf[:, -1], hcb[:, 0])
            y = (hf + hb).astype(x.dtype)
            if ctx_out:
                yc = (hcf + hcb).astype(xc.dtype)
        else:
            y = _pool_grid(u, pool_w[j], pool_scale[j], rows)
            if ctx_out:
                yc = _pool_seq(uc, pool_w[j], pool_scale[j])
        branch = (y * jax.nn.silu(g)) @ w_out[i]
        x = _layer_norm(ALPHA * x + gt[:, None] * branch, ln_g[i], ln_b[i])
        if ctx_out:
            branch_c = (yc * jax.nn.silu(gc)) @ w_out[i]
            xc = _layer_norm(ALPHA * xc + gtc * branch_c, ln_g[i], ln_b[i])
    return x


import jax as _jax
import jax.numpy as _jnp

TWIN_FORMAT = 'train_step'
FWD_PARAMS = ['x', 'c', 'ctx', 'c_ctx', 'w_mod', 'b_mod', 'w_in', 'w_out', 'ln_g', 'ln_b', 'conv_w', 'conv_b', 'lru_wa', 'lru_ba', 'lru_wx', 'lru_bx', 'lru_lam', 'pool_w', 'pool_scale']
TWIN_WEIGHTS = ['c_ctx', 'w_mod', 'b_mod', 'w_in', 'w_out', 'ln_g', 'ln_b', 'conv_w', 'conv_b', 'lru_wa', 'lru_ba', 'lru_wx', 'lru_bx', 'lru_lam', 'pool_w', 'pool_scale']
TWIN_DIFF_INPUT = 'x'
TWIN_INPUTS = ['x', 'c', 'ctx', 'c_ctx', 'w_mod', 'b_mod', 'w_in', 'w_out', 'ln_g', 'ln_b', 'conv_w', 'conv_b', 'lru_wa', 'lru_ba', 'lru_wx', 'lru_bx', 'lru_lam', 'pool_w', 'pool_scale', 'loss_target', 'm_c_ctx', 'm_w_mod', 'm_b_mod', 'm_w_in', 'm_w_out', 'm_ln_g', 'm_ln_b', 'm_conv_w', 'm_conv_b', 'm_lru_wa', 'm_lru_ba', 'm_lru_wx', 'm_lru_bx', 'm_lru_lam', 'm_pool_w', 'm_pool_scale', 'v_c_ctx', 'v_w_mod', 'v_b_mod', 'v_w_in', 'v_w_out', 'v_ln_g', 'v_ln_b', 'v_conv_w', 'v_conv_b', 'v_lru_wa', 'v_lru_ba', 'v_lru_wx', 'v_lru_bx', 'v_lru_lam', 'v_pool_w', 'v_pool_scale']
TWIN_OUTPUTS = ['loss', 'grad_x', 'grad_c_ctx', 'grad_w_mod', 'grad_b_mod', 'grad_w_in', 'grad_w_out', 'grad_ln_g', 'grad_ln_b', 'grad_conv_w', 'grad_conv_b', 'grad_lru_wa', 'grad_lru_ba', 'grad_lru_wx', 'grad_lru_bx', 'grad_lru_lam', 'grad_pool_w', 'grad_pool_scale', 'delta_c_ctx', 'delta_w_mod', 'delta_b_mod', 'delta_w_in', 'delta_w_out', 'delta_ln_g', 'delta_ln_b', 'delta_conv_w', 'delta_conv_b', 'delta_lru_wa', 'delta_lru_ba', 'delta_lru_wx', 'delta_lru_bx', 'delta_lru_lam', 'delta_pool_w', 'delta_pool_scale', 'new_m_c_ctx', 'new_m_w_mod', 'new_m_b_mod', 'new_m_w_in', 'new_m_w_out', 'new_m_ln_g', 'new_m_ln_b', 'new_m_conv_w', 'new_m_conv_b', 'new_m_lru_wa', 'new_m_lru_ba', 'new_m_lru_wx', 'new_m_lru_bx', 'new_m_lru_lam', 'new_m_pool_w', 'new_m_pool_scale', 'new_v_c_ctx', 'new_v_w_mod', 'new_v_b_mod', 'new_v_w_in', 'new_v_w_out', 'new_v_ln_g', 'new_v_ln_b', 'new_v_conv_w', 'new_v_conv_b', 'new_v_lru_wa', 'new_v_lru_ba', 'new_v_lru_wx', 'new_v_lru_bx', 'new_v_lru_lam', 'new_v_pool_w', 'new_v_pool_scale']
TWIN_LEAF_KINDS = {'loss': 'loss', 'grad_x': 'grad_x', 'grad_c_ctx': 'grad_w', 'grad_w_mod': 'grad_w', 'grad_b_mod': 'grad_w', 'grad_w_in': 'grad_w', 'grad_w_out': 'grad_w', 'grad_ln_g': 'grad_w', 'grad_ln_b': 'grad_w', 'grad_conv_w': 'grad_w', 'grad_conv_b': 'grad_w', 'grad_lru_wa': 'grad_w', 'grad_lru_ba': 'grad_w', 'grad_lru_wx': 'grad_w', 'grad_lru_bx': 'grad_w', 'grad_lru_lam': 'grad_w', 'grad_pool_w': 'grad_w', 'grad_pool_scale': 'grad_w', 'delta_c_ctx': 'delta_w', 'delta_w_mod': 'delta_w', 'delta_b_mod': 'delta_w', 'delta_w_in': 'delta_w', 'delta_w_out': 'delta_w', 'delta_ln_g': 'delta_w', 'delta_ln_b': 'delta_w', 'delta_conv_w': 'delta_w', 'delta_conv_b': 'delta_w', 'delta_lru_wa': 'delta_w', 'delta_lru_ba': 'delta_w', 'delta_lru_wx': 'delta_w', 'delta_lru_bx': 'delta_w', 'delta_lru_lam': 'delta_w', 'delta_pool_w': 'delta_w', 'delta_pool_scale': 'delta_w', 'new_m_c_ctx': 'new_m', 'new_m_w_mod': 'new_m', 'new_m_b_mod': 'new_m', 'new_m_w_in': 'new_m', 'new_m_w_out': 'new_m', 'new_m_ln_g': 'new_m', 'new_m_ln_b': 'new_m', 'new_m_conv_w': 'new_m', 'new_m_conv_b': 'new_m', 'new_m_lru_wa': 'new_m', 'new_m_lru_ba': 'new_m', 'new_m_lru_wx': 'new_m', 'new_m_lru_bx': 'new_m', 'new_m_lru_lam': 'new_m', 'new_m_pool_w': 'new_m', 'new_m_pool_scale': 'new_m', 'new_v_c_ctx': 'new_v', 'new_v_w_mod': 'new_v', 'new_v_b_mod': 'new_v', 'new_v_w_in': 'new_v', 'new_v_w_out': 'new_v', 'new_v_ln_g': 'new_v', 'new_v_ln_b': 'new_v', 'new_v_conv_w': 'new_v', 'new_v_conv_b': 'new_v', 'new_v_lru_wa': 'new_v', 'new_v_lru_ba': 'new_v', 'new_v_lru_wx': 'new_v', 'new_v_lru_bx': 'new_v', 'new_v_lru_lam': 'new_v', 'new_v_pool_w': 'new_v', 'new_v_pool_scale': 'new_v'}


def _forward(args):
    return _fwd_reference(*[args[k] for k in FWD_PARAMS])


def _output_shape():
    def fwd():
        inp = _fwd_setup_inputs(0)
        return _fwd_reference(*[inp[k] for k in FWD_PARAMS])
    out = _jax.eval_shape(fwd)
    return out.shape, out.dtype

N_MICROBATCH = 1
ADAM_LR = 0.001
ADAM_B1 = 0.9
ADAM_B2 = 0.999
ADAM_EPS = 1e-08
ADAM_WD = 0.01
ADAM_STEP = 10
PER_EXAMPLE_BATCH_AXIS = {'x': 0, 'c': 0, 'ctx': 0, 'loss_target': 0}
SHARED_INPUTS = []
_WEIGHT_DTYPES = {'c_ctx': _jnp.float32, 'w_mod': _jnp.float32, 'b_mod': _jnp.float32, 'w_in': _jnp.float32, 'w_out': _jnp.float32, 'ln_g': _jnp.float32, 'ln_b': _jnp.float32, 'conv_w': _jnp.float32, 'conv_b': _jnp.float32, 'lru_wa': _jnp.float32, 'lru_ba': _jnp.float32, 'lru_wx': _jnp.float32, 'lru_bx': _jnp.float32, 'lru_lam': _jnp.float32, 'pool_w': _jnp.float32, 'pool_scale': _jnp.float32}
MOMENT_SCALE = {'c_ctx': 1.796835e-02, 'w_mod': 9.167257e-02, 'b_mod': 1.462061e-01, 'w_in': 3.742374e-02, 'w_out': 1.067183e-01, 'ln_g': 4.533816e+01, 'ln_b': 2.015673e+00, 'conv_w': 5.671447e-02, 'conv_b': 1.645312e-01, 'lru_wa': 3.090405e-03, 'lru_ba': 4.884096e-03, 'lru_wx': 6.092276e-03, 'lru_bx': 1.143255e-02, 'lru_lam': 1.287639e-02, 'pool_w': 1.254150e-02, 'pool_scale': 1.349908e-02}


def _to_microbatches(a, axis):
    t = _jnp.moveaxis(a, axis, 0)
    t = t.reshape((N_MICROBATCH, t.shape[0] // N_MICROBATCH) + t.shape[1:])
    return _jnp.moveaxis(t, 1, axis + 1)


def setup_inputs(seed: int = 0) -> dict:
    inp = _fwd_setup_inputs(seed)
    key = _jax.random.fold_in(_jax.random.key(seed), 7919)
    shape, _ = _output_shape()
    out = dict(inp)
    out["loss_target"] = _jax.random.normal(_jax.random.fold_in(key, 0), shape, _jnp.float32)
    for i, name in enumerate(TWIN_WEIGHTS):
        w = inp[name].astype(_jnp.float32)
        if MOMENT_SCALE is None:
            s = _jnp.sqrt(_jnp.mean(_jnp.square(w)) + 1e-30)
        else:
            s = MOMENT_SCALE[name]
        km, kv = _jax.random.split(_jax.random.fold_in(key, i + 1))
        out[name] = w
        out["m_" + name] = s * _jax.random.normal(km, w.shape, _jnp.float32)
        out["v_" + name] = (s * s) * _jax.random.uniform(kv, w.shape, _jnp.float32, 0.5, 1.5)
    if N_MICROBATCH > 1:
        for name, axis in PER_EXAMPLE_BATCH_AXIS.items():
            out[name] = _to_microbatches(out[name], axis)
    return {'x': out['x'], 'c': out['c'], 'ctx': out['ctx'], 'c_ctx': out['c_ctx'], 'w_mod': out['w_mod'], 'b_mod': out['b_mod'], 'w_in': out['w_in'], 'w_out': out['w_out'], 'ln_g': out['ln_g'], 'ln_b': out['ln_b'], 'conv_w': out['conv_w'], 'conv_b': out['conv_b'], 'lru_wa': out['lru_wa'], 'lru_ba': out['lru_ba'], 'lru_wx': out['lru_wx'], 'lru_bx': out['lru_bx'], 'lru_lam': out['lru_lam'], 'pool_w': out['pool_w'], 'pool_scale': out['pool_scale'], 'loss_target': out['loss_target'], 'm_c_ctx': out['m_c_ctx'], 'm_w_mod': out['m_w_mod'], 'm_b_mod': out['m_b_mod'], 'm_w_in': out['m_w_in'], 'm_w_out': out['m_w_out'], 'm_ln_g': out['m_ln_g'], 'm_ln_b': out['m_ln_b'], 'm_conv_w': out['m_conv_w'], 'm_conv_b': out['m_conv_b'], 'm_lru_wa': out['m_lru_wa'], 'm_lru_ba': out['m_lru_ba'], 'm_lru_wx': out['m_lru_wx'], 'm_lru_bx': out['m_lru_bx'], 'm_lru_lam': out['m_lru_lam'], 'm_pool_w': out['m_pool_w'], 'm_pool_scale': out['m_pool_scale'], 'v_c_ctx': out['v_c_ctx'], 'v_w_mod': out['v_w_mod'], 'v_b_mod': out['v_b_mod'], 'v_w_in': out['v_w_in'], 'v_w_out': out['v_w_out'], 'v_ln_g': out['v_ln_g'], 'v_ln_b': out['v_ln_b'], 'v_conv_w': out['v_conv_w'], 'v_conv_b': out['v_conv_b'], 'v_lru_wa': out['v_lru_wa'], 'v_lru_ba': out['v_lru_ba'], 'v_lru_wx': out['v_lru_wx'], 'v_lru_bx': out['v_lru_bx'], 'v_lru_lam': out['v_lru_lam'], 'v_pool_w': out['v_pool_w'], 'v_pool_scale': out['v_pool_scale']}


def _loss(weights, diff, rest, loss_target):
    with _jax.named_scope("forward"):
        args = {**rest, TWIN_DIFF_INPUT: diff, **{k: w.astype(_WEIGHT_DTYPES[k]) for k, w in weights.items()}}
        y = _forward(args)
    with _jax.named_scope("loss_head"):
        err = _jnp.square(y.astype(_jnp.float32) - loss_target)
        return 0.5 * _jnp.sum(_jnp.mean(err, axis=-1)) if err.ndim else 0.5 * err


def _adamw(w, g, m, v):
    m = ADAM_B1 * m + (1.0 - ADAM_B1) * g
    v = ADAM_B2 * v + (1.0 - ADAM_B2) * _jnp.square(g)
    m_hat = m / (1.0 - ADAM_B1 ** ADAM_STEP)
    v_hat = v / (1.0 - ADAM_B2 ** ADAM_STEP)
    delta = -ADAM_LR * (m_hat / (_jnp.sqrt(v_hat) + ADAM_EPS) + ADAM_WD * w)
    return delta, m, v


def reference(x, c, ctx, c_ctx, w_mod, b_mod, w_in, w_out, ln_g, ln_b, conv_w, conv_b, lru_wa, lru_ba, lru_wx, lru_bx, lru_lam, pool_w, pool_scale, loss_target, m_c_ctx, m_w_mod, m_b_mod, m_w_in, m_w_out, m_ln_g, m_ln_b, m_conv_w, m_conv_b, m_lru_wa, m_lru_ba, m_lru_wx, m_lru_bx, m_lru_lam, m_pool_w, m_pool_scale, v_c_ctx, v_w_mod, v_b_mod, v_w_in, v_w_out, v_ln_g, v_ln_b, v_conv_w, v_conv_b, v_lru_wa, v_lru_ba, v_lru_wx, v_lru_bx, v_lru_lam, v_pool_w, v_pool_scale):
    given = dict(x=x, c=c, ctx=ctx, c_ctx=c_ctx, w_mod=w_mod, b_mod=b_mod, w_in=w_in, w_out=w_out, ln_g=ln_g, ln_b=ln_b, conv_w=conv_w, conv_b=conv_b, lru_wa=lru_wa, lru_ba=lru_ba, lru_wx=lru_wx, lru_bx=lru_bx, lru_lam=lru_lam, pool_w=pool_w, pool_scale=pool_scale, loss_target=loss_target, m_c_ctx=m_c_ctx, m_w_mod=m_w_mod, m_b_mod=m_b_mod, m_w_in=m_w_in, m_w_out=m_w_out, m_ln_g=m_ln_g, m_ln_b=m_ln_b, m_conv_w=m_conv_w, m_conv_b=m_conv_b, m_lru_wa=m_lru_wa, m_lru_ba=m_lru_ba, m_lru_wx=m_lru_wx, m_lru_bx=m_lru_bx, m_lru_lam=m_lru_lam, m_pool_w=m_pool_w, m_pool_scale=m_pool_scale, v_c_ctx=v_c_ctx, v_w_mod=v_w_mod, v_b_mod=v_b_mod, v_w_in=v_w_in, v_w_out=v_w_out, v_ln_g=v_ln_g, v_ln_b=v_ln_b, v_conv_w=v_conv_w, v_conv_b=v_conv_b, v_lru_wa=v_lru_wa, v_lru_ba=v_lru_ba, v_lru_wx=v_lru_wx, v_lru_bx=v_lru_bx, v_lru_lam=v_lru_lam, v_pool_w=v_pool_w, v_pool_scale=v_pool_scale)
    weights = {n: given[n] for n in TWIN_WEIGHTS}
    shared = {n: given[n] for n in SHARED_INPUTS}
    per_example = {n: given[n] for n in ['x', 'c', 'ctx']}
    grad_fn = _jax.value_and_grad(_loss, argnums=(0, 1))

    def one_microbatch(ex, loss_target):
        ex = dict(ex)
        diff = ex.pop(TWIN_DIFF_INPUT)
        return grad_fn(weights, diff, {**shared, **ex}, loss_target)

    if N_MICROBATCH == 1:
        loss, (grad_w, grad_x) = one_microbatch(per_example, given["loss_target"])
    else:
        def body(carry, xs):
            loss_sum, grad_sum = carry
            l_k, (gw_k, gx_k) = one_microbatch(xs[0], xs[1])
            with _jax.named_scope("update"):
                return (loss_sum + l_k, _jax.tree.map(_jnp.add, grad_sum, gw_k)), gx_k

        init = (_jnp.zeros((), _jnp.float32), _jax.tree.map(_jnp.zeros_like, weights))
        (loss, grad_w), grad_x = _jax.lax.scan(body, init, (per_example, given["loss_target"]))
    with _jax.named_scope("update"):
        delta_w, new_m, new_v = {}, {}, {}
        for n in TWIN_WEIGHTS:
            delta_w[n], new_m[n], new_v[n] = _adamw(weights[n], grad_w[n], given["m_" + n], given["v_" + n])
    return (loss, grad_x, *[grad_w[n] for n in TWIN_WEIGHTS], *[delta_w[n] for n in TWIN_WEIGHTS],
            *[new_m[n] for n in TWIN_WEIGHTS], *[new_v[n] for n in TWIN_WEIGHTS])
```

```python
import functools

import jax
import jax.numpy as jnp
from jax import lax
from jax.experimental import pallas as pl
from jax.experimental.pallas import tpu as pltpu

F32 = jnp.float32
BF16 = jnp.bfloat16
LANE = 128
SUB = 8
GRID_W = 64
POOL_WINDOWS = (2, 4, 8, 16)
LRU_C = 8.0
DEPTH = 2
ALPHA = float((2 * DEPTH) ** 0.25)
LN_EPS = 1e-5
ADAM_LR, ADAM_B1, ADAM_B2, ADAM_EPS, ADAM_WD, ADAM_STEP = 0.001, 0.9, 0.999, 1e-08, 0.01, 10
N_CHIPS = 4
MESH = pl.DeviceIdType.MESH
ROW_TILE = 512
CONV_TAPS = 4
CONV_LEFT = 2
PAD = 8


def _call(body, **kw):
    return pl.pallas_call(body, **kw)


def _dot(a, b):
    return jnp.dot(a, b, preferred_element_type=F32)


def _dot_nt(a, b):
    return lax.dot_general(a, b, (((1,), (1,)), ((), ())), preferred_element_type=F32)


def _dot_tn(a, b):
    return lax.dot_general(a, b, (((0,), (0,)), ((), ())), preferred_element_type=F32)


def _sigmoid(v):
    return 0.5 * (jnp.tanh(0.5 * v) + 1.0)


def _silu(v):
    return v * _sigmoid(v)


def _dsilu(v):
    s = _sigmoid(v)
    return s * (1.0 + v * (1.0 - s))


def _log_sigmoid(v):
    z = jnp.exp(-jnp.abs(v))
    return jnp.minimum(v, 0.0) - jnp.where(z < 1e-4, z * (1.0 - 0.5 * z), jnp.log(1.0 + z))


def _neg_expm1(t):
    p = t * (1.0 + t * (1.0 / 2 + t * (1.0 / 6 + t * (1.0 / 24 + t * (1.0 / 120 + t * (1.0 / 720))))))
    return jnp.where(t > -0.25, -p, 1.0 - jnp.exp(t))


def _cat(ref, n):
    return jnp.concatenate([ref[k] for k in range(n)], axis=1)


def _put_chunks(ref, val, n, base=0):
    for k in range(n):
        ref[base + k] = val[:, k * LANE:(k + 1) * LANE].astype(ref.dtype)


def _row_tile(rows, want):
    t = min(rows, want)
    assert rows % t == 0
    return t


def _mod_fwd(cc, wm, bm):
    ns, nl, d, c3 = wm.shape

    def body(cc_ref, w_ref, b_ref, o_ref):
        o_ref[...] = _dot(_silu(cc_ref[...]).astype(BF16), w_ref[...]) + b_ref[...]

    return _call(
        body, name="mod_fwd", grid=(nl, ns),
        in_specs=[pl.BlockSpec((8, d), lambda l, s: (0, 0)),
                  pl.BlockSpec((None, None, d, c3), lambda l, s: (s, l, 0, 0)),
                  pl.BlockSpec((None, 1, c3), lambda l, s: (l, 0, s))],
        out_specs=pl.BlockSpec((None, 8, c3), lambda l, s: (l, 0, s)),
        out_shape=jax.ShapeDtypeStruct((nl, 8, ns * c3), F32),
    )(cc, wm, bm)


def _mod_bwd(cc, dm, wm):
    ns, nl, d, c3 = wm.shape

    def body(cc_ref, dm_ref, w_ref, dw_ref, db_ref, dcc_ref):
        l, s = pl.program_id(0), pl.program_id(1)
        cv = cc_ref[...]
        dmb = dm_ref[...].astype(BF16)
        dw_ref[...] = _dot_tn(_silu(cv).astype(BF16), dmb)
        db_ref[...] = jnp.sum(dm_ref[...], axis=0, keepdims=True)

        @pl.when((l == 0) & (s == 0))
        def _():
            dcc_ref[...] = jnp.zeros_like(dcc_ref)

        @pl.when(l == 0)
        def _():
            dcc_ref[...] += _dot_nt(dmb, w_ref[...]) * _dsilu(cv)

    return _call(
        body, name="mod_bwd", grid=(nl, ns),
        in_specs=[pl.BlockSpec((8, d), lambda l, s: (0, 0)),
                  pl.BlockSpec((None, 8, c3), lambda l, s: (l, 0, s)),
                  pl.BlockSpec((None, None, d, c3), lambda l, s: (s, 0, 0, 0))],
        out_specs=[pl.BlockSpec((None, d, c3), lambda l, s: (l, 0, s)),
                   pl.BlockSpec((None, 1, c3), lambda l, s: (l, 0, s)),
                   pl.BlockSpec((8, d), lambda l, s: (0, 0))],
        out_shape=[jax.ShapeDtypeStruct((nl, d, ns * c3), F32),
                   jax.ShapeDtypeStruct((nl, 1, ns * c3), F32),
                   jax.ShapeDtypeStruct((8, d), F32)],
    )(cc, dm, wm)


def _inproj_fwd(xin, sc1, sh, w, name):
    rows, d = xin.shape
    ns, _, n4 = w.shape
    cpb = n4 // LANE
    tm = _row_tile(rows, 256)

    def body(x_ref, sc_ref, sh_ref, w_ref, o_ref):
        h = (x_ref[...] * sc_ref[...] + sh_ref[...]).astype(BF16)
        for s in range(ns):
            _put_chunks(o_ref, _dot(h, w_ref[s]), cpb, base=s * cpb)

    return _call(
        body, name=name, grid=(rows // tm,),
        in_specs=[pl.BlockSpec((tm, d), lambda i: (i, 0)),
                  pl.BlockSpec((1, d), lambda i: (0, 0)),
                  pl.BlockSpec((1, d), lambda i: (0, 0)),
                  pl.BlockSpec((ns, d, n4), lambda i: (0, 0, 0))],
        out_specs=pl.BlockSpec((ns * cpb, tm, LANE), lambda i: (0, i, 0)),
        out_shape=jax.ShapeDtypeStruct((ns * cpb, rows, LANE), F32),
    )(xin, sc1, sh, w)


def _inproj_bwd_x(dparts, xin, dxres, sc1, w, name):
    rows, d = xin.shape
    npart = len(dparts)
    e = dparts[0].shape[1]
    ns, _, n4 = w.shape
    per = e // n4
    assert per * npart == ns
    tm = _row_tile(rows, 256)
    has_res = dxres is not None

    def body(*refs):
        dp = refs[:npart]
        x_ref, sc_ref, w_ref = refs[npart:npart + 3]
        rest = refs[npart + 3:]
        if has_res:
            res_ref, dx_ref, dsc_ref, dsh_ref = rest
        else:
            dsc_ref, dsh_ref = rest
        i = pl.program_id(0)
        dh = jnp.zeros((tm, d), F32)
        for p in range(npart):
            v = dp[p][...]
            for q in range(per):
                dh = dh + _dot_nt(v[:, q * n4:(q + 1) * n4], w_ref[p * per + q])

        @pl.when(i == 0)
        def _():
            dsc_ref[...] = jnp.zeros_like(dsc_ref)
            dsh_ref[...] = jnp.zeros_like(dsh_ref)

        dsc_ref[...] += jnp.sum(dh * x_ref[...], axis=0, keepdims=True)
        dsh_ref[...] += jnp.sum(dh, axis=0, keepdims=True)
        if has_res:
            dx_ref[...] = res_ref[...] + dh * sc_ref[...]

    row_spec = pl.BlockSpec((tm, d), lambda i: (i, 0))
    vec_spec = pl.BlockSpec((1, d), lambda i: (0, 0))
    in_specs = [pl.BlockSpec((tm, e), lambda i: (i, 0))] * npart + [row_spec, vec_spec,
                                                                     pl.BlockSpec((ns, d, n4), lambda i: (0, 0, 0))]
    args = list(dparts) + [xin, sc1, w]
    out_specs, out_shape = [vec_spec, vec_spec], [jax.ShapeDtypeStruct((1, d), F32)] * 2
    if has_res:
        in_specs.append(row_spec)
        args.append(dxres)
        out_specs = [row_spec] + out_specs
        out_shape = [jax.ShapeDtypeStruct((rows, d), F32)] + out_shape
    return _call(body, name=name, grid=(rows // tm,), in_specs=in_specs, out_specs=out_specs, out_shape=out_shape)(*args)


def _inproj_bwd_w(xin, sc1, sh, dparts, init, name):
    rows, d = xin.shape
    npart = len(dparts)
    e = dparts[0].shape[1]
    n4 = e // 2
    ns = 2 * npart
    tm = _row_tile(rows, 512)
    nt = rows // tm
    has_init = init is not None

    def body(*refs):
        x_ref, sc_ref, sh_ref = refs[:3]
        dp = refs[3:3 + npart]
        if has_init:
            init_ref, o_ref = refs[3 + npart:]
        else:
            (o_ref,) = refs[3 + npart:]
        s, i = pl.program_id(0), pl.program_id(1)
        h = (x_ref[...] * sc_ref[...] + sh_ref[...]).astype(BF16)

        @pl.when(i == 0)
        def _():
            o_ref[...] = init_ref[...] if has_init else jnp.zeros_like(o_ref)

        for p in range(npart):
            @pl.when(s // 2 == p)
            def _(p=p):
                o_ref[...] += _dot_tn(h, dp[p][...])

    in_specs = [pl.BlockSpec((tm, d), lambda s, i: (i, 0)),
                pl.BlockSpec((1, d), lambda s, i: (0, 0)),
                pl.BlockSpec((1, d), lambda s, i: (0, 0))]
    in_specs += [pl.BlockSpec((tm, n4), lambda s, i: (i, s % 2))] * npart
    args = [xin, sc1, sh] + list(dparts)
    o_spec = pl.BlockSpec((None, d, n4), lambda s, i: (s, 0, 0))
    if has_init:
        in_specs.append(o_spec)
        args.append(init)
    return _call(body, name=name, grid=(ns, nt), in_specs=in_specs, out_specs=o_spec,
                 out_shape=jax.ShapeDtypeStruct((ns, d, n4), F32))(*args)


def _ln_stats(r):
    mu = jnp.mean(r, axis=-1, keepdims=True)
    var = jnp.mean(jnp.square(r - mu), axis=-1, keepdims=True)
    rstd = lax.rsqrt(var + LN_EPS)
    return (r - mu) * rstd, rstd


def _outproj_fwd(y, ug, xin, gt, wout, lg, lb, target, name):
    nch, rows, _ = y.shape
    e, d = wout.shape
    tm = _row_tile(rows, 256)
    with_loss = target is not None

    def body(*refs):
        y_ref, g_ref, x_ref, gt_ref, w_ref, lg_ref, lb_ref = refs[:7]
        if with_loss:
            t_ref, br_ref, dxo_ref, loss_ref = refs[7:]
        else:
            br_ref, xo_ref = refs[7:]
        z = jnp.concatenate([(y_ref[k] * _silu(g_ref[k])).astype(BF16) for k in range(nch)], axis=1)
        br = _dot(z, w_ref[...])
        br_ref[...] = br
        xhat, _ = _ln_stats(ALPHA * x_ref[...] + gt_ref[...] * br)
        xo = xhat * lg_ref[...] + lb_ref[...]
        if with_loss:
            err = xo - t_ref[...]
            dxo_ref[...] = err * (1.0 / d)

            @pl.when(pl.program_id(0) == 0)
            def _():
                loss_ref[...] = jnp.zeros_like(loss_ref)

            loss_ref[...] += jnp.sum(err * err)
        else:
            xo_ref[...] = xo

    chunk_spec = pl.BlockSpec((nch, tm, LANE), lambda i: (0, i, 0))
    g_spec = pl.BlockSpec((nch, tm, LANE), lambda i: (1, i, 0))
    row_spec = pl.BlockSpec((tm, d), lambda i: (i, 0))
    vec_spec = pl.BlockSpec((1, d), lambda i: (0, 0))
    in_specs = [chunk_spec, g_spec, row_spec, vec_spec, pl.BlockSpec((e, d), lambda i: (0, 0)), vec_spec, vec_spec]
    args = [y, ug, xin, gt, wout, lg, lb]
    out_specs = [row_spec, row_spec]
    out_shape = [jax.ShapeDtypeStruct((rows, d), F32)] * 2
    if with_loss:
        in_specs.append(row_spec)
        args.append(target)
        out_specs.append(pl.BlockSpec((1, LANE), lambda i: (0, 0)))
        out_shape.append(jax.ShapeDtypeStruct((1, LANE), F32))
    return _call(body, name=name, grid=(rows // tm,), in_specs=in_specs, out_specs=out_specs, out_shape=out_shape)(*args)


def _outproj_bwd(dxo, xin, br, y, ug, gt, lg, wout, name):
    nch, rows, _ = y.shape
    e, d = wout.shape
    tm = _row_tile(rows, 256)

    def body(dxo_ref, x_ref, br_ref, y_ref, g_ref, gt_ref, lg_ref, w_ref,
             dy_ref, dg_ref, dxres_ref, dbr_ref, dlg_ref, dlb_ref, dgt_ref):
        dxo_v = dxo_ref[...]
        brv = br_ref[...]
        xhat, rstd = _ln_stats(ALPHA * x_ref[...] + gt_ref[...] * brv)
        dxh = dxo_v * lg_ref[...]
        dr = rstd * (dxh - jnp.mean(dxh, axis=-1, keepdims=True) - xhat * jnp.mean(dxh * xhat, axis=-1, keepdims=True))

        @pl.when(pl.program_id(0) == 0)
        def _():
            dlg_ref[...] = jnp.zeros_like(dlg_ref)
            dlb_ref[...] = jnp.zeros_like(dlb_ref)
            dgt_ref[...] = jnp.zeros_like(dgt_ref)

        dlg_ref[...] += jnp.sum(dxo_v * xhat, axis=0, keepdims=True)
        dlb_ref[...] += jnp.sum(dxo_v, axis=0, keepdims=True)
        dgt_ref[...] += jnp.sum(dr * brv, axis=0, keepdims=True)
        dxres_ref[...] = ALPHA * dr
        dbr = (gt_ref[...] * dr).astype(BF16)
        dbr_ref[...] = dbr
        dz = _dot_nt(dbr, w_ref[...])
        for k in range(nch):
            dzk = dz[:, k * LANE:(k + 1) * LANE]
            gk = g_ref[k]
            dy_ref[k] = dzk * _silu(gk)
            dg_ref[:, k * LANE:(k + 1) * LANE] = (dzk * y_ref[k] * _dsilu(gk)).astype(BF16)

    chunk_spec = pl.BlockSpec((nch, tm, LANE), lambda i: (0, i, 0))
    g_spec = pl.BlockSpec((nch, tm, LANE), lambda i: (1, i, 0))
    row_spec = pl.BlockSpec((tm, d), lambda i: (i, 0))
    vec_spec = pl.BlockSpec((1, d), lambda i: (0, 0))
    return _call(
        body, name=name, grid=(rows // tm,),
        in_specs=[row_spec, row_spec, row_spec, chunk_spec, g_spec, vec_spec, vec_spec, pl.BlockSpec((e, d), lambda i: (0, 0))],
        out_specs=[chunk_spec, pl.BlockSpec((tm, e), lambda i: (i, 0)), row_spec, row_spec, vec_spec, vec_spec, vec_spec],
        out_shape=[jax.ShapeDtypeStruct((nch, rows, LANE), F32), jax.ShapeDtypeStruct((rows, e), BF16),
                   jax.ShapeDtypeStruct((rows, d), F32), jax.ShapeDtypeStruct((rows, d), BF16)]
        + [jax.ShapeDtypeStruct((1, d), F32)] * 3,
    )(dxo, xin, br, y, ug, gt, lg, wout)


def _outproj_bwd_w(y, ug, dbr, name):
    nch, rows, _ = y.shape
    d = dbr.shape[1]
    e = nch * LANE
    tm = _row_tile(rows, 512)

    def body(y_ref, g_ref, dbr_ref, o_ref):
        @pl.when(pl.program_id(0) == 0)
        def _():
            o_ref[...] = jnp.zeros_like(o_ref)

        z = jnp.concatenate([(y_ref[k] * _silu(g_ref[k])).astype(BF16) for k in range(nch)], axis=1)
        o_ref[...] += _dot_tn(z, dbr_ref[...])

    return _call(
        body, name=name, grid=(rows // tm,),
        in_specs=[pl.BlockSpec((nch, tm, LANE), lambda i: (0, i, 0)),
                  pl.BlockSpec((nch, tm, LANE), lambda i: (1, i, 0)),
                  pl.BlockSpec((tm, d), lambda i: (i, 0))],
        out_specs=pl.BlockSpec((e, d), lambda i: (0, 0)),
        out_shape=jax.ShapeDtypeStruct((e, d), F32),
    )(y, ug, dbr)


def _nseg(length):
    n = 64 if length >= 1024 else 8
    assert length % (n * SUB) == 0
    return n


def _scan(a_ref, b_ref, h_ref, tmp, *, length, init, reverse, a_shift, store):
    nseg = _nseg(length)
    ls = length // nseg
    nq = nseg // SUB
    h_end, p_end, carry = tmp

    def rows(q, tt, shift=0):
        return pl.ds(PAD + shift + q * SUB * ls + tt, SUB, stride=ls)

    def sweep(hs, ps, do_store):
        def step(it, st):
            hs = list(st[:nq])
            ps = list(st[nq:])
            base = pl.multiple_of((ls - SUB - it * SUB) if reverse else it * SUB, SUB)
            for j in range(SUB):
                tt = base + ((SUB - 1 - j) if reverse else j)
                for q in range(nq):
                    a = a_ref[rows(q, tt, a_shift), :]
                    hs[q] = a * hs[q] + b_ref[rows(q, tt), :]
                    if ps:
                        ps[q] = a * ps[q]
                    if do_store:
                        h_ref[rows(q, tt), :] = hs[q]
            return tuple(hs) + tuple(ps)

        out = lax.fori_loop(0, ls // SUB, step, tuple(hs) + tuple(ps))
        return out[:nq], out[nq:]

    zeros = jnp.zeros((SUB, LANE), F32)
    hs, ps = sweep([zeros] * nq, [zeros + 1.0] * nq, False)
    for q in range(nq):
        h_end[pl.ds(q * SUB, SUB), :] = hs[q]
        p_end[pl.ds(q * SUB, SUB), :] = ps[q]

    def chain(k, c):
        s = (nseg - 1 - k) if reverse else k
        carry[pl.ds(s, 1), :] = c
        return p_end[pl.ds(s, 1), :] * c + h_end[pl.ds(s, 1), :]

    final = lax.fori_loop(0, nseg, chain, init)
    if store:
        sweep([carry[pl.ds(q * SUB, SUB), :] for q in range(nq)], [], True)
    return final


def _conv_fwd(src_ref, upad, u_ref, cw, cb, length):
    zeros = jnp.zeros((PAD, LANE), F32)
    upad[pl.ds(0, PAD), :] = zeros
    upad[pl.ds(PAD + length, PAD), :] = zeros
    rt = _row_tile(length, ROW_TILE)

    def copy(i, c):
        t0 = pl.multiple_of(i * rt, rt)
        upad[pl.ds(PAD + t0, rt), :] = src_ref[pl.ds(t0, rt), :]
        return c

    lax.fori_loop(0, length // rt, copy, 0)

    def tile(i, c):
        t0 = pl.multiple_of(i * rt, rt)
        acc = jnp.zeros((rt, LANE), F32)
        for k in range(CONV_TAPS):
            acc = acc + upad[pl.ds(t0 + PAD - CONV_LEFT + k, rt), :] * cw[k:k + 1, :]
        u_ref[pl.ds(t0, rt), :] = acc + cb
        return c

    lax.fori_loop(0, length // rt, tile, 0)


def _gates_fwd(u_ref, a_ref, b_ref, wa, wx, ba, bx, ls, length):
    rt = _row_tile(length, ROW_TILE)

    def tile(i, c):
        t0 = pl.multiple_of(i * rt, rt)
        ut = u_ref[pl.ds(t0, rt), :]
        ub = ut.astype(BF16)
        r = _sigmoid(_dot(ub, wa) + ba)
        ig = _sigmoid(_dot(ub, wx) + bx)
        la = (LRU_C * r) * ls
        a_ref[pl.ds(PAD + t0, rt), :] = jnp.exp(la)
        b_ref[pl.ds(PAD + t0, rt), :] = jnp.sqrt(_neg_expm1(2.0 * la)) * (ig * ut)
        return c

    lax.fori_loop(0, length // rt, tile, 0)


def _lru_specs(e, s_len, t_len):
    return [pl.BlockSpec((CONV_TAPS, LANE), lambda n: (0, n)),
            pl.BlockSpec((1, LANE), lambda n: (0, n)),
            pl.BlockSpec((2, None, LANE, LANE), lambda n: (0, n, 0, 0)),
            pl.BlockSpec((2, None, LANE, LANE), lambda n: (0, n, 0, 0)),
            pl.BlockSpec((2, LANE), lambda n: (0, n)),
            pl.BlockSpec((2, LANE), lambda n: (0, n)),
            pl.BlockSpec((2, LANE), lambda n: (0, n))]


def _rglru_fwd(ug, uc, conv_w, conv_b, wa, wx, ba, bx, lam):
    nb = uc.shape[0]
    s_len, t_len = ug.shape[1], uc.shape[1]
    nseg_max = max(_nseg(s_len), _nseg(t_len))

    def body(u0_ref, uc0_ref, cw_ref, cb_ref, wa_ref, wx_ref, ba_ref, bx_ref, lam_ref, y_ref,
             upad, ubuf, abuf, hbuf, t0_ref, t1_ref, t2_ref):
        tmp = (t0_ref, t1_ref, t2_ref)
        cw, cb = cw_ref[...], cb_ref[...]
        lsig = _log_sigmoid(lam_ref[...])
        zero = jnp.zeros((1, LANE), F32)
        _conv_fwd(uc0_ref, upad, ubuf, cw, cb, t_len)
        h0 = []
        for dr in range(2):
            _gates_fwd(ubuf, abuf, hbuf, wa_ref[dr], wx_ref[dr], ba_ref[dr:dr + 1, :], bx_ref[dr:dr + 1, :],
                       lsig[dr:dr + 1, :], t_len)
            h0.append(_scan(abuf, hbuf, hbuf, tmp, length=t_len, init=zero, reverse=(dr == 1), a_shift=0, store=False))
        _conv_fwd(u0_ref, upad, ubuf, cw, cb, s_len)
        rt = _row_tile(s_len, ROW_TILE)
        for dr in range(2):
            _gates_fwd(ubuf, abuf, hbuf, wa_ref[dr], wx_ref[dr], ba_ref[dr:dr + 1, :], bx_ref[dr:dr + 1, :],
                       lsig[dr:dr + 1, :], s_len)
            _scan(abuf, hbuf, hbuf, tmp, length=s_len, init=h0[dr], reverse=(dr == 1), a_shift=0, store=True)

            def acc(i, c, dr=dr):
                t0 = pl.multiple_of(i * rt, rt)
                h = hbuf[pl.ds(PAD + t0, rt), :]
                if dr == 0:
                    y_ref[pl.ds(t0, rt), :] = h
                else:
                    y_ref[pl.ds(t0, rt), :] += h
                return c

            lax.fori_loop(0, s_len // rt, acc, 0)

    seq = pltpu.VMEM((s_len + 2 * PAD, LANE), F32)
    small = pltpu.VMEM((nseg_max, LANE), F32)
    return _call(
        body, name="rglru_fwd", grid=(nb,),
        in_specs=[pl.BlockSpec((None, s_len, LANE), lambda n: (n, 0, 0)),
                  pl.BlockSpec((None, t_len, LANE), lambda n: (n, 0, 0))] + _lru_specs(nb * LANE, s_len, t_len),
        out_specs=pl.BlockSpec((None, s_len, LANE), lambda n: (n, 0, 0)),
        out_shape=jax.ShapeDtypeStruct((nb, s_len, LANE), F32),
        scratch_shapes=[seq, pltpu.VMEM((s_len, LANE), F32), seq, seq, small, small, small],
    )(ug, uc, conv_w, conv_b, wa, wx, ba, bx, lam)


def _rglru_bwd(ug, uc, dy, conv_w, conv_b, wa, wx, ba, bx, lam):
    nb = uc.shape[0]
    e = nb * LANE
    s_len, t_len = ug.shape[1], uc.shape[1]
    nseg_max = max(_nseg(s_len), _nseg(t_len))

    def body(u0_ref, uc0_ref, dy_ref, cw_ref, cb_ref, wa_ref, wx_ref, ba_ref, bx_ref, lam_ref,
             du_ref, duc_ref, dcw_ref, dcb_ref, dwa_ref, dwx_ref, dba_ref, dbx_ref, dlam_ref,
             upad, ubuf, abuf, hbuf, lbuf, dubuf, cpad, cu, ca0, ch0, ca1, ch1, t0_ref, t1_ref, t2_ref):
        tmp = (t0_ref, t1_ref, t2_ref)
        cw, cb = cw_ref[...], cb_ref[...]
        lam_v = lam_ref[...]
        lsig = _log_sigmoid(lam_v)
        zero = jnp.zeros((1, LANE), F32)
        zpad = jnp.zeros((PAD, LANE), F32)
        for ref in (dcw_ref, dcb_ref, dwa_ref, dwx_ref, dba_ref, dbx_ref, dlam_ref):
            ref[...] = jnp.zeros_like(ref)

        def params(dr):
            return (wa_ref[dr], wx_ref[dr], ba_ref[dr:dr + 1, :], bx_ref[dr:dr + 1, :], lsig[dr:dr + 1, :])

        def direction_bwd(dr, u_ref, a_ref, h_ref, l_ref, dub, length, first):
            wa_d, wx_d, ba_d, bx_d, ls_d = params(dr)
            rt = _row_tile(length, ROW_TILE)
            prev = 1 if dr == 1 else -1

            def tile(i, c):
                t0 = pl.multiple_of(i * rt, rt)
                ut = u_ref[pl.ds(t0, rt), :]
                ub = ut.astype(BF16)
                r = _sigmoid(_dot(ub, wa_d) + ba_d)
                ig = _sigmoid(_dot(ub, wx_d) + bx_d)
                la = (LRU_C * r) * ls_d
                a = a_ref[pl.ds(PAD + t0, rt), :]
                q = _neg_expm1(2.0 * la)
                rs = lax.rsqrt(q)
                sq = q * rs
                lm = l_ref[pl.ds(PAD + t0, rt), :]
                da = lm * h_ref[pl.ds(PAD + t0 + prev, rt), :]
                dsq = lm * ig * ut
                dig = lm * sq * ut
                dla = da * a - dsq * (a * a) * rs
                dr_ = dla * (LRU_C * ls_d)
                dlam_ref[dr:dr + 1, :] += jnp.sum(dla * (LRU_C * r), axis=0, keepdims=True)
                dpr = dr_ * r * (1.0 - r)
                dpi = dig * ig * (1.0 - ig)
                dba_ref[dr:dr + 1, :] += jnp.sum(dpr, axis=0, keepdims=True)
                dbx_ref[dr:dr + 1, :] += jnp.sum(dpi, axis=0, keepdims=True)
                dprb, dpib = dpr.astype(BF16), dpi.astype(BF16)
                dwa_ref[dr] += _dot_tn(ub, dprb)
                dwx_ref[dr] += _dot_tn(ub, dpib)
                dut = lm * sq * ig + _dot_nt(dprb, wa_d) + _dot_nt(dpib, wx_d)
                if first:
                    dub[pl.ds(PAD + t0, rt), :] = dut
                else:
                    dub[pl.ds(PAD + t0, rt), :] += dut
                return c

            lax.fori_loop(0, length // rt, tile, 0)

        def conv_bwd(dub, src_pad, out_ref, length):
            rt = _row_tile(length, ROW_TILE)

            def tile(i, c):
                t0 = pl.multiple_of(i * rt, rt)
                dut = dub[pl.ds(PAD + t0, rt), :]
                dcb_ref[...] += jnp.sum(dut, axis=0, keepdims=True)
                acc = jnp.zeros((rt, LANE), F32)
                for k in range(CONV_TAPS):
                    sh = CONV_LEFT - k
                    acc = acc + dub[pl.ds(PAD + t0 + sh, rt), :] * cw[k:k + 1, :]
                    dcw_ref[k:k + 1, :] += jnp.sum(dut * src_pad[pl.ds(PAD + t0 - sh, rt), :], axis=0, keepdims=True)
                out_ref[pl.ds(t0, rt), :] = acc.astype(out_ref.dtype)
                return c

            lax.fori_loop(0, length // rt, tile, 0)

        _conv_fwd(uc0_ref, cpad, cu, cw, cb, t_len)
        cbufs = ((ca0, ch0), (ca1, ch1))
        h0 = []
        for dr in range(2):
            ca, chh = cbufs[dr]
            _gates_fwd(cu, ca, chh, *params(dr), t_len)
            h0.append(_scan(ca, chh, chh, tmp, length=t_len, init=zero, reverse=(dr == 1), a_shift=0, store=True))
        _conv_fwd(u0_ref, upad, ubuf, cw, cb, s_len)
        rt = _row_tile(s_len, ROW_TILE)
        dh0 = []
        for dr in range(2):
            rev = dr == 1
            _gates_fwd(ubuf, abuf, hbuf, *params(dr), s_len)
            _scan(abuf, hbuf, hbuf, tmp, length=s_len, init=h0[dr], reverse=rev, a_shift=0, store=True)
            first_row = PAD + s_len if rev else PAD - 1
            hbuf[pl.ds(first_row, 1), :] = h0[dr]
            end_row = PAD - 1 if rev else PAD + s_len
            abuf[pl.ds(end_row, 1), :] = zero

            def copy(i, c):
                t0 = pl.multiple_of(i * rt, rt)
                lbuf[pl.ds(PAD + t0, rt), :] = dy_ref[pl.ds(t0, rt), :]
                return c

            lax.fori_loop(0, s_len // rt, copy, 0)
            _scan(abuf, lbuf, lbuf, tmp, length=s_len, init=zero, reverse=not rev, a_shift=(-1 if rev else 1), store=True)
            start = PAD + s_len - 1 if rev else PAD
            dh0.append(abuf[pl.ds(start, 1), :] * lbuf[pl.ds(start, 1), :])
            direction_bwd(dr, ubuf, abuf, hbuf, lbuf, dubuf, s_len, first=(dr == 0))
        dubuf[pl.ds(0, PAD), :] = zpad
        dubuf[pl.ds(PAD + s_len, PAD), :] = zpad
        conv_bwd(dubuf, upad, du_ref, s_len)
        lc = lbuf
        duc_buf = dubuf
        for dr in range(2):
            rev = dr == 1
            ca, chh = cbufs[dr]
            first_row = PAD + t_len if rev else PAD - 1
            chh[pl.ds(first_row, 1), :] = zero
            end_row = PAD - 1 if rev else PAD + t_len
            ca[pl.ds(end_row, 1), :] = zero + 1.0
            rtc = _row_tile(t_len, ROW_TILE)

            def clear(i, c):
                t0 = pl.multiple_of(i * rtc, rtc)
                lc[pl.ds(PAD + t0, rtc), :] = jnp.zeros((rtc, LANE), F32)
                return c

            lax.fori_loop(0, t_len // rtc, clear, 0)
            _scan(ca, lc, lc, tmp, length=t_len, init=dh0[dr], reverse=not rev, a_shift=(-1 if rev else 1), store=True)
            direction_bwd(dr, cu, ca, chh, lc, duc_buf, t_len, first=(dr == 0))
        duc_buf[pl.ds(0, PAD), :] = zpad
        duc_buf[pl.ds(PAD + t_len, PAD), :] = zpad
        conv_bwd(duc_buf, cpad, duc_ref, t_len)
        dlam_ref[...] = dlam_ref[...] * (1.0 - _sigmoid(lam_v))

    seq = pltpu.VMEM((s_len + 2 * PAD, LANE), F32)
    cseq = pltpu.VMEM((t_len + 2 * PAD, LANE), F32)
    small = pltpu.VMEM((nseg_max, LANE), F32)
    vec2 = pl.BlockSpec((2, LANE), lambda n: (0, n))
    wspec = pl.BlockSpec((2, None, LANE, LANE), lambda n: (0, n, 0, 0))
    return _call(
        body, name="rglru_bwd", grid=(nb,),
        in_specs=[pl.BlockSpec((None, s_len, LANE), lambda n: (n, 0, 0)),
                  pl.BlockSpec((None, t_len, LANE), lambda n: (n, 0, 0)),
                  pl.BlockSpec((None, s_len, LANE), lambda n: (n, 0, 0))] + _lru_specs(e, s_len, t_len),
        out_specs=[pl.BlockSpec((s_len, LANE), lambda n: (0, n)),
                   pl.BlockSpec((t_len, LANE), lambda n: (0, n)),
                   pl.BlockSpec((CONV_TAPS, LANE), lambda n: (0, n)),
                   pl.BlockSpec((1, LANE), lambda n: (0, n)),
                   wspec, wspec, vec2, vec2, vec2],
        out_shape=[jax.ShapeDtypeStruct((s_len, e), BF16), jax.ShapeDtypeStruct((t_len, e), BF16),
                   jax.ShapeDtypeStruct((CONV_TAPS, e), F32), jax.ShapeDtypeStruct((1, e), F32),
                   jax.ShapeDtypeStruct((2, nb, LANE, LANE), F32), jax.ShapeDtypeStruct((2, nb, LANE, LANE), F32),
                   jax.ShapeDtypeStruct((2, e), F32), jax.ShapeDtypeStruct((2, e), F32), jax.ShapeDtypeStruct((2, e), F32)],
        scratch_shapes=[seq, pltpu.VMEM((s_len, LANE), F32), seq, seq, seq, seq,
                        cseq, pltpu.VMEM((t_len, LANE), F32), cseq, cseq, cseq, cseq, small, small, small],
    )(ug, uc, dy, conv_w, conv_b, wa, wx, ba, bx, lam)


def _pool_windows(src_ref, out_ref, colbuf, rowbuf, half, transpose, s_len):
    gw = GRID_W
    lg = gw.bit_length() - 1
    n_rows = s_len // gw
    cm, rm = 16, 8 * gw
    rt = _row_tile(s_len, ROW_TILE)
    assert rt % gw == 0
    offs = range(-half, half)
    colbuf[pl.ds(0, cm), :] = jnp.zeros((cm, LANE), F32)
    colbuf[pl.ds(cm + s_len, cm), :] = jnp.zeros((cm, LANE), F32)

    def zrow(i, c):
        t0 = pl.multiple_of(i * gw, gw)
        rowbuf[pl.ds(t0, gw), :] = jnp.zeros((gw, LANE), F32)
        rowbuf[pl.ds(rm + s_len + t0, gw), :] = jnp.zeros((gw, LANE), F32)
        return c

    lax.fori_loop(0, rm // gw, zrow, 0)

    def counts(t0):
        tok = t0 + lax.broadcasted_iota(jnp.int32, (rt, LANE), 0)
        col = tok & (gw - 1)
        row = tok >> lg
        ccnt = (jnp.minimum(col + half, gw) - jnp.maximum(col - half, 0)).astype(F32)
        rcnt = (jnp.minimum(row + half, n_rows) - jnp.maximum(row - half, 0)).astype(F32)
        return col, ccnt, rcnt

    def col_sum(t0, col, sign):
        acc = jnp.zeros((rt, LANE), F32)
        for o in offs:
            so = sign * o
            ok = (col + so >= 0) & (col + so < gw)
            acc = acc + jnp.where(ok, colbuf[pl.ds(cm + t0 + so, rt), :], 0.0)
        return acc

    def row_sum(t0, sign):
        acc = jnp.zeros((rt, LANE), F32)
        for o in offs:
            acc = acc + rowbuf[pl.ds(rm + t0 + sign * o * gw, rt), :]
        return acc

    def loop(fn):
        def step(i, c):
            fn(pl.multiple_of(i * rt, rt))
            return c
        lax.fori_loop(0, s_len // rt, step, 0)

    if not transpose:
        def fill(t0):
            colbuf[pl.ds(cm + t0, rt), :] = src_ref[pl.ds(t0, rt), :]

        def cols(t0):
            col, ccnt, _ = counts(t0)
            rowbuf[pl.ds(rm + t0, rt), :] = col_sum(t0, col, 1) / ccnt

        def rows(t0):
            _, _, rcnt = counts(t0)
            out_ref[pl.ds(t0, rt), :] = (row_sum(t0, 1) / rcnt - src_ref[pl.ds(t0, rt), :]).astype(out_ref.dtype)

        loop(fill)
        loop(cols)
        loop(rows)
    else:
        def fill(t0):
            _, _, rcnt = counts(t0)
            rowbuf[pl.ds(rm + t0, rt), :] = src_ref[pl.ds(t0, rt), :] / rcnt

        def rows(t0):
            _, ccnt, _ = counts(t0)
            colbuf[pl.ds(cm + t0, rt), :] = row_sum(t0, -1) / ccnt

        def cols(t0):
            col, _, _ = counts(t0)
            out_ref[pl.ds(t0, rt), :] = (col_sum(t0, col, -1) - src_ref[pl.ds(t0, rt), :]).astype(out_ref.dtype)

        loop(fill)
        loop(rows)
        loop(cols)


def _pool_map(src, nb, transpose, out_chunk_major, name):
    s_len = src.shape[1]
    cpg = nb // len(POOL_WINDOWS)

    def body(src_ref, out_ref, colbuf, rowbuf):
        n = pl.program_id(0)
        for gi, w in enumerate(POOL_WINDOWS):
            @pl.when(n // cpg == gi)
            def _(w=w):
                _pool_windows(src_ref, out_ref, colbuf, rowbuf, w // 2, transpose, s_len)

    if out_chunk_major:
        out_spec = pl.BlockSpec((None, s_len, LANE), lambda n: (n, 0, 0))
        out_shape = jax.ShapeDtypeStruct((nb, s_len, LANE), BF16)
    else:
        out_spec = pl.BlockSpec((s_len, LANE), lambda n: (0, n))
        out_shape = jax.ShapeDtypeStruct((s_len, nb * LANE), BF16)
    return _call(
        body, name=name, grid=(nb,),
        in_specs=[pl.BlockSpec((None, s_len, LANE), lambda n: (n, 0, 0))],
        out_specs=out_spec, out_shape=out_shape,
        scratch_shapes=[pltpu.VMEM((s_len + 32, LANE), F32), pltpu.VMEM((s_len + 16 * GRID_W, LANE), F32)],
    )(src)


def _pool_mm_fwd(dm, wp, scale):
    nb, rows, _ = dm.shape
    ng, pg, _ = wp.shape
    cpg = pg // LANE
    tm = _row_tile(rows, 512)

    def body(d_ref, w_ref, s_ref, y_ref):
        _put_chunks(y_ref, _dot(_cat(d_ref, cpg), w_ref[...]) * s_ref[...], cpg)

    cspec = pl.BlockSpec((cpg, tm, LANE), lambda i, g: (g, i, 0))
    return _call(
        body, name="pool_mm_fwd", grid=(rows // tm, ng),
        in_specs=[cspec, pl.BlockSpec((None, pg, pg), lambda i, g: (g, 0, 0)), pl.BlockSpec((1, pg), lambda i, g: (0, g))],
        out_specs=cspec, out_shape=jax.ShapeDtypeStruct((nb, rows, LANE), F32),
    )(dm, wp, scale)


def _pool_mm_bwd(dy, dm, wp, scale):
    nb, rows, _ = dm.shape
    ng, pg, _ = wp.shape
    cpg = pg // LANE
    tm = _row_tile(rows, 512)

    def body(dy_ref, d_ref, w_ref, s_ref, dd_ref, dwp_ref, dsc_ref):
        @pl.when(pl.program_id(1) == 0)
        def _():
            dwp_ref[...] = jnp.zeros_like(dwp_ref)
            dsc_ref[...] = jnp.zeros_like(dsc_ref)

        dyv = _cat(dy_ref, cpg)
        dc = _cat(d_ref, cpg)
        w = w_ref[...]
        dsc_ref[...] += jnp.sum(dyv * _dot(dc, w), axis=0, keepdims=True)
        dyp = (dyv * s_ref[...]).astype(BF16)
        _put_chunks(dd_ref, _dot_nt(dyp, w), cpg)
        dwp_ref[...] += _dot_tn(dc, dyp)

    cspec = pl.BlockSpec((cpg, tm, LANE), lambda g, i: (g, i, 0))
    wspec = pl.BlockSpec((None, pg, pg), lambda g, i: (g, 0, 0))
    sspec = pl.BlockSpec((1, pg), lambda g, i: (0, g))
    return _call(
        body, name="pool_mm_bwd", grid=(ng, rows // tm),
        in_specs=[cspec, cspec, wspec, sspec],
        out_specs=[cspec, wspec, sspec],
        out_shape=[jax.ShapeDtypeStruct((nb, rows, LANE), F32), jax.ShapeDtypeStruct((ng, pg, pg), F32),
                   jax.ShapeDtypeStruct((1, ng * pg), F32)],
    )(dy, dm, wp, scale)


def _adamw(w, g, m, v):
    rows = w.shape[0]
    tm = _row_tile(rows, 1024) if rows % 1024 == 0 else rows

    def body(w_ref, g_ref, m_ref, v_ref, d_ref, nm_ref, nv_ref):
        gv = g_ref[...]
        nm = ADAM_B1 * m_ref[...] + (1.0 - ADAM_B1) * gv
        nv = ADAM_B2 * v_ref[...] + (1.0 - ADAM_B2) * jnp.square(gv)
        m_hat = nm / (1.0 - ADAM_B1 ** ADAM_STEP)
        v_hat = nv / (1.0 - ADAM_B2 ** ADAM_STEP)
        d_ref[...] = -ADAM_LR * (m_hat / (jnp.sqrt(v_hat) + ADAM_EPS) + ADAM_WD * w_ref[...])
        nm_ref[...] = nm
        nv_ref[...] = nv

    spec = pl.BlockSpec((tm, LANE), lambda i: (i, 0))
    return _call(body, name="adamw", grid=(rows // tm,), in_specs=[spec] * 4, out_specs=[spec] * 3,
                 out_shape=[jax.ShapeDtypeStruct((rows, LANE), F32)] * 3)(w, g, m, v)


def _place():
    x, y, c = lax.axis_index("x"), lax.axis_index("y"), lax.axis_index("c")
    return x, y, c


def _other_chips(x, y):
    return [(1 - x, y), (x, 1 - y), (1 - x, 1 - y)]


HBM_SPEC = pl.BlockSpec(memory_space=pl.ANY)


def _all_gather_chips(buf, name):
    rows = buf.shape[0]
    half = rows // 2
    assert half * 2 == rows and half % (32 // buf.dtype.itemsize) == 0

    def body(in_ref, out_ref, send_sems, recv_sems, local_sem):
        x, y, c = _place()
        me = 2 * x + y
        chips = _other_chips(x, y)
        mine = pl.ds(c * half, half)
        theirs = pl.ds((1 - c) * half, half)
        local = pltpu.make_async_copy(in_ref, out_ref.at[me], local_sem)
        local.start()

        def push(k, src, dst, to):
            return pltpu.make_async_remote_copy(src_ref=src, dst_ref=dst, send_sem=send_sems.at[k],
                                                recv_sem=recv_sems.at[k], device_id=to, device_id_type=MESH)

        first = [push(j, in_ref.at[mine], out_ref.at[me, mine], (cx, cy, c)) for j, (cx, cy) in enumerate(chips)]
        for cp in first:
            cp.start()
        passed = []
        for j, (cx, cy) in enumerate(chips):
            slab = out_ref.at[2 * cx + cy, mine]
            push(j, slab, slab, (x, y, c)).wait_recv()
            fwd = push(3 + j, slab, slab, (x, y, 1 - c))
            fwd.start()
            passed.append(fwd)
        for j, (cx, cy) in enumerate(chips):
            slab = out_ref.at[2 * cx + cy, theirs]
            push(3 + j, slab, slab, (x, y, c)).wait_recv()
        for cp in first + passed:
            cp.wait_send()
        local.wait()

    return _call(
        body, name=name, in_specs=[HBM_SPEC], out_specs=HBM_SPEC,
        out_shape=jax.ShapeDtypeStruct((N_CHIPS, rows, LANE), buf.dtype),
        scratch_shapes=[pltpu.SemaphoreType.DMA((6,)), pltpu.SemaphoreType.DMA((6,)), pltpu.SemaphoreType.DMA],
    )(buf)


def _sibling_swap(g):
    _, rows, _ = g.shape
    half = rows // 2

    def body(g_ref, out_ref, send_sem, recv_sem):
        x, y, c = _place()
        cp = pltpu.make_async_remote_copy(src_ref=g_ref.at[:, pl.ds((1 - c) * half, half)], dst_ref=out_ref,
                                          send_sem=send_sem, recv_sem=recv_sem, device_id=(x, y, 1 - c), device_id_type=MESH)
        cp.start()
        cp.wait()

    return _call(body, name="rs_sibling_swap", in_specs=[HBM_SPEC], out_specs=HBM_SPEC,
                 out_shape=jax.ShapeDtypeStruct((N_CHIPS, half, LANE), F32),
                 scratch_shapes=[pltpu.SemaphoreType.DMA, pltpu.SemaphoreType.DMA])(g)


def _pair_add(g, got, cidx):
    _, rows, _ = g.shape
    half = rows // 2
    tm = _row_tile(half, 1024) if half % 1024 == 0 else half
    nt = half // tm

    def body(c_ref, a_ref, b_ref, o_ref):
        o_ref[...] = a_ref[...] + b_ref[...]

    return _call(
        body, name="rs_pair_add",
        grid_spec=pltpu.PrefetchScalarGridSpec(
            num_scalar_prefetch=1, grid=(N_CHIPS, nt),
            in_specs=[pl.BlockSpec((None, tm, LANE), lambda s, i, c_ref: (s, c_ref[0] * nt + i, 0)),
                      pl.BlockSpec((None, tm, LANE), lambda s, i, c_ref: (s, i, 0))],
            out_specs=pl.BlockSpec((None, tm, LANE), lambda s, i, c_ref: (s, i, 0))),
        out_shape=jax.ShapeDtypeStruct((N_CHIPS, half, LANE), F32),
    )(cidx, g, got)


def _chip_exchange(p):
    _, half, _ = p.shape

    def body(p_ref, out_ref, send_sems, recv_sems, local_sem):
        x, y, c = _place()
        me = 2 * x + y
        chips = _other_chips(x, y)
        local = pltpu.make_async_copy(p_ref.at[me], out_ref.at[me], local_sem)
        local.start()
        sends = []
        for j, (cx, cy) in enumerate(chips):
            cp = pltpu.make_async_remote_copy(src_ref=p_ref.at[2 * cx + cy], dst_ref=out_ref.at[me], send_sem=send_sems.at[j],
                                              recv_sem=recv_sems.at[j], device_id=(cx, cy, c), device_id_type=MESH)
            cp.start()
            sends.append(cp)
        for j, (cx, cy) in enumerate(chips):
            slab = out_ref.at[2 * cx + cy]
            pltpu.make_async_remote_copy(src_ref=slab, dst_ref=slab, send_sem=send_sems.at[j], recv_sem=recv_sems.at[j],
                                         device_id=(x, y, c), device_id_type=MESH).wait_recv()
        for cp in sends:
            cp.wait_send()
        local.wait()

    return _call(body, name="rs_chip_exchange", in_specs=[HBM_SPEC], out_specs=HBM_SPEC,
                 out_shape=jax.ShapeDtypeStruct((N_CHIPS, half, LANE), F32),
                 scratch_shapes=[pltpu.SemaphoreType.DMA((3,)), pltpu.SemaphoreType.DMA((3,)), pltpu.SemaphoreType.DMA])(p)


def _sum_chips(parts):
    _, half, _ = parts.shape
    tm = _row_tile(half, 1024) if half % 1024 == 0 else half

    def body(p_ref, o_ref):
        o_ref[...] = (p_ref[0] + p_ref[1]) + (p_ref[2] + p_ref[3])

    return _call(body, name="rs_sum_chips", grid=(half // tm,),
                 in_specs=[pl.BlockSpec((N_CHIPS, tm, LANE), lambda i: (0, i, 0))],
                 out_specs=pl.BlockSpec((tm, LANE), lambda i: (i, 0)),
                 out_shape=jax.ShapeDtypeStruct((half, LANE), F32))(parts)


def _sibling_gather(hpart):
    half = hpart.shape[0]

    def body(h_ref, out_ref, send_sem, recv_sem, local_sem):
        x, y, c = _place()
        local = pltpu.make_async_copy(h_ref, out_ref.at[pl.ds(c * half, half)], local_sem)
        local.start()
        cp = pltpu.make_async_remote_copy(src_ref=h_ref, dst_ref=out_ref.at[pl.ds(c * half, half)], send_sem=send_sem,
                                          recv_sem=recv_sem, device_id=(x, y, 1 - c), device_id_type=MESH)
        cp.start()
        other = out_ref.at[pl.ds((1 - c) * half, half)]
        pltpu.make_async_remote_copy(src_ref=other, dst_ref=other, send_sem=send_sem, recv_sem=recv_sem,
                                     device_id=(x, y, c), device_id_type=MESH).wait_recv()
        cp.wait_send()
        local.wait()

    return _call(body, name="rs_sibling_gather", in_specs=[HBM_SPEC], out_specs=HBM_SPEC,
                 out_shape=jax.ShapeDtypeStruct((2 * half, LANE), F32),
                 scratch_shapes=[pltpu.SemaphoreType.DMA, pltpu.SemaphoreType.DMA, pltpu.SemaphoreType.DMA])(hpart)


def _reduce_scatter(g):
    c = lax.axis_index("c")
    got = _sibling_swap(g)
    pair = _pair_add(g, got, jnp.reshape(c, (1,)).astype(jnp.int32))
    parts = _chip_exchange(pair)
    return _sibling_gather(_sum_chips(parts))


def _round_up(n, m):
    return (n + m - 1) // m * m


def _to_rows(flat, rows=None):
    n = flat.shape[-1]
    r = _round_up(n, LANE) // LANE if rows is None else rows
    pad = r * LANE - n
    if pad:
        flat = jnp.pad(flat, [(0, 0)] * (flat.ndim - 1) + [(0, pad)])
    return flat.reshape(flat.shape[:-1] + (r, LANE))


def _f32_as_bf16_rows(a):
    return lax.bitcast_convert_type(a, BF16).reshape(a.shape[0], 2 * LANE).reshape(2 * a.shape[0], LANE)


def _bf16_rows_as_f32(a):
    r = a.shape[-2] // 2
    return lax.bitcast_convert_type(a.reshape(a.shape[:-2] + (r, LANE, 2)), F32)


SHARDED = (("w_mod", 2), ("w_in", 2), ("w_out", 1), ("pool_w", 2), ("conv_w", 2), ("lru_ba", 2), ("lru_bx", 2),
           ("lru_lam", 2), ("pool_scale", 1))
BIG = ("w_mod", "w_in", "w_out", "pool_w")
REPLICATED = ("c_ctx", "b_mod", "ln_g", "ln_b", "conv_b", "lru_wa", "lru_wx")
WEIGHTS = ("c_ctx", "w_mod", "b_mod", "w_in", "w_out", "ln_g", "ln_b", "conv_w", "conv_b", "lru_wa", "lru_ba", "lru_wx",
           "lru_bx", "lru_lam", "pool_w", "pool_scale")


def _shard_major(full, axis):
    shp = full.shape
    n = shp[axis] // N_CHIPS
    t = full.reshape(shp[:axis] + (N_CHIPS, n) + shp[axis + 1:])
    return jnp.moveaxis(t, axis, 0).reshape(N_CHIPS, -1)


def kernel(x, c, ctx, c_ctx, w_mod, b_mod, w_in, w_out, ln_g, ln_b, conv_w, conv_b, lru_wa, lru_ba, lru_wx, lru_bx, lru_lam, pool_w, pool_scale, loss_target, m_c_ctx, m_w_mod, m_b_mod, m_w_in, m_w_out, m_ln_g, m_ln_b, m_conv_w, m_conv_b, m_lru_wa, m_lru_ba, m_lru_wx, m_lru_bx, m_lru_lam, m_pool_w, m_pool_scale, v_c_ctx, v_w_mod, v_b_mod, v_w_in, v_w_out, v_ln_g, v_ln_b, v_conv_w, v_conv_b, v_lru_wa, v_lru_ba, v_lru_wx, v_lru_bx, v_lru_lam, v_pool_w, v_pool_scale):
    weights = dict(c_ctx=c_ctx, w_mod=w_mod, b_mod=b_mod, w_in=w_in, w_out=w_out, ln_g=ln_g, ln_b=ln_b, conv_w=conv_w,
                   conv_b=conv_b, lru_wa=lru_wa, lru_ba=lru_ba, lru_wx=lru_wx, lru_bx=lru_bx, lru_lam=lru_lam,
                   pool_w=pool_w, pool_scale=pool_scale)
    mom1 = dict(c_ctx=m_c_ctx, w_mod=m_w_mod, b_mod=m_b_mod, w_in=m_w_in, w_out=m_w_out, ln_g=m_ln_g, ln_b=m_ln_b,
                conv_w=m_conv_w, conv_b=m_conv_b, lru_wa=m_lru_wa, lru_ba=m_lru_ba, lru_wx=m_lru_wx, lru_bx=m_lru_bx,
                lru_lam=m_lru_lam, pool_w=m_pool_w, pool_scale=m_pool_scale)
    mom2 = dict(c_ctx=v_c_ctx, w_mod=v_w_mod, b_mod=v_b_mod, w_in=v_w_in, w_out=v_w_out, ln_g=v_ln_g, ln_b=v_ln_b,
                conv_w=v_conv_w, conv_b=v_conv_b, lru_wa=v_lru_wa, lru_ba=v_lru_ba, lru_wx=v_lru_wx, lru_bx=v_lru_bx,
                lru_lam=v_lru_lam, pool_w=v_pool_w, pool_scale=v_pool_scale)
    xs, cv, cx = x[0], c, ctx[0]
    s_len, d = xs.shape
    e = w_out.shape[1] * N_CHIPS
    nb = e // LANE
    c3 = w_mod.shape[2]
    n4 = w_in.shape[2]

    big = jnp.concatenate([weights[n].astype(BF16).reshape(-1) for n in BIG])
    small = jnp.concatenate([weights[n].reshape(-1) for n, _ in SHARDED if n not in BIG])
    n_big, n_small = big.shape[0], small.shape[0]
    big_rows = _round_up(n_big, 32 * LANE) // LANE
    small_rows = _round_up(n_small, 16 * LANE) // LANE
    packed = jnp.concatenate([_to_rows(big, big_rows), _f32_as_bf16_rows(_to_rows(small, small_rows))], axis=0)
    gathered = _all_gather_chips(packed, "gather_weights")
    gbig = gathered[:, :big_rows].reshape(N_CHIPS, -1)
    gsmall = _bf16_rows_as_f32(gathered[:, big_rows:]).reshape(N_CHIPS, -1)
    full = {}
    off = 0
    for n in BIG:
        shp = weights[n].shape
        k = int(np_prod(shp))
        full[n] = gbig[:, off:off + k].reshape((N_CHIPS,) + shp)
        off += k
    off = 0
    for n, axis in SHARDED:
        if n in BIG:
            continue
        shp = weights[n].shape
        k = int(np_prod(shp))
        t = gsmall[:, off:off + k].reshape((N_CHIPS,) + shp)
        full[n] = jnp.moveaxis(t, 0, axis).reshape(shp[:axis] + (N_CHIPS * shp[axis],) + shp[axis + 1:])
        off += k
    wm_all = full["w_mod"]
    win_all = full["w_in"]
    wout_all = full["w_out"]
    wp_all = full["pool_w"]
    win = [win_all[:, l] for l in range(DEPTH)]
    wout = [wout_all[:, l].reshape(e, d) for l in range(DEPTH)]
    pg = wp_all.shape[-1]
    wp = jnp.moveaxis(wp_all[:, 0], 0, 1).reshape(len(POOL_WINDOWS), pg, pg)
    conv_w_f, lru_ba_f, lru_bx_f, lru_lam_f = full["conv_w"][0], full["lru_ba"][0], full["lru_bx"][0], full["lru_lam"][0]
    scale_f = full["pool_scale"]
    wa_b, wx_b = lru_wa[0].astype(BF16), lru_wx[0].astype(BF16)

    cc = jnp.concatenate([cv, c_ctx[None, :], jnp.zeros((6, d), F32)], axis=0)
    mod = _mod_fwd(cc, wm_all, b_mod[:, None, :])

    def mod_parts(l, row):
        v = mod[l, row]
        return v[None, :d], 1.0 + v[None, d:2 * d], v[None, 2 * d:]

    sh0, sc0, gt0 = mod_parts(0, 0)
    shc, scc, _ = mod_parts(0, 1)
    sh1, sc1, gt1 = mod_parts(1, 0)
    lg = [ln_g[l][None, :] for l in range(DEPTH)]
    lb = [ln_b[l][None, :] for l in range(DEPTH)]

    ug0 = _inproj_fwd(xs, sc0, sh0, win[0], "inproj_fwd0")
    uc0 = _inproj_fwd(cx, scc, shc, win[0][:2], "inproj_fwd_ctx")
    lru_args = (conv_w_f, conv_b, wa_b, wx_b, lru_ba_f, lru_bx_f, lru_lam_f)
    y0 = _rglru_fwd(ug0, uc0, *lru_args)
    br0, x1 = _outproj_fwd(y0, ug0, xs, gt0, wout[0], lg[0], lb[0], None, "outproj_fwd0")
    ug1 = _inproj_fwd(x1, sc1, sh1, win[1], "inproj_fwd1")
    d1 = _pool_map(ug1, nb, False, True, "pool_fwd")
    y1 = _pool_mm_fwd(d1, wp, scale_f)
    br1, dxo, loss_part = _outproj_fwd(y1, ug1, x1, gt1, wout[1], lg[1], lb[1], loss_target[0], "outproj_fwd1")
    loss = lax.psum(loss_part[0, 0] * (0.5 / d), ("x", "y", "c"))

    dy1, dg1, dxres1, dbr1, dlg1, dlb1, dgt1 = _outproj_bwd(dxo, x1, br1, y1, ug1, gt1, lg[1], wout[1], "outproj_bwd1")
    dwout1 = _outproj_bwd_w(y1, ug1, dbr1, "outproj_bwd_w1")
    dd1, dwp, dscale = _pool_mm_bwd(dy1, d1, wp, scale_f)
    du1 = _pool_map(dd1, nb, True, False, "pool_bwd")
    dx1, dsc1, dsh1 = _inproj_bwd_x([du1, dg1], x1, dxres1, sc1, win[1], "inproj_bwd_x1")
    dwin1 = _inproj_bwd_w(x1, sc1, sh1, [du1, dg1], None, "inproj_bwd_w1")

    dy0, dg0, dxres0, dbr0, dlg0, dlb0, dgt0 = _outproj_bwd(dx1, xs, br0, y0, ug0, gt0, lg[0], wout[0], "outproj_bwd0")
    dwout0 = _outproj_bwd_w(y0, ug0, dbr0, "outproj_bwd_w0")
    du0, duc, dconv_w, dconv_b, dwa, dwx, dba, dbx, dlam = _rglru_bwd(ug0, uc0, dy0, *lru_args)
    grad_x, dsc0, dsh0 = _inproj_bwd_x([du0, dg0], xs, dxres0, sc0, win[0], "inproj_bwd_x0")
    dscc, dshc = _inproj_bwd_x([duc], cx, None, scc, win[0][:2], "inproj_bwd_x_ctx")
    dwin0c = _inproj_bwd_w(cx, scc, shc, [duc, jnp.zeros_like(duc)], None, "inproj_bwd_w_ctx")
    dwin0 = _inproj_bwd_w(xs, sc0, sh0, [du0, dg0], dwin0c, "inproj_bwd_w0")

    zd = jnp.zeros((1, d), F32)
    dm0 = jnp.concatenate([jnp.concatenate([dsh0, dsc0, dgt0], axis=1), jnp.concatenate([dshc, dscc, zd], axis=1),
                           jnp.zeros((6, 3 * d), F32)], axis=0)
    dm1 = jnp.concatenate([jnp.concatenate([dsh1, dsc1, dgt1], axis=1), jnp.zeros((7, 3 * d), F32)], axis=0)
    dwm, dbm, dcc = _mod_bwd(cc, jnp.stack([dm0, dm1]), wm_all)

    local = {
        "w_mod": dwm, "w_in": jnp.stack([dwin0, dwin1], axis=1), "w_out": jnp.stack([dwout0, dwout1]),
        "pool_w": dwp[None], "conv_w": dconv_w[None], "lru_ba": dba[None], "lru_bx": dbx[None], "lru_lam": dlam[None],
        "pool_scale": dscale,
        "c_ctx": dcc[1], "b_mod": dbm[:, 0], "ln_g": jnp.concatenate([dlg0, dlg1]), "ln_b": jnp.concatenate([dlb0, dlb1]),
        "conv_b": dconv_b, "lru_wa": dwa[None], "lru_wx": dwx[None],
    }
    slabs = []
    for n, axis in SHARDED:
        if n == "w_in":
            slabs.append(local[n].reshape(N_CHIPS, -1))
        else:
            slabs.append(_shard_major(local[n], axis))
    rep = jnp.concatenate([local[n].reshape(-1) for n in REPLICATED])
    n_rep = rep.shape[0]
    rep_rows = _round_up(n_rep, N_CHIPS * 16 * LANE) // (N_CHIPS * LANE)
    rep4 = jnp.pad(rep, (0, N_CHIPS * rep_rows * LANE - n_rep)).reshape(N_CHIPS, rep_rows * LANE)
    n_main = sum(s.shape[1] for s in slabs)
    main_rows = _round_up(_round_up(n_main, LANE) // LANE + rep_rows, 2048) - rep_rows
    main4 = jnp.concatenate(slabs, axis=1)
    main4 = jnp.pad(main4, ((0, 0), (0, main_rows * LANE - n_main)))
    gbuf = jnp.concatenate([main4, rep4], axis=1).reshape(N_CHIPS, main_rows + rep_rows, LANE)
    red = _reduce_scatter(gbuf)
    rep_all = _all_gather_chips(red[main_rows:], "gather_replicated").reshape(-1)[:n_rep]
    g_main = red[:main_rows].reshape(-1)

    def flat_of(src):
        parts = [src[n].reshape(-1) for n, _ in SHARDED]
        parts.append(jnp.zeros((main_rows * LANE - n_main,), F32))
        parts += [src[n].reshape(-1) for n in REPLICATED]
        return _to_rows(jnp.concatenate(parts), None)

    g_all = _to_rows(jnp.concatenate([g_main, rep_all]))
    total_rows = _round_up(g_all.shape[0], 1024)
    padr = lambda a: jnp.pad(a, ((0, total_rows - a.shape[0]), (0, 0)))
    w_all, m_all, v_all, g_all = padr(flat_of(weights)), padr(flat_of(mom1)), padr(flat_of(mom2)), padr(g_all)
    delta, new_m, new_v = _adamw(w_all, g_all, m_all, v_all)

    def split(flat_rows):
        flat = flat_rows.reshape(-1)
        out, off = {}, 0
        for n, _ in SHARDED:
            k = int(np_prod(weights[n].shape))
            out[n] = flat[off:off + k].reshape(weights[n].shape)
            off += k
        off = main_rows * LANE
        for n in REPLICATED:
            k = int(np_prod(weights[n].shape))
            out[n] = flat[off:off + k].reshape(weights[n].shape)
            off += k
        return out

    outs = [split(a) for a in (g_all, delta, new_m, new_v)]
    result = [loss, grad_x[None]]
    for group in outs:
        result += [group[n] for n in WEIGHTS]
    return tuple(result)


def np_prod(shape):
    p = 1
    for s in shape:
        p *= int(s)
    return p
```

```python
import functools

import jax
import jax.numpy as jnp
from jax import lax
from jax.experimental import pallas as pl
from jax.experimental.pallas import tpu as pltpu

F32 = jnp.float32
BF16 = jnp.bfloat16
LANE = 128
SUB = 8
GRID_W = 64
POOL_WINDOWS = (2, 4, 8, 16)
LRU_C = 8.0
DEPTH = 2
ALPHA = float((2 * DEPTH) ** 0.25)
LN_EPS = 1e-5
ADAM_LR, ADAM_B1, ADAM_B2, ADAM_EPS, ADAM_WD, ADAM_STEP = 0.001, 0.9, 0.999, 1e-08, 0.01, 10
N_CHIPS = 4
MESH = pl.DeviceIdType.MESH
ROW_TILE = 512
CONV_TAPS = 4
CONV_LEFT = 2
PAD = 8


def _call(body, **kw):
    return pl.pallas_call(body, **kw)


def _dot(a, b):
    return jnp.dot(a, b, preferred_element_type=F32)


def _dot_nt(a, b):
    return lax.dot_general(a, b, (((1,), (1,)), ((), ())), preferred_element_type=F32)


def _dot_tn(a, b):
    return lax.dot_general(a, b, (((0,), (0,)), ((), ())), preferred_element_type=F32)


def _sigmoid(v):
    return 0.5 * (jnp.tanh(0.5 * v) + 1.0)


def _silu(v):
    return v * _sigmoid(v)


def _dsilu(v):
    s = _sigmoid(v)
    return s * (1.0 + v * (1.0 - s))


def _log_sigmoid(v):
    z = jnp.exp(-jnp.abs(v))
    return jnp.minimum(v, 0.0) - jnp.where(z < 1e-4, z * (1.0 - 0.5 * z), jnp.log(1.0 + z))


def _one_minus_sq(la, a):
    return -jnp.tanh(la) * (a * a + 1.0)


def _cat(ref, n):
    return jnp.concatenate([ref[k] for k in range(n)], axis=1)


def _put_chunks(ref, val, n, base=0):
    for k in range(n):
        ref[base + k] = val[:, k * LANE:(k + 1) * LANE].astype(ref.dtype)


def _row_tile(rows, want):
    t = min(rows, want)
    assert rows % t == 0
    return t


def _mod_fwd(cc, wm, bm):
    ns, nl, d, c3 = wm.shape

    def body(cc_ref, w_ref, b_ref, o_ref):
        o_ref[...] = _dot(_silu(cc_ref[...]).astype(BF16), w_ref[...]) + b_ref[...]

    return _call(
        body, name="mod_fwd", grid=(nl, ns),
        in_specs=[pl.BlockSpec((8, d), lambda l, s: (0, 0)),
                  pl.BlockSpec((None, None, d, c3), lambda l, s: (s, l, 0, 0)),
                  pl.BlockSpec((None, 1, c3), lambda l, s: (l, 0, s))],
        out_specs=pl.BlockSpec((None, 8, c3), lambda l, s: (l, 0, s)),
        out_shape=jax.ShapeDtypeStruct((nl, 8, ns * c3), F32),
    )(cc, wm, bm)


def _mod_bwd(cc, dm, wm):
    ns, nl, d, c3 = wm.shape

    def body(cc_ref, dm_ref, w_ref, dw_ref, db_ref, dcc_ref):
        l, s = pl.program_id(0), pl.program_id(1)
        cv = cc_ref[...]
        dmb = dm_ref[...].astype(BF16)
        dw_ref[...] = _dot_tn(_silu(cv).astype(BF16), dmb)
        db_ref[...] = jnp.sum(dm_ref[...], axis=0, keepdims=True)

        @pl.when((l == 0) & (s == 0))
        def _():
            dcc_ref[...] = jnp.zeros_like(dcc_ref)

        @pl.when(l == 0)
        def _():
            dcc_ref[...] += _dot_nt(dmb, w_ref[...]) * _dsilu(cv)

    return _call(
        body, name="mod_bwd", grid=(nl, ns),
        in_specs=[pl.BlockSpec((8, d), lambda l, s: (0, 0)),
                  pl.BlockSpec((None, 8, c3), lambda l, s: (l, 0, s)),
                  pl.BlockSpec((None, None, d, c3), lambda l, s: (s, 0, 0, 0))],
        out_specs=[pl.BlockSpec((None, d, c3), lambda l, s: (l, 0, s)),
                   pl.BlockSpec((None, 1, c3), lambda l, s: (l, 0, s)),
                   pl.BlockSpec((8, d), lambda l, s: (0, 0))],
        out_shape=[jax.ShapeDtypeStruct((nl, d, ns * c3), F32),
                   jax.ShapeDtypeStruct((nl, 1, ns * c3), F32),
                   jax.ShapeDtypeStruct((8, d), F32)],
    )(cc, dm, wm)


def _inproj_fwd(xin, sc1, sh, w, name):
    rows, d = xin.shape
    ns, _, n4 = w.shape
    cpb = n4 // LANE
    tm = _row_tile(rows, 256)

    def body(x_ref, sc_ref, sh_ref, w_ref, o_ref):
        h = (x_ref[...] * sc_ref[...] + sh_ref[...]).astype(BF16)
        for s in range(ns):
            _put_chunks(o_ref, _dot(h, w_ref[s]), cpb, base=s * cpb)

    return _call(
        body, name=name, grid=(rows // tm,),
        in_specs=[pl.BlockSpec((tm, d), lambda i: (i, 0)),
                  pl.BlockSpec((1, d), lambda i: (0, 0)),
                  pl.BlockSpec((1, d), lambda i: (0, 0)),
                  pl.BlockSpec((ns, d, n4), lambda i: (0, 0, 0))],
        out_specs=pl.BlockSpec((ns * cpb, tm, LANE), lambda i: (0, i, 0)),
        out_shape=jax.ShapeDtypeStruct((ns * cpb, rows, LANE), F32),
    )(xin, sc1, sh, w)


def _inproj_bwd_x(dparts, xin, dxres, sc1, w, name):
    rows, d = xin.shape
    npart = len(dparts)
    e = dparts[0].shape[1]
    ns, _, n4 = w.shape
    per = e // n4
    assert per * npart == ns
    tm = _row_tile(rows, 256)
    has_res = dxres is not None

    def body(*refs):
        dp = refs[:npart]
        x_ref, sc_ref, w_ref = refs[npart:npart + 3]
        rest = refs[npart + 3:]
        if has_res:
            res_ref, dx_ref, dsc_ref, dsh_ref = rest
        else:
            dsc_ref, dsh_ref = rest
        i = pl.program_id(0)
        dh = jnp.zeros((tm, d), F32)
        for p in range(npart):
            v = dp[p][...]
            for q in range(per):
                dh = dh + _dot_nt(v[:, q * n4:(q + 1) * n4], w_ref[p * per + q])

        @pl.when(i == 0)
        def _():
            dsc_ref[...] = jnp.zeros_like(dsc_ref)
            dsh_ref[...] = jnp.zeros_like(dsh_ref)

        dsc_ref[...] += jnp.sum(dh * x_ref[...], axis=0, keepdims=True)
        dsh_ref[...] += jnp.sum(dh, axis=0, keepdims=True)
        if has_res:
            dx_ref[...] = res_ref[...] + dh * sc_ref[...]

    row_spec = pl.BlockSpec((tm, d), lambda i: (i, 0))
    vec_spec = pl.BlockSpec((1, d), lambda i: (0, 0))
    in_specs = [pl.BlockSpec((tm, e), lambda i: (i, 0))] * npart + [row_spec, vec_spec,
                                                                     pl.BlockSpec((ns, d, n4), lambda i: (0, 0, 0))]
    args = list(dparts) + [xin, sc1, w]
    out_specs, out_shape = [vec_spec, vec_spec], [jax.ShapeDtypeStruct((1, d), F32)] * 2
    if has_res:
        in_specs.append(row_spec)
        args.append(dxres)
        out_specs = [row_spec] + out_specs
        out_shape = [jax.ShapeDtypeStruct((rows, d), F32)] + out_shape
    return _call(body, name=name, grid=(rows // tm,), in_specs=in_specs, out_specs=out_specs, out_shape=out_shape)(*args)


def _inproj_bwd_w(xin, sc1, sh, dparts, init, name):
    rows, d = xin.shape
    npart = len(dparts)
    e = dparts[0].shape[1]
    n4 = e // 2
    ns = 2 * npart
    tm = _row_tile(rows, 512)
    nt = rows // tm
    has_init = init is not None

    def body(*refs):
        x_ref, sc_ref, sh_ref = refs[:3]
        dp = refs[3:3 + npart]
        if has_init:
            init_ref, o_ref = refs[3 + npart:]
        else:
            (o_ref,) = refs[3 + npart:]
        s, i = pl.program_id(0), pl.program_id(1)
        h = (x_ref[...] * sc_ref[...] + sh_ref[...]).astype(BF16)

        @pl.when(i == 0)
        def _():
            o_ref[...] = init_ref[...] if has_init else jnp.zeros_like(o_ref)

        for p in range(npart):
            @pl.when(s // 2 == p)
            def _(p=p):
                o_ref[...] += _dot_tn(h, dp[p][...])

    in_specs = [pl.BlockSpec((tm, d), lambda s, i: (i, 0)),
                pl.BlockSpec((1, d), lambda s, i: (0, 0)),
                pl.BlockSpec((1, d), lambda s, i: (0, 0))]
    in_specs += [pl.BlockSpec((tm, n4), lambda s, i: (i, s % 2))] * npart
    args = [xin, sc1, sh] + list(dparts)
    o_spec = pl.BlockSpec((None, d, n4), lambda s, i: (s, 0, 0))
    if has_init:
        in_specs.append(o_spec)
        args.append(init)
    return _call(body, name=name, grid=(ns, nt), in_specs=in_specs, out_specs=o_spec,
                 out_shape=jax.ShapeDtypeStruct((ns, d, n4), F32))(*args)


def _ln_stats(r):
    mu = jnp.mean(r, axis=-1, keepdims=True)
    var = jnp.mean(jnp.square(r - mu), axis=-1, keepdims=True)
    rstd = lax.rsqrt(var + LN_EPS)
    return (r - mu) * rstd, rstd


def _outproj_fwd(y, ug, xin, gt, wout, lg, lb, target, name):
    nch, rows, _ = y.shape
    e, d = wout.shape
    tm = _row_tile(rows, 256)
    with_loss = target is not None

    def body(*refs):
        y_ref, g_ref, x_ref, gt_ref, w_ref, lg_ref, lb_ref = refs[:7]
        if with_loss:
            t_ref, br_ref, dxo_ref, loss_ref = refs[7:]
        else:
            br_ref, xo_ref = refs[7:]
        z = jnp.concatenate([(y_ref[k] * _silu(g_ref[k])).astype(BF16) for k in range(nch)], axis=1)
        br = _dot(z, w_ref[...])
        br_ref[...] = br
        xhat, _ = _ln_stats(ALPHA * x_ref[...] + gt_ref[...] * br)
        xo = xhat * lg_ref[...] + lb_ref[...]
        if with_loss:
            err = xo - t_ref[...]
            dxo_ref[...] = err * (1.0 / d)

            @pl.when(pl.program_id(0) == 0)
            def _():
                loss_ref[...] = jnp.zeros_like(loss_ref)

            loss_ref[...] += jnp.sum(err * err)
        else:
            xo_ref[...] = xo

    chunk_spec = pl.BlockSpec((nch, tm, LANE), lambda i: (0, i, 0))
    g_spec = pl.BlockSpec((nch, tm, LANE), lambda i: (1, i, 0))
    row_spec = pl.BlockSpec((tm, d), lambda i: (i, 0))
    vec_spec = pl.BlockSpec((1, d), lambda i: (0, 0))
    in_specs = [chunk_spec, g_spec, row_spec, vec_spec, pl.BlockSpec((e, d), lambda i: (0, 0)), vec_spec, vec_spec]
    args = [y, ug, xin, gt, wout, lg, lb]
    out_specs = [row_spec, row_spec]
    out_shape = [jax.ShapeDtypeStruct((rows, d), F32)] * 2
    if with_loss:
        in_specs.append(row_spec)
        args.append(target)
        out_specs.append(pl.BlockSpec((1, LANE), lambda i: (0, 0)))
        out_shape.append(jax.ShapeDtypeStruct((1, LANE), F32))
    return _call(body, name=name, grid=(rows // tm,), in_specs=in_specs, out_specs=out_specs, out_shape=out_shape)(*args)


def _outproj_bwd(dxo, xin, br, y, ug, gt, lg, wout, name):
    nch, rows, _ = y.shape
    e, d = wout.shape
    tm = _row_tile(rows, 256)

    def body(dxo_ref, x_ref, br_ref, y_ref, g_ref, gt_ref, lg_ref, w_ref,
             dy_ref, dg_ref, dxres_ref, dbr_ref, dlg_ref, dlb_ref, dgt_ref):
        dxo_v = dxo_ref[...]
        brv = br_ref[...]
        xhat, rstd = _ln_stats(ALPHA * x_ref[...] + gt_ref[...] * brv)
        dxh = dxo_v * lg_ref[...]
        dr = rstd * (dxh - jnp.mean(dxh, axis=-1, keepdims=True) - xhat * jnp.mean(dxh * xhat, axis=-1, keepdims=True))

        @pl.when(pl.program_id(0) == 0)
        def _():
            dlg_ref[...] = jnp.zeros_like(dlg_ref)
            dlb_ref[...] = jnp.zeros_like(dlb_ref)
            dgt_ref[...] = jnp.zeros_like(dgt_ref)

        dlg_ref[...] += jnp.sum(dxo_v * xhat, axis=0, keepdims=True)
        dlb_ref[...] += jnp.sum(dxo_v, axis=0, keepdims=True)
        dgt_ref[...] += jnp.sum(dr * brv, axis=0, keepdims=True)
        dxres_ref[...] = ALPHA * dr
        dbr = (gt_ref[...] * dr).astype(BF16)
        dbr_ref[...] = dbr
        dz = _dot_nt(dbr, w_ref[...])
        for k in range(nch):
            dzk = dz[:, k * LANE:(k + 1) * LANE]
            gk = g_ref[k]
            dy_ref[k] = dzk * _silu(gk)
            dg_ref[:, k * LANE:(k + 1) * LANE] = (dzk * y_ref[k] * _dsilu(gk)).astype(BF16)

    chunk_spec = pl.BlockSpec((nch, tm, LANE), lambda i: (0, i, 0))
    g_spec = pl.BlockSpec((nch, tm, LANE), lambda i: (1, i, 0))
    row_spec = pl.BlockSpec((tm, d), lambda i: (i, 0))
    vec_spec = pl.BlockSpec((1, d), lambda i: (0, 0))
    return _call(
        body, name=name, grid=(rows // tm,),
        in_specs=[row_spec, row_spec, row_spec, chunk_spec, g_spec, vec_spec, vec_spec, pl.BlockSpec((e, d), lambda i: (0, 0))],
        out_specs=[chunk_spec, pl.BlockSpec((tm, e), lambda i: (i, 0)), row_spec, row_spec, vec_spec, vec_spec, vec_spec],
        out_shape=[jax.ShapeDtypeStruct((nch, rows, LANE), F32), jax.ShapeDtypeStruct((rows, e), BF16),
                   jax.ShapeDtypeStruct((rows, d), F32), jax.ShapeDtypeStruct((rows, d), BF16)]
        + [jax.ShapeDtypeStruct((1, d), F32)] * 3,
    )(dxo, xin, br, y, ug, gt, lg, wout)


def _outproj_bwd_w(y, ug, dbr, name):
    nch, rows, _ = y.shape
    d = dbr.shape[1]
    e = nch * LANE
    tm = _row_tile(rows, 512)

    def body(y_ref, g_ref, dbr_ref, o_ref):
        @pl.when(pl.program_id(0) == 0)
        def _():
            o_ref[...] = jnp.zeros_like(o_ref)

        z = jnp.concatenate([(y_ref[k] * _silu(g_ref[k])).astype(BF16) for k in range(nch)], axis=1)
        o_ref[...] += _dot_tn(z, dbr_ref[...])

    return _call(
        body, name=name, grid=(rows // tm,),
        in_specs=[pl.BlockSpec((nch, tm, LANE), lambda i: (0, i, 0)),
                  pl.BlockSpec((nch, tm, LANE), lambda i: (1, i, 0)),
                  pl.BlockSpec((tm, d), lambda i: (i, 0))],
        out_specs=pl.BlockSpec((e, d), lambda i: (0, 0)),
        out_shape=jax.ShapeDtypeStruct((e, d), F32),
    )(y, ug, dbr)


SCAN_UNROLL = 8


def _scan(a_ref, b_ref, h_ref, *, length, init, reverse, a_shift, store):
    nblk = length // SUB
    assert nblk % SCAN_UNROLL == 0
    row = lax.broadcasted_iota(jnp.int32, (SUB, LANE), 0)
    last = 0 if reverse else SUB - 1
    edges = [(row >= SUB - k) if reverse else (row < k) for k in (1, 2, 4)]

    def local_scan(a, b):
        for k, edge in zip((1, 2, 4), edges):
            sh = (SUB - k) if reverse else k
            b = b + a * jnp.where(edge, 0.0, pltpu.roll(b, sh, 0))
            a = a * jnp.where(edge, 1.0, pltpu.roll(a, sh, 0))
        return a, b

    def step(i, carry):
        base = pl.multiple_of(((nblk // SCAN_UNROLL - 1 - i) if reverse else i) * (SCAN_UNROLL * SUB), SCAN_UNROLL * SUB)
        order = range(SCAN_UNROLL - 1, -1, -1) if reverse else range(SCAN_UNROLL)
        loaded = [(a_ref[pl.ds(PAD + base + j * SUB + a_shift, SUB), :], b_ref[pl.ds(PAD + base + j * SUB, SUB), :])
                  for j in order]
        scanned = [local_scan(a, b) for a, b in loaded]
        for j, (a, b) in zip(order, scanned):
            if store:
                h_ref[pl.ds(PAD + base + j * SUB, SUB), :] = b + a * carry
            a_l = jnp.broadcast_to(a[last:last + 1, :], (SUB, LANE))
            b_l = jnp.broadcast_to(b[last:last + 1, :], (SUB, LANE))
            carry = b_l + a_l * carry
        return carry

    carry = lax.fori_loop(0, nblk // SCAN_UNROLL, step, jnp.broadcast_to(init, (SUB, LANE)))
    return carry[0:1, :]


def _conv_fwd(src_ref, upad, u_ref, cw, cb, length):
    zeros = jnp.zeros((PAD, LANE), F32)
    upad[pl.ds(0, PAD), :] = zeros
    upad[pl.ds(PAD + length, PAD), :] = zeros
    rt = _row_tile(length, ROW_TILE)

    def copy(i, c):
        t0 = pl.multiple_of(i * rt, rt)
        upad[pl.ds(PAD + t0, rt), :] = src_ref[pl.ds(t0, rt), :]
        return c

    lax.fori_loop(0, length // rt, copy, 0)

    def tile(i, c):
        t0 = pl.multiple_of(i * rt, rt)
        acc = jnp.zeros((rt, LANE), F32)
        for k in range(CONV_TAPS):
            acc = acc + upad[pl.ds(t0 + PAD - CONV_LEFT + k, rt), :] * cw[k:k + 1, :]
        u_ref[pl.ds(t0, rt), :] = acc + cb
        return c

    lax.fori_loop(0, length // rt, tile, 0)


def _gates_fwd(u_ref, a_ref, b_ref, wa, wx, ba, bx, ls, length):
    rt = _row_tile(length, ROW_TILE)

    def tile(i, c):
        t0 = pl.multiple_of(i * rt, rt)
        ut = u_ref[pl.ds(t0, rt), :]
        ub = ut.astype(BF16)
        r = _sigmoid(_dot(ub, wa) + ba)
        ig = _sigmoid(_dot(ub, wx) + bx)
        la = (LRU_C * r) * ls
        a = jnp.exp(la)
        a_ref[pl.ds(PAD + t0, rt), :] = a
        b_ref[pl.ds(PAD + t0, rt), :] = jnp.sqrt(_one_minus_sq(la, a)) * (ig * ut)
        return c

    lax.fori_loop(0, length // rt, tile, 0)


def _lru_specs(e, s_len, t_len):
    return [pl.BlockSpec((CONV_TAPS, LANE), lambda n: (0, n)),
            pl.BlockSpec((1, LANE), lambda n: (0, n)),
            pl.BlockSpec((2, None, LANE, LANE), lambda n: (0, n, 0, 0)),
            pl.BlockSpec((2, None, LANE, LANE), lambda n: (0, n, 0, 0)),
            pl.BlockSpec((2, LANE), lambda n: (0, n)),
            pl.BlockSpec((2, LANE), lambda n: (0, n)),
            pl.BlockSpec((2, LANE), lambda n: (0, n))]


def _rglru_fwd(ug, uc, conv_w, conv_b, wa, wx, ba, bx, lam):
    nb = uc.shape[0]
    s_len, t_len = ug.shape[1], uc.shape[1]

    def body(u0_ref, uc0_ref, cw_ref, cb_ref, wa_ref, wx_ref, ba_ref, bx_ref, lam_ref, y_ref,
             upad, ubuf, abuf, hbuf):
        cw, cb = cw_ref[...], cb_ref[...]
        lsig = _log_sigmoid(lam_ref[...])
        zero = jnp.zeros((1, LANE), F32)
        _conv_fwd(uc0_ref, upad, ubuf, cw, cb, t_len)
        h0 = []
        for dr in range(2):
            _gates_fwd(ubuf, abuf, hbuf, wa_ref[dr], wx_ref[dr], ba_ref[dr:dr + 1, :], bx_ref[dr:dr + 1, :],
                       lsig[dr:dr + 1, :], t_len)
            h0.append(_scan(abuf, hbuf, hbuf, length=t_len, init=zero, reverse=(dr == 1), a_shift=0, store=False))
        _conv_fwd(u0_ref, upad, ubuf, cw, cb, s_len)
        rt = _row_tile(s_len, ROW_TILE)
        for dr in range(2):
            _gates_fwd(ubuf, abuf, hbuf, wa_ref[dr], wx_ref[dr], ba_ref[dr:dr + 1, :], bx_ref[dr:dr + 1, :],
                       lsig[dr:dr + 1, :], s_len)
            _scan(abuf, hbuf, hbuf, length=s_len, init=h0[dr], reverse=(dr == 1), a_shift=0, store=True)

            def acc(i, c, dr=dr):
                t0 = pl.multiple_of(i * rt, rt)
                h = hbuf[pl.ds(PAD + t0, rt), :]
                if dr == 0:
                    y_ref[pl.ds(t0, rt), :] = h
                else:
                    y_ref[pl.ds(t0, rt), :] += h
                return c

            lax.fori_loop(0, s_len // rt, acc, 0)

    seq = pltpu.VMEM((s_len + 2 * PAD, LANE), F32)
    return _call(
        body, name="rglru_fwd", grid=(nb,),
        in_specs=[pl.BlockSpec((None, s_len, LANE), lambda n: (n, 0, 0)),
                  pl.BlockSpec((None, t_len, LANE), lambda n: (n, 0, 0))] + _lru_specs(nb * LANE, s_len, t_len),
        out_specs=pl.BlockSpec((None, s_len, LANE), lambda n: (n, 0, 0)),
        out_shape=jax.ShapeDtypeStruct((nb, s_len, LANE), F32),
        scratch_shapes=[seq, pltpu.VMEM((s_len, LANE), F32), seq, seq],
    )(ug, uc, conv_w, conv_b, wa, wx, ba, bx, lam)


def _rglru_bwd(ug, uc, dy, conv_w, conv_b, wa, wx, ba, bx, lam):
    nb = uc.shape[0]
    e = nb * LANE
    s_len, t_len = ug.shape[1], uc.shape[1]

    def body(u0_ref, uc0_ref, dy_ref, cw_ref, cb_ref, wa_ref, wx_ref, ba_ref, bx_ref, lam_ref,
             du_ref, duc_ref, dcw_ref, dcb_ref, dwa_ref, dwx_ref, dba_ref, dbx_ref, dlam_ref,
             upad, ubuf, abuf, hbuf, lbuf, dubuf, cpad, cu, ca0, ch0, ca1, ch1):
        cw, cb = cw_ref[...], cb_ref[...]
        lam_v = lam_ref[...]
        lsig = _log_sigmoid(lam_v)
        zero = jnp.zeros((1, LANE), F32)
        zpad = jnp.zeros((PAD, LANE), F32)
        for ref in (dcw_ref, dcb_ref, dwa_ref, dwx_ref, dba_ref, dbx_ref, dlam_ref):
            ref[...] = jnp.zeros_like(ref)

        def params(dr):
            return (wa_ref[dr], wx_ref[dr], ba_ref[dr:dr + 1, :], bx_ref[dr:dr + 1, :], lsig[dr:dr + 1, :])

        def direction_bwd(dr, u_ref, a_ref, h_ref, l_ref, dub, length, first):
            wa_d, wx_d, ba_d, bx_d, ls_d = params(dr)
            rt = _row_tile(length, ROW_TILE)
            prev = 1 if dr == 1 else -1

            def tile(i, c):
                t0 = pl.multiple_of(i * rt, rt)
                ut = u_ref[pl.ds(t0, rt), :]
                ub = ut.astype(BF16)
                r = _sigmoid(_dot(ub, wa_d) + ba_d)
                ig = _sigmoid(_dot(ub, wx_d) + bx_d)
                la = (LRU_C * r) * ls_d
                a = a_ref[pl.ds(PAD + t0, rt), :]
                q = _one_minus_sq(la, a)
                rs = lax.rsqrt(q)
                sq = q * rs
                lm = l_ref[pl.ds(PAD + t0, rt), :]
                da = lm * h_ref[pl.ds(PAD + t0 + prev, rt), :]
                dsq = lm * ig * ut
                dig = lm * sq * ut
                dla = da * a - dsq * (a * a) * rs
                dr_ = dla * (LRU_C * ls_d)
                dlam_ref[dr:dr + 1, :] += jnp.sum(dla * (LRU_C * r), axis=0, keepdims=True)
                dpr = dr_ * r * (1.0 - r)
                dpi = dig * ig * (1.0 - ig)
                dba_ref[dr:dr + 1, :] += jnp.sum(dpr, axis=0, keepdims=True)
                dbx_ref[dr:dr + 1, :] += jnp.sum(dpi, axis=0, keepdims=True)
                dprb, dpib = dpr.astype(BF16), dpi.astype(BF16)
                dwa_ref[dr] += _dot_tn(ub, dprb)
                dwx_ref[dr] += _dot_tn(ub, dpib)
                dut = lm * sq * ig + _dot_nt(dprb, wa_d) + _dot_nt(dpib, wx_d)
                if first:
                    dub[pl.ds(PAD + t0, rt), :] = dut
                else:
                    dub[pl.ds(PAD + t0, rt), :] += dut
                return c

            lax.fori_loop(0, length // rt, tile, 0)

        def conv_bwd(dub, src_pad, out_ref, length):
            rt = _row_tile(length, ROW_TILE)

            def tile(i, c):
                t0 = pl.multiple_of(i * rt, rt)
                dut = dub[pl.ds(PAD + t0, rt), :]
                dcb_ref[...] += jnp.sum(dut, axis=0, keepdims=True)
                acc = jnp.zeros((rt, LANE), F32)
                for k in range(CONV_TAPS):
                    sh = CONV_LEFT - k
                    acc = acc + dub[pl.ds(PAD + t0 + sh, rt), :] * cw[k:k + 1, :]
                    dcw_ref[k:k + 1, :] += jnp.sum(dut * src_pad[pl.ds(PAD + t0 - sh, rt), :], axis=0, keepdims=True)
                out_ref[pl.ds(t0, rt), :] = acc.astype(out_ref.dtype)
                return c

            lax.fori_loop(0, length // rt, tile, 0)

        _conv_fwd(uc0_ref, cpad, cu, cw, cb, t_len)
        cbufs = ((ca0, ch0), (ca1, ch1))
        h0 = []
        for dr in range(2):
            ca, chh = cbufs[dr]
            _gates_fwd(cu, ca, chh, *params(dr), t_len)
            h0.append(_scan(ca, chh, chh, length=t_len, init=zero, reverse=(dr == 1), a_shift=0, store=True))
        _conv_fwd(u0_ref, upad, ubuf, cw, cb, s_len)
        rt = _row_tile(s_len, ROW_TILE)
        dh0 = []
        for dr in range(2):
            rev = dr == 1
            _gates_fwd(ubuf, abuf, hbuf, *params(dr), s_len)
            _scan(abuf, hbuf, hbuf, length=s_len, init=h0[dr], reverse=rev, a_shift=0, store=True)
            first_row = PAD + s_len if rev else PAD - 1
            hbuf[pl.ds(first_row, 1), :] = h0[dr]
            end_row = PAD - 1 if rev else PAD + s_len
            abuf[pl.ds(end_row, 1), :] = zero

            def copy(i, c):
                t0 = pl.multiple_of(i * rt, rt)
                lbuf[pl.ds(PAD + t0, rt), :] = dy_ref[pl.ds(t0, rt), :]
                return c

            lax.fori_loop(0, s_len // rt, copy, 0)
            _scan(abuf, lbuf, lbuf, length=s_len, init=zero, reverse=not rev, a_shift=(-1 if rev else 1), store=True)
            start = PAD + s_len - 1 if rev else PAD
            dh0.append(abuf[pl.ds(start, 1), :] * lbuf[pl.ds(start, 1), :])
            direction_bwd(dr, ubuf, abuf, hbuf, lbuf, dubuf, s_len, first=(dr == 0))
        dubuf[pl.ds(0, PAD), :] = zpad
        dubuf[pl.ds(PAD + s_len, PAD), :] = zpad
        conv_bwd(dubuf, upad, du_ref, s_len)
        lc = lbuf
        duc_buf = dubuf
        for dr in range(2):
            rev = dr == 1
            ca, chh = cbufs[dr]
            first_row = PAD + t_len if rev else PAD - 1
            chh[pl.ds(first_row, 1), :] = zero
            end_row = PAD - 1 if rev else PAD + t_len
            ca[pl.ds(end_row, 1), :] = zero + 1.0
            rtc = _row_tile(t_len, ROW_TILE)

            def clear(i, c):
                t0 = pl.multiple_of(i * rtc, rtc)
                lc[pl.ds(PAD + t0, rtc), :] = jnp.zeros((rtc, LANE), F32)
                return c

            lax.fori_loop(0, t_len // rtc, clear, 0)
            _scan(ca, lc, lc, length=t_len, init=dh0[dr], reverse=not rev, a_shift=(-1 if rev else 1), store=True)
            direction_bwd(dr, cu, ca, chh, lc, duc_buf, t_len, first=(dr == 0))
        duc_buf[pl.ds(0, PAD), :] = zpad
        duc_buf[pl.ds(PAD + t_len, PAD), :] = zpad
        conv_bwd(duc_buf, cpad, duc_ref, t_len)
        dlam_ref[...] = dlam_ref[...] * (1.0 - _sigmoid(lam_v))

    seq = pltpu.VMEM((s_len + 2 * PAD, LANE), F32)
    cseq = pltpu.VMEM((t_len + 2 * PAD, LANE), F32)
    vec2 = pl.BlockSpec((2, LANE), lambda n: (0, n))
    wspec = pl.BlockSpec((2, None, LANE, LANE), lambda n: (0, n, 0, 0))
    return _call(
        body, name="rglru_bwd", grid=(nb,),
        in_specs=[pl.BlockSpec((None, s_len, LANE), lambda n: (n, 0, 0)),
                  pl.BlockSpec((None, t_len, LANE), lambda n: (n, 0, 0)),
                  pl.BlockSpec((None, s_len, LANE), lambda n: (n, 0, 0))] + _lru_specs(e, s_len, t_len),
        out_specs=[pl.BlockSpec((s_len, LANE), lambda n: (0, n)),
                   pl.BlockSpec((t_len, LANE), lambda n: (0, n)),
                   pl.BlockSpec((CONV_TAPS, LANE), lambda n: (0, n)),
                   pl.BlockSpec((1, LANE), lambda n: (0, n)),
                   wspec, wspec, vec2, vec2, vec2],
        out_shape=[jax.ShapeDtypeStruct((s_len, e), BF16), jax.ShapeDtypeStruct((t_len, e), BF16),
                   jax.ShapeDtypeStruct((CONV_TAPS, e), F32), jax.ShapeDtypeStruct((1, e), F32),
                   jax.ShapeDtypeStruct((2, nb, LANE, LANE), F32), jax.ShapeDtypeStruct((2, nb, LANE, LANE), F32),
                   jax.ShapeDtypeStruct((2, e), F32), jax.ShapeDtypeStruct((2, e), F32), jax.ShapeDtypeStruct((2, e), F32)],
        scratch_shapes=[seq, pltpu.VMEM((s_len, LANE), F32), seq, seq, seq, seq,
                        cseq, pltpu.VMEM((t_len, LANE), F32), cseq, cseq, cseq, cseq],
    )(ug, uc, dy, conv_w, conv_b, wa, wx, ba, bx, lam)


def _pool_windows(src_ref, out_ref, colbuf, rowbuf, half, transpose, s_len):
    gw = GRID_W
    lg = gw.bit_length() - 1
    n_rows = s_len // gw
    cm, rm = 16, 8 * gw
    rt = _row_tile(s_len, ROW_TILE)
    assert rt % gw == 0
    offs = range(-half, half)
    colbuf[pl.ds(0, cm), :] = jnp.zeros((cm, LANE), F32)
    colbuf[pl.ds(cm + s_len, cm), :] = jnp.zeros((cm, LANE), F32)

    def zrow(i, c):
        t0 = pl.multiple_of(i * gw, gw)
        rowbuf[pl.ds(t0, gw), :] = jnp.zeros((gw, LANE), F32)
        rowbuf[pl.ds(rm + s_len + t0, gw), :] = jnp.zeros((gw, LANE), F32)
        return c

    lax.fori_loop(0, rm // gw, zrow, 0)

    def counts(t0):
        tok = t0 + lax.broadcasted_iota(jnp.int32, (rt, LANE), 0)
        col = tok & (gw - 1)
        row = tok >> lg
        ccnt = (jnp.minimum(col + half, gw) - jnp.maximum(col - half, 0)).astype(F32)
        rcnt = (jnp.minimum(row + half, n_rows) - jnp.maximum(row - half, 0)).astype(F32)
        return col, ccnt, rcnt

    def col_sum(t0, col, sign):
        acc = jnp.zeros((rt, LANE), F32)
        for o in offs:
            so = sign * o
            ok = (col + so >= 0) & (col + so < gw)
            acc = acc + jnp.where(ok, colbuf[pl.ds(cm + t0 + so, rt), :], 0.0)
        return acc

    def row_sum(t0, sign):
        acc = jnp.zeros((rt, LANE), F32)
        for o in offs:
            acc = acc + rowbuf[pl.ds(rm + t0 + sign * o * gw, rt), :]
        return acc

    def loop(fn):
        def step(i, c):
            fn(pl.multiple_of(i * rt, rt))
            return c
        lax.fori_loop(0, s_len // rt, step, 0)

    if not transpose:
        def fill(t0):
            colbuf[pl.ds(cm + t0, rt), :] = src_ref[pl.ds(t0, rt), :]

        def cols(t0):
            col, ccnt, _ = counts(t0)
            rowbuf[pl.ds(rm + t0, rt), :] = col_sum(t0, col, 1) / ccnt

        def rows(t0):
            _, _, rcnt = counts(t0)
            out_ref[pl.ds(t0, rt), :] = (row_sum(t0, 1) / rcnt - src_ref[pl.ds(t0, rt), :]).astype(out_ref.dtype)

        loop(fill)
        loop(cols)
        loop(rows)
    else:
        def fill(t0):
            _, _, rcnt = counts(t0)
            rowbuf[pl.ds(rm + t0, rt), :] = src_ref[pl.ds(t0, rt), :] / rcnt

        def rows(t0):
            _, ccnt, _ = counts(t0)
            colbuf[pl.ds(cm + t0, rt), :] = row_sum(t0, -1) / ccnt

        def cols(t0):
            col, _, _ = counts(t0)
            out_ref[pl.ds(t0, rt), :] = (col_sum(t0, col, -1) - src_ref[pl.ds(t0, rt), :]).astype(out_ref.dtype)

        loop(fill)
        loop(rows)
        loop(cols)


def _pool_map(src, nb, transpose, out_chunk_major, name):
    s_len = src.shape[1]
    cpg = nb // len(POOL_WINDOWS)

    def body(src_ref, out_ref, colbuf, rowbuf):
        n = pl.program_id(0)
        for gi, w in enumerate(POOL_WINDOWS):
            @pl.when(n // cpg == gi)
            def _(w=w):
                _pool_windows(src_ref, out_ref, colbuf, rowbuf, w // 2, transpose, s_len)

    if out_chunk_major:
        out_spec = pl.BlockSpec((None, s_len, LANE), lambda n: (n, 0, 0))
        out_shape = jax.ShapeDtypeStruct((nb, s_len, LANE), BF16)
    else:
        out_spec = pl.BlockSpec((s_len, LANE), lambda n: (0, n))
        out_shape = jax.ShapeDtypeStruct((s_len, nb * LANE), BF16)
    return _call(
        body, name=name, grid=(nb,),
        in_specs=[pl.BlockSpec((None, s_len, LANE), lambda n: (n, 0, 0))],
        out_specs=out_spec, out_shape=out_shape,
        scratch_shapes=[pltpu.VMEM((s_len + 32, LANE), F32), pltpu.VMEM((s_len + 16 * GRID_W, LANE), F32)],
    )(src)


def _pool_mm_fwd(dm, wp, scale):
    nb, rows, _ = dm.shape
    ng, pg, _ = wp.shape
    cpg = pg // LANE
    tm = _row_tile(rows, 512)

    def body(d_ref, w_ref, s_ref, y_ref):
        _put_chunks(y_ref, _dot(_cat(d_ref, cpg), w_ref[...]) * s_ref[...], cpg)

    cspec = pl.BlockSpec((cpg, tm, LANE), lambda i, g: (g, i, 0))
    return _call(
        body, name="pool_mm_fwd", grid=(rows // tm, ng),
        in_specs=[cspec, pl.BlockSpec((None, pg, pg), lambda i, g: (g, 0, 0)), pl.BlockSpec((1, pg), lambda i, g: (0, g))],
        out_specs=cspec, out_shape=jax.ShapeDtypeStruct((nb, rows, LANE), F32),
    )(dm, wp, scale)


def _pool_mm_bwd(dy, dm, wp, scale):
    nb, rows, _ = dm.shape
    ng, pg, _ = wp.shape
    cpg = pg // LANE
    tm = _row_tile(rows, 512)

    def body(dy_ref, d_ref, w_ref, s_ref, dd_ref, dwp_ref, dsc_ref):
        @pl.when(pl.program_id(1) == 0)
        def _():
            dwp_ref[...] = jnp.zeros_like(dwp_ref)
            dsc_ref[...] = jnp.zeros_like(dsc_ref)

        dyv = _cat(dy_ref, cpg)
        dc = _cat(d_ref, cpg)
        w = w_ref[...]
        dsc_ref[...] += jnp.sum(dyv * _dot(dc, w), axis=0, keepdims=True)
        dyp = (dyv * s_ref[...]).astype(BF16)
        _put_chunks(dd_ref, _dot_nt(dyp, w), cpg)
        dwp_ref[...] += _dot_tn(dc, dyp)

    cspec = pl.BlockSpec((cpg, tm, LANE), lambda g, i: (g, i, 0))
    wspec = pl.BlockSpec((None, pg, pg), lambda g, i: (g, 0, 0))
    sspec = pl.BlockSpec((1, pg), lambda g, i: (0, g))
    return _call(
        body, name="pool_mm_bwd", grid=(ng, rows // tm),
        in_specs=[cspec, cspec, wspec, sspec],
        out_specs=[cspec, wspec, sspec],
        out_shape=[jax.ShapeDtypeStruct((nb, rows, LANE), F32), jax.ShapeDtypeStruct((ng, pg, pg), F32),
                   jax.ShapeDtypeStruct((1, ng * pg), F32)],
    )(dy, dm, wp, scale)


def _adamw(w, g, m, v):
    rows = w.shape[0]
    tm = _row_tile(rows, 1024) if rows % 1024 == 0 else rows

    def body(w_ref, g_ref, m_ref, v_ref, d_ref, nm_ref, nv_ref):
        gv = g_ref[...]
        nm = ADAM_B1 * m_ref[...] + (1.0 - ADAM_B1) * gv
        nv = ADAM_B2 * v_ref[...] + (1.0 - ADAM_B2) * jnp.square(gv)
        m_hat = nm / (1.0 - ADAM_B1 ** ADAM_STEP)
        v_hat = nv / (1.0 - ADAM_B2 ** ADAM_STEP)
        d_ref[...] = -ADAM_LR * (m_hat / (jnp.sqrt(v_hat) + ADAM_EPS) + ADAM_WD * w_ref[...])
        nm_ref[...] = nm
        nv_ref[...] = nv

    spec = pl.BlockSpec((tm, LANE), lambda i: (i, 0))
    return _call(body, name="adamw", grid=(rows // tm,), in_specs=[spec] * 4, out_specs=[spec] * 3,
                 out_shape=[jax.ShapeDtypeStruct((rows, LANE), F32)] * 3)(w, g, m, v)


def _place():
    x, y, c = lax.axis_index("x"), lax.axis_index("y"), lax.axis_index("c")
    return x, y, c


def _other_chips(x, y):
    return [(1 - x, y), (x, 1 - y), (1 - x, 1 - y)]


HBM_SPEC = pl.BlockSpec(memory_space=pl.ANY)


def _all_gather_chips(buf, name):
    rows = buf.shape[0]
    half = rows // 2
    assert half * 2 == rows and half % (32 // buf.dtype.itemsize) == 0

    def body(in_ref, out_ref, send_sems, recv_sems, local_sem):
        x, y, c = _place()
        me = 2 * x + y
        chips = _other_chips(x, y)
        mine = pl.ds(c * half, half)
        theirs = pl.ds((1 - c) * half, half)
        local = pltpu.make_async_copy(in_ref, out_ref.at[me], local_sem)
        local.start()

        def push(k, src, dst, to):
            return pltpu.make_async_remote_copy(src_ref=src, dst_ref=dst, send_sem=send_sems.at[k],
                                                recv_sem=recv_sems.at[k], device_id=to, device_id_type=MESH)

        first = [push(j, in_ref.at[mine], out_ref.at[me, mine], (cx, cy, c)) for j, (cx, cy) in enumerate(chips)]
        for cp in first:
            cp.start()
        passed = []
        for j, (cx, cy) in enumerate(chips):
            slab = out_ref.at[2 * cx + cy, mine]
            push(j, slab, slab, (x, y, c)).wait_recv()
            fwd = push(3 + j, slab, slab, (x, y, 1 - c))
            fwd.start()
            passed.append(fwd)
        for j, (cx, cy) in enumerate(chips):
            slab = out_ref.at[2 * cx + cy, theirs]
            push(3 + j, slab, slab, (x, y, c)).wait_recv()
        for cp in first + passed:
            cp.wait_send()
        local.wait()

    return _call(
        body, name=name, in_specs=[HBM_SPEC], out_specs=HBM_SPEC,
        out_shape=jax.ShapeDtypeStruct((N_CHIPS, rows, LANE), buf.dtype),
        scratch_shapes=[pltpu.SemaphoreType.DMA((6,)), pltpu.SemaphoreType.DMA((6,)), pltpu.SemaphoreType.DMA],
    )(buf)


def _sibling_swap(g):
    _, rows, _ = g.shape
    half = rows // 2

    def body(g_ref, out_ref, send_sem, recv_sem):
        x, y, c = _place()
        cp = pltpu.make_async_remote_copy(src_ref=g_ref.at[:, pl.ds((1 - c) * half, half)], dst_ref=out_ref,
                                          send_sem=send_sem, recv_sem=recv_sem, device_id=(x, y, 1 - c), device_id_type=MESH)
        cp.start()
        cp.wait()

    return _call(body, name="rs_sibling_swap", in_specs=[HBM_SPEC], out_specs=HBM_SPEC,
                 out_shape=jax.ShapeDtypeStruct((N_CHIPS, half, LANE), F32),
                 scratch_shapes=[pltpu.SemaphoreType.DMA, pltpu.SemaphoreType.DMA])(g)


def _pair_add(g, got, cidx):
    _, rows, _ = g.shape
    half = rows // 2
    tm = _row_tile(half, 1024) if half % 1024 == 0 else half
    nt = half // tm

    def body(c_ref, a_ref, b_ref, o_ref):
        o_ref[...] = a_ref[...] + b_ref[...]

    return _call(
        body, name="rs_pair_add",
        grid_spec=pltpu.PrefetchScalarGridSpec(
            num_scalar_prefetch=1, grid=(N_CHIPS, nt),
            in_specs=[pl.BlockSpec((None, tm, LANE), lambda s, i, c_ref: (s, c_ref[0] * nt + i, 0)),
                      pl.BlockSpec((None, tm, LANE), lambda s, i, c_ref: (s, i, 0))],
            out_specs=pl.BlockSpec((None, tm, LANE), lambda s, i, c_ref: (s, i, 0))),
        out_shape=jax.ShapeDtypeStruct((N_CHIPS, half, LANE), F32),
    )(cidx, g, got)


def _chip_exchange(p):
    _, half, _ = p.shape

    def body(p_ref, out_ref, send_sems, recv_sems, local_sem):
        x, y, c = _place()
        me = 2 * x + y
        chips = _other_chips(x, y)
        local = pltpu.make_async_copy(p_ref.at[me], out_ref.at[me], local_sem)
        local.start()
        sends = []
        for j, (cx, cy) in enumerate(chips):
            cp = pltpu.make_async_remote_copy(src_ref=p_ref.at[2 * cx + cy], dst_ref=out_ref.at[me], send_sem=send_sems.at[j],
                                              recv_sem=recv_sems.at[j], device_id=(cx, cy, c), device_id_type=MESH)
            cp.start()
            sends.append(cp)
        for j, (cx, cy) in enumerate(chips):
            slab = out_ref.at[2 * cx + cy]
            pltpu.make_async_remote_copy(src_ref=slab, dst_ref=slab, send_sem=send_sems.at[j], recv_sem=recv_sems.at[j],
                                         device_id=(x, y, c), device_id_type=MESH).wait_recv()
        for cp in sends:
            cp.wait_send()
        local.wait()

    return _call(body, name="rs_chip_exchange", in_specs=[HBM_SPEC], out_specs=HBM_SPEC,
                 out_shape=jax.ShapeDtypeStruct((N_CHIPS, half, LANE), F32),
                 scratch_shapes=[pltpu.SemaphoreType.DMA((3,)), pltpu.SemaphoreType.DMA((3,)), pltpu.SemaphoreType.DMA])(p)


def _sum_chips(parts):
    _, half, _ = parts.shape
    tm = _row_tile(half, 1024) if half % 1024 == 0 else half

    def body(p_ref, o_ref):
        o_ref[...] = (p_ref[0] + p_ref[1]) + (p_ref[2] + p_ref[3])

    return _call(body, name="rs_sum_chips", grid=(half // tm,),
                 in_specs=[pl.BlockSpec((N_CHIPS, tm, LANE), lambda i: (0, i, 0))],
                 out_specs=pl.BlockSpec((tm, LANE), lambda i: (i, 0)),
                 out_shape=jax.ShapeDtypeStruct((half, LANE), F32))(parts)


def _sibling_gather(hpart):
    half = hpart.shape[0]

    def body(h_ref, out_ref, send_sem, recv_sem, local_sem):
        x, y, c = _place()
        local = pltpu.make_async_copy(h_ref, out_ref.at[pl.ds(c * half, half)], local_sem)
        local.start()
        cp = pltpu.make_async_remote_copy(src_ref=h_ref, dst_ref=out_ref.at[pl.ds(c * half, half)], send_sem=send_sem,
                                          recv_sem=recv_sem, device_id=(x, y, 1 - c), device_id_type=MESH)
        cp.start()
        other = out_ref.at[pl.ds((1 - c) * half, half)]
        pltpu.make_async_remote_copy(src_ref=other, dst_ref=other, send_sem=send_sem, recv_sem=recv_sem,
                                     device_id=(x, y, c), device_id_type=MESH).wait_recv()
        cp.wait_send()
        local.wait()

    return _call(body, name="rs_sibling_gather", in_specs=[HBM_SPEC], out_specs=HBM_SPEC,
                 out_shape=jax.ShapeDtypeStruct((2 * half, LANE), F32),
                 scratch_shapes=[pltpu.SemaphoreType.DMA, pltpu.SemaphoreType.DMA, pltpu.SemaphoreType.DMA])(hpart)


def _reduce_scatter(g):
    c = lax.axis_index("c")
    got = _sibling_swap(g)
    pair = _pair_add(g, got, jnp.reshape(c, (1,)).astype(jnp.int32))
    parts = _chip_exchange(pair)
    return _sibling_gather(_sum_chips(parts))


def _round_up(n, m):
    return (n + m - 1) // m * m


def _to_rows(flat, rows=None):
    n = flat.shape[-1]
    r = _round_up(n, LANE) // LANE if rows is None else rows
    pad = r * LANE - n
    if pad:
        flat = jnp.pad(flat, [(0, 0)] * (flat.ndim - 1) + [(0, pad)])
    return flat.reshape(flat.shape[:-1] + (r, LANE))


def _f32_as_bf16_rows(a):
    return lax.bitcast_convert_type(a, BF16).reshape(a.shape[0], 2 * LANE).reshape(2 * a.shape[0], LANE)


def _bf16_rows_as_f32(a):
    r = a.shape[-2] // 2
    return lax.bitcast_convert_type(a.reshape(a.shape[:-2] + (r, LANE, 2)), F32)


SHARDED = (("w_mod", 2), ("w_in", 2), ("w_out", 1), ("pool_w", 2), ("conv_w", 2), ("lru_ba", 2), ("lru_bx", 2),
           ("lru_lam", 2), ("pool_scale", 1))
BIG = ("w_mod", "w_in", "w_out", "pool_w")
REPLICATED = ("c_ctx", "b_mod", "ln_g", "ln_b", "conv_b", "lru_wa", "lru_wx")
WEIGHTS = ("c_ctx", "w_mod", "b_mod", "w_in", "w_out", "ln_g", "ln_b", "conv_w", "conv_b", "lru_wa", "lru_ba", "lru_wx",
           "lru_bx", "lru_lam", "pool_w", "pool_scale")


def _shard_major(full, axis):
    shp = full.shape
    n = shp[axis] // N_CHIPS
    t = full.reshape(shp[:axis] + (N_CHIPS, n) + shp[axis + 1:])
    return jnp.moveaxis(t, axis, 0).reshape(N_CHIPS, -1)


def kernel(x, c, ctx, c_ctx, w_mod, b_mod, w_in, w_out, ln_g, ln_b, conv_w, conv_b, lru_wa, lru_ba, lru_wx, lru_bx, lru_lam, pool_w, pool_scale, loss_target, m_c_ctx, m_w_mod, m_b_mod, m_w_in, m_w_out, m_ln_g, m_ln_b, m_conv_w, m_conv_b, m_lru_wa, m_lru_ba, m_lru_wx, m_lru_bx, m_lru_lam, m_pool_w, m_pool_scale, v_c_ctx, v_w_mod, v_b_mod, v_w_in, v_w_out, v_ln_g, v_ln_b, v_conv_w, v_conv_b, v_lru_wa, v_lru_ba, v_lru_wx, v_lru_bx, v_lru_lam, v_pool_w, v_pool_scale):
    weights = dict(c_ctx=c_ctx, w_mod=w_mod, b_mod=b_mod, w_in=w_in, w_out=w_out, ln_g=ln_g, ln_b=ln_b, conv_w=conv_w,
                   conv_b=conv_b, lru_wa=lru_wa, lru_ba=lru_ba, lru_wx=lru_wx, lru_bx=lru_bx, lru_lam=lru_lam,
                   pool_w=pool_w, pool_scale=pool_scale)
    mom1 = dict(c_ctx=m_c_ctx, w_mod=m_w_mod, b_mod=m_b_mod, w_in=m_w_in, w_out=m_w_out, ln_g=m_ln_g, ln_b=m_ln_b,
                conv_w=m_conv_w, conv_b=m_conv_b, lru_wa=m_lru_wa, lru_ba=m_lru_ba, lru_wx=m_lru_wx, lru_bx=m_lru_bx,
                lru_lam=m_lru_lam, pool_w=m_pool_w, pool_scale=m_pool_scale)
    mom2 = dict(c_ctx=v_c_ctx, w_mod=v_w_mod, b_mod=v_b_mod, w_in=v_w_in, w_out=v_w_out, ln_g=v_ln_g, ln_b=v_ln_b,
                conv_w=v_conv_w, conv_b=v_conv_b, lru_wa=v_lru_wa, lru_ba=v_lru_ba, lru_wx=v_lru_wx, lru_bx=v_lru_bx,
                lru_lam=v_lru_lam, pool_w=v_pool_w, pool_scale=v_pool_scale)
    xs, cv, cx = x[0], c, ctx[0]
    s_len, d = xs.shape
    e = w_out.shape[1] * N_CHIPS
    nb = e // LANE
    c3 = w_mod.shape[2]
    n4 = w_in.shape[2]

    big = jnp.concatenate([weights[n].astype(BF16).reshape(-1) for n in BIG])
    small = jnp.concatenate([weights[n].reshape(-1) for n, _ in SHARDED if n not in BIG])
    n_big, n_small = big.shape[0], small.shape[0]
    big_rows = _round_up(n_big, 32 * LANE) // LANE
    small_rows = _round_up(n_small, 16 * LANE) // LANE
    packed = jnp.concatenate([_to_rows(big, big_rows), _f32_as_bf16_rows(_to_rows(small, small_rows))], axis=0)
    gathered = _all_gather_chips(packed, "gather_weights")
    gbig = gathered[:, :big_rows].reshape(N_CHIPS, -1)
    gsmall = _bf16_rows_as_f32(gathered[:, big_rows:]).reshape(N_CHIPS, -1)
    full = {}
    off = 0
    for n in BIG:
        shp = weights[n].shape
        k = int(np_prod(shp))
        full[n] = gbig[:, off:off + k].reshape((N_CHIPS,) + shp)
        off += k
    off = 0
    for n, axis in SHARDED:
        if n in BIG:
            continue
        shp = weights[n].shape
        k = int(np_prod(shp))
        t = gsmall[:, off:off + k].reshape((N_CHIPS,) + shp)
        full[n] = jnp.moveaxis(t, 0, axis).reshape(shp[:axis] + (N_CHIPS * shp[axis],) + shp[axis + 1:])
        off += k
    wm_all = full["w_mod"]
    win_all = full["w_in"]
    wout_all = full["w_out"]
    wp_all = full["pool_w"]
    win = [win_all[:, l] for l in range(DEPTH)]
    wout = [wout_all[:, l].reshape(e, d) for l in range(DEPTH)]
    pg = wp_all.shape[-1]
    wp = jnp.moveaxis(wp_all[:, 0], 0, 1).reshape(len(POOL_WINDOWS), pg, pg)
    conv_w_f, lru_ba_f, lru_bx_f, lru_lam_f = full["conv_w"][0], full["lru_ba"][0], full["lru_bx"][0], full["lru_lam"][0]
    scale_f = full["pool_scale"]
    wa_b, wx_b = lru_wa[0].astype(BF16), lru_wx[0].astype(BF16)

    cc = jnp.concatenate([cv, c_ctx[None, :], jnp.zeros((6, d), F32)], axis=0)
    mod = _mod_fwd(cc, wm_all, b_mod[:, None, :])

    def mod_parts(l, row):
        v = mod[l, row]
        return v[None, :d], 1.0 + v[None, d:2 * d], v[None, 2 * d:]

    sh0, sc0, gt0 = mod_parts(0, 0)
    shc, scc, _ = mod_parts(0, 1)
    sh1, sc1, gt1 = mod_parts(1, 0)
    lg = [ln_g[l][None, :] for l in range(DEPTH)]
    lb = [ln_b[l][None, :] for l in range(DEPTH)]

    ug0 = _inproj_fwd(xs, sc0, sh0, win[0], "inproj_fwd0")
    uc0 = _inproj_fwd(cx, scc, shc, win[0][:2], "inproj_fwd_ctx")
    lru_args = (conv_w_f, conv_b, wa_b, wx_b, lru_ba_f, lru_bx_f, lru_lam_f)
    y0 = _rglru_fwd(ug0, uc0, *lru_args)
    br0, x1 = _outproj_fwd(y0, ug0, xs, gt0, wout[0], lg[0], lb[0], None, "outproj_fwd0")
    ug1 = _inproj_fwd(x1, sc1, sh1, win[1], "inproj_fwd1")
    d1 = _pool_map(ug1, nb, False, True, "pool_fwd")
    y1 = _pool_mm_fwd(d1, wp, scale_f)
    br1, dxo, loss_part = _outproj_fwd(y1, ug1, x1, gt1, wout[1], lg[1], lb[1], loss_target[0], "outproj_fwd1")
    loss = lax.psum(loss_part[0, 0] * (0.5 / d), ("x", "y", "c"))

    dy1, dg1, dxres1, dbr1, dlg1, dlb1, dgt1 = _outproj_bwd(dxo, x1, br1, y1, ug1, gt1, lg[1], wout[1], "outproj_bwd1")
    dwout1 = _outproj_bwd_w(y1, ug1, dbr1, "outproj_bwd_w1")
    dd1, dwp, dscale = _pool_mm_bwd(dy1, d1, wp, scale_f)
    du1 = _pool_map(dd1, nb, True, False, "pool_bwd")
    dx1, dsc1, dsh1 = _inproj_bwd_x([du1, dg1], x1, dxres1, sc1, win[1], "inproj_bwd_x1")
    dwin1 = _inproj_bwd_w(x1, sc1, sh1, [du1, dg1], None, "inproj_bwd_w1")

    dy0, dg0, dxres0, dbr0, dlg0, dlb0, dgt0 = _outproj_bwd(dx1, xs, br0, y0, ug0, gt0, lg[0], wout[0], "outproj_bwd0")
    dwout0 = _outproj_bwd_w(y0, ug0, dbr0, "outproj_bwd_w0")
    du0, duc, dconv_w, dconv_b, dwa, dwx, dba, dbx, dlam = _rglru_bwd(ug0, uc0, dy0, *lru_args)
    grad_x, dsc0, dsh0 = _inproj_bwd_x([du0, dg0], xs, dxres0, sc0, win[0], "inproj_bwd_x0")
    dscc, dshc = _inproj_bwd_x([duc], cx, None, scc, win[0][:2], "inproj_bwd_x_ctx")
    dwin0c = _inproj_bwd_w(cx, scc, shc, [duc, jnp.zeros_like(duc)], None, "inproj_bwd_w_ctx")
    dwin0 = _inproj_bwd_w(xs, sc0, sh0, [du0, dg0], dwin0c, "inproj_bwd_w0")

    zd = jnp.zeros((1, d), F32)
    dm0 = jnp.concatenate([jnp.concatenate([dsh0, dsc0, dgt0], axis=1), jnp.concatenate([dshc, dscc, zd], axis=1),
                           jnp.zeros((6, 3 * d), F32)], axis=0)
    dm1 = jnp.concatenate([jnp.concatenate([dsh1, dsc1, dgt1], axis=1), jnp.zeros((7, 3 * d), F32)], axis=0)
    dwm, dbm, dcc = _mod_bwd(cc, jnp.stack([dm0, dm1]), wm_all)

    local = {
        "w_mod": dwm, "w_in": jnp.stack([dwin0, dwin1], axis=1), "w_out": jnp.stack([dwout0, dwout1]),
        "pool_w": dwp[None], "conv_w": dconv_w[None], "lru_ba": dba[None], "lru_bx": dbx[None], "lru_lam": dlam[None],
        "pool_scale": dscale,
        "c_ctx": dcc[1], "b_mod": dbm[:, 0], "ln_g": jnp.concatenate([dlg0, dlg1]), "ln_b": jnp.concatenate([dlb0, dlb1]),
        "conv_b": dconv_b, "lru_wa": dwa[None], "lru_wx": dwx[None],
    }
    slabs = []
    for n, axis in SHARDED:
        if n == "w_in":
            slabs.append(local[n].reshape(N_CHIPS, -1))
        else:
            slabs.append(_shard_major(local[n], axis))
    rep = jnp.concatenate([local[n].reshape(-1) for n in REPLICATED])
    n_rep = rep.shape[0]
    rep_rows = _round_up(n_rep, N_CHIPS * 16 * LANE) // (N_CHIPS * LANE)
    rep4 = jnp.pad(rep, (0, N_CHIPS * rep_rows * LANE - n_rep)).reshape(N_CHIPS, rep_rows * LANE)
    n_main = sum(s.shape[1] for s in slabs)
    main_rows = _round_up(_round_up(n_main, LANE) // LANE + rep_rows, 2048) - rep_rows
    main4 = jnp.concatenate(slabs, axis=1)
    main4 = jnp.pad(main4, ((0, 0), (0, main_rows * LANE - n_main)))
    gbuf = jnp.concatenate([main4, rep4], axis=1).reshape(N_CHIPS, main_rows + rep_rows, LANE)
    red = _reduce_scatter(gbuf)
    rep_all = _all_gather_chips(red[main_rows:], "gather_replicated").reshape(-1)[:n_rep]
    g_main = red[:main_rows].reshape(-1)

    def flat_of(src):
        parts = [src[n].reshape(-1) for n, _ in SHARDED]
        parts.append(jnp.zeros((main_rows * LANE - n_main,), F32))
        parts += [src[n].reshape(-1) for n in REPLICATED]
        return _to_rows(jnp.concatenate(parts), None)

    g_all = _to_rows(jnp.concatenate([g_main, rep_all]))
    total_rows = _round_up(g_all.shape[0], 1024)
    padr = lambda a: jnp.pad(a, ((0, total_rows - a.shape[0]), (0, 0)))
    w_all, m_all, v_all, g_all = padr(flat_of(weights)), padr(flat_of(mom1)), padr(flat_of(mom2)), padr(g_all)
    delta, new_m, new_v = _adamw(w_all, g_all, m_all, v_all)

    def split(flat_rows):
        flat = flat_rows.reshape(-1)
        out, off = {}, 0
        for n, _ in SHARDED:
            k = int(np_prod(weights[n].shape))
            out[n] = flat[off:off + k].reshape(weights[n].shape)
            off += k
        off = main_rows * LANE
        for n in REPLICATED:
            k = int(np_prod(weights[n].shape))
            out[n] = flat[off:off + k].reshape(weights[n].shape)
            off += k
        return out

    outs = [split(a) for a in (g_all, delta, new_m, new_v)]
    result = [loss, grad_x[None]]
    for group in outs:
        result += [group[n] for n in WEIGHTS]
    return tuple(result)


def np_prod(shape):
    p = 1
    for s in shape:
        p *= int(s)
    return p
```

```python
import jax
import jax.numpy as jnp
from jax import lax
from jax.experimental import pallas as pl
from jax.experimental.pallas import tpu as pltpu

F32 = jnp.float32
BF16 = jnp.bfloat16
LANE = 128
SUB = 8
GRID_W = 64
POOL_WINDOWS = (2, 4, 8, 16)
LRU_C = 8.0
DEPTH = 2
ALPHA = float((2 * DEPTH) ** 0.25)
LN_EPS = 1e-5
ADAM_LR, ADAM_B1, ADAM_B2, ADAM_EPS, ADAM_WD, ADAM_STEP = 0.001, 0.9, 0.999, 1e-08, 0.01, 10
N_CHIPS = 4
N_DEV = 8
MESH = pl.DeviceIdType.MESH
ROW_TILE = 512
CONV_TAPS = 4
CONV_LEFT = 2
PAD = 8
SCAN_UNROLL = 8
RS_TILE = 128
VEC_KINDS = 4


def _call(body, **kw):
    return pl.pallas_call(body, **kw)


def _dot(a, b):
    return jnp.dot(a, b, preferred_element_type=F32)


def _dot_nt(a, b):
    return lax.dot_general(a, b, (((1,), (1,)), ((), ())), preferred_element_type=F32)


def _dot_tn(a, b):
    return lax.dot_general(a, b, (((0,), (0,)), ((), ())), preferred_element_type=F32)


def _sigmoid(v):
    return 0.5 * (jnp.tanh(0.5 * v) + 1.0)


def _silu(v):
    return v * _sigmoid(v)


def _dsilu(v):
    s = _sigmoid(v)
    return s * (1.0 + v * (1.0 - s))


def _log_sigmoid(v):
    z = jnp.exp(-jnp.abs(v))
    return jnp.minimum(v, 0.0) - jnp.where(z < 1e-4, z * (1.0 - 0.5 * z), jnp.log(1.0 + z))


def _one_minus_sq(la, a):
    return -jnp.tanh(la) * (a * a + 1.0)


def _cat(ref, n):
    return jnp.concatenate([ref[k] for k in range(n)], axis=1)


def _put_chunks(ref, val, n, base=0):
    for k in range(n):
        ref[base + k] = val[:, k * LANE:(k + 1) * LANE].astype(ref.dtype)


def _row_tile(rows, want):
    t = min(rows, want)
    assert rows % t == 0
    return t


ANY_SPEC = pl.BlockSpec(memory_space=pl.ANY)


def _mod_fwd(cc, wm, bm):
    ns, nl, d, c3 = wm.shape

    def body(cc_ref, w_ref, b_ref, o_ref):
        o_ref[...] = _dot(_silu(cc_ref[...]).astype(BF16), w_ref[...]) + b_ref[...]

    return _call(
        body, name="mod_fwd", grid=(nl, ns),
        in_specs=[pl.BlockSpec((8, d), lambda l, s: (0, 0)),
                  pl.BlockSpec((None, None, d, c3), lambda l, s: (s, l, 0, 0)),
                  pl.BlockSpec((None, 1, c3), lambda l, s: (l, 0, s))],
        out_specs=pl.BlockSpec((None, 8, c3), lambda l, s: (l, 0, s)),
        out_shape=jax.ShapeDtypeStruct((nl, 8, ns * c3), F32),
    )(cc, wm, bm)


def _mod_bwd_shard(gt, cctx, place, c3):
    d = cctx.shape[1]

    def body(p_ref, cs_ref, dm_ref, dmx_ref, cx_ref, o_ref):
        l = pl.program_id(0)
        lhs = jnp.concatenate([_silu(cs_ref[...]), _silu(cx_ref[...]), jnp.zeros((7, d), F32)], axis=0).astype(BF16)
        dmx = jnp.where(l == 0, jnp.sum(dmx_ref[...], axis=0, keepdims=True), 0.0)
        rhs = jnp.concatenate([dm_ref[...], dmx, jnp.zeros((7, c3), F32)], axis=0).astype(BF16)
        o_ref[...] = _dot_tn(lhs, rhs)

    return _call(
        body, name="mod_bwd_shard",
        grid_spec=pltpu.PrefetchScalarGridSpec(
            num_scalar_prefetch=1, grid=(DEPTH,),
            in_specs=[pl.BlockSpec((None, N_DEV, d), lambda l, p: (0, 0, 0)),
                      pl.BlockSpec((None, N_DEV, c3), lambda l, p: (1 + 2 * l, 0, p[1])),
                      pl.BlockSpec((None, N_DEV, c3), lambda l, p: (2, 0, p[1])),
                      pl.BlockSpec((1, d), lambda l, p: (0, 0))],
            out_specs=pl.BlockSpec((None, d, c3), lambda l, p: (l, 0, 0))),
        out_shape=jax.ShapeDtypeStruct((DEPTH, d, c3), F32),
    )(place, gt, gt, gt, cctx)


def _mod_bwd_rep(gt, cctx, wm):
    ns, _, d, c3 = wm.shape
    kinds = gt.shape[0]

    def body(g_ref, cx_ref, w_ref, db_ref, dc_ref, sm_ref):
        dm0 = jnp.sum(g_ref[1], axis=0, keepdims=True)
        dmx = jnp.sum(g_ref[2], axis=0, keepdims=True)
        dm1 = jnp.sum(g_ref[3], axis=0, keepdims=True)
        db_ref[0:1, :] = dm0 + dmx
        db_ref[1:2, :] = dm1
        dmxb = jnp.broadcast_to(dmx, (SUB, ns * c3)).astype(BF16)
        acc = jnp.zeros((SUB, d), F32)
        for s in range(ns):
            acc = acc + _dot_nt(dmxb[:, s * c3:(s + 1) * c3], w_ref[s])
        dc_ref[...] = acc[0:1, :] * _dsilu(cx_ref[...])
        for k in range(VEC_KINDS, kinds):
            sm_ref[k - VEC_KINDS:k - VEC_KINDS + 1, :] = jnp.sum(g_ref[k], axis=0, keepdims=True)

    return _call(
        body, name="mod_bwd_rep", grid=(1,),
        in_specs=[pl.BlockSpec(gt.shape, lambda i: (0, 0, 0)),
                  pl.BlockSpec((1, d), lambda i: (0, 0)),
                  pl.BlockSpec((ns, None, d, c3), lambda i: (0, 0, 0, 0))],
        out_specs=[pl.BlockSpec((DEPTH, ns * c3), lambda i: (0, 0)), pl.BlockSpec((1, d), lambda i: (0, 0)),
                   pl.BlockSpec((kinds - VEC_KINDS, ns * c3), lambda i: (0, 0))],
        out_shape=[jax.ShapeDtypeStruct((DEPTH, ns * c3), F32), jax.ShapeDtypeStruct((1, d), F32),
                   jax.ShapeDtypeStruct((kinds - VEC_KINDS, ns * c3), F32)],
    )(gt, cctx, wm)


def _inproj_fwd(xin, sc1, sh, w, name):
    rows, d = xin.shape
    ns, _, n4 = w.shape
    cpb = n4 // LANE
    tm = _row_tile(rows, 256)

    def body(x_ref, sc_ref, sh_ref, w_ref, o_ref):
        h = (x_ref[...] * sc_ref[...] + sh_ref[...]).astype(BF16)
        for s in range(ns):
            _put_chunks(o_ref, _dot(h, w_ref[s]), cpb, base=s * cpb)

    return _call(
        body, name=name, grid=(rows // tm,),
        in_specs=[pl.BlockSpec((tm, d), lambda i: (i, 0)),
                  pl.BlockSpec((1, d), lambda i: (0, 0)),
                  pl.BlockSpec((1, d), lambda i: (0, 0)),
                  pl.BlockSpec((ns, d, n4), lambda i: (0, 0, 0))],
        out_specs=pl.BlockSpec((ns * cpb, tm, LANE), lambda i: (0, i, 0)),
        out_shape=jax.ShapeDtypeStruct((ns * cpb, rows, LANE), F32),
    )(xin, sc1, sh, w)


def _inproj_bwd_x(dparts, xin, dxres, sc1, w, name):
    rows, d = xin.shape
    npart = len(dparts)
    e = dparts[0].shape[1]
    ns, _, n4 = w.shape
    per = e // n4
    assert per * npart == ns
    tm = _row_tile(rows, 256)
    has_res = dxres is not None

    def body(*refs):
        dp = refs[:npart]
        x_ref, sc_ref, w_ref = refs[npart:npart + 3]
        rest = refs[npart + 3:]
        if has_res:
            res_ref, dx_ref, dsc_ref, dsh_ref = rest
        else:
            dsc_ref, dsh_ref = rest
        i = pl.program_id(0)
        dh = jnp.zeros((tm, d), F32)
        for p in range(npart):
            v = dp[p][...]
            for q in range(per):
                dh = dh + _dot_nt(v[:, q * n4:(q + 1) * n4], w_ref[p * per + q])

        @pl.when(i == 0)
        def _():
            dsc_ref[...] = jnp.zeros_like(dsc_ref)
            dsh_ref[...] = jnp.zeros_like(dsh_ref)

        dsc_ref[...] += jnp.sum(dh * x_ref[...], axis=0, keepdims=True)
        dsh_ref[...] += jnp.sum(dh, axis=0, keepdims=True)
        if has_res:
            dx_ref[...] = res_ref[...] + dh * sc_ref[...]

    row_spec = pl.BlockSpec((tm, d), lambda i: (i, 0))
    vec_spec = pl.BlockSpec((1, d), lambda i: (0, 0))
    in_specs = [pl.BlockSpec((tm, e), lambda i: (i, 0))] * npart + [row_spec, vec_spec,
                                                                     pl.BlockSpec((ns, d, n4), lambda i: (0, 0, 0))]
    args = list(dparts) + [xin, sc1, w]
    out_specs, out_shape = [vec_spec, vec_spec], [jax.ShapeDtypeStruct((1, d), F32)] * 2
    if has_res:
        in_specs.append(row_spec)
        args.append(dxres)
        out_specs = [row_spec] + out_specs
        out_shape = [jax.ShapeDtypeStruct((rows, d), F32)] + out_shape
    return _call(body, name=name, grid=(rows // tm,), in_specs=in_specs, out_specs=out_specs, out_shape=out_shape)(*args)


def _inproj_bwd_w(xin, sc1, sh, dparts, init, gbuf, name):
    rows, d = xin.shape
    npart = len(dparts)
    e = dparts[0].shape[1]
    n4 = e // 2
    ns = 2 * npart
    tm = _row_tile(rows, 512)
    nt = rows // tm
    has_init = init is not None
    into = gbuf is not None
    assert not into or (ns == N_CHIPS and gbuf.shape[2] == n4)

    def body(*refs):
        x_ref, sc_ref, sh_ref = refs[:3]
        dp = refs[3:3 + npart]
        init_ref = refs[3 + npart] if has_init else None
        o_ref = refs[-1]
        s, i = pl.program_id(0), pl.program_id(1)
        h = (x_ref[...] * sc_ref[...] + sh_ref[...]).astype(BF16)

        @pl.when(i == 0)
        def _():
            o_ref[...] = init_ref[...] if has_init else jnp.zeros_like(o_ref)

        for p in range(npart):
            @pl.when(s // 2 == p)
            def _(p=p):
                o_ref[...] += _dot_tn(h, dp[p][...])

    in_specs = [pl.BlockSpec((tm, d), lambda s, i: (i, 0)),
                pl.BlockSpec((1, d), lambda s, i: (0, 0)),
                pl.BlockSpec((1, d), lambda s, i: (0, 0))]
    in_specs += [pl.BlockSpec((tm, n4), lambda s, i: (i, s % 2))] * npart
    args = [xin, sc1, sh] + list(dparts)
    o_spec = pl.BlockSpec((None, d, n4), lambda s, i: (s, 0, 0))
    if has_init:
        in_specs.append(o_spec)
        args.append(init)
    extra = {}
    if into:
        in_specs.append(ANY_SPEC)
        args.append(gbuf)
        extra = dict(input_output_aliases={len(args) - 1: 0})
    out_shape = jax.ShapeDtypeStruct(gbuf.shape if into else (ns, d, n4), F32)
    return _call(body, name=name, grid=(ns, nt), in_specs=in_specs, out_specs=o_spec, out_shape=out_shape, **extra)(*args)


def _ln_stats(r):
    mu = jnp.mean(r, axis=-1, keepdims=True)
    var = jnp.mean(jnp.square(r - mu), axis=-1, keepdims=True)
    rstd = lax.rsqrt(var + LN_EPS)
    return (r - mu) * rstd, rstd


def _outproj_fwd(y, ug, xin, gt, wout, lg, lb, target, name):
    nch, rows, _ = y.shape
    e, d = wout.shape
    tm = _row_tile(rows, 256)
    with_loss = target is not None

    def body(*refs):
        y_ref, g_ref, x_ref, gt_ref, w_ref, lg_ref, lb_ref = refs[:7]
        if with_loss:
            t_ref, br_ref, dxo_ref, loss_ref = refs[7:]
        else:
            br_ref, xo_ref = refs[7:]
        z = jnp.concatenate([(y_ref[k] * _silu(g_ref[k])).astype(BF16) for k in range(nch)], axis=1)
        br = _dot(z, w_ref[...])
        br_ref[...] = br
        xhat, _ = _ln_stats(ALPHA * x_ref[...] + gt_ref[...] * br)
        xo = xhat * lg_ref[...] + lb_ref[...]
        if with_loss:
            err = xo - t_ref[...]
            dxo_ref[...] = err * (1.0 / d)

            @pl.when(pl.program_id(0) == 0)
            def _():
                loss_ref[...] = jnp.zeros_like(loss_ref)

            loss_ref[...] += jnp.sum(err * err)
        else:
            xo_ref[...] = xo

    chunk_spec = pl.BlockSpec((nch, tm, LANE), lambda i: (0, i, 0))
    g_spec = pl.BlockSpec((nch, tm, LANE), lambda i: (1, i, 0))
    row_spec = pl.BlockSpec((tm, d), lambda i: (i, 0))
    vec_spec = pl.BlockSpec((1, d), lambda i: (0, 0))
    in_specs = [chunk_spec, g_spec, row_spec, vec_spec, pl.BlockSpec((e, d), lambda i: (0, 0)), vec_spec, vec_spec]
    args = [y, ug, xin, gt, wout, lg, lb]
    out_specs = [row_spec, row_spec]
    out_shape = [jax.ShapeDtypeStruct((rows, d), F32)] * 2
    if with_loss:
        in_specs.append(row_spec)
        args.append(target)
        out_specs.append(pl.BlockSpec((1, LANE), lambda i: (0, 0)))
        out_shape.append(jax.ShapeDtypeStruct((1, LANE), F32))
    return _call(body, name=name, grid=(rows // tm,), in_specs=in_specs, out_specs=out_specs, out_shape=out_shape)(*args)


def _outproj_bwd(dxo, xin, br, y, ug, gt, lg, wout, name):
    nch, rows, _ = y.shape
    e, d = wout.shape
    tm = _row_tile(rows, 256)

    def body(dxo_ref, x_ref, br_ref, y_ref, g_ref, gt_ref, lg_ref, w_ref,
             dy_ref, dg_ref, dxres_ref, dbr_ref, dlg_ref, dlb_ref, dgt_ref):
        dxo_v = dxo_ref[...]
        brv = br_ref[...]
        xhat, rstd = _ln_stats(ALPHA * x_ref[...] + gt_ref[...] * brv)
        dxh = dxo_v * lg_ref[...]
        dr = rstd * (dxh - jnp.mean(dxh, axis=-1, keepdims=True) - xhat * jnp.mean(dxh * xhat, axis=-1, keepdims=True))

        @pl.when(pl.program_id(0) == 0)
        def _():
            dlg_ref[...] = jnp.zeros_like(dlg_ref)
            dlb_ref[...] = jnp.zeros_like(dlb_ref)
            dgt_ref[...] = jnp.zeros_like(dgt_ref)

        dlg_ref[...] += jnp.sum(dxo_v * xhat, axis=0, keepdims=True)
        dlb_ref[...] += jnp.sum(dxo_v, axis=0, keepdims=True)
        dgt_ref[...] += jnp.sum(dr * brv, axis=0, keepdims=True)
        dxres_ref[...] = ALPHA * dr
        dbr = (gt_ref[...] * dr).astype(BF16)
        dbr_ref[...] = dbr
        dz = _dot_nt(dbr, w_ref[...])
        for k in range(nch):
            dzk = dz[:, k * LANE:(k + 1) * LANE]
            gk = g_ref[k]
            dy_ref[k] = dzk * _silu(gk)
            dg_ref[:, k * LANE:(k + 1) * LANE] = (dzk * y_ref[k] * _dsilu(gk)).astype(BF16)

    chunk_spec = pl.BlockSpec((nch, tm, LANE), lambda i: (0, i, 0))
    g_spec = pl.BlockSpec((nch, tm, LANE), lambda i: (1, i, 0))
    row_spec = pl.BlockSpec((tm, d), lambda i: (i, 0))
    vec_spec = pl.BlockSpec((1, d), lambda i: (0, 0))
    return _call(
        body, name=name, grid=(rows // tm,),
        in_specs=[row_spec, row_spec, row_spec, chunk_spec, g_spec, vec_spec, vec_spec, pl.BlockSpec((e, d), lambda i: (0, 0))],
        out_specs=[chunk_spec, pl.BlockSpec((tm, e), lambda i: (i, 0)), row_spec, row_spec, vec_spec, vec_spec, vec_spec],
        out_shape=[jax.ShapeDtypeStruct((nch, rows, LANE), F32), jax.ShapeDtypeStruct((rows, e), BF16),
                   jax.ShapeDtypeStruct((rows, d), F32), jax.ShapeDtypeStruct((rows, d), BF16)]
        + [jax.ShapeDtypeStruct((1, d), F32)] * 3,
    )(dxo, xin, br, y, ug, gt, lg, wout)


def _outproj_bwd_w(y, ug, dbr, gbuf, row0, name):
    nch, rows, _ = y.shape
    d = dbr.shape[1]
    e = nch * LANE
    es = e // N_CHIPS
    tm = _row_tile(rows, 512)
    assert gbuf.shape[2] == d and row0 % es == 0

    def body(y_ref, g_ref, dbr_ref, buf_ref, o_ref):
        @pl.when(pl.program_id(0) == 0)
        def _():
            o_ref[...] = jnp.zeros_like(o_ref)

        z = jnp.concatenate([(y_ref[k] * _silu(g_ref[k])).astype(BF16) for k in range(nch)], axis=1)
        o_ref[...] += _dot_tn(z, dbr_ref[...]).reshape(N_CHIPS, es, d)

    return _call(
        body, name=name, grid=(rows // tm,),
        in_specs=[pl.BlockSpec((nch, tm, LANE), lambda i: (0, i, 0)),
                  pl.BlockSpec((nch, tm, LANE), lambda i: (1, i, 0)),
                  pl.BlockSpec((tm, d), lambda i: (i, 0)),
                  ANY_SPEC],
        out_specs=pl.BlockSpec((N_CHIPS, es, d), lambda i: (0, row0 // es, 0)),
        out_shape=jax.ShapeDtypeStruct(gbuf.shape, F32),
        input_output_aliases={3: 0},
    )(y, ug, dbr, gbuf)


def _scan(a_ref, b_ref, h_ref, *, length, init, reverse, a_shift, store):
    nblk = length // SUB
    assert nblk % SCAN_UNROLL == 0
    row = lax.broadcasted_iota(jnp.int32, (SUB, LANE), 0)
    last = 0 if reverse else SUB - 1
    edges = [(row >= SUB - k) if reverse else (row < k) for k in (1, 2, 4)]

    def local_scan(a, b):
        for k, edge in zip((1, 2, 4), edges):
            sh = (SUB - k) if reverse else k
            b = b + a * jnp.where(edge, 0.0, pltpu.roll(b, sh, 0))
            a = a * jnp.where(edge, 1.0, pltpu.roll(a, sh, 0))
        return a, b

    def step(i, carry):
        base = pl.multiple_of(((nblk // SCAN_UNROLL - 1 - i) if reverse else i) * (SCAN_UNROLL * SUB), SCAN_UNROLL * SUB)
        order = range(SCAN_UNROLL - 1, -1, -1) if reverse else range(SCAN_UNROLL)
        loaded = [(a_ref[pl.ds(PAD + base + j * SUB + a_shift, SUB), :], b_ref[pl.ds(PAD + base + j * SUB, SUB), :])
                  for j in order]
        scanned = [local_scan(a, b) for a, b in loaded]
        for j, (a, b) in zip(order, scanned):
            if store:
                h_ref[pl.ds(PAD + base + j * SUB, SUB), :] = b + a * carry
            a_l = jnp.broadcast_to(a[last:last + 1, :], (SUB, LANE))
            b_l = jnp.broadcast_to(b[last:last + 1, :], (SUB, LANE))
            carry = b_l + a_l * carry
        return carry

    carry = lax.fori_loop(0, nblk // SCAN_UNROLL, step, jnp.broadcast_to(init, (SUB, LANE)))
    return carry[0:1, :]


def _conv_fwd(src_ref, upad, u_ref, cw, cb, length):
    zeros = jnp.zeros((PAD, LANE), F32)
    upad[pl.ds(0, PAD), :] = zeros
    upad[pl.ds(PAD + length, PAD), :] = zeros
    rt = _row_tile(length, ROW_TILE)

    def copy(i, c):
        t0 = pl.multiple_of(i * rt, rt)
        upad[pl.ds(PAD + t0, rt), :] = src_ref[pl.ds(t0, rt), :]
        return c

    lax.fori_loop(0, length // rt, copy, 0)

    def tile(i, c):
        t0 = pl.multiple_of(i * rt, rt)
        acc = jnp.zeros((rt, LANE), F32)
        for k in range(CONV_TAPS):
            acc = acc + upad[pl.ds(t0 + PAD - CONV_LEFT + k, rt), :] * cw[k:k + 1, :]
        u_ref[pl.ds(t0, rt), :] = acc + cb
        return c

    lax.fori_loop(0, length // rt, tile, 0)


def _gates_fwd(u_ref, a_ref, b_ref, wa, wx, ba, bx, ls, length):
    rt = _row_tile(length, ROW_TILE)

    def tile(i, c):
        t0 = pl.multiple_of(i * rt, rt)
        ut = u_ref[pl.ds(t0, rt), :]
        ub = ut.astype(BF16)
        r = _sigmoid(_dot(ub, wa) + ba)
        ig = _sigmoid(_dot(ub, wx) + bx)
        la = (LRU_C * r) * ls
        a = jnp.exp(la)
        a_ref[pl.ds(PAD + t0, rt), :] = a
        b_ref[pl.ds(PAD + t0, rt), :] = jnp.sqrt(_one_minus_sq(la, a)) * (ig * ut)
        return c

    lax.fori_loop(0, length // rt, tile, 0)


def _lru_specs():
    return [pl.BlockSpec((CONV_TAPS, LANE), lambda n: (0, n)),
            pl.BlockSpec((1, LANE), lambda n: (0, n)),
            pl.BlockSpec((2, None, LANE, LANE), lambda n: (0, n, 0, 0)),
            pl.BlockSpec((2, None, LANE, LANE), lambda n: (0, n, 0, 0)),
            pl.BlockSpec((2, LANE), lambda n: (0, n)),
            pl.BlockSpec((2, LANE), lambda n: (0, n)),
            pl.BlockSpec((2, LANE), lambda n: (0, n))]


def _rglru_fwd(ug, uc, conv_w, conv_b, wa, wx, ba, bx, lam):
    nb = uc.shape[0]
    s_len, t_len = ug.shape[1], uc.shape[1]

    def body(u0_ref, uc0_ref, cw_ref, cb_ref, wa_ref, wx_ref, ba_ref, bx_ref, lam_ref, y_ref,
             upad, ubuf, abuf, hbuf):
        cw, cb = cw_ref[...], cb_ref[...]
        lsig = _log_sigmoid(lam_ref[...])
        zero = jnp.zeros((1, LANE), F32)
        _conv_fwd(uc0_ref, upad, ubuf, cw, cb, t_len)
        h0 = []
        for dr in range(2):
            _gates_fwd(ubuf, abuf, hbuf, wa_ref[dr], wx_ref[dr], ba_ref[dr:dr + 1, :], bx_ref[dr:dr + 1, :],
                       lsig[dr:dr + 1, :], t_len)
            h0.append(_scan(abuf, hbuf, hbuf, length=t_len, init=zero, reverse=(dr == 1), a_shift=0, store=False))
        _conv_fwd(u0_ref, upad, ubuf, cw, cb, s_len)
        rt = _row_tile(s_len, ROW_TILE)
        for dr in range(2):
            _gates_fwd(ubuf, abuf, hbuf, wa_ref[dr], wx_ref[dr], ba_ref[dr:dr + 1, :], bx_ref[dr:dr + 1, :],
                       lsig[dr:dr + 1, :], s_len)
            _scan(abuf, hbuf, hbuf, length=s_len, init=h0[dr], reverse=(dr == 1), a_shift=0, store=True)

            def acc(i, c, dr=dr):
                t0 = pl.multiple_of(i * rt, rt)
                h = hbuf[pl.ds(PAD + t0, rt), :]
                if dr == 0:
                    y_ref[pl.ds(t0, rt), :] = h
                else:
                    y_ref[pl.ds(t0, rt), :] += h
                return c

            lax.fori_loop(0, s_len // rt, acc, 0)

    seq = pltpu.VMEM((s_len + 2 * PAD, LANE), F32)
    return _call(
        body, name="rglru_fwd", grid=(nb,),
        in_specs=[pl.BlockSpec((None, s_len, LANE), lambda n: (n, 0, 0)),
                  pl.BlockSpec((None, t_len, LANE), lambda n: (n, 0, 0))] + _lru_specs(),
        out_specs=pl.BlockSpec((None, s_len, LANE), lambda n: (n, 0, 0)),
        out_shape=jax.ShapeDtypeStruct((nb, s_len, LANE), F32),
        scratch_shapes=[seq, pltpu.VMEM((s_len, LANE), F32), seq, seq],
    )(ug, uc, conv_w, conv_b, wa, wx, ba, bx, lam)


def _rglru_bwd(ug, uc, dy, conv_w, conv_b, wa, wx, ba, bx, lam):
    nb = uc.shape[0]
    e = nb * LANE
    s_len, t_len = ug.shape[1], uc.shape[1]

    def body(u0_ref, uc0_ref, dy_ref, cw_ref, cb_ref, wa_ref, wx_ref, ba_ref, bx_ref, lam_ref,
             du_ref, duc_ref, dcw_ref, dcb_ref, dwa_ref, dwx_ref, dba_ref, dbx_ref, dlam_ref,
             upad, ubuf, abuf, hbuf, lbuf, dubuf, cpad, cu, ca0, ch0, ca1, ch1):
        cw, cb = cw_ref[...], cb_ref[...]
        lam_v = lam_ref[...]
        lsig = _log_sigmoid(lam_v)
        zero = jnp.zeros((1, LANE), F32)
        zpad = jnp.zeros((PAD, LANE), F32)
        for ref in (dcw_ref, dcb_ref, dwa_ref, dwx_ref, dba_ref, dbx_ref, dlam_ref):
            ref[...] = jnp.zeros_like(ref)

        def params(dr):
            return (wa_ref[dr], wx_ref[dr], ba_ref[dr:dr + 1, :], bx_ref[dr:dr + 1, :], lsig[dr:dr + 1, :])

        def direction_bwd(dr, u_ref, a_ref, h_ref, l_ref, dub, length, first):
            wa_d, wx_d, ba_d, bx_d, ls_d = params(dr)
            rt = _row_tile(length, ROW_TILE)
            prev = 1 if dr == 1 else -1

            def tile(i, c):
                t0 = pl.multiple_of(i * rt, rt)
                ut = u_ref[pl.ds(t0, rt), :]
                ub = ut.astype(BF16)
                r = _sigmoid(_dot(ub, wa_d) + ba_d)
                ig = _sigmoid(_dot(ub, wx_d) + bx_d)
                la = (LRU_C * r) * ls_d
                a = a_ref[pl.ds(PAD + t0, rt), :]
                q = _one_minus_sq(la, a)
                rs = lax.rsqrt(q)
                sq = q * rs
                lm = l_ref[pl.ds(PAD + t0, rt), :]
                da = lm * h_ref[pl.ds(PAD + t0 + prev, rt), :]
                dsq = lm * ig * ut
                dig = lm * sq * ut
                dla = da * a - dsq * (a * a) * rs
                dr_ = dla * (LRU_C * ls_d)
                dlam_ref[dr:dr + 1, :] += jnp.sum(dla * (LRU_C * r), axis=0, keepdims=True)
                dpr = dr_ * r * (1.0 - r)
                dpi = dig * ig * (1.0 - ig)
                dba_ref[dr:dr + 1, :] += jnp.sum(dpr, axis=0, keepdims=True)
                dbx_ref[dr:dr + 1, :] += jnp.sum(dpi, axis=0, keepdims=True)
                dprb, dpib = dpr.astype(BF16), dpi.astype(BF16)
                dwa_ref[dr] += _dot_tn(ub, dprb)
                dwx_ref[dr] += _dot_tn(ub, dpib)
                dut = lm * sq * ig + _dot_nt(dprb, wa_d) + _dot_nt(dpib, wx_d)
                if first:
                    dub[pl.ds(PAD + t0, rt), :] = dut
                else:
                    dub[pl.ds(PAD + t0, rt), :] += dut
                return c

            lax.fori_loop(0, length // rt, tile, 0)

        def conv_bwd(dub, src_pad, out_ref, length):
            rt = _row_tile(length, ROW_TILE)

            def tile(i, c):
                t0 = pl.multiple_of(i * rt, rt)
                dut = dub[pl.ds(PAD + t0, rt), :]
                dcb_ref[...] += jnp.sum(dut, axis=0, keepdims=True)
                acc = jnp.zeros((rt, LANE), F32)
                for k in range(CONV_TAPS):
                    sh = CONV_LEFT - k
                    acc = acc + dub[pl.ds(PAD + t0 + sh, rt), :] * cw[k:k + 1, :]
                    dcw_ref[k:k + 1, :] += jnp.sum(dut * src_pad[pl.ds(PAD + t0 - sh, rt), :], axis=0, keepdims=True)
                out_ref[pl.ds(t0, rt), :] = acc.astype(out_ref.dtype)
                return c

            lax.fori_loop(0, length // rt, tile, 0)

        _conv_fwd(uc0_ref, cpad, cu, cw, cb, t_len)
        cbufs = ((ca0, ch0), (ca1, ch1))
        h0 = []
        for dr in range(2):
            ca, chh = cbufs[dr]
            _gates_fwd(cu, ca, chh, *params(dr), t_len)
            h0.append(_scan(ca, chh, chh, length=t_len, init=zero, reverse=(dr == 1), a_shift=0, store=True))
        _conv_fwd(u0_ref, upad, ubuf, cw, cb, s_len)
        rt = _row_tile(s_len, ROW_TILE)
        dh0 = []
        for dr in range(2):
            rev = dr == 1
            _gates_fwd(ubuf, abuf, hbuf, *params(dr), s_len)
            _scan(abuf, hbuf, hbuf, length=s_len, init=h0[dr], reverse=rev, a_shift=0, store=True)
            first_row = PAD + s_len if rev else PAD - 1
            hbuf[pl.ds(first_row, 1), :] = h0[dr]
            end_row = PAD - 1 if rev else PAD + s_len
            abuf[pl.ds(end_row, 1), :] = zero

            def copy(i, c):
                t0 = pl.multiple_of(i * rt, rt)
                lbuf[pl.ds(PAD + t0, rt), :] = dy_ref[pl.ds(t0, rt), :]
                return c

            lax.fori_loop(0, s_len // rt, copy, 0)
            _scan(abuf, lbuf, lbuf, length=s_len, init=zero, reverse=not rev, a_shift=(-1 if rev else 1), store=True)
            start = PAD + s_len - 1 if rev else PAD
            dh0.append(abuf[pl.ds(start, 1), :] * lbuf[pl.ds(start, 1), :])
            direction_bwd(dr, ubuf, abuf, hbuf, lbuf, dubuf, s_len, first=(dr == 0))
        dubuf[pl.ds(0, PAD), :] = zpad
        dubuf[pl.ds(PAD + s_len, PAD), :] = zpad
        conv_bwd(dubuf, upad, du_ref, s_len)
        lc = lbuf
        duc_buf = dubuf
        for dr in range(2):
            rev = dr == 1
            ca, chh = cbufs[dr]
            first_row = PAD + t_len if rev else PAD - 1
            chh[pl.ds(first_row, 1), :] = zero
            end_row = PAD - 1 if rev else PAD + t_len
            ca[pl.ds(end_row, 1), :] = zero + 1.0
            rtc = _row_tile(t_len, ROW_TILE)

            def clear(i, c):
                t0 = pl.multiple_of(i * rtc, rtc)
                lc[pl.ds(PAD + t0, rtc), :] = jnp.zeros((rtc, LANE), F32)
                return c

            lax.fori_loop(0, t_len // rtc, clear, 0)
            _scan(ca, lc, lc, length=t_len, init=dh0[dr], reverse=not rev, a_shift=(-1 if rev else 1), store=True)
            direction_bwd(dr, cu, ca, chh, lc, duc_buf, t_len, first=(dr == 0))
        duc_buf[pl.ds(0, PAD), :] = zpad
        duc_buf[pl.ds(PAD + t_len, PAD), :] = zpad
        conv_bwd(duc_buf, cpad, duc_ref, t_len)
        dlam_ref[...] = dlam_ref[...] * (1.0 - _sigmoid(lam_v))

    seq = pltpu.VMEM((s_len + 2 * PAD, LANE), F32)
    cseq = pltpu.VMEM((t_len + 2 * PAD, LANE), F32)
    vec2 = pl.BlockSpec((2, LANE), lambda n: (0, n))
    wspec = pl.BlockSpec((2, None, LANE, LANE), lambda n: (0, n, 0, 0))
    return _call(
        body, name="rglru_bwd", grid=(nb,),
        in_specs=[pl.BlockSpec((None, s_len, LANE), lambda n: (n, 0, 0)),
                  pl.BlockSpec((None, t_len, LANE), lambda n: (n, 0, 0)),
                  pl.BlockSpec((None, s_len, LANE), lambda n: (n, 0, 0))] + _lru_specs(),
        out_specs=[pl.BlockSpec((s_len, LANE), lambda n: (0, n)),
                   pl.BlockSpec((t_len, LANE), lambda n: (0, n)),
                   pl.BlockSpec((CONV_TAPS, LANE), lambda n: (0, n)),
                   pl.BlockSpec((1, LANE), lambda n: (0, n)),
                   wspec, wspec, vec2, vec2, vec2],
        out_shape=[jax.ShapeDtypeStruct((s_len, e), BF16), jax.ShapeDtypeStruct((t_len, e), BF16),
                   jax.ShapeDtypeStruct((CONV_TAPS, e), F32), jax.ShapeDtypeStruct((1, e), F32),
                   jax.ShapeDtypeStruct((2, nb, LANE, LANE), F32), jax.ShapeDtypeStruct((2, nb, LANE, LANE), F32),
                   jax.ShapeDtypeStruct((2, e), F32), jax.ShapeDtypeStruct((2, e), F32), jax.ShapeDtypeStruct((2, e), F32)],
        scratch_shapes=[seq, pltpu.VMEM((s_len, LANE), F32), seq, seq, seq, seq,
                        cseq, pltpu.VMEM((t_len, LANE), F32), cseq, cseq, cseq, cseq],
    )(ug, uc, dy, conv_w, conv_b, wa, wx, ba, bx, lam)


def _pool_windows(src_ref, out_ref, colbuf, rowbuf, half, transpose, s_len):
    gw = GRID_W
    lg = gw.bit_length() - 1
    n_rows = s_len // gw
    cm, rm = 16, 8 * gw
    rt = _row_tile(s_len, ROW_TILE)
    assert rt % gw == 0
    offs = range(-half, half)
    colbuf[pl.ds(0, cm), :] = jnp.zeros((cm, LANE), F32)
    colbuf[pl.ds(cm + s_len, cm), :] = jnp.zeros((cm, LANE), F32)

    def zrow(i, c):
        t0 = pl.multiple_of(i * gw, gw)
        rowbuf[pl.ds(t0, gw), :] = jnp.zeros((gw, LANE), F32)
        rowbuf[pl.ds(rm + s_len + t0, gw), :] = jnp.zeros((gw, LANE), F32)
        return c

    lax.fori_loop(0, rm // gw, zrow, 0)

    def counts(t0):
        tok = t0 + lax.broadcasted_iota(jnp.int32, (rt, LANE), 0)
        col = tok & (gw - 1)
        row = tok >> lg
        ccnt = (jnp.minimum(col + half, gw) - jnp.maximum(col - half, 0)).astype(F32)
        rcnt = (jnp.minimum(row + half, n_rows) - jnp.maximum(row - half, 0)).astype(F32)
        return col, ccnt, rcnt

    def col_sum(t0, col, sign):
        acc = jnp.zeros((rt, LANE), F32)
        for o in offs:
            so = sign * o
            ok = (col + so >= 0) & (col + so < gw)
            acc = acc + jnp.where(ok, colbuf[pl.ds(cm + t0 + so, rt), :], 0.0)
        return acc

    def row_sum(t0, sign):
        acc = jnp.zeros((rt, LANE), F32)
        for o in offs:
            acc = acc + rowbuf[pl.ds(rm + t0 + sign * o * gw, rt), :]
        return acc

    def loop(fn):
        def step(i, c):
            fn(pl.multiple_of(i * rt, rt))
            return c
        lax.fori_loop(0, s_len // rt, step, 0)

    if not transpose:
        def fill(t0):
            colbuf[pl.ds(cm + t0, rt), :] = src_ref[pl.ds(t0, rt), :]

        def cols(t0):
            col, ccnt, _ = counts(t0)
            rowbuf[pl.ds(rm + t0, rt), :] = col_sum(t0, col, 1) / ccnt

        def rows(t0):
            _, _, rcnt = counts(t0)
            out_ref[pl.ds(t0, rt), :] = (row_sum(t0, 1) / rcnt - src_ref[pl.ds(t0, rt), :]).astype(out_ref.dtype)

        loop(fill)
        loop(cols)
        loop(rows)
    else:
        def fill(t0):
            _, _, rcnt = counts(t0)
            rowbuf[pl.ds(rm + t0, rt), :] = src_ref[pl.ds(t0, rt), :] / rcnt

        def rows(t0):
            _, ccnt, _ = counts(t0)
            colbuf[pl.ds(cm + t0, rt), :] = row_sum(t0, -1) / ccnt

        def cols(t0):
            col, _, _ = counts(t0)
            out_ref[pl.ds(t0, rt), :] = (col_sum(t0, col, -1) - src_ref[pl.ds(t0, rt), :]).astype(out_ref.dtype)

        loop(fill)
        loop(rows)
        loop(cols)


def _pool_map(src, nb, transpose, out_chunk_major, name):
    s_len = src.shape[1]
    cpg = nb // len(POOL_WINDOWS)

    def body(src_ref, out_ref, colbuf, rowbuf):
        n = pl.program_id(0)
        for gi, w in enumerate(POOL_WINDOWS):
            @pl.when(n // cpg == gi)
            def _(w=w):
                _pool_windows(src_ref, out_ref, colbuf, rowbuf, w // 2, transpose, s_len)

    if out_chunk_major:
        out_spec = pl.BlockSpec((None, s_len, LANE), lambda n: (n, 0, 0))
        out_shape = jax.ShapeDtypeStruct((nb, s_len, LANE), BF16)
    else:
        out_spec = pl.BlockSpec((s_len, LANE), lambda n: (0, n))
        out_shape = jax.ShapeDtypeStruct((s_len, nb * LANE), BF16)
    return _call(
        body, name=name, grid=(nb,),
        in_specs=[pl.BlockSpec((None, s_len, LANE), lambda n: (n, 0, 0))],
        out_specs=out_spec, out_shape=out_shape,
        scratch_shapes=[pltpu.VMEM((s_len + 32, LANE), F32), pltpu.VMEM((s_len + 16 * GRID_W, LANE), F32)],
    )(src)


def _group_weight(w_ref):
    return jnp.concatenate([w_ref[k] for k in range(N_CHIPS)], axis=0)


def _pool_mm_fwd(dm, wp, scale):
    nb, rows, _ = dm.shape
    _, ng, pq, pg = wp.shape
    cpg = pg // LANE
    tm = _row_tile(rows, 512)

    def body(d_ref, w_ref, s_ref, y_ref):
        _put_chunks(y_ref, _dot(_cat(d_ref, cpg), _group_weight(w_ref)) * s_ref[...], cpg)

    cspec = pl.BlockSpec((cpg, tm, LANE), lambda i, g: (g, i, 0))
    return _call(
        body, name="pool_mm_fwd", grid=(rows // tm, ng),
        in_specs=[cspec, pl.BlockSpec((N_CHIPS, None, pq, pg), lambda i, g: (0, g, 0, 0)),
                  pl.BlockSpec((1, pg), lambda i, g: (0, g))],
        out_specs=cspec, out_shape=jax.ShapeDtypeStruct((nb, rows, LANE), F32),
    )(dm, wp, scale)


def _pool_mm_bwd(dy, dm, wp, scale, gbuf, row0):
    nb, rows, _ = dm.shape
    _, ng, pq, pg = wp.shape
    cpg = pg // LANE
    tm = _row_tile(rows, 512)
    nt = rows // tm
    assert gbuf.shape[2] == 2 * pg and row0 % pq == 0

    def body(dy_ref, d_ref, w_ref, s_ref, buf_ref, dd_ref, dwp_ref, dsc_ref, acc):
        i = pl.program_id(1)

        @pl.when(i == 0)
        def _():
            acc[...] = jnp.zeros_like(acc)
            dsc_ref[...] = jnp.zeros_like(dsc_ref)

        dyv = _cat(dy_ref, cpg)
        dc = _cat(d_ref, cpg)
        w = _group_weight(w_ref)
        dsc_ref[...] += jnp.sum(dyv * _dot(dc, w), axis=0, keepdims=True)
        dyp = (dyv * s_ref[...]).astype(BF16)
        _put_chunks(dd_ref, _dot_nt(dyp, w), cpg)
        acc[...] += _dot_tn(dc, dyp)

        @pl.when(i == nt - 1)
        def _():
            dwp_ref[...] = acc[...].reshape(N_CHIPS, pq, pg)

    cspec = pl.BlockSpec((cpg, tm, LANE), lambda g, i: (g, i, 0))
    sspec = pl.BlockSpec((1, pg), lambda g, i: (0, g))
    return _call(
        body, name="pool_mm_bwd", grid=(ng, nt),
        in_specs=[cspec, cspec, pl.BlockSpec((N_CHIPS, None, pq, pg), lambda g, i: (0, g, 0, 0)), sspec, ANY_SPEC],
        out_specs=[cspec, pl.BlockSpec((N_CHIPS, pq, pg), lambda g, i: (0, row0 // pq + g // 2, g % 2)), sspec],
        out_shape=[jax.ShapeDtypeStruct((nb, rows, LANE), F32), jax.ShapeDtypeStruct(gbuf.shape, F32),
                   jax.ShapeDtypeStruct((1, ng * pg), F32)],
        scratch_shapes=[pltpu.VMEM((pg, pg), F32)],
        input_output_aliases={4: 1},
    )(dy, dm, wp, scale, gbuf)


def _adamw_math(w, g, m, v):
    nm = ADAM_B1 * m + (1.0 - ADAM_B1) * g
    nv = ADAM_B2 * v + (1.0 - ADAM_B2) * jnp.square(g)
    m_hat = nm / (1.0 - ADAM_B1 ** ADAM_STEP)
    v_hat = nv / (1.0 - ADAM_B2 ** ADAM_STEP)
    return -ADAM_LR * (m_hat / (jnp.sqrt(v_hat) + ADAM_EPS) + ADAM_WD * w), nm, nv


def _adamw_param(w3, m3, v3, gsrcs, pick, tm, name):
    n_blk, rows, cols = w3.shape
    ng = len(gsrcs)

    def body(*refs):
        w_ref, m_ref, v_ref = refs[:3]
        g_refs = refs[3:3 + ng]
        go_ref, d_ref, nm_ref, nv_ref = refs[3 + ng:]
        g = pick(pl.program_id(0), [r[...] for r in g_refs])
        go_ref[...] = g
        d_ref[...], nm_ref[...], nv_ref[...] = _adamw_math(w_ref[...], g, m_ref[...], v_ref[...])

    spec = pl.BlockSpec((None, tm, cols), lambda n, i: (n, i, 0))
    return _call(
        body, name=name, grid=(n_blk, rows // tm),
        in_specs=[spec] * 3 + [pl.BlockSpec(shape, imap) for _, shape, imap in gsrcs],
        out_specs=[spec] * 4, out_shape=[jax.ShapeDtypeStruct(w3.shape, F32)] * 4,
    )(w3, m3, v3, *[a for a, _, _ in gsrcs])


def _adamw_flat(w, g, m, v):
    def body(w_ref, g_ref, m_ref, v_ref, d_ref, nm_ref, nv_ref):
        d_ref[...], nm_ref[...], nv_ref[...] = _adamw_math(w_ref[...], g_ref[...], m_ref[...], v_ref[...])

    spec = pl.BlockSpec(w.shape, lambda i: (0, 0))
    return _call(body, name="adamw_small", grid=(1,), in_specs=[spec] * 4, out_specs=[spec] * 3,
                 out_shape=[jax.ShapeDtypeStruct(w.shape, F32)] * 3)(w, g, m, v)


def _place():
    return lax.axis_index("x"), lax.axis_index("y"), lax.axis_index("c")


def _other_chips(x, y):
    return [(1 - x, y), (x, 1 - y), (1 - x, 1 - y)]


def _gather_chips(arrays, name, row0=0, rows=None):
    n = len(arrays)
    nrows = [a.shape[0] if rows is None else rows for a in arrays]
    halves = [r // 2 for r in nrows]
    for a, r, h in zip(arrays, nrows, halves):
        assert 2 * h == r and h % (32 // a.dtype.itemsize) == 0

    def body(*refs):
        ins, outs = refs[:n], refs[n:2 * n]
        send_sems, recv_sems, local_sems = refs[2 * n:]
        x, y, c = _place()
        me = 2 * x + y
        chips = _other_chips(x, y)

        def mine(k):
            return pl.ds(c * halves[k], halves[k])

        def theirs(k):
            return pl.ds((1 - c) * halves[k], halves[k])

        def push(k, j, src, dst, to):
            return pltpu.make_async_remote_copy(src_ref=src, dst_ref=dst, send_sem=send_sems.at[6 * k + j],
                                                recv_sem=recv_sems.at[6 * k + j], device_id=to, device_id_type=MESH)

        local = [pltpu.make_async_copy(ins[k].at[pl.ds(row0, nrows[k])], outs[k].at[me], local_sems.at[k]) for k in range(n)]
        for cp in local:
            cp.start()
        started = []
        for j, (cx, cy) in enumerate(chips):
            for k in range(n):
                cp = push(k, j, ins[k].at[pl.ds(row0 + c * halves[k], halves[k])], outs[k].at[me, mine(k)], (cx, cy, c))
                cp.start()
                started.append(cp)
        for j, (cx, cy) in enumerate(chips):
            for k in range(n):
                slab = outs[k].at[2 * cx + cy, mine(k)]
                push(k, j, slab, slab, (x, y, c)).wait_recv()
                fwd = push(k, 3 + j, slab, slab, (x, y, 1 - c))
                fwd.start()
                started.append(fwd)
        for j, (cx, cy) in enumerate(chips):
            for k in range(n):
                slab = outs[k].at[2 * cx + cy, theirs(k)]
                push(k, 3 + j, slab, slab, (x, y, c)).wait_recv()
        for cp in started:
            cp.wait_send()
        for cp in local:
            cp.wait()

    return _call(
        body, name=name, in_specs=[ANY_SPEC] * n, out_specs=[ANY_SPEC] * n,
        out_shape=[jax.ShapeDtypeStruct((N_CHIPS, r, a.shape[1]), a.dtype) for a, r in zip(arrays, nrows)],
        scratch_shapes=[pltpu.SemaphoreType.DMA((6 * n,)), pltpu.SemaphoreType.DMA((6 * n,)), pltpu.SemaphoreType.DMA((n,))],
    )(*arrays)


def _gather_devices(v):
    shape = v.shape

    def body(v_ref, out_ref, send_sems, recv_sems):
        x, y, c = _place()
        me = 4 * x + 2 * y + c
        out_ref[me] = v_ref[...]
        sends = []
        for k in range(1, N_DEV):
            to = (me + k) % N_DEV
            cp = pltpu.make_async_remote_copy(src_ref=v_ref, dst_ref=out_ref.at[me], send_sem=send_sems.at[k],
                                              recv_sem=recv_sems.at[k], device_id=(to // 4, (to // 2) % 2, to % 2),
                                              device_id_type=MESH)
            cp.start()
            sends.append(cp)
        for k in range(1, N_DEV):
            frm = (me + N_DEV - k) % N_DEV
            pltpu.make_async_remote_copy(src_ref=v_ref, dst_ref=out_ref.at[frm], send_sem=send_sems.at[k],
                                         recv_sem=recv_sems.at[k], device_id=(x, y, c), device_id_type=MESH).wait_recv()
        for cp in sends:
            cp.wait_send()

    vspec = pl.BlockSpec(memory_space=pltpu.VMEM)
    return _call(body, name="gather_devices", in_specs=[vspec], out_specs=vspec,
                 out_shape=jax.ShapeDtypeStruct((N_DEV,) + shape, F32),
                 scratch_shapes=[pltpu.SemaphoreType.DMA((N_DEV,)), pltpu.SemaphoreType.DMA((N_DEV,))])(v)


def _sibling_swap(g):
    _, rows, w = g.shape
    half = rows // 2

    def body(g_ref, out_ref, send_sem, recv_sem):
        x, y, c = _place()
        cp = pltpu.make_async_remote_copy(src_ref=g_ref.at[:, pl.ds((1 - c) * half, half)], dst_ref=out_ref,
                                          send_sem=send_sem, recv_sem=recv_sem, device_id=(x, y, 1 - c), device_id_type=MESH)
        cp.start()
        cp.wait()

    return _call(body, name="rs_sibling_swap", in_specs=[ANY_SPEC], out_specs=ANY_SPEC,
                 out_shape=jax.ShapeDtypeStruct((N_CHIPS, half, w), F32),
                 scratch_shapes=[pltpu.SemaphoreType.DMA, pltpu.SemaphoreType.DMA])(g)


def _pair_add(g, got, place):
    _, rows, w = g.shape
    half = rows // 2
    tm = _row_tile(half, RS_TILE)
    nt = half // tm

    def body(p_ref, a_ref, b_ref, o_ref, own_ref):
        v = a_ref[...] + b_ref[...]
        o_ref[...] = v.astype(BF16)

        @pl.when(pl.program_id(1) == p_ref[1])
        def _():
            own_ref[...] = v

    return _call(
        body, name="rs_pair_add",
        grid_spec=pltpu.PrefetchScalarGridSpec(
            num_scalar_prefetch=1, grid=(nt, N_CHIPS),
            in_specs=[pl.BlockSpec((None, tm, w), lambda i, s, p: (s, p[0] * nt + i, 0)),
                      pl.BlockSpec((None, tm, w), lambda i, s, p: (s, i, 0))],
            out_specs=[pl.BlockSpec((None, tm, w), lambda i, s, p: (s, i, 0)),
                       pl.BlockSpec((tm, w), lambda i, s, p: (i, 0))]),
        out_shape=[jax.ShapeDtypeStruct((N_CHIPS, half, w), BF16), jax.ShapeDtypeStruct((half, w), F32)],
    )(place, g, got)


def _chip_exchange(p):
    _, half, w = p.shape

    def body(p_ref, out_ref, send_sems, recv_sems, local_sem):
        x, y, c = _place()
        me = 2 * x + y
        chips = _other_chips(x, y)
        local = pltpu.make_async_copy(p_ref.at[me], out_ref.at[me], local_sem)
        local.start()
        sends = []
        for j, (cx, cy) in enumerate(chips):
            cp = pltpu.make_async_remote_copy(src_ref=p_ref.at[2 * cx + cy], dst_ref=out_ref.at[me], send_sem=send_sems.at[j],
                                              recv_sem=recv_sems.at[j], device_id=(cx, cy, c), device_id_type=MESH)
            cp.start()
            sends.append(cp)
        for j, (cx, cy) in enumerate(chips):
            slab = out_ref.at[2 * cx + cy]
            pltpu.make_async_remote_copy(src_ref=slab, dst_ref=slab, send_sem=send_sems.at[j], recv_sem=recv_sems.at[j],
                                         device_id=(x, y, c), device_id_type=MESH).wait_recv()
        for cp in sends:
            cp.wait_send()
        local.wait()

    return _call(body, name="rs_chip_exchange", in_specs=[ANY_SPEC], out_specs=ANY_SPEC,
                 out_shape=jax.ShapeDtypeStruct((N_CHIPS, half, w), p.dtype),
                 scratch_shapes=[pltpu.SemaphoreType.DMA((3,)), pltpu.SemaphoreType.DMA((3,)), pltpu.SemaphoreType.DMA])(p)


def _sum_chips(parts, own, place):
    _, half, w = parts.shape
    tm = _row_tile(half, RS_TILE)

    def body(p_ref, parts_ref, own_ref, o_ref):
        me = p_ref[1]
        t = [jnp.where(me == q, own_ref[...], parts_ref[q].astype(F32)) for q in range(N_CHIPS)]
        o_ref[...] = (t[0] + t[1]) + (t[2] + t[3])

    return _call(
        body, name="rs_sum_chips",
        grid_spec=pltpu.PrefetchScalarGridSpec(
            num_scalar_prefetch=1, grid=(half // tm,),
            in_specs=[pl.BlockSpec((N_CHIPS, tm, w), lambda i, p: (0, i, 0)), pl.BlockSpec((tm, w), lambda i, p: (i, 0))],
            out_specs=pl.BlockSpec((tm, w), lambda i, p: (i, 0))),
        out_shape=jax.ShapeDtypeStruct((half, w), F32),
    )(place, parts, own)


def _sibling_gather(hpart):
    half, w = hpart.shape

    def body(h_ref, out_ref, send_sem, recv_sem, local_sem):
        x, y, c = _place()
        local = pltpu.make_async_copy(h_ref, out_ref.at[pl.ds(c * half, half)], local_sem)
        local.start()
        cp = pltpu.make_async_remote_copy(src_ref=h_ref, dst_ref=out_ref.at[pl.ds(c * half, half)], send_sem=send_sem,
                                          recv_sem=recv_sem, device_id=(x, y, 1 - c), device_id_type=MESH)
        cp.start()
        other = out_ref.at[pl.ds((1 - c) * half, half)]
        pltpu.make_async_remote_copy(src_ref=other, dst_ref=other, send_sem=send_sem, recv_sem=recv_sem,
                                     device_id=(x, y, c), device_id_type=MESH).wait_recv()
        cp.wait_send()
        local.wait()

    return _call(body, name="rs_sibling_gather", in_specs=[ANY_SPEC], out_specs=ANY_SPEC,
                 out_shape=jax.ShapeDtypeStruct((2 * half, w), F32),
                 scratch_shapes=[pltpu.SemaphoreType.DMA, pltpu.SemaphoreType.DMA, pltpu.SemaphoreType.DMA])(hpart)


def _reduce_scatter(g, place):
    pair, own = _pair_add(g, _sibling_swap(g), place)
    return _sibling_gather(_sum_chips(_chip_exchange(pair), own, place))


WEIGHTS = ("c_ctx", "w_mod", "b_mod", "w_in", "w_out", "ln_g", "ln_b", "conv_w", "conv_b", "lru_wa", "lru_ba", "lru_wx",
           "lru_bx", "lru_lam", "pool_w", "pool_scale")
SMALL_GATHERED = ("conv_w", "lru_ba", "lru_bx", "lru_lam", "pool_scale")
SMALL_UPDATED = ("c_ctx", "b_mod", "ln_g", "ln_b", "conv_w", "conv_b", "lru_ba", "lru_bx", "lru_lam", "pool_scale")


def _size(shape):
    n = 1
    for s in shape:
        n *= int(s)
    return n


def _pack_rows(parts, rows):
    flat = jnp.concatenate([p.reshape(-1) for p in parts])
    return jnp.pad(flat, (0, rows * LANE - flat.shape[0])).reshape(rows, LANE)


def kernel(x, c, ctx, c_ctx, w_mod, b_mod, w_in, w_out, ln_g, ln_b, conv_w, conv_b, lru_wa, lru_ba, lru_wx, lru_bx, lru_lam, pool_w, pool_scale, loss_target, m_c_ctx, m_w_mod, m_b_mod, m_w_in, m_w_out, m_ln_g, m_ln_b, m_conv_w, m_conv_b, m_lru_wa, m_lru_ba, m_lru_wx, m_lru_bx, m_lru_lam, m_pool_w, m_pool_scale, v_c_ctx, v_w_mod, v_b_mod, v_w_in, v_w_out, v_ln_g, v_ln_b, v_conv_w, v_conv_b, v_lru_wa, v_lru_ba, v_lru_wx, v_lru_bx, v_lru_lam, v_pool_w, v_pool_scale):
    weights = dict(c_ctx=c_ctx, w_mod=w_mod, b_mod=b_mod, w_in=w_in, w_out=w_out, ln_g=ln_g, ln_b=ln_b, conv_w=conv_w,
                   conv_b=conv_b, lru_wa=lru_wa, lru_ba=lru_ba, lru_wx=lru_wx, lru_bx=lru_bx, lru_lam=lru_lam,
                   pool_w=pool_w, pool_scale=pool_scale)
    mom1 = dict(c_ctx=m_c_ctx, w_mod=m_w_mod, b_mod=m_b_mod, w_in=m_w_in, w_out=m_w_out, ln_g=m_ln_g, ln_b=m_ln_b,
                conv_w=m_conv_w, conv_b=m_conv_b, lru_wa=m_lru_wa, lru_ba=m_lru_ba, lru_wx=m_lru_wx, lru_bx=m_lru_bx,
                lru_lam=m_lru_lam, pool_w=m_pool_w, pool_scale=m_pool_scale)
    mom2 = dict(c_ctx=v_c_ctx, w_mod=v_w_mod, b_mod=v_b_mod, w_in=v_w_in, w_out=v_w_out, ln_g=v_ln_g, ln_b=v_ln_b,
                conv_w=v_conv_w, conv_b=v_conv_b, lru_wa=v_lru_wa, lru_ba=v_lru_ba, lru_wx=v_lru_wx, lru_bx=v_lru_bx,
                lru_lam=v_lru_lam, pool_w=v_pool_w, pool_scale=v_pool_scale)
    xs, cx, target = x[0], ctx[0], loss_target[0]
    s_len, d = xs.shape
    es = w_out.shape[1]
    e = es * N_CHIPS
    nb = e // LANE
    c3 = w_mod.shape[2]
    n4 = w_in.shape[2]
    pq, pg = pool_w.shape[2], pool_w.shape[3]
    ng = len(POOL_WINDOWS)
    width = n4
    assert width == d and 2 * pg == width and 2 * nb * LANE == N_CHIPS * width and d % (2 * N_CHIPS) == 0
    px, py, pc = _place()
    place = jnp.stack([pc, 2 * px + py]).astype(jnp.int32)
    cctx2 = c_ctx[None, :]

    n_small = sum(_size(weights[n].shape) for n in SMALL_GATHERED)
    small_rows = (n_small + 16 * LANE - 1) // (16 * LANE) * 16
    small = _pack_rows([weights[n] for n in SMALL_GATHERED], small_rows)
    wm_g, win0, wout0_g, small_g = _gather_chips(
        [w_mod.astype(BF16).reshape(DEPTH * d, c3), w_in[0].astype(BF16), w_out[0].astype(BF16), small], "gather_weights0")
    win1, wout1_g, wp_g = _gather_chips(
        [w_in[1].astype(BF16), w_out[1].astype(BF16), pool_w.astype(BF16).reshape(ng * pq, pg)], "gather_weights1")
    wm_all = wm_g.reshape(N_CHIPS, DEPTH, d, c3)
    win = [win0, win1]
    wout = [wout0_g.reshape(e, d), wout1_g.reshape(e, d)]
    wp = wp_g.reshape(N_CHIPS, ng, pq, pg)
    sg = small_g.reshape(N_CHIPS, -1)
    full, off = {}, 0
    for n in SMALL_GATHERED:
        shp = weights[n].shape
        k = _size(shp)
        lead = k // shp[-1]
        full[n] = jnp.moveaxis(sg[:, off:off + k].reshape(N_CHIPS, lead, shp[-1]), 0, 1).reshape(lead, N_CHIPS * shp[-1])
        off += k
    wa_b, wx_b = lru_wa[0].astype(BF16), lru_wx[0].astype(BF16)
    lru_args = (full["conv_w"], conv_b, wa_b, wx_b, full["lru_ba"], full["lru_bx"], full["lru_lam"])
    scale_f = full["pool_scale"]

    cc = jnp.concatenate([c, cctx2, jnp.zeros((6, d), F32)], axis=0)
    mod = _mod_fwd(cc, wm_all, b_mod[:, None, :])

    def mod_parts(l, row):
        v = mod[l, row]
        return v[None, :d], 1.0 + v[None, d:2 * d], v[None, 2 * d:]

    sh0, sc0, gt0 = mod_parts(0, 0)
    shc, scc, _ = mod_parts(0, 1)
    sh1, sc1, gt1 = mod_parts(1, 0)
    lg = [ln_g[l][None, :] for l in range(DEPTH)]
    lb = [ln_b[l][None, :] for l in range(DEPTH)]

    ug0 = _inproj_fwd(xs, sc0, sh0, win[0], "inproj_fwd0")
    uc0 = _inproj_fwd(cx, scc, shc, win[0][:2], "inproj_fwd_ctx")
    y0 = _rglru_fwd(ug0, uc0, *lru_args)
    br0, x1 = _outproj_fwd(y0, ug0, xs, gt0, wout[0], lg[0], lb[0], None, "outproj_fwd0")
    ug1 = _inproj_fwd(x1, sc1, sh1, win[1], "inproj_fwd1")
    d1 = _pool_map(ug1, nb, False, True, "pool_fwd")
    y1 = _pool_mm_fwd(d1, wp, scale_f)
    br1, dxo, loss_part = _outproj_fwd(y1, ug1, x1, gt1, wout[1], lg[1], lb[1], target, "outproj_fwd1")
    loss = lax.psum(loss_part[0, 0] * (0.5 / d), ("x", "y", "c"))

    row_wout = d
    row_tail = d + es
    wq = 2 * (nb // N_CHIPS) * LANE * LANE // width
    whole = lambda r: (r + 2 * RS_TILE - 1) // (2 * RS_TILE) * (2 * RS_TILE)
    rows1 = whole(row_tail + pg // 2)
    rows0 = whole(row_tail + 2 * wq)
    gbuf1 = jnp.zeros((N_CHIPS, rows1, width), F32)
    gbuf0 = jnp.zeros((N_CHIPS, rows0, width), F32)

    dy1, dg1, dxres1, dbr1, dlg1, dlb1, dgt1 = _outproj_bwd(dxo, x1, br1, y1, ug1, gt1, lg[1], wout[1], "outproj_bwd1")
    gbuf1 = _outproj_bwd_w(y1, ug1, dbr1, gbuf1, row_wout, "outproj_bwd_w1")
    dd1, gbuf1, dscale = _pool_mm_bwd(dy1, d1, wp, scale_f, gbuf1, row_tail)
    du1 = _pool_map(dd1, nb, True, False, "pool_bwd")
    dx1, dsc1, dsh1 = _inproj_bwd_x([du1, dg1], x1, dxres1, sc1, win[1], "inproj_bwd_x1")
    gbuf1 = _inproj_bwd_w(x1, sc1, sh1, [du1, dg1], None, gbuf1, "inproj_bwd_w1")
    red1 = _reduce_scatter(gbuf1, place)

    dy0, dg0, dxres0, dbr0, dlg0, dlb0, dgt0 = _outproj_bwd(dx1, xs, br0, y0, ug0, gt0, lg[0], wout[0], "outproj_bwd0")
    gbuf0 = _outproj_bwd_w(y0, ug0, dbr0, gbuf0, row_wout, "outproj_bwd_w0")
    du0, duc, dconv_w, dconv_b, dwa, dwx, dba, dbx, dlam = _rglru_bwd(ug0, uc0, dy0, *lru_args)
    grad_x, dsc0, dsh0 = _inproj_bwd_x([du0, dg0], xs, dxres0, sc0, win[0], "inproj_bwd_x0")
    dscc, dshc = _inproj_bwd_x([duc], cx, None, scc, win[0][:2], "inproj_bwd_x_ctx")
    dwin0c = _inproj_bwd_w(cx, scc, shc, [duc, jnp.zeros_like(duc)], None, None, "inproj_bwd_w_ctx")
    gbuf0 = _inproj_bwd_w(xs, sc0, sh0, [du0, dg0], dwin0c, gbuf0, "inproj_bwd_w0")

    def quarter(dw):
        t = dw.reshape(2, N_CHIPS, nb // N_CHIPS, LANE, LANE)
        return jnp.transpose(t, (1, 3, 0, 2, 4)).reshape(N_CHIPS, LANE, 2 * (nb // N_CHIPS) * LANE).reshape(N_CHIPS, wq, width)

    tail0 = jnp.concatenate([quarter(dwa), quarter(dwx)], axis=1)
    gbuf0 = lax.dynamic_update_slice(gbuf0, tail0, (0, row_tail, 0))
    red0 = _reduce_scatter(gbuf0, place)
    (rep,) = _gather_chips([red0], "gather_replicated", row0=row_tail, rows=2 * wq)

    zd = jnp.zeros((1, d), F32)
    small_names = ("conv_b", "ln_g", "ln_b", "pool_scale", "conv_w", "lru_ba", "lru_bx", "lru_lam")
    small_local = dict(conv_b=dconv_b, ln_g=jnp.concatenate([dlg0, dlg1]), ln_b=jnp.concatenate([dlb0, dlb1]),
                       pool_scale=dscale, conv_w=dconv_w, lru_ba=dba, lru_bx=dbx, lru_lam=dlam)
    small_flat = jnp.concatenate([small_local[n].reshape(-1) for n in small_names])
    n_vec = VEC_KINDS + (small_flat.shape[0] + 3 * d - 1) // (3 * d)
    n_vec = (n_vec + SUB - 1) // SUB * SUB
    vec = jnp.concatenate([
        jnp.concatenate([c, zd, zd], axis=1), jnp.concatenate([dsh0, dsc0, dgt0], axis=1),
        jnp.concatenate([dshc, dscc, zd], axis=1), jnp.concatenate([dsh1, dsc1, dgt1], axis=1),
        jnp.pad(small_flat, (0, (n_vec - VEC_KINDS) * 3 * d - small_flat.shape[0])).reshape(n_vec - VEC_KINDS, 3 * d)],
        axis=0)
    gt_all = jnp.swapaxes(_gather_devices(vec), 0, 1)
    g_wmod = _mod_bwd_shard(gt_all, cctx2, place, c3)
    g_bmod, g_cctx, small_sum = _mod_bwd_rep(gt_all, cctx2, wm_all)

    tmw = _row_tile(d, 256)
    red_src = lambda red, r0, tm: (red, (tm, width), lambda n, i: (r0 // tm + i, 0))
    by_layer = lambda n, gs: jnp.where(n == 0, gs[0], gs[1])
    outs = {}
    outs["w_in"] = _adamw_param(w_in, m_w_in, v_w_in, [red_src(red0, 0, tmw), red_src(red1, 0, tmw)], by_layer, tmw, "adamw_w_in")
    outs["w_out"] = _adamw_param(w_out, m_w_out, v_w_out, [red_src(red0, row_wout, tmw), red_src(red1, row_wout, tmw)],
                                 by_layer, tmw, "adamw_w_out")
    outs["w_mod"] = _adamw_param(w_mod, m_w_mod, v_w_mod, [(g_wmod, (None, tmw, c3), lambda n, i: (n, i, 0))],
                                 lambda n, gs: gs[0], tmw, "adamw_w_mod")
    pw = [a.reshape(ng, pq, pg) for a in (pool_w, m_pool_w, v_pool_w)]
    outs["pool_w"] = [o.reshape(pool_w.shape) for o in _adamw_param(
        *pw, [(red1, (pq, pg), lambda n, i: (row_tail // pq + n // 2, n % 2))], lambda n, gs: gs[0], pq, "adamw_pool_w")]
    bq = nb // N_CHIPS
    rep_src = lambda r0: (rep, (None, LANE, LANE), lambda n, i: ((n % nb) // bq, r0 // LANE, (n // nb) * bq + n % bq))
    for name, r0, trio in (("lru_wa", 0, (lru_wa, m_lru_wa, v_lru_wa)), ("lru_wx", wq, (lru_wx, m_lru_wx, v_lru_wx))):
        blocks = [a.reshape(2 * nb, LANE, LANE) for a in trio]
        outs[name] = [o.reshape(lru_wa.shape) for o in _adamw_param(*blocks, [rep_src(r0)], lambda n, gs: gs[0], LANE,
                                                                    "adamw_" + name)]

    eq = e // N_CHIPS
    sums = small_sum.reshape(-1)
    g_small = {"c_ctx": g_cctx[0], "b_mod": g_bmod}
    off = 0
    for n in small_names:
        full_g = sums[off:off + small_local[n].size].reshape(small_local[n].shape)
        off += small_local[n].size
        if n in SMALL_GATHERED:
            full_g = lax.dynamic_slice_in_dim(full_g, place[1] * eq, eq, axis=full_g.ndim - 1)
        g_small[n] = full_g
    n_upd = sum(_size(weights[n].shape) for n in SMALL_UPDATED)
    upd_rows = (n_upd + SUB * LANE - 1) // (SUB * LANE) * SUB
    packed = [_pack_rows([src[n] for n in SMALL_UPDATED], upd_rows) for src in (weights, g_small, mom1, mom2)]
    small_out = _adamw_flat(*packed)
    off = 0
    for n in SMALL_UPDATED:
        k = _size(weights[n].shape)
        outs[n] = [g_small[n].reshape(weights[n].shape)] + [a.reshape(-1)[off:off + k].reshape(weights[n].shape) for a in small_out]
        off += k

    result = [loss, grad_x[None]]
    for j in range(4):
        result += [outs[n][j] for n in WEIGHTS]
    return tuple(result)
```

```python
import jax
import jax.numpy as jnp
from jax import lax
from jax.experimental import pallas as pl
from jax.experimental.pallas import tpu as pltpu

F32 = jnp.float32
BF16 = jnp.bfloat16
LANE = 128
SUB = 8
GRID_W = 64
POOL_WINDOWS = (2, 4, 8, 16)
LRU_C = 8.0
DEPTH = 2
ALPHA = float((2 * DEPTH) ** 0.25)
LN_EPS = 1e-5
ADAM_LR, ADAM_B1, ADAM_B2, ADAM_EPS, ADAM_WD, ADAM_STEP = 0.001, 0.9, 0.999, 1e-08, 0.01, 10
N_CHIPS = 4
N_DEV = 8
MESH = pl.DeviceIdType.MESH
ROW_TILE = 512
GATE_TILE = 256
GATE_UNROLL = 2
CONV_TAPS = 4
CONV_LEFT = 2
PAD = 8
SCAN_UNROLL = 8
RS_TILE = 128
VEC_KINDS = 4


def _call(body, **kw):
    return pl.pallas_call(body, **kw)


def _dot(a, b):
    return jnp.dot(a, b, preferred_element_type=F32)


def _dot_nt(a, b):
    return lax.dot_general(a, b, (((1,), (1,)), ((), ())), preferred_element_type=F32)


def _dot_tn(a, b):
    return lax.dot_general(a, b, (((0,), (0,)), ((), ())), preferred_element_type=F32)


def _sigmoid(v):
    return 0.5 * (jnp.tanh(0.5 * v) + 1.0)


def _silu(v):
    return v * _sigmoid(v)


def _dsilu(v):
    s = _sigmoid(v)
    return s * (1.0 + v * (1.0 - s))


def _log_sigmoid(v):
    z = jnp.exp(-jnp.abs(v))
    return jnp.minimum(v, 0.0) - jnp.where(z < 1e-4, z * (1.0 - 0.5 * z), jnp.log(1.0 + z))


def _one_minus_sq(la, a):
    return -jnp.tanh(la) * (a * a + 1.0)


def _cat(ref, n):
    return jnp.concatenate([ref[k] for k in range(n)], axis=1)


def _put_chunks(ref, val, n, base=0):
    for k in range(n):
        ref[base + k] = val[:, k * LANE:(k + 1) * LANE].astype(ref.dtype)


def _row_tile(rows, want):
    t = min(rows, want)
    assert rows % t == 0
    return t


ANY_SPEC = pl.BlockSpec(memory_space=pl.ANY)


def _mod_fwd(cvec, cctx, wm, bm):
    ns, nl, d, c3 = wm.shape

    def body(c_ref, cx_ref, w_ref, b_ref, o_ref):
        cc = jnp.concatenate([c_ref[...], cx_ref[...], jnp.zeros((SUB - 2, d), F32)], axis=0)
        o_ref[...] = _dot(_silu(cc).astype(BF16), w_ref[...]) + b_ref[...]

    return _call(
        body, name="mod_fwd", grid=(nl, ns),
        in_specs=[pl.BlockSpec((1, d), lambda l, s: (0, 0)),
                  pl.BlockSpec((1, d), lambda l, s: (0, 0)),
                  pl.BlockSpec((None, None, d, c3), lambda l, s: (s, l, 0, 0)),
                  pl.BlockSpec((None, 1, c3), lambda l, s: (l, 0, s))],
        out_specs=pl.BlockSpec((None, 8, c3), lambda l, s: (l, 0, s)),
        out_shape=jax.ShapeDtypeStruct((nl, 8, ns * c3), F32),
    )(cvec, cctx, wm, bm)


def _rows_kernel(parts, rows, cols, name):
    def body(*refs):
        o_ref = refs[-1]
        o_ref[...] = jnp.zeros_like(o_ref)
        for ref, (a, r0, c0) in zip(refs[:-1], parts):
            for k in range(a.shape[0]):
                o_ref[r0 + k:r0 + k + 1, c0:c0 + a.shape[1]] = ref[k:k + 1, :]

    return _call(body, name=name, grid=(1,),
                 in_specs=[pl.BlockSpec(a.shape, lambda i: (0, 0)) for a, _, _ in parts],
                 out_specs=pl.BlockSpec((rows, cols), lambda i: (0, 0)),
                 out_shape=jax.ShapeDtypeStruct((rows, cols), F32))(*[a for a, _, _ in parts])


def _mod_bwd_shard(gt, cctx, place, c3):
    d = cctx.shape[1]

    def body(p_ref, cs_ref, dm_ref, dmx_ref, cx_ref, o_ref):
        l = pl.program_id(0)
        lhs = jnp.concatenate([_silu(cs_ref[...]), _silu(cx_ref[...]), jnp.zeros((7, d), F32)], axis=0).astype(BF16)
        dmx = jnp.where(l == 0, jnp.sum(dmx_ref[...], axis=0, keepdims=True), 0.0)
        rhs = jnp.concatenate([dm_ref[...], dmx, jnp.zeros((7, c3), F32)], axis=0).astype(BF16)
        o_ref[...] = _dot_tn(lhs, rhs)

    return _call(
        body, name="mod_bwd_shard",
        grid_spec=pltpu.PrefetchScalarGridSpec(
            num_scalar_prefetch=1, grid=(DEPTH,),
            in_specs=[pl.BlockSpec((None, N_DEV, d), lambda l, p: (0, 0, 0)),
                      pl.BlockSpec((None, N_DEV, c3), lambda l, p: (1 + 2 * l, 0, p[1])),
                      pl.BlockSpec((None, N_DEV, c3), lambda l, p: (2, 0, p[1])),
                      pl.BlockSpec((1, d), lambda l, p: (0, 0))],
            out_specs=pl.BlockSpec((None, d, c3), lambda l, p: (l, 0, 0))),
        out_shape=jax.ShapeDtypeStruct((DEPTH, d, c3), F32),
    )(place, gt, gt, gt, cctx)


def _small_layout(d, e):
    k = VEC_KINDS
    return {
        "conv_b": ((1, e), [(0, k, 0)]),
        "ln_g": ((2, d), [(0, k, e), (1, k + 1, e)]),
        "pool_scale": ((1, e), [(0, k + 1, 0)]),
        "ln_b": ((2, d), [(0, k + 2, 0), (1, k + 2, d)]),
        "conv_w": ((CONV_TAPS, e), [(t, k + 3 + t, 0) for t in range(CONV_TAPS)]),
        "lru_ba": ((2, e), [(j, k + 7 + j, 0) for j in range(2)]),
        "lru_bx": ((2, e), [(j, k + 9 + j, 0) for j in range(2)]),
        "lru_lam": ((2, e), [(j, k + 11 + j, 0) for j in range(2)]),
    }


VEC_ROWS = 24


def _mod_bwd_rep(gt, cctx, wm):
    ns, _, d, c3 = wm.shape
    layout = _small_layout(d, ns * c3 - d)
    names = list(layout)

    def body(g_ref, cx_ref, w_ref, db_ref, dc_ref, *small_refs):
        dm0 = jnp.sum(g_ref[1], axis=0, keepdims=True)
        dmx = jnp.sum(g_ref[2], axis=0, keepdims=True)
        dm1 = jnp.sum(g_ref[3], axis=0, keepdims=True)
        db_ref[0:1, :] = dm0 + dmx
        db_ref[1:2, :] = dm1
        dmxb = jnp.broadcast_to(dmx, (SUB, ns * c3)).astype(BF16)
        acc = jnp.zeros((SUB, d), F32)
        for s in range(ns):
            acc = acc + _dot_nt(dmxb[:, s * c3:(s + 1) * c3], w_ref[s])
        dc_ref[...] = acc[0:1, :] * _dsilu(cx_ref[...])
        for ref, name in zip(small_refs, names):
            shape, places = layout[name]
            for arr_row, vec_row, col0 in places:
                total = jnp.sum(g_ref[vec_row], axis=0, keepdims=True)
                ref[arr_row:arr_row + 1, :] = total[:, col0:col0 + shape[1]]

    outs = _call(
        body, name="mod_bwd_rep", grid=(1,),
        in_specs=[pl.BlockSpec(gt.shape, lambda i: (0, 0, 0)),
                  pl.BlockSpec((1, d), lambda i: (0, 0)),
                  pl.BlockSpec((ns, None, d, c3), lambda i: (0, 0, 0, 0))],
        out_specs=[pl.BlockSpec((DEPTH, ns * c3), lambda i: (0, 0)), pl.BlockSpec((1, d), lambda i: (0, 0))]
        + [pl.BlockSpec(layout[n][0], lambda i: (0, 0)) for n in names],
        out_shape=[jax.ShapeDtypeStruct((DEPTH, ns * c3), F32), jax.ShapeDtypeStruct((1, d), F32)]
        + [jax.ShapeDtypeStruct(layout[n][0], F32) for n in names],
    )(gt, cctx, wm)
    return outs[0], outs[1], dict(zip(names, outs[2:]))


def _inproj_fwd(xin, sc1, sh, w, name):
    rows, d = xin.shape
    ns, _, n4 = w.shape
    cpb = n4 // LANE
    tm = _row_tile(rows, 256)

    def body(x_ref, sc_ref, sh_ref, w_ref, o_ref):
        h = (x_ref[...] * sc_ref[...] + sh_ref[...]).astype(BF16)
        for s in range(ns):
            _put_chunks(o_ref, _dot(h, w_ref[s]), cpb, base=s * cpb)

    return _call(
        body, name=name, grid=(rows // tm,),
        in_specs=[pl.BlockSpec((tm, d), lambda i: (i, 0)),
                  pl.BlockSpec((1, d), lambda i: (0, 0)),
                  pl.BlockSpec((1, d), lambda i: (0, 0)),
                  pl.BlockSpec((ns, d, n4), lambda i: (0, 0, 0))],
        out_specs=pl.BlockSpec((ns * cpb, tm, LANE), lambda i: (0, i, 0)),
        out_shape=jax.ShapeDtypeStruct((ns * cpb, rows, LANE), F32),
    )(xin, sc1, sh, w)


def _inproj_bwd_x(dparts, xin, dxres, sc1, w, name):
    rows, d = xin.shape
    npart = len(dparts)
    e = dparts[0].shape[1]
    ns, _, n4 = w.shape
    per = e // n4
    assert per * npart == ns
    tm = _row_tile(rows, 256)
    has_res = dxres is not None

    def body(*refs):
        dp = refs[:npart]
        x_ref, sc_ref, w_ref = refs[npart:npart + 3]
        rest = refs[npart + 3:]
        if has_res:
            res_ref, dx_ref, dsc_ref, dsh_ref = rest
        else:
            dsc_ref, dsh_ref = rest
        i = pl.program_id(0)
        dh = jnp.zeros((tm, d), F32)
        for p in range(npart):
            v = dp[p][...]
            for q in range(per):
                dh = dh + _dot_nt(v[:, q * n4:(q + 1) * n4], w_ref[p * per + q])

        @pl.when(i == 0)
        def _():
            dsc_ref[...] = jnp.zeros_like(dsc_ref)
            dsh_ref[...] = jnp.zeros_like(dsh_ref)

        dsc_ref[...] += jnp.sum(dh * x_ref[...], axis=0, keepdims=True)
        dsh_ref[...] += jnp.sum(dh, axis=0, keepdims=True)
        if has_res:
            dx_ref[...] = res_ref[...] + dh * sc_ref[...]

    row_spec = pl.BlockSpec((tm, d), lambda i: (i, 0))
    vec_spec = pl.BlockSpec((1, d), lambda i: (0, 0))
    in_specs = [pl.BlockSpec((tm, e), lambda i: (i, 0))] * npart + [row_spec, vec_spec,
                                                                     pl.BlockSpec((ns, d, n4), lambda i: (0, 0, 0))]
    args = list(dparts) + [xin, sc1, w]
    out_specs, out_shape = [vec_spec, vec_spec], [jax.ShapeDtypeStruct((1, d), F32)] * 2
    if has_res:
        in_specs.append(row_spec)
        args.append(dxres)
        out_specs = [row_spec] + out_specs
        out_shape = [jax.ShapeDtypeStruct((rows, d), F32)] + out_shape
    return _call(body, name=name, grid=(rows // tm,), in_specs=in_specs, out_specs=out_specs, out_shape=out_shape)(*args)


def _inproj_bwd_w(xin, sc1, sh, dparts, init, gbuf, name):
    rows, d = xin.shape
    npart = len(dparts)
    e = dparts[0].shape[1]
    n4 = e // 2
    ns = 2 * npart
    tm = _row_tile(rows, 512)
    nt = rows // tm
    has_init = init is not None
    into = gbuf is not None
    assert not into or (ns == N_CHIPS and gbuf.shape[2] == n4)

    def body(*refs):
        x_ref, sc_ref, sh_ref = refs[:3]
        dp = refs[3:3 + npart]
        init_ref = refs[3 + npart] if has_init else None
        o_ref = refs[-1]
        s, i = pl.program_id(0), pl.program_id(1)
        h = (x_ref[...] * sc_ref[...] + sh_ref[...]).astype(BF16)

        @pl.when(i == 0)
        def _():
            o_ref[...] = init_ref[...] if has_init else jnp.zeros_like(o_ref)

        for p in range(npart):
            @pl.when(s // 2 == p)
            def _(p=p):
                o_ref[...] += _dot_tn(h, dp[p][...])

    in_specs = [pl.BlockSpec((tm, d), lambda s, i: (i, 0)),
                pl.BlockSpec((1, d), lambda s, i: (0, 0)),
                pl.BlockSpec((1, d), lambda s, i: (0, 0))]
    in_specs += [pl.BlockSpec((tm, n4), lambda s, i: (i, s % 2))] * npart
    args = [xin, sc1, sh] + list(dparts)
    o_spec = pl.BlockSpec((None, d, n4), lambda s, i: (s, 0, 0))
    if has_init:
        in_specs.append(o_spec)
        args.append(init)
    extra = {}
    if into:
        in_specs.append(ANY_SPEC)
        args.append(gbuf)
        extra = dict(input_output_aliases={len(args) - 1: 0})
    out_shape = jax.ShapeDtypeStruct(gbuf.shape if into else (ns, d, n4), F32)
    return _call(body, name=name, grid=(ns, nt), in_specs=in_specs, out_specs=o_spec, out_shape=out_shape, **extra)(*args)


def _ln_stats(r):
    mu = jnp.mean(r, axis=-1, keepdims=True)
    var = jnp.mean(jnp.square(r - mu), axis=-1, keepdims=True)
    rstd = lax.rsqrt(var + LN_EPS)
    return (r - mu) * rstd, rstd


def _outproj_fwd(y, ug, xin, gt, wout, lg, lb, target, name):
    nch, rows, _ = y.shape
    e, d = wout.shape
    tm = _row_tile(rows, 256)
    with_loss = target is not None

    def body(*refs):
        y_ref, g_ref, x_ref, gt_ref, w_ref, lg_ref, lb_ref = refs[:7]
        if with_loss:
            t_ref, br_ref, dxo_ref, loss_ref = refs[7:]
        else:
            br_ref, xo_ref = refs[7:]
        z = jnp.concatenate([(y_ref[k] * _silu(g_ref[k])).astype(BF16) for k in range(nch)], axis=1)
        br = _dot(z, w_ref[...])
        br_ref[...] = br
        xhat, _ = _ln_stats(ALPHA * x_ref[...] + gt_ref[...] * br)
        xo = xhat * lg_ref[...] + lb_ref[...]
        if with_loss:
            err = xo - t_ref[...]
            dxo_ref[...] = err * (1.0 / d)

            @pl.when(pl.program_id(0) == 0)
            def _():
                loss_ref[...] = jnp.zeros_like(loss_ref)

            loss_ref[...] += jnp.sum(err * err)
        else:
            xo_ref[...] = xo

    chunk_spec = pl.BlockSpec((nch, tm, LANE), lambda i: (0, i, 0))
    g_spec = pl.BlockSpec((nch, tm, LANE), lambda i: (1, i, 0))
    row_spec = pl.BlockSpec((tm, d), lambda i: (i, 0))
    vec_spec = pl.BlockSpec((1, d), lambda i: (0, 0))
    in_specs = [chunk_spec, g_spec, row_spec, vec_spec, pl.BlockSpec((e, d), lambda i: (0, 0)), vec_spec, vec_spec]
    args = [y, ug, xin, gt, wout, lg, lb]
    out_specs = [row_spec, row_spec]
    out_shape = [jax.ShapeDtypeStruct((rows, d), F32)] * 2
    if with_loss:
        in_specs.append(row_spec)
        args.append(target)
        out_specs.append(pl.BlockSpec((1, LANE), lambda i: (0, 0)))
        out_shape.append(jax.ShapeDtypeStruct((1, LANE), F32))
    return _call(body, name=name, grid=(rows // tm,), in_specs=in_specs, out_specs=out_specs, out_shape=out_shape)(*args)


def _outproj_bwd(dxo, xin, br, y, ug, gt, lg, wout, name):
    nch, rows, _ = y.shape
    e, d = wout.shape
    tm = _row_tile(rows, 256)

    def body(dxo_ref, x_ref, br_ref, y_ref, g_ref, gt_ref, lg_ref, w_ref,
             dy_ref, dg_ref, dxres_ref, dbr_ref, dlg_ref, dlb_ref, dgt_ref):
        dxo_v = dxo_ref[...]
        brv = br_ref[...]
        xhat, rstd = _ln_stats(ALPHA * x_ref[...] + gt_ref[...] * brv)
        dxh = dxo_v * lg_ref[...]
        dr = rstd * (dxh - jnp.mean(dxh, axis=-1, keepdims=True) - xhat * jnp.mean(dxh * xhat, axis=-1, keepdims=True))

        @pl.when(pl.program_id(0) == 0)
        def _():
            dlg_ref[...] = jnp.zeros_like(dlg_ref)
            dlb_ref[...] = jnp.zeros_like(dlb_ref)
            dgt_ref[...] = jnp.zeros_like(dgt_ref)

        dlg_ref[...] += jnp.sum(dxo_v * xhat, axis=0, keepdims=True)
        dlb_ref[...] += jnp.sum(dxo_v, axis=0, keepdims=True)
        dgt_ref[...] += jnp.sum(dr * brv, axis=0, keepdims=True)
        dxres_ref[...] = ALPHA * dr
        dbr = (gt_ref[...] * dr).astype(BF16)
        dbr_ref[...] = dbr
        dz = _dot_nt(dbr, w_ref[...])
        for k in range(nch):
            dzk = dz[:, k * LANE:(k + 1) * LANE]
            gk = g_ref[k]
            dy_ref[k] = dzk * _silu(gk)
            dg_ref[:, k * LANE:(k + 1) * LANE] = (dzk * y_ref[k] * _dsilu(gk)).astype(BF16)

    chunk_spec = pl.BlockSpec((nch, tm, LANE), lambda i: (0, i, 0))
    g_spec = pl.BlockSpec((nch, tm, LANE), lambda i: (1, i, 0))
    row_spec = pl.BlockSpec((tm, d), lambda i: (i, 0))
    vec_spec = pl.BlockSpec((1, d), lambda i: (0, 0))
    return _call(
        body, name=name, grid=(rows // tm,),
        in_specs=[row_spec, row_spec, row_spec, chunk_spec, g_spec, vec_spec, vec_spec, pl.BlockSpec((e, d), lambda i: (0, 0))],
        out_specs=[chunk_spec, pl.BlockSpec((tm, e), lambda i: (i, 0)), row_spec, row_spec, vec_spec, vec_spec, vec_spec],
        out_shape=[jax.ShapeDtypeStruct((nch, rows, LANE), F32), jax.ShapeDtypeStruct((rows, e), BF16),
                   jax.ShapeDtypeStruct((rows, d), F32), jax.ShapeDtypeStruct((rows, d), BF16)]
        + [jax.ShapeDtypeStruct((1, d), F32)] * 3,
    )(dxo, xin, br, y, ug, gt, lg, wout)


def _outproj_bwd_w(y, ug, dbr, gbuf, row0, name):
    nch, rows, _ = y.shape
    d = dbr.shape[1]
    e = nch * LANE
    es = e // N_CHIPS
    tm = _row_tile(rows, 512)
    assert gbuf.shape[2] == d and row0 % es == 0

    def body(y_ref, g_ref, dbr_ref, buf_ref, o_ref):
        @pl.when(pl.program_id(0) == 0)
        def _():
            o_ref[...] = jnp.zeros_like(o_ref)

        z = jnp.concatenate([(y_ref[k] * _silu(g_ref[k])).astype(BF16) for k in range(nch)], axis=1)
        o_ref[...] += _dot_tn(z, dbr_ref[...]).reshape(N_CHIPS, es, d)

    return _call(
        body, name=name, grid=(rows // tm,),
        in_specs=[pl.BlockSpec((nch, tm, LANE), lambda i: (0, i, 0)),
                  pl.BlockSpec((nch, tm, LANE), lambda i: (1, i, 0)),
                  pl.BlockSpec((tm, d), lambda i: (i, 0)),
                  ANY_SPEC],
        out_specs=pl.BlockSpec((N_CHIPS, es, d), lambda i: (0, row0 // es, 0)),
        out_shape=jax.ShapeDtypeStruct(gbuf.shape, F32),
        input_output_aliases={3: 0},
    )(y, ug, dbr, gbuf)


def _scan(a_ref, b_ref, h_ref, *, length, init, reverse, a_shift, store):
    nblk = length // SUB
    assert nblk % SCAN_UNROLL == 0
    row = lax.broadcasted_iota(jnp.int32, (SUB, LANE), 0)
    last = 0 if reverse else SUB - 1
    edges = [(row >= SUB - k) if reverse else (row < k) for k in (1, 2, 4)]

    def local_scan(a, b):
        for k, edge in zip((1, 2, 4), edges):
            sh = (SUB - k) if reverse else k
            b = b + a * jnp.where(edge, 0.0, pltpu.roll(b, sh, 0))
            a = a * jnp.where(edge, 1.0, pltpu.roll(a, sh, 0))
        return a, b

    def step(i, carry):
        base = pl.multiple_of(((nblk // SCAN_UNROLL - 1 - i) if reverse else i) * (SCAN_UNROLL * SUB), SCAN_UNROLL * SUB)
        order = range(SCAN_UNROLL - 1, -1, -1) if reverse else range(SCAN_UNROLL)
        loaded = [(a_ref[pl.ds(PAD + base + j * SUB + a_shift, SUB), :], b_ref[pl.ds(PAD + base + j * SUB, SUB), :])
                  for j in order]
        scanned = [local_scan(a, b) for a, b in loaded]
        for j, (a, b) in zip(order, scanned):
            if store:
                h_ref[pl.ds(PAD + base + j * SUB, SUB), :] = b + a * carry
            a_l = jnp.broadcast_to(a[last:last + 1, :], (SUB, LANE))
            b_l = jnp.broadcast_to(b[last:last + 1, :], (SUB, LANE))
            carry = b_l + a_l * carry
        return carry

    carry = lax.fori_loop(0, nblk // SCAN_UNROLL, step, jnp.broadcast_to(init, (SUB, LANE)))
    return carry[0:1, :]


def _conv_fwd(src_ref, upad, u_ref, cw, cb, length):
    zeros = jnp.zeros((PAD, LANE), F32)
    upad[pl.ds(0, PAD), :] = zeros
    upad[pl.ds(PAD + length, PAD), :] = zeros
    rt = _row_tile(length, ROW_TILE)

    def copy(i, c):
        t0 = pl.multiple_of(i * rt, rt)
        upad[pl.ds(PAD + t0, rt), :] = src_ref[pl.ds(t0, rt), :]
        return c

    lax.fori_loop(0, length // rt, copy, 0)

    def tile(i, c):
        t0 = pl.multiple_of(i * rt, rt)
        acc = jnp.zeros((rt, LANE), F32)
        for k in range(CONV_TAPS):
            acc = acc + upad[pl.ds(t0 + PAD - CONV_LEFT + k, rt), :] * cw[k:k + 1, :]
        u_ref[pl.ds(t0, rt), :] = acc + cb
        return c

    lax.fori_loop(0, length // rt, tile, 0)


def _gates_fwd(u_ref, a_ref, b_ref, wa, wx, ba, bx, ls, length):
    rt = _row_tile(length, GATE_TILE)

    def tile(i, c):
        t0 = pl.multiple_of(i * rt, rt)
        ut = u_ref[pl.ds(t0, rt), :]
        ub = ut.astype(BF16)
        r = _sigmoid(_dot(ub, wa) + ba)
        ig = _sigmoid(_dot(ub, wx) + bx)
        la = (LRU_C * r) * ls
        a = jnp.exp(la)
        a_ref[pl.ds(PAD + t0, rt), :] = a
        b_ref[pl.ds(PAD + t0, rt), :] = jnp.sqrt(_one_minus_sq(la, a)) * (ig * ut)
        return c

    lax.fori_loop(0, length // rt, tile, 0, unroll=min(GATE_UNROLL, length // rt))


def _lru_specs():
    return [pl.BlockSpec((CONV_TAPS, LANE), lambda n: (0, n)),
            pl.BlockSpec((1, LANE), lambda n: (0, n)),
            pl.BlockSpec((2, None, LANE, LANE), lambda n: (0, n, 0, 0)),
            pl.BlockSpec((2, None, LANE, LANE), lambda n: (0, n, 0, 0)),
            pl.BlockSpec((2, LANE), lambda n: (0, n)),
            pl.BlockSpec((2, LANE), lambda n: (0, n)),
            pl.BlockSpec((2, LANE), lambda n: (0, n))]


def _rglru_fwd(ug, uc, conv_w, conv_b, wa, wx, ba, bx, lam):
    nb = uc.shape[0]
    s_len, t_len = ug.shape[1], uc.shape[1]

    def body(u0_ref, uc0_ref, cw_ref, cb_ref, wa_ref, wx_ref, ba_ref, bx_ref, lam_ref, y_ref,
             upad, ubuf, abuf, hbuf):
        cw, cb = cw_ref[...], cb_ref[...]
        lsig = _log_sigmoid(lam_ref[...])
        zero = jnp.zeros((1, LANE), F32)
        _conv_fwd(uc0_ref, upad, ubuf, cw, cb, t_len)
        h0 = []
        for dr in range(2):
            _gates_fwd(ubuf, abuf, hbuf, wa_ref[dr], wx_ref[dr], ba_ref[dr:dr + 1, :], bx_ref[dr:dr + 1, :],
                       lsig[dr:dr + 1, :], t_len)
            h0.append(_scan(abuf, hbuf, hbuf, length=t_len, init=zero, reverse=(dr == 1), a_shift=0, store=False))
        _conv_fwd(u0_ref, upad, ubuf, cw, cb, s_len)
        rt = _row_tile(s_len, ROW_TILE)
        for dr in range(2):
            _gates_fwd(ubuf, abuf, hbuf, wa_ref[dr], wx_ref[dr], ba_ref[dr:dr + 1, :], bx_ref[dr:dr + 1, :],
                       lsig[dr:dr + 1, :], s_len)
            _scan(abuf, hbuf, hbuf, length=s_len, init=h0[dr], reverse=(dr == 1), a_shift=0, store=True)

            def acc(i, c, dr=dr):
                t0 = pl.multiple_of(i * rt, rt)
                h = hbuf[pl.ds(PAD + t0, rt), :]
                if dr == 0:
                    y_ref[pl.ds(t0, rt), :] = h
                else:
                    y_ref[pl.ds(t0, rt), :] += h
                return c

            lax.fori_loop(0, s_len // rt, acc, 0)

    seq = pltpu.VMEM((s_len + 2 * PAD, LANE), F32)
    return _call(
        body, name="rglru_fwd", grid=(nb,),
        in_specs=[pl.BlockSpec((None, s_len, LANE), lambda n: (n, 0, 0)),
                  pl.BlockSpec((None, t_len, LANE), lambda n: (n, 0, 0))] + _lru_specs(),
        out_specs=pl.BlockSpec((None, s_len, LANE), lambda n: (n, 0, 0)),
        out_shape=jax.ShapeDtypeStruct((nb, s_len, LANE), F32),
        scratch_shapes=[seq, pltpu.VMEM((s_len, LANE), F32), seq, seq],
    )(ug, uc, conv_w, conv_b, wa, wx, ba, bx, lam)


def _rglru_bwd(ug, uc, dy, conv_w, conv_b, wa, wx, ba, bx, lam):
    nb = uc.shape[0]
    e = nb * LANE
    s_len, t_len = ug.shape[1], uc.shape[1]

    def body(u0_ref, uc0_ref, dy_ref, cw_ref, cb_ref, wa_ref, wx_ref, ba_ref, bx_ref, lam_ref,
             du_ref, duc_ref, dcw_ref, dcb_ref, dwa_ref, dwx_ref, dba_ref, dbx_ref, dlam_ref,
             upad, ubuf, abuf, hbuf, lbuf, dubuf, cpad, cu, ca0, ch0, ca1, ch1):
        cw, cb = cw_ref[...], cb_ref[...]
        lam_v = lam_ref[...]
        lsig = _log_sigmoid(lam_v)
        zero = jnp.zeros((1, LANE), F32)
        zpad = jnp.zeros((PAD, LANE), F32)
        for ref in (dcw_ref, dcb_ref, dwa_ref, dwx_ref, dba_ref, dbx_ref, dlam_ref):
            ref[...] = jnp.zeros_like(ref)

        def params(dr):
            return (wa_ref[dr], wx_ref[dr], ba_ref[dr:dr + 1, :], bx_ref[dr:dr + 1, :], lsig[dr:dr + 1, :])

        def direction_bwd(dr, u_ref, a_ref, h_ref, l_ref, dub, length, first):
            wa_d, wx_d, ba_d, bx_d, ls_d = params(dr)
            rt = _row_tile(length, GATE_TILE)
            prev = 1 if dr == 1 else -1

            def tile(i, c):
                t0 = pl.multiple_of(i * rt, rt)
                ut = u_ref[pl.ds(t0, rt), :]
                ub = ut.astype(BF16)
                r = _sigmoid(_dot(ub, wa_d) + ba_d)
                ig = _sigmoid(_dot(ub, wx_d) + bx_d)
                la = (LRU_C * r) * ls_d
                a = a_ref[pl.ds(PAD + t0, rt), :]
                q = _one_minus_sq(la, a)
                rs = lax.rsqrt(q)
                sq = q * rs
                lm = l_ref[pl.ds(PAD + t0, rt), :]
                da = lm * h_ref[pl.ds(PAD + t0 + prev, rt), :]
                dsq = lm * ig * ut
                dig = lm * sq * ut
                dla = da * a - dsq * (a * a) * rs
                dr_ = dla * (LRU_C * ls_d)
                dlam_ref[dr:dr + 1, :] += jnp.sum(dla * (LRU_C * r), axis=0, keepdims=True)
                dpr = dr_ * r * (1.0 - r)
                dpi = dig * ig * (1.0 - ig)
                dba_ref[dr:dr + 1, :] += jnp.sum(dpr, axis=0, keepdims=True)
                dbx_ref[dr:dr + 1, :] += jnp.sum(dpi, axis=0, keepdims=True)
                dprb, dpib = dpr.astype(BF16), dpi.astype(BF16)
                dwa_ref[dr] += _dot_tn(ub, dprb)
                dwx_ref[dr] += _dot_tn(ub, dpib)
                dut = lm * sq * ig + _dot_nt(dprb, wa_d) + _dot_nt(dpib, wx_d)
                if first:
                    dub[pl.ds(PAD + t0, rt), :] = dut
                else:
                    dub[pl.ds(PAD + t0, rt), :] += dut
                return c

            lax.fori_loop(0, length // rt, tile, 0, unroll=min(GATE_UNROLL, length // rt))

        def conv_bwd(dub, src_pad, out_ref, length):
            rt = _row_tile(length, ROW_TILE)

            def tile(i, c):
                t0 = pl.multiple_of(i * rt, rt)
                dut = dub[pl.ds(PAD + t0, rt), :]
                dcb_ref[...] += jnp.sum(dut, axis=0, keepdims=True)
                acc = jnp.zeros((rt, LANE), F32)
                for k in range(CONV_TAPS):
                    sh = CONV_LEFT - k
                    acc = acc + dub[pl.ds(PAD + t0 + sh, rt), :] * cw[k:k + 1, :]
                    dcw_ref[k:k + 1, :] += jnp.sum(dut * src_pad[pl.ds(PAD + t0 - sh, rt), :], axis=0, keepdims=True)
                out_ref[pl.ds(t0, rt), :] = acc.astype(out_ref.dtype)
                return c

            lax.fori_loop(0, length // rt, tile, 0)

        _conv_fwd(uc0_ref, cpad, cu, cw, cb, t_len)
        cbufs = ((ca0, ch0), (ca1, ch1))
        h0 = []
        for dr in range(2):
            ca, chh = cbufs[dr]
            _gates_fwd(cu, ca, chh, *params(dr), t_len)
            h0.append(_scan(ca, chh, chh, length=t_len, init=zero, reverse=(dr == 1), a_shift=0, store=True))
        _conv_fwd(u0_ref, upad, ubuf, cw, cb, s_len)
        rt = _row_tile(s_len, ROW_TILE)
        dh0 = []
        for dr in range(2):
            rev = dr == 1
            _gates_fwd(ubuf, abuf, hbuf, *params(dr), s_len)
            _scan(abuf, hbuf, hbuf, length=s_len, init=h0[dr], reverse=rev, a_shift=0, store=True)
            first_row = PAD + s_len if rev else PAD - 1
            hbuf[pl.ds(first_row, 1), :] = h0[dr]
            end_row = PAD - 1 if rev else PAD + s_len
            abuf[pl.ds(end_row, 1), :] = zero

            def copy(i, c):
                t0 = pl.multiple_of(i * rt, rt)
                lbuf[pl.ds(PAD + t0, rt), :] = dy_ref[pl.ds(t0, rt), :]
                return c

            lax.fori_loop(0, s_len // rt, copy, 0)
            _scan(abuf, lbuf, lbuf, length=s_len, init=zero, reverse=not rev, a_shift=(-1 if rev else 1), store=True)
            start = PAD + s_len - 1 if rev else PAD
            dh0.append(abuf[pl.ds(start, 1), :] * lbuf[pl.ds(start, 1), :])
            direction_bwd(dr, ubuf, abuf, hbuf, lbuf, dubuf, s_len, first=(dr == 0))
        dubuf[pl.ds(0, PAD), :] = zpad
        dubuf[pl.ds(PAD + s_len, PAD), :] = zpad
        conv_bwd(dubuf, upad, du_ref, s_len)
        lc = lbuf
        duc_buf = dubuf
        for dr in range(2):
            rev = dr == 1
            ca, chh = cbufs[dr]
            first_row = PAD + t_len if rev else PAD - 1
            chh[pl.ds(first_row, 1), :] = zero
            end_row = PAD - 1 if rev else PAD + t_len
            ca[pl.ds(end_row, 1), :] = zero + 1.0
            rtc = _row_tile(t_len, ROW_TILE)

            def clear(i, c):
                t0 = pl.multiple_of(i * rtc, rtc)
                lc[pl.ds(PAD + t0, rtc), :] = jnp.zeros((rtc, LANE), F32)
                return c

            lax.fori_loop(0, t_len // rtc, clear, 0)
            _scan(ca, lc, lc, length=t_len, init=dh0[dr], reverse=not rev, a_shift=(-1 if rev else 1), store=True)
            direction_bwd(dr, cu, ca, chh, lc, duc_buf, t_len, first=(dr == 0))
        duc_buf[pl.ds(0, PAD), :] = zpad
        duc_buf[pl.ds(PAD + t_len, PAD), :] = zpad
        conv_bwd(duc_buf, cpad, duc_ref, t_len)
        dlam_ref[...] = dlam_ref[...] * (1.0 - _sigmoid(lam_v))

    seq = pltpu.VMEM((s_len + 2 * PAD, LANE), F32)
    cseq = pltpu.VMEM((t_len + 2 * PAD, LANE), F32)
    vec2 = pl.BlockSpec((2, LANE), lambda n: (0, n))
    wspec = pl.BlockSpec((2, None, LANE, LANE), lambda n: (0, n, 0, 0))
    return _call(
        body, name="rglru_bwd", grid=(nb,),
        in_specs=[pl.BlockSpec((None, s_len, LANE), lambda n: (n, 0, 0)),
                  pl.BlockSpec((None, t_len, LANE), lambda n: (n, 0, 0)),
                  pl.BlockSpec((None, s_len, LANE), lambda n: (n, 0, 0))] + _lru_specs(),
        out_specs=[pl.BlockSpec((s_len, LANE), lambda n: (0, n)),
                   pl.BlockSpec((t_len, LANE), lambda n: (0, n)),
                   pl.BlockSpec((CONV_TAPS, LANE), lambda n: (0, n)),
                   pl.BlockSpec((1, LANE), lambda n: (0, n)),
                   wspec, wspec, vec2, vec2, vec2],
        out_shape=[jax.ShapeDtypeStruct((s_len, e), BF16), jax.ShapeDtypeStruct((t_len, e), BF16),
                   jax.ShapeDtypeStruct((CONV_TAPS, e), F32), jax.ShapeDtypeStruct((1, e), F32),
                   jax.ShapeDtypeStruct((2, nb, LANE, LANE), F32), jax.ShapeDtypeStruct((2, nb, LANE, LANE), F32),
                   jax.ShapeDtypeStruct((2, e), F32), jax.ShapeDtypeStruct((2, e), F32), jax.ShapeDtypeStruct((2, e), F32)],
        scratch_shapes=[seq, pltpu.VMEM((s_len, LANE), F32), seq, seq, seq, seq,
                        cseq, pltpu.VMEM((t_len, LANE), F32), cseq, cseq, cseq, cseq],
    )(ug, uc, dy, conv_w, conv_b, wa, wx, ba, bx, lam)


def _pool_windows(src_ref, out_ref, colbuf, rowbuf, half, transpose, s_len):
    gw = GRID_W
    lg = gw.bit_length() - 1
    n_rows = s_len // gw
    cm, rm = 16, 8 * gw
    rt = _row_tile(s_len, ROW_TILE)
    assert rt % gw == 0
    offs = range(-half, half)
    colbuf[pl.ds(0, cm), :] = jnp.zeros((cm, LANE), F32)
    colbuf[pl.ds(cm + s_len, cm), :] = jnp.zeros((cm, LANE), F32)

    def zrow(i, c):
        t0 = pl.multiple_of(i * gw, gw)
        rowbuf[pl.ds(t0, gw), :] = jnp.zeros((gw, LANE), F32)
        rowbuf[pl.ds(rm + s_len + t0, gw), :] = jnp.zeros((gw, LANE), F32)
        return c

    lax.fori_loop(0, rm // gw, zrow, 0)

    def counts(t0):
        tok = t0 + lax.broadcasted_iota(jnp.int32, (rt, LANE), 0)
        col = tok & (gw - 1)
        row = tok >> lg
        ccnt = (jnp.minimum(col + half, gw) - jnp.maximum(col - half, 0)).astype(F32)
        rcnt = (jnp.minimum(row + half, n_rows) - jnp.maximum(row - half, 0)).astype(F32)
        return col, ccnt, rcnt

    def col_sum(t0, col, sign):
        acc = jnp.zeros((rt, LANE), F32)
        for o in offs:
            so = sign * o
            ok = (col + so >= 0) & (col + so < gw)
            acc = acc + jnp.where(ok, colbuf[pl.ds(cm + t0 + so, rt), :], 0.0)
        return acc

    def row_sum(t0, sign):
        acc = jnp.zeros((rt, LANE), F32)
        for o in offs:
            acc = acc + rowbuf[pl.ds(rm + t0 + sign * o * gw, rt), :]
        return acc

    def loop(fn):
        def step(i, c):
            fn(pl.multiple_of(i * rt, rt))
            return c
        lax.fori_loop(0, s_len // rt, step, 0)

    if not transpose:
        def fill(t0):
            colbuf[pl.ds(cm + t0, rt), :] = src_ref[pl.ds(t0, rt), :]

        def cols(t0):
            col, ccnt, _ = counts(t0)
            rowbuf[pl.ds(rm + t0, rt), :] = col_sum(t0, col, 1) / ccnt

        def rows(t0):
            _, _, rcnt = counts(t0)
            out_ref[pl.ds(t0, rt), :] = (row_sum(t0, 1) / rcnt - src_ref[pl.ds(t0, rt), :]).astype(out_ref.dtype)

        loop(fill)
        loop(cols)
        loop(rows)
    else:
        def fill(t0):
            _, _, rcnt = counts(t0)
            rowbuf[pl.ds(rm + t0, rt), :] = src_ref[pl.ds(t0, rt), :] / rcnt

        def rows(t0):
            _, ccnt, _ = counts(t0)
            colbuf[pl.ds(cm + t0, rt), :] = row_sum(t0, -1) / ccnt

        def cols(t0):
            col, _, _ = counts(t0)
            out_ref[pl.ds(t0, rt), :] = (col_sum(t0, col, -1) - src_ref[pl.ds(t0, rt), :]).astype(out_ref.dtype)

        loop(fill)
        loop(rows)
        loop(cols)


def _pool_map(src, nb, transpose, out_chunk_major, name):
    s_len = src.shape[1]
    cpg = nb // len(POOL_WINDOWS)

    def body(src_ref, out_ref, colbuf, rowbuf):
        n = pl.program_id(0)
        for gi, w in enumerate(POOL_WINDOWS):
            @pl.when(n // cpg == gi)
            def _(w=w):
                _pool_windows(src_ref, out_ref, colbuf, rowbuf, w // 2, transpose, s_len)

    if out_chunk_major:
        out_spec = pl.BlockSpec((None, s_len, LANE), lambda n: (n, 0, 0))
        out_shape = jax.ShapeDtypeStruct((nb, s_len, LANE), BF16)
    else:
        out_spec = pl.BlockSpec((s_len, LANE), lambda n: (0, n))
        out_shape = jax.ShapeDtypeStruct((s_len, nb * LANE), BF16)
    return _call(
        body, name=name, grid=(nb,),
        in_specs=[pl.BlockSpec((None, s_len, LANE), lambda n: (n, 0, 0))],
        out_specs=out_spec, out_shape=out_shape,
        scratch_shapes=[pltpu.VMEM((s_len + 32, LANE), F32), pltpu.VMEM((s_len + 16 * GRID_W, LANE), F32)],
    )(src)


def _group_weight(w_ref):
    return jnp.concatenate([w_ref[k] for k in range(N_CHIPS)], axis=0)


def _pool_mm_fwd(dm, wp, scale):
    nb, rows, _ = dm.shape
    _, ng, pq, pg = wp.shape
    cpg = pg // LANE
    tm = _row_tile(rows, 512)

    def body(d_ref, w_ref, s_ref, y_ref):
        _put_chunks(y_ref, _dot(_cat(d_ref, cpg), _group_weight(w_ref)) * s_ref[...], cpg)

    cspec = pl.BlockSpec((cpg, tm, LANE), lambda i, g: (g, i, 0))
    return _call(
        body, name="pool_mm_fwd", grid=(rows // tm, ng),
        in_specs=[cspec, pl.BlockSpec((N_CHIPS, None, pq, pg), lambda i, g: (0, g, 0, 0)),
                  pl.BlockSpec((1, pg), lambda i, g: (0, g))],
        out_specs=cspec, out_shape=jax.ShapeDtypeStruct((nb, rows, LANE), F32),
    )(dm, wp, scale)


def _pool_mm_bwd(dy, dm, wp, scale, gbuf, row0):
    nb, rows, _ = dm.shape
    _, ng, pq, pg = wp.shape
    cpg = pg // LANE
    tm = _row_tile(rows, 512)
    nt = rows // tm
    assert gbuf.shape[2] == 2 * pg and row0 % pq == 0

    def body(dy_ref, d_ref, w_ref, s_ref, buf_ref, dd_ref, dwp_ref, dsc_ref, acc):
        i = pl.program_id(1)

        @pl.when(i == 0)
        def _():
            acc[...] = jnp.zeros_like(acc)
            dsc_ref[...] = jnp.zeros_like(dsc_ref)

        dyv = _cat(dy_ref, cpg)
        dc = _cat(d_ref, cpg)
        w = _group_weight(w_ref)
        dsc_ref[...] += jnp.sum(dyv * _dot(dc, w), axis=0, keepdims=True)
        dyp = (dyv * s_ref[...]).astype(BF16)
        _put_chunks(dd_ref, _dot_nt(dyp, w), cpg)
        acc[...] += _dot_tn(dc, dyp)

        @pl.when(i == nt - 1)
        def _():
            dwp_ref[...] = acc[...].reshape(N_CHIPS, pq, pg)

    cspec = pl.BlockSpec((cpg, tm, LANE), lambda g, i: (g, i, 0))
    sspec = pl.BlockSpec((1, pg), lambda g, i: (0, g))
    return _call(
        body, name="pool_mm_bwd", grid=(ng, nt),
        in_specs=[cspec, cspec, pl.BlockSpec((N_CHIPS, None, pq, pg), lambda g, i: (0, g, 0, 0)), sspec, ANY_SPEC],
        out_specs=[cspec, pl.BlockSpec((N_CHIPS, pq, pg), lambda g, i: (0, row0 // pq + g // 2, g % 2)), sspec],
        out_shape=[jax.ShapeDtypeStruct((nb, rows, LANE), F32), jax.ShapeDtypeStruct(gbuf.shape, F32),
                   jax.ShapeDtypeStruct((1, ng * pg), F32)],
        scratch_shapes=[pltpu.VMEM((pg, pg), F32)],
        input_output_aliases={4: 1},
    )(dy, dm, wp, scale, gbuf)


def _adamw_math(w, g, m, v):
    nm = ADAM_B1 * m + (1.0 - ADAM_B1) * g
    nv = ADAM_B2 * v + (1.0 - ADAM_B2) * jnp.square(g)
    m_hat = nm / (1.0 - ADAM_B1 ** ADAM_STEP)
    v_hat = nv / (1.0 - ADAM_B2 ** ADAM_STEP)
    return -ADAM_LR * (m_hat / (jnp.sqrt(v_hat) + ADAM_EPS) + ADAM_WD * w), nm, nv


def _adamw_param(w3, m3, v3, gsrcs, pick, tm, name):
    n_blk, rows, cols = w3.shape
    ng = len(gsrcs)

    def body(*refs):
        w_ref, m_ref, v_ref = refs[:3]
        g_refs = refs[3:3 + ng]
        go_ref, d_ref, nm_ref, nv_ref = refs[3 + ng:]
        g = pick(pl.program_id(0), [r[...] for r in g_refs])
        go_ref[...] = g
        d_ref[...], nm_ref[...], nv_ref[...] = _adamw_math(w_ref[...], g, m_ref[...], v_ref[...])

    spec = pl.BlockSpec((None, tm, cols), lambda n, i: (n, i, 0))
    return _call(
        body, name=name, grid=(n_blk, rows // tm),
        in_specs=[spec] * 3 + [pl.BlockSpec(shape, imap) for _, shape, imap in gsrcs],
        out_specs=[spec] * 4, out_shape=[jax.ShapeDtypeStruct(w3.shape, F32)] * 4,
    )(w3, m3, v3, *[a for a, _, _ in gsrcs])


def _adamw_small(quads):
    n = len(quads)

    def body(*refs):
        ins, outs = refs[:4 * n], refs[4 * n:]
        for k in range(n):
            w, g, m, v = (r[...] for r in ins[4 * k:4 * k + 4])
            outs[3 * k][...], outs[3 * k + 1][...], outs[3 * k + 2][...] = _adamw_math(w, g, m, v)

    flat = [a for q in quads for a in q]
    res = _call(body, name="adamw_small", grid=(1,),
                in_specs=[pl.BlockSpec(a.shape, lambda i: (0, 0)) for a in flat],
                out_specs=[pl.BlockSpec(q[0].shape, lambda i: (0, 0)) for q in quads for _ in range(3)],
                out_shape=[jax.ShapeDtypeStruct(q[0].shape, F32) for q in quads for _ in range(3)])(*flat)
    return [tuple(res[3 * k:3 * k + 3]) for k in range(n)]


def _place():
    return lax.axis_index("x"), lax.axis_index("y"), lax.axis_index("c")


def _other_chips(x, y):
    return [(1 - x, y), (x, 1 - y), (1 - x, 1 - y)]


def _gather_chips(arrays, name, row0=0, rows=None):
    n = len(arrays)
    nrows = [a.shape[0] if rows is None else rows for a in arrays]
    halves = [r // 2 for r in nrows]
    for a, r, h in zip(arrays, nrows, halves):
        assert 2 * h == r and h % (32 // a.dtype.itemsize) == 0

    def body(*refs):
        ins, outs = refs[:n], refs[n:2 * n]
        send_sems, recv_sems, local_sems = refs[2 * n:]
        x, y, c = _place()
        me = 2 * x + y
        chips = _other_chips(x, y)

        def mine(k):
            return pl.ds(c * halves[k], halves[k])

        def theirs(k):
            return pl.ds((1 - c) * halves[k], halves[k])

        def push(k, j, src, dst, to):
            return pltpu.make_async_remote_copy(src_ref=src, dst_ref=dst, send_sem=send_sems.at[6 * k + j],
                                                recv_sem=recv_sems.at[6 * k + j], device_id=to, device_id_type=MESH)

        local = [pltpu.make_async_copy(ins[k].at[pl.ds(row0, nrows[k])], outs[k].at[me], local_sems.at[k]) for k in range(n)]
        for cp in local:
            cp.start()
        started = []
        for j, (cx, cy) in enumerate(chips):
            for k in range(n):
                cp = push(k, j, ins[k].at[pl.ds(row0 + c * halves[k], halves[k])], outs[k].at[me, mine(k)], (cx, cy, c))
                cp.start()
                started.append(cp)
        for j, (cx, cy) in enumerate(chips):
            for k in range(n):
                slab = outs[k].at[2 * cx + cy, mine(k)]
                push(k, j, slab, slab, (x, y, c)).wait_recv()
                fwd = push(k, 3 + j, slab, slab, (x, y, 1 - c))
                fwd.start()
                started.append(fwd)
        for j, (cx, cy) in enumerate(chips):
            for k in range(n):
                slab = outs[k].at[2 * cx + cy, theirs(k)]
                push(k, 3 + j, slab, slab, (x, y, c)).wait_recv()
        for cp in started:
            cp.wait_send()
        for cp in local:
            cp.wait()

    return _call(
        body, name=name, in_specs=[ANY_SPEC] * n, out_specs=[ANY_SPEC] * n,
        out_shape=[jax.ShapeDtypeStruct((N_CHIPS, r, a.shape[1]), a.dtype) for a, r in zip(arrays, nrows)],
        scratch_shapes=[pltpu.SemaphoreType.DMA((6 * n,)), pltpu.SemaphoreType.DMA((6 * n,)), pltpu.SemaphoreType.DMA((n,))],
    )(*arrays)


def _gather_devices(v):
    shape = v.shape

    def body(v_ref, out_ref, send_sems, recv_sems):
        x, y, c = _place()
        me = 4 * x + 2 * y + c
        out_ref[me] = v_ref[...]
        sends = []
        for k in range(1, N_DEV):
            to = (me + k) % N_DEV
            cp = pltpu.make_async_remote_copy(src_ref=v_ref, dst_ref=out_ref.at[me], send_sem=send_sems.at[k],
                                              recv_sem=recv_sems.at[k], device_id=(to // 4, (to // 2) % 2, to % 2),
                                              device_id_type=MESH)
            cp.start()
            sends.append(cp)
        for k in range(1, N_DEV):
            frm = (me + N_DEV - k) % N_DEV
            pltpu.make_async_remote_copy(src_ref=v_ref, dst_ref=out_ref.at[frm], send_sem=send_sems.at[k],
                                         recv_sem=recv_sems.at[k], device_id=(x, y, c), device_id_type=MESH).wait_recv()
        for cp in sends:
            cp.wait_send()

    vspec = pl.BlockSpec(memory_space=pltpu.VMEM)
    return _call(body, name="gather_devices", in_specs=[vspec], out_specs=vspec,
                 out_shape=jax.ShapeDtypeStruct((N_DEV,) + shape, F32),
                 scratch_shapes=[pltpu.SemaphoreType.DMA((N_DEV,)), pltpu.SemaphoreType.DMA((N_DEV,))])(v)


HBM_SPEC = pl.BlockSpec(memory_space=pltpu.HBM)
SEM_SPEC = pl.BlockSpec(memory_space=pltpu.SEMAPHORE)
SIDE_EFFECT = pltpu.SideEffectType.DATAFLOW_SIDE_EFFECTING


def _push_copies(src_refs, land_refs, send_sems, recv_sems, per_peer):
    x, y, c = _place()
    me = 2 * x + y
    copies = []
    for j, (cx, cy) in enumerate(_other_chips(x, y)):
        for k, (src, land) in enumerate(zip(src_refs, land_refs)):
            copies.append(pltpu.make_async_remote_copy(
                src_ref=src.at[2 * cx + cy] if per_peer else src, dst_ref=land.at[me], send_sem=send_sems.at[3 * k + j],
                recv_sem=recv_sems.at[3 * k + j], device_id=(cx, cy, c), device_id_type=MESH))
    return copies


def _push_start(srcs, lands, per_peer, name):
    n = len(srcs)

    def body(*refs):
        src_refs, land_refs = refs[:n], refs[n:2 * n]
        send_sems, recv_sems = refs[2 * n], refs[2 * n + 1]
        token = refs[-1]
        for cp in _push_copies(src_refs, land_refs, send_sems, recv_sems, per_peer):
            cp.start()
        token[...] = jnp.zeros_like(token)

    bufs = [pltpu.with_memory_space_constraint(a, pltpu.HBM) for a in list(srcs) + list(lands)]
    res = _call(
        body, name=name,
        out_shape=[pltpu.SemaphoreType.DMA((3 * n,)), pltpu.SemaphoreType.DMA((3 * n,))]
        + [pltpu.HBM(a.shape, a.dtype) for a in bufs] + [jax.ShapeDtypeStruct((SUB, LANE), F32)],
        in_specs=[HBM_SPEC] * (2 * n),
        out_specs=[SEM_SPEC, SEM_SPEC] + [HBM_SPEC] * (2 * n) + [pl.BlockSpec(memory_space=pltpu.VMEM)],
        input_output_aliases={i: 2 + i for i in range(2 * n)},
        compiler_params=pltpu.CompilerParams(has_side_effects=SIDE_EFFECT),
    )(*bufs)
    return res[0], res[1], list(res[2:2 + n]), list(res[2 + n:2 + 2 * n]), res[-1]


def _push_wait(send_sems, recv_sems, srcs, lands, per_peer, after, name):
    n = len(srcs)

    def body(*refs):
        src_refs, land_refs = refs[:n], refs[n:2 * n]
        send_sems, recv_sems = refs[2 * n], refs[2 * n + 1]
        for cp in _push_copies(src_refs, land_refs, send_sems, recv_sems, per_peer):
            cp.wait_send()
            cp.wait_recv()

    res = _call(
        body, name=name,
        out_shape=[pltpu.HBM(a.shape, a.dtype) for a in list(srcs) + list(lands)],
        in_specs=[HBM_SPEC] * (2 * n) + [SEM_SPEC, SEM_SPEC, ANY_SPEC],
        out_specs=[HBM_SPEC] * (2 * n),
        input_output_aliases={i: i for i in range(2 * n)},
        compiler_params=pltpu.CompilerParams(has_side_effects=SIDE_EFFECT),
    )(*srcs, *lands, send_sems, recv_sems, after)
    return list(res[n:])


def _sibling_swap(g):
    _, rows, w = g.shape
    half = rows // 2

    def body(g_ref, out_ref, send_sem, recv_sem):
        x, y, c = _place()
        cp = pltpu.make_async_remote_copy(src_ref=g_ref.at[:, pl.ds((1 - c) * half, half)], dst_ref=out_ref,
                                          send_sem=send_sem, recv_sem=recv_sem, device_id=(x, y, 1 - c), device_id_type=MESH)
        cp.start()
        cp.wait()

    return _call(body, name="rs_sibling_swap", in_specs=[ANY_SPEC], out_specs=ANY_SPEC,
                 out_shape=jax.ShapeDtypeStruct((N_CHIPS, half, w), F32),
                 scratch_shapes=[pltpu.SemaphoreType.DMA, pltpu.SemaphoreType.DMA])(g)


def _pair_add(g, got, place):
    _, rows, w = g.shape
    half = rows // 2
    tm = _row_tile(half, RS_TILE)
    nt = half // tm

    def body(p_ref, a_ref, b_ref, o_ref, own_ref):
        v = a_ref[...] + b_ref[...]
        o_ref[...] = v.astype(BF16)

        @pl.when(pl.program_id(1) == p_ref[1])
        def _():
            own_ref[...] = v

    return _call(
        body, name="rs_pair_add",
        grid_spec=pltpu.PrefetchScalarGridSpec(
            num_scalar_prefetch=1, grid=(nt, N_CHIPS),
            in_specs=[pl.BlockSpec((None, tm, w), lambda i, s, p: (s, p[0] * nt + i, 0)),
                      pl.BlockSpec((None, tm, w), lambda i, s, p: (s, i, 0))],
            out_specs=[pl.BlockSpec((None, tm, w), lambda i, s, p: (s, i, 0)),
                       pl.BlockSpec((tm, w), lambda i, s, p: (i, 0))]),
        out_shape=[jax.ShapeDtypeStruct((N_CHIPS, half, w), BF16), jax.ShapeDtypeStruct((half, w), F32)],
    )(place, g, got)


def _chip_exchange(p):
    _, half, w = p.shape

    def body(p_ref, out_ref, send_sems, recv_sems, local_sem):
        x, y, c = _place()
        me = 2 * x + y
        chips = _other_chips(x, y)
        local = pltpu.make_async_copy(p_ref.at[me], out_ref.at[me], local_sem)
        local.start()
        sends = []
        for j, (cx, cy) in enumerate(chips):
            cp = pltpu.make_async_remote_copy(src_ref=p_ref.at[2 * cx + cy], dst_ref=out_ref.at[me], send_sem=send_sems.at[j],
                                              recv_sem=recv_sems.at[j], device_id=(cx, cy, c), device_id_type=MESH)
            cp.start()
            sends.append(cp)
        for j, (cx, cy) in enumerate(chips):
            slab = out_ref.at[2 * cx + cy]
            pltpu.make_async_remote_copy(src_ref=slab, dst_ref=slab, send_sem=send_sems.at[j], recv_sem=recv_sems.at[j],
                                         device_id=(x, y, c), device_id_type=MESH).wait_recv()
        for cp in sends:
            cp.wait_send()
        local.wait()

    return _call(body, name="rs_chip_exchange", in_specs=[ANY_SPEC], out_specs=ANY_SPEC,
                 out_shape=jax.ShapeDtypeStruct((N_CHIPS, half, w), p.dtype),
                 scratch_shapes=[pltpu.SemaphoreType.DMA((3,)), pltpu.SemaphoreType.DMA((3,)), pltpu.SemaphoreType.DMA])(p)


def _sum_chips(parts, own, place):
    _, half, w = parts.shape
    tm = _row_tile(half, RS_TILE)

    def body(p_ref, parts_ref, own_ref, o_ref):
        me = p_ref[1]
        t = [jnp.where(me == q, own_ref[...], parts_ref[q].astype(F32)) for q in range(N_CHIPS)]
        o_ref[...] = (t[0] + t[1]) + (t[2] + t[3])

    return _call(
        body, name="rs_sum_chips",
        grid_spec=pltpu.PrefetchScalarGridSpec(
            num_scalar_prefetch=1, grid=(half // tm,),
            in_specs=[pl.BlockSpec((N_CHIPS, tm, w), lambda i, p: (0, i, 0)), pl.BlockSpec((tm, w), lambda i, p: (i, 0))],
            out_specs=pl.BlockSpec((tm, w), lambda i, p: (i, 0))),
        out_shape=jax.ShapeDtypeStruct((half, w), F32),
    )(place, parts, own)


def _sibling_gather(hpart):
    half, w = hpart.shape

    def body(h_ref, out_ref, send_sem, recv_sem, local_sem):
        x, y, c = _place()
        local = pltpu.make_async_copy(h_ref, out_ref.at[pl.ds(c * half, half)], local_sem)
        local.start()
        cp = pltpu.make_async_remote_copy(src_ref=h_ref, dst_ref=out_ref.at[pl.ds(c * half, half)], send_sem=send_sem,
                                          recv_sem=recv_sem, device_id=(x, y, 1 - c), device_id_type=MESH)
        cp.start()
        other = out_ref.at[pl.ds((1 - c) * half, half)]
        pltpu.make_async_remote_copy(src_ref=other, dst_ref=other, send_sem=send_sem, recv_sem=recv_sem,
                                     device_id=(x, y, c), device_id_type=MESH).wait_recv()
        cp.wait_send()
        local.wait()

    return _call(body, name="rs_sibling_gather", in_specs=[ANY_SPEC], out_specs=ANY_SPEC,
                 out_shape=jax.ShapeDtypeStruct((2 * half, w), F32),
                 scratch_shapes=[pltpu.SemaphoreType.DMA, pltpu.SemaphoreType.DMA, pltpu.SemaphoreType.DMA])(hpart)


def _reduce_scatter(g, place):
    pair, own = _pair_add(g, _sibling_swap(g), place)
    return _sibling_gather(_sum_chips(_chip_exchange(pair), own, place))


WEIGHTS = ("c_ctx", "w_mod", "b_mod", "w_in", "w_out", "ln_g", "ln_b", "conv_w", "conv_b", "lru_wa", "lru_ba", "lru_wx",
           "lru_bx", "lru_lam", "pool_w", "pool_scale")
SMALL_GATHERED = ("conv_w", "lru_ba", "lru_bx", "lru_lam", "pool_scale")
SMALL_UPDATED = ("c_ctx", "b_mod", "ln_g", "ln_b", "conv_w", "conv_b", "lru_ba", "lru_bx", "lru_lam", "pool_scale")


def kernel(x, c, ctx, c_ctx, w_mod, b_mod, w_in, w_out, ln_g, ln_b, conv_w, conv_b, lru_wa, lru_ba, lru_wx, lru_bx, lru_lam, pool_w, pool_scale, loss_target, m_c_ctx, m_w_mod, m_b_mod, m_w_in, m_w_out, m_ln_g, m_ln_b, m_conv_w, m_conv_b, m_lru_wa, m_lru_ba, m_lru_wx, m_lru_bx, m_lru_lam, m_pool_w, m_pool_scale, v_c_ctx, v_w_mod, v_b_mod, v_w_in, v_w_out, v_ln_g, v_ln_b, v_conv_w, v_conv_b, v_lru_wa, v_lru_ba, v_lru_wx, v_lru_bx, v_lru_lam, v_pool_w, v_pool_scale):
    weights = dict(c_ctx=c_ctx, w_mod=w_mod, b_mod=b_mod, w_in=w_in, w_out=w_out, ln_g=ln_g, ln_b=ln_b, conv_w=conv_w,
                   conv_b=conv_b, lru_wa=lru_wa, lru_ba=lru_ba, lru_wx=lru_wx, lru_bx=lru_bx, lru_lam=lru_lam,
                   pool_w=pool_w, pool_scale=pool_scale)
    mom1 = dict(c_ctx=m_c_ctx, w_mod=m_w_mod, b_mod=m_b_mod, w_in=m_w_in, w_out=m_w_out, ln_g=m_ln_g, ln_b=m_ln_b,
                conv_w=m_conv_w, conv_b=m_conv_b, lru_wa=m_lru_wa, lru_ba=m_lru_ba, lru_wx=m_lru_wx, lru_bx=m_lru_bx,
                lru_lam=m_lru_lam, pool_w=m_pool_w, pool_scale=m_pool_scale)
    mom2 = dict(c_ctx=v_c_ctx, w_mod=v_w_mod, b_mod=v_b_mod, w_in=v_w_in, w_out=v_w_out, ln_g=v_ln_g, ln_b=v_ln_b,
                conv_w=v_conv_w, conv_b=v_conv_b, lru_wa=v_lru_wa, lru_ba=v_lru_ba, lru_wx=v_lru_wx, lru_bx=v_lru_bx,
                lru_lam=v_lru_lam, pool_w=v_pool_w, pool_scale=v_pool_scale)
    xs, cx, target = x[0], ctx[0], loss_target[0]
    s_len, d = xs.shape
    es = w_out.shape[1]
    e = es * N_CHIPS
    nb = e // LANE
    c3 = w_mod.shape[2]
    n4 = w_in.shape[2]
    pq, pg = pool_w.shape[2], pool_w.shape[3]
    ng = len(POOL_WINDOWS)
    width = n4
    assert width == d and 2 * pg == width and 2 * nb * LANE == N_CHIPS * width and d % (2 * N_CHIPS) == 0
    px, py, pc = _place()
    place = jnp.stack([pc, 2 * px + py]).astype(jnp.int32)
    cctx2 = c_ctx[None, :]

    eq = e // N_CHIPS
    small_rows = [(conv_w[0], 0), (lru_ba[0], CONV_TAPS), (lru_bx[0], CONV_TAPS + 2), (lru_lam[0], CONV_TAPS + 4),
                  (pool_scale, CONV_TAPS + 6)]
    small = _rows_kernel([(a, r, 0) for a, r in small_rows], 2 * SUB, eq, "pack_small_weights")
    wm_g, win0, sg = _gather_chips([w_mod.astype(BF16).reshape(DEPTH * d, c3), w_in[0].astype(BF16), small], "gather_weights0")
    later = [w_out[0].astype(BF16), w_in[1].astype(BF16), w_out[1].astype(BF16), pool_w.astype(BF16).reshape(ng * pq, pg)]
    w_send, w_recv, later, later_lands, w_token = _push_start(
        later, [jnp.broadcast_to(a[None], (N_CHIPS,) + a.shape) for a in later], False, "gather_weights1_start")
    wm_all = wm_g.reshape(N_CHIPS, DEPTH, d, c3)
    full = {n: jnp.swapaxes(sg[:, r:r + a.shape[0]], 0, 1).reshape(a.shape[0], e)
            for n, (a, r) in zip(SMALL_GATHERED, small_rows)}
    wa_b, wx_b = lru_wa[0].astype(BF16), lru_wx[0].astype(BF16)
    lru_args = (full["conv_w"], conv_b, wa_b, wx_b, full["lru_ba"], full["lru_bx"], full["lru_lam"])
    scale_f = full["pool_scale"]

    mod = _mod_fwd(c + w_token[0:1, 0:1], cctx2, wm_all, b_mod[:, None, :])

    def mod_parts(l, row):
        v = mod[l, row]
        return v[None, :d], 1.0 + v[None, d:2 * d], v[None, 2 * d:]

    sh0, sc0, gt0 = mod_parts(0, 0)
    shc, scc, _ = mod_parts(0, 1)
    sh1, sc1, gt1 = mod_parts(1, 0)
    lg = [ln_g[l][None, :] for l in range(DEPTH)]
    lb = [ln_b[l][None, :] for l in range(DEPTH)]

    ug0 = _inproj_fwd(xs, sc0, sh0, win0, "inproj_fwd0")
    uc0 = _inproj_fwd(cx, scc, shc, win0[:2], "inproj_fwd_ctx")
    y0 = _rglru_fwd(ug0, uc0, *lru_args)
    wout0_g, win1, wout1_g, wp_g = _push_wait(w_send, w_recv, later, later_lands, False, y0, "gather_weights1_wait")
    win = [win0, win1]
    wout = [wout0_g.reshape(e, d), wout1_g.reshape(e, d)]
    wp = wp_g.reshape(N_CHIPS, ng, pq, pg)
    br0, x1 = _outproj_fwd(y0, ug0, xs, gt0, wout[0], lg[0], lb[0], None, "outproj_fwd0")
    ug1 = _inproj_fwd(x1, sc1, sh1, win[1], "inproj_fwd1")
    d1 = _pool_map(ug1, nb, False, True, "pool_fwd")
    y1 = _pool_mm_fwd(d1, wp, scale_f)
    br1, dxo, loss_part = _outproj_fwd(y1, ug1, x1, gt1, wout[1], lg[1], lb[1], target, "outproj_fwd1")
    loss = lax.psum(loss_part[0, 0] * (0.5 / d), ("x", "y", "c"))

    row_wout = d
    row_tail = d + es
    wq = 2 * (nb // N_CHIPS) * LANE * LANE // width
    whole = lambda r: (r + 2 * RS_TILE - 1) // (2 * RS_TILE) * (2 * RS_TILE)
    rows1 = whole(row_tail + pg // 2)
    rows0 = whole(row_tail + 2 * wq)
    gbuf1 = jnp.zeros((N_CHIPS, rows1, width), F32)
    gbuf0 = jnp.zeros((N_CHIPS, rows0, width), F32)

    dy1, dg1, dxres1, dbr1, dlg1, dlb1, dgt1 = _outproj_bwd(dxo, x1, br1, y1, ug1, gt1, lg[1], wout[1], "outproj_bwd1")
    gbuf1 = _outproj_bwd_w(y1, ug1, dbr1, gbuf1, row_wout, "outproj_bwd_w1")
    dd1, gbuf1, dscale = _pool_mm_bwd(dy1, d1, wp, scale_f, gbuf1, row_tail)
    du1 = _pool_map(dd1, nb, True, False, "pool_bwd")
    dx1, dsc1, dsh1 = _inproj_bwd_x([du1, dg1], x1, dxres1, sc1, win[1], "inproj_bwd_x1")
    gbuf1 = _inproj_bwd_w(x1, sc1, sh1, [du1, dg1], None, gbuf1, "inproj_bwd_w1")
    pair1, own1 = _pair_add(gbuf1, _sibling_swap(gbuf1), place)
    g_send, g_recv, pair1, parts1, g_token = _push_start([pair1], [jnp.zeros_like(pair1)], True, "rs_exchange1_start")

    dy0, dg0, dxres0, dbr0, dlg0, dlb0, dgt0 = _outproj_bwd(dx1, xs, br0, y0, ug0, gt0 + g_token[0:1, 0:1], lg[0], wout[0],
                                                            "outproj_bwd0")
    gbuf0 = _outproj_bwd_w(y0, ug0, dbr0, gbuf0, row_wout, "outproj_bwd_w0")
    du0, duc, dconv_w, dconv_b, dwa, dwx, dba, dbx, dlam = _rglru_bwd(ug0, uc0, dy0, *lru_args)
    grad_x, dsc0, dsh0 = _inproj_bwd_x([du0, dg0], xs, dxres0, sc0, win[0], "inproj_bwd_x0")
    dscc, dshc = _inproj_bwd_x([duc], cx, None, scc, win[0][:2], "inproj_bwd_x_ctx")
    dwin0c = _inproj_bwd_w(cx, scc, shc, [duc, jnp.zeros_like(duc)], None, None, "inproj_bwd_w_ctx")
    gbuf0 = _inproj_bwd_w(xs, sc0, sh0, [du0, dg0], dwin0c, gbuf0, "inproj_bwd_w0")

    def quarter(dw):
        t = dw.reshape(2, N_CHIPS, nb // N_CHIPS, LANE, LANE)
        return jnp.transpose(t, (1, 3, 0, 2, 4)).reshape(N_CHIPS, LANE, 2 * (nb // N_CHIPS) * LANE).reshape(N_CHIPS, wq, width)

    tail0 = jnp.concatenate([quarter(dwa), quarter(dwx)], axis=1)
    gbuf0 = lax.dynamic_update_slice(gbuf0, tail0, (0, row_tail, 0))
    (parts1,) = _push_wait(g_send, g_recv, pair1, parts1, True, gbuf0, "rs_exchange1_wait")
    red1 = _sibling_gather(_sum_chips(parts1, own1, place))
    red0 = _reduce_scatter(gbuf0, place)
    (rep,) = _gather_chips([red0], "gather_replicated", row0=row_tail, rows=2 * wq)

    k0 = VEC_KINDS
    vec = _rows_kernel(
        [(c, 0, 0), (dsh0, 1, 0), (dsc0, 1, d), (dgt0, 1, 2 * d), (dshc, 2, 0), (dscc, 2, d),
         (dsh1, 3, 0), (dsc1, 3, d), (dgt1, 3, 2 * d),
         (dconv_b, k0, 0), (dlg0, k0, e), (dscale, k0 + 1, 0), (dlg1, k0 + 1, e), (dlb0, k0 + 2, 0), (dlb1, k0 + 2, d),
         (dconv_w, k0 + 3, 0), (dba, k0 + 7, 0), (dbx, k0 + 9, 0), (dlam, k0 + 11, 0)], VEC_ROWS, 3 * d, "pack_vec")
    gt_all = jnp.swapaxes(_gather_devices(vec), 0, 1)
    g_wmod = _mod_bwd_shard(gt_all, cctx2, place, c3)
    g_bmod, g_cctx, g_small = _mod_bwd_rep(gt_all, cctx2, wm_all)

    tmw = _row_tile(d, 256)
    red_src = lambda red, r0, tm: (red, (tm, width), lambda n, i: (r0 // tm + i, 0))
    by_layer = lambda n, gs: jnp.where(n == 0, gs[0], gs[1])
    outs = {}
    outs["w_in"] = _adamw_param(w_in, m_w_in, v_w_in, [red_src(red0, 0, tmw), red_src(red1, 0, tmw)], by_layer, tmw, "adamw_w_in")
    outs["w_out"] = _adamw_param(w_out, m_w_out, v_w_out, [red_src(red0, row_wout, tmw), red_src(red1, row_wout, tmw)],
                                 by_layer, tmw, "adamw_w_out")
    outs["w_mod"] = _adamw_param(w_mod, m_w_mod, v_w_mod, [(g_wmod, (None, tmw, c3), lambda n, i: (n, i, 0))],
                                 lambda n, gs: gs[0], tmw, "adamw_w_mod")
    pw = [a.reshape(ng, pq, pg) for a in (pool_w, m_pool_w, v_pool_w)]
    outs["pool_w"] = [o.reshape(pool_w.shape) for o in _adamw_param(
        *pw, [(red1, (pq, pg), lambda n, i: (row_tail // pq + n // 2, n % 2))], lambda n, gs: gs[0], pq, "adamw_pool_w")]
    bq = nb // N_CHIPS
    rep_src = lambda r0: (rep, (None, LANE, LANE), lambda n, i: ((n % nb) // bq, r0 // LANE, (n // nb) * bq + n % bq))
    for name, r0, trio in (("lru_wa", 0, (lru_wa, m_lru_wa, v_lru_wa)), ("lru_wx", wq, (lru_wx, m_lru_wx, v_lru_wx))):
        blocks = [a.reshape(2 * nb, LANE, LANE) for a in trio]
        outs[name] = [o.reshape(lru_wa.shape) for o in _adamw_param(*blocks, [rep_src(r0)], lambda n, gs: gs[0], LANE,
                                                                    "adamw_" + name)]

    g_small = dict(g_small, c_ctx=g_cctx, b_mod=g_bmod)
    for n in SMALL_GATHERED:
        g_small[n] = lax.dynamic_slice_in_dim(g_small[n], place[1] * eq, eq, axis=1)
    as2d = lambda a: a.reshape(-1, a.shape[-1])
    quads = [(as2d(weights[n]), g_small[n], as2d(mom1[n]), as2d(mom2[n])) for n in SMALL_UPDATED]
    for n, (q, res) in zip(SMALL_UPDATED, zip(quads, _adamw_small(quads))):
        outs[n] = [a.reshape(weights[n].shape) for a in (q[1],) + res]

    result = [loss, grad_x[None]]
    for j in range(4):
        result += [outs[n][j] for n in WEIGHTS]
    return tuple(result)
```

```python
import jax
import jax.numpy as jnp
from jax import lax
from jax.experimental import pallas as pl
from jax.experimental.pallas import tpu as pltpu

F32 = jnp.float32
BF16 = jnp.bfloat16
LANE = 128
SUB = 8
GRID_W = 64
POOL_WINDOWS = (2, 4, 8, 16)
LRU_C = 8.0
DEPTH = 2
ALPHA = float((2 * DEPTH) ** 0.25)
LN_EPS = 1e-5
ADAM_LR, ADAM_B1, ADAM_B2, ADAM_EPS, ADAM_WD, ADAM_STEP = 0.001, 0.9, 0.999, 1e-08, 0.01, 10
N_CHIPS = 4
N_DEV = 8
MESH = pl.DeviceIdType.MESH
ROW_TILE = 512
GATE_TILE = 512
GATE_UNROLL = 1
CONV_TAPS = 4
CONV_LEFT = 2
PAD = 8
SCAN_UNROLL = 8
RS_TILE = 128
POOL_CPAD = 16
VEC_KINDS = 4


def _call(body, **kw):
    return pl.pallas_call(body, **kw)


def _dot(a, b):
    return jnp.dot(a, b, preferred_element_type=F32)


def _dot_nt(a, b):
    return lax.dot_general(a, b, (((1,), (1,)), ((), ())), preferred_element_type=F32)


def _dot_tn(a, b):
    return lax.dot_general(a, b, (((0,), (0,)), ((), ())), preferred_element_type=F32)


def _sigmoid(v):
    return 0.5 * (jnp.tanh(0.5 * v) + 1.0)


def _silu(v):
    return v * _sigmoid(v)


def _dsilu(v):
    s = _sigmoid(v)
    return s * (1.0 + v * (1.0 - s))


def _log_sigmoid(v):
    z = jnp.exp(-jnp.abs(v))
    return jnp.minimum(v, 0.0) - jnp.where(z < 1e-4, z * (1.0 - 0.5 * z), jnp.log(1.0 + z))


def _one_minus_sq(la, a):
    return -jnp.tanh(la) * (a * a + 1.0)


def _cat(ref, n):
    return jnp.concatenate([ref[k] for k in range(n)], axis=1)


def _put_chunks(ref, val, n, base=0):
    for k in range(n):
        ref[base + k] = val[:, k * LANE:(k + 1) * LANE].astype(ref.dtype)


def _row_tile(rows, want):
    t = min(rows, want)
    assert rows % t == 0
    return t


ANY_SPEC = pl.BlockSpec(memory_space=pl.ANY)


def _mod_fwd(cvec, cctx, wm, bm):
    ns, nl, d, c3 = wm.shape

    def body(c_ref, cx_ref, w_ref, b_ref, o_ref):
        cc = jnp.concatenate([c_ref[...], cx_ref[...], jnp.zeros((SUB - 2, d), F32)], axis=0)
        o_ref[...] = _dot(_silu(cc).astype(BF16), w_ref[...]) + b_ref[...]

    return _call(
        body, name="mod_fwd", grid=(nl, ns),
        in_specs=[pl.BlockSpec((1, d), lambda l, s: (0, 0)),
                  pl.BlockSpec((1, d), lambda l, s: (0, 0)),
                  pl.BlockSpec((None, None, d, c3), lambda l, s: (s, l, 0, 0)),
                  pl.BlockSpec((None, 1, c3), lambda l, s: (l, 0, s))],
        out_specs=pl.BlockSpec((None, 8, c3), lambda l, s: (l, 0, s)),
        out_shape=jax.ShapeDtypeStruct((nl, 8, ns * c3), F32),
    )(cvec, cctx, wm, bm)


def _rows_kernel(parts, rows, cols, name):
    def body(*refs):
        o_ref = refs[-1]
        o_ref[...] = jnp.zeros_like(o_ref)
        for ref, (a, r0, c0) in zip(refs[:-1], parts):
            for k in range(a.shape[0]):
                o_ref[r0 + k:r0 + k + 1, c0:c0 + a.shape[1]] = ref[k:k + 1, :]

    return _call(body, name=name, grid=(1,),
                 in_specs=[pl.BlockSpec(a.shape, lambda i: (0, 0)) for a, _, _ in parts],
                 out_specs=pl.BlockSpec((rows, cols), lambda i: (0, 0)),
                 out_shape=jax.ShapeDtypeStruct((rows, cols), F32))(*[a for a, _, _ in parts])


def _mod_bwd_shard(gt, cctx, place, c3):
    d = cctx.shape[1]

    def body(p_ref, cs_ref, dm_ref, dmx_ref, cx_ref, o_ref):
        l = pl.program_id(0)
        lhs = jnp.concatenate([_silu(cs_ref[...]), _silu(cx_ref[...]), jnp.zeros((7, d), F32)], axis=0).astype(BF16)
        dmx = jnp.where(l == 0, jnp.sum(dmx_ref[...], axis=0, keepdims=True), 0.0)
        rhs = jnp.concatenate([dm_ref[...], dmx, jnp.zeros((7, c3), F32)], axis=0).astype(BF16)
        o_ref[...] = _dot_tn(lhs, rhs)

    return _call(
        body, name="mod_bwd_shard",
        grid_spec=pltpu.PrefetchScalarGridSpec(
            num_scalar_prefetch=1, grid=(DEPTH,),
            in_specs=[pl.BlockSpec((None, N_DEV, d), lambda l, p: (0, 0, 0)),
                      pl.BlockSpec((None, N_DEV, c3), lambda l, p: (1 + 2 * l, 0, p[1])),
                      pl.BlockSpec((None, N_DEV, c3), lambda l, p: (2, 0, p[1])),
                      pl.BlockSpec((1, d), lambda l, p: (0, 0))],
            out_specs=pl.BlockSpec((None, d, c3), lambda l, p: (l, 0, 0))),
        out_shape=jax.ShapeDtypeStruct((DEPTH, d, c3), F32),
    )(place, gt, gt, gt, cctx)


def _small_layout(d, e):
    k = VEC_KINDS
    return {
        "conv_b": ((1, e), [(0, k, 0)]),
        "ln_g": ((2, d), [(0, k, e), (1, k + 1, e)]),
        "pool_scale": ((1, e), [(0, k + 1, 0)]),
        "ln_b": ((2, d), [(0, k + 2, 0), (1, k + 2, d)]),
        "conv_w": ((CONV_TAPS, e), [(t, k + 3 + t, 0) for t in range(CONV_TAPS)]),
        "lru_ba": ((2, e), [(j, k + 7 + j, 0) for j in range(2)]),
        "lru_bx": ((2, e), [(j, k + 9 + j, 0) for j in range(2)]),
        "lru_lam": ((2, e), [(j, k + 11 + j, 0) for j in range(2)]),
    }


VEC_ROWS = 24


def _mod_bwd_rep(gt, cctx, wm):
    ns, _, d, c3 = wm.shape
    layout = _small_layout(d, ns * c3 - d)
    names = list(layout)

    def body(g_ref, cx_ref, w_ref, db_ref, dc_ref, *small_refs):
        dm0 = jnp.sum(g_ref[1], axis=0, keepdims=True)
        dmx = jnp.sum(g_ref[2], axis=0, keepdims=True)
        dm1 = jnp.sum(g_ref[3], axis=0, keepdims=True)
        db_ref[0:1, :] = dm0 + dmx
        db_ref[1:2, :] = dm1
        dmxb = jnp.broadcast_to(dmx, (SUB, ns * c3)).astype(BF16)
        acc = jnp.zeros((SUB, d), F32)
        for s in range(ns):
            acc = acc + _dot_nt(dmxb[:, s * c3:(s + 1) * c3], w_ref[s])
        dc_ref[...] = acc[0:1, :] * _dsilu(cx_ref[...])
        for ref, name in zip(small_refs, names):
            shape, places = layout[name]
            for arr_row, vec_row, col0 in places:
                total = jnp.sum(g_ref[vec_row], axis=0, keepdims=True)
                ref[arr_row:arr_row + 1, :] = total[:, col0:col0 + shape[1]]

    outs = _call(
        body, name="mod_bwd_rep", grid=(1,),
        in_specs=[pl.BlockSpec(gt.shape, lambda i: (0, 0, 0)),
                  pl.BlockSpec((1, d), lambda i: (0, 0)),
                  pl.BlockSpec((ns, None, d, c3), lambda i: (0, 0, 0, 0))],
        out_specs=[pl.BlockSpec((DEPTH, ns * c3), lambda i: (0, 0)), pl.BlockSpec((1, d), lambda i: (0, 0))]
        + [pl.BlockSpec(layout[n][0], lambda i: (0, 0)) for n in names],
        out_shape=[jax.ShapeDtypeStruct((DEPTH, ns * c3), F32), jax.ShapeDtypeStruct((1, d), F32)]
        + [jax.ShapeDtypeStruct(layout[n][0], F32) for n in names],
    )(gt, cctx, wm)
    return outs[0], outs[1], dict(zip(names, outs[2:]))


def _inproj_fwd(xin, sc1, sh, w, name):
    rows, d = xin.shape
    ns, _, n4 = w.shape
    cpb = n4 // LANE
    tm = _row_tile(rows, 256)

    def body(x_ref, sc_ref, sh_ref, w_ref, o_ref):
        h = (x_ref[...] * sc_ref[...] + sh_ref[...]).astype(BF16)
        for s in range(ns):
            _put_chunks(o_ref, _dot(h, w_ref[s]), cpb, base=s * cpb)

    return _call(
        body, name=name, grid=(rows // tm,),
        in_specs=[pl.BlockSpec((tm, d), lambda i: (i, 0)),
                  pl.BlockSpec((1, d), lambda i: (0, 0)),
                  pl.BlockSpec((1, d), lambda i: (0, 0)),
                  pl.BlockSpec((ns, d, n4), lambda i: (0, 0, 0))],
        out_specs=pl.BlockSpec((ns * cpb, tm, LANE), lambda i: (0, i, 0)),
        out_shape=jax.ShapeDtypeStruct((ns * cpb, rows, LANE), F32),
    )(xin, sc1, sh, w)


def _inproj_bwd_x(dparts, xin, dxres, sc1, w, name):
    rows, d = xin.shape
    npart = len(dparts)
    e = dparts[0].shape[1]
    ns, _, n4 = w.shape
    per = e // n4
    assert per * npart == ns
    tm = _row_tile(rows, 256)
    has_res = dxres is not None

    def body(*refs):
        dp = refs[:npart]
        x_ref, sc_ref, w_ref = refs[npart:npart + 3]
        rest = refs[npart + 3:]
        if has_res:
            res_ref, dx_ref, dsc_ref, dsh_ref = rest
        else:
            dsc_ref, dsh_ref = rest
        i = pl.program_id(0)
        dh = jnp.zeros((tm, d), F32)
        for p in range(npart):
            v = dp[p][...]
            for q in range(per):
                dh = dh + _dot_nt(v[:, q * n4:(q + 1) * n4], w_ref[p * per + q])

        @pl.when(i == 0)
        def _():
            dsc_ref[...] = jnp.zeros_like(dsc_ref)
            dsh_ref[...] = jnp.zeros_like(dsh_ref)

        dsc_ref[...] += jnp.sum(dh * x_ref[...], axis=0, keepdims=True)
        dsh_ref[...] += jnp.sum(dh, axis=0, keepdims=True)
        if has_res:
            dx_ref[...] = res_ref[...] + dh * sc_ref[...]

    row_spec = pl.BlockSpec((tm, d), lambda i: (i, 0))
    vec_spec = pl.BlockSpec((1, d), lambda i: (0, 0))
    in_specs = [pl.BlockSpec((tm, e), lambda i: (i, 0))] * npart + [row_spec, vec_spec,
                                                                     pl.BlockSpec((ns, d, n4), lambda i: (0, 0, 0))]
    args = list(dparts) + [xin, sc1, w]
    out_specs, out_shape = [vec_spec, vec_spec], [jax.ShapeDtypeStruct((1, d), F32)] * 2
    if has_res:
        in_specs.append(row_spec)
        args.append(dxres)
        out_specs = [row_spec] + out_specs
        out_shape = [jax.ShapeDtypeStruct((rows, d), F32)] + out_shape
    return _call(body, name=name, grid=(rows // tm,), in_specs=in_specs, out_specs=out_specs, out_shape=out_shape)(*args)


def _inproj_bwd_w(xin, sc1, sh, dparts, init, gbuf, name):
    rows, d = xin.shape
    npart = len(dparts)
    e = dparts[0].shape[1]
    n4 = e // 2
    ns = 2 * npart
    tm = _row_tile(rows, 512)
    nt = rows // tm
    has_init = init is not None
    into = gbuf is not None
    assert not into or (ns == N_CHIPS and gbuf.shape[2] == n4)

    def body(*refs):
        x_ref, sc_ref, sh_ref = refs[:3]
        dp = refs[3:3 + npart]
        init_ref = refs[3 + npart] if has_init else None
        o_ref = refs[-1]
        s, i = pl.program_id(0), pl.program_id(1)
        h = (x_ref[...] * sc_ref[...] + sh_ref[...]).astype(BF16)

        @pl.when(i == 0)
        def _():
            o_ref[...] = init_ref[...] if has_init else jnp.zeros_like(o_ref)

        for p in range(npart):
            @pl.when(s // 2 == p)
            def _(p=p):
                o_ref[...] += _dot_tn(h, dp[p][...])

    in_specs = [pl.BlockSpec((tm, d), lambda s, i: (i, 0)),
                pl.BlockSpec((1, d), lambda s, i: (0, 0)),
                pl.BlockSpec((1, d), lambda s, i: (0, 0))]
    in_specs += [pl.BlockSpec((tm, n4), lambda s, i: (i, s % 2))] * npart
    args = [xin, sc1, sh] + list(dparts)
    o_spec = pl.BlockSpec((None, d, n4), lambda s, i: (s, 0, 0))
    if has_init:
        in_specs.append(o_spec)
        args.append(init)
    extra = {}
    if into:
        in_specs.append(ANY_SPEC)
        args.append(gbuf)
        extra = dict(input_output_aliases={len(args) - 1: 0})
    out_shape = jax.ShapeDtypeStruct(gbuf.shape if into else (ns, d, n4), F32)
    return _call(body, name=name, grid=(ns, nt), in_specs=in_specs, out_specs=o_spec, out_shape=out_shape, **extra)(*args)


def _ln_stats(r):
    mu = jnp.mean(r, axis=-1, keepdims=True)
    var = jnp.mean(jnp.square(r - mu), axis=-1, keepdims=True)
    rstd = lax.rsqrt(var + LN_EPS)
    return (r - mu) * rstd, rstd


def _outproj_fwd(y, ug, xin, gt, wout, lg, lb, target, name):
    nch, rows, _ = y.shape
    e, d = wout.shape
    tm = _row_tile(rows, 256)
    with_loss = target is not None

    def body(*refs):
        y_ref, g_ref, x_ref, gt_ref, w_ref, lg_ref, lb_ref = refs[:7]
        if with_loss:
            t_ref, br_ref, dxo_ref, loss_ref = refs[7:]
        else:
            br_ref, xo_ref = refs[7:]
        z = jnp.concatenate([(y_ref[k] * _silu(g_ref[k])).astype(BF16) for k in range(nch)], axis=1)
        br = _dot(z, w_ref[...])
        br_ref[...] = br
        xhat, _ = _ln_stats(ALPHA * x_ref[...] + gt_ref[...] * br)
        xo = xhat * lg_ref[...] + lb_ref[...]
        if with_loss:
            err = xo - t_ref[...]
            dxo_ref[...] = err * (1.0 / d)

            @pl.when(pl.program_id(0) == 0)
            def _():
                loss_ref[...] = jnp.zeros_like(loss_ref)

            loss_ref[...] += jnp.sum(err * err)
        else:
            xo_ref[...] = xo

    chunk_spec = pl.BlockSpec((nch, tm, LANE), lambda i: (0, i, 0))
    g_spec = pl.BlockSpec((nch, tm, LANE), lambda i: (1, i, 0))
    row_spec = pl.BlockSpec((tm, d), lambda i: (i, 0))
    vec_spec = pl.BlockSpec((1, d), lambda i: (0, 0))
    in_specs = [chunk_spec, g_spec, row_spec, vec_spec, pl.BlockSpec((e, d), lambda i: (0, 0)), vec_spec, vec_spec]
    args = [y, ug, xin, gt, wout, lg, lb]
    out_specs = [row_spec, row_spec]
    out_shape = [jax.ShapeDtypeStruct((rows, d), F32)] * 2
    if with_loss:
        in_specs.append(row_spec)
        args.append(target)
        out_specs.append(pl.BlockSpec((1, LANE), lambda i: (0, 0)))
        out_shape.append(jax.ShapeDtypeStruct((1, LANE), F32))
    return _call(body, name=name, grid=(rows // tm,), in_specs=in_specs, out_specs=out_specs, out_shape=out_shape)(*args)


def _outproj_bwd(dxo, xin, br, y, ug, gt, lg, wout, name):
    nch, rows, _ = y.shape
    e, d = wout.shape
    tm = _row_tile(rows, 256)

    def body(dxo_ref, x_ref, br_ref, y_ref, g_ref, gt_ref, lg_ref, w_ref,
             dy_ref, dg_ref, dxres_ref, dbr_ref, dlg_ref, dlb_ref, dgt_ref):
        dxo_v = dxo_ref[...]
        brv = br_ref[...]
        xhat, rstd = _ln_stats(ALPHA * x_ref[...] + gt_ref[...] * brv)
        dxh = dxo_v * lg_ref[...]
        dr = rstd * (dxh - jnp.mean(dxh, axis=-1, keepdims=True) - xhat * jnp.mean(dxh * xhat, axis=-1, keepdims=True))

        @pl.when(pl.program_id(0) == 0)
        def _():
            dlg_ref[...] = jnp.zeros_like(dlg_ref)
            dlb_ref[...] = jnp.zeros_like(dlb_ref)
            dgt_ref[...] = jnp.zeros_like(dgt_ref)

        dlg_ref[...] += jnp.sum(dxo_v * xhat, axis=0, keepdims=True)
        dlb_ref[...] += jnp.sum(dxo_v, axis=0, keepdims=True)
        dgt_ref[...] += jnp.sum(dr * brv, axis=0, keepdims=True)
        dxres_ref[...] = ALPHA * dr
        dbr = (gt_ref[...] * dr).astype(BF16)
        dbr_ref[...] = dbr
        dz = _dot_nt(dbr, w_ref[...])
        for k in range(nch):
            dzk = dz[:, k * LANE:(k + 1) * LANE]
            gk = g_ref[k]
            dy_ref[k] = dzk * _silu(gk)
            dg_ref[:, k * LANE:(k + 1) * LANE] = (dzk * y_ref[k] * _dsilu(gk)).astype(BF16)

    chunk_spec = pl.BlockSpec((nch, tm, LANE), lambda i: (0, i, 0))
    g_spec = pl.BlockSpec((nch, tm, LANE), lambda i: (1, i, 0))
    row_spec = pl.BlockSpec((tm, d), lambda i: (i, 0))
    vec_spec = pl.BlockSpec((1, d), lambda i: (0, 0))
    return _call(
        body, name=name, grid=(rows // tm,),
        in_specs=[row_spec, row_spec, row_spec, chunk_spec, g_spec, vec_spec, vec_spec, pl.BlockSpec((e, d), lambda i: (0, 0))],
        out_specs=[chunk_spec, pl.BlockSpec((tm, e), lambda i: (i, 0)), row_spec, row_spec, vec_spec, vec_spec, vec_spec],
        out_shape=[jax.ShapeDtypeStruct((nch, rows, LANE), F32), jax.ShapeDtypeStruct((rows, e), BF16),
                   jax.ShapeDtypeStruct((rows, d), F32), jax.ShapeDtypeStruct((rows, d), BF16)]
        + [jax.ShapeDtypeStruct((1, d), F32)] * 3,
    )(dxo, xin, br, y, ug, gt, lg, wout)


def _outproj_bwd_w(y, ug, dbr, gbuf, row0, name):
    nch, rows, _ = y.shape
    d = dbr.shape[1]
    e = nch * LANE
    es = e // N_CHIPS
    tm = _row_tile(rows, 512)
    assert gbuf.shape[2] == d and row0 % es == 0

    def body(y_ref, g_ref, dbr_ref, buf_ref, o_ref):
        @pl.when(pl.program_id(0) == 0)
        def _():
            o_ref[...] = jnp.zeros_like(o_ref)

        z = jnp.concatenate([(y_ref[k] * _silu(g_ref[k])).astype(BF16) for k in range(nch)], axis=1)
        o_ref[...] += _dot_tn(z, dbr_ref[...]).reshape(N_CHIPS, es, d)

    return _call(
        body, name=name, grid=(rows // tm,),
        in_specs=[pl.BlockSpec((nch, tm, LANE), lambda i: (0, i, 0)),
                  pl.BlockSpec((nch, tm, LANE), lambda i: (1, i, 0)),
                  pl.BlockSpec((tm, d), lambda i: (i, 0)),
                  ANY_SPEC],
        out_specs=pl.BlockSpec((N_CHIPS, es, d), lambda i: (0, row0 // es, 0)),
        out_shape=jax.ShapeDtypeStruct(gbuf.shape, F32),
        input_output_aliases={3: 0},
    )(y, ug, dbr, gbuf)


def _scan(a_ref, b_ref, h_ref, *, length, init, reverse, a_shift, store):
    nblk = length // SUB
    assert nblk % SCAN_UNROLL == 0
    row = lax.broadcasted_iota(jnp.int32, (SUB, LANE), 0)
    last = 0 if reverse else SUB - 1
    edges = [(row >= SUB - k) if reverse else (row < k) for k in (1, 2, 4)]

    def local_scan(a, b):
        for k, edge in zip((1, 2, 4), edges):
            sh = (SUB - k) if reverse else k
            b = b + a * jnp.where(edge, 0.0, pltpu.roll(b, sh, 0))
            a = a * jnp.where(edge, 1.0, pltpu.roll(a, sh, 0))
        return a, b

    def step(i, carry):
        base = pl.multiple_of(((nblk // SCAN_UNROLL - 1 - i) if reverse else i) * (SCAN_UNROLL * SUB), SCAN_UNROLL * SUB)
        order = range(SCAN_UNROLL - 1, -1, -1) if reverse else range(SCAN_UNROLL)
        loaded = [(a_ref[pl.ds(PAD + base + j * SUB + a_shift, SUB), :], b_ref[pl.ds(PAD + base + j * SUB, SUB), :])
                  for j in order]
        scanned = [local_scan(a, b) for a, b in loaded]
        for j, (a, b) in zip(order, scanned):
            if store:
                h_ref[pl.ds(PAD + base + j * SUB, SUB), :] = b + a * carry
            a_l = jnp.broadcast_to(a[last:last + 1, :], (SUB, LANE))
            b_l = jnp.broadcast_to(b[last:last + 1, :], (SUB, LANE))
            carry = b_l + a_l * carry
        return carry

    carry = lax.fori_loop(0, nblk // SCAN_UNROLL, step, jnp.broadcast_to(init, (SUB, LANE)))
    return carry[0:1, :]


def _conv_fwd(src_ref, upad, u_ref, cw, cb, length):
    zeros = jnp.zeros((PAD, LANE), F32)
    upad[pl.ds(0, PAD), :] = zeros
    upad[pl.ds(PAD + length, PAD), :] = zeros
    rt = _row_tile(length, ROW_TILE)

    def copy(i, c):
        t0 = pl.multiple_of(i * rt, rt)
        upad[pl.ds(PAD + t0, rt), :] = src_ref[pl.ds(t0, rt), :]
        return c

    lax.fori_loop(0, length // rt, copy, 0)

    def tile(i, c):
        t0 = pl.multiple_of(i * rt, rt)
        acc = jnp.zeros((rt, LANE), F32)
        for k in range(CONV_TAPS):
            acc = acc + upad[pl.ds(t0 + PAD - CONV_LEFT + k, rt), :] * cw[k:k + 1, :]
        u_ref[pl.ds(t0, rt), :] = acc + cb
        return c

    lax.fori_loop(0, length // rt, tile, 0)


def _gates_fwd(u_ref, a_ref, b_ref, wa, wx, ba, bx, ls, length):
    rt = _row_tile(length, GATE_TILE)

    def tile(i, c):
        t0 = pl.multiple_of(i * rt, rt)
        ut = u_ref[pl.ds(t0, rt), :]
        ub = ut.astype(BF16)
        r = _sigmoid(_dot(ub, wa) + ba)
        ig = _sigmoid(_dot(ub, wx) + bx)
        la = (LRU_C * r) * ls
        a = jnp.exp(la)
        a_ref[pl.ds(PAD + t0, rt), :] = a
        b_ref[pl.ds(PAD + t0, rt), :] = jnp.sqrt(_one_minus_sq(la, a)) * (ig * ut)
        return c

    lax.fori_loop(0, length // rt, tile, 0, unroll=min(GATE_UNROLL, length // rt))


def _lru_specs():
    return [pl.BlockSpec((CONV_TAPS, LANE), lambda n: (0, n)),
            pl.BlockSpec((1, LANE), lambda n: (0, n)),
            pl.BlockSpec((2, None, LANE, LANE), lambda n: (0, n, 0, 0)),
            pl.BlockSpec((2, None, LANE, LANE), lambda n: (0, n, 0, 0)),
            pl.BlockSpec((2, LANE), lambda n: (0, n)),
            pl.BlockSpec((2, LANE), lambda n: (0, n)),
            pl.BlockSpec((2, LANE), lambda n: (0, n))]


def _rglru_fwd(ug, uc, conv_w, conv_b, wa, wx, ba, bx, lam):
    nb = uc.shape[0]
    s_len, t_len = ug.shape[1], uc.shape[1]

    def body(u0_ref, uc0_ref, cw_ref, cb_ref, wa_ref, wx_ref, ba_ref, bx_ref, lam_ref, y_ref,
             upad, ubuf, abuf, hbuf):
        cw, cb = cw_ref[...], cb_ref[...]
        lsig = _log_sigmoid(lam_ref[...])
        zero = jnp.zeros((1, LANE), F32)
        _conv_fwd(uc0_ref, upad, ubuf, cw, cb, t_len)
        h0 = []
        for dr in range(2):
            _gates_fwd(ubuf, abuf, hbuf, wa_ref[dr], wx_ref[dr], ba_ref[dr:dr + 1, :], bx_ref[dr:dr + 1, :],
                       lsig[dr:dr + 1, :], t_len)
            h0.append(_scan(abuf, hbuf, hbuf, length=t_len, init=zero, reverse=(dr == 1), a_shift=0, store=False))
        _conv_fwd(u0_ref, upad, ubuf, cw, cb, s_len)
        rt = _row_tile(s_len, ROW_TILE)
        for dr in range(2):
            _gates_fwd(ubuf, abuf, hbuf, wa_ref[dr], wx_ref[dr], ba_ref[dr:dr + 1, :], bx_ref[dr:dr + 1, :],
                       lsig[dr:dr + 1, :], s_len)
            _scan(abuf, hbuf, hbuf, length=s_len, init=h0[dr], reverse=(dr == 1), a_shift=0, store=True)

            def acc(i, c, dr=dr):
                t0 = pl.multiple_of(i * rt, rt)
                h = hbuf[pl.ds(PAD + t0, rt), :]
                if dr == 0:
                    y_ref[pl.ds(t0, rt), :] = h
                else:
                    y_ref[pl.ds(t0, rt), :] += h
                return c

            lax.fori_loop(0, s_len // rt, acc, 0)

    seq = pltpu.VMEM((s_len + 2 * PAD, LANE), F32)
    return _call(
        body, name="rglru_fwd", grid=(nb,),
        in_specs=[pl.BlockSpec((None, s_len, LANE), lambda n: (n, 0, 0)),
                  pl.BlockSpec((None, t_len, LANE), lambda n: (n, 0, 0))] + _lru_specs(),
        out_specs=pl.BlockSpec((None, s_len, LANE), lambda n: (n, 0, 0)),
        out_shape=jax.ShapeDtypeStruct((nb, s_len, LANE), F32),
        scratch_shapes=[seq, pltpu.VMEM((s_len, LANE), F32), seq, seq],
    )(ug, uc, conv_w, conv_b, wa, wx, ba, bx, lam)


def _rglru_bwd(ug, uc, dy, conv_w, conv_b, wa, wx, ba, bx, lam):
    nb = uc.shape[0]
    e = nb * LANE
    s_len, t_len = ug.shape[1], uc.shape[1]

    def body(u0_ref, uc0_ref, dy_ref, cw_ref, cb_ref, wa_ref, wx_ref, ba_ref, bx_ref, lam_ref,
             du_ref, duc_ref, dcw_ref, dcb_ref, dwa_ref, dwx_ref, dba_ref, dbx_ref, dlam_ref,
             upad, ubuf, abuf, hbuf, lbuf, dubuf, cpad, cu, ca0, ch0, ca1, ch1):
        cw, cb = cw_ref[...], cb_ref[...]
        lam_v = lam_ref[...]
        lsig = _log_sigmoid(lam_v)
        zero = jnp.zeros((1, LANE), F32)
        zpad = jnp.zeros((PAD, LANE), F32)
        for ref in (dcw_ref, dcb_ref, dwa_ref, dwx_ref, dba_ref, dbx_ref, dlam_ref):
            ref[...] = jnp.zeros_like(ref)

        def params(dr):
            return (wa_ref[dr], wx_ref[dr], ba_ref[dr:dr + 1, :], bx_ref[dr:dr + 1, :], lsig[dr:dr + 1, :])

        def direction_bwd(dr, u_ref, a_ref, h_ref, l_ref, dub, length, first):
            wa_d, wx_d, ba_d, bx_d, ls_d = params(dr)
            rt = _row_tile(length, GATE_TILE)
            prev = 1 if dr == 1 else -1

            def tile(i, c):
                t0 = pl.multiple_of(i * rt, rt)
                ut = u_ref[pl.ds(t0, rt), :]
                ub = ut.astype(BF16)
                r = _sigmoid(_dot(ub, wa_d) + ba_d)
                ig = _sigmoid(_dot(ub, wx_d) + bx_d)
                la = (LRU_C * r) * ls_d
                a = a_ref[pl.ds(PAD + t0, rt), :]
                q = _one_minus_sq(la, a)
                rs = lax.rsqrt(q)
                sq = q * rs
                lm = l_ref[pl.ds(PAD + t0, rt), :]
                da = lm * h_ref[pl.ds(PAD + t0 + prev, rt), :]
                dsq = lm * ig * ut
                dig = lm * sq * ut
                dla = da * a - dsq * (a * a) * rs
                dr_ = dla * (LRU_C * ls_d)
                dlam_ref[dr:dr + 1, :] += jnp.sum(dla * (LRU_C * r), axis=0, keepdims=True)
                dpr = dr_ * r * (1.0 - r)
                dpi = dig * ig * (1.0 - ig)
                dba_ref[dr:dr + 1, :] += jnp.sum(dpr, axis=0, keepdims=True)
                dbx_ref[dr:dr + 1, :] += jnp.sum(dpi, axis=0, keepdims=True)
                dprb, dpib = dpr.astype(BF16), dpi.astype(BF16)
                dwa_ref[dr] += _dot_tn(ub, dprb)
                dwx_ref[dr] += _dot_tn(ub, dpib)
                dut = lm * sq * ig + _dot_nt(dprb, wa_d) + _dot_nt(dpib, wx_d)
                if first:
                    dub[pl.ds(PAD + t0, rt), :] = dut
                else:
                    dub[pl.ds(PAD + t0, rt), :] += dut
                return c

            lax.fori_loop(0, length // rt, tile, 0, unroll=min(GATE_UNROLL, length // rt))

        def conv_bwd(dub, src_pad, out_ref, length):
            rt = _row_tile(length, ROW_TILE)

            def tile(i, c):
                t0 = pl.multiple_of(i * rt, rt)
                dut = dub[pl.ds(PAD + t0, rt), :]
                dcb_ref[...] += jnp.sum(dut, axis=0, keepdims=True)
                acc = jnp.zeros((rt, LANE), F32)
                for k in range(CONV_TAPS):
                    sh = CONV_LEFT - k
                    acc = acc + dub[pl.ds(PAD + t0 + sh, rt), :] * cw[k:k + 1, :]
                    dcw_ref[k:k + 1, :] += jnp.sum(dut * src_pad[pl.ds(PAD + t0 - sh, rt), :], axis=0, keepdims=True)
                out_ref[pl.ds(t0, rt), :] = acc.astype(out_ref.dtype)
                return c

            lax.fori_loop(0, length // rt, tile, 0)

        _conv_fwd(uc0_ref, cpad, cu, cw, cb, t_len)
        cbufs = ((ca0, ch0), (ca1, ch1))
        h0 = []
        for dr in range(2):
            ca, chh = cbufs[dr]
            _gates_fwd(cu, ca, chh, *params(dr), t_len)
            h0.append(_scan(ca, chh, chh, length=t_len, init=zero, reverse=(dr == 1), a_shift=0, store=True))
        _conv_fwd(u0_ref, upad, ubuf, cw, cb, s_len)
        rt = _row_tile(s_len, ROW_TILE)
        dh0 = []
        for dr in range(2):
            rev = dr == 1
            _gates_fwd(ubuf, abuf, hbuf, *params(dr), s_len)
            _scan(abuf, hbuf, hbuf, length=s_len, init=h0[dr], reverse=rev, a_shift=0, store=True)
            first_row = PAD + s_len if rev else PAD - 1
            hbuf[pl.ds(first_row, 1), :] = h0[dr]
            end_row = PAD - 1 if rev else PAD + s_len
            abuf[pl.ds(end_row, 1), :] = zero

            def copy(i, c):
                t0 = pl.multiple_of(i * rt, rt)
                lbuf[pl.ds(PAD + t0, rt), :] = dy_ref[pl.ds(t0, rt), :]
                return c

            lax.fori_loop(0, s_len // rt, copy, 0)
            _scan(abuf, lbuf, lbuf, length=s_len, init=zero, reverse=not rev, a_shift=(-1 if rev else 1), store=True)
            start = PAD + s_len - 1 if rev else PAD
            dh0.append(abuf[pl.ds(start, 1), :] * lbuf[pl.ds(start, 1), :])
            direction_bwd(dr, ubuf, abuf, hbuf, lbuf, dubuf, s_len, first=(dr == 0))
        dubuf[pl.ds(0, PAD), :] = zpad
        dubuf[pl.ds(PAD + s_len, PAD), :] = zpad
        conv_bwd(dubuf, upad, du_ref, s_len)
        lc = lbuf
        duc_buf = dubuf
        for dr in range(2):
            rev = dr == 1
            ca, chh = cbufs[dr]
            first_row = PAD + t_len if rev else PAD - 1
            chh[pl.ds(first_row, 1), :] = zero
            end_row = PAD - 1 if rev else PAD + t_len
            ca[pl.ds(end_row, 1), :] = zero + 1.0
            rtc = _row_tile(t_len, ROW_TILE)

            def clear(i, c):
                t0 = pl.multiple_of(i * rtc, rtc)
                lc[pl.ds(PAD + t0, rtc), :] = jnp.zeros((rtc, LANE), F32)
                return c

            lax.fori_loop(0, t_len // rtc, clear, 0)
            _scan(ca, lc, lc, length=t_len, init=dh0[dr], reverse=not rev, a_shift=(-1 if rev else 1), store=True)
            direction_bwd(dr, cu, ca, chh, lc, duc_buf, t_len, first=(dr == 0))
        duc_buf[pl.ds(0, PAD), :] = zpad
        duc_buf[pl.ds(PAD + t_len, PAD), :] = zpad
        conv_bwd(duc_buf, cpad, duc_ref, t_len)
        dlam_ref[...] = dlam_ref[...] * (1.0 - _sigmoid(lam_v))

    seq = pltpu.VMEM((s_len + 2 * PAD, LANE), F32)
    cseq = pltpu.VMEM((t_len + 2 * PAD, LANE), F32)
    vec2 = pl.BlockSpec((2, LANE), lambda n: (0, n))
    wspec = pl.BlockSpec((2, None, LANE, LANE), lambda n: (0, n, 0, 0))
    return _call(
        body, name="rglru_bwd", grid=(nb,),
        in_specs=[pl.BlockSpec((None, s_len, LANE), lambda n: (n, 0, 0)),
                  pl.BlockSpec((None, t_len, LANE), lambda n: (n, 0, 0)),
                  pl.BlockSpec((None, s_len, LANE), lambda n: (n, 0, 0))] + _lru_specs(),
        out_specs=[pl.BlockSpec((s_len, LANE), lambda n: (0, n)),
                   pl.BlockSpec((t_len, LANE), lambda n: (0, n)),
                   pl.BlockSpec((CONV_TAPS, LANE), lambda n: (0, n)),
                   pl.BlockSpec((1, LANE), lambda n: (0, n)),
                   wspec, wspec, vec2, vec2, vec2],
        out_shape=[jax.ShapeDtypeStruct((s_len, e), BF16), jax.ShapeDtypeStruct((t_len, e), BF16),
                   jax.ShapeDtypeStruct((CONV_TAPS, e), F32), jax.ShapeDtypeStruct((1, e), F32),
                   jax.ShapeDtypeStruct((2, nb, LANE, LANE), F32), jax.ShapeDtypeStruct((2, nb, LANE, LANE), F32),
                   jax.ShapeDtypeStruct((2, e), F32), jax.ShapeDtypeStruct((2, e), F32), jax.ShapeDtypeStruct((2, e), F32)],
        scratch_shapes=[seq, pltpu.VMEM((s_len, LANE), F32), seq, seq, seq, seq,
                        cseq, pltpu.VMEM((t_len, LANE), F32), cseq, cseq, cseq, cseq],
    )(ug, uc, dy, conv_w, conv_b, wa, wx, ba, bx, lam)


def _pool_windows(src_ref, out_ref, colbuf, rowbuf, half, transpose, s_len):
    gw = GRID_W
    lg = gw.bit_length() - 1
    n_rows = s_len // gw
    cp, rm = POOL_CPAD, 8 * gw
    stride = gw + 2 * cp
    rt = _row_tile(s_len, ROW_TILE)
    assert rt % gw == 0 and half <= cp
    gpt = rt // gw
    offs = range(-half, half)
    zmargin = jnp.zeros((cp, LANE), F32)

    def zcol(r, c):
        base = pl.multiple_of(r * stride, SUB)
        colbuf[pl.ds(base, cp), :] = zmargin
        colbuf[pl.ds(base + cp + gw, cp), :] = zmargin
        return c

    lax.fori_loop(0, n_rows, zcol, 0)

    def zrow(i, c):
        t0 = pl.multiple_of(i * gw, gw)
        rowbuf[pl.ds(t0, gw), :] = jnp.zeros((gw, LANE), F32)
        rowbuf[pl.ds(rm + s_len + t0, gw), :] = jnp.zeros((gw, LANE), F32)
        return c

    lax.fori_loop(0, rm // gw, zrow, 0)

    col = lax.broadcasted_iota(jnp.int32, (gw, LANE), 0)
    ccnt = (jnp.minimum(col + half, gw) - jnp.maximum(col - half, 0)).astype(F32)

    def row_counts(t0):
        row = (t0 + lax.broadcasted_iota(jnp.int32, (rt, LANE), 0)) >> lg
        return (jnp.minimum(row + half, n_rows) - jnp.maximum(row - half, 0)).astype(F32)

    def col_base(t0, g):
        return pl.multiple_of((t0 // gw) * stride, SUB) + g * stride + cp

    def col_sum(t0, g, sign):
        acc = jnp.zeros((gw, LANE), F32)
        for o in offs:
            acc = acc + colbuf[pl.ds(col_base(t0, g) + sign * o, gw), :]
        return acc

    def row_sum(t0, sign):
        acc = jnp.zeros((rt, LANE), F32)
        for o in offs:
            acc = acc + rowbuf[pl.ds(rm + t0 + sign * o * gw, rt), :]
        return acc

    n_tiles = s_len // rt
    assert rt >= half * gw

    def loop(fn, edges=False):
        def step(i, c):
            t0 = pl.multiple_of(i * rt, rt)
            fn(t0, False) if edges else fn(t0)
            return c
        if edges:
            fn(0, True)
            if n_tiles > 1:
                fn(s_len - rt, True)
            lax.fori_loop(1, n_tiles - 1, step, 0)
        else:
            lax.fori_loop(0, n_tiles, step, 0)

    inv_ccnt = 1.0 / ccnt

    def by_row_count(v, t0, edge):
        return v / row_counts(t0) if edge else v * (1.0 / (2 * half))

    if not transpose:
        def fill(t0):
            for g in range(gpt):
                colbuf[pl.ds(col_base(t0, g), gw), :] = src_ref[pl.ds(t0 + g * gw, gw), :]

        def cols(t0):
            for g in range(gpt):
                rowbuf[pl.ds(rm + t0 + g * gw, gw), :] = col_sum(t0, g, 1) * inv_ccnt

        def rows(t0, edge):
            mean = by_row_count(row_sum(t0, 1), t0, edge)
            out_ref[pl.ds(t0, rt), :] = (mean - src_ref[pl.ds(t0, rt), :]).astype(out_ref.dtype)

        loop(fill)
        loop(cols)
        loop(rows, edges=True)
    else:
        def fill(t0, edge):
            rowbuf[pl.ds(rm + t0, rt), :] = by_row_count(src_ref[pl.ds(t0, rt), :], t0, edge)

        def rows(t0):
            acc = row_sum(t0, -1)
            for g in range(gpt):
                colbuf[pl.ds(col_base(t0, g), gw), :] = acc[g * gw:(g + 1) * gw, :] * inv_ccnt

        def cols(t0):
            for g in range(gpt):
                rows_g = pl.ds(t0 + g * gw, gw)
                out_ref[rows_g, :] = (col_sum(t0, g, -1) - src_ref[rows_g, :]).astype(out_ref.dtype)

        loop(fill, edges=True)
        loop(rows)
        loop(cols)


def _pool_map(src, nb, transpose, out_chunk_major, name):
    s_len = src.shape[1]
    cpg = nb // len(POOL_WINDOWS)

    def body(src_ref, out_ref, colbuf, rowbuf):
        n = pl.program_id(0)
        for gi, w in enumerate(POOL_WINDOWS):
            @pl.when(n // cpg == gi)
            def _(w=w):
                _pool_windows(src_ref, out_ref, colbuf, rowbuf, w // 2, transpose, s_len)

    if out_chunk_major:
        out_spec = pl.BlockSpec((None, s_len, LANE), lambda n: (n, 0, 0))
        out_shape = jax.ShapeDtypeStruct((nb, s_len, LANE), BF16)
    else:
        out_spec = pl.BlockSpec((s_len, LANE), lambda n: (0, n))
        out_shape = jax.ShapeDtypeStruct((s_len, nb * LANE), BF16)
    return _call(
        body, name=name, grid=(nb,),
        in_specs=[pl.BlockSpec((None, s_len, LANE), lambda n: (n, 0, 0))],
        out_specs=out_spec, out_shape=out_shape,
        scratch_shapes=[pltpu.VMEM((s_len // GRID_W * (GRID_W + 2 * POOL_CPAD), LANE), F32),
                        pltpu.VMEM((s_len + 16 * GRID_W, LANE), F32)],
    )(src)


def _group_weight(w_ref):
    return jnp.concatenate([w_ref[k] for k in range(N_CHIPS)], axis=0)


def _pool_mm_fwd(dm, wp, scale):
    nb, rows, _ = dm.shape
    _, ng, pq, pg = wp.shape
    cpg = pg // LANE
    tm = _row_tile(rows, 512)

    def body(d_ref, w_ref, s_ref, y_ref):
        _put_chunks(y_ref, _dot(_cat(d_ref, cpg), _group_weight(w_ref)) * s_ref[...], cpg)

    cspec = pl.BlockSpec((cpg, tm, LANE), lambda i, g: (g, i, 0))
    return _call(
        body, name="pool_mm_fwd", grid=(rows // tm, ng),
        in_specs=[cspec, pl.BlockSpec((N_CHIPS, None, pq, pg), lambda i, g: (0, g, 0, 0)),
                  pl.BlockSpec((1, pg), lambda i, g: (0, g))],
        out_specs=cspec, out_shape=jax.ShapeDtypeStruct((nb, rows, LANE), F32),
    )(dm, wp, scale)


def _pool_mm_bwd(dy, dm, wp, scale, gbuf, row0):
    nb, rows, _ = dm.shape
    _, ng, pq, pg = wp.shape
    cpg = pg // LANE
    tm = _row_tile(rows, 512)
    nt = rows // tm
    assert gbuf.shape[2] == 2 * pg and row0 % pq == 0

    def body(dy_ref, d_ref, w_ref, s_ref, buf_ref, dd_ref, dwp_ref, dsc_ref, acc):
        i = pl.program_id(1)

        @pl.when(i == 0)
        def _():
            acc[...] = jnp.zeros_like(acc)
            dsc_ref[...] = jnp.zeros_like(dsc_ref)

        dyv = _cat(dy_ref, cpg)
        dc = _cat(d_ref, cpg)
        w = _group_weight(w_ref)
        dsc_ref[...] += jnp.sum(dyv * _dot(dc, w), axis=0, keepdims=True)
        dyp = (dyv * s_ref[...]).astype(BF16)
        _put_chunks(dd_ref, _dot_nt(dyp, w), cpg)
        acc[...] += _dot_tn(dc, dyp)

        @pl.when(i == nt - 1)
        def _():
            dwp_ref[...] = acc[...].reshape(N_CHIPS, pq, pg)

    cspec = pl.BlockSpec((cpg, tm, LANE), lambda g, i: (g, i, 0))
    sspec = pl.BlockSpec((1, pg), lambda g, i: (0, g))
    return _call(
        body, name="pool_mm_bwd", grid=(ng, nt),
        in_specs=[cspec, cspec, pl.BlockSpec((N_CHIPS, None, pq, pg), lambda g, i: (0, g, 0, 0)), sspec, ANY_SPEC],
        out_specs=[cspec, pl.BlockSpec((N_CHIPS, pq, pg), lambda g, i: (0, row0 // pq + g // 2, g % 2)), sspec],
        out_shape=[jax.ShapeDtypeStruct((nb, rows, LANE), F32), jax.ShapeDtypeStruct(gbuf.shape, F32),
                   jax.ShapeDtypeStruct((1, ng * pg), F32)],
        scratch_shapes=[pltpu.VMEM((pg, pg), F32)],
        input_output_aliases={4: 1},
    )(dy, dm, wp, scale, gbuf)


def _adamw_math(w, g, m, v):
    nm = ADAM_B1 * m + (1.0 - ADAM_B1) * g
    nv = ADAM_B2 * v + (1.0 - ADAM_B2) * jnp.square(g)
    m_hat = nm / (1.0 - ADAM_B1 ** ADAM_STEP)
    v_hat = nv / (1.0 - ADAM_B2 ** ADAM_STEP)
    return -ADAM_LR * (m_hat / (jnp.sqrt(v_hat) + ADAM_EPS) + ADAM_WD * w), nm, nv


def _adamw_param(w3, m3, v3, gsrcs, pick, tm, name):
    n_blk, rows, cols = w3.shape
    ng = len(gsrcs)

    def body(*refs):
        w_ref, m_ref, v_ref = refs[:3]
        g_refs = refs[3:3 + ng]
        go_ref, d_ref, nm_ref, nv_ref = refs[3 + ng:]
        g = pick(pl.program_id(0), [r[...] for r in g_refs])
        go_ref[...] = g
        d_ref[...], nm_ref[...], nv_ref[...] = _adamw_math(w_ref[...], g, m_ref[...], v_ref[...])

    spec = pl.BlockSpec((None, tm, cols), lambda n, i: (n, i, 0))
    return _call(
        body, name=name, grid=(n_blk, rows // tm),
        in_specs=[spec] * 3 + [pl.BlockSpec(shape, imap) for _, shape, imap in gsrcs],
        out_specs=[spec] * 4, out_shape=[jax.ShapeDtypeStruct(w3.shape, F32)] * 4,
    )(w3, m3, v3, *[a for a, _, _ in gsrcs])


def _adamw_small(quads):
    n = len(quads)

    def body(*refs):
        ins, outs = refs[:4 * n], refs[4 * n:]
        for k in range(n):
            w, g, m, v = (r[...] for r in ins[4 * k:4 * k + 4])
            outs[3 * k][...], outs[3 * k + 1][...], outs[3 * k + 2][...] = _adamw_math(w, g, m, v)

    flat = [a for q in quads for a in q]
    res = _call(body, name="adamw_small", grid=(1,),
                in_specs=[pl.BlockSpec(a.shape, lambda i: (0, 0)) for a in flat],
                out_specs=[pl.BlockSpec(q[0].shape, lambda i: (0, 0)) for q in quads for _ in range(3)],
                out_shape=[jax.ShapeDtypeStruct(q[0].shape, F32) for q in quads for _ in range(3)])(*flat)
    return [tuple(res[3 * k:3 * k + 3]) for k in range(n)]


def _place():
    return lax.axis_index("x"), lax.axis_index("y"), lax.axis_index("c")


def _other_chips(x, y):
    return [(1 - x, y), (x, 1 - y), (1 - x, 1 - y)]


def _gather_chips(arrays, name, row0=0, rows=None):
    n = len(arrays)
    nrows = [a.shape[0] if rows is None else rows for a in arrays]
    halves = [r // 2 for r in nrows]
    for a, r, h in zip(arrays, nrows, halves):
        assert 2 * h == r and h % (32 // a.dtype.itemsize) == 0

    def body(*refs):
        ins, outs = refs[:n], refs[n:2 * n]
        send_sems, recv_sems, local_sems = refs[2 * n:]
        x, y, c = _place()
        me = 2 * x + y
        chips = _other_chips(x, y)

        def mine(k):
            return pl.ds(c * halves[k], halves[k])

        def theirs(k):
            return pl.ds((1 - c) * halves[k], halves[k])

        def push(k, j, src, dst, to):
            return pltpu.make_async_remote_copy(src_ref=src, dst_ref=dst, send_sem=send_sems.at[6 * k + j],
                                                recv_sem=recv_sems.at[6 * k + j], device_id=to, device_id_type=MESH)

        local = [pltpu.make_async_copy(ins[k].at[pl.ds(row0, nrows[k])], outs[k].at[me], local_sems.at[k]) for k in range(n)]
        for cp in local:
            cp.start()
        started = []
        for j, (cx, cy) in enumerate(chips):
            for k in range(n):
                cp = push(k, j, ins[k].at[pl.ds(row0 + c * halves[k], halves[k])], outs[k].at[me, mine(k)], (cx, cy, c))
                cp.start()
                started.append(cp)
        for j, (cx, cy) in enumerate(chips):
            for k in range(n):
                slab = outs[k].at[2 * cx + cy, mine(k)]
                push(k, j, slab, slab, (x, y, c)).wait_recv()
                fwd = push(k, 3 + j, slab, slab, (x, y, 1 - c))
                fwd.start()
                started.append(fwd)
        for j, (cx, cy) in enumerate(chips):
            for k in range(n):
                slab = outs[k].at[2 * cx + cy, theirs(k)]
                push(k, 3 + j, slab, slab, (x, y, c)).wait_recv()
        for cp in started:
            cp.wait_send()
        for cp in local:
            cp.wait()

    return _call(
        body, name=name, in_specs=[ANY_SPEC] * n, out_specs=[ANY_SPEC] * n,
        out_shape=[jax.ShapeDtypeStruct((N_CHIPS, r, a.shape[1]), a.dtype) for a, r in zip(arrays, nrows)],
        scratch_shapes=[pltpu.SemaphoreType.DMA((6 * n,)), pltpu.SemaphoreType.DMA((6 * n,)), pltpu.SemaphoreType.DMA((n,))],
    )(*arrays)


def _gather_devices(v):
    shape = v.shape

    def body(v_ref, out_ref, send_sems, recv_sems):
        x, y, c = _place()
        me = 4 * x + 2 * y + c
        out_ref[me] = v_ref[...]
        sends = []
        for k in range(1, N_DEV):
            to = (me + k) % N_DEV
            cp = pltpu.make_async_remote_copy(src_ref=v_ref, dst_ref=out_ref.at[me], send_sem=send_sems.at[k],
                                              recv_sem=recv_sems.at[k], device_id=(to // 4, (to // 2) % 2, to % 2),
                                              device_id_type=MESH)
            cp.start()
            sends.append(cp)
        for k in range(1, N_DEV):
            frm = (me + N_DEV - k) % N_DEV
            pltpu.make_async_remote_copy(src_ref=v_ref, dst_ref=out_ref.at[frm], send_sem=send_sems.at[k],
                                         recv_sem=recv_sems.at[k], device_id=(x, y, c), device_id_type=MESH).wait_recv()
        for cp in sends:
            cp.wait_send()

    vspec = pl.BlockSpec(memory_space=pltpu.VMEM)
    return _call(body, name="gather_devices", in_specs=[vspec], out_specs=vspec,
                 out_shape=jax.ShapeDtypeStruct((N_DEV,) + shape, F32),
                 scratch_shapes=[pltpu.SemaphoreType.DMA((N_DEV,)), pltpu.SemaphoreType.DMA((N_DEV,))])(v)


HBM_SPEC = pl.BlockSpec(memory_space=pltpu.HBM)
SEM_SPEC = pl.BlockSpec(memory_space=pltpu.SEMAPHORE)
SIDE_EFFECT = pltpu.SideEffectType.DATAFLOW_SIDE_EFFECTING


def _push_copies(src_refs, land_refs, send_sems, recv_sems, per_peer):
    x, y, c = _place()
    me = 2 * x + y
    copies = []
    for j, (cx, cy) in enumerate(_other_chips(x, y)):
        for k, (src, land) in enumerate(zip(src_refs, land_refs)):
            copies.append(pltpu.make_async_remote_copy(
                src_ref=src.at[2 * cx + cy] if per_peer else src, dst_ref=land.at[me], send_sem=send_sems.at[3 * k + j],
                recv_sem=recv_sems.at[3 * k + j], device_id=(cx, cy, c), device_id_type=MESH))
    return copies


def _push_start(srcs, lands, per_peer, after, name):
    n = len(srcs)

    def body(*refs):
        src_refs, land_refs = refs[:n], refs[n:2 * n]
        send_sems, recv_sems = refs[2 * n + 1], refs[2 * n + 2]
        token = refs[-1]
        for cp in _push_copies(src_refs, land_refs, send_sems, recv_sems, per_peer):
            cp.start()
        token[...] = jnp.zeros_like(token)

    bufs = [pltpu.with_memory_space_constraint(a, pltpu.HBM) for a in list(srcs) + list(lands)]
    res = _call(
        body, name=name,
        out_shape=[pltpu.SemaphoreType.DMA((3 * n,)), pltpu.SemaphoreType.DMA((3 * n,))]
        + [pltpu.HBM(a.shape, a.dtype) for a in bufs] + [jax.ShapeDtypeStruct((SUB, LANE), F32)],
        in_specs=[HBM_SPEC] * (2 * n) + [ANY_SPEC],
        out_specs=[SEM_SPEC, SEM_SPEC] + [HBM_SPEC] * (2 * n) + [pl.BlockSpec(memory_space=pltpu.VMEM)],
        input_output_aliases={i: 2 + i for i in range(2 * n)},
        compiler_params=pltpu.CompilerParams(has_side_effects=SIDE_EFFECT),
    )(*bufs, after)
    return res[0], res[1], list(res[2:2 + n]), list(res[2 + n:2 + 2 * n]), res[-1]


def _push_wait(send_sems, recv_sems, srcs, lands, per_peer, after, name):
    n = len(srcs)

    def body(*refs):
        src_refs, land_refs = refs[:n], refs[n:2 * n]
        send_sems, recv_sems = refs[2 * n], refs[2 * n + 1]
        for cp in _push_copies(src_refs, land_refs, send_sems, recv_sems, per_peer):
            cp.wait_send()
            cp.wait_recv()

    res = _call(
        body, name=name,
        out_shape=[pltpu.HBM(a.shape, a.dtype) for a in list(srcs) + list(lands)],
        in_specs=[HBM_SPEC] * (2 * n) + [SEM_SPEC, SEM_SPEC, ANY_SPEC],
        out_specs=[HBM_SPEC] * (2 * n),
        input_output_aliases={i: i for i in range(2 * n)},
        compiler_params=pltpu.CompilerParams(has_side_effects=SIDE_EFFECT),
    )(*srcs, *lands, send_sems, recv_sems, after)
    return list(res[n:])


def _sibling_swap(g):
    _, rows, w = g.shape
    half = rows // 2

    def body(g_ref, out_ref, send_sem, recv_sem):
        x, y, c = _place()
        cp = pltpu.make_async_remote_copy(src_ref=g_ref.at[:, pl.ds((1 - c) * half, half)], dst_ref=out_ref,
                                          send_sem=send_sem, recv_sem=recv_sem, device_id=(x, y, 1 - c), device_id_type=MESH)
        cp.start()
        cp.wait()

    return _call(body, name="rs_sibling_swap", in_specs=[ANY_SPEC], out_specs=ANY_SPEC,
                 out_shape=jax.ShapeDtypeStruct((N_CHIPS, half, w), F32),
                 scratch_shapes=[pltpu.SemaphoreType.DMA, pltpu.SemaphoreType.DMA])(g)


def _pair_add(g, got, place):
    _, rows, w = g.shape
    half = rows // 2
    tm = _row_tile(half, RS_TILE)
    nt = half // tm

    def body(p_ref, a_ref, b_ref, o_ref, own_ref):
        v = a_ref[...] + b_ref[...]
        o_ref[...] = v.astype(BF16)

        @pl.when(pl.program_id(1) == p_ref[1])
        def _():
            own_ref[...] = v

    return _call(
        body, name="rs_pair_add",
        grid_spec=pltpu.PrefetchScalarGridSpec(
            num_scalar_prefetch=1, grid=(nt, N_CHIPS),
            in_specs=[pl.BlockSpec((None, tm, w), lambda i, s, p: (s, p[0] * nt + i, 0)),
                      pl.BlockSpec((None, tm, w), lambda i, s, p: (s, i, 0))],
            out_specs=[pl.BlockSpec((None, tm, w), lambda i, s, p: (s, i, 0)),
                       pl.BlockSpec((tm, w), lambda i, s, p: (i, 0))]),
        out_shape=[jax.ShapeDtypeStruct((N_CHIPS, half, w), BF16), jax.ShapeDtypeStruct((half, w), F32)],
    )(place, g, got)


def _chip_exchange(p):
    _, half, w = p.shape

    def body(p_ref, out_ref, send_sems, recv_sems, local_sem):
        x, y, c = _place()
        me = 2 * x + y
        chips = _other_chips(x, y)
        local = pltpu.make_async_copy(p_ref.at[me], out_ref.at[me], local_sem)
        local.start()
        sends = []
        for j, (cx, cy) in enumerate(chips):
            cp = pltpu.make_async_remote_copy(src_ref=p_ref.at[2 * cx + cy], dst_ref=out_ref.at[me], send_sem=send_sems.at[j],
                                              recv_sem=recv_sems.at[j], device_id=(cx, cy, c), device_id_type=MESH)
            cp.start()
            sends.append(cp)
        for j, (cx, cy) in enumerate(chips):
            slab = out_ref.at[2 * cx + cy]
            pltpu.make_async_remote_copy(src_ref=slab, dst_ref=slab, send_sem=send_sems.at[j], recv_sem=recv_sems.at[j],
                                         device_id=(x, y, c), device_id_type=MESH).wait_recv()
        for cp in sends:
            cp.wait_send()
        local.wait()

    return _call(body, name="rs_chip_exchange", in_specs=[ANY_SPEC], out_specs=ANY_SPEC,
                 out_shape=jax.ShapeDtypeStruct((N_CHIPS, half, w), p.dtype),
                 scratch_shapes=[pltpu.SemaphoreType.DMA((3,)), pltpu.SemaphoreType.DMA((3,)), pltpu.SemaphoreType.DMA])(p)


def _sum_chips(parts, own, place):
    _, half, w = parts.shape
    tm = _row_tile(half, RS_TILE)

    def body(p_ref, parts_ref, own_ref, o_ref):
        me = p_ref[1]
        t = [jnp.where(me == q, own_ref[...], parts_ref[q].astype(F32)) for q in range(N_CHIPS)]
        o_ref[...] = (t[0] + t[1]) + (t[2] + t[3])

    return _call(
        body, name="rs_sum_chips",
        grid_spec=pltpu.PrefetchScalarGridSpec(
            num_scalar_prefetch=1, grid=(half // tm,),
            in_specs=[pl.BlockSpec((N_CHIPS, tm, w), lambda i, p: (0, i, 0)), pl.BlockSpec((tm, w), lambda i, p: (i, 0))],
            out_specs=pl.BlockSpec((tm, w), lambda i, p: (i, 0))),
        out_shape=jax.ShapeDtypeStruct((half, w), F32),
    )(place, parts, own)


def _sibling_gather(hpart):
    half, w = hpart.shape

    def body(h_ref, out_ref, send_sem, recv_sem, local_sem):
        x, y, c = _place()
        local = pltpu.make_async_copy(h_ref, out_ref.at[pl.ds(c * half, half)], local_sem)
        local.start()
        cp = pltpu.make_async_remote_copy(src_ref=h_ref, dst_ref=out_ref.at[pl.ds(c * half, half)], send_sem=send_sem,
                                          recv_sem=recv_sem, device_id=(x, y, 1 - c), device_id_type=MESH)
        cp.start()
        other = out_ref.at[pl.ds((1 - c) * half, half)]
        pltpu.make_async_remote_copy(src_ref=other, dst_ref=other, send_sem=send_sem, recv_sem=recv_sem,
                                     device_id=(x, y, c), device_id_type=MESH).wait_recv()
        cp.wait_send()
        local.wait()

    return _call(body, name="rs_sibling_gather", in_specs=[ANY_SPEC], out_specs=ANY_SPEC,
                 out_shape=jax.ShapeDtypeStruct((2 * half, w), F32),
                 scratch_shapes=[pltpu.SemaphoreType.DMA, pltpu.SemaphoreType.DMA, pltpu.SemaphoreType.DMA])(hpart)


def _reduce_scatter(g, place):
    pair, own = _pair_add(g, _sibling_swap(g), place)
    return _sibling_gather(_sum_chips(_chip_exchange(pair), own, place))


WEIGHTS = ("c_ctx", "w_mod", "b_mod", "w_in", "w_out", "ln_g", "ln_b", "conv_w", "conv_b", "lru_wa", "lru_ba", "lru_wx",
           "lru_bx", "lru_lam", "pool_w", "pool_scale")
SMALL_GATHERED = ("conv_w", "lru_ba", "lru_bx", "lru_lam", "pool_scale")
SMALL_UPDATED = ("c_ctx", "b_mod", "ln_g", "ln_b", "conv_w", "conv_b", "lru_ba", "lru_bx", "lru_lam", "pool_scale")


def kernel(x, c, ctx, c_ctx, w_mod, b_mod, w_in, w_out, ln_g, ln_b, conv_w, conv_b, lru_wa, lru_ba, lru_wx, lru_bx, lru_lam, pool_w, pool_scale, loss_target, m_c_ctx, m_w_mod, m_b_mod, m_w_in, m_w_out, m_ln_g, m_ln_b, m_conv_w, m_conv_b, m_lru_wa, m_lru_ba, m_lru_wx, m_lru_bx, m_lru_lam, m_pool_w, m_pool_scale, v_c_ctx, v_w_mod, v_b_mod, v_w_in, v_w_out, v_ln_g, v_ln_b, v_conv_w, v_conv_b, v_lru_wa, v_lru_ba, v_lru_wx, v_lru_bx, v_lru_lam, v_pool_w, v_pool_scale):
    weights = dict(c_ctx=c_ctx, w_mod=w_mod, b_mod=b_mod, w_in=w_in, w_out=w_out, ln_g=ln_g, ln_b=ln_b, conv_w=conv_w,
                   conv_b=conv_b, lru_wa=lru_wa, lru_ba=lru_ba, lru_wx=lru_wx, lru_bx=lru_bx, lru_lam=lru_lam,
                   pool_w=pool_w, pool_scale=pool_scale)
    mom1 = dict(c_ctx=m_c_ctx, w_mod=m_w_mod, b_mod=m_b_mod, w_in=m_w_in, w_out=m_w_out, ln_g=m_ln_g, ln_b=m_ln_b,
                conv_w=m_conv_w, conv_b=m_conv_b, lru_wa=m_lru_wa, lru_ba=m_lru_ba, lru_wx=m_lru_wx, lru_bx=m_lru_bx,
                lru_lam=m_lru_lam, pool_w=m_pool_w, pool_scale=m_pool_scale)
    mom2 = dict(c_ctx=v_c_ctx, w_mod=v_w_mod, b_mod=v_b_mod, w_in=v_w_in, w_out=v_w_out, ln_g=v_ln_g, ln_b=v_ln_b,
                conv_w=v_conv_w, conv_b=v_conv_b, lru_wa=v_lru_wa, lru_ba=v_lru_ba, lru_wx=v_lru_wx, lru_bx=v_lru_bx,
                lru_lam=v_lru_lam, pool_w=v_pool_w, pool_scale=v_pool_scale)
    xs, cx, target = x[0], ctx[0], loss_target[0]
    s_len, d = xs.shape
    es = w_out.shape[1]
    e = es * N_CHIPS
    nb = e // LANE
    c3 = w_mod.shape[2]
    n4 = w_in.shape[2]
    pq, pg = pool_w.shape[2], pool_w.shape[3]
    ng = len(POOL_WINDOWS)
    width = n4
    assert width == d and 2 * pg == width and 2 * nb * LANE == N_CHIPS * width and d % (2 * N_CHIPS) == 0
    px, py, pc = _place()
    place = jnp.stack([pc, 2 * px + py]).astype(jnp.int32)
    cctx2 = c_ctx[None, :]

    eq = e // N_CHIPS
    small_rows = [(conv_w[0], 0), (lru_ba[0], CONV_TAPS), (lru_bx[0], CONV_TAPS + 2), (lru_lam[0], CONV_TAPS + 4),
                  (pool_scale, CONV_TAPS + 6)]
    small = _rows_kernel([(a, r, 0) for a, r in small_rows], 2 * SUB, eq, "pack_small_weights")
    wm_g, win0, sg = _gather_chips([w_mod.astype(BF16).reshape(DEPTH * d, c3), w_in[0].astype(BF16), small], "gather_weights0")
    later = [w_out[0].astype(BF16), w_in[1].astype(BF16), w_out[1].astype(BF16), pool_w.astype(BF16).reshape(ng * pq, pg)]
    w_send, w_recv, later, later_lands, w_token = _push_start(
        later, [jnp.broadcast_to(a[None], (N_CHIPS,) + a.shape) for a in later], False, sg, "gather_weights1_start")
    wm_all = wm_g.reshape(N_CHIPS, DEPTH, d, c3)
    full = {n: jnp.swapaxes(sg[:, r:r + a.shape[0]], 0, 1).reshape(a.shape[0], e)
            for n, (a, r) in zip(SMALL_GATHERED, small_rows)}
    wa_b, wx_b = lru_wa[0].astype(BF16), lru_wx[0].astype(BF16)
    lru_args = (full["conv_w"], conv_b, wa_b, wx_b, full["lru_ba"], full["lru_bx"], full["lru_lam"])
    scale_f = full["pool_scale"]

    mod = _mod_fwd(c + w_token[0:1, 0:1], cctx2, wm_all, b_mod[:, None, :])

    def mod_parts(l, row):
        v = mod[l, row]
        return v[None, :d], 1.0 + v[None, d:2 * d], v[None, 2 * d:]

    sh0, sc0, gt0 = mod_parts(0, 0)
    shc, scc, _ = mod_parts(0, 1)
    sh1, sc1, gt1 = mod_parts(1, 0)
    lg = [ln_g[l][None, :] for l in range(DEPTH)]
    lb = [ln_b[l][None, :] for l in range(DEPTH)]

    ug0 = _inproj_fwd(xs, sc0, sh0, win0, "inproj_fwd0")
    uc0 = _inproj_fwd(cx, scc, shc, win0[:2], "inproj_fwd_ctx")
    y0 = _rglru_fwd(ug0, uc0, *lru_args)
    wout0_g, win1, wout1_g, wp_g = _push_wait(w_send, w_recv, later, later_lands, False, y0, "gather_weights1_wait")
    win = [win0, win1]
    wout = [wout0_g.reshape(e, d), wout1_g.reshape(e, d)]
    wp = wp_g.reshape(N_CHIPS, ng, pq, pg)
    br0, x1 = _outproj_fwd(y0, ug0, xs, gt0, wout[0], lg[0], lb[0], None, "outproj_fwd0")
    ug1 = _inproj_fwd(x1, sc1, sh1, win[1], "inproj_fwd1")
    d1 = _pool_map(ug1, nb, False, True, "pool_fwd")
    y1 = _pool_mm_fwd(d1, wp, scale_f)
    br1, dxo, loss_part = _outproj_fwd(y1, ug1, x1, gt1, wout[1], lg[1], lb[1], target, "outproj_fwd1")
    loss = lax.psum(loss_part[0, 0] * (0.5 / d), ("x", "y", "c"))

    row_wout = d
    row_tail = d + es
    wq = 2 * (nb // N_CHIPS) * LANE * LANE // width
    whole = lambda r: (r + 2 * RS_TILE - 1) // (2 * RS_TILE) * (2 * RS_TILE)
    rows1 = whole(row_tail + pg // 2)
    rows0 = whole(row_tail + 2 * wq)
    gbuf1 = jnp.zeros((N_CHIPS, rows1, width), F32)
    gbuf0 = jnp.zeros((N_CHIPS, rows0, width), F32)

    dy1, dg1, dxres1, dbr1, dlg1, dlb1, dgt1 = _outproj_bwd(dxo, x1, br1, y1, ug1, gt1, lg[1], wout[1], "outproj_bwd1")
    gbuf1 = _outproj_bwd_w(y1, ug1, dbr1, gbuf1, row_wout, "outproj_bwd_w1")
    dd1, gbuf1, dscale = _pool_mm_bwd(dy1, d1, wp, scale_f, gbuf1, row_tail)
    du1 = _pool_map(dd1, nb, True, False, "pool_bwd")
    dx1, dsc1, dsh1 = _inproj_bwd_x([du1, dg1], x1, dxres1, sc1, win[1], "inproj_bwd_x1")
    gbuf1 = _inproj_bwd_w(x1, sc1, sh1, [du1, dg1], None, gbuf1, "inproj_bwd_w1")
    pair1, own1 = _pair_add(gbuf1, _sibling_swap(gbuf1), place)
    g_send, g_recv, pair1, parts1, g_token = _push_start([pair1], [jnp.zeros_like(pair1)], True, own1, "rs_exchange1_start")

    dy0, dg0, dxres0, dbr0, dlg0, dlb0, dgt0 = _outproj_bwd(dx1, xs, br0, y0, ug0, gt0 + g_token[0:1, 0:1], lg[0], wout[0],
                                                            "outproj_bwd0")
    gbuf0 = _outproj_bwd_w(y0, ug0, dbr0, gbuf0, row_wout, "outproj_bwd_w0")
    du0, duc, dconv_w, dconv_b, dwa, dwx, dba, dbx, dlam = _rglru_bwd(ug0, uc0, dy0, *lru_args)
    grad_x, dsc0, dsh0 = _inproj_bwd_x([du0, dg0], xs, dxres0, sc0, win[0], "inproj_bwd_x0")
    dscc, dshc = _inproj_bwd_x([duc], cx, None, scc, win[0][:2], "inproj_bwd_x_ctx")
    dwin0c = _inproj_bwd_w(cx, scc, shc, [duc, jnp.zeros_like(duc)], None, None, "inproj_bwd_w_ctx")
    gbuf0 = _inproj_bwd_w(xs, sc0, sh0, [du0, dg0], dwin0c, gbuf0, "inproj_bwd_w0")

    def quarter(dw):
        t = dw.reshape(2, N_CHIPS, nb // N_CHIPS, LANE, LANE)
        return jnp.transpose(t, (1, 3, 0, 2, 4)).reshape(N_CHIPS, LANE, 2 * (nb // N_CHIPS) * LANE).reshape(N_CHIPS, wq, width)

    tail0 = jnp.concatenate([quarter(dwa), quarter(dwx)], axis=1)
    gbuf0 = lax.dynamic_update_slice(gbuf0, tail0, (0, row_tail, 0))
    (parts1,) = _push_wait(g_send, g_recv, pair1, parts1, True, gbuf0, "rs_exchange1_wait")
    red1 = _sibling_gather(_sum_chips(parts1, own1, place))
    red0 = _reduce_scatter(gbuf0, place)
    (rep,) = _gather_chips([red0], "gather_replicated", row0=row_tail, rows=2 * wq)

    k0 = VEC_KINDS
    vec = _rows_kernel(
        [(c, 0, 0), (dsh0, 1, 0), (dsc0, 1, d), (dgt0, 1, 2 * d), (dshc, 2, 0), (dscc, 2, d),
         (dsh1, 3, 0), (dsc1, 3, d), (dgt1, 3, 2 * d),
         (dconv_b, k0, 0), (dlg0, k0, e), (dscale, k0 + 1, 0), (dlg1, k0 + 1, e), (dlb0, k0 + 2, 0), (dlb1, k0 + 2, d),
         (dconv_w, k0 + 3, 0), (dba, k0 + 7, 0), (dbx, k0 + 9, 0), (dlam, k0 + 11, 0)], VEC_ROWS, 3 * d, "pack_vec")
    gt_all = jnp.swapaxes(_gather_devices(vec), 0, 1)
    g_wmod = _mod_bwd_shard(gt_all, cctx2, place, c3)
    g_bmod, g_cctx, g_small = _mod_bwd_rep(gt_all, cctx2, wm_all)

    tmw = _row_tile(d, 256)
    red_src = lambda red, r0, tm: (red, (tm, width), lambda n, i: (r0 // tm + i, 0))
    by_layer = lambda n, gs: jnp.where(n == 0, gs[0], gs[1])
    outs = {}
    outs["w_in"] = _adamw_param(w_in, m_w_in, v_w_in, [red_src(red0, 0, tmw), red_src(red1, 0, tmw)], by_layer, tmw, "adamw_w_in")
    outs["w_out"] = _adamw_param(w_out, m_w_out, v_w_out, [red_src(red0, row_wout, tmw), red_src(red1, row_wout, tmw)],
                                 by_layer, tmw, "adamw_w_out")
    outs["w_mod"] = _adamw_param(w_mod, m_w_mod, v_w_mod, [(g_wmod, (None, tmw, c3), lambda n, i: (n, i, 0))],
                                 lambda n, gs: gs[0], tmw, "adamw_w_mod")
    pw = [a.reshape(ng, pq, pg) for a in (pool_w, m_pool_w, v_pool_w)]
    outs["pool_w"] = [o.reshape(pool_w.shape) for o in _adamw_param(
        *pw, [(red1, (pq, pg), lambda n, i: (row_tail // pq + n // 2, n % 2))], lambda n, gs: gs[0], pq, "adamw_pool_w")]
    bq = nb // N_CHIPS
    rep_src = lambda r0: (rep, (None, LANE, LANE), lambda n, i: ((n % nb) // bq, r0 // LANE, (n // nb) * bq + n % bq))
    for name, r0, trio in (("lru_wa", 0, (lru_wa, m_lru_wa, v_lru_wa)), ("lru_wx", wq, (lru_wx, m_lru_wx, v_lru_wx))):
        blocks = [a.reshape(2 * nb, LANE, LANE) for a in trio]
        outs[name] = [o.reshape(lru_wa.shape) for o in _adamw_param(*blocks, [rep_src(r0)], lambda n, gs: gs[0], LANE,
                                                                    "adamw_" + name)]

    g_small = dict(g_small, c_ctx=g_cctx, b_mod=g_bmod)
    for n in SMALL_GATHERED:
        g_small[n] = lax.dynamic_slice_in_dim(g_small[n], place[1] * eq, eq, axis=1)
    as2d = lambda a: a.reshape(-1, a.shape[-1])
    quads = [(as2d(weights[n]), g_small[n], as2d(mom1[n]), as2d(mom2[n])) for n in SMALL_UPDATED]
    for n, (q, res) in zip(SMALL_UPDATED, zip(quads, _adamw_small(quads))):
        outs[n] = [a.reshape(weights[n].shape) for a in (q[1],) + res]

    result = [loss, grad_x[None]]
    for j in range(4):
        result += [outs[n][j] for n in WEIGHTS]
    return tuple(result)
```

```python
import jax
import jax.numpy as jnp
from jax import lax
from jax.experimental import pallas as pl
from jax.experimental.pallas import tpu as pltpu

F32 = jnp.float32
BF16 = jnp.bfloat16
LANE = 128
SUB = 8
GRID_W = 64
POOL_WINDOWS = (2, 4, 8, 16)
LRU_C = 8.0
DEPTH = 2
ALPHA = float((2 * DEPTH) ** 0.25)
LN_EPS = 1e-5
ADAM_LR, ADAM_B1, ADAM_B2, ADAM_EPS, ADAM_WD, ADAM_STEP = 0.001, 0.9, 0.999, 1e-08, 0.01, 10
N_CHIPS = 4
N_DEV = 8
MESH = pl.DeviceIdType.MESH
ROW_TILE = 512
GATE_TILE = 512
GATE_UNROLL = 1
CONV_TAPS = 4
CONV_LEFT = 2
PAD = 8
SCAN_UNROLL = 8
RS_TILE = 128
POOL_CPAD = 16
VEC_KINDS = 4


def _call(body, **kw):
    return pl.pallas_call(body, **kw)


def _dot(a, b):
    return jnp.dot(a, b, preferred_element_type=F32)


def _dot_nt(a, b):
    return lax.dot_general(a, b, (((1,), (1,)), ((), ())), preferred_element_type=F32)


def _dot_tn(a, b):
    return lax.dot_general(a, b, (((0,), (0,)), ((), ())), preferred_element_type=F32)


def _sigmoid(v):
    return 0.5 * (jnp.tanh(0.5 * v) + 1.0)


def _silu(v):
    return v * _sigmoid(v)


def _dsilu(v):
    s = _sigmoid(v)
    return s * (1.0 + v * (1.0 - s))


def _log_sigmoid(v):
    z = jnp.exp(-jnp.abs(v))
    return jnp.minimum(v, 0.0) - jnp.where(z < 1e-4, z * (1.0 - 0.5 * z), jnp.log(1.0 + z))


def _one_minus_sq(la, a):
    return -jnp.tanh(la) * (a * a + 1.0)


def _cat(ref, n):
    return jnp.concatenate([ref[k] for k in range(n)], axis=1)


def _put_chunks(ref, val, n, base=0):
    for k in range(n):
        ref[base + k] = val[:, k * LANE:(k + 1) * LANE].astype(ref.dtype)


def _row_tile(rows, want):
    t = min(rows, want)
    assert rows % t == 0
    return t


ANY_SPEC = pl.BlockSpec(memory_space=pl.ANY)


def _mod_fwd(cvec, cctx, wm, bm):
    ns, nl, d, c3 = wm.shape

    def body(c_ref, cx_ref, w_ref, b_ref, o_ref):
        cc = jnp.concatenate([c_ref[...], cx_ref[...], jnp.zeros((SUB - 2, d), F32)], axis=0)
        o_ref[...] = _dot(_silu(cc).astype(BF16), w_ref[...]) + b_ref[...]

    return _call(
        body, name="mod_fwd", grid=(nl, ns),
        in_specs=[pl.BlockSpec((1, d), lambda l, s: (0, 0)),
                  pl.BlockSpec((1, d), lambda l, s: (0, 0)),
                  pl.BlockSpec((None, None, d, c3), lambda l, s: (s, l, 0, 0)),
                  pl.BlockSpec((None, 1, c3), lambda l, s: (l, 0, s))],
        out_specs=pl.BlockSpec((None, 8, c3), lambda l, s: (l, 0, s)),
        out_shape=jax.ShapeDtypeStruct((nl, 8, ns * c3), F32),
    )(cvec, cctx, wm, bm)


def _rows_kernel(parts, rows, cols, name):
    def body(*refs):
        o_ref = refs[-1]
        o_ref[...] = jnp.zeros_like(o_ref)
        for ref, (a, r0, c0) in zip(refs[:-1], parts):
            for k in range(a.shape[0]):
                o_ref[r0 + k:r0 + k + 1, c0:c0 + a.shape[1]] = ref[k:k + 1, :]

    return _call(body, name=name, grid=(1,),
                 in_specs=[pl.BlockSpec(a.shape, lambda i: (0, 0)) for a, _, _ in parts],
                 out_specs=pl.BlockSpec((rows, cols), lambda i: (0, 0)),
                 out_shape=jax.ShapeDtypeStruct((rows, cols), F32))(*[a for a, _, _ in parts])


def _mod_bwd_shard(gt, cctx, place, c3):
    d = cctx.shape[1]

    def body(p_ref, cs_ref, dm_ref, dmx_ref, cx_ref, o_ref):
        l = pl.program_id(0)
        lhs = jnp.concatenate([_silu(cs_ref[...]), _silu(cx_ref[...]), jnp.zeros((7, d), F32)], axis=0).astype(BF16)
        dmx = jnp.where(l == 0, jnp.sum(dmx_ref[...], axis=0, keepdims=True), 0.0)
        rhs = jnp.concatenate([dm_ref[...], dmx, jnp.zeros((7, c3), F32)], axis=0).astype(BF16)
        o_ref[...] = _dot_tn(lhs, rhs)

    return _call(
        body, name="mod_bwd_shard",
        grid_spec=pltpu.PrefetchScalarGridSpec(
            num_scalar_prefetch=1, grid=(DEPTH,),
            in_specs=[pl.BlockSpec((None, N_DEV, d), lambda l, p: (0, 0, 0)),
                      pl.BlockSpec((None, N_DEV, c3), lambda l, p: (1 + 2 * l, 0, p[1])),
                      pl.BlockSpec((None, N_DEV, c3), lambda l, p: (2, 0, p[1])),
                      pl.BlockSpec((1, d), lambda l, p: (0, 0))],
            out_specs=pl.BlockSpec((None, d, c3), lambda l, p: (l, 0, 0))),
        out_shape=jax.ShapeDtypeStruct((DEPTH, d, c3), F32),
    )(place, gt, gt, gt, cctx)


def _small_layout(d, e):
    k = VEC_KINDS
    return {
        "conv_b": ((1, e), [(0, k, 0)]),
        "ln_g": ((2, d), [(0, k, e), (1, k + 1, e)]),
        "pool_scale": ((1, e), [(0, k + 1, 0)]),
        "ln_b": ((2, d), [(0, k + 2, 0), (1, k + 2, d)]),
        "conv_w": ((CONV_TAPS, e), [(t, k + 3 + t, 0) for t in range(CONV_TAPS)]),
        "lru_ba": ((2, e), [(j, k + 7 + j, 0) for j in range(2)]),
        "lru_bx": ((2, e), [(j, k + 9 + j, 0) for j in range(2)]),
        "lru_lam": ((2, e), [(j, k + 11 + j, 0) for j in range(2)]),
    }


VEC_ROWS = 24


def _mod_bwd_rep(gt, cctx, wm):
    ns, _, d, c3 = wm.shape
    layout = _small_layout(d, ns * c3 - d)
    names = list(layout)

    def body(g_ref, cx_ref, w_ref, db_ref, dc_ref, *small_refs):
        dm0 = jnp.sum(g_ref[1], axis=0, keepdims=True)
        dmx = jnp.sum(g_ref[2], axis=0, keepdims=True)
        dm1 = jnp.sum(g_ref[3], axis=0, keepdims=True)
        db_ref[0:1, :] = dm0 + dmx
        db_ref[1:2, :] = dm1
        dmxb = jnp.broadcast_to(dmx, (SUB, ns * c3)).astype(BF16)
        acc = jnp.zeros((SUB, d), F32)
        for s in range(ns):
            acc = acc + _dot_nt(dmxb[:, s * c3:(s + 1) * c3], w_ref[s])
        dc_ref[...] = acc[0:1, :] * _dsilu(cx_ref[...])
        for ref, name in zip(small_refs, names):
            shape, places = layout[name]
            for arr_row, vec_row, col0 in places:
                total = jnp.sum(g_ref[vec_row], axis=0, keepdims=True)
                ref[arr_row:arr_row + 1, :] = total[:, col0:col0 + shape[1]]

    outs = _call(
        body, name="mod_bwd_rep", grid=(1,),
        in_specs=[pl.BlockSpec(gt.shape, lambda i: (0, 0, 0)),
                  pl.BlockSpec((1, d), lambda i: (0, 0)),
                  pl.BlockSpec((ns, None, d, c3), lambda i: (0, 0, 0, 0))],
        out_specs=[pl.BlockSpec((DEPTH, ns * c3), lambda i: (0, 0)), pl.BlockSpec((1, d), lambda i: (0, 0))]
        + [pl.BlockSpec(layout[n][0], lambda i: (0, 0)) for n in names],
        out_shape=[jax.ShapeDtypeStruct((DEPTH, ns * c3), F32), jax.ShapeDtypeStruct((1, d), F32)]
        + [jax.ShapeDtypeStruct(layout[n][0], F32) for n in names],
    )(gt, cctx, wm)
    return outs[0], outs[1], dict(zip(names, outs[2:]))


def _inproj_fwd(xin, sc1, sh, w, name):
    rows, d = xin.shape
    ns, _, n4 = w.shape
    cpb = n4 // LANE
    tm = _row_tile(rows, 256)

    def body(x_ref, sc_ref, sh_ref, w_ref, o_ref):
        h = (x_ref[...] * sc_ref[...] + sh_ref[...]).astype(BF16)
        for s in range(ns):
            _put_chunks(o_ref, _dot(h, w_ref[s]), cpb, base=s * cpb)

    return _call(
        body, name=name, grid=(rows // tm,),
        in_specs=[pl.BlockSpec((tm, d), lambda i: (i, 0)),
                  pl.BlockSpec((1, d), lambda i: (0, 0)),
                  pl.BlockSpec((1, d), lambda i: (0, 0)),
                  pl.BlockSpec((ns, d, n4), lambda i: (0, 0, 0))],
        out_specs=pl.BlockSpec((ns * cpb, tm, LANE), lambda i: (0, i, 0)),
        out_shape=jax.ShapeDtypeStruct((ns * cpb, rows, LANE), F32),
    )(xin, sc1, sh, w)


def _inproj_bwd_x(dparts, xin, dxres, sc1, w, name):
    rows, d = xin.shape
    npart = len(dparts)
    e = dparts[0].shape[1]
    ns, _, n4 = w.shape
    per = e // n4
    assert per * npart == ns
    tm = _row_tile(rows, 256)
    has_res = dxres is not None

    def body(*refs):
        dp = refs[:npart]
        x_ref, sc_ref, w_ref = refs[npart:npart + 3]
        rest = refs[npart + 3:]
        if has_res:
            res_ref, dx_ref, dsc_ref, dsh_ref = rest
        else:
            dsc_ref, dsh_ref = rest
        i = pl.program_id(0)
        dh = jnp.zeros((tm, d), F32)
        for p in range(npart):
            v = dp[p][...]
            for q in range(per):
                dh = dh + _dot_nt(v[:, q * n4:(q + 1) * n4], w_ref[p * per + q])

        @pl.when(i == 0)
        def _():
            dsc_ref[...] = jnp.zeros_like(dsc_ref)
            dsh_ref[...] = jnp.zeros_like(dsh_ref)

        dsc_ref[...] += jnp.sum(dh * x_ref[...], axis=0, keepdims=True)
        dsh_ref[...] += jnp.sum(dh, axis=0, keepdims=True)
        if has_res:
            dx_ref[...] = res_ref[...] + dh * sc_ref[...]

    row_spec = pl.BlockSpec((tm, d), lambda i: (i, 0))
    vec_spec = pl.BlockSpec((1, d), lambda i: (0, 0))
    in_specs = [pl.BlockSpec((tm, e), lambda i: (i, 0))] * npart + [row_spec, vec_spec,
                                                                     pl.BlockSpec((ns, d, n4), lambda i: (0, 0, 0))]
    args = list(dparts) + [xin, sc1, w]
    out_specs, out_shape = [vec_spec, vec_spec], [jax.ShapeDtypeStruct((1, d), F32)] * 2
    if has_res:
        in_specs.append(row_spec)
        args.append(dxres)
        out_specs = [row_spec] + out_specs
        out_shape = [jax.ShapeDtypeStruct((rows, d), F32)] + out_shape
    return _call(body, name=name, grid=(rows // tm,), in_specs=in_specs, out_specs=out_specs, out_shape=out_shape)(*args)


def _inproj_bwd_w(xin, sc1, sh, dparts, init, gbuf, name):
    rows, d = xin.shape
    npart = len(dparts)
    e = dparts[0].shape[1]
    n4 = e // 2
    ns = 2 * npart
    tm = _row_tile(rows, 512)
    nt = rows // tm
    has_init = init is not None
    into = gbuf is not None
    assert not into or (ns == N_CHIPS and gbuf.shape[2] == n4)

    def body(*refs):
        x_ref, sc_ref, sh_ref = refs[:3]
        dp = refs[3:3 + npart]
        init_ref = refs[3 + npart] if has_init else None
        o_ref = refs[-1]
        s, i = pl.program_id(0), pl.program_id(1)
        h = (x_ref[...] * sc_ref[...] + sh_ref[...]).astype(BF16)

        @pl.when(i == 0)
        def _():
            o_ref[...] = init_ref[...] if has_init else jnp.zeros_like(o_ref)

        for p in range(npart):
            @pl.when(s // 2 == p)
            def _(p=p):
                o_ref[...] += _dot_tn(h, dp[p][...])

    in_specs = [pl.BlockSpec((tm, d), lambda s, i: (i, 0)),
                pl.BlockSpec((1, d), lambda s, i: (0, 0)),
                pl.BlockSpec((1, d), lambda s, i: (0, 0))]
    in_specs += [pl.BlockSpec((tm, n4), lambda s, i: (i, s % 2))] * npart
    args = [xin, sc1, sh] + list(dparts)
    o_spec = pl.BlockSpec((None, d, n4), lambda s, i: (s, 0, 0))
    if has_init:
        in_specs.append(o_spec)
        args.append(init)
    extra = {}
    if into:
        in_specs.append(ANY_SPEC)
        args.append(gbuf)
        extra = dict(input_output_aliases={len(args) - 1: 0})
    out_shape = jax.ShapeDtypeStruct(gbuf.shape if into else (ns, d, n4), F32)
    return _call(body, name=name, grid=(ns, nt), in_specs=in_specs, out_specs=o_spec, out_shape=out_shape, **extra)(*args)


def _ln_stats(r):
    mu = jnp.mean(r, axis=-1, keepdims=True)
    var = jnp.mean(jnp.square(r - mu), axis=-1, keepdims=True)
    rstd = lax.rsqrt(var + LN_EPS)
    return (r - mu) * rstd, rstd


def _outproj_fwd(y, ug, xin, gt, wout, lg, lb, target, name):
    nch, rows, _ = y.shape
    e, d = wout.shape
    tm = _row_tile(rows, 256)
    with_loss = target is not None

    def body(*refs):
        y_ref, g_ref, x_ref, gt_ref, w_ref, lg_ref, lb_ref = refs[:7]
        if with_loss:
            t_ref, br_ref, dxo_ref, loss_ref = refs[7:]
        else:
            br_ref, xo_ref = refs[7:]
        z = jnp.concatenate([(y_ref[k] * _silu(g_ref[k])).astype(BF16) for k in range(nch)], axis=1)
        br = _dot(z, w_ref[...])
        br_ref[...] = br
        xhat, _ = _ln_stats(ALPHA * x_ref[...] + gt_ref[...] * br)
        xo = xhat * lg_ref[...] + lb_ref[...]
        if with_loss:
            err = xo - t_ref[...]
            dxo_ref[...] = err * (1.0 / d)

            @pl.when(pl.program_id(0) == 0)
            def _():
                loss_ref[...] = jnp.zeros_like(loss_ref)

            loss_ref[...] += jnp.sum(err * err)
        else:
            xo_ref[...] = xo

    chunk_spec = pl.BlockSpec((nch, tm, LANE), lambda i: (0, i, 0))
    g_spec = pl.BlockSpec((nch, tm, LANE), lambda i: (1, i, 0))
    row_spec = pl.BlockSpec((tm, d), lambda i: (i, 0))
    vec_spec = pl.BlockSpec((1, d), lambda i: (0, 0))
    in_specs = [chunk_spec, g_spec, row_spec, vec_spec, pl.BlockSpec((e, d), lambda i: (0, 0)), vec_spec, vec_spec]
    args = [y, ug, xin, gt, wout, lg, lb]
    out_specs = [row_spec, row_spec]
    out_shape = [jax.ShapeDtypeStruct((rows, d), F32)] * 2
    if with_loss:
        in_specs.append(row_spec)
        args.append(target)
        out_specs.append(pl.BlockSpec((1, LANE), lambda i: (0, 0)))
        out_shape.append(jax.ShapeDtypeStruct((1, LANE), F32))
    return _call(body, name=name, grid=(rows // tm,), in_specs=in_specs, out_specs=out_specs, out_shape=out_shape)(*args)


def _outproj_bwd(dxo, xin, br, y, ug, gt, lg, wout, name):
    nch, rows, _ = y.shape
    e, d = wout.shape
    tm = _row_tile(rows, 256)

    def body(dxo_ref, x_ref, br_ref, y_ref, g_ref, gt_ref, lg_ref, w_ref,
             dy_ref, dg_ref, dxres_ref, dbr_ref, dlg_ref, dlb_ref, dgt_ref):
        dxo_v = dxo_ref[...]
        brv = br_ref[...]
        xhat, rstd = _ln_stats(ALPHA * x_ref[...] + gt_ref[...] * brv)
        dxh = dxo_v * lg_ref[...]
        dr = rstd * (dxh - jnp.mean(dxh, axis=-1, keepdims=True) - xhat * jnp.mean(dxh * xhat, axis=-1, keepdims=True))

        @pl.when(pl.program_id(0) == 0)
        def _():
            dlg_ref[...] = jnp.zeros_like(dlg_ref)
            dlb_ref[...] = jnp.zeros_like(dlb_ref)
            dgt_ref[...] = jnp.zeros_like(dgt_ref)

        dlg_ref[...] += jnp.sum(dxo_v * xhat, axis=0, keepdims=True)
        dlb_ref[...] += jnp.sum(dxo_v, axis=0, keepdims=True)
        dgt_ref[...] += jnp.sum(dr * brv, axis=0, keepdims=True)
        dxres_ref[...] = ALPHA * dr
        dbr = (gt_ref[...] * dr).astype(BF16)
        dbr_ref[...] = dbr
        dz = _dot_nt(dbr, w_ref[...])
        for k in range(nch):
            dzk = dz[:, k * LANE:(k + 1) * LANE]
            gk = g_ref[k]
            dy_ref[k] = dzk * _silu(gk)
            dg_ref[:, k * LANE:(k + 1) * LANE] = (dzk * y_ref[k] * _dsilu(gk)).astype(BF16)

    chunk_spec = pl.BlockSpec((nch, tm, LANE), lambda i: (0, i, 0))
    g_spec = pl.BlockSpec((nch, tm, LANE), lambda i: (1, i, 0))
    row_spec = pl.BlockSpec((tm, d), lambda i: (i, 0))
    vec_spec = pl.BlockSpec((1, d), lambda i: (0, 0))
    return _call(
        body, name=name, grid=(rows // tm,),
        in_specs=[row_spec, row_spec, row_spec, chunk_spec, g_spec, vec_spec, vec_spec, pl.BlockSpec((e, d), lambda i: (0, 0))],
        out_specs=[chunk_spec, pl.BlockSpec((tm, e), lambda i: (i, 0)), row_spec, row_spec, vec_spec, vec_spec, vec_spec],
        out_shape=[jax.ShapeDtypeStruct((nch, rows, LANE), F32), jax.ShapeDtypeStruct((rows, e), BF16),
                   jax.ShapeDtypeStruct((rows, d), F32), jax.ShapeDtypeStruct((rows, d), BF16)]
        + [jax.ShapeDtypeStruct((1, d), F32)] * 3,
    )(dxo, xin, br, y, ug, gt, lg, wout)


def _outproj_bwd_w(y, ug, dbr, gbuf, row0, name):
    nch, rows, _ = y.shape
    d = dbr.shape[1]
    e = nch * LANE
    es = e // N_CHIPS
    tm = _row_tile(rows, 512)
    assert gbuf.shape[2] == d and row0 % es == 0

    def body(y_ref, g_ref, dbr_ref, buf_ref, o_ref):
        @pl.when(pl.program_id(0) == 0)
        def _():
            o_ref[...] = jnp.zeros_like(o_ref)

        z = jnp.concatenate([(y_ref[k] * _silu(g_ref[k])).astype(BF16) for k in range(nch)], axis=1)
        o_ref[...] += _dot_tn(z, dbr_ref[...]).reshape(N_CHIPS, es, d)

    return _call(
        body, name=name, grid=(rows // tm,),
        in_specs=[pl.BlockSpec((nch, tm, LANE), lambda i: (0, i, 0)),
                  pl.BlockSpec((nch, tm, LANE), lambda i: (1, i, 0)),
                  pl.BlockSpec((tm, d), lambda i: (i, 0)),
                  ANY_SPEC],
        out_specs=pl.BlockSpec((N_CHIPS, es, d), lambda i: (0, row0 // es, 0)),
        out_shape=jax.ShapeDtypeStruct(gbuf.shape, F32),
        input_output_aliases={3: 0},
    )(y, ug, dbr, gbuf)


def _scan(a_ref, b_ref, h_ref, *, length, init, reverse, a_shift, store):
    nblk = length // SUB
    assert nblk % SCAN_UNROLL == 0
    row = lax.broadcasted_iota(jnp.int32, (SUB, LANE), 0)
    last = 0 if reverse else SUB - 1
    edges = [(row >= SUB - k) if reverse else (row < k) for k in (1, 2, 4)]

    def local_scan(a, b):
        for k, edge in zip((1, 2, 4), edges):
            sh = (SUB - k) if reverse else k
            b = b + a * jnp.where(edge, 0.0, pltpu.roll(b, sh, 0))
            a = a * jnp.where(edge, 1.0, pltpu.roll(a, sh, 0))
        return a, b

    def step(i, carry):
        base = pl.multiple_of(((nblk // SCAN_UNROLL - 1 - i) if reverse else i) * (SCAN_UNROLL * SUB), SCAN_UNROLL * SUB)
        order = range(SCAN_UNROLL - 1, -1, -1) if reverse else range(SCAN_UNROLL)
        loaded = [(a_ref[pl.ds(PAD + base + j * SUB + a_shift, SUB), :], b_ref[pl.ds(PAD + base + j * SUB, SUB), :])
                  for j in order]
        scanned = [local_scan(a, b) for a, b in loaded]
        for j, (a, b) in zip(order, scanned):
            if store:
                h_ref[pl.ds(PAD + base + j * SUB, SUB), :] = b + a * carry
            a_l = jnp.broadcast_to(a[last:last + 1, :], (SUB, LANE))
            b_l = jnp.broadcast_to(b[last:last + 1, :], (SUB, LANE))
            carry = b_l + a_l * carry
        return carry

    carry = lax.fori_loop(0, nblk // SCAN_UNROLL, step, jnp.broadcast_to(init, (SUB, LANE)))
    return carry[0:1, :]


def _conv_fwd(src_ref, upad, u_ref, cw, cb, length):
    zeros = jnp.zeros((PAD, LANE), F32)
    upad[pl.ds(0, PAD), :] = zeros
    upad[pl.ds(PAD + length, PAD), :] = zeros
    rt = _row_tile(length, ROW_TILE)

    def copy(i, c):
        t0 = pl.multiple_of(i * rt, rt)
        upad[pl.ds(PAD + t0, rt), :] = src_ref[pl.ds(t0, rt), :]
        return c

    lax.fori_loop(0, length // rt, copy, 0)

    def tile(i, c):
        t0 = pl.multiple_of(i * rt, rt)
        acc = jnp.zeros((rt, LANE), F32)
        for k in range(CONV_TAPS):
            acc = acc + upad[pl.ds(t0 + PAD - CONV_LEFT + k, rt), :] * cw[k:k + 1, :]
        u_ref[pl.ds(t0, rt), :] = acc + cb
        return c

    lax.fori_loop(0, length // rt, tile, 0)


def _gates_fwd(u_ref, a_ref, b_ref, wa, wx, ba, bx, ls, length):
    rt = _row_tile(length, GATE_TILE)

    def tile(i, c):
        t0 = pl.multiple_of(i * rt, rt)
        ut = u_ref[pl.ds(t0, rt), :]
        ub = ut.astype(BF16)
        r = _sigmoid(_dot(ub, wa) + ba)
        ig = _sigmoid(_dot(ub, wx) + bx)
        la = (LRU_C * r) * ls
        a = jnp.exp(la)
        a_ref[pl.ds(PAD + t0, rt), :] = a
        b_ref[pl.ds(PAD + t0, rt), :] = jnp.sqrt(_one_minus_sq(la, a)) * (ig * ut)
        return c

    lax.fori_loop(0, length // rt, tile, 0, unroll=min(GATE_UNROLL, length // rt))


def _lru_specs():
    return [pl.BlockSpec((CONV_TAPS, LANE), lambda n: (0, n)),
            pl.BlockSpec((1, LANE), lambda n: (0, n)),
            pl.BlockSpec((2, None, LANE, LANE), lambda n: (0, n, 0, 0)),
            pl.BlockSpec((2, None, LANE, LANE), lambda n: (0, n, 0, 0)),
            pl.BlockSpec((2, LANE), lambda n: (0, n)),
            pl.BlockSpec((2, LANE), lambda n: (0, n)),
            pl.BlockSpec((2, LANE), lambda n: (0, n))]


def _rglru_fwd(ug, uc, conv_w, conv_b, wa, wx, ba, bx, lam):
    nb = uc.shape[0]
    s_len, t_len = ug.shape[1], uc.shape[1]

    def body(u0_ref, uc0_ref, cw_ref, cb_ref, wa_ref, wx_ref, ba_ref, bx_ref, lam_ref, y_ref,
             upad, ubuf, abuf, hbuf):
        cw, cb = cw_ref[...], cb_ref[...]
        lsig = _log_sigmoid(lam_ref[...])
        zero = jnp.zeros((1, LANE), F32)
        _conv_fwd(uc0_ref, upad, ubuf, cw, cb, t_len)
        h0 = []
        for dr in range(2):
            _gates_fwd(ubuf, abuf, hbuf, wa_ref[dr], wx_ref[dr], ba_ref[dr:dr + 1, :], bx_ref[dr:dr + 1, :],
                       lsig[dr:dr + 1, :], t_len)
            h0.append(_scan(abuf, hbuf, hbuf, length=t_len, init=zero, reverse=(dr == 1), a_shift=0, store=False))
        _conv_fwd(u0_ref, upad, ubuf, cw, cb, s_len)
        rt = _row_tile(s_len, ROW_TILE)
        for dr in range(2):
            _gates_fwd(ubuf, abuf, hbuf, wa_ref[dr], wx_ref[dr], ba_ref[dr:dr + 1, :], bx_ref[dr:dr + 1, :],
                       lsig[dr:dr + 1, :], s_len)
            _scan(abuf, hbuf, hbuf, length=s_len, init=h0[dr], reverse=(dr == 1), a_shift=0, store=True)

            def acc(i, c, dr=dr):
                t0 = pl.multiple_of(i * rt, rt)
                h = hbuf[pl.ds(PAD + t0, rt), :]
                if dr == 0:
                    y_ref[pl.ds(t0, rt), :] = h
                else:
                    y_ref[pl.ds(t0, rt), :] += h
                return c

            lax.fori_loop(0, s_len // rt, acc, 0)

    seq = pltpu.VMEM((s_len + 2 * PAD, LANE), F32)
    return _call(
        body, name="rglru_fwd", grid=(nb,),
        in_specs=[pl.BlockSpec((None, s_len, LANE), lambda n: (n, 0, 0)),
                  pl.BlockSpec((None, t_len, LANE), lambda n: (n, 0, 0))] + _lru_specs(),
        out_specs=pl.BlockSpec((None, s_len, LANE), lambda n: (n, 0, 0)),
        out_shape=jax.ShapeDtypeStruct((nb, s_len, LANE), F32),
        scratch_shapes=[seq, pltpu.VMEM((s_len, LANE), F32), seq, seq],
    )(ug, uc, conv_w, conv_b, wa, wx, ba, bx, lam)


def _rglru_bwd(ug, uc, dy, conv_w, conv_b, wa, wx, ba, bx, lam):
    nb = uc.shape[0]
    e = nb * LANE
    s_len, t_len = ug.shape[1], uc.shape[1]

    def body(u0_ref, uc0_ref, dy_ref, cw_ref, cb_ref, wa_ref, wx_ref, ba_ref, bx_ref, lam_ref,
             du_ref, duc_ref, dcw_ref, dcb_ref, dwa_ref, dwx_ref, dba_ref, dbx_ref, dlam_ref,
             upad, ubuf, abuf, hbuf, lbuf, dubuf, cpad, cu, ca0, ch0, ca1, ch1):
        cw, cb = cw_ref[...], cb_ref[...]
        lam_v = lam_ref[...]
        lsig = _log_sigmoid(lam_v)
        zero = jnp.zeros((1, LANE), F32)
        zpad = jnp.zeros((PAD, LANE), F32)
        for ref in (dcw_ref, dcb_ref, dwa_ref, dwx_ref, dba_ref, dbx_ref, dlam_ref):
            ref[...] = jnp.zeros_like(ref)

        def params(dr):
            return (wa_ref[dr], wx_ref[dr], ba_ref[dr:dr + 1, :], bx_ref[dr:dr + 1, :], lsig[dr:dr + 1, :])

        def direction_bwd(dr, u_ref, a_ref, h_ref, l_ref, dub, length, first):
            wa_d, wx_d, ba_d, bx_d, ls_d = params(dr)
            rt = _row_tile(length, GATE_TILE)
            prev = 1 if dr == 1 else -1

            def tile(i, c):
                t0 = pl.multiple_of(i * rt, rt)
                ut = u_ref[pl.ds(t0, rt), :]
                ub = ut.astype(BF16)
                r = _sigmoid(_dot(ub, wa_d) + ba_d)
                ig = _sigmoid(_dot(ub, wx_d) + bx_d)
                la = (LRU_C * r) * ls_d
                a = a_ref[pl.ds(PAD + t0, rt), :]
                q = _one_minus_sq(la, a)
                rs = lax.rsqrt(q)
                sq = q * rs
                lm = l_ref[pl.ds(PAD + t0, rt), :]
                da = lm * h_ref[pl.ds(PAD + t0 + prev, rt), :]
                dsq = lm * ig * ut
                dig = lm * sq * ut
                dla = da * a - dsq * (a * a) * rs
                dr_ = dla * (LRU_C * ls_d)
                dlam_ref[dr:dr + 1, :] += jnp.sum(dla * (LRU_C * r), axis=0, keepdims=True)
                dpr = dr_ * r * (1.0 - r)
                dpi = dig * ig * (1.0 - ig)
                dba_ref[dr:dr + 1, :] += jnp.sum(dpr, axis=0, keepdims=True)
                dbx_ref[dr:dr + 1, :] += jnp.sum(dpi, axis=0, keepdims=True)
                dprb, dpib = dpr.astype(BF16), dpi.astype(BF16)
                dwa_ref[dr] += _dot_tn(ub, dprb)
                dwx_ref[dr] += _dot_tn(ub, dpib)
                dut = lm * sq * ig + _dot_nt(dprb, wa_d) + _dot_nt(dpib, wx_d)
                if first:
                    dub[pl.ds(PAD + t0, rt), :] = dut
                else:
                    dub[pl.ds(PAD + t0, rt), :] += dut
                return c

            lax.fori_loop(0, length // rt, tile, 0, unroll=min(GATE_UNROLL, length // rt))

        def conv_bwd(dub, src_pad, out_ref, length):
            rt = _row_tile(length, ROW_TILE)

            def tile(i, c):
                t0 = pl.multiple_of(i * rt, rt)
                dut = dub[pl.ds(PAD + t0, rt), :]
                dcb_ref[...] += jnp.sum(dut, axis=0, keepdims=True)
                acc = jnp.zeros((rt, LANE), F32)
                for k in range(CONV_TAPS):
                    sh = CONV_LEFT - k
                    acc = acc + dub[pl.ds(PAD + t0 + sh, rt), :] * cw[k:k + 1, :]
                    dcw_ref[k:k + 1, :] += jnp.sum(dut * src_pad[pl.ds(PAD + t0 - sh, rt), :], axis=0, keepdims=True)
                out_ref[pl.ds(t0, rt), :] = acc.astype(out_ref.dtype)
                return c

            lax.fori_loop(0, length // rt, tile, 0)

        _conv_fwd(uc0_ref, cpad, cu, cw, cb, t_len)
        cbufs = ((ca0, ch0), (ca1, ch1))
        h0 = []
        for dr in range(2):
            ca, chh = cbufs[dr]
            _gates_fwd(cu, ca, chh, *params(dr), t_len)
            h0.append(_scan(ca, chh, chh, length=t_len, init=zero, reverse=(dr == 1), a_shift=0, store=True))
        _conv_fwd(u0_ref, upad, ubuf, cw, cb, s_len)
        rt = _row_tile(s_len, ROW_TILE)
        dh0 = []
        for dr in range(2):
            rev = dr == 1
            _gates_fwd(ubuf, abuf, hbuf, *params(dr), s_len)
            _scan(abuf, hbuf, hbuf, length=s_len, init=h0[dr], reverse=rev, a_shift=0, store=True)
            first_row = PAD + s_len if rev else PAD - 1
            hbuf[pl.ds(first_row, 1), :] = h0[dr]
            end_row = PAD - 1 if rev else PAD + s_len
            abuf[pl.ds(end_row, 1), :] = zero

            def copy(i, c):
                t0 = pl.multiple_of(i * rt, rt)
                lbuf[pl.ds(PAD + t0, rt), :] = dy_ref[pl.ds(t0, rt), :]
                return c

            lax.fori_loop(0, s_len // rt, copy, 0)
            _scan(abuf, lbuf, lbuf, length=s_len, init=zero, reverse=not rev, a_shift=(-1 if rev else 1), store=True)
            start = PAD + s_len - 1 if rev else PAD
            dh0.append(abuf[pl.ds(start, 1), :] * lbuf[pl.ds(start, 1), :])
            direction_bwd(dr, ubuf, abuf, hbuf, lbuf, dubuf, s_len, first=(dr == 0))
        dubuf[pl.ds(0, PAD), :] = zpad
        dubuf[pl.ds(PAD + s_len, PAD), :] = zpad
        conv_bwd(dubuf, upad, du_ref, s_len)
        lc = lbuf
        duc_buf = dubuf
        for dr in range(2):
            rev = dr == 1
            ca, chh = cbufs[dr]
            first_row = PAD + t_len if rev else PAD - 1
            chh[pl.ds(first_row, 1), :] = zero
            end_row = PAD - 1 if rev else PAD + t_len
            ca[pl.ds(end_row, 1), :] = zero + 1.0
            rtc = _row_tile(t_len, ROW_TILE)

            def clear(i, c):
                t0 = pl.multiple_of(i * rtc, rtc)
                lc[pl.ds(PAD + t0, rtc), :] = jnp.zeros((rtc, LANE), F32)
                return c

            lax.fori_loop(0, t_len // rtc, clear, 0)
            _scan(ca, lc, lc, length=t_len, init=dh0[dr], reverse=not rev, a_shift=(-1 if rev else 1), store=True)
            direction_bwd(dr, cu, ca, chh, lc, duc_buf, t_len, first=(dr == 0))
        duc_buf[pl.ds(0, PAD), :] = zpad
        duc_buf[pl.ds(PAD + t_len, PAD), :] = zpad
        conv_bwd(duc_buf, cpad, duc_ref, t_len)
        dlam_ref[...] = dlam_ref[...] * (1.0 - _sigmoid(lam_v))

    seq = pltpu.VMEM((s_len + 2 * PAD, LANE), F32)
    cseq = pltpu.VMEM((t_len + 2 * PAD, LANE), F32)
    vec2 = pl.BlockSpec((2, LANE), lambda n: (0, n))
    wspec = pl.BlockSpec((2, None, LANE, LANE), lambda n: (0, n, 0, 0))
    return _call(
        body, name="rglru_bwd", grid=(nb,),
        in_specs=[pl.BlockSpec((None, s_len, LANE), lambda n: (n, 0, 0)),
                  pl.BlockSpec((None, t_len, LANE), lambda n: (n, 0, 0)),
                  pl.BlockSpec((None, s_len, LANE), lambda n: (n, 0, 0))] + _lru_specs(),
        out_specs=[pl.BlockSpec((s_len, LANE), lambda n: (0, n)),
                   pl.BlockSpec((t_len, LANE), lambda n: (0, n)),
                   pl.BlockSpec((CONV_TAPS, LANE), lambda n: (0, n)),
                   pl.BlockSpec((1, LANE), lambda n: (0, n)),
                   wspec, wspec, vec2, vec2, vec2],
        out_shape=[jax.ShapeDtypeStruct((s_len, e), BF16), jax.ShapeDtypeStruct((t_len, e), BF16),
                   jax.ShapeDtypeStruct((CONV_TAPS, e), F32), jax.ShapeDtypeStruct((1, e), F32),
                   jax.ShapeDtypeStruct((2, nb, LANE, LANE), F32), jax.ShapeDtypeStruct((2, nb, LANE, LANE), F32),
                   jax.ShapeDtypeStruct((2, e), F32), jax.ShapeDtypeStruct((2, e), F32), jax.ShapeDtypeStruct((2, e), F32)],
        scratch_shapes=[seq, pltpu.VMEM((s_len, LANE), F32), seq, seq, seq, seq,
                        cseq, pltpu.VMEM((t_len, LANE), F32), cseq, cseq, cseq, cseq],
    )(ug, uc, dy, conv_w, conv_b, wa, wx, ba, bx, lam)


def _pool_windows(src_ref, out_ref, colbuf, rowbuf, half, transpose, s_len):
    gw = GRID_W
    lg = gw.bit_length() - 1
    n_rows = s_len // gw
    cp, rm = POOL_CPAD, 8 * gw
    stride = gw + 2 * cp
    rt = _row_tile(s_len, ROW_TILE)
    assert rt % gw == 0 and half <= cp
    gpt = rt // gw
    offs = range(-half, half)
    zmargin = jnp.zeros((cp, LANE), F32)

    def zcol(r, c):
        base = pl.multiple_of(r * stride, SUB)
        colbuf[pl.ds(base, cp), :] = zmargin
        colbuf[pl.ds(base + cp + gw, cp), :] = zmargin
        return c

    lax.fori_loop(0, n_rows, zcol, 0)

    def zrow(i, c):
        t0 = pl.multiple_of(i * gw, gw)
        rowbuf[pl.ds(t0, gw), :] = jnp.zeros((gw, LANE), F32)
        rowbuf[pl.ds(rm + s_len + t0, gw), :] = jnp.zeros((gw, LANE), F32)
        return c

    lax.fori_loop(0, rm // gw, zrow, 0)

    col = lax.broadcasted_iota(jnp.int32, (gw, LANE), 0)
    ccnt = (jnp.minimum(col + half, gw) - jnp.maximum(col - half, 0)).astype(F32)

    def row_counts(t0):
        row = (t0 + lax.broadcasted_iota(jnp.int32, (rt, LANE), 0)) >> lg
        return (jnp.minimum(row + half, n_rows) - jnp.maximum(row - half, 0)).astype(F32)

    def col_base(t0, g):
        return pl.multiple_of((t0 // gw) * stride, SUB) + g * stride + cp

    def col_sum(t0, g, sign):
        acc = jnp.zeros((gw, LANE), F32)
        for o in offs:
            acc = acc + colbuf[pl.ds(col_base(t0, g) + sign * o, gw), :]
        return acc

    def row_sum(t0, sign):
        acc = jnp.zeros((rt, LANE), F32)
        for o in offs:
            acc = acc + rowbuf[pl.ds(rm + t0 + sign * o * gw, rt), :]
        return acc

    n_tiles = s_len // rt
    assert rt >= half * gw

    def loop(fn, edges=False):
        def step(i, c):
            t0 = pl.multiple_of(i * rt, rt)
            fn(t0, False) if edges else fn(t0)
            return c
        if edges:
            fn(0, True)
            if n_tiles > 1:
                fn(s_len - rt, True)
            lax.fori_loop(1, n_tiles - 1, step, 0)
        else:
            lax.fori_loop(0, n_tiles, step, 0)

    inv_ccnt = 1.0 / ccnt

    def by_row_count(v, t0, edge):
        return v / row_counts(t0) if edge else v * (1.0 / (2 * half))

    if not transpose:
        def fill(t0):
            for g in range(gpt):
                colbuf[pl.ds(col_base(t0, g), gw), :] = src_ref[pl.ds(t0 + g * gw, gw), :]

        def cols(t0):
            for g in range(gpt):
                rowbuf[pl.ds(rm + t0 + g * gw, gw), :] = col_sum(t0, g, 1) * inv_ccnt

        def rows(t0, edge):
            mean = by_row_count(row_sum(t0, 1), t0, edge)
            out_ref[pl.ds(t0, rt), :] = (mean - src_ref[pl.ds(t0, rt), :]).astype(out_ref.dtype)

        loop(fill)
        loop(cols)
        loop(rows, edges=True)
    else:
        def fill(t0, edge):
            rowbuf[pl.ds(rm + t0, rt), :] = by_row_count(src_ref[pl.ds(t0, rt), :], t0, edge)

        def rows(t0):
            acc = row_sum(t0, -1)
            for g in range(gpt):
                colbuf[pl.ds(col_base(t0, g), gw), :] = acc[g * gw:(g + 1) * gw, :] * inv_ccnt

        def cols(t0):
            for g in range(gpt):
                rows_g = pl.ds(t0 + g * gw, gw)
                out_ref[rows_g, :] = (col_sum(t0, g, -1) - src_ref[rows_g, :]).astype(out_ref.dtype)

        loop(fill, edges=True)
        loop(rows)
        loop(cols)


def _pool_map(src, nb, transpose, out_chunk_major, name):
    s_len = src.shape[1]
    cpg = nb // len(POOL_WINDOWS)

    def body(src_ref, out_ref, colbuf, rowbuf):
        n = pl.program_id(0)
        for gi, w in enumerate(POOL_WINDOWS):
            @pl.when(n // cpg == gi)
            def _(w=w):
                _pool_windows(src_ref, out_ref, colbuf, rowbuf, w // 2, transpose, s_len)

    if out_chunk_major:
        out_spec = pl.BlockSpec((None, s_len, LANE), lambda n: (n, 0, 0))
        out_shape = jax.ShapeDtypeStruct((nb, s_len, LANE), BF16)
    else:
        out_spec = pl.BlockSpec((s_len, LANE), lambda n: (0, n))
        out_shape = jax.ShapeDtypeStruct((s_len, nb * LANE), BF16)
    return _call(
        body, name=name, grid=(nb,),
        in_specs=[pl.BlockSpec((None, s_len, LANE), lambda n: (n, 0, 0))],
        out_specs=out_spec, out_shape=out_shape,
        scratch_shapes=[pltpu.VMEM((s_len // GRID_W * (GRID_W + 2 * POOL_CPAD), LANE), F32),
                        pltpu.VMEM((s_len + 16 * GRID_W, LANE), F32)],
    )(src)


def _group_weight(w_ref):
    return jnp.concatenate([w_ref[k] for k in range(N_CHIPS)], axis=0)


def _pool_mm_fwd(dm, wp, scale):
    nb, rows, _ = dm.shape
    _, ng, pq, pg = wp.shape
    cpg = pg // LANE
    tm = _row_tile(rows, 512)

    def body(d_ref, w_ref, s_ref, y_ref):
        _put_chunks(y_ref, _dot(_cat(d_ref, cpg), _group_weight(w_ref)) * s_ref[...], cpg)

    cspec = pl.BlockSpec((cpg, tm, LANE), lambda i, g: (g, i, 0))
    return _call(
        body, name="pool_mm_fwd", grid=(rows // tm, ng),
        in_specs=[cspec, pl.BlockSpec((N_CHIPS, None, pq, pg), lambda i, g: (0, g, 0, 0)),
                  pl.BlockSpec((1, pg), lambda i, g: (0, g))],
        out_specs=cspec, out_shape=jax.ShapeDtypeStruct((nb, rows, LANE), F32),
    )(dm, wp, scale)


def _pool_mm_bwd(dy, dm, wp, scale, gbuf, row0):
    nb, rows, _ = dm.shape
    _, ng, pq, pg = wp.shape
    cpg = pg // LANE
    tm = _row_tile(rows, 512)
    nt = rows // tm
    assert gbuf.shape[2] == 2 * pg and row0 % pq == 0

    def body(dy_ref, d_ref, w_ref, s_ref, buf_ref, dd_ref, dwp_ref, dsc_ref, acc):
        i = pl.program_id(1)

        @pl.when(i == 0)
        def _():
            acc[...] = jnp.zeros_like(acc)
            dsc_ref[...] = jnp.zeros_like(dsc_ref)

        dyv = _cat(dy_ref, cpg)
        dc = _cat(d_ref, cpg)
        w = _group_weight(w_ref)
        dsc_ref[...] += jnp.sum(dyv * _dot(dc, w), axis=0, keepdims=True)
        dyp = (dyv * s_ref[...]).astype(BF16)
        _put_chunks(dd_ref, _dot_nt(dyp, w), cpg)
        acc[...] += _dot_tn(dc, dyp)

        @pl.when(i == nt - 1)
        def _():
            dwp_ref[...] = acc[...].reshape(N_CHIPS, pq, pg)

    cspec = pl.BlockSpec((cpg, tm, LANE), lambda g, i: (g, i, 0))
    sspec = pl.BlockSpec((1, pg), lambda g, i: (0, g))
    return _call(
        body, name="pool_mm_bwd", grid=(ng, nt),
        in_specs=[cspec, cspec, pl.BlockSpec((N_CHIPS, None, pq, pg), lambda g, i: (0, g, 0, 0)), sspec, ANY_SPEC],
        out_specs=[cspec, pl.BlockSpec((N_CHIPS, pq, pg), lambda g, i: (0, row0 // pq + g // 2, g % 2)), sspec],
        out_shape=[jax.ShapeDtypeStruct((nb, rows, LANE), F32), jax.ShapeDtypeStruct(gbuf.shape, F32),
                   jax.ShapeDtypeStruct((1, ng * pg), F32)],
        scratch_shapes=[pltpu.VMEM((pg, pg), F32)],
        input_output_aliases={4: 1},
    )(dy, dm, wp, scale, gbuf)


def _adamw_math(w, g, m, v):
    nm = ADAM_B1 * m + (1.0 - ADAM_B1) * g
    nv = ADAM_B2 * v + (1.0 - ADAM_B2) * jnp.square(g)
    m_hat = nm / (1.0 - ADAM_B1 ** ADAM_STEP)
    v_hat = nv / (1.0 - ADAM_B2 ** ADAM_STEP)
    return -ADAM_LR * (m_hat / (jnp.sqrt(v_hat) + ADAM_EPS) + ADAM_WD * w), nm, nv


def _adamw_param(w3, m3, v3, gsrcs, pick, tm, name):
    n_blk, rows, cols = w3.shape
    ng = len(gsrcs)

    def body(*refs):
        w_ref, m_ref, v_ref = refs[:3]
        g_refs = refs[3:3 + ng]
        go_ref, d_ref, nm_ref, nv_ref = refs[3 + ng:]
        g = pick(pl.program_id(0), [r[...] for r in g_refs])
        go_ref[...] = g
        d_ref[...], nm_ref[...], nv_ref[...] = _adamw_math(w_ref[...], g, m_ref[...], v_ref[...])

    spec = pl.BlockSpec((None, tm, cols), lambda n, i: (n, i, 0))
    return _call(
        body, name=name, grid=(n_blk, rows // tm),
        in_specs=[spec] * 3 + [pl.BlockSpec(shape, imap) for _, shape, imap in gsrcs],
        out_specs=[spec] * 4, out_shape=[jax.ShapeDtypeStruct(w3.shape, F32)] * 4,
    )(w3, m3, v3, *[a for a, _, _ in gsrcs])


def _adamw_small(quads):
    n = len(quads)

    def body(*refs):
        ins, outs = refs[:4 * n], refs[4 * n:]
        for k in range(n):
            w, g, m, v = (r[...] for r in ins[4 * k:4 * k + 4])
            outs[3 * k][...], outs[3 * k + 1][...], outs[3 * k + 2][...] = _adamw_math(w, g, m, v)

    flat = [a for q in quads for a in q]
    res = _call(body, name="adamw_small", grid=(1,),
                in_specs=[pl.BlockSpec(a.shape, lambda i: (0, 0)) for a in flat],
                out_specs=[pl.BlockSpec(q[0].shape, lambda i: (0, 0)) for q in quads for _ in range(3)],
                out_shape=[jax.ShapeDtypeStruct(q[0].shape, F32) for q in quads for _ in range(3)])(*flat)
    return [tuple(res[3 * k:3 * k + 3]) for k in range(n)]


def _place():
    return lax.axis_index("x"), lax.axis_index("y"), lax.axis_index("c")


def _other_chips(x, y):
    return [(1 - x, y), (x, 1 - y), (1 - x, 1 - y)]


def _gather_chips(arrays, name):
    n = len(arrays)
    halves = [a.shape[0] // 2 for a in arrays]
    for a, h in zip(arrays, halves):
        assert 2 * h == a.shape[0] and h % (32 // a.dtype.itemsize) == 0
    lands = [jnp.broadcast_to(a[None], (N_CHIPS,) + a.shape) for a in arrays]

    def body(*refs):
        outs = refs[n:2 * n]
        send_sems, recv_sems = refs[2 * n:]
        x, y, c = _place()
        me = 2 * x + y
        chips = _other_chips(x, y)

        def mine(k):
            return pl.ds(c * halves[k], halves[k])

        def theirs(k):
            return pl.ds((1 - c) * halves[k], halves[k])

        def push(k, j, src, dst, to):
            return pltpu.make_async_remote_copy(src_ref=src, dst_ref=dst, send_sem=send_sems.at[6 * k + j],
                                                recv_sem=recv_sems.at[6 * k + j], device_id=to, device_id_type=MESH)

        started = []
        for j, (cx, cy) in enumerate(chips):
            for k in range(n):
                own = outs[k].at[me, mine(k)]
                cp = push(k, j, own, own, (cx, cy, c))
                cp.start()
                started.append(cp)
        for j, (cx, cy) in enumerate(chips):
            for k in range(n):
                slab = outs[k].at[2 * cx + cy, mine(k)]
                push(k, j, slab, slab, (x, y, c)).wait_recv()
                fwd = push(k, 3 + j, slab, slab, (x, y, 1 - c))
                fwd.start()
                started.append(fwd)
        for j, (cx, cy) in enumerate(chips):
            for k in range(n):
                slab = outs[k].at[2 * cx + cy, theirs(k)]
                push(k, 3 + j, slab, slab, (x, y, c)).wait_recv()
        for cp in started:
            cp.wait_send()

    return _call(
        body, name=name, in_specs=[ANY_SPEC] * n, out_specs=[ANY_SPEC] * n,
        out_shape=[jax.ShapeDtypeStruct(a.shape, a.dtype) for a in lands],
        input_output_aliases={k: k for k in range(n)},
        scratch_shapes=[pltpu.SemaphoreType.DMA((6 * n,)), pltpu.SemaphoreType.DMA((6 * n,))],
    )(*lands)


def _gather_devices(v):
    shape = v.shape

    def body(v_ref, out_ref, send_sems, recv_sems):
        x, y, c = _place()
        me = 4 * x + 2 * y + c
        out_ref[me] = v_ref[...]
        sends = []
        for k in range(1, N_DEV):
            to = (me + k) % N_DEV
            cp = pltpu.make_async_remote_copy(src_ref=v_ref, dst_ref=out_ref.at[me], send_sem=send_sems.at[k],
                                              recv_sem=recv_sems.at[k], device_id=(to // 4, (to // 2) % 2, to % 2),
                                              device_id_type=MESH)
            cp.start()
            sends.append(cp)
        for k in range(1, N_DEV):
            frm = (me + N_DEV - k) % N_DEV
            pltpu.make_async_remote_copy(src_ref=v_ref, dst_ref=out_ref.at[frm], send_sem=send_sems.at[k],
                                         recv_sem=recv_sems.at[k], device_id=(x, y, c), device_id_type=MESH).wait_recv()
        for cp in sends:
            cp.wait_send()

    vspec = pl.BlockSpec(memory_space=pltpu.VMEM)
    return _call(body, name="gather_devices", in_specs=[vspec], out_specs=vspec,
                 out_shape=jax.ShapeDtypeStruct((N_DEV,) + shape, F32),
                 scratch_shapes=[pltpu.SemaphoreType.DMA((N_DEV,)), pltpu.SemaphoreType.DMA((N_DEV,))])(v)


HBM_SPEC = pl.BlockSpec(memory_space=pltpu.HBM)
SEM_SPEC = pl.BlockSpec(memory_space=pltpu.SEMAPHORE)
SIDE_EFFECT = pltpu.SideEffectType.DATAFLOW_SIDE_EFFECTING


def _push_copies(src_refs, land_refs, send_sems, recv_sems, per_peer):
    x, y, c = _place()
    me = 2 * x + y
    copies = []
    for j, (cx, cy) in enumerate(_other_chips(x, y)):
        for k, (src, land) in enumerate(zip(src_refs, land_refs)):
            copies.append(pltpu.make_async_remote_copy(
                src_ref=src.at[2 * cx + cy] if per_peer else src, dst_ref=land.at[me], send_sem=send_sems.at[3 * k + j],
                recv_sem=recv_sems.at[3 * k + j], device_id=(cx, cy, c), device_id_type=MESH))
    return copies


def _push_start(srcs, lands, per_peer, after, name):
    n = len(srcs)

    def body(*refs):
        src_refs, land_refs = refs[:n], refs[n:2 * n]
        send_sems, recv_sems = refs[2 * n + 1], refs[2 * n + 2]
        token = refs[-1]
        for cp in _push_copies(src_refs, land_refs, send_sems, recv_sems, per_peer):
            cp.start()
        token[...] = jnp.zeros_like(token)

    bufs = [pltpu.with_memory_space_constraint(a, pltpu.HBM) for a in list(srcs) + list(lands)]
    res = _call(
        body, name=name,
        out_shape=[pltpu.SemaphoreType.DMA((3 * n,)), pltpu.SemaphoreType.DMA((3 * n,))]
        + [pltpu.HBM(a.shape, a.dtype) for a in bufs] + [jax.ShapeDtypeStruct((SUB, LANE), F32)],
        in_specs=[HBM_SPEC] * (2 * n) + [ANY_SPEC],
        out_specs=[SEM_SPEC, SEM_SPEC] + [HBM_SPEC] * (2 * n) + [pl.BlockSpec(memory_space=pltpu.VMEM)],
        input_output_aliases={i: 2 + i for i in range(2 * n)},
        compiler_params=pltpu.CompilerParams(has_side_effects=SIDE_EFFECT),
    )(*bufs, after)
    return res[0], res[1], list(res[2:2 + n]), list(res[2 + n:2 + 2 * n]), res[-1]


def _push_wait(send_sems, recv_sems, srcs, lands, per_peer, after, name):
    n = len(srcs)

    def body(*refs):
        src_refs, land_refs = refs[:n], refs[n:2 * n]
        send_sems, recv_sems = refs[2 * n], refs[2 * n + 1]
        for cp in _push_copies(src_refs, land_refs, send_sems, recv_sems, per_peer):
            cp.wait_send()
            cp.wait_recv()

    res = _call(
        body, name=name,
        out_shape=[pltpu.HBM(a.shape, a.dtype) for a in list(srcs) + list(lands)],
        in_specs=[HBM_SPEC] * (2 * n) + [SEM_SPEC, SEM_SPEC, ANY_SPEC],
        out_specs=[HBM_SPEC] * (2 * n),
        input_output_aliases={i: i for i in range(2 * n)},
        compiler_params=pltpu.CompilerParams(has_side_effects=SIDE_EFFECT),
    )(*srcs, *lands, send_sems, recv_sems, after)
    return list(res[n:])


def _sibling_swap(g):
    _, rows, w = g.shape
    half = rows // 2

    def body(g_ref, out_ref, send_sem, recv_sem):
        x, y, c = _place()
        cp = pltpu.make_async_remote_copy(src_ref=g_ref.at[:, pl.ds((1 - c) * half, half)], dst_ref=out_ref,
                                          send_sem=send_sem, recv_sem=recv_sem, device_id=(x, y, 1 - c), device_id_type=MESH)
        cp.start()
        cp.wait()

    return _call(body, name="rs_sibling_swap", in_specs=[ANY_SPEC], out_specs=ANY_SPEC,
                 out_shape=jax.ShapeDtypeStruct((N_CHIPS, half, w), F32),
                 scratch_shapes=[pltpu.SemaphoreType.DMA, pltpu.SemaphoreType.DMA])(g)


def _pair_add(g, got, place):
    _, rows, w = g.shape
    half = rows // 2
    tm = _row_tile(half, RS_TILE)
    nt = half // tm

    def body(p_ref, a_ref, b_ref, o_ref, own_ref):
        v = a_ref[...] + b_ref[...]
        o_ref[...] = v.astype(BF16)

        @pl.when(pl.program_id(1) == p_ref[1])
        def _():
            own_ref[...] = v

    return _call(
        body, name="rs_pair_add",
        grid_spec=pltpu.PrefetchScalarGridSpec(
            num_scalar_prefetch=1, grid=(nt, N_CHIPS),
            in_specs=[pl.BlockSpec((None, tm, w), lambda i, s, p: (s, p[0] * nt + i, 0)),
                      pl.BlockSpec((None, tm, w), lambda i, s, p: (s, i, 0))],
            out_specs=[pl.BlockSpec((None, tm, w), lambda i, s, p: (s, i, 0)),
                       pl.BlockSpec((tm, w), lambda i, s, p: (i, 0))]),
        out_shape=[jax.ShapeDtypeStruct((N_CHIPS, half, w), BF16), jax.ShapeDtypeStruct((half, w), F32)],
    )(place, g, got)


def _sum_chips(parts, own, place):
    _, half, w = parts.shape
    tm = _row_tile(half, RS_TILE)
    nt = half // tm

    def body(p_ref, parts_ref, own_ref, o_ref):
        me = p_ref[1]
        t = [jnp.where(me == q, own_ref[...], parts_ref[q].astype(F32)) for q in range(N_CHIPS)]
        o_ref[...] = (t[0] + t[1]) + (t[2] + t[3])

    return _call(
        body, name="rs_sum_chips",
        grid_spec=pltpu.PrefetchScalarGridSpec(
            num_scalar_prefetch=1, grid=(nt,),
            in_specs=[pl.BlockSpec((N_CHIPS, tm, w), lambda i, p: (0, i, 0)), pl.BlockSpec((tm, w), lambda i, p: (i, 0))],
            out_specs=pl.BlockSpec((tm, w), lambda i, p: (p[0] * nt + i, 0))),
        out_shape=jax.ShapeDtypeStruct((2 * half, w), F32),
    )(place, parts, own)


def _sibling_gather(red):
    rows, w = red.shape
    half = rows // 2

    def body(in_ref, out_ref, send_sem, recv_sem):
        x, y, c = _place()
        mine = out_ref.at[pl.ds(c * half, half)]
        cp = pltpu.make_async_remote_copy(src_ref=mine, dst_ref=mine, send_sem=send_sem, recv_sem=recv_sem,
                                          device_id=(x, y, 1 - c), device_id_type=MESH)
        cp.start()
        other = out_ref.at[pl.ds((1 - c) * half, half)]
        pltpu.make_async_remote_copy(src_ref=other, dst_ref=other, send_sem=send_sem, recv_sem=recv_sem,
                                     device_id=(x, y, c), device_id_type=MESH).wait_recv()
        cp.wait_send()

    return _call(body, name="rs_sibling_gather", in_specs=[ANY_SPEC], out_specs=ANY_SPEC,
                 out_shape=jax.ShapeDtypeStruct(red.shape, F32), input_output_aliases={0: 0},
                 scratch_shapes=[pltpu.SemaphoreType.DMA, pltpu.SemaphoreType.DMA])(red)


def _rs_begin(g, place, name):
    pair, own = _pair_add(g, _sibling_swap(g), place)
    send, recv, pair, parts, token = _push_start([pair], [jnp.zeros_like(pair)], True, own, name + "_start")
    return (send, recv, pair, parts, own), token


def _rs_end(state, place, after, name):
    send, recv, pair, parts, own = state
    (parts,) = _push_wait(send, recv, pair, parts, True, after, name + "_wait")
    return _sibling_gather(_sum_chips(parts, own, place))


WEIGHTS = ("c_ctx", "w_mod", "b_mod", "w_in", "w_out", "ln_g", "ln_b", "conv_w", "conv_b", "lru_wa", "lru_ba", "lru_wx",
           "lru_bx", "lru_lam", "pool_w", "pool_scale")
SMALL_GATHERED = ("conv_w", "lru_ba", "lru_bx", "lru_lam", "pool_scale")
SMALL_UPDATED = ("c_ctx", "b_mod", "ln_g", "ln_b", "conv_w", "conv_b", "lru_ba", "lru_bx", "lru_lam", "pool_scale")


def kernel(x, c, ctx, c_ctx, w_mod, b_mod, w_in, w_out, ln_g, ln_b, conv_w, conv_b, lru_wa, lru_ba, lru_wx, lru_bx, lru_lam, pool_w, pool_scale, loss_target, m_c_ctx, m_w_mod, m_b_mod, m_w_in, m_w_out, m_ln_g, m_ln_b, m_conv_w, m_conv_b, m_lru_wa, m_lru_ba, m_lru_wx, m_lru_bx, m_lru_lam, m_pool_w, m_pool_scale, v_c_ctx, v_w_mod, v_b_mod, v_w_in, v_w_out, v_ln_g, v_ln_b, v_conv_w, v_conv_b, v_lru_wa, v_lru_ba, v_lru_wx, v_lru_bx, v_lru_lam, v_pool_w, v_pool_scale):
    weights = dict(c_ctx=c_ctx, w_mod=w_mod, b_mod=b_mod, w_in=w_in, w_out=w_out, ln_g=ln_g, ln_b=ln_b, conv_w=conv_w,
                   conv_b=conv_b, lru_wa=lru_wa, lru_ba=lru_ba, lru_wx=lru_wx, lru_bx=lru_bx, lru_lam=lru_lam,
                   pool_w=pool_w, pool_scale=pool_scale)
    mom1 = dict(c_ctx=m_c_ctx, w_mod=m_w_mod, b_mod=m_b_mod, w_in=m_w_in, w_out=m_w_out, ln_g=m_ln_g, ln_b=m_ln_b,
                conv_w=m_conv_w, conv_b=m_conv_b, lru_wa=m_lru_wa, lru_ba=m_lru_ba, lru_wx=m_lru_wx, lru_bx=m_lru_bx,
                lru_lam=m_lru_lam, pool_w=m_pool_w, pool_scale=m_pool_scale)
    mom2 = dict(c_ctx=v_c_ctx, w_mod=v_w_mod, b_mod=v_b_mod, w_in=v_w_in, w_out=v_w_out, ln_g=v_ln_g, ln_b=v_ln_b,
                conv_w=v_conv_w, conv_b=v_conv_b, lru_wa=v_lru_wa, lru_ba=v_lru_ba, lru_wx=v_lru_wx, lru_bx=v_lru_bx,
                lru_lam=v_lru_lam, pool_w=v_pool_w, pool_scale=v_pool_scale)
    xs, cx, target = x[0], ctx[0], loss_target[0]
    s_len, d = xs.shape
    es = w_out.shape[1]
    e = es * N_CHIPS
    nb = e // LANE
    c3 = w_mod.shape[2]
    n4 = w_in.shape[2]
    pq, pg = pool_w.shape[2], pool_w.shape[3]
    ng = len(POOL_WINDOWS)
    width = n4
    assert width == d and 2 * pg == width and 2 * nb * LANE == N_CHIPS * width and d % (2 * N_CHIPS) == 0
    px, py, pc = _place()
    place = jnp.stack([pc, 2 * px + py]).astype(jnp.int32)
    cctx2 = c_ctx[None, :]

    eq = e // N_CHIPS
    small_rows = [(conv_w[0], 0), (lru_ba[0], CONV_TAPS), (lru_bx[0], CONV_TAPS + 2), (lru_lam[0], CONV_TAPS + 4),
                  (pool_scale, CONV_TAPS + 6)]
    small = _rows_kernel([(a, r, 0) for a, r in small_rows], 2 * SUB, eq, "pack_small_weights")
    wm_g, win0, sg = _gather_chips([w_mod.astype(BF16).reshape(DEPTH * d, c3), w_in[0].astype(BF16), small], "gather_weights0")
    later = [w_out[0].astype(BF16), w_in[1].astype(BF16), w_out[1].astype(BF16), pool_w.astype(BF16).reshape(ng * pq, pg)]
    w_send, w_recv, later, later_lands, w_token = _push_start(
        later, [jnp.broadcast_to(a[None], (N_CHIPS,) + a.shape) for a in later], False, sg, "gather_weights1_start")
    wm_all = wm_g.reshape(N_CHIPS, DEPTH, d, c3)
    full = {n: jnp.swapaxes(sg[:, r:r + a.shape[0]], 0, 1).reshape(a.shape[0], e)
            for n, (a, r) in zip(SMALL_GATHERED, small_rows)}
    wa_b, wx_b = lru_wa[0].astype(BF16), lru_wx[0].astype(BF16)
    lru_args = (full["conv_w"], conv_b, wa_b, wx_b, full["lru_ba"], full["lru_bx"], full["lru_lam"])
    scale_f = full["pool_scale"]

    mod = _mod_fwd(c + w_token[0:1, 0:1], cctx2, wm_all, b_mod[:, None, :])

    def mod_parts(l, row):
        v = mod[l, row]
        return v[None, :d], 1.0 + v[None, d:2 * d], v[None, 2 * d:]

    sh0, sc0, gt0 = mod_parts(0, 0)
    shc, scc, _ = mod_parts(0, 1)
    sh1, sc1, gt1 = mod_parts(1, 0)
    lg = [ln_g[l][None, :] for l in range(DEPTH)]
    lb = [ln_b[l][None, :] for l in range(DEPTH)]

    ug0 = _inproj_fwd(xs, sc0, sh0, win0, "inproj_fwd0")
    uc0 = _inproj_fwd(cx, scc, shc, win0[:2], "inproj_fwd_ctx")
    y0 = _rglru_fwd(ug0, uc0, *lru_args)
    wout0_g, win1, wout1_g, wp_g = _push_wait(w_send, w_recv, later, later_lands, False, y0, "gather_weights1_wait")
    win = [win0, win1]
    wout = [wout0_g.reshape(e, d), wout1_g.reshape(e, d)]
    wp = wp_g.reshape(N_CHIPS, ng, pq, pg)
    br0, x1 = _outproj_fwd(y0, ug0, xs, gt0, wout[0], lg[0], lb[0], None, "outproj_fwd0")
    ug1 = _inproj_fwd(x1, sc1, sh1, win[1], "inproj_fwd1")
    d1 = _pool_map(ug1, nb, False, True, "pool_fwd")
    y1 = _pool_mm_fwd(d1, wp, scale_f)
    br1, dxo, loss_part = _outproj_fwd(y1, ug1, x1, gt1, wout[1], lg[1], lb[1], target, "outproj_fwd1")
    loss = lax.psum(loss_part[0, 0] * (0.5 / d), ("x", "y", "c"))

    row_wout = d
    row_tail = d + es
    wq = 2 * (nb // N_CHIPS) * LANE * LANE // width
    whole = lambda r: (r + 2 * RS_TILE - 1) // (2 * RS_TILE) * (2 * RS_TILE)
    rows1 = whole(row_tail + pg // 2)
    rows0 = whole(row_tail + 2 * wq)
    gbuf1 = jnp.zeros((N_CHIPS, rows1, width), F32)
    gbuf0 = jnp.zeros((N_CHIPS, rows0, width), F32)

    dy1, dg1, dxres1, dbr1, dlg1, dlb1, dgt1 = _outproj_bwd(dxo, x1, br1, y1, ug1, gt1, lg[1], wout[1], "outproj_bwd1")
    gbuf1 = _outproj_bwd_w(y1, ug1, dbr1, gbuf1, row_wout, "outproj_bwd_w1")
    dd1, gbuf1, dscale = _pool_mm_bwd(dy1, d1, wp, scale_f, gbuf1, row_tail)
    du1 = _pool_map(dd1, nb, True, False, "pool_bwd")
    dx1, dsc1, dsh1 = _inproj_bwd_x([du1, dg1], x1, dxres1, sc1, win[1], "inproj_bwd_x1")
    gbuf1 = _inproj_bwd_w(x1, sc1, sh1, [du1, dg1], None, gbuf1, "inproj_bwd_w1")
    rs1, token1 = _rs_begin(gbuf1, place, "rs_exchange1")

    dy0, dg0, dxres0, dbr0, dlg0, dlb0, dgt0 = _outproj_bwd(dx1, xs, br0, y0, ug0, gt0 + token1[0:1, 0:1], lg[0], wout[0],
                                                            "outproj_bwd0")
    gbuf0 = _outproj_bwd_w(y0, ug0, dbr0, gbuf0, row_wout, "outproj_bwd_w0")
    du0, duc, dconv_w, dconv_b, dwa, dwx, dba, dbx, dlam = _rglru_bwd(ug0, uc0, dy0, *lru_args)
    dwin0c = _inproj_bwd_w(cx, scc, shc, [duc, jnp.zeros_like(duc)], None, None, "inproj_bwd_w_ctx")
    gbuf0 = _inproj_bwd_w(xs, sc0, sh0, [du0, dg0], dwin0c, gbuf0, "inproj_bwd_w0")

    def quarter(dw):
        t = dw.reshape(2, N_CHIPS, nb // N_CHIPS, LANE, LANE)
        return jnp.transpose(t, (1, 3, 0, 2, 4)).reshape(N_CHIPS, LANE, 2 * (nb // N_CHIPS) * LANE).reshape(N_CHIPS, wq, width)

    tail0 = jnp.concatenate([quarter(dwa), quarter(dwx)], axis=1)
    gbuf0 = lax.dynamic_update_slice(gbuf0, tail0, (0, row_tail, 0))
    red1 = _rs_end(rs1, place, gbuf0, "rs_exchange1")
    rs0, token0 = _rs_begin(gbuf0, place, "rs_exchange0")
    grad_x, dsc0, dsh0 = _inproj_bwd_x([du0, dg0], xs, dxres0, sc0 + token0[0:1, 0:1], win[0], "inproj_bwd_x0")
    dscc, dshc = _inproj_bwd_x([duc], cx, None, scc, win[0][:2], "inproj_bwd_x_ctx")

    k0 = VEC_KINDS
    vec = _rows_kernel(
        [(c, 0, 0), (dsh0, 1, 0), (dsc0, 1, d), (dgt0, 1, 2 * d), (dshc, 2, 0), (dscc, 2, d),
         (dsh1, 3, 0), (dsc1, 3, d), (dgt1, 3, 2 * d),
         (dconv_b, k0, 0), (dlg0, k0, e), (dscale, k0 + 1, 0), (dlg1, k0 + 1, e), (dlb0, k0 + 2, 0), (dlb1, k0 + 2, d),
         (dconv_w, k0 + 3, 0), (dba, k0 + 7, 0), (dbx, k0 + 9, 0), (dlam, k0 + 11, 0)], VEC_ROWS, 3 * d, "pack_vec")
    gt_all = jnp.swapaxes(_gather_devices(vec), 0, 1)
    g_wmod = _mod_bwd_shard(gt_all, cctx2, place, c3)
    g_bmod, g_cctx, g_small = _mod_bwd_rep(gt_all, cctx2, wm_all)
    red0 = _rs_end(rs0, place, g_bmod, "rs_exchange0")
    (rep,) = _gather_chips([red0[row_tail:row_tail + 2 * wq]], "gather_replicated")

    tmw = _row_tile(d, 256)
    red_src = lambda red, r0, tm: (red, (tm, width), lambda n, i: (r0 // tm + i, 0))
    by_layer = lambda n, gs: jnp.where(n == 0, gs[0], gs[1])
    outs = {}
    outs["w_in"] = _adamw_param(w_in, m_w_in, v_w_in, [red_src(red0, 0, tmw), red_src(red1, 0, tmw)], by_layer, tmw, "adamw_w_in")
    outs["w_out"] = _adamw_param(w_out, m_w_out, v_w_out, [red_src(red0, row_wout, tmw), red_src(red1, row_wout, tmw)],
                                 by_layer, tmw, "adamw_w_out")
    outs["w_mod"] = _adamw_param(w_mod, m_w_mod, v_w_mod, [(g_wmod, (None, tmw, c3), lambda n, i: (n, i, 0))],
                                 lambda n, gs: gs[0], tmw, "adamw_w_mod")
    pw = [a.reshape(ng, pq, pg) for a in (pool_w, m_pool_w, v_pool_w)]
    outs["pool_w"] = [o.reshape(pool_w.shape) for o in _adamw_param(
        *pw, [(red1, (pq, pg), lambda n, i: (row_tail // pq + n // 2, n % 2))], lambda n, gs: gs[0], pq, "adamw_pool_w")]
    bq = nb // N_CHIPS
    rep_src = lambda r0: (rep, (None, LANE, bq * LANE), lambda n, i: (n % N_CHIPS, r0 // LANE, n // N_CHIPS))
    stack = lambda n, gs: jnp.concatenate([gs[0][:, k * LANE:(k + 1) * LANE] for k in range(bq)], axis=0)
    for name, r0, trio in (("lru_wa", 0, (lru_wa, m_lru_wa, v_lru_wa)), ("lru_wx", wq, (lru_wx, m_lru_wx, v_lru_wx))):
        blocks = [a.reshape(2 * N_CHIPS, bq * LANE, LANE) for a in trio]
        outs[name] = [o.reshape(lru_wa.shape) for o in _adamw_param(*blocks, [rep_src(r0)], stack, bq * LANE, "adamw_" + name)]

    g_small = dict(g_small, c_ctx=g_cctx, b_mod=g_bmod)
    for n in SMALL_GATHERED:
        g_small[n] = lax.dynamic_slice_in_dim(g_small[n], place[1] * eq, eq, axis=1)
    as2d = lambda a: a.reshape(-1, a.shape[-1])
    quads = [(as2d(weights[n]), g_small[n], as2d(mom1[n]), as2d(mom2[n])) for n in SMALL_UPDATED]
    for n, (q, res) in zip(SMALL_UPDATED, zip(quads, _adamw_small(quads))):
        outs[n] = [a.reshape(weights[n].shape) for a in (q[1],) + res]

    result = [loss, grad_x[None]]
    for j in range(4):
        result += [outs[n][j] for n in WEIGHTS]
    return tuple(result)
```

```python
import jax
import jax.numpy as jnp
from jax import lax
from jax.experimental import pallas as pl
from jax.experimental.pallas import tpu as pltpu

F32 = jnp.float32
BF16 = jnp.bfloat16
LANE = 128
SUB = 8
GRID_W = 64
POOL_WINDOWS = (2, 4, 8, 16)
LRU_C = 8.0
DEPTH = 2
ALPHA = float((2 * DEPTH) ** 0.25)
LN_EPS = 1e-5
ADAM_LR, ADAM_B1, ADAM_B2, ADAM_EPS, ADAM_WD, ADAM_STEP = 0.001, 0.9, 0.999, 1e-08, 0.01, 10
N_CHIPS = 4
N_DEV = 8
MESH = pl.DeviceIdType.MESH
ROW_TILE = 512
GATE_TILE = 512
GATE_UNROLL = 1
CONV_TAPS = 4
CONV_LEFT = 2
PAD = 8
SCAN_UNROLL = 8
RS_TILE = 128
LN_ROWS = 128
POOL_CPAD = 16
VEC_KINDS = 4


def _call(body, **kw):
    return pl.pallas_call(body, **kw)


def _dot(a, b):
    return jnp.dot(a, b, preferred_element_type=F32)


def _dot_nt(a, b):
    return lax.dot_general(a, b, (((1,), (1,)), ((), ())), preferred_element_type=F32)


def _dot_tn(a, b):
    return lax.dot_general(a, b, (((0,), (0,)), ((), ())), preferred_element_type=F32)


def _sigmoid(v):
    return 0.5 * (jnp.tanh(0.5 * v) + 1.0)


def _silu(v):
    return v * _sigmoid(v)


def _dsilu(v):
    s = _sigmoid(v)
    return s * (1.0 + v * (1.0 - s))


def _log_sigmoid(v):
    z = jnp.exp(-jnp.abs(v))
    return jnp.minimum(v, 0.0) - jnp.where(z < 1e-4, z * (1.0 - 0.5 * z), jnp.log(1.0 + z))


def _one_minus_sq(la, a):
    return -jnp.tanh(la) * (a * a + 1.0)


def _cat(ref, n):
    return jnp.concatenate([ref[k] for k in range(n)], axis=1)


def _put_chunks(ref, val, n, base=0):
    for k in range(n):
        ref[base + k] = val[:, k * LANE:(k + 1) * LANE].astype(ref.dtype)


def _row_tile(rows, want):
    t = min(rows, want)
    assert rows % t == 0
    return t


ANY_SPEC = pl.BlockSpec(memory_space=pl.ANY)


def _mod_fwd(cvec, cctx, wm, bm):
    ns, nl, d, c3 = wm.shape

    def body(c_ref, cx_ref, w_ref, b_ref, o_ref):
        cc = jnp.concatenate([c_ref[...], cx_ref[...], jnp.zeros((SUB - 2, d), F32)], axis=0)
        o_ref[...] = _dot(_silu(cc).astype(BF16), w_ref[...]) + b_ref[...]

    return _call(
        body, name="mod_fwd", grid=(nl, ns),
        in_specs=[pl.BlockSpec((1, d), lambda l, s: (0, 0)),
                  pl.BlockSpec((1, d), lambda l, s: (0, 0)),
                  pl.BlockSpec((None, None, d, c3), lambda l, s: (s, l, 0, 0)),
                  pl.BlockSpec((None, 1, c3), lambda l, s: (l, 0, s))],
        out_specs=pl.BlockSpec((None, 8, c3), lambda l, s: (l, 0, s)),
        out_shape=jax.ShapeDtypeStruct((nl, 8, ns * c3), F32),
    )(cvec, cctx, wm, bm)


def _rows_kernel(parts, rows, cols, name):
    def body(*refs):
        o_ref = refs[-1]
        o_ref[...] = jnp.zeros_like(o_ref)
        for ref, (a, r0, c0) in zip(refs[:-1], parts):
            for k in range(a.shape[0]):
                o_ref[r0 + k:r0 + k + 1, c0:c0 + a.shape[1]] = ref[k:k + 1, :]

    return _call(body, name=name, grid=(1,),
                 in_specs=[pl.BlockSpec(a.shape, lambda i: (0, 0)) for a, _, _ in parts],
                 out_specs=pl.BlockSpec((rows, cols), lambda i: (0, 0)),
                 out_shape=jax.ShapeDtypeStruct((rows, cols), F32))(*[a for a, _, _ in parts])


def _mod_bwd_shard(gt, cctx, place, c3):
    d = cctx.shape[1]

    def body(p_ref, cs_ref, dm_ref, dmx_ref, cx_ref, o_ref):
        l = pl.program_id(0)
        lhs = jnp.concatenate([_silu(cs_ref[...]), _silu(cx_ref[...]), jnp.zeros((7, d), F32)], axis=0).astype(BF16)
        dmx = jnp.where(l == 0, jnp.sum(dmx_ref[...], axis=0, keepdims=True), 0.0)
        rhs = jnp.concatenate([dm_ref[...], dmx, jnp.zeros((7, c3), F32)], axis=0).astype(BF16)
        o_ref[...] = _dot_tn(lhs, rhs)

    return _call(
        body, name="mod_bwd_shard",
        grid_spec=pltpu.PrefetchScalarGridSpec(
            num_scalar_prefetch=1, grid=(DEPTH,),
            in_specs=[pl.BlockSpec((None, N_DEV, d), lambda l, p: (0, 0, 0)),
                      pl.BlockSpec((None, N_DEV, c3), lambda l, p: (1 + 2 * l, 0, p[1])),
                      pl.BlockSpec((None, N_DEV, c3), lambda l, p: (2, 0, p[1])),
                      pl.BlockSpec((1, d), lambda l, p: (0, 0))],
            out_specs=pl.BlockSpec((None, d, c3), lambda l, p: (l, 0, 0))),
        out_shape=jax.ShapeDtypeStruct((DEPTH, d, c3), F32),
    )(place, gt, gt, gt, cctx)


def _small_layout(d, e):
    k = VEC_KINDS
    return {
        "conv_b": ((1, e), [(0, k, 0)]),
        "ln_g": ((2, d), [(0, k, e), (1, k + 1, e)]),
        "pool_scale": ((1, e), [(0, k + 1, 0)]),
        "ln_b": ((2, d), [(0, k + 2, 0), (1, k + 2, d)]),
        "conv_w": ((CONV_TAPS, e), [(t, k + 3 + t, 0) for t in range(CONV_TAPS)]),
        "lru_ba": ((2, e), [(j, k + 7 + j, 0) for j in range(2)]),
        "lru_bx": ((2, e), [(j, k + 9 + j, 0) for j in range(2)]),
        "lru_lam": ((2, e), [(j, k + 11 + j, 0) for j in range(2)]),
    }


VEC_ROWS = 24


def _mod_bwd_rep(gt, cctx, wm):
    ns, _, d, c3 = wm.shape
    layout = _small_layout(d, ns * c3 - d)
    names = list(layout)

    def body(g_ref, cx_ref, w_ref, db_ref, dc_ref, *small_refs):
        dm0 = jnp.sum(g_ref[1], axis=0, keepdims=True)
        dmx = jnp.sum(g_ref[2], axis=0, keepdims=True)
        dm1 = jnp.sum(g_ref[3], axis=0, keepdims=True)
        db_ref[0:1, :] = dm0 + dmx
        db_ref[1:2, :] = dm1
        dmxb = jnp.broadcast_to(dmx, (SUB, ns * c3)).astype(BF16)
        acc = jnp.zeros((SUB, d), F32)
        for s in range(ns):
            acc = acc + _dot_nt(dmxb[:, s * c3:(s + 1) * c3], w_ref[s])
        dc_ref[...] = acc[0:1, :] * _dsilu(cx_ref[...])
        for ref, name in zip(small_refs, names):
            shape, places = layout[name]
            for arr_row, vec_row, col0 in places:
                total = jnp.sum(g_ref[vec_row], axis=0, keepdims=True)
                ref[arr_row:arr_row + 1, :] = total[:, col0:col0 + shape[1]]

    outs = _call(
        body, name="mod_bwd_rep", grid=(1,),
        in_specs=[pl.BlockSpec(gt.shape, lambda i: (0, 0, 0)),
                  pl.BlockSpec((1, d), lambda i: (0, 0)),
                  pl.BlockSpec((ns, None, d, c3), lambda i: (0, 0, 0, 0))],
        out_specs=[pl.BlockSpec((DEPTH, ns * c3), lambda i: (0, 0)), pl.BlockSpec((1, d), lambda i: (0, 0))]
        + [pl.BlockSpec(layout[n][0], lambda i: (0, 0)) for n in names],
        out_shape=[jax.ShapeDtypeStruct((DEPTH, ns * c3), F32), jax.ShapeDtypeStruct((1, d), F32)]
        + [jax.ShapeDtypeStruct(layout[n][0], F32) for n in names],
    )(gt, cctx, wm)
    return outs[0], outs[1], dict(zip(names, outs[2:]))


def _inproj_fwd(xin, sc1, sh, w, name):
    rows, d = xin.shape
    ns, _, n4 = w.shape
    cpb = n4 // LANE
    tm = _row_tile(rows, 256)

    def body(x_ref, sc_ref, sh_ref, w_ref, o_ref):
        h = (x_ref[...] * sc_ref[...] + sh_ref[...]).astype(BF16)
        for s in range(ns):
            _put_chunks(o_ref, _dot(h, w_ref[s]), cpb, base=s * cpb)

    return _call(
        body, name=name, grid=(rows // tm,),
        in_specs=[pl.BlockSpec((tm, d), lambda i: (i, 0)),
                  pl.BlockSpec((1, d), lambda i: (0, 0)),
                  pl.BlockSpec((1, d), lambda i: (0, 0)),
                  pl.BlockSpec((ns, d, n4), lambda i: (0, 0, 0))],
        out_specs=pl.BlockSpec((ns * cpb, tm, LANE), lambda i: (0, i, 0)),
        out_shape=jax.ShapeDtypeStruct((ns * cpb, rows, LANE), F32),
    )(xin, sc1, sh, w)


def _inproj_bwd_x(dparts, xin, dxres, sc1, w, name):
    rows, d = xin.shape
    npart = len(dparts)
    e = dparts[0].shape[1]
    ns, _, n4 = w.shape
    per = e // n4
    assert per * npart == ns
    tm = _row_tile(rows, 256)
    has_res = dxres is not None

    def body(*refs):
        dp = refs[:npart]
        x_ref, sc_ref, w_ref = refs[npart:npart + 3]
        rest = refs[npart + 3:]
        if has_res:
            res_ref, dx_ref, dsc_ref, dsh_ref = rest
        else:
            dsc_ref, dsh_ref = rest
        i = pl.program_id(0)
        dh = jnp.zeros((tm, d), F32)
        for p in range(npart):
            v = dp[p][...]
            for q in range(per):
                dh = dh + _dot_nt(v[:, q * n4:(q + 1) * n4], w_ref[p * per + q])

        @pl.when(i == 0)
        def _():
            dsc_ref[...] = jnp.zeros_like(dsc_ref)
            dsh_ref[...] = jnp.zeros_like(dsh_ref)

        dsc_ref[...] += jnp.sum(dh * x_ref[...], axis=0, keepdims=True)
        dsh_ref[...] += jnp.sum(dh, axis=0, keepdims=True)
        if has_res:
            dx_ref[...] = res_ref[...] + dh * sc_ref[...]

    row_spec = pl.BlockSpec((tm, d), lambda i: (i, 0))
    vec_spec = pl.BlockSpec((1, d), lambda i: (0, 0))
    in_specs = [pl.BlockSpec((tm, e), lambda i: (i, 0))] * npart + [row_spec, vec_spec,
                                                                     pl.BlockSpec((ns, d, n4), lambda i: (0, 0, 0))]
    args = list(dparts) + [xin, sc1, w]
    out_specs, out_shape = [vec_spec, vec_spec], [jax.ShapeDtypeStruct((1, d), F32)] * 2
    if has_res:
        in_specs.append(row_spec)
        args.append(dxres)
        out_specs = [row_spec] + out_specs
        out_shape = [jax.ShapeDtypeStruct((rows, d), F32)] + out_shape
    return _call(body, name=name, grid=(rows // tm,), in_specs=in_specs, out_specs=out_specs, out_shape=out_shape)(*args)


def _inproj_bwd_w(xin, sc1, sh, dparts, init, gbuf, name):
    rows, d = xin.shape
    npart = len(dparts)
    e = dparts[0].shape[1]
    n4 = e // 2
    ns = 2 * npart
    tm = _row_tile(rows, 1024)
    nt = rows // tm
    has_init = init is not None
    into = gbuf is not None
    assert not into or (ns == N_CHIPS and gbuf.shape[2] == n4)

    def body(*refs):
        x_ref, sc_ref, sh_ref = refs[:3]
        dp = refs[3:3 + npart]
        init_ref = refs[3 + npart] if has_init else None
        o_ref = refs[-1]
        s, i = pl.program_id(0), pl.program_id(1)
        h = (x_ref[...] * sc_ref[...] + sh_ref[...]).astype(BF16)

        @pl.when(i == 0)
        def _():
            o_ref[...] = init_ref[...] if has_init else jnp.zeros_like(o_ref)

        for p in range(npart):
            @pl.when(s // 2 == p)
            def _(p=p):
                o_ref[...] += _dot_tn(h, dp[p][...])

    in_specs = [pl.BlockSpec((tm, d), lambda s, i: (i, 0)),
                pl.BlockSpec((1, d), lambda s, i: (0, 0)),
                pl.BlockSpec((1, d), lambda s, i: (0, 0))]
    in_specs += [pl.BlockSpec((tm, n4), lambda s, i: (i, s % 2))] * npart
    args = [xin, sc1, sh] + list(dparts)
    o_spec = pl.BlockSpec((None, d, n4), lambda s, i: (s, 0, 0))
    if has_init:
        in_specs.append(o_spec)
        args.append(init)
    extra = {}
    if into:
        in_specs.append(ANY_SPEC)
        args.append(gbuf)
        extra = dict(input_output_aliases={len(args) - 1: 0})
    out_shape = jax.ShapeDtypeStruct(gbuf.shape if into else (ns, d, n4), F32)
    return _call(body, name=name, grid=(ns, nt), in_specs=in_specs, out_specs=o_spec, out_shape=out_shape, **extra)(*args)


def _ln_stats(r):
    mu = jnp.mean(r, axis=-1, keepdims=True)
    var = jnp.mean(jnp.square(r - mu), axis=-1, keepdims=True)
    rstd = lax.rsqrt(var + LN_EPS)
    return (r - mu) * rstd, rstd


def _outproj_fwd(y, ug, xin, gt, wout, lg, lb, target, name):
    nch, rows, _ = y.shape
    e, d = wout.shape
    tm = _row_tile(rows, 256)
    with_loss = target is not None

    def body(*refs):
        y_ref, g_ref, x_ref, gt_ref, w_ref, lg_ref, lb_ref = refs[:7]
        if with_loss:
            t_ref, br_ref, dxo_ref, loss_ref = refs[7:]
        else:
            br_ref, xo_ref = refs[7:]
        z = jnp.concatenate([(y_ref[k] * _silu(g_ref[k])).astype(BF16) for k in range(nch)], axis=1)
        br_ref[...] = _dot(z, w_ref[...])
        if with_loss:
            @pl.when(pl.program_id(0) == 0)
            def _():
                loss_ref[...] = jnp.zeros_like(loss_ref)

        def norm(j, c):
            rows = pl.ds(pl.multiple_of(j * LN_ROWS, LN_ROWS), LN_ROWS)
            xhat, _ = _ln_stats(ALPHA * x_ref[rows, :] + gt_ref[...] * br_ref[rows, :])
            xo = xhat * lg_ref[...] + lb_ref[...]
            if with_loss:
                err = xo - t_ref[rows, :]
                dxo_ref[rows, :] = err * (1.0 / d)
                col = jnp.sum(err * err, axis=0, keepdims=True)
                loss_ref[...] += sum(col[:, k * LANE:(k + 1) * LANE] for k in range(d // LANE))
            else:
                xo_ref[rows, :] = xo
            return c

        lax.fori_loop(0, tm // LN_ROWS, norm, 0)

    chunk_spec = pl.BlockSpec((nch, tm, LANE), lambda i: (0, i, 0))
    g_spec = pl.BlockSpec((nch, tm, LANE), lambda i: (1, i, 0))
    row_spec = pl.BlockSpec((tm, d), lambda i: (i, 0))
    vec_spec = pl.BlockSpec((1, d), lambda i: (0, 0))
    in_specs = [chunk_spec, g_spec, row_spec, vec_spec, pl.BlockSpec((e, d), lambda i: (0, 0)), vec_spec, vec_spec]
    args = [y, ug, xin, gt, wout, lg, lb]
    out_specs = [row_spec, row_spec]
    out_shape = [jax.ShapeDtypeStruct((rows, d), F32)] * 2
    if with_loss:
        in_specs.append(row_spec)
        args.append(target)
        out_specs.append(pl.BlockSpec((1, LANE), lambda i: (0, 0)))
        out_shape.append(jax.ShapeDtypeStruct((1, LANE), F32))
    return _call(body, name=name, grid=(rows // tm,), in_specs=in_specs, out_specs=out_specs, out_shape=out_shape)(*args)


def _outproj_bwd(dxo, xin, br, y, ug, gt, lg, wout, name):
    nch, rows, _ = y.shape
    e, d = wout.shape
    tm = _row_tile(rows, 256)

    def body(dxo_ref, x_ref, br_ref, y_ref, g_ref, gt_ref, lg_ref, w_ref,
             dy_ref, dg_ref, dxres_ref, dbr_ref, dlg_ref, dlb_ref, dgt_ref):
        @pl.when(pl.program_id(0) == 0)
        def _():
            dlg_ref[...] = jnp.zeros_like(dlg_ref)
            dlb_ref[...] = jnp.zeros_like(dlb_ref)
            dgt_ref[...] = jnp.zeros_like(dgt_ref)

        def norm_bwd(j, c):
            rows = pl.ds(pl.multiple_of(j * LN_ROWS, LN_ROWS), LN_ROWS)
            dxo_v = dxo_ref[rows, :]
            brv = br_ref[rows, :]
            xhat, rstd = _ln_stats(ALPHA * x_ref[rows, :] + gt_ref[...] * brv)
            dxh = dxo_v * lg_ref[...]
            dr = rstd * (dxh - jnp.mean(dxh, axis=-1, keepdims=True) - xhat * jnp.mean(dxh * xhat, axis=-1, keepdims=True))
            dlg_ref[...] += jnp.sum(dxo_v * xhat, axis=0, keepdims=True)
            dlb_ref[...] += jnp.sum(dxo_v, axis=0, keepdims=True)
            dgt_ref[...] += jnp.sum(dr * brv, axis=0, keepdims=True)
            dxres_ref[rows, :] = ALPHA * dr
            dbr_ref[rows, :] = (gt_ref[...] * dr).astype(BF16)
            return c

        lax.fori_loop(0, tm // LN_ROWS, norm_bwd, 0)
        dz = _dot_nt(dbr_ref[...], w_ref[...])
        for k in range(nch):
            dzk = dz[:, k * LANE:(k + 1) * LANE]
            gk = g_ref[k]
            dy_ref[k] = dzk * _silu(gk)
            dg_ref[:, k * LANE:(k + 1) * LANE] = (dzk * y_ref[k] * _dsilu(gk)).astype(BF16)

    chunk_spec = pl.BlockSpec((nch, tm, LANE), lambda i: (0, i, 0))
    g_spec = pl.BlockSpec((nch, tm, LANE), lambda i: (1, i, 0))
    row_spec = pl.BlockSpec((tm, d), lambda i: (i, 0))
    vec_spec = pl.BlockSpec((1, d), lambda i: (0, 0))
    return _call(
        body, name=name, grid=(rows // tm,),
        in_specs=[row_spec, row_spec, row_spec, chunk_spec, g_spec, vec_spec, vec_spec, pl.BlockSpec((e, d), lambda i: (0, 0))],
        out_specs=[chunk_spec, pl.BlockSpec((tm, e), lambda i: (i, 0)), row_spec, row_spec, vec_spec, vec_spec, vec_spec],
        out_shape=[jax.ShapeDtypeStruct((nch, rows, LANE), F32), jax.ShapeDtypeStruct((rows, e), BF16),
                   jax.ShapeDtypeStruct((rows, d), F32), jax.ShapeDtypeStruct((rows, d), BF16)]
        + [jax.ShapeDtypeStruct((1, d), F32)] * 3,
    )(dxo, xin, br, y, ug, gt, lg, wout)


def _outproj_bwd_w(y, ug, dbr, gbuf, row0, name):
    nch, rows, _ = y.shape
    d = dbr.shape[1]
    e = nch * LANE
    es = e // N_CHIPS
    tm = _row_tile(rows, 512)
    assert gbuf.shape[2] == d and row0 % es == 0

    def body(y_ref, g_ref, dbr_ref, buf_ref, o_ref):
        @pl.when(pl.program_id(0) == 0)
        def _():
            o_ref[...] = jnp.zeros_like(o_ref)

        z = jnp.concatenate([(y_ref[k] * _silu(g_ref[k])).astype(BF16) for k in range(nch)], axis=1)
        o_ref[...] += _dot_tn(z, dbr_ref[...]).reshape(N_CHIPS, es, d)

    return _call(
        body, name=name, grid=(rows // tm,),
        in_specs=[pl.BlockSpec((nch, tm, LANE), lambda i: (0, i, 0)),
                  pl.BlockSpec((nch, tm, LANE), lambda i: (1, i, 0)),
                  pl.BlockSpec((tm, d), lambda i: (i, 0)),
                  ANY_SPEC],
        out_specs=pl.BlockSpec((N_CHIPS, es, d), lambda i: (0, row0 // es, 0)),
        out_shape=jax.ShapeDtypeStruct(gbuf.shape, F32),
        input_output_aliases={3: 0},
    )(y, ug, dbr, gbuf)


def _scan(a_ref, b_ref, h_ref, *, length, init, reverse, a_shift, store):
    nblk = length // SUB
    assert nblk % SCAN_UNROLL == 0
    row = lax.broadcasted_iota(jnp.int32, (SUB, LANE), 0)
    last = 0 if reverse else SUB - 1
    edges = [(row >= SUB - k) if reverse else (row < k) for k in (1, 2, 4)]

    def local_scan(a, b):
        for k, edge in zip((1, 2, 4), edges):
            sh = (SUB - k) if reverse else k
            b = b + a * jnp.where(edge, 0.0, pltpu.roll(b, sh, 0))
            a = a * jnp.where(edge, 1.0, pltpu.roll(a, sh, 0))
        return a, b

    def step(i, carry):
        base = pl.multiple_of(((nblk // SCAN_UNROLL - 1 - i) if reverse else i) * (SCAN_UNROLL * SUB), SCAN_UNROLL * SUB)
        order = range(SCAN_UNROLL - 1, -1, -1) if reverse else range(SCAN_UNROLL)
        loaded = [(a_ref[pl.ds(PAD + base + j * SUB + a_shift, SUB), :], b_ref[pl.ds(PAD + base + j * SUB, SUB), :])
                  for j in order]
        scanned = [local_scan(a, b) for a, b in loaded]
        for j, (a, b) in zip(order, scanned):
            if store:
                h_ref[pl.ds(PAD + base + j * SUB, SUB), :] = b + a * carry
            a_l = jnp.broadcast_to(a[last:last + 1, :], (SUB, LANE))
            b_l = jnp.broadcast_to(b[last:last + 1, :], (SUB, LANE))
            carry = b_l + a_l * carry
        return carry

    carry = lax.fori_loop(0, nblk // SCAN_UNROLL, step, jnp.broadcast_to(init, (SUB, LANE)))
    return carry[0:1, :]


def _conv_fwd(src_ref, upad, u_ref, cw, cb, length):
    zeros = jnp.zeros((PAD, LANE), F32)
    upad[pl.ds(0, PAD), :] = zeros
    upad[pl.ds(PAD + length, PAD), :] = zeros
    rt = _row_tile(length, ROW_TILE)

    def copy(i, c):
        t0 = pl.multiple_of(i * rt, rt)
        upad[pl.ds(PAD + t0, rt), :] = src_ref[pl.ds(t0, rt), :]
        return c

    lax.fori_loop(0, length // rt, copy, 0)

    def tile(i, c):
        t0 = pl.multiple_of(i * rt, rt)
        acc = jnp.zeros((rt, LANE), F32)
        for k in range(CONV_TAPS):
            acc = acc + upad[pl.ds(t0 + PAD - CONV_LEFT + k, rt), :] * cw[k:k + 1, :]
        u_ref[pl.ds(t0, rt), :] = acc + cb
        return c

    lax.fori_loop(0, length // rt, tile, 0)


def _gates_fwd(u_ref, a_ref, b_ref, wa, wx, ba, bx, ls, length, keep=None):
    rt = _row_tile(length, GATE_TILE)

    def tile(i, c):
        t0 = pl.multiple_of(i * rt, rt)
        ut = u_ref[pl.ds(t0, rt), :]
        ub = ut.astype(BF16)
        r = _sigmoid(_dot(ub, wa) + ba)
        ig = _sigmoid(_dot(ub, wx) + bx)
        if keep is not None:
            keep[0][pl.ds(t0, rt), :] = r
            keep[1][pl.ds(t0, rt), :] = ig
        la = (LRU_C * r) * ls
        a = jnp.exp(la)
        a_ref[pl.ds(PAD + t0, rt), :] = a
        b_ref[pl.ds(PAD + t0, rt), :] = jnp.sqrt(_one_minus_sq(la, a)) * (ig * ut)
        return c

    lax.fori_loop(0, length // rt, tile, 0, unroll=min(GATE_UNROLL, length // rt))


def _lru_specs():
    return [pl.BlockSpec((CONV_TAPS, LANE), lambda n: (0, n)),
            pl.BlockSpec((1, LANE), lambda n: (0, n)),
            pl.BlockSpec((2, None, LANE, LANE), lambda n: (0, n, 0, 0)),
            pl.BlockSpec((2, None, LANE, LANE), lambda n: (0, n, 0, 0)),
            pl.BlockSpec((2, LANE), lambda n: (0, n)),
            pl.BlockSpec((2, LANE), lambda n: (0, n)),
            pl.BlockSpec((2, LANE), lambda n: (0, n))]


def _rglru_fwd(ug, uc, conv_w, conv_b, wa, wx, ba, bx, lam):
    nb = uc.shape[0]
    s_len, t_len = ug.shape[1], uc.shape[1]

    def body(u0_ref, uc0_ref, cw_ref, cb_ref, wa_ref, wx_ref, ba_ref, bx_ref, lam_ref, y_ref,
             upad, ubuf, abuf, hbuf):
        cw, cb = cw_ref[...], cb_ref[...]
        lsig = _log_sigmoid(lam_ref[...])
        zero = jnp.zeros((1, LANE), F32)
        _conv_fwd(uc0_ref, upad, ubuf, cw, cb, t_len)
        h0 = []
        for dr in range(2):
            _gates_fwd(ubuf, abuf, hbuf, wa_ref[dr], wx_ref[dr], ba_ref[dr:dr + 1, :], bx_ref[dr:dr + 1, :],
                       lsig[dr:dr + 1, :], t_len)
            h0.append(_scan(abuf, hbuf, hbuf, length=t_len, init=zero, reverse=(dr == 1), a_shift=0, store=False))
        _conv_fwd(u0_ref, upad, ubuf, cw, cb, s_len)
        rt = _row_tile(s_len, ROW_TILE)
        for dr in range(2):
            _gates_fwd(ubuf, abuf, hbuf, wa_ref[dr], wx_ref[dr], ba_ref[dr:dr + 1, :], bx_ref[dr:dr + 1, :],
                       lsig[dr:dr + 1, :], s_len)
            _scan(abuf, hbuf, hbuf, length=s_len, init=h0[dr], reverse=(dr == 1), a_shift=0, store=True)

            def acc(i, c, dr=dr):
                t0 = pl.multiple_of(i * rt, rt)
                h = hbuf[pl.ds(PAD + t0, rt), :]
                if dr == 0:
                    y_ref[pl.ds(t0, rt), :] = h
                else:
                    y_ref[pl.ds(t0, rt), :] += h
                return c

            lax.fori_loop(0, s_len // rt, acc, 0)

    seq = pltpu.VMEM((s_len + 2 * PAD, LANE), F32)
    return _call(
        body, name="rglru_fwd", grid=(nb,),
        in_specs=[pl.BlockSpec((None, s_len, LANE), lambda n: (n, 0, 0)),
                  pl.BlockSpec((None, t_len, LANE), lambda n: (n, 0, 0))] + _lru_specs(),
        out_specs=pl.BlockSpec((None, s_len, LANE), lambda n: (n, 0, 0)),
        out_shape=jax.ShapeDtypeStruct((nb, s_len, LANE), F32),
        scratch_shapes=[seq, pltpu.VMEM((s_len, LANE), F32), seq, seq],
    )(ug, uc, conv_w, conv_b, wa, wx, ba, bx, lam)


def _rglru_bwd(ug, uc, dy, conv_w, conv_b, wa, wx, ba, bx, lam):
    nb = uc.shape[0]
    e = nb * LANE
    s_len, t_len = ug.shape[1], uc.shape[1]

    def body(u0_ref, uc0_ref, dy_ref, cw_ref, cb_ref, wa_ref, wx_ref, ba_ref, bx_ref, lam_ref,
             du_ref, duc_ref, dcw_ref, dcb_ref, dwa_ref, dwx_ref, dba_ref, dbx_ref, dlam_ref,
             upad, ubuf, abuf, hbuf, lbuf, dubuf, rbuf, ibuf, cpad, cu, ca0, ch0, ca1, ch1, cr0, ci0, cr1, ci1):
        cw, cb = cw_ref[...], cb_ref[...]
        lam_v = lam_ref[...]
        lsig = _log_sigmoid(lam_v)
        zero = jnp.zeros((1, LANE), F32)
        zpad = jnp.zeros((PAD, LANE), F32)
        for ref in (dcw_ref, dcb_ref, dwa_ref, dwx_ref, dba_ref, dbx_ref, dlam_ref):
            ref[...] = jnp.zeros_like(ref)

        def params(dr):
            return (wa_ref[dr], wx_ref[dr], ba_ref[dr:dr + 1, :], bx_ref[dr:dr + 1, :], lsig[dr:dr + 1, :])

        def direction_bwd(dr, u_ref, a_ref, h_ref, l_ref, gates, dub, length, first):
            wa_d, wx_d, ba_d, bx_d, ls_d = params(dr)
            rt = _row_tile(length, GATE_TILE)
            prev = 1 if dr == 1 else -1

            def tile(i, c):
                t0 = pl.multiple_of(i * rt, rt)
                ut = u_ref[pl.ds(t0, rt), :]
                ub = ut.astype(BF16)
                r = gates[0][pl.ds(t0, rt), :]
                ig = gates[1][pl.ds(t0, rt), :]
                la = (LRU_C * r) * ls_d
                a = a_ref[pl.ds(PAD + t0, rt), :]
                q = _one_minus_sq(la, a)
                rs = lax.rsqrt(q)
                sq = q * rs
                lm = l_ref[pl.ds(PAD + t0, rt), :]
                da = lm * h_ref[pl.ds(PAD + t0 + prev, rt), :]
                dsq = lm * ig * ut
                dig = lm * sq * ut
                dla = da * a - dsq * (a * a) * rs
                dr_ = dla * (LRU_C * ls_d)
                dlam_ref[dr:dr + 1, :] += jnp.sum(dla * (LRU_C * r), axis=0, keepdims=True)
                dpr = dr_ * r * (1.0 - r)
                dpi = dig * ig * (1.0 - ig)
                dba_ref[dr:dr + 1, :] += jnp.sum(dpr, axis=0, keepdims=True)
                dbx_ref[dr:dr + 1, :] += jnp.sum(dpi, axis=0, keepdims=True)
                dprb, dpib = dpr.astype(BF16), dpi.astype(BF16)
                dwa_ref[dr] += _dot_tn(ub, dprb)
                dwx_ref[dr] += _dot_tn(ub, dpib)
                dut = lm * sq * ig + _dot_nt(dprb, wa_d) + _dot_nt(dpib, wx_d)
                if first:
                    dub[pl.ds(PAD + t0, rt), :] = dut
                else:
                    dub[pl.ds(PAD + t0, rt), :] += dut
                return c

            lax.fori_loop(0, length // rt, tile, 0, unroll=min(GATE_UNROLL, length // rt))

        def conv_bwd(dub, src_pad, out_ref, length):
            rt = _row_tile(length, ROW_TILE)

            def tile(i, c):
                t0 = pl.multiple_of(i * rt, rt)
                dut = dub[pl.ds(PAD + t0, rt), :]
                dcb_ref[...] += jnp.sum(dut, axis=0, keepdims=True)
                acc = jnp.zeros((rt, LANE), F32)
                for k in range(CONV_TAPS):
                    sh = CONV_LEFT - k
                    acc = acc + dub[pl.ds(PAD + t0 + sh, rt), :] * cw[k:k + 1, :]
                    dcw_ref[k:k + 1, :] += jnp.sum(dut * src_pad[pl.ds(PAD + t0 - sh, rt), :], axis=0, keepdims=True)
                out_ref[pl.ds(t0, rt), :] = acc.astype(out_ref.dtype)
                return c

            lax.fori_loop(0, length // rt, tile, 0)

        _conv_fwd(uc0_ref, cpad, cu, cw, cb, t_len)
        cbufs = ((ca0, ch0), (ca1, ch1))
        cgates = ((cr0, ci0), (cr1, ci1))
        h0 = []
        for dr in range(2):
            ca, chh = cbufs[dr]
            _gates_fwd(cu, ca, chh, *params(dr), t_len, keep=cgates[dr])
            h0.append(_scan(ca, chh, chh, length=t_len, init=zero, reverse=(dr == 1), a_shift=0, store=True))
        _conv_fwd(u0_ref, upad, ubuf, cw, cb, s_len)
        rt = _row_tile(s_len, ROW_TILE)
        dh0 = []
        for dr in range(2):
            rev = dr == 1
            _gates_fwd(ubuf, abuf, hbuf, *params(dr), s_len, keep=(rbuf, ibuf))
            _scan(abuf, hbuf, hbuf, length=s_len, init=h0[dr], reverse=rev, a_shift=0, store=True)
            first_row = PAD + s_len if rev else PAD - 1
            hbuf[pl.ds(first_row, 1), :] = h0[dr]
            end_row = PAD - 1 if rev else PAD + s_len
            abuf[pl.ds(end_row, 1), :] = zero

            def copy(i, c):
                t0 = pl.multiple_of(i * rt, rt)
                lbuf[pl.ds(PAD + t0, rt), :] = dy_ref[pl.ds(t0, rt), :]
                return c

            lax.fori_loop(0, s_len // rt, copy, 0)
            _scan(abuf, lbuf, lbuf, length=s_len, init=zero, reverse=not rev, a_shift=(-1 if rev else 1), store=True)
            start = PAD + s_len - 1 if rev else PAD
            dh0.append(abuf[pl.ds(start, 1), :] * lbuf[pl.ds(start, 1), :])
            direction_bwd(dr, ubuf, abuf, hbuf, lbuf, (rbuf, ibuf), dubuf, s_len, first=(dr == 0))
        dubuf[pl.ds(0, PAD), :] = zpad
        dubuf[pl.ds(PAD + s_len, PAD), :] = zpad
        conv_bwd(dubuf, upad, du_ref, s_len)
        lc = lbuf
        duc_buf = dubuf
        for dr in range(2):
            rev = dr == 1
            ca, chh = cbufs[dr]
            first_row = PAD + t_len if rev else PAD - 1
            chh[pl.ds(first_row, 1), :] = zero
            end_row = PAD - 1 if rev else PAD + t_len
            ca[pl.ds(end_row, 1), :] = zero + 1.0
            rtc = _row_tile(t_len, ROW_TILE)

            def clear(i, c):
                t0 = pl.multiple_of(i * rtc, rtc)
                lc[pl.ds(PAD + t0, rtc), :] = jnp.zeros((rtc, LANE), F32)
                return c

            lax.fori_loop(0, t_len // rtc, clear, 0)
            _scan(ca, lc, lc, length=t_len, init=dh0[dr], reverse=not rev, a_shift=(-1 if rev else 1), store=True)
            direction_bwd(dr, cu, ca, chh, lc, cgates[dr], duc_buf, t_len, first=(dr == 0))
        duc_buf[pl.ds(0, PAD), :] = zpad
        duc_buf[pl.ds(PAD + t_len, PAD), :] = zpad
        conv_bwd(duc_buf, cpad, duc_ref, t_len)
        dlam_ref[...] = dlam_ref[...] * (1.0 - _sigmoid(lam_v))

    seq = pltpu.VMEM((s_len + 2 * PAD, LANE), F32)
    cseq = pltpu.VMEM((t_len + 2 * PAD, LANE), F32)
    flat = pltpu.VMEM((s_len, LANE), F32)
    cflat = pltpu.VMEM((t_len, LANE), F32)
    vec2 = pl.BlockSpec((2, LANE), lambda n: (0, n))
    wspec = pl.BlockSpec((2, None, LANE, LANE), lambda n: (0, n, 0, 0))
    return _call(
        body, name="rglru_bwd", grid=(nb,),
        in_specs=[pl.BlockSpec((None, s_len, LANE), lambda n: (n, 0, 0)),
                  pl.BlockSpec((None, t_len, LANE), lambda n: (n, 0, 0)),
                  pl.BlockSpec((None, s_len, LANE), lambda n: (n, 0, 0))] + _lru_specs(),
        out_specs=[pl.BlockSpec((s_len, LANE), lambda n: (0, n)),
                   pl.BlockSpec((t_len, LANE), lambda n: (0, n)),
                   pl.BlockSpec((CONV_TAPS, LANE), lambda n: (0, n)),
                   pl.BlockSpec((1, LANE), lambda n: (0, n)),
                   wspec, wspec, vec2, vec2, vec2],
        out_shape=[jax.ShapeDtypeStruct((s_len, e), BF16), jax.ShapeDtypeStruct((t_len, e), BF16),
                   jax.ShapeDtypeStruct((CONV_TAPS, e), F32), jax.ShapeDtypeStruct((1, e), F32),
                   jax.ShapeDtypeStruct((2, nb, LANE, LANE), F32), jax.ShapeDtypeStruct((2, nb, LANE, LANE), F32),
                   jax.ShapeDtypeStruct((2, e), F32), jax.ShapeDtypeStruct((2, e), F32), jax.ShapeDtypeStruct((2, e), F32)],
        scratch_shapes=[seq, flat, seq, seq, seq, seq, flat, flat,
                        cseq, cflat, cseq, cseq, cseq, cseq, cflat, cflat, cflat, cflat],
    )(ug, uc, dy, conv_w, conv_b, wa, wx, ba, bx, lam)


def _pool_windows(src_ref, out_ref, colbuf, rowbuf, half, transpose, s_len):
    gw = GRID_W
    lg = gw.bit_length() - 1
    n_rows = s_len // gw
    cp, rm = POOL_CPAD, 8 * gw
    stride = gw + 2 * cp
    rt = _row_tile(s_len, ROW_TILE)
    assert rt % gw == 0 and half <= cp
    gpt = rt // gw
    offs = range(-half, half)
    zmargin = jnp.zeros((cp, LANE), F32)

    def zcol(r, c):
        base = pl.multiple_of(r * stride, SUB)
        colbuf[pl.ds(base, cp), :] = zmargin
        colbuf[pl.ds(base + cp + gw, cp), :] = zmargin
        return c

    lax.fori_loop(0, n_rows, zcol, 0)

    def zrow(i, c):
        t0 = pl.multiple_of(i * gw, gw)
        rowbuf[pl.ds(t0, gw), :] = jnp.zeros((gw, LANE), F32)
        rowbuf[pl.ds(rm + s_len + t0, gw), :] = jnp.zeros((gw, LANE), F32)
        return c

    lax.fori_loop(0, rm // gw, zrow, 0)

    col = lax.broadcasted_iota(jnp.int32, (gw, LANE), 0)
    ccnt = (jnp.minimum(col + half, gw) - jnp.maximum(col - half, 0)).astype(F32)

    def row_counts(t0):
        row = (t0 + lax.broadcasted_iota(jnp.int32, (rt, LANE), 0)) >> lg
        return (jnp.minimum(row + half, n_rows) - jnp.maximum(row - half, 0)).astype(F32)

    def col_base(t0, g):
        return pl.multiple_of((t0 // gw) * stride, SUB) + g * stride + cp

    def col_sum(t0, g, sign):
        acc = jnp.zeros((gw, LANE), F32)
        for o in offs:
            acc = acc + colbuf[pl.ds(col_base(t0, g) + sign * o, gw), :]
        return acc

    def row_sum(t0, sign):
        acc = jnp.zeros((rt, LANE), F32)
        for o in offs:
            acc = acc + rowbuf[pl.ds(rm + t0 + sign * o * gw, rt), :]
        return acc

    n_tiles = s_len // rt
    assert rt >= half * gw

    def loop(fn, edges=False):
        def step(i, c):
            t0 = pl.multiple_of(i * rt, rt)
            fn(t0, False) if edges else fn(t0)
            return c
        if edges:
            fn(0, True)
            if n_tiles > 1:
                fn(s_len - rt, True)
            lax.fori_loop(1, n_tiles - 1, step, 0)
        else:
            lax.fori_loop(0, n_tiles, step, 0)

    inv_ccnt = 1.0 / ccnt

    def by_row_count(v, t0, edge):
        return v / row_counts(t0) if edge else v * (1.0 / (2 * half))

    if not transpose:
        def fill(t0):
            for g in range(gpt):
                colbuf[pl.ds(col_base(t0, g), gw), :] = src_ref[pl.ds(t0 + g * gw, gw), :]

        def cols(t0):
            for g in range(gpt):
                rowbuf[pl.ds(rm + t0 + g * gw, gw), :] = col_sum(t0, g, 1) * inv_ccnt

        def rows(t0, edge):
            mean = by_row_count(row_sum(t0, 1), t0, edge)
            out_ref[pl.ds(t0, rt), :] = (mean - src_ref[pl.ds(t0, rt), :]).astype(out_ref.dtype)

        loop(fill)
        loop(cols)
        loop(rows, edges=True)
    else:
        def fill(t0, edge):
            rowbuf[pl.ds(rm + t0, rt), :] = by_row_count(src_ref[pl.ds(t0, rt), :], t0, edge)

        def rows(t0):
            acc = row_sum(t0, -1)
            for g in range(gpt):
                colbuf[pl.ds(col_base(t0, g), gw), :] = acc[g * gw:(g + 1) * gw, :] * inv_ccnt

        def cols(t0):
            for g in range(gpt):
                rows_g = pl.ds(t0 + g * gw, gw)
                out_ref[rows_g, :] = (col_sum(t0, g, -1) - src_ref[rows_g, :]).astype(out_ref.dtype)

        loop(fill, edges=True)
        loop(rows)
        loop(cols)


def _pool_map(src, nb, transpose, out_chunk_major, name):
    s_len = src.shape[1]
    cpg = nb // len(POOL_WINDOWS)

    def body(src_ref, out_ref, colbuf, rowbuf):
        n = pl.program_id(0)
        for gi, w in enumerate(POOL_WINDOWS):
            @pl.when(n // cpg == gi)
            def _(w=w):
                _pool_windows(src_ref, out_ref, colbuf, rowbuf, w // 2, transpose, s_len)

    if out_chunk_major:
        out_spec = pl.BlockSpec((None, s_len, LANE), lambda n: (n, 0, 0))
        out_shape = jax.ShapeDtypeStruct((nb, s_len, LANE), BF16)
    else:
        out_spec = pl.BlockSpec((s_len, LANE), lambda n: (0, n))
        out_shape = jax.ShapeDtypeStruct((s_len, nb * LANE), BF16)
    return _call(
        body, name=name, grid=(nb,),
        in_specs=[pl.BlockSpec((None, s_len, LANE), lambda n: (n, 0, 0))],
        out_specs=out_spec, out_shape=out_shape,
        scratch_shapes=[pltpu.VMEM((s_len // GRID_W * (GRID_W + 2 * POOL_CPAD), LANE), F32),
                        pltpu.VMEM((s_len + 16 * GRID_W, LANE), F32)],
    )(src)


def _group_weight(w_ref):
    return jnp.concatenate([w_ref[k] for k in range(N_CHIPS)], axis=0)


def _pool_mm_fwd(dm, wp, scale):
    nb, rows, _ = dm.shape
    _, ng, pq, pg = wp.shape
    cpg = pg // LANE
    tm = _row_tile(rows, 2048)

    def body(d_ref, w_ref, s_ref, y_ref):
        _put_chunks(y_ref, _dot(_cat(d_ref, cpg), _group_weight(w_ref)) * s_ref[...], cpg)

    cspec = pl.BlockSpec((cpg, tm, LANE), lambda i, g: (g, i, 0))
    return _call(
        body, name="pool_mm_fwd", grid=(rows // tm, ng),
        in_specs=[cspec, pl.BlockSpec((N_CHIPS, None, pq, pg), lambda i, g: (0, g, 0, 0)),
                  pl.BlockSpec((1, pg), lambda i, g: (0, g))],
        out_specs=cspec, out_shape=jax.ShapeDtypeStruct((nb, rows, LANE), F32),
    )(dm, wp, scale)


def _pool_mm_bwd(dy, dm, wp, scale, gbuf, row0):
    nb, rows, _ = dm.shape
    _, ng, pq, pg = wp.shape
    cpg = pg // LANE
    tm = _row_tile(rows, 1024)
    nt = rows // tm
    assert gbuf.shape[2] == 2 * pg and row0 % pq == 0

    def body(dy_ref, d_ref, w_ref, s_ref, buf_ref, dd_ref, dwp_ref, dsc_ref, acc):
        i = pl.program_id(1)

        @pl.when(i == 0)
        def _():
            acc[...] = jnp.zeros_like(acc)
            dsc_ref[...] = jnp.zeros_like(dsc_ref)

        dyv = _cat(dy_ref, cpg)
        dc = _cat(d_ref, cpg)
        w = _group_weight(w_ref)
        dsc_ref[...] += jnp.sum(dyv * _dot(dc, w), axis=0, keepdims=True)
        dyp = (dyv * s_ref[...]).astype(BF16)
        _put_chunks(dd_ref, _dot_nt(dyp, w), cpg)
        acc[...] += _dot_tn(dc, dyp)

        @pl.when(i == nt - 1)
        def _():
            dwp_ref[...] = acc[...].reshape(N_CHIPS, pq, pg)

    cspec = pl.BlockSpec((cpg, tm, LANE), lambda g, i: (g, i, 0))
    sspec = pl.BlockSpec((1, pg), lambda g, i: (0, g))
    return _call(
        body, name="pool_mm_bwd", grid=(ng, nt),
        in_specs=[cspec, cspec, pl.BlockSpec((N_CHIPS, None, pq, pg), lambda g, i: (0, g, 0, 0)), sspec, ANY_SPEC],
        out_specs=[cspec, pl.BlockSpec((N_CHIPS, pq, pg), lambda g, i: (0, row0 // pq + g // 2, g % 2)), sspec],
        out_shape=[jax.ShapeDtypeStruct((nb, rows, LANE), F32), jax.ShapeDtypeStruct(gbuf.shape, F32),
                   jax.ShapeDtypeStruct((1, ng * pg), F32)],
        scratch_shapes=[pltpu.VMEM((pg, pg), F32)],
        input_output_aliases={4: 1},
    )(dy, dm, wp, scale, gbuf)


def _adamw_math(w, g, m, v):
    nm = ADAM_B1 * m + (1.0 - ADAM_B1) * g
    nv = ADAM_B2 * v + (1.0 - ADAM_B2) * jnp.square(g)
    m_hat = nm / (1.0 - ADAM_B1 ** ADAM_STEP)
    v_hat = nv / (1.0 - ADAM_B2 ** ADAM_STEP)
    return -ADAM_LR * (m_hat / (jnp.sqrt(v_hat) + ADAM_EPS) + ADAM_WD * w), nm, nv


def _adamw_param(w3, m3, v3, gsrcs, pick, tm, name):
    n_blk, rows, cols = w3.shape
    ng = len(gsrcs)

    def body(*refs):
        w_ref, m_ref, v_ref = refs[:3]
        g_refs = refs[3:3 + ng]
        go_ref, d_ref, nm_ref, nv_ref = refs[3 + ng:]
        g = pick(pl.program_id(0), [r[...] for r in g_refs])
        go_ref[...] = g
        d_ref[...], nm_ref[...], nv_ref[...] = _adamw_math(w_ref[...], g, m_ref[...], v_ref[...])

    spec = pl.BlockSpec((None, tm, cols), lambda n, i: (n, i, 0))
    return _call(
        body, name=name, grid=(n_blk, rows // tm),
        in_specs=[spec] * 3 + [pl.BlockSpec(shape, imap) for _, shape, imap in gsrcs],
        out_specs=[spec] * 4, out_shape=[jax.ShapeDtypeStruct(w3.shape, F32)] * 4,
    )(w3, m3, v3, *[a for a, _, _ in gsrcs])


def _adamw_small(quads):
    n = len(quads)

    def body(*refs):
        ins, outs = refs[:4 * n], refs[4 * n:]
        for k in range(n):
            w, g, m, v = (r[...] for r in ins[4 * k:4 * k + 4])
            outs[3 * k][...], outs[3 * k + 1][...], outs[3 * k + 2][...] = _adamw_math(w, g, m, v)

    flat = [a for q in quads for a in q]
    res = _call(body, name="adamw_small", grid=(1,),
                in_specs=[pl.BlockSpec(a.shape, lambda i: (0, 0)) for a in flat],
                out_specs=[pl.BlockSpec(q[0].shape, lambda i: (0, 0)) for q in quads for _ in range(3)],
                out_shape=[jax.ShapeDtypeStruct(q[0].shape, F32) for q in quads for _ in range(3)])(*flat)
    return [tuple(res[3 * k:3 * k + 3]) for k in range(n)]


def _place():
    return lax.axis_index("x"), lax.axis_index("y"), lax.axis_index("c")


def _other_chips(x, y):
    return [(1 - x, y), (x, 1 - y), (1 - x, 1 - y)]


def _gather_chips(arrays, name):
    n = len(arrays)
    halves = [a.shape[0] // 2 for a in arrays]
    for a, h in zip(arrays, halves):
        assert 2 * h == a.shape[0] and h % (32 // a.dtype.itemsize) == 0
    lands = [jnp.broadcast_to(a[None], (N_CHIPS,) + a.shape) for a in arrays]

    def body(*refs):
        outs = refs[n:2 * n]
        send_sems, recv_sems = refs[2 * n:]
        x, y, c = _place()
        me = 2 * x + y
        chips = _other_chips(x, y)

        def mine(k):
            return pl.ds(c * halves[k], halves[k])

        def theirs(k):
            return pl.ds((1 - c) * halves[k], halves[k])

        def push(k, j, src, dst, to):
            return pltpu.make_async_remote_copy(src_ref=src, dst_ref=dst, send_sem=send_sems.at[6 * k + j],
                                                recv_sem=recv_sems.at[6 * k + j], device_id=to, device_id_type=MESH)

        started = []
        for j, (cx, cy) in enumerate(chips):
            for k in range(n):
                own = outs[k].at[me, mine(k)]
                cp = push(k, j, own, own, (cx, cy, c))
                cp.start()
                started.append(cp)
        for j, (cx, cy) in enumerate(chips):
            for k in range(n):
                slab = outs[k].at[2 * cx + cy, mine(k)]
                push(k, j, slab, slab, (x, y, c)).wait_recv()
                fwd = push(k, 3 + j, slab, slab, (x, y, 1 - c))
                fwd.start()
                started.append(fwd)
        for j, (cx, cy) in enumerate(chips):
            for k in range(n):
                slab = outs[k].at[2 * cx + cy, theirs(k)]
                push(k, 3 + j, slab, slab, (x, y, c)).wait_recv()
        for cp in started:
            cp.wait_send()

    return _call(
        body, name=name, in_specs=[ANY_SPEC] * n, out_specs=[ANY_SPEC] * n,
        out_shape=[jax.ShapeDtypeStruct(a.shape, a.dtype) for a in lands],
        input_output_aliases={k: k for k in range(n)},
        scratch_shapes=[pltpu.SemaphoreType.DMA((6 * n,)), pltpu.SemaphoreType.DMA((6 * n,))],
    )(*lands)


def _gather_devices(v):
    shape = v.shape

    def body(v_ref, out_ref, send_sems, recv_sems):
        x, y, c = _place()
        me = 4 * x + 2 * y + c
        out_ref[me] = v_ref[...]
        sends = []
        for k in range(1, N_DEV):
            to = (me + k) % N_DEV
            cp = pltpu.make_async_remote_copy(src_ref=v_ref, dst_ref=out_ref.at[me], send_sem=send_sems.at[k],
                                              recv_sem=recv_sems.at[k], device_id=(to // 4, (to // 2) % 2, to % 2),
                                              device_id_type=MESH)
            cp.start()
            sends.append(cp)
        for k in range(1, N_DEV):
            frm = (me + N_DEV - k) % N_DEV
            pltpu.make_async_remote_copy(src_ref=v_ref, dst_ref=out_ref.at[frm], send_sem=send_sems.at[k],
                                         recv_sem=recv_sems.at[k], device_id=(x, y, c), device_id_type=MESH).wait_recv()
        for cp in sends:
            cp.wait_send()

    vspec = pl.BlockSpec(memory_space=pltpu.VMEM)
    return _call(body, name="gather_devices", in_specs=[vspec], out_specs=vspec,
                 out_shape=jax.ShapeDtypeStruct((N_DEV,) + shape, F32),
                 scratch_shapes=[pltpu.SemaphoreType.DMA((N_DEV,)), pltpu.SemaphoreType.DMA((N_DEV,))])(v)


HBM_SPEC = pl.BlockSpec(memory_space=pltpu.HBM)
SEM_SPEC = pl.BlockSpec(memory_space=pltpu.SEMAPHORE)
SIDE_EFFECT = pltpu.SideEffectType.DATAFLOW_SIDE_EFFECTING


def _push_copies(src_refs, land_refs, send_sems, recv_sems, per_peer):
    x, y, c = _place()
    me = 2 * x + y
    copies = []
    for j, (cx, cy) in enumerate(_other_chips(x, y)):
        for k, (src, land) in enumerate(zip(src_refs, land_refs)):
            copies.append(pltpu.make_async_remote_copy(
                src_ref=src.at[2 * cx + cy] if per_peer else src, dst_ref=land.at[me], send_sem=send_sems.at[3 * k + j],
                recv_sem=recv_sems.at[3 * k + j], device_id=(cx, cy, c), device_id_type=MESH))
    return copies


def _push_start(srcs, lands, per_peer, after, name):
    n = len(srcs)

    def body(*refs):
        src_refs, land_refs = refs[:n], refs[n:2 * n]
        send_sems, recv_sems = refs[2 * n + 1], refs[2 * n + 2]
        token = refs[-1]
        for cp in _push_copies(src_refs, land_refs, send_sems, recv_sems, per_peer):
            cp.start()
        token[...] = jnp.zeros_like(token)

    bufs = [pltpu.with_memory_space_constraint(a, pltpu.HBM) for a in list(srcs) + list(lands)]
    res = _call(
        body, name=name,
        out_shape=[pltpu.SemaphoreType.DMA((3 * n,)), pltpu.SemaphoreType.DMA((3 * n,))]
        + [pltpu.HBM(a.shape, a.dtype) for a in bufs] + [jax.ShapeDtypeStruct((SUB, LANE), F32)],
        in_specs=[HBM_SPEC] * (2 * n) + [ANY_SPEC],
        out_specs=[SEM_SPEC, SEM_SPEC] + [HBM_SPEC] * (2 * n) + [pl.BlockSpec(memory_space=pltpu.VMEM)],
        input_output_aliases={i: 2 + i for i in range(2 * n)},
        compiler_params=pltpu.CompilerParams(has_side_effects=SIDE_EFFECT),
    )(*bufs, after)
    return res[0], res[1], list(res[2:2 + n]), list(res[2 + n:2 + 2 * n]), res[-1]


def _push_wait(send_sems, recv_sems, srcs, lands, per_peer, after, name):
    n = len(srcs)

    def body(*refs):
        src_refs, land_refs = refs[:n], refs[n:2 * n]
        send_sems, recv_sems = refs[2 * n], refs[2 * n + 1]
        for cp in _push_copies(src_refs, land_refs, send_sems, recv_sems, per_peer):
            cp.wait_send()
            cp.wait_recv()

    res = _call(
        body, name=name,
        out_shape=[pltpu.HBM(a.shape, a.dtype) for a in list(srcs) + list(lands)],
        in_specs=[HBM_SPEC] * (2 * n) + [SEM_SPEC, SEM_SPEC, ANY_SPEC],
        out_specs=[HBM_SPEC] * (2 * n),
        input_output_aliases={i: i for i in range(2 * n)},
        compiler_params=pltpu.CompilerParams(has_side_effects=SIDE_EFFECT),
    )(*srcs, *lands, send_sems, recv_sems, after)
    return list(res[n:])


def _sibling_swap(g):
    _, rows, w = g.shape
    half = rows // 2

    def body(g_ref, out_ref, send_sem, recv_sem):
        x, y, c = _place()
        cp = pltpu.make_async_remote_copy(src_ref=g_ref.at[:, pl.ds((1 - c) * half, half)], dst_ref=out_ref,
                                          send_sem=send_sem, recv_sem=recv_sem, device_id=(x, y, 1 - c), device_id_type=MESH)
        cp.start()
        cp.wait()

    return _call(body, name="rs_sibling_swap", in_specs=[ANY_SPEC], out_specs=ANY_SPEC,
                 out_shape=jax.ShapeDtypeStruct((N_CHIPS, half, w), F32),
                 scratch_shapes=[pltpu.SemaphoreType.DMA, pltpu.SemaphoreType.DMA])(g)


def _pair_add(g, got, place):
    _, rows, w = g.shape
    half = rows // 2
    tm = _row_tile(half, RS_TILE)
    nt = half // tm

    def body(p_ref, a_ref, b_ref, o_ref, own_ref):
        v = a_ref[...] + b_ref[...]
        o_ref[...] = v.astype(BF16)

        @pl.when(pl.program_id(1) == p_ref[1])
        def _():
            own_ref[...] = v

    return _call(
        body, name="rs_pair_add",
        grid_spec=pltpu.PrefetchScalarGridSpec(
            num_scalar_prefetch=1, grid=(nt, N_CHIPS),
            in_specs=[pl.BlockSpec((None, tm, w), lambda i, s, p: (s, p[0] * nt + i, 0)),
                      pl.BlockSpec((None, tm, w), lambda i, s, p: (s, i, 0))],
            out_specs=[pl.BlockSpec((None, tm, w), lambda i, s, p: (s, i, 0)),
                       pl.BlockSpec((tm, w), lambda i, s, p: (i, 0))]),
        out_shape=[jax.ShapeDtypeStruct((N_CHIPS, half, w), BF16), jax.ShapeDtypeStruct((half, w), F32)],
    )(place, g, got)


def _sum_chips(parts, own, place):
    _, half, w = parts.shape
    tm = _row_tile(half, RS_TILE)
    nt = half // tm

    def body(p_ref, parts_ref, own_ref, o_ref):
        me = p_ref[1]
        t = [jnp.where(me == q, own_ref[...], parts_ref[q].astype(F32)) for q in range(N_CHIPS)]
        o_ref[...] = (t[0] + t[1]) + (t[2] + t[3])

    return _call(
        body, name="rs_sum_chips",
        grid_spec=pltpu.PrefetchScalarGridSpec(
            num_scalar_prefetch=1, grid=(nt,),
            in_specs=[pl.BlockSpec((N_CHIPS, tm, w), lambda i, p: (0, i, 0)), pl.BlockSpec((tm, w), lambda i, p: (i, 0))],
            out_specs=pl.BlockSpec((tm, w), lambda i, p: (p[0] * nt + i, 0))),
        out_shape=jax.ShapeDtypeStruct((2 * half, w), F32),
    )(place, parts, own)


def _sibling_gather(red):
    rows, w = red.shape
    half = rows // 2

    def body(in_ref, out_ref, send_sem, recv_sem):
        x, y, c = _place()
        mine = out_ref.at[pl.ds(c * half, half)]
        cp = pltpu.make_async_remote_copy(src_ref=mine, dst_ref=mine, send_sem=send_sem, recv_sem=recv_sem,
                                          device_id=(x, y, 1 - c), device_id_type=MESH)
        cp.start()
        other = out_ref.at[pl.ds((1 - c) * half, half)]
        pltpu.make_async_remote_copy(src_ref=other, dst_ref=other, send_sem=send_sem, recv_sem=recv_sem,
                                     device_id=(x, y, c), device_id_type=MESH).wait_recv()
        cp.wait_send()

    return _call(body, name="rs_sibling_gather", in_specs=[ANY_SPEC], out_specs=ANY_SPEC,
                 out_shape=jax.ShapeDtypeStruct(red.shape, F32), input_output_aliases={0: 0},
                 scratch_shapes=[pltpu.SemaphoreType.DMA, pltpu.SemaphoreType.DMA])(red)


def _rs_begin(g, place, name):
    pair, own = _pair_add(g, _sibling_swap(g), place)
    send, recv, pair, parts, token = _push_start([pair], [jnp.zeros_like(pair)], True, own, name + "_start")
    return (send, recv, pair, parts, own), token


def _rs_end(state, place, after, name):
    send, recv, pair, parts, own = state
    (parts,) = _push_wait(send, recv, pair, parts, True, after, name + "_wait")
    return _sibling_gather(_sum_chips(parts, own, place))


WEIGHTS = ("c_ctx", "w_mod", "b_mod", "w_in", "w_out", "ln_g", "ln_b", "conv_w", "conv_b", "lru_wa", "lru_ba", "lru_wx",
           "lru_bx", "lru_lam", "pool_w", "pool_scale")
SMALL_GATHERED = ("conv_w", "lru_ba", "lru_bx", "lru_lam", "pool_scale")
SMALL_UPDATED = ("c_ctx", "b_mod", "ln_g", "ln_b", "conv_w", "conv_b", "lru_ba", "lru_bx", "lru_lam", "pool_scale")


def kernel(x, c, ctx, c_ctx, w_mod, b_mod, w_in, w_out, ln_g, ln_b, conv_w, conv_b, lru_wa, lru_ba, lru_wx, lru_bx, lru_lam, pool_w, pool_scale, loss_target, m_c_ctx, m_w_mod, m_b_mod, m_w_in, m_w_out, m_ln_g, m_ln_b, m_conv_w, m_conv_b, m_lru_wa, m_lru_ba, m_lru_wx, m_lru_bx, m_lru_lam, m_pool_w, m_pool_scale, v_c_ctx, v_w_mod, v_b_mod, v_w_in, v_w_out, v_ln_g, v_ln_b, v_conv_w, v_conv_b, v_lru_wa, v_lru_ba, v_lru_wx, v_lru_bx, v_lru_lam, v_pool_w, v_pool_scale):
    weights = dict(c_ctx=c_ctx, w_mod=w_mod, b_mod=b_mod, w_in=w_in, w_out=w_out, ln_g=ln_g, ln_b=ln_b, conv_w=conv_w,
                   conv_b=conv_b, lru_wa=lru_wa, lru_ba=lru_ba, lru_wx=lru_wx, lru_bx=lru_bx, lru_lam=lru_lam,
                   pool_w=pool_w, pool_scale=pool_scale)
    mom1 = dict(c_ctx=m_c_ctx, w_mod=m_w_mod, b_mod=m_b_mod, w_in=m_w_in, w_out=m_w_out, ln_g=m_ln_g, ln_b=m_ln_b,
                conv_w=m_conv_w, conv_b=m_conv_b, lru_wa=m_lru_wa, lru_ba=m_lru_ba, lru_wx=m_lru_wx, lru_bx=m_lru_bx,
                lru_lam=m_lru_lam, pool_w=m_pool_w, pool_scale=m_pool_scale)
    mom2 = dict(c_ctx=v_c_ctx, w_mod=v_w_mod, b_mod=v_b_mod, w_in=v_w_in, w_out=v_w_out, ln_g=v_ln_g, ln_b=v_ln_b,
                conv_w=v_conv_w, conv_b=v_conv_b, lru_wa=v_lru_wa, lru_ba=v_lru_ba, lru_wx=v_lru_wx, lru_bx=v_lru_bx,
                lru_lam=v_lru_lam, pool_w=v_pool_w, pool_scale=v_pool_scale)
    xs, cx, target = x[0], ctx[0], loss_target[0]
    s_len, d = xs.shape
    es = w_out.shape[1]
    e = es * N_CHIPS
    nb = e // LANE
    c3 = w_mod.shape[2]
    n4 = w_in.shape[2]
    pq, pg = pool_w.shape[2], pool_w.shape[3]
    ng = len(POOL_WINDOWS)
    width = n4
    assert width == d and 2 * pg == width and 2 * nb * LANE == N_CHIPS * width and d % (2 * N_CHIPS) == 0
    px, py, pc = _place()
    place = jnp.stack([pc, 2 * px + py]).astype(jnp.int32)
    cctx2 = c_ctx[None, :]

    eq = e // N_CHIPS
    small_rows = [(conv_w[0], 0), (lru_ba[0], CONV_TAPS), (lru_bx[0], CONV_TAPS + 2), (lru_lam[0], CONV_TAPS + 4),
                  (pool_scale, CONV_TAPS + 6)]
    small = _rows_kernel([(a, r, 0) for a, r in small_rows], 2 * SUB, eq, "pack_small_weights")
    wm_g, win0, sg = _gather_chips([w_mod.astype(BF16).reshape(DEPTH * d, c3), w_in[0].astype(BF16), small], "gather_weights0")
    later = [w_out[0].astype(BF16), w_in[1].astype(BF16), w_out[1].astype(BF16), pool_w.astype(BF16).reshape(ng * pq, pg)]
    w_send, w_recv, later, later_lands, w_token = _push_start(
        later, [jnp.broadcast_to(a[None], (N_CHIPS,) + a.shape) for a in later], False, sg, "gather_weights1_start")
    wm_all = wm_g.reshape(N_CHIPS, DEPTH, d, c3)
    full = {n: jnp.swapaxes(sg[:, r:r + a.shape[0]], 0, 1).reshape(a.shape[0], e)
            for n, (a, r) in zip(SMALL_GATHERED, small_rows)}
    wa_b, wx_b = lru_wa[0].astype(BF16), lru_wx[0].astype(BF16)
    lru_args = (full["conv_w"], conv_b, wa_b, wx_b, full["lru_ba"], full["lru_bx"], full["lru_lam"])
    scale_f = full["pool_scale"]

    mod = _mod_fwd(c + w_token[0:1, 0:1], cctx2, wm_all, b_mod[:, None, :])

    def mod_parts(l, row):
        v = mod[l, row]
        return v[None, :d], 1.0 + v[None, d:2 * d], v[None, 2 * d:]

    sh0, sc0, gt0 = mod_parts(0, 0)
    shc, scc, _ = mod_parts(0, 1)
    sh1, sc1, gt1 = mod_parts(1, 0)
    lg = [ln_g[l][None, :] for l in range(DEPTH)]
    lb = [ln_b[l][None, :] for l in range(DEPTH)]

    ug0 = _inproj_fwd(xs, sc0, sh0, win0, "inproj_fwd0")
    uc0 = _inproj_fwd(cx, scc, shc, win0[:2], "inproj_fwd_ctx")
    y0 = _rglru_fwd(ug0, uc0, *lru_args)
    wout0_g, win1, wout1_g, wp_g = _push_wait(w_send, w_recv, later, later_lands, False, y0, "gather_weights1_wait")
    win = [win0, win1]
    wout = [wout0_g.reshape(e, d), wout1_g.reshape(e, d)]
    wp = wp_g.reshape(N_CHIPS, ng, pq, pg)
    br0, x1 = _outproj_fwd(y0, ug0, xs, gt0, wout[0], lg[0], lb[0], None, "outproj_fwd0")
    ug1 = _inproj_fwd(x1, sc1, sh1, win[1], "inproj_fwd1")
    d1 = _pool_map(ug1, nb, False, True, "pool_fwd")
    y1 = _pool_mm_fwd(d1, wp, scale_f)
    br1, dxo, loss_part = _outproj_fwd(y1, ug1, x1, gt1, wout[1], lg[1], lb[1], target, "outproj_fwd1")
    loss = lax.psum(jnp.sum(loss_part) * (0.5 / d), ("x", "y", "c"))

    row_wout = d
    row_tail = d + es
    wq = 2 * (nb // N_CHIPS) * LANE * LANE // width
    whole = lambda r: (r + 2 * RS_TILE - 1) // (2 * RS_TILE) * (2 * RS_TILE)
    rows1 = whole(row_tail + pg // 2)
    rows0 = whole(row_tail + 2 * wq)
    gbuf1 = jnp.zeros((N_CHIPS, rows1, width), F32)
    gbuf0 = jnp.zeros((N_CHIPS, rows0, width), F32)

    dy1, dg1, dxres1, dbr1, dlg1, dlb1, dgt1 = _outproj_bwd(dxo, x1, br1, y1, ug1, gt1, lg[1], wout[1], "outproj_bwd1")
    gbuf1 = _outproj_bwd_w(y1, ug1, dbr1, gbuf1, row_wout, "outproj_bwd_w1")
    dd1, gbuf1, dscale = _pool_mm_bwd(dy1, d1, wp, scale_f, gbuf1, row_tail)
    du1 = _pool_map(dd1, nb, True, False, "pool_bwd")
    dx1, dsc1, dsh1 = _inproj_bwd_x([du1, dg1], x1, dxres1, sc1, win[1], "inproj_bwd_x1")
    gbuf1 = _inproj_bwd_w(x1, sc1, sh1, [du1, dg1], None, gbuf1, "inproj_bwd_w1")
    rs1, token1 = _rs_begin(gbuf1, place, "rs_exchange1")

    dy0, dg0, dxres0, dbr0, dlg0, dlb0, dgt0 = _outproj_bwd(dx1, xs, br0, y0, ug0, gt0 + token1[0:1, 0:1], lg[0], wout[0],
                                                            "outproj_bwd0")
    gbuf0 = _outproj_bwd_w(y0, ug0, dbr0, gbuf0, row_wout, "outproj_bwd_w0")
    du0, duc, dconv_w, dconv_b, dwa, dwx, dba, dbx, dlam = _rglru_bwd(ug0, uc0, dy0, *lru_args)
    dwin0c = _inproj_bwd_w(cx, scc, shc, [duc, jnp.zeros_like(duc)], None, None, "inproj_bwd_w_ctx")
    gbuf0 = _inproj_bwd_w(xs, sc0, sh0, [du0, dg0], dwin0c, gbuf0, "inproj_bwd_w0")

    def quarter(dw):
        t = dw.reshape(2, N_CHIPS, nb // N_CHIPS, LANE, LANE)
        return jnp.transpose(t, (1, 3, 0, 2, 4)).reshape(N_CHIPS, LANE, 2 * (nb // N_CHIPS) * LANE).reshape(N_CHIPS, wq, width)

    tail0 = jnp.concatenate([quarter(dwa), quarter(dwx)], axis=1)
    gbuf0 = lax.dynamic_update_slice(gbuf0, tail0, (0, row_tail, 0))
    red1 = _rs_end(rs1, place, gbuf0, "rs_exchange1")
    rs0, token0 = _rs_begin(gbuf0, place, "rs_exchange0")
    grad_x, dsc0, dsh0 = _inproj_bwd_x([du0, dg0], xs, dxres0, sc0 + token0[0:1, 0:1], win[0], "inproj_bwd_x0")
    dscc, dshc = _inproj_bwd_x([duc], cx, None, scc, win[0][:2], "inproj_bwd_x_ctx")

    k0 = VEC_KINDS
    vec = _rows_kernel(
        [(c, 0, 0), (dsh0, 1, 0), (dsc0, 1, d), (dgt0, 1, 2 * d), (dshc, 2, 0), (dscc, 2, d),
         (dsh1, 3, 0), (dsc1, 3, d), (dgt1, 3, 2 * d),
         (dconv_b, k0, 0), (dlg0, k0, e), (dscale, k0 + 1, 0), (dlg1, k0 + 1, e), (dlb0, k0 + 2, 0), (dlb1, k0 + 2, d),
         (dconv_w, k0 + 3, 0), (dba, k0 + 7, 0), (dbx, k0 + 9, 0), (dlam, k0 + 11, 0)], VEC_ROWS, 3 * d, "pack_vec")
    gt_all = jnp.swapaxes(_gather_devices(vec), 0, 1)
    g_wmod = _mod_bwd_shard(gt_all, cctx2, place, c3)
    g_bmod, g_cctx, g_small = _mod_bwd_rep(gt_all, cctx2, wm_all)
    red0 = _rs_end(rs0, place, g_bmod, "rs_exchange0")
    (rep,) = _gather_chips([red0[row_tail:row_tail + 2 * wq]], "gather_replicated")

    tmw = _row_tile(d, 256)
    red_src = lambda red, r0, tm: (red, (tm, width), lambda n, i: (r0 // tm + i, 0))
    by_layer = lambda n, gs: jnp.where(n == 0, gs[0], gs[1])
    outs = {}
    outs["w_in"] = _adamw_param(w_in, m_w_in, v_w_in, [red_src(red0, 0, tmw), red_src(red1, 0, tmw)], by_layer, tmw, "adamw_w_in")
    outs["w_out"] = _adamw_param(w_out, m_w_out, v_w_out, [red_src(red0, row_wout, tmw), red_src(red1, row_wout, tmw)],
                                 by_layer, tmw, "adamw_w_out")
    outs["w_mod"] = _adamw_param(w_mod, m_w_mod, v_w_mod, [(g_wmod, (None, tmw, c3), lambda n, i: (n, i, 0))],
                                 lambda n, gs: gs[0], tmw, "adamw_w_mod")
    pw = [a.reshape(ng, pq, pg) for a in (pool_w, m_pool_w, v_pool_w)]
    outs["pool_w"] = [o.reshape(pool_w.shape) for o in _adamw_param(
        *pw, [(red1, (pq, pg), lambda n, i: (row_tail // pq + n // 2, n % 2))], lambda n, gs: gs[0], pq, "adamw_pool_w")]
    bq = nb // N_CHIPS
    rep_src = lambda r0: (rep, (None, LANE, bq * LANE), lambda n, i: (n % N_CHIPS, r0 // LANE, n // N_CHIPS))
    stack = lambda n, gs: jnp.concatenate([gs[0][:, k * LANE:(k + 1) * LANE] for k in range(bq)], axis=0)
    for name, r0, trio in (("lru_wa", 0, (lru_wa, m_lru_wa, v_lru_wa)), ("lru_wx", wq, (lru_wx, m_lru_wx, v_lru_wx))):
        blocks = [a.reshape(2 * N_CHIPS, bq * LANE, LANE) for a in trio]
        outs[name] = [o.reshape(lru_wa.shape) for o in _adamw_param(*blocks, [rep_src(r0)], stack, bq * LANE, "adamw_" + name)]

    g_small = dict(g_small, c_ctx=g_cctx, b_mod=g_bmod)
    for n in SMALL_GATHERED:
        g_small[n] = lax.dynamic_slice_in_dim(g_small[n], place[1] * eq, eq, axis=1)
    as2d = lambda a: a.reshape(-1, a.shape[-1])
    quads = [(as2d(weights[n]), g_small[n], as2d(mom1[n]), as2d(mom2[n])) for n in SMALL_UPDATED]
    for n, (q, res) in zip(SMALL_UPDATED, zip(quads, _adamw_small(quads))):
        outs[n] = [a.reshape(weights[n].shape) for a in (q[1],) + res]

    result = [loss, grad_x[None]]
    for j in range(4):
        result += [outs[n][j] for n in WEIGHTS]
    return tuple(result)
```

```python
import jax
import jax.numpy as jnp
from jax import lax
from jax.experimental import pallas as pl
from jax.experimental.pallas import tpu as pltpu

F32 = jnp.float32
BF16 = jnp.bfloat16
LANE = 128
SUB = 8
GRID_W = 64
POOL_WINDOWS = (2, 4, 8, 16)
LRU_C = 8.0
DEPTH = 2
ALPHA = float((2 * DEPTH) ** 0.25)
LN_EPS = 1e-5
ADAM_LR, ADAM_B1, ADAM_B2, ADAM_EPS, ADAM_WD, ADAM_STEP = 0.001, 0.9, 0.999, 1e-08, 0.01, 10
N_CHIPS = 4
N_DEV = 8
MESH = pl.DeviceIdType.MESH
ROW_TILE = 512
GATE_TILE = 2048
GATE_UNROLL = 1
CONV_TAPS = 4
CONV_LEFT = 2
PAD = 8
SCAN_UNROLL = 8
RS_TILE = 128
LN_ROWS = 128
POOL_CPAD = 16
VEC_KINDS = 4


def _call(body, **kw):
    return pl.pallas_call(body, **kw)


def _dot(a, b):
    return jnp.dot(a, b, preferred_element_type=F32)


def _dot_nt(a, b):
    return lax.dot_general(a, b, (((1,), (1,)), ((), ())), preferred_element_type=F32)


def _dot_tn(a, b):
    return lax.dot_general(a, b, (((0,), (0,)), ((), ())), preferred_element_type=F32)


def _sigmoid(v):
    return 0.5 * (jnp.tanh(0.5 * v) + 1.0)


def _silu(v):
    return v * _sigmoid(v)


def _dsilu(v):
    s = _sigmoid(v)
    return s * (1.0 + v * (1.0 - s))


def _log_sigmoid(v):
    z = jnp.exp(-jnp.abs(v))
    return jnp.minimum(v, 0.0) - jnp.where(z < 1e-4, z * (1.0 - 0.5 * z), jnp.log(1.0 + z))


def _one_minus_sq(la, a):
    return -jnp.tanh(la) * (a * a + 1.0)


def _cat(ref, n):
    return jnp.concatenate([ref[k] for k in range(n)], axis=1)


def _put_chunks(ref, val, n, base=0):
    for k in range(n):
        ref[base + k] = val[:, k * LANE:(k + 1) * LANE].astype(ref.dtype)


def _row_tile(rows, want):
    t = min(rows, want)
    assert rows % t == 0
    return t


ANY_SPEC = pl.BlockSpec(memory_space=pl.ANY)


def _mod_fwd(cvec, cctx, wm, bm):
    ns, nl, d, c3 = wm.shape

    def body(c_ref, cx_ref, w_ref, b_ref, o_ref):
        cc = jnp.concatenate([c_ref[...], cx_ref[...], jnp.zeros((SUB - 2, d), F32)], axis=0)
        o_ref[...] = _dot(_silu(cc).astype(BF16), w_ref[...]) + b_ref[...]

    return _call(
        body, name="mod_fwd", grid=(nl, ns),
        in_specs=[pl.BlockSpec((1, d), lambda l, s: (0, 0)),
                  pl.BlockSpec((1, d), lambda l, s: (0, 0)),
                  pl.BlockSpec((None, None, d, c3), lambda l, s: (s, l, 0, 0)),
                  pl.BlockSpec((None, 1, c3), lambda l, s: (l, 0, s))],
        out_specs=pl.BlockSpec((None, 8, c3), lambda l, s: (l, 0, s)),
        out_shape=jax.ShapeDtypeStruct((nl, 8, ns * c3), F32),
    )(cvec, cctx, wm, bm)


def _rows_kernel(parts, rows, cols, name):
    def body(*refs):
        o_ref = refs[-1]
        o_ref[...] = jnp.zeros_like(o_ref)
        for ref, (a, r0, c0) in zip(refs[:-1], parts):
            for k in range(a.shape[0]):
                o_ref[r0 + k:r0 + k + 1, c0:c0 + a.shape[1]] = ref[k:k + 1, :]

    return _call(body, name=name, grid=(1,),
                 in_specs=[pl.BlockSpec(a.shape, lambda i: (0, 0)) for a, _, _ in parts],
                 out_specs=pl.BlockSpec((rows, cols), lambda i: (0, 0)),
                 out_shape=jax.ShapeDtypeStruct((rows, cols), F32))(*[a for a, _, _ in parts])


def _mod_bwd_shard(gt, cctx, place, c3):
    d = cctx.shape[1]

    def body(p_ref, cs_ref, dm_ref, dmx_ref, cx_ref, o_ref):
        l = pl.program_id(0)
        lhs = jnp.concatenate([_silu(cs_ref[...]), _silu(cx_ref[...]), jnp.zeros((7, d), F32)], axis=0).astype(BF16)
        dmx = jnp.where(l == 0, jnp.sum(dmx_ref[...], axis=0, keepdims=True), 0.0)
        rhs = jnp.concatenate([dm_ref[...], dmx, jnp.zeros((7, c3), F32)], axis=0).astype(BF16)
        o_ref[...] = _dot_tn(lhs, rhs)

    return _call(
        body, name="mod_bwd_shard",
        grid_spec=pltpu.PrefetchScalarGridSpec(
            num_scalar_prefetch=1, grid=(DEPTH,),
            in_specs=[pl.BlockSpec((None, N_DEV, d), lambda l, p: (0, 0, 0)),
                      pl.BlockSpec((None, N_DEV, c3), lambda l, p: (1 + 2 * l, 0, p[1])),
                      pl.BlockSpec((None, N_DEV, c3), lambda l, p: (2, 0, p[1])),
                      pl.BlockSpec((1, d), lambda l, p: (0, 0))],
            out_specs=pl.BlockSpec((None, d, c3), lambda l, p: (l, 0, 0))),
        out_shape=jax.ShapeDtypeStruct((DEPTH, d, c3), F32),
    )(place, gt, gt, gt, cctx)


def _small_layout(d, e):
    k = VEC_KINDS
    return {
        "conv_b": ((1, e), [(0, k, 0)]),
        "ln_g": ((2, d), [(0, k, e), (1, k + 1, e)]),
        "pool_scale": ((1, e), [(0, k + 1, 0)]),
        "ln_b": ((2, d), [(0, k + 2, 0), (1, k + 2, d)]),
        "conv_w": ((CONV_TAPS, e), [(t, k + 3 + t, 0) for t in range(CONV_TAPS)]),
        "lru_ba": ((2, e), [(j, k + 7 + j, 0) for j in range(2)]),
        "lru_bx": ((2, e), [(j, k + 9 + j, 0) for j in range(2)]),
        "lru_lam": ((2, e), [(j, k + 11 + j, 0) for j in range(2)]),
    }


VEC_ROWS = 24


def _mod_bwd_rep(gt, cctx, wm):
    ns, _, d, c3 = wm.shape
    layout = _small_layout(d, ns * c3 - d)
    names = list(layout)

    def body(g_ref, cx_ref, w_ref, db_ref, dc_ref, *small_refs):
        dm0 = jnp.sum(g_ref[1], axis=0, keepdims=True)
        dmx = jnp.sum(g_ref[2], axis=0, keepdims=True)
        dm1 = jnp.sum(g_ref[3], axis=0, keepdims=True)
        db_ref[0:1, :] = dm0 + dmx
        db_ref[1:2, :] = dm1
        dmxb = jnp.broadcast_to(dmx, (SUB, ns * c3)).astype(BF16)
        acc = jnp.zeros((SUB, d), F32)
        for s in range(ns):
            acc = acc + _dot_nt(dmxb[:, s * c3:(s + 1) * c3], w_ref[s])
        dc_ref[...] = acc[0:1, :] * _dsilu(cx_ref[...])
        for ref, name in zip(small_refs, names):
            shape, places = layout[name]
            for arr_row, vec_row, col0 in places:
                total = jnp.sum(g_ref[vec_row], axis=0, keepdims=True)
                ref[arr_row:arr_row + 1, :] = total[:, col0:col0 + shape[1]]

    outs = _call(
        body, name="mod_bwd_rep", grid=(1,),
        in_specs=[pl.BlockSpec(gt.shape, lambda i: (0, 0, 0)),
                  pl.BlockSpec((1, d), lambda i: (0, 0)),
                  pl.BlockSpec((ns, None, d, c3), lambda i: (0, 0, 0, 0))],
        out_specs=[pl.BlockSpec((DEPTH, ns * c3), lambda i: (0, 0)), pl.BlockSpec((1, d), lambda i: (0, 0))]
        + [pl.BlockSpec(layout[n][0], lambda i: (0, 0)) for n in names],
        out_shape=[jax.ShapeDtypeStruct((DEPTH, ns * c3), F32), jax.ShapeDtypeStruct((1, d), F32)]
        + [jax.ShapeDtypeStruct(layout[n][0], F32) for n in names],
    )(gt, cctx, wm)
    return outs[0], outs[1], dict(zip(names, outs[2:]))


def _inproj_fwd(xin, sc1, sh, w, name):
    rows, d = xin.shape
    ns, _, n4 = w.shape
    cpb = n4 // LANE
    tm = _row_tile(rows, 256)
    assert ns in (2, 4)

    def body(x_ref, sc_ref, sh_ref, w_ref, *o_refs):
        h = (x_ref[...] * sc_ref[...] + sh_ref[...]).astype(BF16)
        for s in range(ns):
            _put_chunks(o_refs[s // 2], _dot(h, w_ref[s]), cpb, base=(s % 2) * cpb)

    spec = pl.BlockSpec((2 * cpb, tm, LANE), lambda i: (0, i, 0))
    dtypes = (F32, BF16)[:ns // 2]
    res = _call(
        body, name=name, grid=(rows // tm,),
        in_specs=[pl.BlockSpec((tm, d), lambda i: (i, 0)),
                  pl.BlockSpec((1, d), lambda i: (0, 0)),
                  pl.BlockSpec((1, d), lambda i: (0, 0)),
                  pl.BlockSpec((ns, d, n4), lambda i: (0, 0, 0))],
        out_specs=[spec] * len(dtypes),
        out_shape=[jax.ShapeDtypeStruct((2 * cpb, rows, LANE), t) for t in dtypes],
    )(xin, sc1, sh, w)
    return res[0] if ns == 2 else tuple(res)


def _inproj_bwd_x(dparts, xin, dxres, sc1, w, name):
    rows, d = xin.shape
    npart = len(dparts)
    e = dparts[0].shape[1]
    ns, _, n4 = w.shape
    per = e // n4
    assert per * npart == ns
    tm = _row_tile(rows, 256)
    has_res = dxres is not None

    def body(*refs):
        dp = refs[:npart]
        x_ref, sc_ref, w_ref = refs[npart:npart + 3]
        rest = refs[npart + 3:]
        if has_res:
            res_ref, dx_ref, dsc_ref, dsh_ref = rest
        else:
            dsc_ref, dsh_ref = rest
        i = pl.program_id(0)
        dh = jnp.zeros((tm, d), F32)
        for p in range(npart):
            v = dp[p][...]
            for q in range(per):
                dh = dh + _dot_nt(v[:, q * n4:(q + 1) * n4], w_ref[p * per + q])

        @pl.when(i == 0)
        def _():
            dsc_ref[...] = jnp.zeros_like(dsc_ref)
            dsh_ref[...] = jnp.zeros_like(dsh_ref)

        dsc_ref[...] += jnp.sum(dh * x_ref[...], axis=0, keepdims=True)
        dsh_ref[...] += jnp.sum(dh, axis=0, keepdims=True)
        if has_res:
            dx_ref[...] = res_ref[...] + dh * sc_ref[...]

    row_spec = pl.BlockSpec((tm, d), lambda i: (i, 0))
    vec_spec = pl.BlockSpec((1, d), lambda i: (0, 0))
    in_specs = [pl.BlockSpec((tm, e), lambda i: (i, 0))] * npart + [row_spec, vec_spec,
                                                                     pl.BlockSpec((ns, d, n4), lambda i: (0, 0, 0))]
    args = list(dparts) + [xin, sc1, w]
    out_specs, out_shape = [vec_spec, vec_spec], [jax.ShapeDtypeStruct((1, d), F32)] * 2
    if has_res:
        in_specs.append(row_spec)
        args.append(dxres)
        out_specs = [row_spec] + out_specs
        out_shape = [jax.ShapeDtypeStruct((rows, d), F32)] + out_shape
    return _call(body, name=name, grid=(rows // tm,), in_specs=in_specs, out_specs=out_specs, out_shape=out_shape)(*args)


def _inproj_bwd_w(xin, sc1, sh, dparts, init, gbuf, name):
    rows, d = xin.shape
    npart = len(dparts)
    e = dparts[0].shape[1]
    n4 = e // 2
    ns = 2 * npart
    tm = _row_tile(rows, 1024)
    nt = rows // tm
    has_init = init is not None
    into = gbuf is not None
    assert not into or (ns == N_CHIPS and gbuf.shape[2] == n4)

    def body(*refs):
        x_ref, sc_ref, sh_ref = refs[:3]
        dp = refs[3:3 + npart]
        init_ref = refs[3 + npart] if has_init else None
        o_ref = refs[-1]
        s, i = pl.program_id(0), pl.program_id(1)
        h = (x_ref[...] * sc_ref[...] + sh_ref[...]).astype(BF16)

        @pl.when(i == 0)
        def _():
            o_ref[...] = init_ref[...] if has_init else jnp.zeros_like(o_ref)

        for p in range(npart):
            @pl.when(s // 2 == p)
            def _(p=p):
                o_ref[...] += _dot_tn(h, dp[p][...])

    in_specs = [pl.BlockSpec((tm, d), lambda s, i: (i, 0)),
                pl.BlockSpec((1, d), lambda s, i: (0, 0)),
                pl.BlockSpec((1, d), lambda s, i: (0, 0))]
    in_specs += [pl.BlockSpec((tm, n4), lambda s, i: (i, s % 2))] * npart
    args = [xin, sc1, sh] + list(dparts)
    o_spec = pl.BlockSpec((None, d, n4), lambda s, i: (s, 0, 0))
    if has_init:
        in_specs.append(o_spec)
        args.append(init)
    extra = {}
    if into:
        in_specs.append(ANY_SPEC)
        args.append(gbuf)
        extra = dict(input_output_aliases={len(args) - 1: 0})
    out_shape = jax.ShapeDtypeStruct(gbuf.shape if into else (ns, d, n4), F32)
    return _call(body, name=name, grid=(ns, nt), in_specs=in_specs, out_specs=o_spec, out_shape=out_shape, **extra)(*args)


def _gated(y_ref, g_ref, nch):
    return jnp.concatenate([(y_ref[k].astype(F32) * _silu(g_ref[k].astype(F32))).astype(BF16) for k in range(nch)], axis=1)


def _ln_stats(r):
    mu = jnp.mean(r, axis=-1, keepdims=True)
    var = jnp.mean(jnp.square(r - mu), axis=-1, keepdims=True)
    rstd = lax.rsqrt(var + LN_EPS)
    return (r - mu) * rstd, rstd


def _outproj_fwd(y, ug, xin, gt, wout, lg, lb, target, name):
    nch, rows, _ = y.shape
    e, d = wout.shape
    tm = _row_tile(rows, 256)
    with_loss = target is not None

    def body(*refs):
        y_ref, g_ref, x_ref, gt_ref, w_ref, lg_ref, lb_ref = refs[:7]
        if with_loss:
            t_ref, br_ref, dxo_ref, loss_ref = refs[7:]
        else:
            br_ref, xo_ref = refs[7:]
        z = _gated(y_ref, g_ref, nch)
        br_ref[...] = _dot(z, w_ref[...])
        if with_loss:
            @pl.when(pl.program_id(0) == 0)
            def _():
                loss_ref[...] = jnp.zeros_like(loss_ref)

        def norm(j, c):
            rows = pl.ds(pl.multiple_of(j * LN_ROWS, LN_ROWS), LN_ROWS)
            xhat, _ = _ln_stats(ALPHA * x_ref[rows, :] + gt_ref[...] * br_ref[rows, :])
            xo = xhat * lg_ref[...] + lb_ref[...]
            if with_loss:
                err = xo - t_ref[rows, :]
                dxo_ref[rows, :] = err * (1.0 / d)
                col = jnp.sum(err * err, axis=0, keepdims=True)
                loss_ref[...] += sum(col[:, k * LANE:(k + 1) * LANE] for k in range(d // LANE))
            else:
                xo_ref[rows, :] = xo
            return c

        lax.fori_loop(0, tm // LN_ROWS, norm, 0)

    chunk_spec = pl.BlockSpec((nch, tm, LANE), lambda i: (0, i, 0))
    g_spec = chunk_spec
    row_spec = pl.BlockSpec((tm, d), lambda i: (i, 0))
    vec_spec = pl.BlockSpec((1, d), lambda i: (0, 0))
    in_specs = [chunk_spec, g_spec, row_spec, vec_spec, pl.BlockSpec((e, d), lambda i: (0, 0)), vec_spec, vec_spec]
    args = [y, ug, xin, gt, wout, lg, lb]
    out_specs = [row_spec, row_spec]
    out_shape = [jax.ShapeDtypeStruct((rows, d), F32)] * 2
    if with_loss:
        in_specs.append(row_spec)
        args.append(target)
        out_specs.append(pl.BlockSpec((1, LANE), lambda i: (0, 0)))
        out_shape.append(jax.ShapeDtypeStruct((1, LANE), F32))
    return _call(body, name=name, grid=(rows // tm,), in_specs=in_specs, out_specs=out_specs, out_shape=out_shape)(*args)


def _outproj_bwd(dxo, xin, br, y, ug, gt, lg, wout, name):
    nch, rows, _ = y.shape
    e, d = wout.shape
    tm = _row_tile(rows, 256)

    def body(dxo_ref, x_ref, br_ref, y_ref, g_ref, gt_ref, lg_ref, w_ref,
             dy_ref, dg_ref, dxres_ref, dbr_ref, dlg_ref, dlb_ref, dgt_ref):
        @pl.when(pl.program_id(0) == 0)
        def _():
            dlg_ref[...] = jnp.zeros_like(dlg_ref)
            dlb_ref[...] = jnp.zeros_like(dlb_ref)
            dgt_ref[...] = jnp.zeros_like(dgt_ref)

        def norm_bwd(j, c):
            rows = pl.ds(pl.multiple_of(j * LN_ROWS, LN_ROWS), LN_ROWS)
            dxo_v = dxo_ref[rows, :]
            brv = br_ref[rows, :]
            xhat, rstd = _ln_stats(ALPHA * x_ref[rows, :] + gt_ref[...] * brv)
            dxh = dxo_v * lg_ref[...]
            dr = rstd * (dxh - jnp.mean(dxh, axis=-1, keepdims=True) - xhat * jnp.mean(dxh * xhat, axis=-1, keepdims=True))
            dlg_ref[...] += jnp.sum(dxo_v * xhat, axis=0, keepdims=True)
            dlb_ref[...] += jnp.sum(dxo_v, axis=0, keepdims=True)
            dgt_ref[...] += jnp.sum(dr * brv, axis=0, keepdims=True)
            dxres_ref[rows, :] = ALPHA * dr
            dbr_ref[rows, :] = (gt_ref[...] * dr).astype(BF16)
            return c

        lax.fori_loop(0, tm // LN_ROWS, norm_bwd, 0)
        dz = _dot_nt(dbr_ref[...], w_ref[...])
        for k in range(nch):
            dzk = dz[:, k * LANE:(k + 1) * LANE]
            gk = g_ref[k].astype(F32)
            dy_ref[k] = dzk * _silu(gk)
            dg_ref[:, k * LANE:(k + 1) * LANE] = (dzk * y_ref[k].astype(F32) * _dsilu(gk)).astype(BF16)

    chunk_spec = pl.BlockSpec((nch, tm, LANE), lambda i: (0, i, 0))
    g_spec = chunk_spec
    row_spec = pl.BlockSpec((tm, d), lambda i: (i, 0))
    vec_spec = pl.BlockSpec((1, d), lambda i: (0, 0))
    return _call(
        body, name=name, grid=(rows // tm,),
        in_specs=[row_spec, row_spec, row_spec, chunk_spec, g_spec, vec_spec, vec_spec, pl.BlockSpec((e, d), lambda i: (0, 0))],
        out_specs=[chunk_spec, pl.BlockSpec((tm, e), lambda i: (i, 0)), row_spec, row_spec, vec_spec, vec_spec, vec_spec],
        out_shape=[jax.ShapeDtypeStruct((nch, rows, LANE), F32), jax.ShapeDtypeStruct((rows, e), BF16),
                   jax.ShapeDtypeStruct((rows, d), F32), jax.ShapeDtypeStruct((rows, d), BF16)]
        + [jax.ShapeDtypeStruct((1, d), F32)] * 3,
    )(dxo, xin, br, y, ug, gt, lg, wout)


def _outproj_bwd_w(y, ug, dbr, gbuf, row0, name):
    nch, rows, _ = y.shape
    d = dbr.shape[1]
    e = nch * LANE
    es = e // N_CHIPS
    tm = _row_tile(rows, 512)
    assert gbuf.shape[2] == d and row0 % es == 0

    def body(y_ref, g_ref, dbr_ref, buf_ref, o_ref):
        @pl.when(pl.program_id(0) == 0)
        def _():
            o_ref[...] = jnp.zeros_like(o_ref)

        z = _gated(y_ref, g_ref, nch)
        o_ref[...] += _dot_tn(z, dbr_ref[...]).reshape(N_CHIPS, es, d)

    return _call(
        body, name=name, grid=(rows // tm,),
        in_specs=[pl.BlockSpec((nch, tm, LANE), lambda i: (0, i, 0)),
                  pl.BlockSpec((nch, tm, LANE), lambda i: (0, i, 0)),
                  pl.BlockSpec((tm, d), lambda i: (i, 0)),
                  ANY_SPEC],
        out_specs=pl.BlockSpec((N_CHIPS, es, d), lambda i: (0, row0 // es, 0)),
        out_shape=jax.ShapeDtypeStruct(gbuf.shape, F32),
        input_output_aliases={3: 0},
    )(y, ug, dbr, gbuf)


def _scan(a_ref, b_ref, h_ref, *, length, init, reverse, a_shift, store):
    nblk = length // SUB
    assert nblk % SCAN_UNROLL == 0
    row = lax.broadcasted_iota(jnp.int32, (SUB, LANE), 0)
    last = 0 if reverse else SUB - 1
    edges = [(row >= SUB - k) if reverse else (row < k) for k in (1, 2, 4)]

    def local_scan(a, b):
        for k, edge in zip((1, 2, 4), edges):
            sh = (SUB - k) if reverse else k
            b = b + a * jnp.where(edge, 0.0, pltpu.roll(b, sh, 0))
            a = a * jnp.where(edge, 1.0, pltpu.roll(a, sh, 0))
        return a, b

    def step(i, carry):
        base = pl.multiple_of(((nblk // SCAN_UNROLL - 1 - i) if reverse else i) * (SCAN_UNROLL * SUB), SCAN_UNROLL * SUB)
        order = range(SCAN_UNROLL - 1, -1, -1) if reverse else range(SCAN_UNROLL)
        loaded = [(a_ref[pl.ds(PAD + base + j * SUB + a_shift, SUB), :], b_ref[pl.ds(PAD + base + j * SUB, SUB), :])
                  for j in order]
        scanned = [local_scan(a, b) for a, b in loaded]
        for j, (a, b) in zip(order, scanned):
            if store:
                h_ref[pl.ds(PAD + base + j * SUB, SUB), :] = b + a * carry
            a_l = jnp.broadcast_to(a[last:last + 1, :], (SUB, LANE))
            b_l = jnp.broadcast_to(b[last:last + 1, :], (SUB, LANE))
            carry = b_l + a_l * carry
        return carry

    carry = lax.fori_loop(0, nblk // SCAN_UNROLL, step, jnp.broadcast_to(init, (SUB, LANE)))
    return carry[0:1, :]


def _conv_fwd(src_ref, upad, u_ref, cw, cb, length):
    zeros = jnp.zeros((PAD, LANE), F32)
    upad[pl.ds(0, PAD), :] = zeros
    upad[pl.ds(PAD + length, PAD), :] = zeros
    rt = _row_tile(length, ROW_TILE)

    def copy(i, c):
        t0 = pl.multiple_of(i * rt, rt)
        upad[pl.ds(PAD + t0, rt), :] = src_ref[pl.ds(t0, rt), :]
        return c

    lax.fori_loop(0, length // rt, copy, 0)

    def tile(i, c):
        t0 = pl.multiple_of(i * rt, rt)
        acc = jnp.zeros((rt, LANE), F32)
        for k in range(CONV_TAPS):
            acc = acc + upad[pl.ds(t0 + PAD - CONV_LEFT + k, rt), :] * cw[k:k + 1, :]
        u_ref[pl.ds(t0, rt), :] = acc + cb
        return c

    lax.fori_loop(0, length // rt, tile, 0)


def _gates_fwd(u_ref, a_ref, b_ref, wa, wx, ba, bx, ls, length, keep=None):
    rt = _row_tile(length, GATE_TILE)

    def tile(i, c):
        t0 = pl.multiple_of(i * rt, rt)
        ut = u_ref[pl.ds(t0, rt), :]
        ub = ut.astype(BF16)
        r = _sigmoid(_dot(ub, wa) + ba)
        ig = _sigmoid(_dot(ub, wx) + bx)
        if keep is not None:
            keep[0][pl.ds(t0, rt), :] = r
            keep[1][pl.ds(t0, rt), :] = ig
        la = (LRU_C * r) * ls
        a = jnp.exp(la)
        a_ref[pl.ds(PAD + t0, rt), :] = a
        b_ref[pl.ds(PAD + t0, rt), :] = jnp.sqrt(_one_minus_sq(la, a)) * (ig * ut)
        return c

    lax.fori_loop(0, length // rt, tile, 0, unroll=min(GATE_UNROLL, length // rt))


def _lru_specs():
    return [pl.BlockSpec((CONV_TAPS, LANE), lambda n: (0, n)),
            pl.BlockSpec((1, LANE), lambda n: (0, n)),
            pl.BlockSpec((2, None, LANE, LANE), lambda n: (0, n, 0, 0)),
            pl.BlockSpec((2, None, LANE, LANE), lambda n: (0, n, 0, 0)),
            pl.BlockSpec((2, LANE), lambda n: (0, n)),
            pl.BlockSpec((2, LANE), lambda n: (0, n)),
            pl.BlockSpec((2, LANE), lambda n: (0, n))]


def _rglru_fwd(ug, uc, conv_w, conv_b, wa, wx, ba, bx, lam):
    nb = uc.shape[0]
    s_len, t_len = ug.shape[1], uc.shape[1]

    def body(u0_ref, uc0_ref, cw_ref, cb_ref, wa_ref, wx_ref, ba_ref, bx_ref, lam_ref, y_ref,
             upad, ubuf, abuf, hbuf):
        cw, cb = cw_ref[...], cb_ref[...]
        lsig = _log_sigmoid(lam_ref[...])
        zero = jnp.zeros((1, LANE), F32)
        _conv_fwd(uc0_ref, upad, ubuf, cw, cb, t_len)
        h0 = []
        for dr in range(2):
            _gates_fwd(ubuf, abuf, hbuf, wa_ref[dr], wx_ref[dr], ba_ref[dr:dr + 1, :], bx_ref[dr:dr + 1, :],
                       lsig[dr:dr + 1, :], t_len)
            h0.append(_scan(abuf, hbuf, hbuf, length=t_len, init=zero, reverse=(dr == 1), a_shift=0, store=False))
        _conv_fwd(u0_ref, upad, ubuf, cw, cb, s_len)
        rt = _row_tile(s_len, ROW_TILE)
        for dr in range(2):
            _gates_fwd(ubuf, abuf, hbuf, wa_ref[dr], wx_ref[dr], ba_ref[dr:dr + 1, :], bx_ref[dr:dr + 1, :],
                       lsig[dr:dr + 1, :], s_len)
            _scan(abuf, hbuf, hbuf, length=s_len, init=h0[dr], reverse=(dr == 1), a_shift=0, store=True)

            def acc(i, c, dr=dr):
                t0 = pl.multiple_of(i * rt, rt)
                h = hbuf[pl.ds(PAD + t0, rt), :]
                if dr == 0:
                    upad[pl.ds(PAD + t0, rt), :] = h
                else:
                    y_ref[pl.ds(t0, rt), :] = (upad[pl.ds(PAD + t0, rt), :] + h).astype(y_ref.dtype)
                return c

            lax.fori_loop(0, s_len // rt, acc, 0)

    seq = pltpu.VMEM((s_len + 2 * PAD, LANE), F32)
    return _call(
        body, name="rglru_fwd", grid=(nb,),
        in_specs=[pl.BlockSpec((None, s_len, LANE), lambda n: (n, 0, 0)),
                  pl.BlockSpec((None, t_len, LANE), lambda n: (n, 0, 0))] + _lru_specs(),
        out_specs=pl.BlockSpec((None, s_len, LANE), lambda n: (n, 0, 0)),
        out_shape=jax.ShapeDtypeStruct((nb, s_len, LANE), BF16),
        scratch_shapes=[seq, pltpu.VMEM((s_len, LANE), F32), seq, seq],
    )(ug, uc, conv_w, conv_b, wa, wx, ba, bx, lam)


def _rglru_bwd(ug, uc, dy, conv_w, conv_b, wa, wx, ba, bx, lam):
    nb = uc.shape[0]
    e = nb * LANE
    s_len, t_len = ug.shape[1], uc.shape[1]

    def body(u0_ref, uc0_ref, dy_ref, cw_ref, cb_ref, wa_ref, wx_ref, ba_ref, bx_ref, lam_ref,
             du_ref, duc_ref, dcw_ref, dcb_ref, dwa_ref, dwx_ref, dba_ref, dbx_ref, dlam_ref,
             upad, ubuf, abuf, hbuf, lbuf, dubuf, rbuf, ibuf, cpad, cu, ca0, ch0, ca1, ch1, cr0, ci0, cr1, ci1):
        cw, cb = cw_ref[...], cb_ref[...]
        lam_v = lam_ref[...]
        lsig = _log_sigmoid(lam_v)
        zero = jnp.zeros((1, LANE), F32)
        zpad = jnp.zeros((PAD, LANE), F32)
        for ref in (dcw_ref, dcb_ref, dwa_ref, dwx_ref, dba_ref, dbx_ref, dlam_ref):
            ref[...] = jnp.zeros_like(ref)

        def params(dr):
            return (wa_ref[dr], wx_ref[dr], ba_ref[dr:dr + 1, :], bx_ref[dr:dr + 1, :], lsig[dr:dr + 1, :])

        def direction_bwd(dr, u_ref, a_ref, h_ref, l_ref, gates, dub, length, first):
            wa_d, wx_d, ba_d, bx_d, ls_d = params(dr)
            rt = _row_tile(length, GATE_TILE)
            prev = 1 if dr == 1 else -1

            def tile(i, c):
                t0 = pl.multiple_of(i * rt, rt)
                ut = u_ref[pl.ds(t0, rt), :]
                ub = ut.astype(BF16)
                r = gates[0][pl.ds(t0, rt), :]
                ig = gates[1][pl.ds(t0, rt), :]
                la = (LRU_C * r) * ls_d
                a = a_ref[pl.ds(PAD + t0, rt), :]
                q = _one_minus_sq(la, a)
                rs = lax.rsqrt(q)
                sq = q * rs
                lm = l_ref[pl.ds(PAD + t0, rt), :]
                da = lm * h_ref[pl.ds(PAD + t0 + prev, rt), :]
                dsq = lm * ig * ut
                dig = lm * sq * ut
                dla = da * a - dsq * (a * a) * rs
                dr_ = dla * (LRU_C * ls_d)
                dlam_ref[dr:dr + 1, :] += jnp.sum(dla * (LRU_C * r), axis=0, keepdims=True)
                dpr = dr_ * r * (1.0 - r)
                dpi = dig * ig * (1.0 - ig)
                dba_ref[dr:dr + 1, :] += jnp.sum(dpr, axis=0, keepdims=True)
                dbx_ref[dr:dr + 1, :] += jnp.sum(dpi, axis=0, keepdims=True)
                dprb, dpib = dpr.astype(BF16), dpi.astype(BF16)
                dwa_ref[dr] += _dot_tn(ub, dprb)
                dwx_ref[dr] += _dot_tn(ub, dpib)
                dut = lm * sq * ig + _dot_nt(dprb, wa_d) + _dot_nt(dpib, wx_d)
                if first:
                    dub[pl.ds(PAD + t0, rt), :] = dut
                else:
                    dub[pl.ds(PAD + t0, rt), :] += dut
                return c

            lax.fori_loop(0, length // rt, tile, 0, unroll=min(GATE_UNROLL, length // rt))

        def conv_bwd(dub, src_pad, out_ref, length):
            rt = _row_tile(length, ROW_TILE)

            def tile(i, c):
                t0 = pl.multiple_of(i * rt, rt)
                dut = dub[pl.ds(PAD + t0, rt), :]
                dcb_ref[...] += jnp.sum(dut, axis=0, keepdims=True)
                acc = jnp.zeros((rt, LANE), F32)
                for k in range(CONV_TAPS):
                    sh = CONV_LEFT - k
                    acc = acc + dub[pl.ds(PAD + t0 + sh, rt), :] * cw[k:k + 1, :]
                    dcw_ref[k:k + 1, :] += jnp.sum(dut * src_pad[pl.ds(PAD + t0 - sh, rt), :], axis=0, keepdims=True)
                out_ref[pl.ds(t0, rt), :] = acc.astype(out_ref.dtype)
                return c

            lax.fori_loop(0, length // rt, tile, 0)

        _conv_fwd(uc0_ref, cpad, cu, cw, cb, t_len)
        cbufs = ((ca0, ch0), (ca1, ch1))
        cgates = ((cr0, ci0), (cr1, ci1))
        h0 = []
        for dr in range(2):
            ca, chh = cbufs[dr]
            _gates_fwd(cu, ca, chh, *params(dr), t_len, keep=cgates[dr])
            h0.append(_scan(ca, chh, chh, length=t_len, init=zero, reverse=(dr == 1), a_shift=0, store=True))
        _conv_fwd(u0_ref, upad, ubuf, cw, cb, s_len)
        rt = _row_tile(s_len, ROW_TILE)
        dh0 = []
        for dr in range(2):
            rev = dr == 1
            _gates_fwd(ubuf, abuf, hbuf, *params(dr), s_len, keep=(rbuf, ibuf))
            _scan(abuf, hbuf, hbuf, length=s_len, init=h0[dr], reverse=rev, a_shift=0, store=True)
            first_row = PAD + s_len if rev else PAD - 1
            hbuf[pl.ds(first_row, 1), :] = h0[dr]
            end_row = PAD - 1 if rev else PAD + s_len
            abuf[pl.ds(end_row, 1), :] = zero

            def copy(i, c):
                t0 = pl.multiple_of(i * rt, rt)
                lbuf[pl.ds(PAD + t0, rt), :] = dy_ref[pl.ds(t0, rt), :]
                return c

            lax.fori_loop(0, s_len // rt, copy, 0)
            _scan(abuf, lbuf, lbuf, length=s_len, init=zero, reverse=not rev, a_shift=(-1 if rev else 1), store=True)
            start = PAD + s_len - 1 if rev else PAD
            dh0.append(abuf[pl.ds(start, 1), :] * lbuf[pl.ds(start, 1), :])
            direction_bwd(dr, ubuf, abuf, hbuf, lbuf, (rbuf, ibuf), dubuf, s_len, first=(dr == 0))
        dubuf[pl.ds(0, PAD), :] = zpad
        dubuf[pl.ds(PAD + s_len, PAD), :] = zpad
        conv_bwd(dubuf, upad, du_ref, s_len)
        lc = lbuf
        duc_buf = dubuf
        for dr in range(2):
            rev = dr == 1
            ca, chh = cbufs[dr]
            first_row = PAD + t_len if rev else PAD - 1
            chh[pl.ds(first_row, 1), :] = zero
            end_row = PAD - 1 if rev else PAD + t_len
            ca[pl.ds(end_row, 1), :] = zero + 1.0
            rtc = _row_tile(t_len, ROW_TILE)

            def clear(i, c):
                t0 = pl.multiple_of(i * rtc, rtc)
                lc[pl.ds(PAD + t0, rtc), :] = jnp.zeros((rtc, LANE), F32)
                return c

            lax.fori_loop(0, t_len // rtc, clear, 0)
            _scan(ca, lc, lc, length=t_len, init=dh0[dr], reverse=not rev, a_shift=(-1 if rev else 1), store=True)
            direction_bwd(dr, cu, ca, chh, lc, cgates[dr], duc_buf, t_len, first=(dr == 0))
        duc_buf[pl.ds(0, PAD), :] = zpad
        duc_buf[pl.ds(PAD + t_len, PAD), :] = zpad
        conv_bwd(duc_buf, cpad, duc_ref, t_len)
        dlam_ref[...] = dlam_ref[...] * (1.0 - _sigmoid(lam_v))

    seq = pltpu.VMEM((s_len + 2 * PAD, LANE), F32)
    cseq = pltpu.VMEM((t_len + 2 * PAD, LANE), F32)
    flat = pltpu.VMEM((s_len, LANE), F32)
    cflat = pltpu.VMEM((t_len, LANE), F32)
    vec2 = pl.BlockSpec((2, LANE), lambda n: (0, n))
    wspec = pl.BlockSpec((2, None, LANE, LANE), lambda n: (0, n, 0, 0))
    return _call(
        body, name="rglru_bwd", grid=(nb,),
        in_specs=[pl.BlockSpec((None, s_len, LANE), lambda n: (n, 0, 0)),
                  pl.BlockSpec((None, t_len, LANE), lambda n: (n, 0, 0)),
                  pl.BlockSpec((None, s_len, LANE), lambda n: (n, 0, 0))] + _lru_specs(),
        out_specs=[pl.BlockSpec((s_len, LANE), lambda n: (0, n)),
                   pl.BlockSpec((t_len, LANE), lambda n: (0, n)),
                   pl.BlockSpec((CONV_TAPS, LANE), lambda n: (0, n)),
                   pl.BlockSpec((1, LANE), lambda n: (0, n)),
                   wspec, wspec, vec2, vec2, vec2],
        out_shape=[jax.ShapeDtypeStruct((s_len, e), BF16), jax.ShapeDtypeStruct((t_len, e), BF16),
                   jax.ShapeDtypeStruct((CONV_TAPS, e), F32), jax.ShapeDtypeStruct((1, e), F32),
                   jax.ShapeDtypeStruct((2, nb, LANE, LANE), F32), jax.ShapeDtypeStruct((2, nb, LANE, LANE), F32),
                   jax.ShapeDtypeStruct((2, e), F32), jax.ShapeDtypeStruct((2, e), F32), jax.ShapeDtypeStruct((2, e), F32)],
        scratch_shapes=[seq, flat, seq, seq, seq, seq, flat, flat,
                        cseq, cflat, cseq, cseq, cseq, cseq, cflat, cflat, cflat, cflat],
    )(ug, uc, dy, conv_w, conv_b, wa, wx, ba, bx, lam)


def _pool_windows(src_ref, out_ref, colbuf, rowbuf, half, transpose, s_len):
    gw = GRID_W
    lg = gw.bit_length() - 1
    n_rows = s_len // gw
    cp, rm = POOL_CPAD, 8 * gw
    stride = gw + 2 * cp
    rt = _row_tile(s_len, ROW_TILE)
    assert rt % gw == 0 and half <= cp
    gpt = rt // gw
    offs = range(-half, half)
    zmargin = jnp.zeros((cp, LANE), F32)

    def zcol(r, c):
        base = pl.multiple_of(r * stride, SUB)
        colbuf[pl.ds(base, cp), :] = zmargin
        colbuf[pl.ds(base + cp + gw, cp), :] = zmargin
        return c

    lax.fori_loop(0, n_rows, zcol, 0)

    def zrow(i, c):
        t0 = pl.multiple_of(i * gw, gw)
        rowbuf[pl.ds(t0, gw), :] = jnp.zeros((gw, LANE), F32)
        rowbuf[pl.ds(rm + s_len + t0, gw), :] = jnp.zeros((gw, LANE), F32)
        return c

    lax.fori_loop(0, rm // gw, zrow, 0)

    col = lax.broadcasted_iota(jnp.int32, (gw, LANE), 0)
    ccnt = (jnp.minimum(col + half, gw) - jnp.maximum(col - half, 0)).astype(F32)

    def row_counts(t0):
        row = (t0 + lax.broadcasted_iota(jnp.int32, (rt, LANE), 0)) >> lg
        return (jnp.minimum(row + half, n_rows) - jnp.maximum(row - half, 0)).astype(F32)

    def col_base(t0, g):
        return pl.multiple_of((t0 // gw) * stride, SUB) + g * stride + cp

    def col_sum(t0, g, sign):
        acc = jnp.zeros((gw, LANE), F32)
        for o in offs:
            acc = acc + colbuf[pl.ds(col_base(t0, g) + sign * o, gw), :]
        return acc

    def row_sum(t0, sign):
        acc = jnp.zeros((rt, LANE), F32)
        for o in offs:
            acc = acc + rowbuf[pl.ds(rm + t0 + sign * o * gw, rt), :]
        return acc

    n_tiles = s_len // rt
    assert rt >= half * gw

    def loop(fn, edges=False):
        def step(i, c):
            t0 = pl.multiple_of(i * rt, rt)
            fn(t0, False) if edges else fn(t0)
            return c
        if edges:
            fn(0, True)
            if n_tiles > 1:
                fn(s_len - rt, True)
            lax.fori_loop(1, n_tiles - 1, step, 0)
        else:
            lax.fori_loop(0, n_tiles, step, 0)

    inv_ccnt = 1.0 / ccnt

    def by_row_count(v, t0, edge):
        return v / row_counts(t0) if edge else v * (1.0 / (2 * half))

    if not transpose:
        def fill(t0):
            for g in range(gpt):
                colbuf[pl.ds(col_base(t0, g), gw), :] = src_ref[pl.ds(t0 + g * gw, gw), :]

        def cols(t0):
            for g in range(gpt):
                rowbuf[pl.ds(rm + t0 + g * gw, gw), :] = col_sum(t0, g, 1) * inv_ccnt

        def rows(t0, edge):
            mean = by_row_count(row_sum(t0, 1), t0, edge)
            out_ref[pl.ds(t0, rt), :] = (mean - src_ref[pl.ds(t0, rt), :]).astype(out_ref.dtype)

        loop(fill)
        loop(cols)
        loop(rows, edges=True)
    else:
        def fill(t0, edge):
            rowbuf[pl.ds(rm + t0, rt), :] = by_row_count(src_ref[pl.ds(t0, rt), :], t0, edge)

        def rows(t0):
            acc = row_sum(t0, -1)
            for g in range(gpt):
                colbuf[pl.ds(col_base(t0, g), gw), :] = acc[g * gw:(g + 1) * gw, :] * inv_ccnt

        def cols(t0):
            for g in range(gpt):
                rows_g = pl.ds(t0 + g * gw, gw)
                out_ref[rows_g, :] = (col_sum(t0, g, -1) - src_ref[rows_g, :]).astype(out_ref.dtype)

        loop(fill, edges=True)
        loop(rows)
        loop(cols)


def _pool_map(src, nb, transpose, out_chunk_major, name):
    s_len = src.shape[1]
    cpg = nb // len(POOL_WINDOWS)

    def body(src_ref, out_ref, colbuf, rowbuf):
        n = pl.program_id(0)
        for gi, w in enumerate(POOL_WINDOWS):
            @pl.when(n // cpg == gi)
            def _(w=w):
                _pool_windows(src_ref, out_ref, colbuf, rowbuf, w // 2, transpose, s_len)

    if out_chunk_major:
        out_spec = pl.BlockSpec((None, s_len, LANE), lambda n: (n, 0, 0))
        out_shape = jax.ShapeDtypeStruct((nb, s_len, LANE), BF16)
    else:
        out_spec = pl.BlockSpec((s_len, LANE), lambda n: (0, n))
        out_shape = jax.ShapeDtypeStruct((s_len, nb * LANE), BF16)
    return _call(
        body, name=name, grid=(nb,),
        in_specs=[pl.BlockSpec((None, s_len, LANE), lambda n: (n, 0, 0))],
        out_specs=out_spec, out_shape=out_shape,
        scratch_shapes=[pltpu.VMEM((s_len // GRID_W * (GRID_W + 2 * POOL_CPAD), LANE), F32),
                        pltpu.VMEM((s_len + 16 * GRID_W, LANE), F32)],
    )(src)


def _group_weight(w_ref):
    return jnp.concatenate([w_ref[k] for k in range(N_CHIPS)], axis=0)


def _pool_mm_fwd(dm, wp, scale):
    nb, rows, _ = dm.shape
    _, ng, pq, pg = wp.shape
    cpg = pg // LANE
    tm = _row_tile(rows, 2048)

    def body(d_ref, w_ref, s_ref, y_ref):
        _put_chunks(y_ref, _dot(_cat(d_ref, cpg), _group_weight(w_ref)) * s_ref[...], cpg)

    cspec = pl.BlockSpec((cpg, tm, LANE), lambda i, g: (g, i, 0))
    return _call(
        body, name="pool_mm_fwd", grid=(rows // tm, ng),
        in_specs=[cspec, pl.BlockSpec((N_CHIPS, None, pq, pg), lambda i, g: (0, g, 0, 0)),
                  pl.BlockSpec((1, pg), lambda i, g: (0, g))],
        out_specs=cspec, out_shape=jax.ShapeDtypeStruct((nb, rows, LANE), BF16),
    )(dm, wp, scale)


def _pool_mm_bwd(dy, dm, wp, scale, gbuf, row0):
    nb, rows, _ = dm.shape
    _, ng, pq, pg = wp.shape
    cpg = pg // LANE
    tm = _row_tile(rows, 1024)
    nt = rows // tm
    assert gbuf.shape[2] == 2 * pg and row0 % pq == 0

    def body(dy_ref, d_ref, w_ref, s_ref, buf_ref, dd_ref, dwp_ref, dsc_ref, acc):
        i = pl.program_id(1)

        @pl.when(i == 0)
        def _():
            acc[...] = jnp.zeros_like(acc)
            dsc_ref[...] = jnp.zeros_like(dsc_ref)

        dyv = _cat(dy_ref, cpg)
        dc = _cat(d_ref, cpg)
        w = _group_weight(w_ref)
        dsc_ref[...] += jnp.sum(dyv * _dot(dc, w), axis=0, keepdims=True)
        dyp = (dyv * s_ref[...]).astype(BF16)
        _put_chunks(dd_ref, _dot_nt(dyp, w), cpg)
        acc[...] += _dot_tn(dc, dyp)

        @pl.when(i == nt - 1)
        def _():
            dwp_ref[...] = acc[...].reshape(N_CHIPS, pq, pg)

    cspec = pl.BlockSpec((cpg, tm, LANE), lambda g, i: (g, i, 0))
    sspec = pl.BlockSpec((1, pg), lambda g, i: (0, g))
    return _call(
        body, name="pool_mm_bwd", grid=(ng, nt),
        in_specs=[cspec, cspec, pl.BlockSpec((N_CHIPS, None, pq, pg), lambda g, i: (0, g, 0, 0)), sspec, ANY_SPEC],
        out_specs=[cspec, pl.BlockSpec((N_CHIPS, pq, pg), lambda g, i: (0, row0 // pq + g // 2, g % 2)), sspec],
        out_shape=[jax.ShapeDtypeStruct((nb, rows, LANE), F32), jax.ShapeDtypeStruct(gbuf.shape, F32),
                   jax.ShapeDtypeStruct((1, ng * pg), F32)],
        scratch_shapes=[pltpu.VMEM((pg, pg), F32)],
        input_output_aliases={4: 1},
    )(dy, dm, wp, scale, gbuf)


def _adamw_math(w, g, m, v):
    nm = ADAM_B1 * m + (1.0 - ADAM_B1) * g
    nv = ADAM_B2 * v + (1.0 - ADAM_B2) * jnp.square(g)
    m_hat = nm / (1.0 - ADAM_B1 ** ADAM_STEP)
    v_hat = nv / (1.0 - ADAM_B2 ** ADAM_STEP)
    return -ADAM_LR * (m_hat / (jnp.sqrt(v_hat) + ADAM_EPS) + ADAM_WD * w), nm, nv


def _adamw_param(w3, m3, v3, gsrcs, pick, tm, name):
    n_blk, rows, cols = w3.shape
    ng = len(gsrcs)

    def body(*refs):
        w_ref, m_ref, v_ref = refs[:3]
        g_refs = refs[3:3 + ng]
        go_ref, d_ref, nm_ref, nv_ref = refs[3 + ng:]
        g = pick(pl.program_id(0), [r[...] for r in g_refs])
        go_ref[...] = g
        d_ref[...], nm_ref[...], nv_ref[...] = _adamw_math(w_ref[...], g, m_ref[...], v_ref[...])

    spec = pl.BlockSpec((None, tm, cols), lambda n, i: (n, i, 0))
    return _call(
        body, name=name, grid=(n_blk, rows // tm),
        in_specs=[spec] * 3 + [pl.BlockSpec(shape, imap) for _, shape, imap in gsrcs],
        out_specs=[spec] * 4, out_shape=[jax.ShapeDtypeStruct(w3.shape, F32)] * 4,
    )(w3, m3, v3, *[a for a, _, _ in gsrcs])


def _adamw_small(quads):
    n = len(quads)

    def body(*refs):
        ins, outs = refs[:4 * n], refs[4 * n:]
        for k in range(n):
            w, g, m, v = (r[...] for r in ins[4 * k:4 * k + 4])
            outs[3 * k][...], outs[3 * k + 1][...], outs[3 * k + 2][...] = _adamw_math(w, g, m, v)

    flat = [a for q in quads for a in q]
    res = _call(body, name="adamw_small", grid=(1,),
                in_specs=[pl.BlockSpec(a.shape, lambda i: (0, 0)) for a in flat],
                out_specs=[pl.BlockSpec(q[0].shape, lambda i: (0, 0)) for q in quads for _ in range(3)],
                out_shape=[jax.ShapeDtypeStruct(q[0].shape, F32) for q in quads for _ in range(3)])(*flat)
    return [tuple(res[3 * k:3 * k + 3]) for k in range(n)]


def _place():
    return lax.axis_index("x"), lax.axis_index("y"), lax.axis_index("c")


def _other_chips(x, y):
    return [(1 - x, y), (x, 1 - y), (1 - x, 1 - y)]


def _gather_chips(arrays, name):
    n = len(arrays)
    halves = [a.shape[0] // 2 for a in arrays]
    for a, h in zip(arrays, halves):
        assert 2 * h == a.shape[0] and h % (32 // a.dtype.itemsize) == 0
    lands = [jnp.broadcast_to(a[None], (N_CHIPS,) + a.shape) for a in arrays]

    def body(*refs):
        outs = refs[n:2 * n]
        send_sems, recv_sems = refs[2 * n:]
        x, y, c = _place()
        me = 2 * x + y
        chips = _other_chips(x, y)

        def mine(k):
            return pl.ds(c * halves[k], halves[k])

        def theirs(k):
            return pl.ds((1 - c) * halves[k], halves[k])

        def push(k, j, src, dst, to):
            return pltpu.make_async_remote_copy(src_ref=src, dst_ref=dst, send_sem=send_sems.at[6 * k + j],
                                                recv_sem=recv_sems.at[6 * k + j], device_id=to, device_id_type=MESH)

        started = []
        for j, (cx, cy) in enumerate(chips):
            for k in range(n):
                own = outs[k].at[me, mine(k)]
                cp = push(k, j, own, own, (cx, cy, c))
                cp.start()
                started.append(cp)
        for j, (cx, cy) in enumerate(chips):
            for k in range(n):
                slab = outs[k].at[2 * cx + cy, mine(k)]
                push(k, j, slab, slab, (x, y, c)).wait_recv()
                fwd = push(k, 3 + j, slab, slab, (x, y, 1 - c))
                fwd.start()
                started.append(fwd)
        for j, (cx, cy) in enumerate(chips):
            for k in range(n):
                slab = outs[k].at[2 * cx + cy, theirs(k)]
                push(k, 3 + j, slab, slab, (x, y, c)).wait_recv()
        for cp in started:
            cp.wait_send()

    return _call(
        body, name=name, in_specs=[ANY_SPEC] * n, out_specs=[ANY_SPEC] * n,
        out_shape=[jax.ShapeDtypeStruct(a.shape, a.dtype) for a in lands],
        input_output_aliases={k: k for k in range(n)},
        scratch_shapes=[pltpu.SemaphoreType.DMA((6 * n,)), pltpu.SemaphoreType.DMA((6 * n,))],
    )(*lands)


def _gather_devices(v):
    shape = v.shape

    def body(v_ref, out_ref, send_sems, recv_sems):
        x, y, c = _place()
        me = 4 * x + 2 * y + c
        out_ref[me] = v_ref[...]
        sends = []
        for k in range(1, N_DEV):
            to = (me + k) % N_DEV
            cp = pltpu.make_async_remote_copy(src_ref=v_ref, dst_ref=out_ref.at[me], send_sem=send_sems.at[k],
                                              recv_sem=recv_sems.at[k], device_id=(to // 4, (to // 2) % 2, to % 2),
                                              device_id_type=MESH)
            cp.start()
            sends.append(cp)
        for k in range(1, N_DEV):
            frm = (me + N_DEV - k) % N_DEV
            pltpu.make_async_remote_copy(src_ref=v_ref, dst_ref=out_ref.at[frm], send_sem=send_sems.at[k],
                                         recv_sem=recv_sems.at[k], device_id=(x, y, c), device_id_type=MESH).wait_recv()
        for cp in sends:
            cp.wait_send()

    vspec = pl.BlockSpec(memory_space=pltpu.VMEM)
    return _call(body, name="gather_devices", in_specs=[vspec], out_specs=vspec,
                 out_shape=jax.ShapeDtypeStruct((N_DEV,) + shape, F32),
                 scratch_shapes=[pltpu.SemaphoreType.DMA((N_DEV,)), pltpu.SemaphoreType.DMA((N_DEV,))])(v)


HBM_SPEC = pl.BlockSpec(memory_space=pltpu.HBM)
SEM_SPEC = pl.BlockSpec(memory_space=pltpu.SEMAPHORE)
SIDE_EFFECT = pltpu.SideEffectType.DATAFLOW_SIDE_EFFECTING


def _push_copies(src_refs, land_refs, send_sems, recv_sems, per_peer):
    x, y, c = _place()
    me = 2 * x + y
    copies = []
    for j, (cx, cy) in enumerate(_other_chips(x, y)):
        for k, (src, land) in enumerate(zip(src_refs, land_refs)):
            copies.append(pltpu.make_async_remote_copy(
                src_ref=src.at[2 * cx + cy] if per_peer else src, dst_ref=land.at[me], send_sem=send_sems.at[3 * k + j],
                recv_sem=recv_sems.at[3 * k + j], device_id=(cx, cy, c), device_id_type=MESH))
    return copies


def _push_start(srcs, lands, per_peer, after, name):
    n = len(srcs)

    def body(*refs):
        src_refs, land_refs = refs[:n], refs[n:2 * n]
        send_sems, recv_sems = refs[2 * n + 1], refs[2 * n + 2]
        token = refs[-1]
        for cp in _push_copies(src_refs, land_refs, send_sems, recv_sems, per_peer):
            cp.start()
        token[...] = jnp.zeros_like(token)

    bufs = [pltpu.with_memory_space_constraint(a, pltpu.HBM) for a in list(srcs) + list(lands)]
    res = _call(
        body, name=name,
        out_shape=[pltpu.SemaphoreType.DMA((3 * n,)), pltpu.SemaphoreType.DMA((3 * n,))]
        + [pltpu.HBM(a.shape, a.dtype) for a in bufs] + [jax.ShapeDtypeStruct((SUB, LANE), F32)],
        in_specs=[HBM_SPEC] * (2 * n) + [ANY_SPEC],
        out_specs=[SEM_SPEC, SEM_SPEC] + [HBM_SPEC] * (2 * n) + [pl.BlockSpec(memory_space=pltpu.VMEM)],
        input_output_aliases={i: 2 + i for i in range(2 * n)},
        compiler_params=pltpu.CompilerParams(has_side_effects=SIDE_EFFECT),
    )(*bufs, after)
    return res[0], res[1], list(res[2:2 + n]), list(res[2 + n:2 + 2 * n]), res[-1]


def _push_wait(send_sems, recv_sems, srcs, lands, per_peer, after, name):
    n = len(srcs)

    def body(*refs):
        src_refs, land_refs = refs[:n], refs[n:2 * n]
        send_sems, recv_sems = refs[2 * n], refs[2 * n + 1]
        for cp in _push_copies(src_refs, land_refs, send_sems, recv_sems, per_peer):
            cp.wait_send()
            cp.wait_recv()

    res = _call(
        body, name=name,
        out_shape=[pltpu.HBM(a.shape, a.dtype) for a in list(srcs) + list(lands)],
        in_specs=[HBM_SPEC] * (2 * n) + [SEM_SPEC, SEM_SPEC, ANY_SPEC],
        out_specs=[HBM_SPEC] * (2 * n),
        input_output_aliases={i: i for i in range(2 * n)},
        compiler_params=pltpu.CompilerParams(has_side_effects=SIDE_EFFECT),
    )(*srcs, *lands, send_sems, recv_sems, after)
    return list(res[n:])


def _sibling_swap(g):
    _, rows, w = g.shape
    half = rows // 2

    def body(g_ref, out_ref, send_sem, recv_sem):
        x, y, c = _place()
        cp = pltpu.make_async_remote_copy(src_ref=g_ref.at[:, pl.ds((1 - c) * half, half)], dst_ref=out_ref,
                                          send_sem=send_sem, recv_sem=recv_sem, device_id=(x, y, 1 - c), device_id_type=MESH)
        cp.start()
        cp.wait()

    return _call(body, name="rs_sibling_swap", in_specs=[ANY_SPEC], out_specs=ANY_SPEC,
                 out_shape=jax.ShapeDtypeStruct((N_CHIPS, half, w), F32),
                 scratch_shapes=[pltpu.SemaphoreType.DMA, pltpu.SemaphoreType.DMA])(g)


def _pair_add(g, got, place):
    _, rows, w = g.shape
    half = rows // 2
    tm = _row_tile(half, RS_TILE)
    nt = half // tm

    def body(p_ref, a_ref, b_ref, o_ref, own_ref):
        v = a_ref[...] + b_ref[...]
        o_ref[...] = v.astype(BF16)

        @pl.when(pl.program_id(1) == p_ref[1])
        def _():
            own_ref[...] = v

    return _call(
        body, name="rs_pair_add",
        grid_spec=pltpu.PrefetchScalarGridSpec(
            num_scalar_prefetch=1, grid=(nt, N_CHIPS),
            in_specs=[pl.BlockSpec((None, tm, w), lambda i, s, p: (s, p[0] * nt + i, 0)),
                      pl.BlockSpec((None, tm, w), lambda i, s, p: (s, i, 0))],
            out_specs=[pl.BlockSpec((None, tm, w), lambda i, s, p: (s, i, 0)),
                       pl.BlockSpec((tm, w), lambda i, s, p: (i, 0))]),
        out_shape=[jax.ShapeDtypeStruct((N_CHIPS, half, w), BF16), jax.ShapeDtypeStruct((half, w), F32)],
    )(place, g, got)


def _sum_chips(parts, own, place):
    _, half, w = parts.shape
    tm = _row_tile(half, RS_TILE)
    nt = half // tm

    def body(p_ref, parts_ref, own_ref, o_ref):
        me = p_ref[1]
        t = [jnp.where(me == q, own_ref[...], parts_ref[q].astype(F32)) for q in range(N_CHIPS)]
        o_ref[...] = (t[0] + t[1]) + (t[2] + t[3])

    return _call(
        body, name="rs_sum_chips",
        grid_spec=pltpu.PrefetchScalarGridSpec(
            num_scalar_prefetch=1, grid=(nt,),
            in_specs=[pl.BlockSpec((N_CHIPS, tm, w), lambda i, p: (0, i, 0)), pl.BlockSpec((tm, w), lambda i, p: (i, 0))],
            out_specs=pl.BlockSpec((tm, w), lambda i, p: (p[0] * nt + i, 0))),
        out_shape=jax.ShapeDtypeStruct((2 * half, w), F32),
    )(place, parts, own)


def _sibling_gather(red):
    rows, w = red.shape
    half = rows // 2

    def body(in_ref, out_ref, send_sem, recv_sem):
        x, y, c = _place()
        mine = out_ref.at[pl.ds(c * half, half)]
        cp = pltpu.make_async_remote_copy(src_ref=mine, dst_ref=mine, send_sem=send_sem, recv_sem=recv_sem,
                                          device_id=(x, y, 1 - c), device_id_type=MESH)
        cp.start()
        other = out_ref.at[pl.ds((1 - c) * half, half)]
        pltpu.make_async_remote_copy(src_ref=other, dst_ref=other, send_sem=send_sem, recv_sem=recv_sem,
                                     device_id=(x, y, c), device_id_type=MESH).wait_recv()
        cp.wait_send()

    return _call(body, name="rs_sibling_gather", in_specs=[ANY_SPEC], out_specs=ANY_SPEC,
                 out_shape=jax.ShapeDtypeStruct(red.shape, F32), input_output_aliases={0: 0},
                 scratch_shapes=[pltpu.SemaphoreType.DMA, pltpu.SemaphoreType.DMA])(red)


def _rs_begin(g, place, name):
    pair, own = _pair_add(g, _sibling_swap(g), place)
    send, recv, pair, parts, token = _push_start([pair], [jnp.zeros_like(pair)], True, own, name + "_start")
    return (send, recv, pair, parts, own), token


def _rs_end(state, place, after, name):
    send, recv, pair, parts, own = state
    (parts,) = _push_wait(send, recv, pair, parts, True, after, name + "_wait")
    return _sibling_gather(_sum_chips(parts, own, place))


WEIGHTS = ("c_ctx", "w_mod", "b_mod", "w_in", "w_out", "ln_g", "ln_b", "conv_w", "conv_b", "lru_wa", "lru_ba", "lru_wx",
           "lru_bx", "lru_lam", "pool_w", "pool_scale")
SMALL_GATHERED = ("conv_w", "lru_ba", "lru_bx", "lru_lam", "pool_scale")
SMALL_UPDATED = ("c_ctx", "b_mod", "ln_g", "ln_b", "conv_w", "conv_b", "lru_ba", "lru_bx", "lru_lam", "pool_scale")


def kernel(x, c, ctx, c_ctx, w_mod, b_mod, w_in, w_out, ln_g, ln_b, conv_w, conv_b, lru_wa, lru_ba, lru_wx, lru_bx, lru_lam, pool_w, pool_scale, loss_target, m_c_ctx, m_w_mod, m_b_mod, m_w_in, m_w_out, m_ln_g, m_ln_b, m_conv_w, m_conv_b, m_lru_wa, m_lru_ba, m_lru_wx, m_lru_bx, m_lru_lam, m_pool_w, m_pool_scale, v_c_ctx, v_w_mod, v_b_mod, v_w_in, v_w_out, v_ln_g, v_ln_b, v_conv_w, v_conv_b, v_lru_wa, v_lru_ba, v_lru_wx, v_lru_bx, v_lru_lam, v_pool_w, v_pool_scale):
    weights = dict(c_ctx=c_ctx, w_mod=w_mod, b_mod=b_mod, w_in=w_in, w_out=w_out, ln_g=ln_g, ln_b=ln_b, conv_w=conv_w,
                   conv_b=conv_b, lru_wa=lru_wa, lru_ba=lru_ba, lru_wx=lru_wx, lru_bx=lru_bx, lru_lam=lru_lam,
                   pool_w=pool_w, pool_scale=pool_scale)
    mom1 = dict(c_ctx=m_c_ctx, w_mod=m_w_mod, b_mod=m_b_mod, w_in=m_w_in, w_out=m_w_out, ln_g=m_ln_g, ln_b=m_ln_b,
                conv_w=m_conv_w, conv_b=m_conv_b, lru_wa=m_lru_wa, lru_ba=m_lru_ba, lru_wx=m_lru_wx, lru_bx=m_lru_bx,
                lru_lam=m_lru_lam, pool_w=m_pool_w, pool_scale=m_pool_scale)
    mom2 = dict(c_ctx=v_c_ctx, w_mod=v_w_mod, b_mod=v_b_mod, w_in=v_w_in, w_out=v_w_out, ln_g=v_ln_g, ln_b=v_ln_b,
                conv_w=v_conv_w, conv_b=v_conv_b, lru_wa=v_lru_wa, lru_ba=v_lru_ba, lru_wx=v_lru_wx, lru_bx=v_lru_bx,
                lru_lam=v_lru_lam, pool_w=v_pool_w, pool_scale=v_pool_scale)
    xs, cx, target = x[0], ctx[0], loss_target[0]
    s_len, d = xs.shape
    es = w_out.shape[1]
    e = es * N_CHIPS
    nb = e // LANE
    c3 = w_mod.shape[2]
    n4 = w_in.shape[2]
    pq, pg = pool_w.shape[2], pool_w.shape[3]
    ng = len(POOL_WINDOWS)
    width = n4
    assert width == d and 2 * pg == width and 2 * nb * LANE == N_CHIPS * width and d % (2 * N_CHIPS) == 0
    px, py, pc = _place()
    place = jnp.stack([pc, 2 * px + py]).astype(jnp.int32)
    cctx2 = c_ctx[None, :]

    eq = e // N_CHIPS
    small_rows = [(conv_w[0], 0), (lru_ba[0], CONV_TAPS), (lru_bx[0], CONV_TAPS + 2), (lru_lam[0], CONV_TAPS + 4),
                  (pool_scale, CONV_TAPS + 6)]
    small = _rows_kernel([(a, r, 0) for a, r in small_rows], 2 * SUB, eq, "pack_small_weights")
    wm_g, win0, sg = _gather_chips([w_mod.astype(BF16).reshape(DEPTH * d, c3), w_in[0].astype(BF16), small], "gather_weights0")
    later = [w_out[0].astype(BF16), w_in[1].astype(BF16), w_out[1].astype(BF16), pool_w.astype(BF16).reshape(ng * pq, pg)]
    w_send, w_recv, later, later_lands, w_token = _push_start(
        later, [jnp.broadcast_to(a[None], (N_CHIPS,) + a.shape) for a in later], False, sg, "gather_weights1_start")
    wm_all = wm_g.reshape(N_CHIPS, DEPTH, d, c3)
    full = {n: jnp.swapaxes(sg[:, r:r + a.shape[0]], 0, 1).reshape(a.shape[0], e)
            for n, (a, r) in zip(SMALL_GATHERED, small_rows)}
    wa_b, wx_b = lru_wa[0].astype(BF16), lru_wx[0].astype(BF16)
    lru_args = (full["conv_w"], conv_b, wa_b, wx_b, full["lru_ba"], full["lru_bx"], full["lru_lam"])
    scale_f = full["pool_scale"]

    mod = _mod_fwd(c + w_token[0:1, 0:1], cctx2, wm_all, b_mod[:, None, :])

    def mod_parts(l, row):
        v = mod[l, row]
        return v[None, :d], 1.0 + v[None, d:2 * d], v[None, 2 * d:]

    sh0, sc0, gt0 = mod_parts(0, 0)
    shc, scc, _ = mod_parts(0, 1)
    sh1, sc1, gt1 = mod_parts(1, 0)
    lg = [ln_g[l][None, :] for l in range(DEPTH)]
    lb = [ln_b[l][None, :] for l in range(DEPTH)]

    uu0, ug0 = _inproj_fwd(xs, sc0, sh0, win0, "inproj_fwd0")
    uc0 = _inproj_fwd(cx, scc, shc, win0[:2], "inproj_fwd_ctx")
    y0 = _rglru_fwd(uu0, uc0, *lru_args)
    wout0_g, win1, wout1_g, wp_g = _push_wait(w_send, w_recv, later, later_lands, False, y0, "gather_weights1_wait")
    win = [win0, win1]
    wout = [wout0_g.reshape(e, d), wout1_g.reshape(e, d)]
    wp = wp_g.reshape(N_CHIPS, ng, pq, pg)
    br0, x1 = _outproj_fwd(y0, ug0, xs, gt0, wout[0], lg[0], lb[0], None, "outproj_fwd0")
    uu1, ug1 = _inproj_fwd(x1, sc1, sh1, win[1], "inproj_fwd1")
    d1 = _pool_map(uu1, nb, False, True, "pool_fwd")
    y1 = _pool_mm_fwd(d1, wp, scale_f)
    br1, dxo, loss_part = _outproj_fwd(y1, ug1, x1, gt1, wout[1], lg[1], lb[1], target, "outproj_fwd1")
    loss = lax.psum(jnp.sum(loss_part) * (0.5 / d), ("x", "y", "c"))

    row_wout = d
    row_tail = d + es
    wq = 2 * (nb // N_CHIPS) * LANE * LANE // width
    whole = lambda r: (r + 2 * RS_TILE - 1) // (2 * RS_TILE) * (2 * RS_TILE)
    rows1 = whole(row_tail + pg // 2)
    rows0 = whole(row_tail + 2 * wq)
    gbuf1 = jnp.zeros((N_CHIPS, rows1, width), F32)
    gbuf0 = jnp.zeros((N_CHIPS, rows0, width), F32)

    dy1, dg1, dxres1, dbr1, dlg1, dlb1, dgt1 = _outproj_bwd(dxo, x1, br1, y1, ug1, gt1, lg[1], wout[1], "outproj_bwd1")
    gbuf1 = _outproj_bwd_w(y1, ug1, dbr1, gbuf1, row_wout, "outproj_bwd_w1")
    dd1, gbuf1, dscale = _pool_mm_bwd(dy1, d1, wp, scale_f, gbuf1, row_tail)
    du1 = _pool_map(dd1, nb, True, False, "pool_bwd")
    dx1, dsc1, dsh1 = _inproj_bwd_x([du1, dg1], x1, dxres1, sc1, win[1], "inproj_bwd_x1")
    gbuf1 = _inproj_bwd_w(x1, sc1, sh1, [du1, dg1], None, gbuf1, "inproj_bwd_w1")
    rs1, token1 = _rs_begin(gbuf1, place, "rs_exchange1")

    dy0, dg0, dxres0, dbr0, dlg0, dlb0, dgt0 = _outproj_bwd(dx1, xs, br0, y0, ug0, gt0 + token1[0:1, 0:1], lg[0], wout[0],
                                                            "outproj_bwd0")
    gbuf0 = _outproj_bwd_w(y0, ug0, dbr0, gbuf0, row_wout, "outproj_bwd_w0")
    du0, duc, dconv_w, dconv_b, dwa, dwx, dba, dbx, dlam = _rglru_bwd(uu0, uc0, dy0, *lru_args)
    dwin0c = _inproj_bwd_w(cx, scc, shc, [duc, jnp.zeros_like(duc)], None, None, "inproj_bwd_w_ctx")
    gbuf0 = _inproj_bwd_w(xs, sc0, sh0, [du0, dg0], dwin0c, gbuf0, "inproj_bwd_w0")

    def quarter(dw):
        t = dw.reshape(2, N_CHIPS, nb // N_CHIPS, LANE, LANE)
        return jnp.transpose(t, (1, 3, 0, 2, 4)).reshape(N_CHIPS, LANE, 2 * (nb // N_CHIPS) * LANE).reshape(N_CHIPS, wq, width)

    tail0 = jnp.concatenate([quarter(dwa), quarter(dwx)], axis=1)
    gbuf0 = lax.dynamic_update_slice(gbuf0, tail0, (0, row_tail, 0))
    red1 = _rs_end(rs1, place, gbuf0, "rs_exchange1")
    rs0, token0 = _rs_begin(gbuf0, place, "rs_exchange0")
    grad_x, dsc0, dsh0 = _inproj_bwd_x([du0, dg0], xs, dxres0, sc0 + token0[0:1, 0:1], win[0], "inproj_bwd_x0")
    dscc, dshc = _inproj_bwd_x([duc], cx, None, scc, win[0][:2], "inproj_bwd_x_ctx")

    k0 = VEC_KINDS
    vec = _rows_kernel(
        [(c, 0, 0), (dsh0, 1, 0), (dsc0, 1, d), (dgt0, 1, 2 * d), (dshc, 2, 0), (dscc, 2, d),
         (dsh1, 3, 0), (dsc1, 3, d), (dgt1, 3, 2 * d),
         (dconv_b, k0, 0), (dlg0, k0, e), (dscale, k0 + 1, 0), (dlg1, k0 + 1, e), (dlb0, k0 + 2, 0), (dlb1, k0 + 2, d),
         (dconv_w, k0 + 3, 0), (dba, k0 + 7, 0), (dbx, k0 + 9, 0), (dlam, k0 + 11, 0)], VEC_ROWS, 3 * d, "pack_vec")
    gt_all = jnp.swapaxes(_gather_devices(vec), 0, 1)
    g_wmod = _mod_bwd_shard(gt_all, cctx2, place, c3)
    g_bmod, g_cctx, g_small = _mod_bwd_rep(gt_all, cctx2, wm_all)
    red0 = _rs_end(rs0, place, g_bmod, "rs_exchange0")
    (rep,) = _gather_chips([red0[row_tail:row_tail + 2 * wq]], "gather_replicated")

    tmw = _row_tile(d, 256)
    red_src = lambda red, r0, tm: (red, (tm, width), lambda n, i: (r0 // tm + i, 0))
    by_layer = lambda n, gs: jnp.where(n == 0, gs[0], gs[1])
    outs = {}
    outs["w_in"] = _adamw_param(w_in, m_w_in, v_w_in, [red_src(red0, 0, tmw), red_src(red1, 0, tmw)], by_layer, tmw, "adamw_w_in")
    outs["w_out"] = _adamw_param(w_out, m_w_out, v_w_out, [red_src(red0, row_wout, tmw), red_src(red1, row_wout, tmw)],
                                 by_layer, tmw, "adamw_w_out")
    outs["w_mod"] = _adamw_param(w_mod, m_w_mod, v_w_mod, [(g_wmod, (None, tmw, c3), lambda n, i: (n, i, 0))],
                                 lambda n, gs: gs[0], tmw, "adamw_w_mod")
    pw = [a.reshape(ng, pq, pg) for a in (pool_w, m_pool_w, v_pool_w)]
    outs["pool_w"] = [o.reshape(pool_w.shape) for o in _adamw_param(
        *pw, [(red1, (pq, pg), lambda n, i: (row_tail // pq + n // 2, n % 2))], lambda n, gs: gs[0], pq, "adamw_pool_w")]
    bq = nb // N_CHIPS
    rep_src = lambda r0: (rep, (None, LANE, bq * LANE), lambda n, i: (n % N_CHIPS, r0 // LANE, n // N_CHIPS))
    stack = lambda n, gs: jnp.concatenate([gs[0][:, k * LANE:(k + 1) * LANE] for k in range(bq)], axis=0)
    for name, r0, trio in (("lru_wa", 0, (lru_wa, m_lru_wa, v_lru_wa)), ("lru_wx", wq, (lru_wx, m_lru_wx, v_lru_wx))):
        blocks = [a.reshape(2 * N_CHIPS, bq * LANE, LANE) for a in trio]
        outs[name] = [o.reshape(lru_wa.shape) for o in _adamw_param(*blocks, [rep_src(r0)], stack, bq * LANE, "adamw_" + name)]

    g_small = dict(g_small, c_ctx=g_cctx, b_mod=g_bmod)
    for n in SMALL_GATHERED:
        g_small[n] = lax.dynamic_slice_in_dim(g_small[n], place[1] * eq, eq, axis=1)
    as2d = lambda a: a.reshape(-1, a.shape[-1])
    quads = [(as2d(weights[n]), g_small[n], as2d(mom1[n]), as2d(mom2[n])) for n in SMALL_UPDATED]
    for n, (q, res) in zip(SMALL_UPDATED, zip(quads, _adamw_small(quads))):
        outs[n] = [a.reshape(weights[n].shape) for a in (q[1],) + res]

    result = [loss, grad_x[None]]
    for j in range(4):
        result += [outs[n][j] for n in WEIGHTS]
    return tuple(result)
```

```python
import jax
import jax.numpy as jnp
from jax import lax
from jax.experimental import pallas as pl
from jax.experimental.pallas import tpu as pltpu

F32 = jnp.float32
BF16 = jnp.bfloat16
LANE = 128
SUB = 8
GRID_W = 64
POOL_WINDOWS = (2, 4, 8, 16)
LRU_C = 8.0
DEPTH = 2
ALPHA = float((2 * DEPTH) ** 0.25)
LN_EPS = 1e-5
ADAM_LR, ADAM_B1, ADAM_B2, ADAM_EPS, ADAM_WD, ADAM_STEP = 0.001, 0.9, 0.999, 1e-08, 0.01, 10
N_CHIPS = 4
N_DEV = 8
MESH = pl.DeviceIdType.MESH
ROW_TILE = 512
GATE_TILE = 2048
GATE_UNROLL = 1
CONV_TAPS = 4
CONV_LEFT = 2
PAD = 8
SCAN_UNROLL = 32
RS_TILE = 448
LN_ROWS = 128
POOL_CPAD = 16
VEC_KINDS = 4


def _call(body, **kw):
    return pl.pallas_call(body, **kw)


def _dot(a, b):
    return jnp.dot(a, b, preferred_element_type=F32)


def _dot_nt(a, b):
    return lax.dot_general(a, b, (((1,), (1,)), ((), ())), preferred_element_type=F32)


def _dot_tn(a, b):
    return lax.dot_general(a, b, (((0,), (0,)), ((), ())), preferred_element_type=F32)


def _sigmoid(v):
    return 0.5 * (jnp.tanh(0.5 * v) + 1.0)


def _silu(v):
    return v * _sigmoid(v)


def _dsilu(v):
    s = _sigmoid(v)
    return s * (1.0 + v * (1.0 - s))


def _log_sigmoid(v):
    z = jnp.exp(-jnp.abs(v))
    return jnp.minimum(v, 0.0) - jnp.where(z < 1e-4, z * (1.0 - 0.5 * z), jnp.log(1.0 + z))


def _one_minus_sq(la, a):
    return -jnp.tanh(la) * (a * a + 1.0)


def _cat(ref, n):
    return jnp.concatenate([ref[k] for k in range(n)], axis=1)


def _put_chunks(ref, val, n, base=0):
    for k in range(n):
        ref[base + k] = val[:, k * LANE:(k + 1) * LANE].astype(ref.dtype)


def _row_tile(rows, want):
    t = min(rows, want)
    assert rows % t == 0
    return t


ANY_SPEC = pl.BlockSpec(memory_space=pl.ANY)


def _mod_fwd(cvec, cctx, wm, bm):
    ns, nl, d, c3 = wm.shape

    def body(c_ref, cx_ref, w_ref, b_ref, o_ref):
        cc = jnp.concatenate([c_ref[...], cx_ref[...], jnp.zeros((SUB - 2, d), F32)], axis=0)
        o_ref[...] = _dot(_silu(cc).astype(BF16), w_ref[...]) + b_ref[...]

    return _call(
        body, name="mod_fwd", grid=(nl, ns),
        in_specs=[pl.BlockSpec((1, d), lambda l, s: (0, 0)),
                  pl.BlockSpec((1, d), lambda l, s: (0, 0)),
                  pl.BlockSpec((None, None, d, c3), lambda l, s: (s, l, 0, 0)),
                  pl.BlockSpec((None, 1, c3), lambda l, s: (l, 0, s))],
        out_specs=pl.BlockSpec((None, 8, c3), lambda l, s: (l, 0, s)),
        out_shape=jax.ShapeDtypeStruct((nl, 8, ns * c3), F32),
    )(cvec, cctx, wm, bm)


def _rows_kernel(parts, rows, cols, name):
    def body(*refs):
        o_ref = refs[-1]
        o_ref[...] = jnp.zeros_like(o_ref)
        for ref, (a, r0, c0) in zip(refs[:-1], parts):
            for k in range(a.shape[0]):
                o_ref[r0 + k:r0 + k + 1, c0:c0 + a.shape[1]] = ref[k:k + 1, :]

    return _call(body, name=name, grid=(1,),
                 in_specs=[pl.BlockSpec(a.shape, lambda i: (0, 0)) for a, _, _ in parts],
                 out_specs=pl.BlockSpec((rows, cols), lambda i: (0, 0)),
                 out_shape=jax.ShapeDtypeStruct((rows, cols), F32))(*[a for a, _, _ in parts])


def _mod_bwd_shard(gt, cctx, place, c3):
    d = cctx.shape[1]

    def body(p_ref, cs_ref, dm_ref, dmx_ref, cx_ref, o_ref):
        l = pl.program_id(0)
        lhs = jnp.concatenate([_silu(cs_ref[...]), _silu(cx_ref[...]), jnp.zeros((7, d), F32)], axis=0).astype(BF16)
        dmx = jnp.where(l == 0, jnp.sum(dmx_ref[...], axis=0, keepdims=True), 0.0)
        rhs = jnp.concatenate([dm_ref[...], dmx, jnp.zeros((7, c3), F32)], axis=0).astype(BF16)
        o_ref[...] = _dot_tn(lhs, rhs)

    return _call(
        body, name="mod_bwd_shard",
        grid_spec=pltpu.PrefetchScalarGridSpec(
            num_scalar_prefetch=1, grid=(DEPTH,),
            in_specs=[pl.BlockSpec((None, N_DEV, d), lambda l, p: (0, 0, 0)),
                      pl.BlockSpec((None, N_DEV, c3), lambda l, p: (1 + 2 * l, 0, p[1])),
                      pl.BlockSpec((None, N_DEV, c3), lambda l, p: (2, 0, p[1])),
                      pl.BlockSpec((1, d), lambda l, p: (0, 0))],
            out_specs=pl.BlockSpec((None, d, c3), lambda l, p: (l, 0, 0))),
        out_shape=jax.ShapeDtypeStruct((DEPTH, d, c3), F32),
    )(place, gt, gt, gt, cctx)


def _small_layout(d, e):
    k = VEC_KINDS
    return {
        "conv_b": ((1, e), [(0, k, 0)]),
        "ln_g": ((2, d), [(0, k, e), (1, k + 1, e)]),
        "pool_scale": ((1, e), [(0, k + 1, 0)]),
        "ln_b": ((2, d), [(0, k + 2, 0), (1, k + 2, d)]),
        "conv_w": ((CONV_TAPS, e), [(t, k + 3 + t, 0) for t in range(CONV_TAPS)]),
        "lru_ba": ((2, e), [(j, k + 7 + j, 0) for j in range(2)]),
        "lru_bx": ((2, e), [(j, k + 9 + j, 0) for j in range(2)]),
        "lru_lam": ((2, e), [(j, k + 11 + j, 0) for j in range(2)]),
    }


VEC_ROWS = 24


def _mod_bwd_rep(gt, cctx, wm):
    ns, _, d, c3 = wm.shape
    layout = _small_layout(d, ns * c3 - d)
    names = list(layout)

    def body(g_ref, cx_ref, w_ref, db_ref, dc_ref, *small_refs):
        dm0 = jnp.sum(g_ref[1], axis=0, keepdims=True)
        dmx = jnp.sum(g_ref[2], axis=0, keepdims=True)
        dm1 = jnp.sum(g_ref[3], axis=0, keepdims=True)
        db_ref[0:1, :] = dm0 + dmx
        db_ref[1:2, :] = dm1
        dmxb = jnp.broadcast_to(dmx, (SUB, ns * c3)).astype(BF16)
        acc = jnp.zeros((SUB, d), F32)
        for s in range(ns):
            acc = acc + _dot_nt(dmxb[:, s * c3:(s + 1) * c3], w_ref[s])
        dc_ref[...] = acc[0:1, :] * _dsilu(cx_ref[...])
        for ref, name in zip(small_refs, names):
            shape, places = layout[name]
            for arr_row, vec_row, col0 in places:
                total = jnp.sum(g_ref[vec_row], axis=0, keepdims=True)
                ref[arr_row:arr_row + 1, :] = total[:, col0:col0 + shape[1]]

    outs = _call(
        body, name="mod_bwd_rep", grid=(1,),
        in_specs=[pl.BlockSpec(gt.shape, lambda i: (0, 0, 0)),
                  pl.BlockSpec((1, d), lambda i: (0, 0)),
                  pl.BlockSpec((ns, None, d, c3), lambda i: (0, 0, 0, 0))],
        out_specs=[pl.BlockSpec((DEPTH, ns * c3), lambda i: (0, 0)), pl.BlockSpec((1, d), lambda i: (0, 0))]
        + [pl.BlockSpec(layout[n][0], lambda i: (0, 0)) for n in names],
        out_shape=[jax.ShapeDtypeStruct((DEPTH, ns * c3), F32), jax.ShapeDtypeStruct((1, d), F32)]
        + [jax.ShapeDtypeStruct(layout[n][0], F32) for n in names],
    )(gt, cctx, wm)
    return outs[0], outs[1], dict(zip(names, outs[2:]))


def _inproj_fwd(xin, sc1, sh, w, name):
    rows, d = xin.shape
    ns, _, n4 = w.shape
    cpb = n4 // LANE
    tm = _row_tile(rows, 256)
    assert ns in (2, 4)

    def body(x_ref, sc_ref, sh_ref, w_ref, *o_refs):
        h = (x_ref[...] * sc_ref[...] + sh_ref[...]).astype(BF16)
        for s in range(ns):
            _put_chunks(o_refs[s // 2], _dot(h, w_ref[s]), cpb, base=(s % 2) * cpb)

    spec = pl.BlockSpec((2 * cpb, tm, LANE), lambda i: (0, i, 0))
    dtypes = (F32, BF16)[:ns // 2]
    res = _call(
        body, name=name, grid=(rows // tm,),
        in_specs=[pl.BlockSpec((tm, d), lambda i: (i, 0)),
                  pl.BlockSpec((1, d), lambda i: (0, 0)),
                  pl.BlockSpec((1, d), lambda i: (0, 0)),
                  pl.BlockSpec((ns, d, n4), lambda i: (0, 0, 0))],
        out_specs=[spec] * len(dtypes),
        out_shape=[jax.ShapeDtypeStruct((2 * cpb, rows, LANE), t) for t in dtypes],
    )(xin, sc1, sh, w)
    return res[0] if ns == 2 else tuple(res)


def _inproj_bwd_x(dparts, xin, dxres, sc1, w, name):
    rows, d = xin.shape
    npart = len(dparts)
    e = dparts[0].shape[1]
    ns, _, n4 = w.shape
    per = e // n4
    assert per * npart == ns
    tm = _row_tile(rows, 256)
    has_res = dxres is not None

    def body(*refs):
        dp = refs[:npart]
        x_ref, sc_ref, w_ref = refs[npart:npart + 3]
        rest = refs[npart + 3:]
        if has_res:
            res_ref, dx_ref, dsc_ref, dsh_ref = rest
        else:
            dsc_ref, dsh_ref = rest
        i = pl.program_id(0)
        dh = jnp.zeros((tm, d), F32)
        for p in range(npart):
            v = dp[p][...]
            for q in range(per):
                dh = dh + _dot_nt(v[:, q * n4:(q + 1) * n4], w_ref[p * per + q])

        @pl.when(i == 0)
        def _():
            dsc_ref[...] = jnp.zeros_like(dsc_ref)
            dsh_ref[...] = jnp.zeros_like(dsh_ref)

        dsc_ref[...] += jnp.sum(dh * x_ref[...], axis=0, keepdims=True)
        dsh_ref[...] += jnp.sum(dh, axis=0, keepdims=True)
        if has_res:
            dx_ref[...] = res_ref[...] + dh * sc_ref[...]

    row_spec = pl.BlockSpec((tm, d), lambda i: (i, 0))
    vec_spec = pl.BlockSpec((1, d), lambda i: (0, 0))
    in_specs = [pl.BlockSpec((tm, e), lambda i: (i, 0))] * npart + [row_spec, vec_spec,
                                                                     pl.BlockSpec((ns, d, n4), lambda i: (0, 0, 0))]
    args = list(dparts) + [xin, sc1, w]
    out_specs, out_shape = [vec_spec, vec_spec], [jax.ShapeDtypeStruct((1, d), F32)] * 2
    if has_res:
        in_specs.append(row_spec)
        args.append(dxres)
        out_specs = [row_spec] + out_specs
        out_shape = [jax.ShapeDtypeStruct((rows, d), F32)] + out_shape
    return _call(body, name=name, grid=(rows // tm,), in_specs=in_specs, out_specs=out_specs, out_shape=out_shape)(*args)


def _inproj_bwd_w(xin, sc1, sh, dparts, init, gbuf, name):
    rows, d = xin.shape
    npart = len(dparts)
    e = dparts[0].shape[1]
    n4 = e // 2
    ns = 2 * npart
    tm = _row_tile(rows, 1024)
    nt = rows // tm
    has_init = init is not None
    into = gbuf is not None
    assert not into or (ns == N_CHIPS and gbuf.shape[2] == n4)

    def body(*refs):
        x_ref, sc_ref, sh_ref = refs[:3]
        dp = refs[3:3 + npart]
        init_ref = refs[3 + npart] if has_init else None
        o_ref = refs[-1]
        s, i = pl.program_id(0), pl.program_id(1)
        h = (x_ref[...] * sc_ref[...] + sh_ref[...]).astype(BF16)

        @pl.when(i == 0)
        def _():
            o_ref[...] = init_ref[...] if has_init else jnp.zeros_like(o_ref)

        for p in range(npart):
            @pl.when(s // 2 == p)
            def _(p=p):
                o_ref[...] += _dot_tn(h, dp[p][...])

    in_specs = [pl.BlockSpec((tm, d), lambda s, i: (i, 0)),
                pl.BlockSpec((1, d), lambda s, i: (0, 0)),
                pl.BlockSpec((1, d), lambda s, i: (0, 0))]
    in_specs += [pl.BlockSpec((tm, n4), lambda s, i: (i, s % 2))] * npart
    args = [xin, sc1, sh] + list(dparts)
    o_spec = pl.BlockSpec((None, d, n4), lambda s, i: (s, 0, 0))
    if has_init:
        in_specs.append(o_spec)
        args.append(init)
    extra = {}
    if into:
        in_specs.append(ANY_SPEC)
        args.append(gbuf)
        extra = dict(input_output_aliases={len(args) - 1: 0})
    out_shape = jax.ShapeDtypeStruct(gbuf.shape if into else (ns, d, n4), F32)
    return _call(body, name=name, grid=(ns, nt), in_specs=in_specs, out_specs=o_spec, out_shape=out_shape, **extra)(*args)


def _gated(y_ref, g_ref, nch):
    return jnp.concatenate([(y_ref[k].astype(F32) * _silu(g_ref[k].astype(F32))).astype(BF16) for k in range(nch)], axis=1)


def _ln_stats(r):
    mu = jnp.mean(r, axis=-1, keepdims=True)
    var = jnp.mean(jnp.square(r - mu), axis=-1, keepdims=True)
    rstd = lax.rsqrt(var + LN_EPS)
    return (r - mu) * rstd, rstd


def _outproj_fwd(y, ug, xin, gt, wout, lg, lb, target, name):
    nch, rows, _ = y.shape
    e, d = wout.shape
    tm = _row_tile(rows, 512)
    with_loss = target is not None

    def body(*refs):
        y_ref, g_ref, x_ref, gt_ref, w_ref, lg_ref, lb_ref = refs[:7]
        if with_loss:
            t_ref, br_ref, dxo_ref, loss_ref = refs[7:]
        else:
            br_ref, xo_ref = refs[7:]
        z = _gated(y_ref, g_ref, nch)
        br_ref[...] = _dot(z, w_ref[...])
        if with_loss:
            @pl.when(pl.program_id(0) == 0)
            def _():
                loss_ref[...] = jnp.zeros_like(loss_ref)

        def norm(j, c):
            rows = pl.ds(pl.multiple_of(j * LN_ROWS, LN_ROWS), LN_ROWS)
            xhat, _ = _ln_stats(ALPHA * x_ref[rows, :] + gt_ref[...] * br_ref[rows, :])
            xo = xhat * lg_ref[...] + lb_ref[...]
            if with_loss:
                err = xo - t_ref[rows, :]
                dxo_ref[rows, :] = err * (1.0 / d)
                col = jnp.sum(err * err, axis=0, keepdims=True)
                loss_ref[...] += sum(col[:, k * LANE:(k + 1) * LANE] for k in range(d // LANE))
            else:
                xo_ref[rows, :] = xo
            return c

        lax.fori_loop(0, tm // LN_ROWS, norm, 0)

    chunk_spec = pl.BlockSpec((nch, tm, LANE), lambda i: (0, i, 0))
    g_spec = chunk_spec
    row_spec = pl.BlockSpec((tm, d), lambda i: (i, 0))
    vec_spec = pl.BlockSpec((1, d), lambda i: (0, 0))
    in_specs = [chunk_spec, g_spec, row_spec, vec_spec, pl.BlockSpec((e, d), lambda i: (0, 0)), vec_spec, vec_spec]
    args = [y, ug, xin, gt, wout, lg, lb]
    out_specs = [row_spec, row_spec]
    out_shape = [jax.ShapeDtypeStruct((rows, d), F32)] * 2
    if with_loss:
        in_specs.append(row_spec)
        args.append(target)
        out_specs.append(pl.BlockSpec((1, LANE), lambda i: (0, 0)))
        out_shape.append(jax.ShapeDtypeStruct((1, LANE), F32))
    return _call(body, name=name, grid=(rows // tm,), in_specs=in_specs, out_specs=out_specs, out_shape=out_shape)(*args)


def _outproj_bwd(dxo, xin, br, y, ug, gt, lg, wout, name):
    nch, rows, _ = y.shape
    e, d = wout.shape
    tm = _row_tile(rows, 256)

    def body(dxo_ref, x_ref, br_ref, y_ref, g_ref, gt_ref, lg_ref, w_ref,
             dy_ref, dg_ref, dxres_ref, dbr_ref, dlg_ref, dlb_ref, dgt_ref):
        @pl.when(pl.program_id(0) == 0)
        def _():
            dlg_ref[...] = jnp.zeros_like(dlg_ref)
            dlb_ref[...] = jnp.zeros_like(dlb_ref)
            dgt_ref[...] = jnp.zeros_like(dgt_ref)

        def norm_bwd(j, c):
            rows = pl.ds(pl.multiple_of(j * LN_ROWS, LN_ROWS), LN_ROWS)
            dxo_v = dxo_ref[rows, :]
            brv = br_ref[rows, :]
            xhat, rstd = _ln_stats(ALPHA * x_ref[rows, :] + gt_ref[...] * brv)
            dxh = dxo_v * lg_ref[...]
            dr = rstd * (dxh - jnp.mean(dxh, axis=-1, keepdims=True) - xhat * jnp.mean(dxh * xhat, axis=-1, keepdims=True))
            dlg_ref[...] += jnp.sum(dxo_v * xhat, axis=0, keepdims=True)
            dlb_ref[...] += jnp.sum(dxo_v, axis=0, keepdims=True)
            dgt_ref[...] += jnp.sum(dr * brv, axis=0, keepdims=True)
            dxres_ref[rows, :] = ALPHA * dr
            dbr_ref[rows, :] = (gt_ref[...] * dr).astype(BF16)
            return c

        lax.fori_loop(0, tm // LN_ROWS, norm_bwd, 0)
        dz = _dot_nt(dbr_ref[...], w_ref[...])
        for k in range(nch):
            dzk = dz[:, k * LANE:(k + 1) * LANE]
            gk = g_ref[k].astype(F32)
            dy_ref[k] = dzk * _silu(gk)
            dg_ref[:, k * LANE:(k + 1) * LANE] = (dzk * y_ref[k].astype(F32) * _dsilu(gk)).astype(BF16)

    chunk_spec = pl.BlockSpec((nch, tm, LANE), lambda i: (0, i, 0))
    g_spec = chunk_spec
    row_spec = pl.BlockSpec((tm, d), lambda i: (i, 0))
    vec_spec = pl.BlockSpec((1, d), lambda i: (0, 0))
    return _call(
        body, name=name, grid=(rows // tm,),
        in_specs=[row_spec, row_spec, row_spec, chunk_spec, g_spec, vec_spec, vec_spec, pl.BlockSpec((e, d), lambda i: (0, 0))],
        out_specs=[chunk_spec, pl.BlockSpec((tm, e), lambda i: (i, 0)), row_spec, row_spec, vec_spec, vec_spec, vec_spec],
        out_shape=[jax.ShapeDtypeStruct((nch, rows, LANE), F32), jax.ShapeDtypeStruct((rows, e), BF16),
                   jax.ShapeDtypeStruct((rows, d), F32), jax.ShapeDtypeStruct((rows, d), BF16)]
        + [jax.ShapeDtypeStruct((1, d), F32)] * 3,
    )(dxo, xin, br, y, ug, gt, lg, wout)


def _outproj_bwd_w(y, ug, dbr, gbuf, row0, name):
    nch, rows, _ = y.shape
    d = dbr.shape[1]
    e = nch * LANE
    es = e // N_CHIPS
    tm = _row_tile(rows, 512)
    assert gbuf.shape[2] == d and row0 % es == 0

    def body(y_ref, g_ref, dbr_ref, buf_ref, o_ref):
        @pl.when(pl.program_id(0) == 0)
        def _():
            o_ref[...] = jnp.zeros_like(o_ref)

        z = _gated(y_ref, g_ref, nch)
        o_ref[...] += _dot_tn(z, dbr_ref[...]).reshape(N_CHIPS, es, d)

    return _call(
        body, name=name, grid=(rows // tm,),
        in_specs=[pl.BlockSpec((nch, tm, LANE), lambda i: (0, i, 0)),
                  pl.BlockSpec((nch, tm, LANE), lambda i: (0, i, 0)),
                  pl.BlockSpec((tm, d), lambda i: (i, 0)),
                  ANY_SPEC],
        out_specs=pl.BlockSpec((N_CHIPS, es, d), lambda i: (0, row0 // es, 0)),
        out_shape=jax.ShapeDtypeStruct(gbuf.shape, F32),
        input_output_aliases={3: 0},
    )(y, ug, dbr, gbuf)


def _scan(a_ref, b_ref, h_ref, *, length, init, reverse, a_shift, store):
    nblk = length // SUB
    unroll = min(SCAN_UNROLL, nblk)
    assert nblk % unroll == 0
    row = lax.broadcasted_iota(jnp.int32, (SUB, LANE), 0)
    last = 0 if reverse else SUB - 1
    edges = [(row >= SUB - k) if reverse else (row < k) for k in (1, 2, 4)]

    def local_scan(a, b):
        for k, edge in zip((1, 2, 4), edges):
            sh = (SUB - k) if reverse else k
            b = b + a * jnp.where(edge, 0.0, pltpu.roll(b, sh, 0))
            a = a * jnp.where(edge, 1.0, pltpu.roll(a, sh, 0))
        return a, b

    def step(i, carry):
        base = pl.multiple_of(((nblk // unroll - 1 - i) if reverse else i) * (unroll * SUB), unroll * SUB)
        order = range(unroll - 1, -1, -1) if reverse else range(unroll)
        loaded = [(a_ref[pl.ds(PAD + base + j * SUB + a_shift, SUB), :], b_ref[pl.ds(PAD + base + j * SUB, SUB), :])
                  for j in order]
        scanned = [local_scan(a, b) for a, b in loaded]
        for j, (a, b) in zip(order, scanned):
            if store:
                h_ref[pl.ds(PAD + base + j * SUB, SUB), :] = b + a * carry
            a_l = jnp.broadcast_to(a[last:last + 1, :], (SUB, LANE))
            b_l = jnp.broadcast_to(b[last:last + 1, :], (SUB, LANE))
            carry = b_l + a_l * carry
        return carry

    carry = lax.fori_loop(0, nblk // unroll, step, jnp.broadcast_to(init, (SUB, LANE)))
    return carry[0:1, :]


def _conv_fwd(src_ref, upad, u_ref, cw, cb, length):
    zeros = jnp.zeros((PAD, LANE), F32)
    upad[pl.ds(0, PAD), :] = zeros
    upad[pl.ds(PAD + length, PAD), :] = zeros
    rt = _row_tile(length, ROW_TILE)

    def copy(i, c):
        t0 = pl.multiple_of(i * rt, rt)
        upad[pl.ds(PAD + t0, rt), :] = src_ref[pl.ds(t0, rt), :]
        return c

    lax.fori_loop(0, length // rt, copy, 0)

    def tile(i, c):
        t0 = pl.multiple_of(i * rt, rt)
        acc = jnp.zeros((rt, LANE), F32)
        for k in range(CONV_TAPS):
            acc = acc + upad[pl.ds(t0 + PAD - CONV_LEFT + k, rt), :] * cw[k:k + 1, :]
        u_ref[pl.ds(t0, rt), :] = acc + cb
        return c

    lax.fori_loop(0, length // rt, tile, 0)


def _gates_fwd(u_ref, a_ref, b_ref, wa, wx, ba, bx, ls, length, keep=None):
    rt = _row_tile(length, GATE_TILE)

    def tile(i, c):
        t0 = pl.multiple_of(i * rt, rt)
        ut = u_ref[pl.ds(t0, rt), :]
        ub = ut.astype(BF16)
        r = _sigmoid(_dot(ub, wa) + ba)
        ig = _sigmoid(_dot(ub, wx) + bx)
        if keep is not None:
            keep[0][pl.ds(t0, rt), :] = r
            keep[1][pl.ds(t0, rt), :] = ig
        la = (LRU_C * r) * ls
        a = jnp.exp(la)
        a_ref[pl.ds(PAD + t0, rt), :] = a
        b_ref[pl.ds(PAD + t0, rt), :] = jnp.sqrt(_one_minus_sq(la, a)) * (ig * ut)
        return c

    lax.fori_loop(0, length // rt, tile, 0, unroll=min(GATE_UNROLL, length // rt))


def _lru_specs():
    return [pl.BlockSpec((CONV_TAPS, LANE), lambda n: (0, n)),
            pl.BlockSpec((1, LANE), lambda n: (0, n)),
            pl.BlockSpec((2, None, LANE, LANE), lambda n: (0, n, 0, 0)),
            pl.BlockSpec((2, None, LANE, LANE), lambda n: (0, n, 0, 0)),
            pl.BlockSpec((2, LANE), lambda n: (0, n)),
            pl.BlockSpec((2, LANE), lambda n: (0, n)),
            pl.BlockSpec((2, LANE), lambda n: (0, n))]


def _rglru_fwd(ug, uc, conv_w, conv_b, wa, wx, ba, bx, lam):
    nb = uc.shape[0]
    s_len, t_len = ug.shape[1], uc.shape[1]

    def body(u0_ref, uc0_ref, cw_ref, cb_ref, wa_ref, wx_ref, ba_ref, bx_ref, lam_ref, y_ref,
             upad, ubuf, abuf, hbuf):
        cw, cb = cw_ref[...], cb_ref[...]
        lsig = _log_sigmoid(lam_ref[...])
        zero = jnp.zeros((1, LANE), F32)
        _conv_fwd(uc0_ref, upad, ubuf, cw, cb, t_len)
        h0 = []
        for dr in range(2):
            _gates_fwd(ubuf, abuf, hbuf, wa_ref[dr], wx_ref[dr], ba_ref[dr:dr + 1, :], bx_ref[dr:dr + 1, :],
                       lsig[dr:dr + 1, :], t_len)
            h0.append(_scan(abuf, hbuf, hbuf, length=t_len, init=zero, reverse=(dr == 1), a_shift=0, store=False))
        _conv_fwd(u0_ref, upad, ubuf, cw, cb, s_len)
        rt = _row_tile(s_len, ROW_TILE)
        for dr in range(2):
            _gates_fwd(ubuf, abuf, hbuf, wa_ref[dr], wx_ref[dr], ba_ref[dr:dr + 1, :], bx_ref[dr:dr + 1, :],
                       lsig[dr:dr + 1, :], s_len)
            _scan(abuf, hbuf, hbuf, length=s_len, init=h0[dr], reverse=(dr == 1), a_shift=0, store=True)

            def acc(i, c, dr=dr):
                t0 = pl.multiple_of(i * rt, rt)
                h = hbuf[pl.ds(PAD + t0, rt), :]
                if dr == 0:
                    upad[pl.ds(PAD + t0, rt), :] = h
                else:
                    y_ref[pl.ds(t0, rt), :] = (upad[pl.ds(PAD + t0, rt), :] + h).astype(y_ref.dtype)
                return c

            lax.fori_loop(0, s_len // rt, acc, 0)

    seq = pltpu.VMEM((s_len + 2 * PAD, LANE), F32)
    return _call(
        body, name="rglru_fwd", grid=(nb,),
        in_specs=[pl.BlockSpec((None, s_len, LANE), lambda n: (n, 0, 0)),
                  pl.BlockSpec((None, t_len, LANE), lambda n: (n, 0, 0))] + _lru_specs(),
        out_specs=pl.BlockSpec((None, s_len, LANE), lambda n: (n, 0, 0)),
        out_shape=jax.ShapeDtypeStruct((nb, s_len, LANE), BF16),
        scratch_shapes=[seq, pltpu.VMEM((s_len, LANE), F32), seq, seq],
    )(ug, uc, conv_w, conv_b, wa, wx, ba, bx, lam)


def _rglru_bwd(ug, uc, dy, conv_w, conv_b, wa, wx, ba, bx, lam):
    nb = uc.shape[0]
    e = nb * LANE
    s_len, t_len = ug.shape[1], uc.shape[1]

    def body(u0_ref, uc0_ref, dy_ref, cw_ref, cb_ref, wa_ref, wx_ref, ba_ref, bx_ref, lam_ref,
             du_ref, duc_ref, dcw_ref, dcb_ref, dwa_ref, dwx_ref, dba_ref, dbx_ref, dlam_ref,
             upad, ubuf, abuf, hbuf, lbuf, dubuf, rbuf, ibuf, cpad, cu, ca0, ch0, ca1, ch1, cr0, ci0, cr1, ci1):
        cw, cb = cw_ref[...], cb_ref[...]
        lam_v = lam_ref[...]
        lsig = _log_sigmoid(lam_v)
        zero = jnp.zeros((1, LANE), F32)
        zpad = jnp.zeros((PAD, LANE), F32)
        for ref in (dcw_ref, dcb_ref, dwa_ref, dwx_ref, dba_ref, dbx_ref, dlam_ref):
            ref[...] = jnp.zeros_like(ref)

        def params(dr):
            return (wa_ref[dr], wx_ref[dr], ba_ref[dr:dr + 1, :], bx_ref[dr:dr + 1, :], lsig[dr:dr + 1, :])

        def direction_bwd(dr, u_ref, a_ref, h_ref, l_ref, gates, dub, length, first):
            wa_d, wx_d, ba_d, bx_d, ls_d = params(dr)
            rt = _row_tile(length, GATE_TILE)
            prev = 1 if dr == 1 else -1

            def tile(i, c):
                t0 = pl.multiple_of(i * rt, rt)
                ut = u_ref[pl.ds(t0, rt), :]
                ub = ut.astype(BF16)
                r = gates[0][pl.ds(t0, rt), :]
                ig = gates[1][pl.ds(t0, rt), :]
                la = (LRU_C * r) * ls_d
                a = a_ref[pl.ds(PAD + t0, rt), :]
                q = _one_minus_sq(la, a)
                rs = lax.rsqrt(q)
                sq = q * rs
                lm = l_ref[pl.ds(PAD + t0, rt), :]
                da = lm * h_ref[pl.ds(PAD + t0 + prev, rt), :]
                dsq = lm * ig * ut
                dig = lm * sq * ut
                dla = da * a - dsq * (a * a) * rs
                dr_ = dla * (LRU_C * ls_d)
                dlam_ref[dr:dr + 1, :] += jnp.sum(dla * (LRU_C * r), axis=0, keepdims=True)
                dpr = dr_ * r * (1.0 - r)
                dpi = dig * ig * (1.0 - ig)
                dba_ref[dr:dr + 1, :] += jnp.sum(dpr, axis=0, keepdims=True)
                dbx_ref[dr:dr + 1, :] += jnp.sum(dpi, axis=0, keepdims=True)
                dprb, dpib = dpr.astype(BF16), dpi.astype(BF16)
                dwa_ref[dr] += _dot_tn(ub, dprb)
                dwx_ref[dr] += _dot_tn(ub, dpib)
                dut = lm * sq * ig + _dot_nt(dprb, wa_d) + _dot_nt(dpib, wx_d)
                if first:
                    dub[pl.ds(PAD + t0, rt), :] = dut
                else:
                    dub[pl.ds(PAD + t0, rt), :] += dut
                return c

            lax.fori_loop(0, length // rt, tile, 0, unroll=min(GATE_UNROLL, length // rt))

        def conv_bwd(dub, src_pad, out_ref, length):
            rt = _row_tile(length, ROW_TILE)

            def tile(i, c):
                t0 = pl.multiple_of(i * rt, rt)
                dut = dub[pl.ds(PAD + t0, rt), :]
                dcb_ref[...] += jnp.sum(dut, axis=0, keepdims=True)
                acc = jnp.zeros((rt, LANE), F32)
                for k in range(CONV_TAPS):
                    sh = CONV_LEFT - k
                    acc = acc + dub[pl.ds(PAD + t0 + sh, rt), :] * cw[k:k + 1, :]
                    dcw_ref[k:k + 1, :] += jnp.sum(dut * src_pad[pl.ds(PAD + t0 - sh, rt), :], axis=0, keepdims=True)
                out_ref[pl.ds(t0, rt), :] = acc.astype(out_ref.dtype)
                return c

            lax.fori_loop(0, length // rt, tile, 0)

        _conv_fwd(uc0_ref, cpad, cu, cw, cb, t_len)
        cbufs = ((ca0, ch0), (ca1, ch1))
        cgates = ((cr0, ci0), (cr1, ci1))
        h0 = []
        for dr in range(2):
            ca, chh = cbufs[dr]
            _gates_fwd(cu, ca, chh, *params(dr), t_len, keep=cgates[dr])
            h0.append(_scan(ca, chh, chh, length=t_len, init=zero, reverse=(dr == 1), a_shift=0, store=True))
        _conv_fwd(u0_ref, upad, ubuf, cw, cb, s_len)
        rt = _row_tile(s_len, ROW_TILE)
        dh0 = []
        for dr in range(2):
            rev = dr == 1
            _gates_fwd(ubuf, abuf, hbuf, *params(dr), s_len, keep=(rbuf, ibuf))
            _scan(abuf, hbuf, hbuf, length=s_len, init=h0[dr], reverse=rev, a_shift=0, store=True)
            first_row = PAD + s_len if rev else PAD - 1
            hbuf[pl.ds(first_row, 1), :] = h0[dr]
            end_row = PAD - 1 if rev else PAD + s_len
            abuf[pl.ds(end_row, 1), :] = zero

            def copy(i, c):
                t0 = pl.multiple_of(i * rt, rt)
                lbuf[pl.ds(PAD + t0, rt), :] = dy_ref[pl.ds(t0, rt), :]
                return c

            lax.fori_loop(0, s_len // rt, copy, 0)
            _scan(abuf, lbuf, lbuf, length=s_len, init=zero, reverse=not rev, a_shift=(-1 if rev else 1), store=True)
            start = PAD + s_len - 1 if rev else PAD
            dh0.append(abuf[pl.ds(start, 1), :] * lbuf[pl.ds(start, 1), :])
            direction_bwd(dr, ubuf, abuf, hbuf, lbuf, (rbuf, ibuf), dubuf, s_len, first=(dr == 0))
        dubuf[pl.ds(0, PAD), :] = zpad
        dubuf[pl.ds(PAD + s_len, PAD), :] = zpad
        conv_bwd(dubuf, upad, du_ref, s_len)
        lc = lbuf
        duc_buf = dubuf
        for dr in range(2):
            rev = dr == 1
            ca, chh = cbufs[dr]
            first_row = PAD + t_len if rev else PAD - 1
            chh[pl.ds(first_row, 1), :] = zero
            end_row = PAD - 1 if rev else PAD + t_len
            ca[pl.ds(end_row, 1), :] = zero + 1.0
            rtc = _row_tile(t_len, ROW_TILE)

            def clear(i, c):
                t0 = pl.multiple_of(i * rtc, rtc)
                lc[pl.ds(PAD + t0, rtc), :] = jnp.zeros((rtc, LANE), F32)
                return c

            lax.fori_loop(0, t_len // rtc, clear, 0)
            _scan(ca, lc, lc, length=t_len, init=dh0[dr], reverse=not rev, a_shift=(-1 if rev else 1), store=True)
            direction_bwd(dr, cu, ca, chh, lc, cgates[dr], duc_buf, t_len, first=(dr == 0))
        duc_buf[pl.ds(0, PAD), :] = zpad
        duc_buf[pl.ds(PAD + t_len, PAD), :] = zpad
        conv_bwd(duc_buf, cpad, duc_ref, t_len)
        dlam_ref[...] = dlam_ref[...] * (1.0 - _sigmoid(lam_v))

    seq = pltpu.VMEM((s_len + 2 * PAD, LANE), F32)
    cseq = pltpu.VMEM((t_len + 2 * PAD, LANE), F32)
    flat = pltpu.VMEM((s_len, LANE), F32)
    cflat = pltpu.VMEM((t_len, LANE), F32)
    vec2 = pl.BlockSpec((2, LANE), lambda n: (0, n))
    wspec = pl.BlockSpec((2, None, LANE, LANE), lambda n: (0, n, 0, 0))
    return _call(
        body, name="rglru_bwd", grid=(nb,),
        in_specs=[pl.BlockSpec((None, s_len, LANE), lambda n: (n, 0, 0)),
                  pl.BlockSpec((None, t_len, LANE), lambda n: (n, 0, 0)),
                  pl.BlockSpec((None, s_len, LANE), lambda n: (n, 0, 0))] + _lru_specs(),
        out_specs=[pl.BlockSpec((s_len, LANE), lambda n: (0, n)),
                   pl.BlockSpec((t_len, LANE), lambda n: (0, n)),
                   pl.BlockSpec((CONV_TAPS, LANE), lambda n: (0, n)),
                   pl.BlockSpec((1, LANE), lambda n: (0, n)),
                   wspec, wspec, vec2, vec2, vec2],
        out_shape=[jax.ShapeDtypeStruct((s_len, e), BF16), jax.ShapeDtypeStruct((t_len, e), BF16),
                   jax.ShapeDtypeStruct((CONV_TAPS, e), F32), jax.ShapeDtypeStruct((1, e), F32),
                   jax.ShapeDtypeStruct((2, nb, LANE, LANE), F32), jax.ShapeDtypeStruct((2, nb, LANE, LANE), F32),
                   jax.ShapeDtypeStruct((2, e), F32), jax.ShapeDtypeStruct((2, e), F32), jax.ShapeDtypeStruct((2, e), F32)],
        scratch_shapes=[seq, flat, seq, seq, seq, seq, flat, flat,
                        cseq, cflat, cseq, cseq, cseq, cseq, cflat, cflat, cflat, cflat],
    )(ug, uc, dy, conv_w, conv_b, wa, wx, ba, bx, lam)


def _pool_windows(src_ref, out_ref, colbuf, rowbuf, half, transpose, s_len):
    gw = GRID_W
    lg = gw.bit_length() - 1
    n_rows = s_len // gw
    cp, rm = POOL_CPAD, 8 * gw
    stride = gw + 2 * cp
    rt = _row_tile(s_len, ROW_TILE)
    assert rt % gw == 0 and half <= cp
    gpt = rt // gw
    offs = range(-half, half)
    zmargin = jnp.zeros((cp, LANE), F32)

    def zcol(r, c):
        base = pl.multiple_of(r * stride, SUB)
        colbuf[pl.ds(base, cp), :] = zmargin
        colbuf[pl.ds(base + cp + gw, cp), :] = zmargin
        return c

    lax.fori_loop(0, n_rows, zcol, 0)

    def zrow(i, c):
        t0 = pl.multiple_of(i * gw, gw)
        rowbuf[pl.ds(t0, gw), :] = jnp.zeros((gw, LANE), F32)
        rowbuf[pl.ds(rm + s_len + t0, gw), :] = jnp.zeros((gw, LANE), F32)
        return c

    lax.fori_loop(0, rm // gw, zrow, 0)

    col = lax.broadcasted_iota(jnp.int32, (gw, LANE), 0)
    ccnt = (jnp.minimum(col + half, gw) - jnp.maximum(col - half, 0)).astype(F32)

    def row_counts(t0):
        row = (t0 + lax.broadcasted_iota(jnp.int32, (rt, LANE), 0)) >> lg
        return (jnp.minimum(row + half, n_rows) - jnp.maximum(row - half, 0)).astype(F32)

    def col_base(t0, g):
        return pl.multiple_of((t0 // gw) * stride, SUB) + g * stride + cp

    def col_sum(t0, g, sign):
        acc = jnp.zeros((gw, LANE), F32)
        for o in offs:
            acc = acc + colbuf[pl.ds(col_base(t0, g) + sign * o, gw), :]
        return acc

    def row_sum(t0, sign):
        acc = jnp.zeros((rt, LANE), F32)
        for o in offs:
            acc = acc + rowbuf[pl.ds(rm + t0 + sign * o * gw, rt), :]
        return acc

    n_tiles = s_len // rt
    assert rt >= half * gw

    def loop(fn, edges=False):
        def step(i, c):
            t0 = pl.multiple_of(i * rt, rt)
            fn(t0, False) if edges else fn(t0)
            return c
        if edges:
            fn(0, True)
            if n_tiles > 1:
                fn(s_len - rt, True)
            lax.fori_loop(1, n_tiles - 1, step, 0)
        else:
            lax.fori_loop(0, n_tiles, step, 0)

    inv_ccnt = 1.0 / ccnt

    def by_row_count(v, t0, edge):
        return v / row_counts(t0) if edge else v * (1.0 / (2 * half))

    if not transpose:
        def fill(t0):
            for g in range(gpt):
                colbuf[pl.ds(col_base(t0, g), gw), :] = src_ref[pl.ds(t0 + g * gw, gw), :]

        def cols(t0):
            for g in range(gpt):
                rowbuf[pl.ds(rm + t0 + g * gw, gw), :] = col_sum(t0, g, 1) * inv_ccnt

        def rows(t0, edge):
            mean = by_row_count(row_sum(t0, 1), t0, edge)
            out_ref[pl.ds(t0, rt), :] = (mean - src_ref[pl.ds(t0, rt), :]).astype(out_ref.dtype)

        loop(fill)
        loop(cols)
        loop(rows, edges=True)
    else:
        def fill(t0, edge):
            rowbuf[pl.ds(rm + t0, rt), :] = by_row_count(src_ref[pl.ds(t0, rt), :], t0, edge)

        def rows(t0):
            acc = row_sum(t0, -1)
            for g in range(gpt):
                colbuf[pl.ds(col_base(t0, g), gw), :] = acc[g * gw:(g + 1) * gw, :] * inv_ccnt

        def cols(t0):
            for g in range(gpt):
                rows_g = pl.ds(t0 + g * gw, gw)
                out_ref[rows_g, :] = (col_sum(t0, g, -1) - src_ref[rows_g, :]).astype(out_ref.dtype)

        loop(fill, edges=True)
        loop(rows)
        loop(cols)


def _pool_map(src, nb, transpose, out_chunk_major, name):
    s_len = src.shape[1]
    cpg = nb // len(POOL_WINDOWS)

    def body(src_ref, out_ref, colbuf, rowbuf):
        n = pl.program_id(0)
        for gi, w in enumerate(POOL_WINDOWS):
            @pl.when(n // cpg == gi)
            def _(w=w):
                _pool_windows(src_ref, out_ref, colbuf, rowbuf, w // 2, transpose, s_len)

    if out_chunk_major:
        out_spec = pl.BlockSpec((None, s_len, LANE), lambda n: (n, 0, 0))
        out_shape = jax.ShapeDtypeStruct((nb, s_len, LANE), BF16)
    else:
        out_spec = pl.BlockSpec((s_len, LANE), lambda n: (0, n))
        out_shape = jax.ShapeDtypeStruct((s_len, nb * LANE), BF16)
    return _call(
        body, name=name, grid=(nb,),
        in_specs=[pl.BlockSpec((None, s_len, LANE), lambda n: (n, 0, 0))],
        out_specs=out_spec, out_shape=out_shape,
        scratch_shapes=[pltpu.VMEM((s_len // GRID_W * (GRID_W + 2 * POOL_CPAD), LANE), F32),
                        pltpu.VMEM((s_len + 16 * GRID_W, LANE), F32)],
    )(src)


def _group_weight(w_ref):
    return jnp.concatenate([w_ref[k] for k in range(N_CHIPS)], axis=0)


def _pool_mm_fwd(dm, wp, scale):
    nb, rows, _ = dm.shape
    _, ng, pq, pg = wp.shape
    cpg = pg // LANE
    tm = _row_tile(rows, 2048)

    def body(d_ref, w_ref, s_ref, y_ref):
        _put_chunks(y_ref, _dot(_cat(d_ref, cpg), _group_weight(w_ref)) * s_ref[...], cpg)

    cspec = pl.BlockSpec((cpg, tm, LANE), lambda i, g: (g, i, 0))
    return _call(
        body, name="pool_mm_fwd", grid=(rows // tm, ng),
        in_specs=[cspec, pl.BlockSpec((N_CHIPS, None, pq, pg), lambda i, g: (0, g, 0, 0)),
                  pl.BlockSpec((1, pg), lambda i, g: (0, g))],
        out_specs=cspec, out_shape=jax.ShapeDtypeStruct((nb, rows, LANE), BF16),
    )(dm, wp, scale)


def _pool_mm_bwd(dy, dm, wp, scale, gbuf, row0):
    nb, rows, _ = dm.shape
    _, ng, pq, pg = wp.shape
    cpg = pg // LANE
    tm = _row_tile(rows, 1024)
    nt = rows // tm
    assert gbuf.shape[2] == 2 * pg and row0 % pq == 0

    def body(dy_ref, d_ref, w_ref, s_ref, buf_ref, dd_ref, dwp_ref, dsc_ref, acc):
        i = pl.program_id(1)

        @pl.when(i == 0)
        def _():
            acc[...] = jnp.zeros_like(acc)
            dsc_ref[...] = jnp.zeros_like(dsc_ref)

        dyv = _cat(dy_ref, cpg)
        dc = _cat(d_ref, cpg)
        w = _group_weight(w_ref)
        dsc_ref[...] += jnp.sum(dyv * _dot(dc, w), axis=0, keepdims=True)
        dyp = (dyv * s_ref[...]).astype(BF16)
        _put_chunks(dd_ref, _dot_nt(dyp, w), cpg)
        acc[...] += _dot_tn(dc, dyp)

        @pl.when(i == nt - 1)
        def _():
            dwp_ref[...] = acc[...].reshape(N_CHIPS, pq, pg)

    cspec = pl.BlockSpec((cpg, tm, LANE), lambda g, i: (g, i, 0))
    sspec = pl.BlockSpec((1, pg), lambda g, i: (0, g))
    return _call(
        body, name="pool_mm_bwd", grid=(ng, nt),
        in_specs=[cspec, cspec, pl.BlockSpec((N_CHIPS, None, pq, pg), lambda g, i: (0, g, 0, 0)), sspec, ANY_SPEC],
        out_specs=[cspec, pl.BlockSpec((N_CHIPS, pq, pg), lambda g, i: (0, row0 // pq + g // 2, g % 2)), sspec],
        out_shape=[jax.ShapeDtypeStruct((nb, rows, LANE), F32), jax.ShapeDtypeStruct(gbuf.shape, F32),
                   jax.ShapeDtypeStruct((1, ng * pg), F32)],
        scratch_shapes=[pltpu.VMEM((pg, pg), F32)],
        input_output_aliases={4: 1},
    )(dy, dm, wp, scale, gbuf)


def _adamw_math(w, g, m, v):
    nm = ADAM_B1 * m + (1.0 - ADAM_B1) * g
    nv = ADAM_B2 * v + (1.0 - ADAM_B2) * jnp.square(g)
    m_hat = nm / (1.0 - ADAM_B1 ** ADAM_STEP)
    v_hat = nv / (1.0 - ADAM_B2 ** ADAM_STEP)
    return -ADAM_LR * (m_hat / (jnp.sqrt(v_hat) + ADAM_EPS) + ADAM_WD * w), nm, nv


def _adamw_param(w3, m3, v3, gsrcs, pick, tm, name):
    n_blk, rows, cols = w3.shape
    ng = len(gsrcs)

    def body(*refs):
        w_ref, m_ref, v_ref = refs[:3]
        g_refs = refs[3:3 + ng]
        go_ref, d_ref, nm_ref, nv_ref = refs[3 + ng:]
        g = pick(pl.program_id(0), [r[...] for r in g_refs])
        go_ref[...] = g
        d_ref[...], nm_ref[...], nv_ref[...] = _adamw_math(w_ref[...], g, m_ref[...], v_ref[...])

    spec = pl.BlockSpec((None, tm, cols), lambda n, i: (n, i, 0))
    return _call(
        body, name=name, grid=(n_blk, rows // tm),
        in_specs=[spec] * 3 + [pl.BlockSpec(shape, imap) for _, shape, imap in gsrcs],
        out_specs=[spec] * 4, out_shape=[jax.ShapeDtypeStruct(w3.shape, F32)] * 4,
    )(w3, m3, v3, *[a for a, _, _ in gsrcs])


def _adamw_small(quads):
    n = len(quads)

    def body(*refs):
        ins, outs = refs[:4 * n], refs[4 * n:]
        for k in range(n):
            w, g, m, v = (r[...] for r in ins[4 * k:4 * k + 4])
            outs[3 * k][...], outs[3 * k + 1][...], outs[3 * k + 2][...] = _adamw_math(w, g, m, v)

    flat = [a for q in quads for a in q]
    res = _call(body, name="adamw_small", grid=(1,),
                in_specs=[pl.BlockSpec(a.shape, lambda i: (0, 0)) for a in flat],
                out_specs=[pl.BlockSpec(q[0].shape, lambda i: (0, 0)) for q in quads for _ in range(3)],
                out_shape=[jax.ShapeDtypeStruct(q[0].shape, F32) for q in quads for _ in range(3)])(*flat)
    return [tuple(res[3 * k:3 * k + 3]) for k in range(n)]


def _place():
    return lax.axis_index("x"), lax.axis_index("y"), lax.axis_index("c")


def _other_chips(x, y):
    return [(1 - x, y), (x, 1 - y), (1 - x, 1 - y)]


def _gather_chips(arrays, name):
    n = len(arrays)
    halves = [a.shape[0] // 2 for a in arrays]
    for a, h in zip(arrays, halves):
        assert 2 * h == a.shape[0] and h % (32 // a.dtype.itemsize) == 0
    lands = [jnp.broadcast_to(a[None], (N_CHIPS,) + a.shape) for a in arrays]

    def body(*refs):
        outs = refs[n:2 * n]
        send_sems, recv_sems = refs[2 * n:]
        x, y, c = _place()
        me = 2 * x + y
        chips = _other_chips(x, y)

        def mine(k):
            return pl.ds(c * halves[k], halves[k])

        def theirs(k):
            return pl.ds((1 - c) * halves[k], halves[k])

        def push(k, j, src, dst, to):
            return pltpu.make_async_remote_copy(src_ref=src, dst_ref=dst, send_sem=send_sems.at[6 * k + j],
                                                recv_sem=recv_sems.at[6 * k + j], device_id=to, device_id_type=MESH)

        started = []
        for j, (cx, cy) in enumerate(chips):
            for k in range(n):
                own = outs[k].at[me, mine(k)]
                cp = push(k, j, own, own, (cx, cy, c))
                cp.start()
                started.append(cp)
        for j, (cx, cy) in enumerate(chips):
            for k in range(n):
                slab = outs[k].at[2 * cx + cy, mine(k)]
                push(k, j, slab, slab, (x, y, c)).wait_recv()
                fwd = push(k, 3 + j, slab, slab, (x, y, 1 - c))
                fwd.start()
                started.append(fwd)
        for j, (cx, cy) in enumerate(chips):
            for k in range(n):
                slab = outs[k].at[2 * cx + cy, theirs(k)]
                push(k, 3 + j, slab, slab, (x, y, c)).wait_recv()
        for cp in started:
            cp.wait_send()

    return _call(
        body, name=name, in_specs=[ANY_SPEC] * n, out_specs=[ANY_SPEC] * n,
        out_shape=[jax.ShapeDtypeStruct(a.shape, a.dtype) for a in lands],
        input_output_aliases={k: k for k in range(n)},
        scratch_shapes=[pltpu.SemaphoreType.DMA((6 * n,)), pltpu.SemaphoreType.DMA((6 * n,))],
    )(*lands)


def _gather_devices(v):
    shape = v.shape

    def body(v_ref, out_ref, send_sems, recv_sems):
        x, y, c = _place()
        me = 4 * x + 2 * y + c
        out_ref[me] = v_ref[...]
        sends = []
        for k in range(1, N_DEV):
            to = (me + k) % N_DEV
            cp = pltpu.make_async_remote_copy(src_ref=v_ref, dst_ref=out_ref.at[me], send_sem=send_sems.at[k],
                                              recv_sem=recv_sems.at[k], device_id=(to // 4, (to // 2) % 2, to % 2),
                                              device_id_type=MESH)
            cp.start()
            sends.append(cp)
        for k in range(1, N_DEV):
            frm = (me + N_DEV - k) % N_DEV
            pltpu.make_async_remote_copy(src_ref=v_ref, dst_ref=out_ref.at[frm], send_sem=send_sems.at[k],
                                         recv_sem=recv_sems.at[k], device_id=(x, y, c), device_id_type=MESH).wait_recv()
        for cp in sends:
            cp.wait_send()

    vspec = pl.BlockSpec(memory_space=pltpu.VMEM)
    return _call(body, name="gather_devices", in_specs=[vspec], out_specs=vspec,
                 out_shape=jax.ShapeDtypeStruct((N_DEV,) + shape, F32),
                 scratch_shapes=[pltpu.SemaphoreType.DMA((N_DEV,)), pltpu.SemaphoreType.DMA((N_DEV,))])(v)


HBM_SPEC = pl.BlockSpec(memory_space=pltpu.HBM)
SEM_SPEC = pl.BlockSpec(memory_space=pltpu.SEMAPHORE)
SIDE_EFFECT = pltpu.SideEffectType.DATAFLOW_SIDE_EFFECTING


def _push_copies(src_refs, land_refs, send_sems, recv_sems, per_peer):
    x, y, c = _place()
    me = 2 * x + y
    copies = []
    for j, (cx, cy) in enumerate(_other_chips(x, y)):
        for k, (src, land) in enumerate(zip(src_refs, land_refs)):
            copies.append(pltpu.make_async_remote_copy(
                src_ref=src.at[2 * cx + cy] if per_peer else src, dst_ref=land.at[me], send_sem=send_sems.at[3 * k + j],
                recv_sem=recv_sems.at[3 * k + j], device_id=(cx, cy, c), device_id_type=MESH))
    return copies


def _push_start(srcs, lands, per_peer, after, name):
    n = len(srcs)

    def body(*refs):
        src_refs, land_refs = refs[:n], refs[n:2 * n]
        send_sems, recv_sems = refs[2 * n + 1], refs[2 * n + 2]
        token = refs[-1]
        for cp in _push_copies(src_refs, land_refs, send_sems, recv_sems, per_peer):
            cp.start()
        token[...] = jnp.zeros_like(token)

    bufs = [pltpu.with_memory_space_constraint(a, pltpu.HBM) for a in list(srcs) + list(lands)]
    res = _call(
        body, name=name,
        out_shape=[pltpu.SemaphoreType.DMA((3 * n,)), pltpu.SemaphoreType.DMA((3 * n,))]
        + [pltpu.HBM(a.shape, a.dtype) for a in bufs] + [jax.ShapeDtypeStruct((SUB, LANE), F32)],
        in_specs=[HBM_SPEC] * (2 * n) + [ANY_SPEC],
        out_specs=[SEM_SPEC, SEM_SPEC] + [HBM_SPEC] * (2 * n) + [pl.BlockSpec(memory_space=pltpu.VMEM)],
        input_output_aliases={i: 2 + i for i in range(2 * n)},
        compiler_params=pltpu.CompilerParams(has_side_effects=SIDE_EFFECT),
    )(*bufs, after)
    return res[0], res[1], list(res[2:2 + n]), list(res[2 + n:2 + 2 * n]), res[-1]


def _push_wait(send_sems, recv_sems, srcs, lands, per_peer, after, name):
    n = len(srcs)

    def body(*refs):
        src_refs, land_refs = refs[:n], refs[n:2 * n]
        send_sems, recv_sems = refs[2 * n], refs[2 * n + 1]
        for cp in _push_copies(src_refs, land_refs, send_sems, recv_sems, per_peer):
            cp.wait_send()
            cp.wait_recv()

    res = _call(
        body, name=name,
        out_shape=[pltpu.HBM(a.shape, a.dtype) for a in list(srcs) + list(lands)],
        in_specs=[HBM_SPEC] * (2 * n) + [SEM_SPEC, SEM_SPEC, ANY_SPEC],
        out_specs=[HBM_SPEC] * (2 * n),
        input_output_aliases={i: i for i in range(2 * n)},
        compiler_params=pltpu.CompilerParams(has_side_effects=SIDE_EFFECT),
    )(*srcs, *lands, send_sems, recv_sems, after)
    return list(res[n:])


def _sibling_swap(g):
    _, rows, w = g.shape
    half = rows // 2

    def body(g_ref, out_ref, send_sem, recv_sem):
        x, y, c = _place()
        cp = pltpu.make_async_remote_copy(src_ref=g_ref.at[:, pl.ds((1 - c) * half, half)], dst_ref=out_ref,
                                          send_sem=send_sem, recv_sem=recv_sem, device_id=(x, y, 1 - c), device_id_type=MESH)
        cp.start()
        cp.wait()

    return _call(body, name="rs_sibling_swap", in_specs=[ANY_SPEC], out_specs=ANY_SPEC,
                 out_shape=jax.ShapeDtypeStruct((N_CHIPS, half, w), F32),
                 scratch_shapes=[pltpu.SemaphoreType.DMA, pltpu.SemaphoreType.DMA])(g)


def _pair_add(g, got, place):
    _, rows, w = g.shape
    half = rows // 2
    tm = _row_tile(half, RS_TILE)
    nt = half // tm

    def body(p_ref, a_ref, b_ref, o_ref, own_ref):
        v = a_ref[...] + b_ref[...]
        o_ref[...] = v.astype(BF16)

        @pl.when(pl.program_id(1) == p_ref[1])
        def _():
            own_ref[...] = v

    return _call(
        body, name="rs_pair_add",
        grid_spec=pltpu.PrefetchScalarGridSpec(
            num_scalar_prefetch=1, grid=(nt, N_CHIPS),
            in_specs=[pl.BlockSpec((None, tm, w), lambda i, s, p: (s, p[0] * nt + i, 0)),
                      pl.BlockSpec((None, tm, w), lambda i, s, p: (s, i, 0))],
            out_specs=[pl.BlockSpec((None, tm, w), lambda i, s, p: (s, i, 0)),
                       pl.BlockSpec((tm, w), lambda i, s, p: (i, 0))]),
        out_shape=[jax.ShapeDtypeStruct((N_CHIPS, half, w), BF16), jax.ShapeDtypeStruct((half, w), F32)],
    )(place, g, got)


def _sum_chips(parts, own, place):
    _, half, w = parts.shape
    tm = _row_tile(half, RS_TILE)
    nt = half // tm

    def body(p_ref, parts_ref, own_ref, o_ref):
        me = p_ref[1]
        t = [jnp.where(me == q, own_ref[...], parts_ref[q].astype(F32)) for q in range(N_CHIPS)]
        o_ref[...] = (t[0] + t[1]) + (t[2] + t[3])

    return _call(
        body, name="rs_sum_chips",
        grid_spec=pltpu.PrefetchScalarGridSpec(
            num_scalar_prefetch=1, grid=(nt,),
            in_specs=[pl.BlockSpec((N_CHIPS, tm, w), lambda i, p: (0, i, 0)), pl.BlockSpec((tm, w), lambda i, p: (i, 0))],
            out_specs=pl.BlockSpec((tm, w), lambda i, p: (p[0] * nt + i, 0))),
        out_shape=jax.ShapeDtypeStruct((2 * half, w), F32),
    )(place, parts, own)


def _sibling_gather(red):
    rows, w = red.shape
    half = rows // 2

    def body(in_ref, out_ref, send_sem, recv_sem):
        x, y, c = _place()
        mine = out_ref.at[pl.ds(c * half, half)]
        cp = pltpu.make_async_remote_copy(src_ref=mine, dst_ref=mine, send_sem=send_sem, recv_sem=recv_sem,
                                          device_id=(x, y, 1 - c), device_id_type=MESH)
        cp.start()
        other = out_ref.at[pl.ds((1 - c) * half, half)]
        pltpu.make_async_remote_copy(src_ref=other, dst_ref=other, send_sem=send_sem, recv_sem=recv_sem,
                                     device_id=(x, y, c), device_id_type=MESH).wait_recv()
        cp.wait_send()

    return _call(body, name="rs_sibling_gather", in_specs=[ANY_SPEC], out_specs=ANY_SPEC,
                 out_shape=jax.ShapeDtypeStruct(red.shape, F32), input_output_aliases={0: 0},
                 scratch_shapes=[pltpu.SemaphoreType.DMA, pltpu.SemaphoreType.DMA])(red)


def _rs_begin(g, place, name):
    pair, own = _pair_add(g, _sibling_swap(g), place)
    send, recv, pair, parts, token = _push_start([pair], [jnp.zeros_like(pair)], True, own, name + "_start")
    return (send, recv, pair, parts, own), token


def _rs_end(state, place, after, name):
    send, recv, pair, parts, own = state
    (parts,) = _push_wait(send, recv, pair, parts, True, after, name + "_wait")
    return _sibling_gather(_sum_chips(parts, own, place))


WEIGHTS = ("c_ctx", "w_mod", "b_mod", "w_in", "w_out", "ln_g", "ln_b", "conv_w", "conv_b", "lru_wa", "lru_ba", "lru_wx",
           "lru_bx", "lru_lam", "pool_w", "pool_scale")
SMALL_GATHERED = ("conv_w", "lru_ba", "lru_bx", "lru_lam", "pool_scale")
SMALL_UPDATED = ("c_ctx", "b_mod", "ln_g", "ln_b", "conv_w", "conv_b", "lru_ba", "lru_bx", "lru_lam", "pool_scale")


def kernel(x, c, ctx, c_ctx, w_mod, b_mod, w_in, w_out, ln_g, ln_b, conv_w, conv_b, lru_wa, lru_ba, lru_wx, lru_bx, lru_lam, pool_w, pool_scale, loss_target, m_c_ctx, m_w_mod, m_b_mod, m_w_in, m_w_out, m_ln_g, m_ln_b, m_conv_w, m_conv_b, m_lru_wa, m_lru_ba, m_lru_wx, m_lru_bx, m_lru_lam, m_pool_w, m_pool_scale, v_c_ctx, v_w_mod, v_b_mod, v_w_in, v_w_out, v_ln_g, v_ln_b, v_conv_w, v_conv_b, v_lru_wa, v_lru_ba, v_lru_wx, v_lru_bx, v_lru_lam, v_pool_w, v_pool_scale):
    weights = dict(c_ctx=c_ctx, w_mod=w_mod, b_mod=b_mod, w_in=w_in, w_out=w_out, ln_g=ln_g, ln_b=ln_b, conv_w=conv_w,
                   conv_b=conv_b, lru_wa=lru_wa, lru_ba=lru_ba, lru_wx=lru_wx, lru_bx=lru_bx, lru_lam=lru_lam,
                   pool_w=pool_w, pool_scale=pool_scale)
    mom1 = dict(c_ctx=m_c_ctx, w_mod=m_w_mod, b_mod=m_b_mod, w_in=m_w_in, w_out=m_w_out, ln_g=m_ln_g, ln_b=m_ln_b,
                conv_w=m_conv_w, conv_b=m_conv_b, lru_wa=m_lru_wa, lru_ba=m_lru_ba, lru_wx=m_lru_wx, lru_bx=m_lru_bx,
                lru_lam=m_lru_lam, pool_w=m_pool_w, pool_scale=m_pool_scale)
    mom2 = dict(c_ctx=v_c_ctx, w_mod=v_w_mod, b_mod=v_b_mod, w_in=v_w_in, w_out=v_w_out, ln_g=v_ln_g, ln_b=v_ln_b,
                conv_w=v_conv_w, conv_b=v_conv_b, lru_wa=v_lru_wa, lru_ba=v_lru_ba, lru_wx=v_lru_wx, lru_bx=v_lru_bx,
                lru_lam=v_lru_lam, pool_w=v_pool_w, pool_scale=v_pool_scale)
    xs, cx, target = x[0], ctx[0], loss_target[0]
    s_len, d = xs.shape
    es = w_out.shape[1]
    e = es * N_CHIPS
    nb = e // LANE
    c3 = w_mod.shape[2]
    n4 = w_in.shape[2]
    pq, pg = pool_w.shape[2], pool_w.shape[3]
    ng = len(POOL_WINDOWS)
    width = n4
    assert width == d and 2 * pg == width and 2 * nb * LANE == N_CHIPS * width and d % (2 * N_CHIPS) == 0
    px, py, pc = _place()
    place = jnp.stack([pc, 2 * px + py]).astype(jnp.int32)
    cctx2 = c_ctx[None, :]

    eq = e // N_CHIPS
    small_rows = [(conv_w[0], 0), (lru_ba[0], CONV_TAPS), (lru_bx[0], CONV_TAPS + 2), (lru_lam[0], CONV_TAPS + 4),
                  (pool_scale, CONV_TAPS + 6)]
    small = _rows_kernel([(a, r, 0) for a, r in small_rows], 2 * SUB, eq, "pack_small_weights")
    wm_g, win0, sg = _gather_chips([w_mod.astype(BF16).reshape(DEPTH * d, c3), w_in[0].astype(BF16), small], "gather_weights0")
    later = [w_out[0].astype(BF16), w_in[1].astype(BF16), w_out[1].astype(BF16), pool_w.astype(BF16).reshape(ng * pq, pg)]
    w_send, w_recv, later, later_lands, w_token = _push_start(
        later, [jnp.broadcast_to(a[None], (N_CHIPS,) + a.shape) for a in later], False, sg, "gather_weights1_start")
    wm_all = wm_g.reshape(N_CHIPS, DEPTH, d, c3)
    full = {n: jnp.swapaxes(sg[:, r:r + a.shape[0]], 0, 1).reshape(a.shape[0], e)
            for n, (a, r) in zip(SMALL_GATHERED, small_rows)}
    wa_b, wx_b = lru_wa[0].astype(BF16), lru_wx[0].astype(BF16)
    lru_args = (full["conv_w"], conv_b, wa_b, wx_b, full["lru_ba"], full["lru_bx"], full["lru_lam"])
    scale_f = full["pool_scale"]

    mod = _mod_fwd(c + w_token[0:1, 0:1], cctx2, wm_all, b_mod[:, None, :])

    def mod_parts(l, row):
        v = mod[l, row]
        return v[None, :d], 1.0 + v[None, d:2 * d], v[None, 2 * d:]

    sh0, sc0, gt0 = mod_parts(0, 0)
    shc, scc, _ = mod_parts(0, 1)
    sh1, sc1, gt1 = mod_parts(1, 0)
    lg = [ln_g[l][None, :] for l in range(DEPTH)]
    lb = [ln_b[l][None, :] for l in range(DEPTH)]

    uu0, ug0 = _inproj_fwd(xs, sc0, sh0, win0, "inproj_fwd0")
    uc0 = _inproj_fwd(cx, scc, shc, win0[:2], "inproj_fwd_ctx")
    y0 = _rglru_fwd(uu0, uc0, *lru_args)
    wout0_g, win1, wout1_g, wp_g = _push_wait(w_send, w_recv, later, later_lands, False, y0, "gather_weights1_wait")
    win = [win0, win1]
    wout = [wout0_g.reshape(e, d), wout1_g.reshape(e, d)]
    wp = wp_g.reshape(N_CHIPS, ng, pq, pg)
    br0, x1 = _outproj_fwd(y0, ug0, xs, gt0, wout[0], lg[0], lb[0], None, "outproj_fwd0")
    uu1, ug1 = _inproj_fwd(x1, sc1, sh1, win[1], "inproj_fwd1")
    d1 = _pool_map(uu1, nb, False, True, "pool_fwd")
    y1 = _pool_mm_fwd(d1, wp, scale_f)
    br1, dxo, loss_part = _outproj_fwd(y1, ug1, x1, gt1, wout[1], lg[1], lb[1], target, "outproj_fwd1")
    loss = lax.psum(jnp.sum(loss_part) * (0.5 / d), ("x", "y", "c"))

    row_wout = d
    row_tail = d + es
    wq = 2 * (nb // N_CHIPS) * LANE * LANE // width
    whole = lambda r: (r + 2 * RS_TILE - 1) // (2 * RS_TILE) * (2 * RS_TILE)
    rows1 = whole(row_tail + pg // 2)
    rows0 = whole(row_tail + 2 * wq)
    gbuf1 = jnp.zeros((N_CHIPS, rows1, width), F32)
    gbuf0 = jnp.zeros((N_CHIPS, rows0, width), F32)

    dy1, dg1, dxres1, dbr1, dlg1, dlb1, dgt1 = _outproj_bwd(dxo, x1, br1, y1, ug1, gt1, lg[1], wout[1], "outproj_bwd1")
    gbuf1 = _outproj_bwd_w(y1, ug1, dbr1, gbuf1, row_wout, "outproj_bwd_w1")
    dd1, gbuf1, dscale = _pool_mm_bwd(dy1, d1, wp, scale_f, gbuf1, row_tail)
    du1 = _pool_map(dd1, nb, True, False, "pool_bwd")
    dx1, dsc1, dsh1 = _inproj_bwd_x([du1, dg1], x1, dxres1, sc1, win[1], "inproj_bwd_x1")
    gbuf1 = _inproj_bwd_w(x1, sc1, sh1, [du1, dg1], None, gbuf1, "inproj_bwd_w1")
    rs1, token1 = _rs_begin(gbuf1, place, "rs_exchange1")

    dy0, dg0, dxres0, dbr0, dlg0, dlb0, dgt0 = _outproj_bwd(dx1, xs, br0, y0, ug0, gt0 + token1[0:1, 0:1], lg[0], wout[0],
                                                            "outproj_bwd0")
    gbuf0 = _outproj_bwd_w(y0, ug0, dbr0, gbuf0, row_wout, "outproj_bwd_w0")
    du0, duc, dconv_w, dconv_b, dwa, dwx, dba, dbx, dlam = _rglru_bwd(uu0, uc0, dy0, *lru_args)
    dwin0c = _inproj_bwd_w(cx, scc, shc, [duc, jnp.zeros_like(duc)], None, None, "inproj_bwd_w_ctx")
    gbuf0 = _inproj_bwd_w(xs, sc0, sh0, [du0, dg0], dwin0c, gbuf0, "inproj_bwd_w0")

    def quarter(dw):
        t = dw.reshape(2, N_CHIPS, nb // N_CHIPS, LANE, LANE)
        return jnp.transpose(t, (1, 3, 0, 2, 4)).reshape(N_CHIPS, LANE, 2 * (nb // N_CHIPS) * LANE).reshape(N_CHIPS, wq, width)

    tail0 = jnp.concatenate([quarter(dwa), quarter(dwx)], axis=1)
    gbuf0 = lax.dynamic_update_slice(gbuf0, tail0, (0, row_tail, 0))
    red1 = _rs_end(rs1, place, gbuf0, "rs_exchange1")
    rs0, token0 = _rs_begin(gbuf0, place, "rs_exchange0")
    grad_x, dsc0, dsh0 = _inproj_bwd_x([du0, dg0], xs, dxres0, sc0 + token0[0:1, 0:1], win[0], "inproj_bwd_x0")
    dscc, dshc = _inproj_bwd_x([duc], cx, None, scc, win[0][:2], "inproj_bwd_x_ctx")

    k0 = VEC_KINDS
    vec = _rows_kernel(
        [(c, 0, 0), (dsh0, 1, 0), (dsc0, 1, d), (dgt0, 1, 2 * d), (dshc, 2, 0), (dscc, 2, d),
         (dsh1, 3, 0), (dsc1, 3, d), (dgt1, 3, 2 * d),
         (dconv_b, k0, 0), (dlg0, k0, e), (dscale, k0 + 1, 0), (dlg1, k0 + 1, e), (dlb0, k0 + 2, 0), (dlb1, k0 + 2, d),
         (dconv_w, k0 + 3, 0), (dba, k0 + 7, 0), (dbx, k0 + 9, 0), (dlam, k0 + 11, 0)], VEC_ROWS, 3 * d, "pack_vec")
    gt_all = jnp.swapaxes(_gather_devices(vec), 0, 1)
    g_wmod = _mod_bwd_shard(gt_all, cctx2, place, c3)
    g_bmod, g_cctx, g_small = _mod_bwd_rep(gt_all, cctx2, wm_all)
    red0 = _rs_end(rs0, place, g_bmod, "rs_exchange0")
    (rep,) = _gather_chips([red0[row_tail:row_tail + 2 * wq]], "gather_replicated")

    tmw = _row_tile(d, 256)
    red_src = lambda red, r0, tm: (red, (tm, width), lambda n, i: (r0 // tm + i, 0))
    by_layer = lambda n, gs: jnp.where(n == 0, gs[0], gs[1])
    outs = {}
    outs["w_in"] = _adamw_param(w_in, m_w_in, v_w_in, [red_src(red0, 0, tmw), red_src(red1, 0, tmw)], by_layer, tmw, "adamw_w_in")
    outs["w_out"] = _adamw_param(w_out, m_w_out, v_w_out, [red_src(red0, row_wout, tmw), red_src(red1, row_wout, tmw)],
                                 by_layer, tmw, "adamw_w_out")
    outs["w_mod"] = _adamw_param(w_mod, m_w_mod, v_w_mod, [(g_wmod, (None, tmw, c3), lambda n, i: (n, i, 0))],
                                 lambda n, gs: gs[0], tmw, "adamw_w_mod")
    pw = [a.reshape(ng, pq, pg) for a in (pool_w, m_pool_w, v_pool_w)]
    outs["pool_w"] = [o.reshape(pool_w.shape) for o in _adamw_param(
        *pw, [(red1, (pq, pg), lambda n, i: (row_tail // pq + n // 2, n % 2))], lambda n, gs: gs[0], pq, "adamw_pool_w")]
    bq = nb // N_CHIPS
    rep_src = lambda r0: (rep, (None, LANE, bq * LANE), lambda n, i: (n % N_CHIPS, r0 // LANE, n // N_CHIPS))
    stack = lambda n, gs: jnp.concatenate([gs[0][:, k * LANE:(k + 1) * LANE] for k in range(bq)], axis=0)
    for name, r0, trio in (("lru_wa", 0, (lru_wa, m_lru_wa, v_lru_wa)), ("lru_wx", wq, (lru_wx, m_lru_wx, v_lru_wx))):
        blocks = [a.reshape(2 * N_CHIPS, bq * LANE, LANE) for a in trio]
        outs[name] = [o.reshape(lru_wa.shape) for o in _adamw_param(*blocks, [rep_src(r0)], stack, bq * LANE, "adamw_" + name)]

    g_small = dict(g_small, c_ctx=g_cctx, b_mod=g_bmod)
    for n in SMALL_GATHERED:
        g_small[n] = lax.dynamic_slice_in_dim(g_small[n], place[1] * eq, eq, axis=1)
    as2d = lambda a: a.reshape(-1, a.shape[-1])
    quads = [(as2d(weights[n]), g_small[n], as2d(mom1[n]), as2d(mom2[n])) for n in SMALL_UPDATED]
    for n, (q, res) in zip(SMALL_UPDATED, zip(quads, _adamw_small(quads))):
        outs[n] = [a.reshape(weights[n].shape) for a in (q[1],) + res]

    result = [loss, grad_x[None]]
    for j in range(4):
        result += [outs[n][j] for n in WEIGHTS]
    return tuple(result)
```

```python
import jax
import jax.numpy as jnp
from jax import lax
from jax.experimental import pallas as pl
from jax.experimental.pallas import tpu as pltpu

F32 = jnp.float32
BF16 = jnp.bfloat16
LANE = 128
SUB = 8
GRID_W = 64
POOL_WINDOWS = (2, 4, 8, 16)
LRU_C = 8.0
DEPTH = 2
ALPHA = float((2 * DEPTH) ** 0.25)
LN_EPS = 1e-5
ADAM_LR, ADAM_B1, ADAM_B2, ADAM_EPS, ADAM_WD, ADAM_STEP = 0.001, 0.9, 0.999, 1e-08, 0.01, 10
N_CHIPS = 4
N_DEV = 8
MESH = pl.DeviceIdType.MESH
ROW_TILE = 512
GATE_TILE = 2048
GATE_UNROLL = 1
CONV_TAPS = 4
CONV_LEFT = 2
PAD = 8
SCAN_UNROLL = 32
RS_TILE = 448
LN_ROWS = 128
POOL_CPAD = 16
VEC_KINDS = 4


def _call(body, **kw):
    return pl.pallas_call(body, **kw)


def _dot(a, b):
    return jnp.dot(a, b, preferred_element_type=F32)


def _dot_nt(a, b):
    return lax.dot_general(a, b, (((1,), (1,)), ((), ())), preferred_element_type=F32)


def _dot_tn(a, b):
    return lax.dot_general(a, b, (((0,), (0,)), ((), ())), preferred_element_type=F32)


def _sigmoid(v):
    return 0.5 * (jnp.tanh(0.5 * v) + 1.0)


def _silu(v):
    return v * _sigmoid(v)


def _dsilu(v):
    s = _sigmoid(v)
    return s * (1.0 + v * (1.0 - s))


def _log_sigmoid(v):
    z = jnp.exp(-jnp.abs(v))
    return jnp.minimum(v, 0.0) - jnp.where(z < 1e-4, z * (1.0 - 0.5 * z), jnp.log(1.0 + z))


def _one_minus_sq(la, a):
    return -jnp.tanh(la) * (a * a + 1.0)


def _cat(ref, n):
    return jnp.concatenate([ref[k] for k in range(n)], axis=1)


def _put_chunks(ref, val, n, base=0):
    for k in range(n):
        ref[base + k] = val[:, k * LANE:(k + 1) * LANE].astype(ref.dtype)


def _row_tile(rows, want):
    t = min(rows, want)
    assert rows % t == 0
    return t


ANY_SPEC = pl.BlockSpec(memory_space=pl.ANY)


def _mod_fwd(cvec, cctx, wm, bm):
    ns, nl, d, c3 = wm.shape

    def body(c_ref, cx_ref, w_ref, b_ref, o_ref):
        cc = jnp.concatenate([c_ref[...], cx_ref[...], jnp.zeros((SUB - 2, d), F32)], axis=0)
        o_ref[...] = _dot(_silu(cc).astype(BF16), w_ref[...]) + b_ref[...]

    return _call(
        body, name="mod_fwd", grid=(nl, ns),
        in_specs=[pl.BlockSpec((1, d), lambda l, s: (0, 0)),
                  pl.BlockSpec((1, d), lambda l, s: (0, 0)),
                  pl.BlockSpec((None, None, d, c3), lambda l, s: (s, l, 0, 0)),
                  pl.BlockSpec((None, 1, c3), lambda l, s: (l, 0, s))],
        out_specs=pl.BlockSpec((None, 8, c3), lambda l, s: (l, 0, s)),
        out_shape=jax.ShapeDtypeStruct((nl, 8, ns * c3), F32),
    )(cvec, cctx, wm, bm)


def _rows_kernel(parts, rows, cols, name):
    def body(*refs):
        o_ref = refs[-1]
        o_ref[...] = jnp.zeros_like(o_ref)
        for ref, (a, r0, c0) in zip(refs[:-1], parts):
            for k in range(a.shape[0]):
                o_ref[r0 + k:r0 + k + 1, c0:c0 + a.shape[1]] = ref[k:k + 1, :]

    return _call(body, name=name, grid=(1,),
                 in_specs=[pl.BlockSpec(a.shape, lambda i: (0, 0)) for a, _, _ in parts],
                 out_specs=pl.BlockSpec((rows, cols), lambda i: (0, 0)),
                 out_shape=jax.ShapeDtypeStruct((rows, cols), F32))(*[a for a, _, _ in parts])


def _mod_bwd_shard(gt, cctx, place, c3):
    d = cctx.shape[1]

    def body(p_ref, cs_ref, dm_ref, dmx_ref, cx_ref, o_ref):
        l = pl.program_id(0)
        lhs = jnp.concatenate([_silu(cs_ref[...]), _silu(cx_ref[...]), jnp.zeros((7, d), F32)], axis=0).astype(BF16)
        dmx = jnp.where(l == 0, jnp.sum(dmx_ref[...], axis=0, keepdims=True), 0.0)
        rhs = jnp.concatenate([dm_ref[...], dmx, jnp.zeros((7, c3), F32)], axis=0).astype(BF16)
        o_ref[...] = _dot_tn(lhs, rhs)

    return _call(
        body, name="mod_bwd_shard",
        grid_spec=pltpu.PrefetchScalarGridSpec(
            num_scalar_prefetch=1, grid=(DEPTH,),
            in_specs=[pl.BlockSpec((None, N_DEV, d), lambda l, p: (0, 0, 0)),
                      pl.BlockSpec((None, N_DEV, c3), lambda l, p: (1 + 2 * l, 0, p[1])),
                      pl.BlockSpec((None, N_DEV, c3), lambda l, p: (2, 0, p[1])),
                      pl.BlockSpec((1, d), lambda l, p: (0, 0))],
            out_specs=pl.BlockSpec((None, d, c3), lambda l, p: (l, 0, 0))),
        out_shape=jax.ShapeDtypeStruct((DEPTH, d, c3), F32),
    )(place, gt, gt, gt, cctx)


def _small_layout(d, e):
    k = VEC_KINDS
    return {
        "conv_b": ((1, e), [(0, k, 0)]),
        "ln_g": ((2, d), [(0, k, e), (1, k + 1, e)]),
        "pool_scale": ((1, e), [(0, k + 1, 0)]),
        "ln_b": ((2, d), [(0, k + 2, 0), (1, k + 2, d)]),
        "conv_w": ((CONV_TAPS, e), [(t, k + 3 + t, 0) for t in range(CONV_TAPS)]),
        "lru_ba": ((2, e), [(j, k + 7 + j, 0) for j in range(2)]),
        "lru_bx": ((2, e), [(j, k + 9 + j, 0) for j in range(2)]),
        "lru_lam": ((2, e), [(j, k + 11 + j, 0) for j in range(2)]),
    }


VEC_ROWS = 24


def _mod_bwd_rep(gt, cctx, wm):
    ns, _, d, c3 = wm.shape
    layout = _small_layout(d, ns * c3 - d)
    names = list(layout)

    def body(g_ref, cx_ref, w_ref, db_ref, dc_ref, loss_ref, *small_refs):
        loss_ref[...] = jnp.zeros_like(loss_ref) + jnp.sum(g_ref[0][:, d:d + LANE])
        dm0 = jnp.sum(g_ref[1], axis=0, keepdims=True)
        dmx = jnp.sum(g_ref[2], axis=0, keepdims=True)
        dm1 = jnp.sum(g_ref[3], axis=0, keepdims=True)
        db_ref[0:1, :] = dm0 + dmx
        db_ref[1:2, :] = dm1
        dmxb = jnp.broadcast_to(dmx, (SUB, ns * c3)).astype(BF16)
        acc = jnp.zeros((SUB, d), F32)
        for s in range(ns):
            acc = acc + _dot_nt(dmxb[:, s * c3:(s + 1) * c3], w_ref[s])
        dc_ref[...] = acc[0:1, :] * _dsilu(cx_ref[...])
        for ref, name in zip(small_refs, names):
            shape, places = layout[name]
            for arr_row, vec_row, col0 in places:
                total = jnp.sum(g_ref[vec_row], axis=0, keepdims=True)
                ref[arr_row:arr_row + 1, :] = total[:, col0:col0 + shape[1]]

    outs = _call(
        body, name="mod_bwd_rep", grid=(1,),
        in_specs=[pl.BlockSpec(gt.shape, lambda i: (0, 0, 0)),
                  pl.BlockSpec((1, d), lambda i: (0, 0)),
                  pl.BlockSpec((ns, None, d, c3), lambda i: (0, 0, 0, 0))],
        out_specs=[pl.BlockSpec((DEPTH, ns * c3), lambda i: (0, 0)), pl.BlockSpec((1, d), lambda i: (0, 0)),
                   pl.BlockSpec((1, LANE), lambda i: (0, 0))]
        + [pl.BlockSpec(layout[n][0], lambda i: (0, 0)) for n in names],
        out_shape=[jax.ShapeDtypeStruct((DEPTH, ns * c3), F32), jax.ShapeDtypeStruct((1, d), F32),
                   jax.ShapeDtypeStruct((1, LANE), F32)]
        + [jax.ShapeDtypeStruct(layout[n][0], F32) for n in names],
    )(gt, cctx, wm)
    return outs[0], outs[1], outs[2], dict(zip(names, outs[3:]))


def _inproj_fwd(xin, sc1, sh, w, name):
    rows, d = xin.shape
    ns, _, n4 = w.shape
    cpb = n4 // LANE
    tm = _row_tile(rows, 512)
    assert ns in (2, 4)

    def body(x_ref, sc_ref, sh_ref, w_ref, *o_refs):
        h = (x_ref[...] * sc_ref[...] + sh_ref[...]).astype(BF16)
        for s in range(ns):
            _put_chunks(o_refs[s // 2], _dot(h, w_ref[s]), cpb, base=(s % 2) * cpb)

    spec = pl.BlockSpec((2 * cpb, tm, LANE), lambda i: (0, i, 0))
    dtypes = (F32, BF16)[:ns // 2]
    res = _call(
        body, name=name, grid=(rows // tm,),
        in_specs=[pl.BlockSpec((tm, d), lambda i: (i, 0)),
                  pl.BlockSpec((1, d), lambda i: (0, 0)),
                  pl.BlockSpec((1, d), lambda i: (0, 0)),
                  pl.BlockSpec((ns, d, n4), lambda i: (0, 0, 0))],
        out_specs=[spec] * len(dtypes),
        out_shape=[jax.ShapeDtypeStruct((2 * cpb, rows, LANE), t) for t in dtypes],
    )(xin, sc1, sh, w)
    return res[0] if ns == 2 else tuple(res)


def _inproj_bwd_x(dparts, xin, dxres, sc1, w, name):
    rows, d = xin.shape
    npart = len(dparts)
    e = dparts[0].shape[1]
    ns, _, n4 = w.shape
    per = e // n4
    assert per * npart == ns
    tm = _row_tile(rows, 512)
    has_res = dxres is not None

    def body(*refs):
        dp = refs[:npart]
        x_ref, sc_ref, w_ref = refs[npart:npart + 3]
        rest = refs[npart + 3:]
        if has_res:
            res_ref, dx_ref, dsc_ref, dsh_ref = rest
        else:
            dsc_ref, dsh_ref = rest
        i = pl.program_id(0)
        dh = jnp.zeros((tm, d), F32)
        for p in range(npart):
            v = dp[p][...]
            for q in range(per):
                dh = dh + _dot_nt(v[:, q * n4:(q + 1) * n4], w_ref[p * per + q])

        @pl.when(i == 0)
        def _():
            dsc_ref[...] = jnp.zeros_like(dsc_ref)
            dsh_ref[...] = jnp.zeros_like(dsh_ref)

        dsc_ref[...] += jnp.sum(dh * x_ref[...], axis=0, keepdims=True)
        dsh_ref[...] += jnp.sum(dh, axis=0, keepdims=True)
        if has_res:
            dx_ref[...] = res_ref[...] + dh * sc_ref[...]

    row_spec = pl.BlockSpec((tm, d), lambda i: (i, 0))
    vec_spec = pl.BlockSpec((1, d), lambda i: (0, 0))
    in_specs = [pl.BlockSpec((tm, e), lambda i: (i, 0))] * npart + [row_spec, vec_spec,
                                                                     pl.BlockSpec((ns, d, n4), lambda i: (0, 0, 0))]
    args = list(dparts) + [xin, sc1, w]
    out_specs, out_shape = [vec_spec, vec_spec], [jax.ShapeDtypeStruct((1, d), F32)] * 2
    if has_res:
        in_specs.append(row_spec)
        args.append(dxres)
        out_specs = [row_spec] + out_specs
        out_shape = [jax.ShapeDtypeStruct((rows, d), F32)] + out_shape
    return _call(body, name=name, grid=(rows // tm,), in_specs=in_specs, out_specs=out_specs, out_shape=out_shape)(*args)


def _inproj_bwd_w(xin, sc1, sh, dparts, init, gbuf, name):
    rows, d = xin.shape
    npart = len(dparts)
    e = dparts[0].shape[1]
    n4 = e // 2
    ns = 2 * npart
    tm = _row_tile(rows, 1024)
    nt = rows // tm
    has_init = init is not None
    into = gbuf is not None
    assert not into or (ns == N_CHIPS and gbuf.shape[2] == n4)

    def body(*refs):
        x_ref, sc_ref, sh_ref = refs[:3]
        dp = refs[3:3 + npart]
        init_ref = refs[3 + npart] if has_init else None
        o_ref = refs[-1]
        s, i = pl.program_id(0), pl.program_id(1)
        h = (x_ref[...] * sc_ref[...] + sh_ref[...]).astype(BF16)

        @pl.when(i == 0)
        def _():
            o_ref[...] = init_ref[...] if has_init else jnp.zeros_like(o_ref)

        for p in range(npart):
            @pl.when(s // 2 == p)
            def _(p=p):
                o_ref[...] += _dot_tn(h, dp[p][...])

    in_specs = [pl.BlockSpec((tm, d), lambda s, i: (i, 0)),
                pl.BlockSpec((1, d), lambda s, i: (0, 0)),
                pl.BlockSpec((1, d), lambda s, i: (0, 0))]
    in_specs += [pl.BlockSpec((tm, n4), lambda s, i: (i, s % 2))] * npart
    args = [xin, sc1, sh] + list(dparts)
    o_spec = pl.BlockSpec((None, d, n4), lambda s, i: (s, 0, 0))
    if has_init:
        in_specs.append(o_spec)
        args.append(init)
    extra = {}
    if into:
        in_specs.append(ANY_SPEC)
        args.append(gbuf)
        extra = dict(input_output_aliases={len(args) - 1: 0})
    out_shape = jax.ShapeDtypeStruct(gbuf.shape if into else (ns, d, n4), F32)
    return _call(body, name=name, grid=(ns, nt), in_specs=in_specs, out_specs=o_spec, out_shape=out_shape, **extra)(*args)


def _gated(y_ref, g_ref, nch):
    return jnp.concatenate([(y_ref[k].astype(F32) * _silu(g_ref[k].astype(F32))).astype(BF16) for k in range(nch)], axis=1)


def _ln_stats(r):
    mu = jnp.mean(r, axis=-1, keepdims=True)
    var = jnp.mean(jnp.square(r - mu), axis=-1, keepdims=True)
    rstd = lax.rsqrt(var + LN_EPS)
    return (r - mu) * rstd, rstd


def _outproj_fwd(y, ug, xin, gt, wout, lg, lb, target, name):
    nch, rows, _ = y.shape
    e, d = wout.shape
    tm = _row_tile(rows, 512)
    with_loss = target is not None

    def body(*refs):
        y_ref, g_ref, x_ref, gt_ref, w_ref, lg_ref, lb_ref = refs[:7]
        if with_loss:
            t_ref, br_ref, dxo_ref, loss_ref = refs[7:]
        else:
            br_ref, xo_ref = refs[7:]
        z = _gated(y_ref, g_ref, nch)
        br_ref[...] = _dot(z, w_ref[...])
        if with_loss:
            @pl.when(pl.program_id(0) == 0)
            def _():
                loss_ref[...] = jnp.zeros_like(loss_ref)

        def norm(j, c):
            rows = pl.ds(pl.multiple_of(j * LN_ROWS, LN_ROWS), LN_ROWS)
            xhat, _ = _ln_stats(ALPHA * x_ref[rows, :] + gt_ref[...] * br_ref[rows, :])
            xo = xhat * lg_ref[...] + lb_ref[...]
            if with_loss:
                err = xo - t_ref[rows, :]
                dxo_ref[rows, :] = err * (1.0 / d)
                col = jnp.sum(err * err, axis=0, keepdims=True)
                loss_ref[...] += sum(col[:, k * LANE:(k + 1) * LANE] for k in range(d // LANE))
            else:
                xo_ref[rows, :] = xo
            return c

        lax.fori_loop(0, tm // LN_ROWS, norm, 0)

    chunk_spec = pl.BlockSpec((nch, tm, LANE), lambda i: (0, i, 0))
    g_spec = chunk_spec
    row_spec = pl.BlockSpec((tm, d), lambda i: (i, 0))
    vec_spec = pl.BlockSpec((1, d), lambda i: (0, 0))
    in_specs = [chunk_spec, g_spec, row_spec, vec_spec, pl.BlockSpec((e, d), lambda i: (0, 0)), vec_spec, vec_spec]
    args = [y, ug, xin, gt, wout, lg, lb]
    out_specs = [row_spec, row_spec]
    out_shape = [jax.ShapeDtypeStruct((rows, d), F32)] * 2
    if with_loss:
        in_specs.append(row_spec)
        args.append(target)
        out_specs.append(pl.BlockSpec((1, LANE), lambda i: (0, 0)))
        out_shape.append(jax.ShapeDtypeStruct((1, LANE), F32))
    return _call(body, name=name, grid=(rows // tm,), in_specs=in_specs, out_specs=out_specs, out_shape=out_shape)(*args)


def _outproj_bwd(dxo, xin, br, y, ug, gt, lg, wout, name):
    nch, rows, _ = y.shape
    e, d = wout.shape
    tm = _row_tile(rows, 256)

    def body(dxo_ref, x_ref, br_ref, y_ref, g_ref, gt_ref, lg_ref, w_ref,
             dy_ref, dg_ref, dxres_ref, dbr_ref, dlg_ref, dlb_ref, dgt_ref):
        @pl.when(pl.program_id(0) == 0)
        def _():
            dlg_ref[...] = jnp.zeros_like(dlg_ref)
            dlb_ref[...] = jnp.zeros_like(dlb_ref)
            dgt_ref[...] = jnp.zeros_like(dgt_ref)

        def norm_bwd(j, c):
            rows = pl.ds(pl.multiple_of(j * LN_ROWS, LN_ROWS), LN_ROWS)
            dxo_v = dxo_ref[rows, :]
            brv = br_ref[rows, :]
            xhat, rstd = _ln_stats(ALPHA * x_ref[rows, :] + gt_ref[...] * brv)
            dxh = dxo_v * lg_ref[...]
            dr = rstd * (dxh - jnp.mean(dxh, axis=-1, keepdims=True) - xhat * jnp.mean(dxh * xhat, axis=-1, keepdims=True))
            dlg_ref[...] += jnp.sum(dxo_v * xhat, axis=0, keepdims=True)
            dlb_ref[...] += jnp.sum(dxo_v, axis=0, keepdims=True)
            dgt_ref[...] += jnp.sum(dr * brv, axis=0, keepdims=True)
            dxres_ref[rows, :] = ALPHA * dr
            dbr_ref[rows, :] = (gt_ref[...] * dr).astype(BF16)
            return c

        lax.fori_loop(0, tm // LN_ROWS, norm_bwd, 0)
        dz = _dot_nt(dbr_ref[...], w_ref[...])
        for k in range(nch):
            dzk = dz[:, k * LANE:(k + 1) * LANE]
            gk = g_ref[k].astype(F32)
            dy_ref[k] = dzk * _silu(gk)
            dg_ref[:, k * LANE:(k + 1) * LANE] = (dzk * y_ref[k].astype(F32) * _dsilu(gk)).astype(BF16)

    chunk_spec = pl.BlockSpec((nch, tm, LANE), lambda i: (0, i, 0))
    g_spec = chunk_spec
    row_spec = pl.BlockSpec((tm, d), lambda i: (i, 0))
    vec_spec = pl.BlockSpec((1, d), lambda i: (0, 0))
    return _call(
        body, name=name, grid=(rows // tm,),
        in_specs=[row_spec, row_spec, row_spec, chunk_spec, g_spec, vec_spec, vec_spec, pl.BlockSpec((e, d), lambda i: (0, 0))],
        out_specs=[chunk_spec, pl.BlockSpec((tm, e), lambda i: (i, 0)), row_spec, row_spec, vec_spec, vec_spec, vec_spec],
        out_shape=[jax.ShapeDtypeStruct((nch, rows, LANE), F32), jax.ShapeDtypeStruct((rows, e), BF16),
                   jax.ShapeDtypeStruct((rows, d), F32), jax.ShapeDtypeStruct((rows, d), BF16)]
        + [jax.ShapeDtypeStruct((1, d), F32)] * 3,
    )(dxo, xin, br, y, ug, gt, lg, wout)


def _outproj_bwd_w(y, ug, dbr, gbuf, row0, name):
    nch, rows, _ = y.shape
    d = dbr.shape[1]
    e = nch * LANE
    es = e // N_CHIPS
    tm = _row_tile(rows, 512)
    assert gbuf.shape[2] == d and row0 % es == 0

    def body(y_ref, g_ref, dbr_ref, buf_ref, o_ref):
        @pl.when(pl.program_id(0) == 0)
        def _():
            o_ref[...] = jnp.zeros_like(o_ref)

        z = _gated(y_ref, g_ref, nch)
        o_ref[...] += _dot_tn(z, dbr_ref[...]).reshape(N_CHIPS, es, d)

    return _call(
        body, name=name, grid=(rows // tm,),
        in_specs=[pl.BlockSpec((nch, tm, LANE), lambda i: (0, i, 0)),
                  pl.BlockSpec((nch, tm, LANE), lambda i: (0, i, 0)),
                  pl.BlockSpec((tm, d), lambda i: (i, 0)),
                  ANY_SPEC],
        out_specs=pl.BlockSpec((N_CHIPS, es, d), lambda i: (0, row0 // es, 0)),
        out_shape=jax.ShapeDtypeStruct(gbuf.shape, F32),
        input_output_aliases={3: 0},
    )(y, ug, dbr, gbuf)


def _scan(a_ref, b_ref, h_ref, *, length, init, reverse, a_shift, store):
    nblk = length // SUB
    unroll = min(SCAN_UNROLL, nblk)
    assert nblk % unroll == 0
    row = lax.broadcasted_iota(jnp.int32, (SUB, LANE), 0)
    last = 0 if reverse else SUB - 1
    edges = [(row >= SUB - k) if reverse else (row < k) for k in (1, 2, 4)]

    def local_scan(a, b):
        for k, edge in zip((1, 2, 4), edges):
            sh = (SUB - k) if reverse else k
            b = b + a * jnp.where(edge, 0.0, pltpu.roll(b, sh, 0))
            a = a * jnp.where(edge, 1.0, pltpu.roll(a, sh, 0))
        return a, b

    def step(i, carry):
        base = pl.multiple_of(((nblk // unroll - 1 - i) if reverse else i) * (unroll * SUB), unroll * SUB)
        order = range(unroll - 1, -1, -1) if reverse else range(unroll)
        loaded = [(a_ref[pl.ds(PAD + base + j * SUB + a_shift, SUB), :], b_ref[pl.ds(PAD + base + j * SUB, SUB), :])
                  for j in order]
        scanned = [local_scan(a, b) for a, b in loaded]
        for j, (a, b) in zip(order, scanned):
            if store:
                h_ref[pl.ds(PAD + base + j * SUB, SUB), :] = b + a * carry
            a_l = jnp.broadcast_to(a[last:last + 1, :], (SUB, LANE))
            b_l = jnp.broadcast_to(b[last:last + 1, :], (SUB, LANE))
            carry = b_l + a_l * carry
        return carry

    carry = lax.fori_loop(0, nblk // unroll, step, jnp.broadcast_to(init, (SUB, LANE)))
    return carry[0:1, :]


def _conv_fwd(src_ref, upad, u_ref, cw, cb, length):
    zeros = jnp.zeros((PAD, LANE), F32)
    upad[pl.ds(0, PAD), :] = zeros
    upad[pl.ds(PAD + length, PAD), :] = zeros
    rt = _row_tile(length, ROW_TILE)

    def copy(i, c):
        t0 = pl.multiple_of(i * rt, rt)
        upad[pl.ds(PAD + t0, rt), :] = src_ref[pl.ds(t0, rt), :]
        return c

    lax.fori_loop(0, length // rt, copy, 0)

    def tile(i, c):
        t0 = pl.multiple_of(i * rt, rt)
        acc = jnp.zeros((rt, LANE), F32)
        for k in range(CONV_TAPS):
            acc = acc + upad[pl.ds(t0 + PAD - CONV_LEFT + k, rt), :] * cw[k:k + 1, :]
        u_ref[pl.ds(t0, rt), :] = acc + cb
        return c

    lax.fori_loop(0, length // rt, tile, 0)


def _gates_fwd(u_ref, a_ref, b_ref, wa, wx, ba, bx, ls, length, keep=None):
    rt = _row_tile(length, GATE_TILE)

    def tile(i, c):
        t0 = pl.multiple_of(i * rt, rt)
        ut = u_ref[pl.ds(t0, rt), :]
        ub = ut.astype(BF16)
        r = _sigmoid(_dot(ub, wa) + ba)
        ig = _sigmoid(_dot(ub, wx) + bx)
        if keep is not None:
            keep[0][pl.ds(t0, rt), :] = r
            keep[1][pl.ds(t0, rt), :] = ig
        la = (LRU_C * r) * ls
        a = jnp.exp(la)
        a_ref[pl.ds(PAD + t0, rt), :] = a
        b_ref[pl.ds(PAD + t0, rt), :] = jnp.sqrt(_one_minus_sq(la, a)) * (ig * ut)
        return c

    lax.fori_loop(0, length // rt, tile, 0, unroll=min(GATE_UNROLL, length // rt))


def _lru_specs():
    return [pl.BlockSpec((CONV_TAPS, LANE), lambda n: (0, n)),
            pl.BlockSpec((1, LANE), lambda n: (0, n)),
            pl.BlockSpec((2, None, LANE, LANE), lambda n: (0, n, 0, 0)),
            pl.BlockSpec((2, None, LANE, LANE), lambda n: (0, n, 0, 0)),
            pl.BlockSpec((2, LANE), lambda n: (0, n)),
            pl.BlockSpec((2, LANE), lambda n: (0, n)),
            pl.BlockSpec((2, LANE), lambda n: (0, n))]


def _rglru_fwd(ug, uc, conv_w, conv_b, wa, wx, ba, bx, lam):
    nb = uc.shape[0]
    s_len, t_len = ug.shape[1], uc.shape[1]

    def body(u0_ref, uc0_ref, cw_ref, cb_ref, wa_ref, wx_ref, ba_ref, bx_ref, lam_ref, y_ref,
             upad, ubuf, abuf, hbuf):
        cw, cb = cw_ref[...], cb_ref[...]
        lsig = _log_sigmoid(lam_ref[...])
        zero = jnp.zeros((1, LANE), F32)
        _conv_fwd(uc0_ref, upad, ubuf, cw, cb, t_len)
        h0 = []
        for dr in range(2):
            _gates_fwd(ubuf, abuf, hbuf, wa_ref[dr], wx_ref[dr], ba_ref[dr:dr + 1, :], bx_ref[dr:dr + 1, :],
                       lsig[dr:dr + 1, :], t_len)
            h0.append(_scan(abuf, hbuf, hbuf, length=t_len, init=zero, reverse=(dr == 1), a_shift=0, store=False))
        _conv_fwd(u0_ref, upad, ubuf, cw, cb, s_len)
        rt = _row_tile(s_len, ROW_TILE)
        for dr in range(2):
            _gates_fwd(ubuf, abuf, hbuf, wa_ref[dr], wx_ref[dr], ba_ref[dr:dr + 1, :], bx_ref[dr:dr + 1, :],
                       lsig[dr:dr + 1, :], s_len)
            _scan(abuf, hbuf, hbuf, length=s_len, init=h0[dr], reverse=(dr == 1), a_shift=0, store=True)

            def acc(i, c, dr=dr):
                t0 = pl.multiple_of(i * rt, rt)
                h = hbuf[pl.ds(PAD + t0, rt), :]
                if dr == 0:
                    upad[pl.ds(PAD + t0, rt), :] = h
                else:
                    y_ref[pl.ds(t0, rt), :] = (upad[pl.ds(PAD + t0, rt), :] + h).astype(y_ref.dtype)
                return c

            lax.fori_loop(0, s_len // rt, acc, 0)

    seq = pltpu.VMEM((s_len + 2 * PAD, LANE), F32)
    return _call(
        body, name="rglru_fwd", grid=(nb,),
        in_specs=[pl.BlockSpec((None, s_len, LANE), lambda n: (n, 0, 0)),
                  pl.BlockSpec((None, t_len, LANE), lambda n: (n, 0, 0))] + _lru_specs(),
        out_specs=pl.BlockSpec((None, s_len, LANE), lambda n: (n, 0, 0)),
        out_shape=jax.ShapeDtypeStruct((nb, s_len, LANE), BF16),
        scratch_shapes=[seq, pltpu.VMEM((s_len, LANE), F32), seq, seq],
    )(ug, uc, conv_w, conv_b, wa, wx, ba, bx, lam)


def _rglru_bwd(ug, uc, dy, conv_w, conv_b, wa, wx, ba, bx, lam):
    nb = uc.shape[0]
    e = nb * LANE
    s_len, t_len = ug.shape[1], uc.shape[1]

    def body(u0_ref, uc0_ref, dy_ref, cw_ref, cb_ref, wa_ref, wx_ref, ba_ref, bx_ref, lam_ref,
             du_ref, duc_ref, dcw_ref, dcb_ref, dwa_ref, dwx_ref, dba_ref, dbx_ref, dlam_ref,
             upad, ubuf, abuf, hbuf, lbuf, dubuf, rbuf, ibuf, cpad, cu, ca0, ch0, ca1, ch1, cr0, ci0, cr1, ci1):
        cw, cb = cw_ref[...], cb_ref[...]
        lam_v = lam_ref[...]
        lsig = _log_sigmoid(lam_v)
        zero = jnp.zeros((1, LANE), F32)
        zpad = jnp.zeros((PAD, LANE), F32)
        for ref in (dcw_ref, dcb_ref, dwa_ref, dwx_ref, dba_ref, dbx_ref, dlam_ref):
            ref[...] = jnp.zeros_like(ref)

        def params(dr):
            return (wa_ref[dr], wx_ref[dr], ba_ref[dr:dr + 1, :], bx_ref[dr:dr + 1, :], lsig[dr:dr + 1, :])

        def direction_bwd(dr, u_ref, a_ref, h_ref, l_ref, gates, dub, length, first):
            wa_d, wx_d, ba_d, bx_d, ls_d = params(dr)
            rt = _row_tile(length, GATE_TILE)
            prev = 1 if dr == 1 else -1

            def tile(i, c):
                t0 = pl.multiple_of(i * rt, rt)
                ut = u_ref[pl.ds(t0, rt), :]
                ub = ut.astype(BF16)
                r = gates[0][pl.ds(t0, rt), :]
                ig = gates[1][pl.ds(t0, rt), :]
                la = (LRU_C * r) * ls_d
                a = a_ref[pl.ds(PAD + t0, rt), :]
                q = _one_minus_sq(la, a)
                rs = lax.rsqrt(q)
                sq = q * rs
                lm = l_ref[pl.ds(PAD + t0, rt), :]
                da = lm * h_ref[pl.ds(PAD + t0 + prev, rt), :]
                dsq = lm * ig * ut
                dig = lm * sq * ut
                dla = da * a - dsq * (a * a) * rs
                dr_ = dla * (LRU_C * ls_d)
                dlam_ref[dr:dr + 1, :] += jnp.sum(dla * (LRU_C * r), axis=0, keepdims=True)
                dpr = dr_ * r * (1.0 - r)
                dpi = dig * ig * (1.0 - ig)
                dba_ref[dr:dr + 1, :] += jnp.sum(dpr, axis=0, keepdims=True)
                dbx_ref[dr:dr + 1, :] += jnp.sum(dpi, axis=0, keepdims=True)
                dprb, dpib = dpr.astype(BF16), dpi.astype(BF16)
                dwa_ref[dr] += _dot_tn(ub, dprb)
                dwx_ref[dr] += _dot_tn(ub, dpib)
                dut = lm * sq * ig + _dot_nt(dprb, wa_d) + _dot_nt(dpib, wx_d)
                if first:
                    dub[pl.ds(PAD + t0, rt), :] = dut
                else:
                    dub[pl.ds(PAD + t0, rt), :] += dut
                return c

            lax.fori_loop(0, length // rt, tile, 0, unroll=min(GATE_UNROLL, length // rt))

        def conv_bwd(dub, src_pad, out_ref, length):
            rt = _row_tile(length, ROW_TILE)

            def tile(i, c):
                t0 = pl.multiple_of(i * rt, rt)
                dut = dub[pl.ds(PAD + t0, rt), :]
                dcb_ref[...] += jnp.sum(dut, axis=0, keepdims=True)
                acc = jnp.zeros((rt, LANE), F32)
                for k in range(CONV_TAPS):
                    sh = CONV_LEFT - k
                    acc = acc + dub[pl.ds(PAD + t0 + sh, rt), :] * cw[k:k + 1, :]
                    dcw_ref[k:k + 1, :] += jnp.sum(dut * src_pad[pl.ds(PAD + t0 - sh, rt), :], axis=0, keepdims=True)
                out_ref[pl.ds(t0, rt), :] = acc.astype(out_ref.dtype)
                return c

            lax.fori_loop(0, length // rt, tile, 0)

        _conv_fwd(uc0_ref, cpad, cu, cw, cb, t_len)
        cbufs = ((ca0, ch0), (ca1, ch1))
        cgates = ((cr0, ci0), (cr1, ci1))
        h0 = []
        for dr in range(2):
            ca, chh = cbufs[dr]
            _gates_fwd(cu, ca, chh, *params(dr), t_len, keep=cgates[dr])
            h0.append(_scan(ca, chh, chh, length=t_len, init=zero, reverse=(dr == 1), a_shift=0, store=True))
        _conv_fwd(u0_ref, upad, ubuf, cw, cb, s_len)
        rt = _row_tile(s_len, ROW_TILE)
        dh0 = []
        for dr in range(2):
            rev = dr == 1
            _gates_fwd(ubuf, abuf, hbuf, *params(dr), s_len, keep=(rbuf, ibuf))
            _scan(abuf, hbuf, hbuf, length=s_len, init=h0[dr], reverse=rev, a_shift=0, store=True)
            first_row = PAD + s_len if rev else PAD - 1
            hbuf[pl.ds(first_row, 1), :] = h0[dr]
            end_row = PAD - 1 if rev else PAD + s_len
            abuf[pl.ds(end_row, 1), :] = zero

            def copy(i, c):
                t0 = pl.multiple_of(i * rt, rt)
                lbuf[pl.ds(PAD + t0, rt), :] = dy_ref[pl.ds(t0, rt), :]
                return c

            lax.fori_loop(0, s_len // rt, copy, 0)
            _scan(abuf, lbuf, lbuf, length=s_len, init=zero, reverse=not rev, a_shift=(-1 if rev else 1), store=True)
            start = PAD + s_len - 1 if rev else PAD
            dh0.append(abuf[pl.ds(start, 1), :] * lbuf[pl.ds(start, 1), :])
            direction_bwd(dr, ubuf, abuf, hbuf, lbuf, (rbuf, ibuf), dubuf, s_len, first=(dr == 0))
        dubuf[pl.ds(0, PAD), :] = zpad
        dubuf[pl.ds(PAD + s_len, PAD), :] = zpad
        conv_bwd(dubuf, upad, du_ref, s_len)
        lc = lbuf
        duc_buf = dubuf
        for dr in range(2):
            rev = dr == 1
            ca, chh = cbufs[dr]
            first_row = PAD + t_len if rev else PAD - 1
            chh[pl.ds(first_row, 1), :] = zero
            end_row = PAD - 1 if rev else PAD + t_len
            ca[pl.ds(end_row, 1), :] = zero + 1.0
            rtc = _row_tile(t_len, ROW_TILE)

            def clear(i, c):
                t0 = pl.multiple_of(i * rtc, rtc)
                lc[pl.ds(PAD + t0, rtc), :] = jnp.zeros((rtc, LANE), F32)
                return c

            lax.fori_loop(0, t_len // rtc, clear, 0)
            _scan(ca, lc, lc, length=t_len, init=dh0[dr], reverse=not rev, a_shift=(-1 if rev else 1), store=True)
            direction_bwd(dr, cu, ca, chh, lc, cgates[dr], duc_buf, t_len, first=(dr == 0))
        duc_buf[pl.ds(0, PAD), :] = zpad
        duc_buf[pl.ds(PAD + t_len, PAD), :] = zpad
        conv_bwd(duc_buf, cpad, duc_ref, t_len)
        dlam_ref[...] = dlam_ref[...] * (1.0 - _sigmoid(lam_v))

    seq = pltpu.VMEM((s_len + 2 * PAD, LANE), F32)
    cseq = pltpu.VMEM((t_len + 2 * PAD, LANE), F32)
    flat = pltpu.VMEM((s_len, LANE), F32)
    cflat = pltpu.VMEM((t_len, LANE), F32)
    vec2 = pl.BlockSpec((2, LANE), lambda n: (0, n))
    wspec = pl.BlockSpec((2, None, LANE, LANE), lambda n: (0, n, 0, 0))
    return _call(
        body, name="rglru_bwd", grid=(nb,),
        in_specs=[pl.BlockSpec((None, s_len, LANE), lambda n: (n, 0, 0)),
                  pl.BlockSpec((None, t_len, LANE), lambda n: (n, 0, 0)),
                  pl.BlockSpec((None, s_len, LANE), lambda n: (n, 0, 0))] + _lru_specs(),
        out_specs=[pl.BlockSpec((s_len, LANE), lambda n: (0, n)),
                   pl.BlockSpec((t_len, LANE), lambda n: (0, n)),
                   pl.BlockSpec((CONV_TAPS, LANE), lambda n: (0, n)),
                   pl.BlockSpec((1, LANE), lambda n: (0, n)),
                   wspec, wspec, vec2, vec2, vec2],
        out_shape=[jax.ShapeDtypeStruct((s_len, e), BF16), jax.ShapeDtypeStruct((t_len, e), BF16),
                   jax.ShapeDtypeStruct((CONV_TAPS, e), F32), jax.ShapeDtypeStruct((1, e), F32),
                   jax.ShapeDtypeStruct((2, nb, LANE, LANE), F32), jax.ShapeDtypeStruct((2, nb, LANE, LANE), F32),
                   jax.ShapeDtypeStruct((2, e), F32), jax.ShapeDtypeStruct((2, e), F32), jax.ShapeDtypeStruct((2, e), F32)],
        scratch_shapes=[seq, flat, seq, seq, seq, seq, flat, flat,
                        cseq, cflat, cseq, cseq, cseq, cseq, cflat, cflat, cflat, cflat],
    )(ug, uc, dy, conv_w, conv_b, wa, wx, ba, bx, lam)


def _pool_windows(src_ref, out_ref, colbuf, rowbuf, half, transpose, s_len):
    gw = GRID_W
    lg = gw.bit_length() - 1
    n_rows = s_len // gw
    cp, rm = POOL_CPAD, 8 * gw
    stride = gw + 2 * cp
    rt = _row_tile(s_len, ROW_TILE)
    assert rt % gw == 0 and half <= cp
    gpt = rt // gw
    offs = range(-half, half)
    zmargin = jnp.zeros((cp, LANE), F32)

    def zcol(r, c):
        base = pl.multiple_of(r * stride, SUB)
        colbuf[pl.ds(base, cp), :] = zmargin
        colbuf[pl.ds(base + cp + gw, cp), :] = zmargin
        return c

    lax.fori_loop(0, n_rows, zcol, 0)

    def zrow(i, c):
        t0 = pl.multiple_of(i * gw, gw)
        rowbuf[pl.ds(t0, gw), :] = jnp.zeros((gw, LANE), F32)
        rowbuf[pl.ds(rm + s_len + t0, gw), :] = jnp.zeros((gw, LANE), F32)
        return c

    lax.fori_loop(0, rm // gw, zrow, 0)

    col = lax.broadcasted_iota(jnp.int32, (gw, LANE), 0)
    ccnt = (jnp.minimum(col + half, gw) - jnp.maximum(col - half, 0)).astype(F32)

    def row_counts(t0):
        row = (t0 + lax.broadcasted_iota(jnp.int32, (rt, LANE), 0)) >> lg
        return (jnp.minimum(row + half, n_rows) - jnp.maximum(row - half, 0)).astype(F32)

    def col_base(t0, g):
        return pl.multiple_of((t0 // gw) * stride, SUB) + g * stride + cp

    def col_sum(t0, g, sign):
        acc = jnp.zeros((gw, LANE), F32)
        for o in offs:
            acc = acc + colbuf[pl.ds(col_base(t0, g) + sign * o, gw), :]
        return acc

    def row_sum(t0, sign):
        acc = jnp.zeros((rt, LANE), F32)
        for o in offs:
            acc = acc + rowbuf[pl.ds(rm + t0 + sign * o * gw, rt), :]
        return acc

    n_tiles = s_len // rt
    assert rt >= half * gw

    def loop(fn, edges=False):
        def step(i, c):
            t0 = pl.multiple_of(i * rt, rt)
            fn(t0, False) if edges else fn(t0)
            return c
        if edges:
            fn(0, True)
            if n_tiles > 1:
                fn(s_len - rt, True)
            lax.fori_loop(1, n_tiles - 1, step, 0)
        else:
            lax.fori_loop(0, n_tiles, step, 0)

    inv_ccnt = 1.0 / ccnt

    def by_row_count(v, t0, edge):
        return v / row_counts(t0) if edge else v * (1.0 / (2 * half))

    if not transpose:
        def fill(t0):
            for g in range(gpt):
                colbuf[pl.ds(col_base(t0, g), gw), :] = src_ref[pl.ds(t0 + g * gw, gw), :]

        def cols(t0):
            for g in range(gpt):
                rowbuf[pl.ds(rm + t0 + g * gw, gw), :] = col_sum(t0, g, 1) * inv_ccnt

        def rows(t0, edge):
            mean = by_row_count(row_sum(t0, 1), t0, edge)
            out_ref[pl.ds(t0, rt), :] = (mean - src_ref[pl.ds(t0, rt), :]).astype(out_ref.dtype)

        loop(fill)
        loop(cols)
        loop(rows, edges=True)
    else:
        def fill(t0, edge):
            rowbuf[pl.ds(rm + t0, rt), :] = by_row_count(src_ref[pl.ds(t0, rt), :], t0, edge)

        def rows(t0):
            acc = row_sum(t0, -1)
            for g in range(gpt):
                colbuf[pl.ds(col_base(t0, g), gw), :] = acc[g * gw:(g + 1) * gw, :] * inv_ccnt

        def cols(t0):
            for g in range(gpt):
                rows_g = pl.ds(t0 + g * gw, gw)
                out_ref[rows_g, :] = (col_sum(t0, g, -1) - src_ref[rows_g, :]).astype(out_ref.dtype)

        loop(fill, edges=True)
        loop(rows)
        loop(cols)


def _pool_map(src, nb, transpose, out_chunk_major, name):
    s_len = src.shape[1]
    cpg = nb // len(POOL_WINDOWS)

    def body(src_ref, out_ref, colbuf, rowbuf):
        n = pl.program_id(0)
        for gi, w in enumerate(POOL_WINDOWS):
            @pl.when(n // cpg == gi)
            def _(w=w):
                _pool_windows(src_ref, out_ref, colbuf, rowbuf, w // 2, transpose, s_len)

    if out_chunk_major:
        out_spec = pl.BlockSpec((None, s_len, LANE), lambda n: (n, 0, 0))
        out_shape = jax.ShapeDtypeStruct((nb, s_len, LANE), BF16)
    else:
        out_spec = pl.BlockSpec((s_len, LANE), lambda n: (0, n))
        out_shape = jax.ShapeDtypeStruct((s_len, nb * LANE), BF16)
    return _call(
        body, name=name, grid=(nb,),
        in_specs=[pl.BlockSpec((None, s_len, LANE), lambda n: (n, 0, 0))],
        out_specs=out_spec, out_shape=out_shape,
        scratch_shapes=[pltpu.VMEM((s_len // GRID_W * (GRID_W + 2 * POOL_CPAD), LANE), F32),
                        pltpu.VMEM((s_len + 16 * GRID_W, LANE), F32)],
    )(src)


def _group_weight(w_ref):
    return jnp.concatenate([w_ref[k] for k in range(N_CHIPS)], axis=0)


def _pool_mm_fwd(dm, wp, scale):
    nb, rows, _ = dm.shape
    _, ng, pq, pg = wp.shape
    cpg = pg // LANE
    tm = _row_tile(rows, 2048)

    def body(d_ref, w_ref, s_ref, y_ref):
        _put_chunks(y_ref, _dot(_cat(d_ref, cpg), _group_weight(w_ref)) * s_ref[...], cpg)

    cspec = pl.BlockSpec((cpg, tm, LANE), lambda i, g: (g, i, 0))
    return _call(
        body, name="pool_mm_fwd", grid=(rows // tm, ng),
        in_specs=[cspec, pl.BlockSpec((N_CHIPS, None, pq, pg), lambda i, g: (0, g, 0, 0)),
                  pl.BlockSpec((1, pg), lambda i, g: (0, g))],
        out_specs=cspec, out_shape=jax.ShapeDtypeStruct((nb, rows, LANE), BF16),
    )(dm, wp, scale)


def _pool_mm_bwd(dy, dm, wp, scale, gbuf, row0):
    nb, rows, _ = dm.shape
    _, ng, pq, pg = wp.shape
    cpg = pg // LANE
    tm = _row_tile(rows, 1024)
    nt = rows // tm
    assert gbuf.shape[2] == 2 * pg and row0 % pq == 0

    def body(dy_ref, d_ref, w_ref, s_ref, buf_ref, dd_ref, dwp_ref, dsc_ref, acc):
        i = pl.program_id(1)

        @pl.when(i == 0)
        def _():
            acc[...] = jnp.zeros_like(acc)
            dsc_ref[...] = jnp.zeros_like(dsc_ref)

        dyv = _cat(dy_ref, cpg)
        dc = _cat(d_ref, cpg)
        w = _group_weight(w_ref)
        dsc_ref[...] += jnp.sum(dyv * _dot(dc, w), axis=0, keepdims=True)
        dyp = (dyv * s_ref[...]).astype(BF16)
        _put_chunks(dd_ref, _dot_nt(dyp, w), cpg)
        acc[...] += _dot_tn(dc, dyp)

        @pl.when(i == nt - 1)
        def _():
            dwp_ref[...] = acc[...].reshape(N_CHIPS, pq, pg)

    cspec = pl.BlockSpec((cpg, tm, LANE), lambda g, i: (g, i, 0))
    sspec = pl.BlockSpec((1, pg), lambda g, i: (0, g))
    return _call(
        body, name="pool_mm_bwd", grid=(ng, nt),
        in_specs=[cspec, cspec, pl.BlockSpec((N_CHIPS, None, pq, pg), lambda g, i: (0, g, 0, 0)), sspec, ANY_SPEC],
        out_specs=[cspec, pl.BlockSpec((N_CHIPS, pq, pg), lambda g, i: (0, row0 // pq + g // 2, g % 2)), sspec],
        out_shape=[jax.ShapeDtypeStruct((nb, rows, LANE), F32), jax.ShapeDtypeStruct(gbuf.shape, F32),
                   jax.ShapeDtypeStruct((1, ng * pg), F32)],
        scratch_shapes=[pltpu.VMEM((pg, pg), F32)],
        input_output_aliases={4: 1},
    )(dy, dm, wp, scale, gbuf)


def _adamw_math(w, g, m, v):
    nm = ADAM_B1 * m + (1.0 - ADAM_B1) * g
    nv = ADAM_B2 * v + (1.0 - ADAM_B2) * jnp.square(g)
    m_hat = nm / (1.0 - ADAM_B1 ** ADAM_STEP)
    v_hat = nv / (1.0 - ADAM_B2 ** ADAM_STEP)
    return -ADAM_LR * (m_hat / (jnp.sqrt(v_hat) + ADAM_EPS) + ADAM_WD * w), nm, nv


def _adamw_param(w3, m3, v3, gsrcs, pick, tm, name):
    n_blk, rows, cols = w3.shape
    ng = len(gsrcs)

    def body(*refs):
        w_ref, m_ref, v_ref = refs[:3]
        g_refs = refs[3:3 + ng]
        go_ref, d_ref, nm_ref, nv_ref = refs[3 + ng:]
        g = pick(pl.program_id(0), [r[...] for r in g_refs])
        go_ref[...] = g
        d_ref[...], nm_ref[...], nv_ref[...] = _adamw_math(w_ref[...], g, m_ref[...], v_ref[...])

    spec = pl.BlockSpec((None, tm, cols), lambda n, i: (n, i, 0))
    return _call(
        body, name=name, grid=(n_blk, rows // tm),
        in_specs=[spec] * 3 + [pl.BlockSpec(shape, imap) for _, shape, imap in gsrcs],
        out_specs=[spec] * 4, out_shape=[jax.ShapeDtypeStruct(w3.shape, F32)] * 4,
    )(w3, m3, v3, *[a for a, _, _ in gsrcs])


def _adamw_small(quads):
    n = len(quads)

    def body(*refs):
        ins, outs = refs[:4 * n], refs[4 * n:]
        for k in range(n):
            w, g, m, v = (r[...] for r in ins[4 * k:4 * k + 4])
            outs[3 * k][...], outs[3 * k + 1][...], outs[3 * k + 2][...] = _adamw_math(w, g, m, v)

    flat = [a for q in quads for a in q]
    res = _call(body, name="adamw_small", grid=(1,),
                in_specs=[pl.BlockSpec(a.shape, lambda i: (0, 0)) for a in flat],
                out_specs=[pl.BlockSpec(q[0].shape, lambda i: (0, 0)) for q in quads for _ in range(3)],
                out_shape=[jax.ShapeDtypeStruct(q[0].shape, F32) for q in quads for _ in range(3)])(*flat)
    return [tuple(res[3 * k:3 * k + 3]) for k in range(n)]


def _place():
    return lax.axis_index("x"), lax.axis_index("y"), lax.axis_index("c")


def _other_chips(x, y):
    return [(1 - x, y), (x, 1 - y), (1 - x, 1 - y)]


def _own_slab(a):
    x, y, _ = _place()
    return lax.dynamic_update_slice(lax.empty((N_CHIPS,) + a.shape, a.dtype), a[None], (2 * x + y, 0, 0))


def _gather_chips(arrays, name):
    n = len(arrays)
    halves = [a.shape[0] // 2 for a in arrays]
    for a, h in zip(arrays, halves):
        assert 2 * h == a.shape[0] and h % (32 // a.dtype.itemsize) == 0
    lands = [_own_slab(a) for a in arrays]

    def body(*refs):
        outs = refs[n:2 * n]
        send_sems, recv_sems = refs[2 * n:]
        x, y, c = _place()
        me = 2 * x + y
        chips = _other_chips(x, y)

        def mine(k):
            return pl.ds(c * halves[k], halves[k])

        def theirs(k):
            return pl.ds((1 - c) * halves[k], halves[k])

        def push(k, j, src, dst, to):
            return pltpu.make_async_remote_copy(src_ref=src, dst_ref=dst, send_sem=send_sems.at[6 * k + j],
                                                recv_sem=recv_sems.at[6 * k + j], device_id=to, device_id_type=MESH)

        started = []
        for j, (cx, cy) in enumerate(chips):
            for k in range(n):
                own = outs[k].at[me, mine(k)]
                cp = push(k, j, own, own, (cx, cy, c))
                cp.start()
                started.append(cp)
        for j, (cx, cy) in enumerate(chips):
            for k in range(n):
                slab = outs[k].at[2 * cx + cy, mine(k)]
                push(k, j, slab, slab, (x, y, c)).wait_recv()
                fwd = push(k, 3 + j, slab, slab, (x, y, 1 - c))
                fwd.start()
                started.append(fwd)
        for j, (cx, cy) in enumerate(chips):
            for k in range(n):
                slab = outs[k].at[2 * cx + cy, theirs(k)]
                push(k, 3 + j, slab, slab, (x, y, c)).wait_recv()
        for cp in started:
            cp.wait_send()

    return _call(
        body, name=name, in_specs=[ANY_SPEC] * n, out_specs=[ANY_SPEC] * n,
        out_shape=[jax.ShapeDtypeStruct(a.shape, a.dtype) for a in lands],
        input_output_aliases={k: k for k in range(n)},
        scratch_shapes=[pltpu.SemaphoreType.DMA((6 * n,)), pltpu.SemaphoreType.DMA((6 * n,))],
    )(*lands)


def _gather_devices(v):
    shape = v.shape

    def body(v_ref, out_ref, send_sems, recv_sems):
        x, y, c = _place()
        me = 4 * x + 2 * y + c
        out_ref[me] = v_ref[...]
        sends = []
        for k in range(1, N_DEV):
            to = (me + k) % N_DEV
            cp = pltpu.make_async_remote_copy(src_ref=v_ref, dst_ref=out_ref.at[me], send_sem=send_sems.at[k],
                                              recv_sem=recv_sems.at[k], device_id=(to // 4, (to // 2) % 2, to % 2),
                                              device_id_type=MESH)
            cp.start()
            sends.append(cp)
        for k in range(1, N_DEV):
            frm = (me + N_DEV - k) % N_DEV
            pltpu.make_async_remote_copy(src_ref=v_ref, dst_ref=out_ref.at[frm], send_sem=send_sems.at[k],
                                         recv_sem=recv_sems.at[k], device_id=(x, y, c), device_id_type=MESH).wait_recv()
        for cp in sends:
            cp.wait_send()

    vspec = pl.BlockSpec(memory_space=pltpu.VMEM)
    return _call(body, name="gather_devices", in_specs=[vspec], out_specs=vspec,
                 out_shape=jax.ShapeDtypeStruct((N_DEV,) + shape, F32),
                 scratch_shapes=[pltpu.SemaphoreType.DMA((N_DEV,)), pltpu.SemaphoreType.DMA((N_DEV,))])(v)


HBM_SPEC = pl.BlockSpec(memory_space=pltpu.HBM)
SEM_SPEC = pl.BlockSpec(memory_space=pltpu.SEMAPHORE)
SIDE_EFFECT = pltpu.SideEffectType.DATAFLOW_SIDE_EFFECTING


def _push_copies(src_refs, land_refs, send_sems, recv_sems, per_peer):
    x, y, c = _place()
    me = 2 * x + y
    copies = []
    for j, (cx, cy) in enumerate(_other_chips(x, y)):
        for k, (src, land) in enumerate(zip(src_refs, land_refs)):
            copies.append(pltpu.make_async_remote_copy(
                src_ref=src.at[2 * cx + cy] if per_peer else src, dst_ref=land.at[me], send_sem=send_sems.at[3 * k + j],
                recv_sem=recv_sems.at[3 * k + j], device_id=(cx, cy, c), device_id_type=MESH))
    return copies


def _push_start(srcs, lands, per_peer, after, name):
    n = len(srcs)

    def body(*refs):
        src_refs, land_refs = refs[:n], refs[n:2 * n]
        send_sems, recv_sems = refs[2 * n + 1], refs[2 * n + 2]
        token = refs[-1]
        for cp in _push_copies(src_refs, land_refs, send_sems, recv_sems, per_peer):
            cp.start()
        token[...] = jnp.zeros_like(token)

    bufs = [pltpu.with_memory_space_constraint(a, pltpu.HBM) for a in list(srcs) + list(lands)]
    res = _call(
        body, name=name,
        out_shape=[pltpu.SemaphoreType.DMA((3 * n,)), pltpu.SemaphoreType.DMA((3 * n,))]
        + [pltpu.HBM(a.shape, a.dtype) for a in bufs] + [jax.ShapeDtypeStruct((SUB, LANE), F32)],
        in_specs=[HBM_SPEC] * (2 * n) + [ANY_SPEC],
        out_specs=[SEM_SPEC, SEM_SPEC] + [HBM_SPEC] * (2 * n) + [pl.BlockSpec(memory_space=pltpu.VMEM)],
        input_output_aliases={i: 2 + i for i in range(2 * n)},
        compiler_params=pltpu.CompilerParams(has_side_effects=SIDE_EFFECT),
    )(*bufs, after)
    return res[0], res[1], list(res[2:2 + n]), list(res[2 + n:2 + 2 * n]), res[-1]


def _push_wait(send_sems, recv_sems, srcs, lands, per_peer, after, name):
    n = len(srcs)

    def body(*refs):
        src_refs, land_refs = refs[:n], refs[n:2 * n]
        send_sems, recv_sems = refs[2 * n], refs[2 * n + 1]
        for cp in _push_copies(src_refs, land_refs, send_sems, recv_sems, per_peer):
            cp.wait_send()
            cp.wait_recv()

    res = _call(
        body, name=name,
        out_shape=[pltpu.HBM(a.shape, a.dtype) for a in list(srcs) + list(lands)],
        in_specs=[HBM_SPEC] * (2 * n) + [SEM_SPEC, SEM_SPEC, ANY_SPEC],
        out_specs=[HBM_SPEC] * (2 * n),
        input_output_aliases={i: i for i in range(2 * n)},
        compiler_params=pltpu.CompilerParams(has_side_effects=SIDE_EFFECT),
    )(*srcs, *lands, send_sems, recv_sems, after)
    return list(res[n:])


def _sibling_swap(g):
    _, rows, w = g.shape
    half = rows // 2

    def body(g_ref, out_ref, send_sem, recv_sem):
        x, y, c = _place()
        cp = pltpu.make_async_remote_copy(src_ref=g_ref.at[:, pl.ds((1 - c) * half, half)], dst_ref=out_ref,
                                          send_sem=send_sem, recv_sem=recv_sem, device_id=(x, y, 1 - c), device_id_type=MESH)
        cp.start()
        cp.wait()

    return _call(body, name="rs_sibling_swap", in_specs=[ANY_SPEC], out_specs=ANY_SPEC,
                 out_shape=jax.ShapeDtypeStruct((N_CHIPS, half, w), F32),
                 scratch_shapes=[pltpu.SemaphoreType.DMA, pltpu.SemaphoreType.DMA])(g)


def _pair_add(g, got, place):
    _, rows, w = g.shape
    half = rows // 2
    tm = _row_tile(half, RS_TILE)
    nt = half // tm

    def body(p_ref, a_ref, b_ref, o_ref, own_ref):
        v = a_ref[...] + b_ref[...]
        o_ref[...] = v.astype(BF16)

        @pl.when(pl.program_id(1) == p_ref[1])
        def _():
            own_ref[...] = v

    return _call(
        body, name="rs_pair_add",
        grid_spec=pltpu.PrefetchScalarGridSpec(
            num_scalar_prefetch=1, grid=(nt, N_CHIPS),
            in_specs=[pl.BlockSpec((None, tm, w), lambda i, s, p: (s, p[0] * nt + i, 0)),
                      pl.BlockSpec((None, tm, w), lambda i, s, p: (s, i, 0))],
            out_specs=[pl.BlockSpec((None, tm, w), lambda i, s, p: (s, i, 0)),
                       pl.BlockSpec((tm, w), lambda i, s, p: (i, 0))]),
        out_shape=[jax.ShapeDtypeStruct((N_CHIPS, half, w), BF16), jax.ShapeDtypeStruct((half, w), F32)],
    )(place, g, got)


def _sum_chips(parts, own, place):
    _, half, w = parts.shape
    tm = _row_tile(half, RS_TILE)
    nt = half // tm

    def body(p_ref, parts_ref, own_ref, o_ref):
        me = p_ref[1]
        t = [jnp.where(me == q, own_ref[...], parts_ref[q].astype(F32)) for q in range(N_CHIPS)]
        o_ref[...] = (t[0] + t[1]) + (t[2] + t[3])

    return _call(
        body, name="rs_sum_chips",
        grid_spec=pltpu.PrefetchScalarGridSpec(
            num_scalar_prefetch=1, grid=(nt,),
            in_specs=[pl.BlockSpec((N_CHIPS, tm, w), lambda i, p: (0, i, 0)), pl.BlockSpec((tm, w), lambda i, p: (i, 0))],
            out_specs=pl.BlockSpec((tm, w), lambda i, p: (p[0] * nt + i, 0))),
        out_shape=jax.ShapeDtypeStruct((2 * half, w), F32),
    )(place, parts, own)


def _sibling_gather(red):
    rows, w = red.shape
    half = rows // 2

    def body(in_ref, out_ref, send_sem, recv_sem):
        x, y, c = _place()
        mine = out_ref.at[pl.ds(c * half, half)]
        cp = pltpu.make_async_remote_copy(src_ref=mine, dst_ref=mine, send_sem=send_sem, recv_sem=recv_sem,
                                          device_id=(x, y, 1 - c), device_id_type=MESH)
        cp.start()
        other = out_ref.at[pl.ds((1 - c) * half, half)]
        pltpu.make_async_remote_copy(src_ref=other, dst_ref=other, send_sem=send_sem, recv_sem=recv_sem,
                                     device_id=(x, y, c), device_id_type=MESH).wait_recv()
        cp.wait_send()

    return _call(body, name="rs_sibling_gather", in_specs=[ANY_SPEC], out_specs=ANY_SPEC,
                 out_shape=jax.ShapeDtypeStruct(red.shape, F32), input_output_aliases={0: 0},
                 scratch_shapes=[pltpu.SemaphoreType.DMA, pltpu.SemaphoreType.DMA])(red)


def _rs_begin(g, place, name):
    pair, own = _pair_add(g, _sibling_swap(g), place)
    send, recv, pair, parts, token = _push_start([pair], [jnp.zeros_like(pair)], True, own, name + "_start")
    return (send, recv, pair, parts, own), token


def _rs_end(state, place, after, name):
    send, recv, pair, parts, own = state
    (parts,) = _push_wait(send, recv, pair, parts, True, after, name + "_wait")
    return _sibling_gather(_sum_chips(parts, own, place))


WEIGHTS = ("c_ctx", "w_mod", "b_mod", "w_in", "w_out", "ln_g", "ln_b", "conv_w", "conv_b", "lru_wa", "lru_ba", "lru_wx",
           "lru_bx", "lru_lam", "pool_w", "pool_scale")
SMALL_GATHERED = ("conv_w", "lru_ba", "lru_bx", "lru_lam", "pool_scale")
SMALL_UPDATED = ("c_ctx", "b_mod", "ln_g", "ln_b", "conv_w", "conv_b", "lru_ba", "lru_bx", "lru_lam", "pool_scale")


def kernel(x, c, ctx, c_ctx, w_mod, b_mod, w_in, w_out, ln_g, ln_b, conv_w, conv_b, lru_wa, lru_ba, lru_wx, lru_bx, lru_lam, pool_w, pool_scale, loss_target, m_c_ctx, m_w_mod, m_b_mod, m_w_in, m_w_out, m_ln_g, m_ln_b, m_conv_w, m_conv_b, m_lru_wa, m_lru_ba, m_lru_wx, m_lru_bx, m_lru_lam, m_pool_w, m_pool_scale, v_c_ctx, v_w_mod, v_b_mod, v_w_in, v_w_out, v_ln_g, v_ln_b, v_conv_w, v_conv_b, v_lru_wa, v_lru_ba, v_lru_wx, v_lru_bx, v_lru_lam, v_pool_w, v_pool_scale):
    weights = dict(c_ctx=c_ctx, w_mod=w_mod, b_mod=b_mod, w_in=w_in, w_out=w_out, ln_g=ln_g, ln_b=ln_b, conv_w=conv_w,
                   conv_b=conv_b, lru_wa=lru_wa, lru_ba=lru_ba, lru_wx=lru_wx, lru_bx=lru_bx, lru_lam=lru_lam,
                   pool_w=pool_w, pool_scale=pool_scale)
    mom1 = dict(c_ctx=m_c_ctx, w_mod=m_w_mod, b_mod=m_b_mod, w_in=m_w_in, w_out=m_w_out, ln_g=m_ln_g, ln_b=m_ln_b,
                conv_w=m_conv_w, conv_b=m_conv_b, lru_wa=m_lru_wa, lru_ba=m_lru_ba, lru_wx=m_lru_wx, lru_bx=m_lru_bx,
                lru_lam=m_lru_lam, pool_w=m_pool_w, pool_scale=m_pool_scale)
    mom2 = dict(c_ctx=v_c_ctx, w_mod=v_w_mod, b_mod=v_b_mod, w_in=v_w_in, w_out=v_w_out, ln_g=v_ln_g, ln_b=v_ln_b,
                conv_w=v_conv_w, conv_b=v_conv_b, lru_wa=v_lru_wa, lru_ba=v_lru_ba, lru_wx=v_lru_wx, lru_bx=v_lru_bx,
                lru_lam=v_lru_lam, pool_w=v_pool_w, pool_scale=v_pool_scale)
    xs, cx, target = x[0], ctx[0], loss_target[0]
    s_len, d = xs.shape
    es = w_out.shape[1]
    e = es * N_CHIPS
    nb = e // LANE
    c3 = w_mod.shape[2]
    n4 = w_in.shape[2]
    pq, pg = pool_w.shape[2], pool_w.shape[3]
    ng = len(POOL_WINDOWS)
    width = n4
    assert width == d and 2 * pg == width and 2 * nb * LANE == N_CHIPS * width and d % (2 * N_CHIPS) == 0
    px, py, pc = _place()
    place = jnp.stack([pc, 2 * px + py]).astype(jnp.int32)
    cctx2 = c_ctx[None, :]

    eq = e // N_CHIPS
    small_rows = [(conv_w[0], 0), (lru_ba[0], CONV_TAPS), (lru_bx[0], CONV_TAPS + 2), (lru_lam[0], CONV_TAPS + 4),
                  (pool_scale, CONV_TAPS + 6)]
    small = _rows_kernel([(a, r, 0) for a, r in small_rows], 2 * SUB, eq, "pack_small_weights")
    wm_g, win0, sg = _gather_chips([w_mod.astype(BF16).reshape(DEPTH * d, c3), w_in[0].astype(BF16), small], "gather_weights0")
    later = [w_out[0].astype(BF16), w_in[1].astype(BF16), w_out[1].astype(BF16), pool_w.astype(BF16).reshape(ng * pq, pg)]
    w_send, w_recv, later, later_lands, w_token = _push_start(
        later, [_own_slab(a) for a in later], False, sg, "gather_weights1_start")
    wm_all = wm_g.reshape(N_CHIPS, DEPTH, d, c3)
    full = {n: jnp.swapaxes(sg[:, r:r + a.shape[0]], 0, 1).reshape(a.shape[0], e)
            for n, (a, r) in zip(SMALL_GATHERED, small_rows)}
    wa_b, wx_b = lru_wa[0].astype(BF16), lru_wx[0].astype(BF16)
    lru_args = (full["conv_w"], conv_b, wa_b, wx_b, full["lru_ba"], full["lru_bx"], full["lru_lam"])
    scale_f = full["pool_scale"]

    mod = _mod_fwd(c + w_token[0:1, 0:1], cctx2, wm_all, b_mod[:, None, :])

    def mod_parts(l, row):
        v = mod[l, row]
        return v[None, :d], 1.0 + v[None, d:2 * d], v[None, 2 * d:]

    sh0, sc0, gt0 = mod_parts(0, 0)
    shc, scc, _ = mod_parts(0, 1)
    sh1, sc1, gt1 = mod_parts(1, 0)
    lg = [ln_g[l][None, :] for l in range(DEPTH)]
    lb = [ln_b[l][None, :] for l in range(DEPTH)]

    uu0, ug0 = _inproj_fwd(xs, sc0, sh0, win0, "inproj_fwd0")
    uc0 = _inproj_fwd(cx, scc, shc, win0[:2], "inproj_fwd_ctx")
    y0 = _rglru_fwd(uu0, uc0, *lru_args)
    wout0_g, win1, wout1_g, wp_g = _push_wait(w_send, w_recv, later, later_lands, False, y0, "gather_weights1_wait")
    win = [win0, win1]
    wout = [wout0_g.reshape(e, d), wout1_g.reshape(e, d)]
    wp = wp_g.reshape(N_CHIPS, ng, pq, pg)
    br0, x1 = _outproj_fwd(y0, ug0, xs, gt0, wout[0], lg[0], lb[0], None, "outproj_fwd0")
    uu1, ug1 = _inproj_fwd(x1, sc1, sh1, win[1], "inproj_fwd1")
    d1 = _pool_map(uu1, nb, False, True, "pool_fwd")
    y1 = _pool_mm_fwd(d1, wp, scale_f)
    br1, dxo, loss_part = _outproj_fwd(y1, ug1, x1, gt1, wout[1], lg[1], lb[1], target, "outproj_fwd1")

    row_wout = d
    row_tail = d + es
    wq = 2 * (nb // N_CHIPS) * LANE * LANE // width
    whole = lambda r: (r + 2 * RS_TILE - 1) // (2 * RS_TILE) * (2 * RS_TILE)
    rows1 = whole(row_tail + pg // 2)
    rows0 = whole(row_tail + 2 * wq)
    gbuf1 = jnp.zeros((N_CHIPS, rows1, width), F32)
    gbuf0 = jnp.zeros((N_CHIPS, rows0, width), F32)

    dy1, dg1, dxres1, dbr1, dlg1, dlb1, dgt1 = _outproj_bwd(dxo, x1, br1, y1, ug1, gt1, lg[1], wout[1], "outproj_bwd1")
    gbuf1 = _outproj_bwd_w(y1, ug1, dbr1, gbuf1, row_wout, "outproj_bwd_w1")
    dd1, gbuf1, dscale = _pool_mm_bwd(dy1, d1, wp, scale_f, gbuf1, row_tail)
    du1 = _pool_map(dd1, nb, True, False, "pool_bwd")
    dx1, dsc1, dsh1 = _inproj_bwd_x([du1, dg1], x1, dxres1, sc1, win[1], "inproj_bwd_x1")
    gbuf1 = _inproj_bwd_w(x1, sc1, sh1, [du1, dg1], None, gbuf1, "inproj_bwd_w1")
    rs1, token1 = _rs_begin(gbuf1, place, "rs_exchange1")

    dy0, dg0, dxres0, dbr0, dlg0, dlb0, dgt0 = _outproj_bwd(dx1, xs, br0, y0, ug0, gt0 + token1[0:1, 0:1], lg[0], wout[0],
                                                            "outproj_bwd0")
    gbuf0 = _outproj_bwd_w(y0, ug0, dbr0, gbuf0, row_wout, "outproj_bwd_w0")
    du0, duc, dconv_w, dconv_b, dwa, dwx, dba, dbx, dlam = _rglru_bwd(uu0, uc0, dy0, *lru_args)
    dwin0c = _inproj_bwd_w(cx, scc, shc, [duc, jnp.zeros_like(duc)], None, None, "inproj_bwd_w_ctx")
    gbuf0 = _inproj_bwd_w(xs, sc0, sh0, [du0, dg0], dwin0c, gbuf0, "inproj_bwd_w0")

    def quarter(dw):
        t = dw.reshape(2, N_CHIPS, nb // N_CHIPS, LANE, LANE)
        return jnp.transpose(t, (1, 3, 0, 2, 4)).reshape(N_CHIPS, LANE, 2 * (nb // N_CHIPS) * LANE).reshape(N_CHIPS, wq, width)

    tail0 = jnp.concatenate([quarter(dwa), quarter(dwx)], axis=1)
    gbuf0 = lax.dynamic_update_slice(gbuf0, tail0, (0, row_tail, 0))
    red1 = _rs_end(rs1, place, gbuf0, "rs_exchange1")
    rs0, token0 = _rs_begin(gbuf0, place, "rs_exchange0")
    grad_x, dsc0, dsh0 = _inproj_bwd_x([du0, dg0], xs, dxres0, sc0 + token0[0:1, 0:1], win[0], "inproj_bwd_x0")
    dscc, dshc = _inproj_bwd_x([duc], cx, None, scc, win[0][:2], "inproj_bwd_x_ctx")

    k0 = VEC_KINDS
    vec = _rows_kernel(
        [(c, 0, 0), (loss_part, 0, d), (dsh0, 1, 0), (dsc0, 1, d), (dgt0, 1, 2 * d), (dshc, 2, 0), (dscc, 2, d),
         (dsh1, 3, 0), (dsc1, 3, d), (dgt1, 3, 2 * d),
         (dconv_b, k0, 0), (dlg0, k0, e), (dscale, k0 + 1, 0), (dlg1, k0 + 1, e), (dlb0, k0 + 2, 0), (dlb1, k0 + 2, d),
         (dconv_w, k0 + 3, 0), (dba, k0 + 7, 0), (dbx, k0 + 9, 0), (dlam, k0 + 11, 0)], VEC_ROWS, 3 * d, "pack_vec")
    gt_all = jnp.swapaxes(_gather_devices(vec), 0, 1)
    g_wmod = _mod_bwd_shard(gt_all, cctx2, place, c3)
    g_bmod, g_cctx, sq_err, g_small = _mod_bwd_rep(gt_all, cctx2, wm_all)
    loss = sq_err[0, 0] * (0.5 / d)
    red0 = _rs_end(rs0, place, g_bmod, "rs_exchange0")
    (rep,) = _gather_chips([red0[row_tail:row_tail + 2 * wq]], "gather_replicated")

    tmw = _row_tile(d, 256)
    red_src = lambda red, r0, tm: (red, (tm, width), lambda n, i: (r0 // tm + i, 0))
    by_layer = lambda n, gs: jnp.where(n == 0, gs[0], gs[1])
    outs = {}
    outs["w_in"] = _adamw_param(w_in, m_w_in, v_w_in, [red_src(red0, 0, tmw), red_src(red1, 0, tmw)], by_layer, tmw, "adamw_w_in")
    outs["w_out"] = _adamw_param(w_out, m_w_out, v_w_out, [red_src(red0, row_wout, tmw), red_src(red1, row_wout, tmw)],
                                 by_layer, tmw, "adamw_w_out")
    outs["w_mod"] = _adamw_param(w_mod, m_w_mod, v_w_mod, [(g_wmod, (None, tmw, c3), lambda n, i: (n, i, 0))],
                                 lambda n, gs: gs[0], tmw, "adamw_w_mod")
    pw = [a.reshape(ng, pq, pg) for a in (pool_w, m_pool_w, v_pool_w)]
    outs["pool_w"] = [o.reshape(pool_w.shape) for o in _adamw_param(
        *pw, [(red1, (pq, pg), lambda n, i: (row_tail // pq + n // 2, n % 2))], lambda n, gs: gs[0], pq, "adamw_pool_w")]
    bq = nb // N_CHIPS
    rep_src = lambda r0: (rep, (None, LANE, bq * LANE), lambda n, i: (n % N_CHIPS, r0 // LANE, n // N_CHIPS))
    stack = lambda n, gs: jnp.concatenate([gs[0][:, k * LANE:(k + 1) * LANE] for k in range(bq)], axis=0)
    for name, r0, trio in (("lru_wa", 0, (lru_wa, m_lru_wa, v_lru_wa)), ("lru_wx", wq, (lru_wx, m_lru_wx, v_lru_wx))):
        blocks = [a.reshape(2 * N_CHIPS, bq * LANE, LANE) for a in trio]
        outs[name] = [o.reshape(lru_wa.shape) for o in _adamw_param(*blocks, [rep_src(r0)], stack, bq * LANE, "adamw_" + name)]

    g_small = dict(g_small, c_ctx=g_cctx, b_mod=g_bmod)
    for n in SMALL_GATHERED:
        g_small[n] = lax.dynamic_slice_in_dim(g_small[n], place[1] * eq, eq, axis=1)
    as2d = lambda a: a.reshape(-1, a.shape[-1])
    quads = [(as2d(weights[n]), g_small[n], as2d(mom1[n]), as2d(mom2[n])) for n in SMALL_UPDATED]
    for n, (q, res) in zip(SMALL_UPDATED, zip(quads, _adamw_small(quads))):
        outs[n] = [a.reshape(weights[n].shape) for a in (q[1],) + res]

    result = [loss, grad_x[None]]
    for j in range(4):
        result += [outs[n][j] for n in WEIGHTS]
    return tuple(result)
```

```python
import jax
import jax.numpy as jnp
from jax import lax
from jax.experimental import pallas as pl
from jax.experimental.pallas import tpu as pltpu

F32 = jnp.float32
BF16 = jnp.bfloat16
LANE = 128
SUB = 8
GRID_W = 64
POOL_WINDOWS = (2, 4, 8, 16)
LRU_C = 8.0
DEPTH = 2
ALPHA = float((2 * DEPTH) ** 0.25)
LN_EPS = 1e-5
ADAM_LR, ADAM_B1, ADAM_B2, ADAM_EPS, ADAM_WD, ADAM_STEP = 0.001, 0.9, 0.999, 1e-08, 0.01, 10
N_CHIPS = 4
N_DEV = 8
MESH = pl.DeviceIdType.MESH
ROW_TILE = 512
GATE_TILE = 2048
GATE_UNROLL = 1
CONV_TAPS = 4
CONV_LEFT = 2
PAD = 8
SCAN_UNROLL = 32
RS_TILE = 448
LN_ROWS = 128
POOL_CPAD = 16
VEC_KINDS = 4


def _call(body, **kw):
    return pl.pallas_call(body, **kw)


def _dot(a, b):
    return jnp.dot(a, b, preferred_element_type=F32)


def _dot_nt(a, b):
    return lax.dot_general(a, b, (((1,), (1,)), ((), ())), preferred_element_type=F32)


def _dot_tn(a, b):
    return lax.dot_general(a, b, (((0,), (0,)), ((), ())), preferred_element_type=F32)


def _sigmoid(v):
    return 0.5 * (jnp.tanh(0.5 * v) + 1.0)


def _silu(v):
    return v * _sigmoid(v)


def _dsilu(v):
    s = _sigmoid(v)
    return s * (1.0 + v * (1.0 - s))


def _log_sigmoid(v):
    z = jnp.exp(-jnp.abs(v))
    return jnp.minimum(v, 0.0) - jnp.where(z < 1e-4, z * (1.0 - 0.5 * z), jnp.log(1.0 + z))


def _one_minus_sq(la, a):
    return -jnp.tanh(la) * (a * a + 1.0)


def _cat(ref, n):
    return jnp.concatenate([ref[k] for k in range(n)], axis=1)


def _put_chunks(ref, val, n, base=0):
    for k in range(n):
        ref[base + k] = val[:, k * LANE:(k + 1) * LANE].astype(ref.dtype)


def _row_tile(rows, want):
    t = min(rows, want)
    assert rows % t == 0
    return t


ANY_SPEC = pl.BlockSpec(memory_space=pl.ANY)


def _mod_fwd(cvec, cctx, wm, bm):
    ns, nl, d, c3 = wm.shape

    def body(c_ref, cx_ref, w_ref, b_ref, o_ref):
        cc = jnp.concatenate([c_ref[...], cx_ref[...], jnp.zeros((SUB - 2, d), F32)], axis=0)
        o_ref[...] = _dot(_silu(cc).astype(BF16), w_ref[...]) + b_ref[...]

    return _call(
        body, name="mod_fwd", grid=(nl, ns),
        in_specs=[pl.BlockSpec((1, d), lambda l, s: (0, 0)),
                  pl.BlockSpec((1, d), lambda l, s: (0, 0)),
                  pl.BlockSpec((None, None, d, c3), lambda l, s: (s, l, 0, 0)),
                  pl.BlockSpec((None, 1, c3), lambda l, s: (l, 0, s))],
        out_specs=pl.BlockSpec((None, 8, c3), lambda l, s: (l, 0, s)),
        out_shape=jax.ShapeDtypeStruct((nl, 8, ns * c3), F32),
    )(cvec, cctx, wm, bm)


def _rows_kernel(parts, rows, cols, name):
    def body(*refs):
        o_ref = refs[-1]
        o_ref[...] = jnp.zeros_like(o_ref)
        for ref, (a, r0, c0) in zip(refs[:-1], parts):
            for k in range(a.shape[0]):
                o_ref[r0 + k:r0 + k + 1, c0:c0 + a.shape[1]] = ref[k:k + 1, :]

    return _call(body, name=name, grid=(1,),
                 in_specs=[pl.BlockSpec(a.shape, lambda i: (0, 0)) for a, _, _ in parts],
                 out_specs=pl.BlockSpec((rows, cols), lambda i: (0, 0)),
                 out_shape=jax.ShapeDtypeStruct((rows, cols), F32))(*[a for a, _, _ in parts])


def _mod_bwd_shard(gt, cctx, place, c3):
    d = cctx.shape[1]

    def body(p_ref, cs_ref, dm_ref, dmx_ref, cx_ref, o_ref):
        l = pl.program_id(0)
        lhs = jnp.concatenate([_silu(cs_ref[...]), _silu(cx_ref[...]), jnp.zeros((7, d), F32)], axis=0).astype(BF16)
        dmx = jnp.where(l == 0, jnp.sum(dmx_ref[...], axis=0, keepdims=True), 0.0)
        rhs = jnp.concatenate([dm_ref[...], dmx, jnp.zeros((7, c3), F32)], axis=0).astype(BF16)
        o_ref[...] = _dot_tn(lhs, rhs)

    return _call(
        body, name="mod_bwd_shard",
        grid_spec=pltpu.PrefetchScalarGridSpec(
            num_scalar_prefetch=1, grid=(DEPTH,),
            in_specs=[pl.BlockSpec((None, N_DEV, d), lambda l, p: (0, 0, 0)),
                      pl.BlockSpec((None, N_DEV, c3), lambda l, p: (1 + 2 * l, 0, p[1])),
                      pl.BlockSpec((None, N_DEV, c3), lambda l, p: (2, 0, p[1])),
                      pl.BlockSpec((1, d), lambda l, p: (0, 0))],
            out_specs=pl.BlockSpec((None, d, c3), lambda l, p: (l, 0, 0))),
        out_shape=jax.ShapeDtypeStruct((DEPTH, d, c3), F32),
    )(place, gt, gt, gt, cctx)


def _small_layout(d, e):
    k = VEC_KINDS
    return {
        "conv_b": ((1, e), [(0, k, 0)]),
        "ln_g": ((2, d), [(0, k, e), (1, k + 1, e)]),
        "pool_scale": ((1, e), [(0, k + 1, 0)]),
        "ln_b": ((2, d), [(0, k + 2, 0), (1, k + 2, d)]),
        "conv_w": ((CONV_TAPS, e), [(t, k + 3 + t, 0) for t in range(CONV_TAPS)]),
        "lru_ba": ((2, e), [(j, k + 7 + j, 0) for j in range(2)]),
        "lru_bx": ((2, e), [(j, k + 9 + j, 0) for j in range(2)]),
        "lru_lam": ((2, e), [(j, k + 11 + j, 0) for j in range(2)]),
    }


VEC_ROWS = 24


def _mod_bwd_rep(gt, cctx, wm):
    ns, _, d, c3 = wm.shape
    layout = _small_layout(d, ns * c3 - d)
    names = list(layout)

    def body(g_ref, cx_ref, w_ref, db_ref, dc_ref, loss_ref, *small_refs):
        loss_ref[...] = jnp.zeros_like(loss_ref) + jnp.sum(g_ref[0][:, d:d + LANE])
        dm0 = jnp.sum(g_ref[1], axis=0, keepdims=True)
        dmx = jnp.sum(g_ref[2], axis=0, keepdims=True)
        dm1 = jnp.sum(g_ref[3], axis=0, keepdims=True)
        db_ref[0:1, :] = dm0 + dmx
        db_ref[1:2, :] = dm1
        dmxb = jnp.broadcast_to(dmx, (SUB, ns * c3)).astype(BF16)
        acc = jnp.zeros((SUB, d), F32)
        for s in range(ns):
            acc = acc + _dot_nt(dmxb[:, s * c3:(s + 1) * c3], w_ref[s])
        dc_ref[...] = acc[0:1, :] * _dsilu(cx_ref[...])
        for ref, name in zip(small_refs, names):
            shape, places = layout[name]
            for arr_row, vec_row, col0 in places:
                total = jnp.sum(g_ref[vec_row], axis=0, keepdims=True)
                ref[arr_row:arr_row + 1, :] = total[:, col0:col0 + shape[1]]

    outs = _call(
        body, name="mod_bwd_rep", grid=(1,),
        in_specs=[pl.BlockSpec(gt.shape, lambda i: (0, 0, 0)),
                  pl.BlockSpec((1, d), lambda i: (0, 0)),
                  pl.BlockSpec((ns, None, d, c3), lambda i: (0, 0, 0, 0))],
        out_specs=[pl.BlockSpec((DEPTH, ns * c3), lambda i: (0, 0)), pl.BlockSpec((1, d), lambda i: (0, 0)),
                   pl.BlockSpec((1, LANE), lambda i: (0, 0))]
        + [pl.BlockSpec(layout[n][0], lambda i: (0, 0)) for n in names],
        out_shape=[jax.ShapeDtypeStruct((DEPTH, ns * c3), F32), jax.ShapeDtypeStruct((1, d), F32),
                   jax.ShapeDtypeStruct((1, LANE), F32)]
        + [jax.ShapeDtypeStruct(layout[n][0], F32) for n in names],
    )(gt, cctx, wm)
    return outs[0], outs[1], outs[2], dict(zip(names, outs[3:]))


def _inproj_fwd(xin, sc1, sh, w, name):
    rows, d = xin.shape
    ns, _, n4 = w.shape
    cpb = n4 // LANE
    tm = _row_tile(rows, 512)
    assert ns in (2, 4)

    def body(x_ref, sc_ref, sh_ref, w_ref, *o_refs):
        h = (x_ref[...] * sc_ref[...] + sh_ref[...]).astype(BF16)
        for s in range(ns):
            _put_chunks(o_refs[s // 2], _dot(h, w_ref[s]), cpb, base=(s % 2) * cpb)

    spec = pl.BlockSpec((2 * cpb, tm, LANE), lambda i: (0, i, 0))
    dtypes = (F32, BF16)[:ns // 2]
    res = _call(
        body, name=name, grid=(rows // tm,),
        in_specs=[pl.BlockSpec((tm, d), lambda i: (i, 0)),
                  pl.BlockSpec((1, d), lambda i: (0, 0)),
                  pl.BlockSpec((1, d), lambda i: (0, 0)),
                  pl.BlockSpec((ns, d, n4), lambda i: (0, 0, 0))],
        out_specs=[spec] * len(dtypes),
        out_shape=[jax.ShapeDtypeStruct((2 * cpb, rows, LANE), t) for t in dtypes],
    )(xin, sc1, sh, w)
    return res[0] if ns == 2 else tuple(res)


def _inproj_bwd_x(dparts, xin, dxres, sc1, w, name):
    rows, d = xin.shape
    npart = len(dparts)
    e = dparts[0].shape[1]
    ns, _, n4 = w.shape
    per = e // n4
    assert per * npart == ns
    tm = _row_tile(rows, 512)
    has_res = dxres is not None

    def body(*refs):
        dp = refs[:npart]
        x_ref, sc_ref, w_ref = refs[npart:npart + 3]
        rest = refs[npart + 3:]
        if has_res:
            res_ref, dx_ref, dsc_ref, dsh_ref = rest
        else:
            dsc_ref, dsh_ref = rest
        i = pl.program_id(0)
        dh = jnp.zeros((tm, d), F32)
        for p in range(npart):
            v = dp[p][...]
            for q in range(per):
                dh = dh + _dot_nt(v[:, q * n4:(q + 1) * n4], w_ref[p * per + q])

        @pl.when(i == 0)
        def _():
            dsc_ref[...] = jnp.zeros_like(dsc_ref)
            dsh_ref[...] = jnp.zeros_like(dsh_ref)

        dsc_ref[...] += jnp.sum(dh * x_ref[...], axis=0, keepdims=True)
        dsh_ref[...] += jnp.sum(dh, axis=0, keepdims=True)
        if has_res:
            dx_ref[...] = res_ref[...] + dh * sc_ref[...]

    row_spec = pl.BlockSpec((tm, d), lambda i: (i, 0))
    vec_spec = pl.BlockSpec((1, d), lambda i: (0, 0))
    in_specs = [pl.BlockSpec((tm, e), lambda i: (i, 0))] * npart + [row_spec, vec_spec,
                                                                     pl.BlockSpec((ns, d, n4), lambda i: (0, 0, 0))]
    args = list(dparts) + [xin, sc1, w]
    out_specs, out_shape = [vec_spec, vec_spec], [jax.ShapeDtypeStruct((1, d), F32)] * 2
    if has_res:
        in_specs.append(row_spec)
        args.append(dxres)
        out_specs = [row_spec] + out_specs
        out_shape = [jax.ShapeDtypeStruct((rows, d), F32)] + out_shape
    return _call(body, name=name, grid=(rows // tm,), in_specs=in_specs, out_specs=out_specs, out_shape=out_shape)(*args)


def _inproj_bwd_w(xin, sc1, sh, dparts, init, gbuf, name):
    rows, d = xin.shape
    npart = len(dparts)
    e = dparts[0].shape[1]
    n4 = e // 2
    ns = 2 * npart
    tm = _row_tile(rows, 1024)
    nt = rows // tm
    has_init = init is not None
    into = gbuf is not None
    assert not into or (ns == N_CHIPS and gbuf.shape[2] == n4)

    def body(*refs):
        x_ref, sc_ref, sh_ref = refs[:3]
        dp = refs[3:3 + npart]
        init_ref = refs[3 + npart] if has_init else None
        o_ref = refs[-1]
        s, i = pl.program_id(0), pl.program_id(1)
        h = (x_ref[...] * sc_ref[...] + sh_ref[...]).astype(BF16)

        @pl.when(i == 0)
        def _():
            o_ref[...] = init_ref[...] if has_init else jnp.zeros_like(o_ref)

        for p in range(npart):
            @pl.when(s // 2 == p)
            def _(p=p):
                o_ref[...] += _dot_tn(h, dp[p][...])

    in_specs = [pl.BlockSpec((tm, d), lambda s, i: (i, 0)),
                pl.BlockSpec((1, d), lambda s, i: (0, 0)),
                pl.BlockSpec((1, d), lambda s, i: (0, 0))]
    in_specs += [pl.BlockSpec((tm, n4), lambda s, i: (i, s % 2))] * npart
    args = [xin, sc1, sh] + list(dparts)
    o_spec = pl.BlockSpec((None, d, n4), lambda s, i: (s, 0, 0))
    if has_init:
        in_specs.append(o_spec)
        args.append(init)
    extra = {}
    if into:
        in_specs.append(ANY_SPEC)
        args.append(gbuf)
        extra = dict(input_output_aliases={len(args) - 1: 0})
    out_shape = jax.ShapeDtypeStruct(gbuf.shape if into else (ns, d, n4), F32)
    return _call(body, name=name, grid=(ns, nt), in_specs=in_specs, out_specs=o_spec, out_shape=out_shape, **extra)(*args)


def _gated(y_ref, g_ref, nch):
    return jnp.concatenate([(y_ref[k].astype(F32) * _silu(g_ref[k].astype(F32))).astype(BF16) for k in range(nch)], axis=1)


def _ln_stats(r):
    mu = jnp.mean(r, axis=-1, keepdims=True)
    var = jnp.mean(jnp.square(r - mu), axis=-1, keepdims=True)
    rstd = lax.rsqrt(var + LN_EPS)
    return (r - mu) * rstd, rstd


def _outproj_fwd(y, ug, xin, gt, wout, lg, lb, target, name):
    nch, rows, _ = y.shape
    e, d = wout.shape
    tm = _row_tile(rows, 512)
    with_loss = target is not None

    def body(*refs):
        y_ref, g_ref, x_ref, gt_ref, w_ref, lg_ref, lb_ref = refs[:7]
        if with_loss:
            t_ref, br_ref, dxo_ref, loss_ref = refs[7:]
        else:
            br_ref, xo_ref = refs[7:]
        z = _gated(y_ref, g_ref, nch)
        br_ref[...] = _dot(z, w_ref[...])
        if with_loss:
            @pl.when(pl.program_id(0) == 0)
            def _():
                loss_ref[...] = jnp.zeros_like(loss_ref)

        def norm(j, c):
            rows = pl.ds(pl.multiple_of(j * LN_ROWS, LN_ROWS), LN_ROWS)
            xhat, _ = _ln_stats(ALPHA * x_ref[rows, :] + gt_ref[...] * br_ref[rows, :])
            xo = xhat * lg_ref[...] + lb_ref[...]
            if with_loss:
                err = xo - t_ref[rows, :]
                dxo_ref[rows, :] = err * (1.0 / d)
                col = jnp.sum(err * err, axis=0, keepdims=True)
                loss_ref[...] += sum(col[:, k * LANE:(k + 1) * LANE] for k in range(d // LANE))
            else:
                xo_ref[rows, :] = xo
            return c

        lax.fori_loop(0, tm // LN_ROWS, norm, 0)

    chunk_spec = pl.BlockSpec((nch, tm, LANE), lambda i: (0, i, 0))
    g_spec = chunk_spec
    row_spec = pl.BlockSpec((tm, d), lambda i: (i, 0))
    vec_spec = pl.BlockSpec((1, d), lambda i: (0, 0))
    in_specs = [chunk_spec, g_spec, row_spec, vec_spec, pl.BlockSpec((e, d), lambda i: (0, 0)), vec_spec, vec_spec]
    args = [y, ug, xin, gt, wout, lg, lb]
    out_specs = [row_spec, row_spec]
    out_shape = [jax.ShapeDtypeStruct((rows, d), F32)] * 2
    if with_loss:
        in_specs.append(row_spec)
        args.append(target)
        out_specs.append(pl.BlockSpec((1, LANE), lambda i: (0, 0)))
        out_shape.append(jax.ShapeDtypeStruct((1, LANE), F32))
    return _call(body, name=name, grid=(rows // tm,), in_specs=in_specs, out_specs=out_specs, out_shape=out_shape)(*args)


def _outproj_bwd(dxo, xin, br, y, ug, gt, lg, wout, name):
    nch, rows, _ = y.shape
    e, d = wout.shape
    tm = _row_tile(rows, 256)

    def body(dxo_ref, x_ref, br_ref, y_ref, g_ref, gt_ref, lg_ref, w_ref,
             dy_ref, dg_ref, dxres_ref, dbr_ref, dlg_ref, dlb_ref, dgt_ref):
        @pl.when(pl.program_id(0) == 0)
        def _():
            dlg_ref[...] = jnp.zeros_like(dlg_ref)
            dlb_ref[...] = jnp.zeros_like(dlb_ref)
            dgt_ref[...] = jnp.zeros_like(dgt_ref)

        def norm_bwd(j, c):
            rows = pl.ds(pl.multiple_of(j * LN_ROWS, LN_ROWS), LN_ROWS)
            dxo_v = dxo_ref[rows, :]
            brv = br_ref[rows, :]
            xhat, rstd = _ln_stats(ALPHA * x_ref[rows, :] + gt_ref[...] * brv)
            dxh = dxo_v * lg_ref[...]
            dr = rstd * (dxh - jnp.mean(dxh, axis=-1, keepdims=True) - xhat * jnp.mean(dxh * xhat, axis=-1, keepdims=True))
            dlg_ref[...] += jnp.sum(dxo_v * xhat, axis=0, keepdims=True)
            dlb_ref[...] += jnp.sum(dxo_v, axis=0, keepdims=True)
            dgt_ref[...] += jnp.sum(dr * brv, axis=0, keepdims=True)
            dxres_ref[rows, :] = ALPHA * dr
            dbr_ref[rows, :] = (gt_ref[...] * dr).astype(BF16)
            return c

        lax.fori_loop(0, tm // LN_ROWS, norm_bwd, 0)
        dz = _dot_nt(dbr_ref[...], w_ref[...])
        for k in range(nch):
            dzk = dz[:, k * LANE:(k + 1) * LANE]
            gk = g_ref[k].astype(F32)
            dy_ref[k] = dzk * _silu(gk)
            dg_ref[:, k * LANE:(k + 1) * LANE] = (dzk * y_ref[k].astype(F32) * _dsilu(gk)).astype(BF16)

    chunk_spec = pl.BlockSpec((nch, tm, LANE), lambda i: (0, i, 0))
    g_spec = chunk_spec
    row_spec = pl.BlockSpec((tm, d), lambda i: (i, 0))
    vec_spec = pl.BlockSpec((1, d), lambda i: (0, 0))
    return _call(
        body, name=name, grid=(rows // tm,),
        in_specs=[row_spec, row_spec, row_spec, chunk_spec, g_spec, vec_spec, vec_spec, pl.BlockSpec((e, d), lambda i: (0, 0))],
        out_specs=[chunk_spec, pl.BlockSpec((tm, e), lambda i: (i, 0)), row_spec, row_spec, vec_spec, vec_spec, vec_spec],
        out_shape=[jax.ShapeDtypeStruct((nch, rows, LANE), F32), jax.ShapeDtypeStruct((rows, e), BF16),
                   jax.ShapeDtypeStruct((rows, d), F32), jax.ShapeDtypeStruct((rows, d), BF16)]
        + [jax.ShapeDtypeStruct((1, d), F32)] * 3,
    )(dxo, xin, br, y, ug, gt, lg, wout)


def _outproj_bwd_w(y, ug, dbr, gbuf, row0, name):
    nch, rows, _ = y.shape
    d = dbr.shape[1]
    e = nch * LANE
    es = e // N_CHIPS
    tm = _row_tile(rows, 512)
    assert gbuf.shape[2] == d and row0 % es == 0

    def body(y_ref, g_ref, dbr_ref, buf_ref, o_ref):
        @pl.when(pl.program_id(0) == 0)
        def _():
            o_ref[...] = jnp.zeros_like(o_ref)

        z = _gated(y_ref, g_ref, nch)
        o_ref[...] += _dot_tn(z, dbr_ref[...]).reshape(N_CHIPS, es, d)

    return _call(
        body, name=name, grid=(rows // tm,),
        in_specs=[pl.BlockSpec((nch, tm, LANE), lambda i: (0, i, 0)),
                  pl.BlockSpec((nch, tm, LANE), lambda i: (0, i, 0)),
                  pl.BlockSpec((tm, d), lambda i: (i, 0)),
                  ANY_SPEC],
        out_specs=pl.BlockSpec((N_CHIPS, es, d), lambda i: (0, row0 // es, 0)),
        out_shape=jax.ShapeDtypeStruct(gbuf.shape, F32),
        input_output_aliases={3: 0},
    )(y, ug, dbr, gbuf)


def _scan(a_ref, b_ref, h_ref, *, length, init, reverse, a_shift, store):
    nblk = length // SUB
    unroll = min(SCAN_UNROLL, nblk)
    assert nblk % unroll == 0
    row = lax.broadcasted_iota(jnp.int32, (SUB, LANE), 0)
    last = 0 if reverse else SUB - 1
    edges = [(row >= SUB - k) if reverse else (row < k) for k in (1, 2, 4)]

    def local_scan(a, b):
        for k, edge in zip((1, 2, 4), edges):
            sh = (SUB - k) if reverse else k
            b = b + a * jnp.where(edge, 0.0, pltpu.roll(b, sh, 0))
            a = a * jnp.where(edge, 1.0, pltpu.roll(a, sh, 0))
        return a, b

    def step(i, carry):
        base = pl.multiple_of(((nblk // unroll - 1 - i) if reverse else i) * (unroll * SUB), unroll * SUB)
        order = range(unroll - 1, -1, -1) if reverse else range(unroll)
        loaded = [(a_ref[pl.ds(PAD + base + j * SUB + a_shift, SUB), :], b_ref[pl.ds(PAD + base + j * SUB, SUB), :])
                  for j in order]
        scanned = [local_scan(a, b) for a, b in loaded]
        for j, (a, b) in zip(order, scanned):
            if store:
                h_ref[pl.ds(PAD + base + j * SUB, SUB), :] = b + a * carry
            a_l = jnp.broadcast_to(a[last:last + 1, :], (SUB, LANE))
            b_l = jnp.broadcast_to(b[last:last + 1, :], (SUB, LANE))
            carry = b_l + a_l * carry
        return carry

    carry = lax.fori_loop(0, nblk // unroll, step, jnp.broadcast_to(init, (SUB, LANE)))
    return carry[0:1, :]


def _conv_fwd(src_ref, upad, u_ref, cw, cb, length):
    zeros = jnp.zeros((PAD, LANE), F32)
    upad[pl.ds(0, PAD), :] = zeros
    upad[pl.ds(PAD + length, PAD), :] = zeros
    rt = _row_tile(length, ROW_TILE)

    def copy(i, c):
        t0 = pl.multiple_of(i * rt, rt)
        upad[pl.ds(PAD + t0, rt), :] = src_ref[pl.ds(t0, rt), :]
        return c

    lax.fori_loop(0, length // rt, copy, 0)

    def tile(i, c):
        t0 = pl.multiple_of(i * rt, rt)
        acc = jnp.zeros((rt, LANE), F32)
        for k in range(CONV_TAPS):
            acc = acc + upad[pl.ds(t0 + PAD - CONV_LEFT + k, rt), :] * cw[k:k + 1, :]
        u_ref[pl.ds(t0, rt), :] = acc + cb
        return c

    lax.fori_loop(0, length // rt, tile, 0)


def _gates_fwd(u_ref, a_ref, b_ref, wa, wx, ba, bx, ls, length, keep=None):
    rt = _row_tile(length, GATE_TILE)

    def tile(i, c):
        t0 = pl.multiple_of(i * rt, rt)
        ut = u_ref[pl.ds(t0, rt), :]
        ub = ut.astype(BF16)
        r = _sigmoid(_dot(ub, wa) + ba)
        ig = _sigmoid(_dot(ub, wx) + bx)
        if keep is not None:
            keep[0][pl.ds(t0, rt), :] = r
            keep[1][pl.ds(t0, rt), :] = ig
        la = (LRU_C * r) * ls
        a = jnp.exp(la)
        a_ref[pl.ds(PAD + t0, rt), :] = a
        b_ref[pl.ds(PAD + t0, rt), :] = jnp.sqrt(_one_minus_sq(la, a)) * (ig * ut)
        return c

    lax.fori_loop(0, length // rt, tile, 0, unroll=min(GATE_UNROLL, length // rt))


def _lru_specs():
    return [pl.BlockSpec((CONV_TAPS, LANE), lambda n: (0, n)),
            pl.BlockSpec((1, LANE), lambda n: (0, n)),
            pl.BlockSpec((2, None, LANE, LANE), lambda n: (0, n, 0, 0)),
            pl.BlockSpec((2, None, LANE, LANE), lambda n: (0, n, 0, 0)),
            pl.BlockSpec((2, LANE), lambda n: (0, n)),
            pl.BlockSpec((2, LANE), lambda n: (0, n)),
            pl.BlockSpec((2, LANE), lambda n: (0, n))]


def _rglru_fwd(ug, uc, conv_w, conv_b, wa, wx, ba, bx, lam):
    nb = uc.shape[0]
    s_len, t_len = ug.shape[1], uc.shape[1]

    def body(u0_ref, uc0_ref, cw_ref, cb_ref, wa_ref, wx_ref, ba_ref, bx_ref, lam_ref, y_ref,
             upad, ubuf, abuf, hbuf):
        cw, cb = cw_ref[...], cb_ref[...]
        lsig = _log_sigmoid(lam_ref[...])
        zero = jnp.zeros((1, LANE), F32)
        _conv_fwd(uc0_ref, upad, ubuf, cw, cb, t_len)
        h0 = []
        for dr in range(2):
            _gates_fwd(ubuf, abuf, hbuf, wa_ref[dr], wx_ref[dr], ba_ref[dr:dr + 1, :], bx_ref[dr:dr + 1, :],
                       lsig[dr:dr + 1, :], t_len)
            h0.append(_scan(abuf, hbuf, hbuf, length=t_len, init=zero, reverse=(dr == 1), a_shift=0, store=False))
        _conv_fwd(u0_ref, upad, ubuf, cw, cb, s_len)
        rt = _row_tile(s_len, ROW_TILE)
        for dr in range(2):
            _gates_fwd(ubuf, abuf, hbuf, wa_ref[dr], wx_ref[dr], ba_ref[dr:dr + 1, :], bx_ref[dr:dr + 1, :],
                       lsig[dr:dr + 1, :], s_len)
            _scan(abuf, hbuf, hbuf, length=s_len, init=h0[dr], reverse=(dr == 1), a_shift=0, store=True)

            def acc(i, c, dr=dr):
                t0 = pl.multiple_of(i * rt, rt)
                h = hbuf[pl.ds(PAD + t0, rt), :]
                if dr == 0:
                    upad[pl.ds(PAD + t0, rt), :] = h
                else:
                    y_ref[pl.ds(t0, rt), :] = (upad[pl.ds(PAD + t0, rt), :] + h).astype(y_ref.dtype)
                return c

            lax.fori_loop(0, s_len // rt, acc, 0)

    seq = pltpu.VMEM((s_len + 2 * PAD, LANE), F32)
    return _call(
        body, name="rglru_fwd", grid=(nb,),
        in_specs=[pl.BlockSpec((None, s_len, LANE), lambda n: (n, 0, 0)),
                  pl.BlockSpec((None, t_len, LANE), lambda n: (n, 0, 0))] + _lru_specs(),
        out_specs=pl.BlockSpec((None, s_len, LANE), lambda n: (n, 0, 0)),
        out_shape=jax.ShapeDtypeStruct((nb, s_len, LANE), BF16),
        scratch_shapes=[seq, pltpu.VMEM((s_len, LANE), F32), seq, seq],
    )(ug, uc, conv_w, conv_b, wa, wx, ba, bx, lam)


def _rglru_bwd(ug, uc, dy, conv_w, conv_b, wa, wx, ba, bx, lam):
    nb = uc.shape[0]
    e = nb * LANE
    s_len, t_len = ug.shape[1], uc.shape[1]

    def body(u0_ref, uc0_ref, dy_ref, cw_ref, cb_ref, wa_ref, wx_ref, ba_ref, bx_ref, lam_ref,
             du_ref, duc_ref, dcw_ref, dcb_ref, dwa_ref, dwx_ref, dba_ref, dbx_ref, dlam_ref,
             upad, ubuf, abuf, hbuf, lbuf, dubuf, rbuf, ibuf, cpad, cu, ca0, ch0, ca1, ch1, cr0, ci0, cr1, ci1):
        cw, cb = cw_ref[...], cb_ref[...]
        lam_v = lam_ref[...]
        lsig = _log_sigmoid(lam_v)
        zero = jnp.zeros((1, LANE), F32)
        zpad = jnp.zeros((PAD, LANE), F32)
        for ref in (dcw_ref, dcb_ref, dwa_ref, dwx_ref, dba_ref, dbx_ref, dlam_ref):
            ref[...] = jnp.zeros_like(ref)

        def params(dr):
            return (wa_ref[dr], wx_ref[dr], ba_ref[dr:dr + 1, :], bx_ref[dr:dr + 1, :], lsig[dr:dr + 1, :])

        def direction_bwd(dr, u_ref, a_ref, h_ref, l_ref, gates, dub, length, first):
            wa_d, wx_d, ba_d, bx_d, ls_d = params(dr)
            rt = _row_tile(length, GATE_TILE)
            prev = 1 if dr == 1 else -1

            def tile(i, c):
                t0 = pl.multiple_of(i * rt, rt)
                ut = u_ref[pl.ds(t0, rt), :]
                ub = ut.astype(BF16)
                r = gates[0][pl.ds(t0, rt), :]
                ig = gates[1][pl.ds(t0, rt), :]
                la = (LRU_C * r) * ls_d
                a = a_ref[pl.ds(PAD + t0, rt), :]
                q = _one_minus_sq(la, a)
                rs = lax.rsqrt(q)
                sq = q * rs
                lm = l_ref[pl.ds(PAD + t0, rt), :]
                da = lm * h_ref[pl.ds(PAD + t0 + prev, rt), :]
                dsq = lm * ig * ut
                dig = lm * sq * ut
                dla = da * a - dsq * (a * a) * rs
                dr_ = dla * (LRU_C * ls_d)
                dlam_ref[dr:dr + 1, :] += jnp.sum(dla * (LRU_C * r), axis=0, keepdims=True)
                dpr = dr_ * r * (1.0 - r)
                dpi = dig * ig * (1.0 - ig)
                dba_ref[dr:dr + 1, :] += jnp.sum(dpr, axis=0, keepdims=True)
                dbx_ref[dr:dr + 1, :] += jnp.sum(dpi, axis=0, keepdims=True)
                dprb, dpib = dpr.astype(BF16), dpi.astype(BF16)
                dwa_ref[dr] += _dot_tn(ub, dprb)
                dwx_ref[dr] += _dot_tn(ub, dpib)
                dut = lm * sq * ig + _dot_nt(dprb, wa_d) + _dot_nt(dpib, wx_d)
                if first:
                    dub[pl.ds(PAD + t0, rt), :] = dut
                else:
                    dub[pl.ds(PAD + t0, rt), :] += dut
                return c

            lax.fori_loop(0, length // rt, tile, 0, unroll=min(GATE_UNROLL, length // rt))

        def conv_bwd(dub, src_pad, out_ref, length):
            rt = _row_tile(length, ROW_TILE)

            def tile(i, c):
                t0 = pl.multiple_of(i * rt, rt)
                dut = dub[pl.ds(PAD + t0, rt), :]
                dcb_ref[...] += jnp.sum(dut, axis=0, keepdims=True)
                acc = jnp.zeros((rt, LANE), F32)
                for k in range(CONV_TAPS):
                    sh = CONV_LEFT - k
                    acc = acc + dub[pl.ds(PAD + t0 + sh, rt), :] * cw[k:k + 1, :]
                    dcw_ref[k:k + 1, :] += jnp.sum(dut * src_pad[pl.ds(PAD + t0 - sh, rt), :], axis=0, keepdims=True)
                out_ref[pl.ds(t0, rt), :] = acc.astype(out_ref.dtype)
                return c

            lax.fori_loop(0, length // rt, tile, 0)

        _conv_fwd(uc0_ref, cpad, cu, cw, cb, t_len)
        cbufs = ((ca0, ch0), (ca1, ch1))
        cgates = ((cr0, ci0), (cr1, ci1))
        h0 = []
        for dr in range(2):
            ca, chh = cbufs[dr]
            _gates_fwd(cu, ca, chh, *params(dr), t_len, keep=cgates[dr])
            h0.append(_scan(ca, chh, chh, length=t_len, init=zero, reverse=(dr == 1), a_shift=0, store=True))
        _conv_fwd(u0_ref, upad, ubuf, cw, cb, s_len)
        rt = _row_tile(s_len, ROW_TILE)
        dh0 = []
        for dr in range(2):
            rev = dr == 1
            _gates_fwd(ubuf, abuf, hbuf, *params(dr), s_len, keep=(rbuf, ibuf))
            _scan(abuf, hbuf, hbuf, length=s_len, init=h0[dr], reverse=rev, a_shift=0, store=True)
            first_row = PAD + s_len if rev else PAD - 1
            hbuf[pl.ds(first_row, 1), :] = h0[dr]
            end_row = PAD - 1 if rev else PAD + s_len
            abuf[pl.ds(end_row, 1), :] = zero

            def copy(i, c):
                t0 = pl.multiple_of(i * rt, rt)
                lbuf[pl.ds(PAD + t0, rt), :] = dy_ref[pl.ds(t0, rt), :]
                return c

            lax.fori_loop(0, s_len // rt, copy, 0)
            _scan(abuf, lbuf, lbuf, length=s_len, init=zero, reverse=not rev, a_shift=(-1 if rev else 1), store=True)
            start = PAD + s_len - 1 if rev else PAD
            dh0.append(abuf[pl.ds(start, 1), :] * lbuf[pl.ds(start, 1), :])
            direction_bwd(dr, ubuf, abuf, hbuf, lbuf, (rbuf, ibuf), dubuf, s_len, first=(dr == 0))
        dubuf[pl.ds(0, PAD), :] = zpad
        dubuf[pl.ds(PAD + s_len, PAD), :] = zpad
        conv_bwd(dubuf, upad, du_ref, s_len)
        lc = lbuf
        duc_buf = dubuf
        for dr in range(2):
            rev = dr == 1
            ca, chh = cbufs[dr]
            first_row = PAD + t_len if rev else PAD - 1
            chh[pl.ds(first_row, 1), :] = zero
            end_row = PAD - 1 if rev else PAD + t_len
            ca[pl.ds(end_row, 1), :] = zero + 1.0
            rtc = _row_tile(t_len, ROW_TILE)

            def clear(i, c):
                t0 = pl.multiple_of(i * rtc, rtc)
                lc[pl.ds(PAD + t0, rtc), :] = jnp.zeros((rtc, LANE), F32)
                return c

            lax.fori_loop(0, t_len // rtc, clear, 0)
            _scan(ca, lc, lc, length=t_len, init=dh0[dr], reverse=not rev, a_shift=(-1 if rev else 1), store=True)
            direction_bwd(dr, cu, ca, chh, lc, cgates[dr], duc_buf, t_len, first=(dr == 0))
        duc_buf[pl.ds(0, PAD), :] = zpad
        duc_buf[pl.ds(PAD + t_len, PAD), :] = zpad
        conv_bwd(duc_buf, cpad, duc_ref, t_len)
        dlam_ref[...] = dlam_ref[...] * (1.0 - _sigmoid(lam_v))

    seq = pltpu.VMEM((s_len + 2 * PAD, LANE), F32)
    cseq = pltpu.VMEM((t_len + 2 * PAD, LANE), F32)
    flat = pltpu.VMEM((s_len, LANE), F32)
    cflat = pltpu.VMEM((t_len, LANE), F32)
    vec2 = pl.BlockSpec((2, LANE), lambda n: (0, n))
    wspec = pl.BlockSpec((2, None, LANE, LANE), lambda n: (0, n, 0, 0))
    return _call(
        body, name="rglru_bwd", grid=(nb,),
        in_specs=[pl.BlockSpec((None, s_len, LANE), lambda n: (n, 0, 0)),
                  pl.BlockSpec((None, t_len, LANE), lambda n: (n, 0, 0)),
                  pl.BlockSpec((None, s_len, LANE), lambda n: (n, 0, 0))] + _lru_specs(),
        out_specs=[pl.BlockSpec((s_len, LANE), lambda n: (0, n)),
                   pl.BlockSpec((t_len, LANE), lambda n: (0, n)),
                   pl.BlockSpec((CONV_TAPS, LANE), lambda n: (0, n)),
                   pl.BlockSpec((1, LANE), lambda n: (0, n)),
                   wspec, wspec, vec2, vec2, vec2],
        out_shape=[jax.ShapeDtypeStruct((s_len, e), BF16), jax.ShapeDtypeStruct((t_len, e), BF16),
                   jax.ShapeDtypeStruct((CONV_TAPS, e), F32), jax.ShapeDtypeStruct((1, e), F32),
                   jax.ShapeDtypeStruct((2, nb, LANE, LANE), F32), jax.ShapeDtypeStruct((2, nb, LANE, LANE), F32),
                   jax.ShapeDtypeStruct((2, e), F32), jax.ShapeDtypeStruct((2, e), F32), jax.ShapeDtypeStruct((2, e), F32)],
        scratch_shapes=[seq, flat, seq, seq, seq, seq, flat, flat,
                        cseq, cflat, cseq, cseq, cseq, cseq, cflat, cflat, cflat, cflat],
    )(ug, uc, dy, conv_w, conv_b, wa, wx, ba, bx, lam)


def _pool_windows(src_ref, out_ref, colbuf, rowbuf, half, transpose, s_len):
    gw = GRID_W
    lg = gw.bit_length() - 1
    n_rows = s_len // gw
    cp, rm = POOL_CPAD, 8 * gw
    stride = gw + 2 * cp
    rt = _row_tile(s_len, ROW_TILE)
    assert rt % gw == 0 and half <= cp
    gpt = rt // gw
    offs = range(-half, half)
    zmargin = jnp.zeros((cp, LANE), F32)

    def zcol(r, c):
        base = pl.multiple_of(r * stride, SUB)
        colbuf[pl.ds(base, cp), :] = zmargin
        colbuf[pl.ds(base + cp + gw, cp), :] = zmargin
        return c

    lax.fori_loop(0, n_rows, zcol, 0)

    def zrow(i, c):
        t0 = pl.multiple_of(i * gw, gw)
        rowbuf[pl.ds(t0, gw), :] = jnp.zeros((gw, LANE), F32)
        rowbuf[pl.ds(rm + s_len + t0, gw), :] = jnp.zeros((gw, LANE), F32)
        return c

    lax.fori_loop(0, rm // gw, zrow, 0)

    col = lax.broadcasted_iota(jnp.int32, (gw, LANE), 0)
    ccnt = (jnp.minimum(col + half, gw) - jnp.maximum(col - half, 0)).astype(F32)

    def row_counts(t0):
        row = (t0 + lax.broadcasted_iota(jnp.int32, (rt, LANE), 0)) >> lg
        return (jnp.minimum(row + half, n_rows) - jnp.maximum(row - half, 0)).astype(F32)

    def col_base(t0, g):
        return pl.multiple_of((t0 // gw) * stride, SUB) + g * stride + cp

    def col_sum(t0, g, sign):
        acc = jnp.zeros((gw, LANE), F32)
        for o in offs:
            acc = acc + colbuf[pl.ds(col_base(t0, g) + sign * o, gw), :]
        return acc

    def row_sum(t0, sign):
        acc = jnp.zeros((rt, LANE), F32)
        for o in offs:
            acc = acc + rowbuf[pl.ds(rm + t0 + sign * o * gw, rt), :]
        return acc

    n_tiles = s_len // rt
    assert rt >= half * gw

    def loop(fn, edges=False):
        def step(i, c):
            t0 = pl.multiple_of(i * rt, rt)
            fn(t0, False) if edges else fn(t0)
            return c
        if edges:
            fn(0, True)
            if n_tiles > 1:
                fn(s_len - rt, True)
            lax.fori_loop(1, n_tiles - 1, step, 0)
        else:
            lax.fori_loop(0, n_tiles, step, 0)

    inv_ccnt = 1.0 / ccnt

    def by_row_count(v, t0, edge):
        return v / row_counts(t0) if edge else v * (1.0 / (2 * half))

    if not transpose:
        def fill(t0):
            for g in range(gpt):
                colbuf[pl.ds(col_base(t0, g), gw), :] = src_ref[pl.ds(t0 + g * gw, gw), :]

        def cols(t0):
            for g in range(gpt):
                rowbuf[pl.ds(rm + t0 + g * gw, gw), :] = col_sum(t0, g, 1) * inv_ccnt

        def rows(t0, edge):
            mean = by_row_count(row_sum(t0, 1), t0, edge)
            out_ref[pl.ds(t0, rt), :] = (mean - src_ref[pl.ds(t0, rt), :]).astype(out_ref.dtype)

        loop(fill)
        loop(cols)
        loop(rows, edges=True)
    else:
        def fill(t0, edge):
            rowbuf[pl.ds(rm + t0, rt), :] = by_row_count(src_ref[pl.ds(t0, rt), :], t0, edge)

        def rows(t0):
            acc = row_sum(t0, -1)
            for g in range(gpt):
                colbuf[pl.ds(col_base(t0, g), gw), :] = acc[g * gw:(g + 1) * gw, :] * inv_ccnt

        def cols(t0):
            for g in range(gpt):
                rows_g = pl.ds(t0 + g * gw, gw)
                out_ref[rows_g, :] = (col_sum(t0, g, -1) - src_ref[rows_g, :]).astype(out_ref.dtype)

        loop(fill, edges=True)
        loop(rows)
        loop(cols)


def _pool_map(src, nb, transpose, out_chunk_major, name):
    s_len = src.shape[1]
    cpg = nb // len(POOL_WINDOWS)

    def body(src_ref, out_ref, colbuf, rowbuf):
        n = pl.program_id(0)
        for gi, w in enumerate(POOL_WINDOWS):
            @pl.when(n // cpg == gi)
            def _(w=w):
                _pool_windows(src_ref, out_ref, colbuf, rowbuf, w // 2, transpose, s_len)

    if out_chunk_major:
        out_spec = pl.BlockSpec((None, s_len, LANE), lambda n: (n, 0, 0))
        out_shape = jax.ShapeDtypeStruct((nb, s_len, LANE), BF16)
    else:
        out_spec = pl.BlockSpec((s_len, LANE), lambda n: (0, n))
        out_shape = jax.ShapeDtypeStruct((s_len, nb * LANE), BF16)
    return _call(
        body, name=name, grid=(nb,),
        in_specs=[pl.BlockSpec((None, s_len, LANE), lambda n: (n, 0, 0))],
        out_specs=out_spec, out_shape=out_shape,
        scratch_shapes=[pltpu.VMEM((s_len // GRID_W * (GRID_W + 2 * POOL_CPAD), LANE), F32),
                        pltpu.VMEM((s_len + 16 * GRID_W, LANE), F32)],
    )(src)


def _group_weight(w_ref):
    return jnp.concatenate([w_ref[k] for k in range(N_CHIPS)], axis=0)


def _pool_mm_fwd(dm, wp, scale):
    nb, rows, _ = dm.shape
    _, ng, pq, pg = wp.shape
    cpg = pg // LANE
    tm = _row_tile(rows, 2048)

    def body(d_ref, w_ref, s_ref, y_ref):
        _put_chunks(y_ref, _dot(_cat(d_ref, cpg), _group_weight(w_ref)) * s_ref[...], cpg)

    cspec = pl.BlockSpec((cpg, tm, LANE), lambda i, g: (g, i, 0))
    return _call(
        body, name="pool_mm_fwd", grid=(rows // tm, ng),
        in_specs=[cspec, pl.BlockSpec((N_CHIPS, None, pq, pg), lambda i, g: (0, g, 0, 0)),
                  pl.BlockSpec((1, pg), lambda i, g: (0, g))],
        out_specs=cspec, out_shape=jax.ShapeDtypeStruct((nb, rows, LANE), BF16),
    )(dm, wp, scale)


def _pool_mm_bwd(dy, dm, wp, scale, gbuf, row0):
    nb, rows, _ = dm.shape
    _, ng, pq, pg = wp.shape
    cpg = pg // LANE
    tm = _row_tile(rows, 1024)
    nt = rows // tm
    assert gbuf.shape[2] == 2 * pg and row0 % pq == 0

    def body(dy_ref, d_ref, w_ref, s_ref, buf_ref, dd_ref, dwp_ref, dsc_ref, acc):
        i = pl.program_id(1)

        @pl.when(i == 0)
        def _():
            acc[...] = jnp.zeros_like(acc)
            dsc_ref[...] = jnp.zeros_like(dsc_ref)

        dyv = _cat(dy_ref, cpg)
        dc = _cat(d_ref, cpg)
        w = _group_weight(w_ref)
        dsc_ref[...] += jnp.sum(dyv * _dot(dc, w), axis=0, keepdims=True)
        dyp = (dyv * s_ref[...]).astype(BF16)
        _put_chunks(dd_ref, _dot_nt(dyp, w), cpg)
        acc[...] += _dot_tn(dc, dyp)

        @pl.when(i == nt - 1)
        def _():
            dwp_ref[...] = acc[...].reshape(N_CHIPS, pq, pg)

    cspec = pl.BlockSpec((cpg, tm, LANE), lambda g, i: (g, i, 0))
    sspec = pl.BlockSpec((1, pg), lambda g, i: (0, g))
    return _call(
        body, name="pool_mm_bwd", grid=(ng, nt),
        in_specs=[cspec, cspec, pl.BlockSpec((N_CHIPS, None, pq, pg), lambda g, i: (0, g, 0, 0)), sspec, ANY_SPEC],
        out_specs=[cspec, pl.BlockSpec((N_CHIPS, pq, pg), lambda g, i: (0, row0 // pq + g // 2, g % 2)), sspec],
        out_shape=[jax.ShapeDtypeStruct((nb, rows, LANE), F32), jax.ShapeDtypeStruct(gbuf.shape, F32),
                   jax.ShapeDtypeStruct((1, ng * pg), F32)],
        scratch_shapes=[pltpu.VMEM((pg, pg), F32)],
        input_output_aliases={4: 1},
    )(dy, dm, wp, scale, gbuf)


def _adamw_math(w, g, m, v):
    nm = ADAM_B1 * m + (1.0 - ADAM_B1) * g
    nv = ADAM_B2 * v + (1.0 - ADAM_B2) * jnp.square(g)
    m_hat = nm / (1.0 - ADAM_B1 ** ADAM_STEP)
    v_hat = nv / (1.0 - ADAM_B2 ** ADAM_STEP)
    return -ADAM_LR * (m_hat / (jnp.sqrt(v_hat) + ADAM_EPS) + ADAM_WD * w), nm, nv


def _adamw_param(w3, m3, v3, gsrcs, pick, tm, name, after=None):
    n_blk, rows, cols = w3.shape
    ng = len(gsrcs)

    def body(*refs):
        w_ref, m_ref, v_ref = refs[:3]
        g_refs = refs[3:3 + ng]
        go_ref, d_ref, nm_ref, nv_ref = refs[-4:]
        g = pick(pl.program_id(0), [r[...] for r in g_refs])
        go_ref[...] = g
        d_ref[...], nm_ref[...], nv_ref[...] = _adamw_math(w_ref[...], g, m_ref[...], v_ref[...])

    spec = pl.BlockSpec((None, tm, cols), lambda n, i: (n, i, 0))
    extra = [] if after is None else [after]
    return _call(
        body, name=name, grid=(n_blk, rows // tm),
        in_specs=[spec] * 3 + [pl.BlockSpec(shape, imap) for _, shape, imap in gsrcs] + [ANY_SPEC] * len(extra),
        out_specs=[spec] * 4, out_shape=[jax.ShapeDtypeStruct(w3.shape, F32)] * 4,
    )(w3, m3, v3, *[a for a, _, _ in gsrcs], *extra)


def _adamw_small(quads):
    n = len(quads)

    def body(*refs):
        ins, outs = refs[:4 * n], refs[4 * n:]
        for k in range(n):
            w, g, m, v = (r[...] for r in ins[4 * k:4 * k + 4])
            outs[3 * k][...], outs[3 * k + 1][...], outs[3 * k + 2][...] = _adamw_math(w, g, m, v)

    flat = [a for q in quads for a in q]
    res = _call(body, name="adamw_small", grid=(1,),
                in_specs=[pl.BlockSpec(a.shape, lambda i: (0, 0)) for a in flat],
                out_specs=[pl.BlockSpec(q[0].shape, lambda i: (0, 0)) for q in quads for _ in range(3)],
                out_shape=[jax.ShapeDtypeStruct(q[0].shape, F32) for q in quads for _ in range(3)])(*flat)
    return [tuple(res[3 * k:3 * k + 3]) for k in range(n)]


def _place():
    return lax.axis_index("x"), lax.axis_index("y"), lax.axis_index("c")


def _other_chips(x, y):
    return [(1 - x, y), (x, 1 - y), (1 - x, 1 - y)]


def _own_slab(a, devices=False):
    x, y, c = _place()
    n, me = (N_DEV, 4 * x + 2 * y + c) if devices else (N_CHIPS, 2 * x + y)
    return lax.dynamic_update_slice(lax.empty((n,) + a.shape, a.dtype), a[None], (me, 0, 0))


def _gather_chips(arrays, name):
    n = len(arrays)
    halves = [a.shape[0] // 2 for a in arrays]
    for a, h in zip(arrays, halves):
        assert 2 * h == a.shape[0] and h % (32 // a.dtype.itemsize) == 0
    lands = [_own_slab(a) for a in arrays]

    def body(*refs):
        outs = refs[n:2 * n]
        send_sems, recv_sems = refs[2 * n:]
        x, y, c = _place()
        me = 2 * x + y
        chips = _other_chips(x, y)

        def mine(k):
            return pl.ds(c * halves[k], halves[k])

        def theirs(k):
            return pl.ds((1 - c) * halves[k], halves[k])

        def push(k, j, src, dst, to):
            return pltpu.make_async_remote_copy(src_ref=src, dst_ref=dst, send_sem=send_sems.at[6 * k + j],
                                                recv_sem=recv_sems.at[6 * k + j], device_id=to, device_id_type=MESH)

        started = []
        for j, (cx, cy) in enumerate(chips):
            for k in range(n):
                own = outs[k].at[me, mine(k)]
                cp = push(k, j, own, own, (cx, cy, c))
                cp.start()
                started.append(cp)
        for j, (cx, cy) in enumerate(chips):
            for k in range(n):
                slab = outs[k].at[2 * cx + cy, mine(k)]
                push(k, j, slab, slab, (x, y, c)).wait_recv()
                fwd = push(k, 3 + j, slab, slab, (x, y, 1 - c))
                fwd.start()
                started.append(fwd)
        for j, (cx, cy) in enumerate(chips):
            for k in range(n):
                slab = outs[k].at[2 * cx + cy, theirs(k)]
                push(k, 3 + j, slab, slab, (x, y, c)).wait_recv()
        for cp in started:
            cp.wait_send()

    return _call(
        body, name=name, in_specs=[ANY_SPEC] * n, out_specs=[ANY_SPEC] * n,
        out_shape=[jax.ShapeDtypeStruct(a.shape, a.dtype) for a in lands],
        input_output_aliases={k: k for k in range(n)},
        scratch_shapes=[pltpu.SemaphoreType.DMA((6 * n,)), pltpu.SemaphoreType.DMA((6 * n,))],
    )(*lands)


HBM_SPEC = pl.BlockSpec(memory_space=pltpu.HBM)
SEM_SPEC = pl.BlockSpec(memory_space=pltpu.SEMAPHORE)
SIDE_EFFECT = pltpu.SideEffectType.DATAFLOW_SIDE_EFFECTING


def _n_peers(kind):
    return N_DEV - 1 if kind == "devices" else N_CHIPS - 1


def _push_copies(src_refs, land_refs, send_sems, recv_sems, kind):
    x, y, c = _place()
    if kind == "devices":
        me = 4 * x + 2 * y + c
        peers = [((me + j) % N_DEV, None) for j in range(1, N_DEV)]
        peers = [((to // 4, (to // 2) % 2, to % 2), None) for to, _ in peers]
    else:
        me = 2 * x + y
        peers = [((cx, cy, c), 2 * cx + cy) for cx, cy in _other_chips(x, y)]
    n = len(peers)
    copies = []
    for j, (dev, slab) in enumerate(peers):
        for k, (src, land) in enumerate(zip(src_refs, land_refs)):
            copies.append(pltpu.make_async_remote_copy(
                src_ref=src.at[slab] if kind == "slab" else src, dst_ref=land.at[me], send_sem=send_sems.at[n * k + j],
                recv_sem=recv_sems.at[n * k + j], device_id=dev, device_id_type=MESH))
    return copies


def _push_start(srcs, lands, kind, after, name):
    n = len(srcs)

    def body(*refs):
        src_refs, land_refs = refs[:n], refs[n:2 * n]
        send_sems, recv_sems = refs[2 * n + 1], refs[2 * n + 2]
        token = refs[-1]
        for cp in _push_copies(src_refs, land_refs, send_sems, recv_sems, kind):
            cp.start()
        token[...] = jnp.zeros_like(token)

    bufs = [pltpu.with_memory_space_constraint(a, pltpu.HBM) for a in list(srcs) + list(lands)]
    res = _call(
        body, name=name,
        out_shape=[pltpu.SemaphoreType.DMA((_n_peers(kind) * n,)), pltpu.SemaphoreType.DMA((_n_peers(kind) * n,))]
        + [pltpu.HBM(a.shape, a.dtype) for a in bufs] + [jax.ShapeDtypeStruct((SUB, LANE), F32)],
        in_specs=[HBM_SPEC] * (2 * n) + [ANY_SPEC],
        out_specs=[SEM_SPEC, SEM_SPEC] + [HBM_SPEC] * (2 * n) + [pl.BlockSpec(memory_space=pltpu.VMEM)],
        input_output_aliases={i: 2 + i for i in range(2 * n)},
        compiler_params=pltpu.CompilerParams(has_side_effects=SIDE_EFFECT),
    )(*bufs, after)
    return res[0], res[1], list(res[2:2 + n]), list(res[2 + n:2 + 2 * n]), res[-1]


def _push_wait(send_sems, recv_sems, srcs, lands, kind, after, name):
    n = len(srcs)

    def body(*refs):
        src_refs, land_refs = refs[:n], refs[n:2 * n]
        send_sems, recv_sems = refs[2 * n], refs[2 * n + 1]
        for cp in _push_copies(src_refs, land_refs, send_sems, recv_sems, kind):
            cp.wait_send()
            cp.wait_recv()

    res = _call(
        body, name=name,
        out_shape=[pltpu.HBM(a.shape, a.dtype) for a in list(srcs) + list(lands)],
        in_specs=[HBM_SPEC] * (2 * n) + [SEM_SPEC, SEM_SPEC, ANY_SPEC],
        out_specs=[HBM_SPEC] * (2 * n),
        input_output_aliases={i: i for i in range(2 * n)},
        compiler_params=pltpu.CompilerParams(has_side_effects=SIDE_EFFECT),
    )(*srcs, *lands, send_sems, recv_sems, after)
    return list(res[n:])


def _sibling_swap(g):
    _, rows, w = g.shape
    half = rows // 2

    def body(g_ref, out_ref, send_sem, recv_sem):
        x, y, c = _place()
        cp = pltpu.make_async_remote_copy(src_ref=g_ref.at[:, pl.ds((1 - c) * half, half)], dst_ref=out_ref,
                                          send_sem=send_sem, recv_sem=recv_sem, device_id=(x, y, 1 - c), device_id_type=MESH)
        cp.start()
        cp.wait()

    return _call(body, name="rs_sibling_swap", in_specs=[ANY_SPEC], out_specs=ANY_SPEC,
                 out_shape=jax.ShapeDtypeStruct((N_CHIPS, half, w), F32),
                 scratch_shapes=[pltpu.SemaphoreType.DMA, pltpu.SemaphoreType.DMA])(g)


def _pair_add(g, got, place):
    _, rows, w = g.shape
    half = rows // 2
    tm = _row_tile(half, RS_TILE)
    nt = half // tm

    def body(p_ref, a_ref, b_ref, o_ref, own_ref):
        v = a_ref[...] + b_ref[...]
        o_ref[...] = v.astype(BF16)

        @pl.when(pl.program_id(1) == p_ref[1])
        def _():
            own_ref[...] = v

    return _call(
        body, name="rs_pair_add",
        grid_spec=pltpu.PrefetchScalarGridSpec(
            num_scalar_prefetch=1, grid=(nt, N_CHIPS),
            in_specs=[pl.BlockSpec((None, tm, w), lambda i, s, p: (s, p[0] * nt + i, 0)),
                      pl.BlockSpec((None, tm, w), lambda i, s, p: (s, i, 0))],
            out_specs=[pl.BlockSpec((None, tm, w), lambda i, s, p: (s, i, 0)),
                       pl.BlockSpec((tm, w), lambda i, s, p: (i, 0))]),
        out_shape=[jax.ShapeDtypeStruct((N_CHIPS, half, w), BF16), jax.ShapeDtypeStruct((half, w), F32)],
    )(place, g, got)


def _sum_chips(parts, own, place):
    _, half, w = parts.shape
    tm = _row_tile(half, RS_TILE)
    nt = half // tm

    def body(p_ref, parts_ref, own_ref, o_ref):
        me = p_ref[1]
        t = [jnp.where(me == q, own_ref[...], parts_ref[q].astype(F32)) for q in range(N_CHIPS)]
        o_ref[...] = (t[0] + t[1]) + (t[2] + t[3])

    return _call(
        body, name="rs_sum_chips",
        grid_spec=pltpu.PrefetchScalarGridSpec(
            num_scalar_prefetch=1, grid=(nt,),
            in_specs=[pl.BlockSpec((N_CHIPS, tm, w), lambda i, p: (0, i, 0)), pl.BlockSpec((tm, w), lambda i, p: (i, 0))],
            out_specs=pl.BlockSpec((tm, w), lambda i, p: (p[0] * nt + i, 0))),
        out_shape=jax.ShapeDtypeStruct((2 * half, w), F32),
    )(place, parts, own)


def _sibling_gather(red):
    rows, w = red.shape
    half = rows // 2

    def body(in_ref, out_ref, send_sem, recv_sem):
        x, y, c = _place()
        mine = out_ref.at[pl.ds(c * half, half)]
        cp = pltpu.make_async_remote_copy(src_ref=mine, dst_ref=mine, send_sem=send_sem, recv_sem=recv_sem,
                                          device_id=(x, y, 1 - c), device_id_type=MESH)
        cp.start()
        other = out_ref.at[pl.ds((1 - c) * half, half)]
        pltpu.make_async_remote_copy(src_ref=other, dst_ref=other, send_sem=send_sem, recv_sem=recv_sem,
                                     device_id=(x, y, c), device_id_type=MESH).wait_recv()
        cp.wait_send()

    return _call(body, name="rs_sibling_gather", in_specs=[ANY_SPEC], out_specs=ANY_SPEC,
                 out_shape=jax.ShapeDtypeStruct(red.shape, F32), input_output_aliases={0: 0},
                 scratch_shapes=[pltpu.SemaphoreType.DMA, pltpu.SemaphoreType.DMA])(red)


def _rs_begin(g, place, name):
    pair, own = _pair_add(g, _sibling_swap(g), place)
    send, recv, pair, parts, token = _push_start([pair], [jnp.zeros_like(pair)], "slab", own, name + "_start")
    return (send, recv, pair, parts, own), token


def _rs_end(state, place, after, name):
    send, recv, pair, parts, own = state
    (parts,) = _push_wait(send, recv, pair, parts, "slab", after, name + "_wait")
    return _sibling_gather(_sum_chips(parts, own, place))


WEIGHTS = ("c_ctx", "w_mod", "b_mod", "w_in", "w_out", "ln_g", "ln_b", "conv_w", "conv_b", "lru_wa", "lru_ba", "lru_wx",
           "lru_bx", "lru_lam", "pool_w", "pool_scale")
SMALL_GATHERED = ("conv_w", "lru_ba", "lru_bx", "lru_lam", "pool_scale")
SMALL_UPDATED = ("c_ctx", "b_mod", "ln_g", "ln_b", "conv_w", "conv_b", "lru_ba", "lru_bx", "lru_lam", "pool_scale")


def kernel(x, c, ctx, c_ctx, w_mod, b_mod, w_in, w_out, ln_g, ln_b, conv_w, conv_b, lru_wa, lru_ba, lru_wx, lru_bx, lru_lam, pool_w, pool_scale, loss_target, m_c_ctx, m_w_mod, m_b_mod, m_w_in, m_w_out, m_ln_g, m_ln_b, m_conv_w, m_conv_b, m_lru_wa, m_lru_ba, m_lru_wx, m_lru_bx, m_lru_lam, m_pool_w, m_pool_scale, v_c_ctx, v_w_mod, v_b_mod, v_w_in, v_w_out, v_ln_g, v_ln_b, v_conv_w, v_conv_b, v_lru_wa, v_lru_ba, v_lru_wx, v_lru_bx, v_lru_lam, v_pool_w, v_pool_scale):
    weights = dict(c_ctx=c_ctx, w_mod=w_mod, b_mod=b_mod, w_in=w_in, w_out=w_out, ln_g=ln_g, ln_b=ln_b, conv_w=conv_w,
                   conv_b=conv_b, lru_wa=lru_wa, lru_ba=lru_ba, lru_wx=lru_wx, lru_bx=lru_bx, lru_lam=lru_lam,
                   pool_w=pool_w, pool_scale=pool_scale)
    mom1 = dict(c_ctx=m_c_ctx, w_mod=m_w_mod, b_mod=m_b_mod, w_in=m_w_in, w_out=m_w_out, ln_g=m_ln_g, ln_b=m_ln_b,
                conv_w=m_conv_w, conv_b=m_conv_b, lru_wa=m_lru_wa, lru_ba=m_lru_ba, lru_wx=m_lru_wx, lru_bx=m_lru_bx,
                lru_lam=m_lru_lam, pool_w=m_pool_w, pool_scale=m_pool_scale)
    mom2 = dict(c_ctx=v_c_ctx, w_mod=v_w_mod, b_mod=v_b_mod, w_in=v_w_in, w_out=v_w_out, ln_g=v_ln_g, ln_b=v_ln_b,
                conv_w=v_conv_w, conv_b=v_conv_b, lru_wa=v_lru_wa, lru_ba=v_lru_ba, lru_wx=v_lru_wx, lru_bx=v_lru_bx,
                lru_lam=v_lru_lam, pool_w=v_pool_w, pool_scale=v_pool_scale)
    xs, cx, target = x[0], ctx[0], loss_target[0]
    s_len, d = xs.shape
    es = w_out.shape[1]
    e = es * N_CHIPS
    nb = e // LANE
    c3 = w_mod.shape[2]
    n4 = w_in.shape[2]
    pq, pg = pool_w.shape[2], pool_w.shape[3]
    ng = len(POOL_WINDOWS)
    width = n4
    assert width == d and 2 * pg == width and 2 * nb * LANE == N_CHIPS * width and d % (2 * N_CHIPS) == 0
    px, py, pc = _place()
    place = jnp.stack([pc, 2 * px + py]).astype(jnp.int32)
    cctx2 = c_ctx[None, :]

    eq = e // N_CHIPS
    small_rows = [(conv_w[0], 0), (lru_ba[0], CONV_TAPS), (lru_bx[0], CONV_TAPS + 2), (lru_lam[0], CONV_TAPS + 4),
                  (pool_scale, CONV_TAPS + 6)]
    small = _rows_kernel([(a, r, 0) for a, r in small_rows], 2 * SUB, eq, "pack_small_weights")
    wm_g, win0, sg = _gather_chips([w_mod.astype(BF16).reshape(DEPTH * d, c3), w_in[0].astype(BF16), small], "gather_weights0")
    later = [w_out[0].astype(BF16), w_in[1].astype(BF16), w_out[1].astype(BF16), pool_w.astype(BF16).reshape(ng * pq, pg)]
    w_send, w_recv, later, later_lands, w_token = _push_start(
        later, [_own_slab(a) for a in later], "same", sg, "gather_weights1_start")
    wm_all = wm_g.reshape(N_CHIPS, DEPTH, d, c3)
    full = {n: jnp.swapaxes(sg[:, r:r + a.shape[0]], 0, 1).reshape(a.shape[0], e)
            for n, (a, r) in zip(SMALL_GATHERED, small_rows)}
    wa_b, wx_b = lru_wa[0].astype(BF16), lru_wx[0].astype(BF16)
    lru_args = (full["conv_w"], conv_b, wa_b, wx_b, full["lru_ba"], full["lru_bx"], full["lru_lam"])
    scale_f = full["pool_scale"]

    mod = _mod_fwd(c + w_token[0:1, 0:1], cctx2, wm_all, b_mod[:, None, :])

    def mod_parts(l, row):
        v = mod[l, row]
        return v[None, :d], 1.0 + v[None, d:2 * d], v[None, 2 * d:]

    sh0, sc0, gt0 = mod_parts(0, 0)
    shc, scc, _ = mod_parts(0, 1)
    sh1, sc1, gt1 = mod_parts(1, 0)
    lg = [ln_g[l][None, :] for l in range(DEPTH)]
    lb = [ln_b[l][None, :] for l in range(DEPTH)]

    uu0, ug0 = _inproj_fwd(xs, sc0, sh0, win0, "inproj_fwd0")
    uc0 = _inproj_fwd(cx, scc, shc, win0[:2], "inproj_fwd_ctx")
    y0 = _rglru_fwd(uu0, uc0, *lru_args)
    wout0_g, win1, wout1_g, wp_g = _push_wait(w_send, w_recv, later, later_lands, "same", y0, "gather_weights1_wait")
    win = [win0, win1]
    wout = [wout0_g.reshape(e, d), wout1_g.reshape(e, d)]
    wp = wp_g.reshape(N_CHIPS, ng, pq, pg)
    br0, x1 = _outproj_fwd(y0, ug0, xs, gt0, wout[0], lg[0], lb[0], None, "outproj_fwd0")
    uu1, ug1 = _inproj_fwd(x1, sc1, sh1, win[1], "inproj_fwd1")
    d1 = _pool_map(uu1, nb, False, True, "pool_fwd")
    y1 = _pool_mm_fwd(d1, wp, scale_f)
    br1, dxo, loss_part = _outproj_fwd(y1, ug1, x1, gt1, wout[1], lg[1], lb[1], target, "outproj_fwd1")

    row_wout = d
    row_tail = d + es
    wq = 2 * (nb // N_CHIPS) * LANE * LANE // width
    whole = lambda r: (r + 2 * RS_TILE - 1) // (2 * RS_TILE) * (2 * RS_TILE)
    rows1 = whole(row_tail + pg // 2)
    rows0 = whole(row_tail + 2 * wq)
    gbuf1 = jnp.zeros((N_CHIPS, rows1, width), F32)
    gbuf0 = jnp.zeros((N_CHIPS, rows0, width), F32)

    dy1, dg1, dxres1, dbr1, dlg1, dlb1, dgt1 = _outproj_bwd(dxo, x1, br1, y1, ug1, gt1, lg[1], wout[1], "outproj_bwd1")
    gbuf1 = _outproj_bwd_w(y1, ug1, dbr1, gbuf1, row_wout, "outproj_bwd_w1")
    dd1, gbuf1, dscale = _pool_mm_bwd(dy1, d1, wp, scale_f, gbuf1, row_tail)
    du1 = _pool_map(dd1, nb, True, False, "pool_bwd")
    dx1, dsc1, dsh1 = _inproj_bwd_x([du1, dg1], x1, dxres1, sc1, win[1], "inproj_bwd_x1")
    gbuf1 = _inproj_bwd_w(x1, sc1, sh1, [du1, dg1], None, gbuf1, "inproj_bwd_w1")
    rs1, token1 = _rs_begin(gbuf1, place, "rs_exchange1")

    dy0, dg0, dxres0, dbr0, dlg0, dlb0, dgt0 = _outproj_bwd(dx1, xs, br0, y0, ug0, gt0 + token1[0:1, 0:1], lg[0], wout[0],
                                                            "outproj_bwd0")
    gbuf0 = _outproj_bwd_w(y0, ug0, dbr0, gbuf0, row_wout, "outproj_bwd_w0")
    du0, duc, dconv_w, dconv_b, dwa, dwx, dba, dbx, dlam = _rglru_bwd(uu0, uc0, dy0, *lru_args)
    dwin0c = _inproj_bwd_w(cx, scc, shc, [duc, jnp.zeros_like(duc)], None, None, "inproj_bwd_w_ctx")
    gbuf0 = _inproj_bwd_w(xs, sc0, sh0, [du0, dg0], dwin0c, gbuf0, "inproj_bwd_w0")

    def quarter(dw):
        t = dw.reshape(2, N_CHIPS, nb // N_CHIPS, LANE, LANE)
        return jnp.transpose(t, (1, 3, 0, 2, 4)).reshape(N_CHIPS, LANE, 2 * (nb // N_CHIPS) * LANE).reshape(N_CHIPS, wq, width)

    tail0 = jnp.concatenate([quarter(dwa), quarter(dwx)], axis=1)
    gbuf0 = lax.dynamic_update_slice(gbuf0, tail0, (0, row_tail, 0))
    red1 = _rs_end(rs1, place, gbuf0, "rs_exchange1")
    rs0, token0 = _rs_begin(gbuf0, place, "rs_exchange0")
    grad_x, dsc0, dsh0 = _inproj_bwd_x([du0, dg0], xs, dxres0, sc0 + token0[0:1, 0:1], win[0], "inproj_bwd_x0")
    dscc, dshc = _inproj_bwd_x([duc], cx, None, scc, win[0][:2], "inproj_bwd_x_ctx")

    k0 = VEC_KINDS
    vec = _rows_kernel(
        [(c, 0, 0), (loss_part, 0, d), (dsh0, 1, 0), (dsc0, 1, d), (dgt0, 1, 2 * d), (dshc, 2, 0), (dscc, 2, d),
         (dsh1, 3, 0), (dsc1, 3, d), (dgt1, 3, 2 * d),
         (dconv_b, k0, 0), (dlg0, k0, e), (dscale, k0 + 1, 0), (dlg1, k0 + 1, e), (dlb0, k0 + 2, 0), (dlb1, k0 + 2, d),
         (dconv_w, k0 + 3, 0), (dba, k0 + 7, 0), (dbx, k0 + 9, 0), (dlam, k0 + 11, 0)], VEC_ROWS, 3 * d, "pack_vec")
    v_send, v_recv, vec_l, vec_lands, v_token = _push_start([vec], [_own_slab(vec, devices=True)], "devices", vec,
                                                            "gather_devices_start")
    red0 = _rs_end(rs0, place, v_token, "rs_exchange0")
    quarters = red0[row_tail:row_tail + 2 * wq]
    q_send, q_recv, q_src, q_lands, q_token = _push_start([quarters], [_own_slab(quarters)], "same", red0,
                                                          "gather_replicated_start")

    tmw = _row_tile(d, 256)
    red_src = lambda red, r0, tm: (red, (tm, width), lambda n, i: (r0 // tm + i, 0))
    by_layer = lambda n, gs: jnp.where(n == 0, gs[0], gs[1])
    outs = {}
    outs["w_in"] = _adamw_param(w_in, m_w_in, v_w_in, [red_src(red0, 0, tmw), red_src(red1, 0, tmw)], by_layer, tmw, "adamw_w_in",
                                after=q_token)
    outs["w_out"] = _adamw_param(w_out, m_w_out, v_w_out, [red_src(red0, row_wout, tmw), red_src(red1, row_wout, tmw)],
                                 by_layer, tmw, "adamw_w_out")
    pw = [a.reshape(ng, pq, pg) for a in (pool_w, m_pool_w, v_pool_w)]
    outs["pool_w"] = [o.reshape(pool_w.shape) for o in _adamw_param(
        *pw, [(red1, (pq, pg), lambda n, i: (row_tail // pq + n // 2, n % 2))], lambda n, gs: gs[0], pq, "adamw_pool_w")]

    (gathered,) = _push_wait(v_send, v_recv, vec_l, vec_lands, "devices", outs["w_out"][1], "gather_devices_wait")
    gt_all = jnp.swapaxes(gathered, 0, 1)
    g_wmod = _mod_bwd_shard(gt_all, cctx2, place, c3)
    g_bmod, g_cctx, sq_err, g_small = _mod_bwd_rep(gt_all, cctx2, wm_all)
    loss = sq_err[0, 0] * (0.5 / d)
    outs["w_mod"] = _adamw_param(w_mod, m_w_mod, v_w_mod, [(g_wmod, (None, tmw, c3), lambda n, i: (n, i, 0))],
                                 lambda n, gs: gs[0], tmw, "adamw_w_mod")
    (rep,) = _push_wait(q_send, q_recv, q_src, q_lands, "same", outs["w_mod"][1], "gather_replicated_wait")
    bq = nb // N_CHIPS
    rep_src = lambda r0: (rep, (None, LANE, bq * LANE), lambda n, i: (n % N_CHIPS, r0 // LANE, n // N_CHIPS))
    stack = lambda n, gs: jnp.concatenate([gs[0][:, k * LANE:(k + 1) * LANE] for k in range(bq)], axis=0)
    for name, r0, trio in (("lru_wa", 0, (lru_wa, m_lru_wa, v_lru_wa)), ("lru_wx", wq, (lru_wx, m_lru_wx, v_lru_wx))):
        blocks = [a.reshape(2 * N_CHIPS, bq * LANE, LANE) for a in trio]
        outs[name] = [o.reshape(lru_wa.shape) for o in _adamw_param(*blocks, [rep_src(r0)], stack, bq * LANE, "adamw_" + name)]

    g_small = dict(g_small, c_ctx=g_cctx, b_mod=g_bmod)
    for n in SMALL_GATHERED:
        g_small[n] = lax.dynamic_slice_in_dim(g_small[n], place[1] * eq, eq, axis=1)
    as2d = lambda a: a.reshape(-1, a.shape[-1])
    quads = [(as2d(weights[n]), g_small[n], as2d(mom1[n]), as2d(mom2[n])) for n in SMALL_UPDATED]
    for n, (q, res) in zip(SMALL_UPDATED, zip(quads, _adamw_small(quads))):
        outs[n] = [a.reshape(weights[n].shape) for a in (q[1],) + res]

    result = [loss, grad_x[None]]
    for j in range(4):
        result += [outs[n][j] for n in WEIGHTS]
    return tuple(result)
```

```python
import jax
import jax.numpy as jnp
from jax import lax
from jax.experimental import pallas as pl
from jax.experimental.pallas import tpu as pltpu

F32 = jnp.float32
BF16 = jnp.bfloat16
LANE = 128
SUB = 8
GRID_W = 64
POOL_WINDOWS = (2, 4, 8, 16)
LRU_C = 8.0
DEPTH = 2
ALPHA = float((2 * DEPTH) ** 0.25)
LN_EPS = 1e-5
ADAM_LR, ADAM_B1, ADAM_B2, ADAM_EPS, ADAM_WD, ADAM_STEP = 0.001, 0.9, 0.999, 1e-08, 0.01, 10
N_CHIPS = 4
N_DEV = 8
MESH = pl.DeviceIdType.MESH
ROW_TILE = 512
GATE_TILE = 2048
GATE_BWD_TILE = 1024
GATE_UNROLL = 1
CONV_TAPS = 4
CONV_LEFT = 2
PAD = 8
SCAN_UNROLL = 32
RS_TILE = 448
LN_ROWS = 128
POOL_CPAD = 16
VEC_KINDS = 4


def _call(body, **kw):
    return pl.pallas_call(body, **kw)


def _dot(a, b):
    return jnp.dot(a, b, preferred_element_type=F32)


def _dot_nt(a, b):
    return lax.dot_general(a, b, (((1,), (1,)), ((), ())), preferred_element_type=F32)


def _dot_tn(a, b):
    return lax.dot_general(a, b, (((0,), (0,)), ((), ())), preferred_element_type=F32)


def _sigmoid(v):
    return 0.5 * (jnp.tanh(0.5 * v) + 1.0)


def _silu(v):
    return v * _sigmoid(v)


def _dsilu(v):
    s = _sigmoid(v)
    return s * (1.0 + v * (1.0 - s))


def _log_sigmoid(v):
    z = jnp.exp(-jnp.abs(v))
    return jnp.minimum(v, 0.0) - jnp.where(z < 1e-4, z * (1.0 - 0.5 * z), jnp.log(1.0 + z))


def _one_minus_sq(la, a):
    return -jnp.tanh(la) * (a * a + 1.0)


def _cat(ref, n):
    return jnp.concatenate([ref[k] for k in range(n)], axis=1)


def _put_chunks(ref, val, n, base=0):
    for k in range(n):
        ref[base + k] = val[:, k * LANE:(k + 1) * LANE].astype(ref.dtype)


def _row_tile(rows, want):
    t = min(rows, want)
    assert rows % t == 0
    return t


ANY_SPEC = pl.BlockSpec(memory_space=pl.ANY)


def _mod_fwd(cvec, cctx, wm, bm):
    ns, nl, d, c3 = wm.shape

    def body(c_ref, cx_ref, w_ref, b_ref, o_ref):
        cc = jnp.concatenate([c_ref[...], cx_ref[...], jnp.zeros((SUB - 2, d), F32)], axis=0)
        o_ref[...] = _dot(_silu(cc).astype(BF16), w_ref[...]) + b_ref[...]

    return _call(
        body, name="mod_fwd", grid=(nl, ns),
        in_specs=[pl.BlockSpec((1, d), lambda l, s: (0, 0)),
                  pl.BlockSpec((1, d), lambda l, s: (0, 0)),
                  pl.BlockSpec((None, None, d, c3), lambda l, s: (s, l, 0, 0)),
                  pl.BlockSpec((None, 1, c3), lambda l, s: (l, 0, s))],
        out_specs=pl.BlockSpec((None, 8, c3), lambda l, s: (l, 0, s)),
        out_shape=jax.ShapeDtypeStruct((nl, 8, ns * c3), F32),
    )(cvec, cctx, wm, bm)


def _rows_kernel(parts, rows, cols, name):
    def body(*refs):
        o_ref = refs[-1]
        o_ref[...] = jnp.zeros_like(o_ref)
        for ref, (a, r0, c0) in zip(refs[:-1], parts):
            for k in range(a.shape[0]):
                o_ref[r0 + k:r0 + k + 1, c0:c0 + a.shape[1]] = ref[k:k + 1, :]

    return _call(body, name=name, grid=(1,),
                 in_specs=[pl.BlockSpec(a.shape, lambda i: (0, 0)) for a, _, _ in parts],
                 out_specs=pl.BlockSpec((rows, cols), lambda i: (0, 0)),
                 out_shape=jax.ShapeDtypeStruct((rows, cols), F32))(*[a for a, _, _ in parts])


def _mod_bwd_shard(gt, cctx, place, c3):
    d = cctx.shape[1]

    def body(p_ref, cs_ref, dm_ref, dmx_ref, cx_ref, o_ref):
        l = pl.program_id(0)
        lhs = jnp.concatenate([_silu(cs_ref[...]), _silu(cx_ref[...]), jnp.zeros((7, d), F32)], axis=0).astype(BF16)
        dmx = jnp.where(l == 0, jnp.sum(dmx_ref[...], axis=0, keepdims=True), 0.0)
        rhs = jnp.concatenate([dm_ref[...], dmx, jnp.zeros((7, c3), F32)], axis=0).astype(BF16)
        o_ref[...] = _dot_tn(lhs, rhs)

    return _call(
        body, name="mod_bwd_shard",
        grid_spec=pltpu.PrefetchScalarGridSpec(
            num_scalar_prefetch=1, grid=(DEPTH,),
            in_specs=[pl.BlockSpec((None, N_DEV, d), lambda l, p: (0, 0, 0)),
                      pl.BlockSpec((None, N_DEV, c3), lambda l, p: (1 + 2 * l, 0, p[1])),
                      pl.BlockSpec((None, N_DEV, c3), lambda l, p: (2, 0, p[1])),
                      pl.BlockSpec((1, d), lambda l, p: (0, 0))],
            out_specs=pl.BlockSpec((None, d, c3), lambda l, p: (l, 0, 0))),
        out_shape=jax.ShapeDtypeStruct((DEPTH, d, c3), F32),
    )(place, gt, gt, gt, cctx)


def _small_layout(d, e):
    k = VEC_KINDS
    return {
        "conv_b": ((1, e), [(0, k, 0)]),
        "ln_g": ((2, d), [(0, k, e), (1, k + 1, e)]),
        "pool_scale": ((1, e), [(0, k + 1, 0)]),
        "ln_b": ((2, d), [(0, k + 2, 0), (1, k + 2, d)]),
        "conv_w": ((CONV_TAPS, e), [(t, k + 3 + t, 0) for t in range(CONV_TAPS)]),
        "lru_ba": ((2, e), [(j, k + 7 + j, 0) for j in range(2)]),
        "lru_bx": ((2, e), [(j, k + 9 + j, 0) for j in range(2)]),
        "lru_lam": ((2, e), [(j, k + 11 + j, 0) for j in range(2)]),
    }


VEC_ROWS = 24


def _mod_bwd_rep(gt, cctx, wm):
    ns, _, d, c3 = wm.shape
    layout = _small_layout(d, ns * c3 - d)
    names = list(layout)

    def body(g_ref, cx_ref, w_ref, db_ref, dc_ref, loss_ref, *small_refs):
        loss_ref[...] = jnp.zeros_like(loss_ref) + jnp.sum(g_ref[0][:, d:d + LANE])
        dm0 = jnp.sum(g_ref[1], axis=0, keepdims=True)
        dmx = jnp.sum(g_ref[2], axis=0, keepdims=True)
        dm1 = jnp.sum(g_ref[3], axis=0, keepdims=True)
        db_ref[0:1, :] = dm0 + dmx
        db_ref[1:2, :] = dm1
        dmxb = jnp.broadcast_to(dmx, (SUB, ns * c3)).astype(BF16)
        acc = jnp.zeros((SUB, d), F32)
        for s in range(ns):
            acc = acc + _dot_nt(dmxb[:, s * c3:(s + 1) * c3], w_ref[s])
        dc_ref[...] = acc[0:1, :] * _dsilu(cx_ref[...])
        for ref, name in zip(small_refs, names):
            shape, places = layout[name]
            for arr_row, vec_row, col0 in places:
                total = jnp.sum(g_ref[vec_row], axis=0, keepdims=True)
                ref[arr_row:arr_row + 1, :] = total[:, col0:col0 + shape[1]]

    outs = _call(
        body, name="mod_bwd_rep", grid=(1,),
        in_specs=[pl.BlockSpec(gt.shape, lambda i: (0, 0, 0)),
                  pl.BlockSpec((1, d), lambda i: (0, 0)),
                  pl.BlockSpec((ns, None, d, c3), lambda i: (0, 0, 0, 0))],
        out_specs=[pl.BlockSpec((DEPTH, ns * c3), lambda i: (0, 0)), pl.BlockSpec((1, d), lambda i: (0, 0)),
                   pl.BlockSpec((1, LANE), lambda i: (0, 0))]
        + [pl.BlockSpec(layout[n][0], lambda i: (0, 0)) for n in names],
        out_shape=[jax.ShapeDtypeStruct((DEPTH, ns * c3), F32), jax.ShapeDtypeStruct((1, d), F32),
                   jax.ShapeDtypeStruct((1, LANE), F32)]
        + [jax.ShapeDtypeStruct(layout[n][0], F32) for n in names],
    )(gt, cctx, wm)
    return outs[0], outs[1], outs[2], dict(zip(names, outs[3:]))


def _inproj_fwd(xin, sc1, sh, w, name):
    rows, d = xin.shape
    ns, _, n4 = w.shape
    cpb = n4 // LANE
    tm = _row_tile(rows, 512)
    assert ns in (2, 4)

    def body(x_ref, sc_ref, sh_ref, w_ref, *o_refs):
        h = (x_ref[...] * sc_ref[...] + sh_ref[...]).astype(BF16)
        for s in range(ns):
            _put_chunks(o_refs[s // 2], _dot(h, w_ref[s]), cpb, base=(s % 2) * cpb)

    spec = pl.BlockSpec((2 * cpb, tm, LANE), lambda i: (0, i, 0))
    dtypes = (F32, BF16)[:ns // 2]
    res = _call(
        body, name=name, grid=(rows // tm,),
        in_specs=[pl.BlockSpec((tm, d), lambda i: (i, 0)),
                  pl.BlockSpec((1, d), lambda i: (0, 0)),
                  pl.BlockSpec((1, d), lambda i: (0, 0)),
                  pl.BlockSpec((ns, d, n4), lambda i: (0, 0, 0))],
        out_specs=[spec] * len(dtypes),
        out_shape=[jax.ShapeDtypeStruct((2 * cpb, rows, LANE), t) for t in dtypes],
    )(xin, sc1, sh, w)
    return res[0] if ns == 2 else tuple(res)


def _inproj_bwd_x(dparts, xin, dxres, sc1, w, name):
    rows, d = xin.shape
    npart = len(dparts)
    e = dparts[0].shape[1]
    ns, _, n4 = w.shape
    per = e // n4
    assert per * npart == ns
    tm = _row_tile(rows, 512)
    has_res = dxres is not None

    def body(*refs):
        dp = refs[:npart]
        x_ref, sc_ref, w_ref = refs[npart:npart + 3]
        rest = refs[npart + 3:]
        if has_res:
            res_ref, dx_ref, dsc_ref, dsh_ref = rest
        else:
            dsc_ref, dsh_ref = rest
        i = pl.program_id(0)
        dh = jnp.zeros((tm, d), F32)
        for p in range(npart):
            v = dp[p][...]
            for q in range(per):
                dh = dh + _dot_nt(v[:, q * n4:(q + 1) * n4], w_ref[p * per + q])

        @pl.when(i == 0)
        def _():
            dsc_ref[...] = jnp.zeros_like(dsc_ref)
            dsh_ref[...] = jnp.zeros_like(dsh_ref)

        dsc_ref[...] += jnp.sum(dh * x_ref[...], axis=0, keepdims=True)
        dsh_ref[...] += jnp.sum(dh, axis=0, keepdims=True)
        if has_res:
            dx_ref[...] = res_ref[...] + dh * sc_ref[...]

    row_spec = pl.BlockSpec((tm, d), lambda i: (i, 0))
    vec_spec = pl.BlockSpec((1, d), lambda i: (0, 0))
    in_specs = [pl.BlockSpec((tm, e), lambda i: (i, 0))] * npart + [row_spec, vec_spec,
                                                                     pl.BlockSpec((ns, d, n4), lambda i: (0, 0, 0))]
    args = list(dparts) + [xin, sc1, w]
    out_specs, out_shape = [vec_spec, vec_spec], [jax.ShapeDtypeStruct((1, d), F32)] * 2
    if has_res:
        in_specs.append(row_spec)
        args.append(dxres)
        out_specs = [row_spec] + out_specs
        out_shape = [jax.ShapeDtypeStruct((rows, d), F32)] + out_shape
    return _call(body, name=name, grid=(rows // tm,), in_specs=in_specs, out_specs=out_specs, out_shape=out_shape)(*args)


def _inproj_bwd_w(xin, sc1, sh, dparts, init, gbuf, name):
    rows, d = xin.shape
    npart = len(dparts)
    e = dparts[0].shape[1]
    n4 = e // 2
    ns = 2 * npart
    tm = _row_tile(rows, 1024)
    nt = rows // tm
    has_init = init is not None
    into = gbuf is not None
    assert not into or (ns == N_CHIPS and gbuf.shape[2] == n4)

    def body(*refs):
        x_ref, sc_ref, sh_ref = refs[:3]
        dp = refs[3:3 + npart]
        init_ref = refs[3 + npart] if has_init else None
        o_ref = refs[-1]
        s, i = pl.program_id(0), pl.program_id(1)
        h = (x_ref[...] * sc_ref[...] + sh_ref[...]).astype(BF16)

        @pl.when(i == 0)
        def _():
            o_ref[...] = init_ref[...] if has_init else jnp.zeros_like(o_ref)

        for p in range(npart):
            @pl.when(s // 2 == p)
            def _(p=p):
                o_ref[...] += _dot_tn(h, dp[p][...])

    in_specs = [pl.BlockSpec((tm, d), lambda s, i: (i, 0)),
                pl.BlockSpec((1, d), lambda s, i: (0, 0)),
                pl.BlockSpec((1, d), lambda s, i: (0, 0))]
    in_specs += [pl.BlockSpec((tm, n4), lambda s, i: (i, s % 2))] * npart
    args = [xin, sc1, sh] + list(dparts)
    o_spec = pl.BlockSpec((None, d, n4), lambda s, i: (s, 0, 0))
    if has_init:
        in_specs.append(o_spec)
        args.append(init)
    extra = {}
    if into:
        in_specs.append(ANY_SPEC)
        args.append(gbuf)
        extra = dict(input_output_aliases={len(args) - 1: 0})
    out_shape = jax.ShapeDtypeStruct(gbuf.shape if into else (ns, d, n4), F32)
    return _call(body, name=name, grid=(ns, nt), in_specs=in_specs, out_specs=o_spec, out_shape=out_shape, **extra)(*args)


def _gated(y_ref, g_ref, nch):
    return jnp.concatenate([(y_ref[k].astype(F32) * _silu(g_ref[k].astype(F32))).astype(BF16) for k in range(nch)], axis=1)


def _ln_stats(r):
    mu = jnp.mean(r, axis=-1, keepdims=True)
    var = jnp.mean(jnp.square(r - mu), axis=-1, keepdims=True)
    rstd = lax.rsqrt(var + LN_EPS)
    return (r - mu) * rstd, rstd


def _outproj_fwd(y, ug, xin, gt, wout, lg, lb, target, name):
    nch, rows, _ = y.shape
    e, d = wout.shape
    tm = _row_tile(rows, 512)
    with_loss = target is not None

    def body(*refs):
        y_ref, g_ref, x_ref, gt_ref, w_ref, lg_ref, lb_ref = refs[:7]
        if with_loss:
            t_ref, br_ref, dxo_ref, loss_ref = refs[7:]
        else:
            br_ref, xo_ref = refs[7:]
        z = _gated(y_ref, g_ref, nch)
        br_ref[...] = _dot(z, w_ref[...])
        if with_loss:
            @pl.when(pl.program_id(0) == 0)
            def _():
                loss_ref[...] = jnp.zeros_like(loss_ref)

        def norm(j, c):
            rows = pl.ds(pl.multiple_of(j * LN_ROWS, LN_ROWS), LN_ROWS)
            xhat, _ = _ln_stats(ALPHA * x_ref[rows, :] + gt_ref[...] * br_ref[rows, :])
            xo = xhat * lg_ref[...] + lb_ref[...]
            if with_loss:
                err = xo - t_ref[rows, :]
                dxo_ref[rows, :] = err * (1.0 / d)
                col = jnp.sum(err * err, axis=0, keepdims=True)
                loss_ref[...] += sum(col[:, k * LANE:(k + 1) * LANE] for k in range(d // LANE))
            else:
                xo_ref[rows, :] = xo
            return c

        lax.fori_loop(0, tm // LN_ROWS, norm, 0)

    chunk_spec = pl.BlockSpec((nch, tm, LANE), lambda i: (0, i, 0))
    g_spec = chunk_spec
    row_spec = pl.BlockSpec((tm, d), lambda i: (i, 0))
    vec_spec = pl.BlockSpec((1, d), lambda i: (0, 0))
    in_specs = [chunk_spec, g_spec, row_spec, vec_spec, pl.BlockSpec((e, d), lambda i: (0, 0)), vec_spec, vec_spec]
    args = [y, ug, xin, gt, wout, lg, lb]
    out_specs = [row_spec, row_spec]
    out_shape = [jax.ShapeDtypeStruct((rows, d), F32)] * 2
    if with_loss:
        in_specs.append(row_spec)
        args.append(target)
        out_specs.append(pl.BlockSpec((1, LANE), lambda i: (0, 0)))
        out_shape.append(jax.ShapeDtypeStruct((1, LANE), F32))
    return _call(body, name=name, grid=(rows // tm,), in_specs=in_specs, out_specs=out_specs, out_shape=out_shape)(*args)


def _outproj_bwd(dxo, xin, br, y, ug, gt, lg, wout, name):
    nch, rows, _ = y.shape
    e, d = wout.shape
    tm = _row_tile(rows, 256)

    def body(dxo_ref, x_ref, br_ref, y_ref, g_ref, gt_ref, lg_ref, w_ref,
             dy_ref, dg_ref, dxres_ref, dbr_ref, dlg_ref, dlb_ref, dgt_ref):
        @pl.when(pl.program_id(0) == 0)
        def _():
            dlg_ref[...] = jnp.zeros_like(dlg_ref)
            dlb_ref[...] = jnp.zeros_like(dlb_ref)
            dgt_ref[...] = jnp.zeros_like(dgt_ref)

        def norm_bwd(j, c):
            rows = pl.ds(pl.multiple_of(j * LN_ROWS, LN_ROWS), LN_ROWS)
            dxo_v = dxo_ref[rows, :]
            brv = br_ref[rows, :]
            xhat, rstd = _ln_stats(ALPHA * x_ref[rows, :] + gt_ref[...] * brv)
            dxh = dxo_v * lg_ref[...]
            dr = rstd * (dxh - jnp.mean(dxh, axis=-1, keepdims=True) - xhat * jnp.mean(dxh * xhat, axis=-1, keepdims=True))
            dlg_ref[...] += jnp.sum(dxo_v * xhat, axis=0, keepdims=True)
            dlb_ref[...] += jnp.sum(dxo_v, axis=0, keepdims=True)
            dgt_ref[...] += jnp.sum(dr * brv, axis=0, keepdims=True)
            dxres_ref[rows, :] = ALPHA * dr
            dbr_ref[rows, :] = (gt_ref[...] * dr).astype(BF16)
            return c

        lax.fori_loop(0, tm // LN_ROWS, norm_bwd, 0)
        dz = _dot_nt(dbr_ref[...], w_ref[...])
        for k in range(nch):
            dzk = dz[:, k * LANE:(k + 1) * LANE]
            gk = g_ref[k].astype(F32)
            dy_ref[k] = dzk * _silu(gk)
            dg_ref[:, k * LANE:(k + 1) * LANE] = (dzk * y_ref[k].astype(F32) * _dsilu(gk)).astype(BF16)

    chunk_spec = pl.BlockSpec((nch, tm, LANE), lambda i: (0, i, 0))
    g_spec = chunk_spec
    row_spec = pl.BlockSpec((tm, d), lambda i: (i, 0))
    vec_spec = pl.BlockSpec((1, d), lambda i: (0, 0))
    return _call(
        body, name=name, grid=(rows // tm,),
        in_specs=[row_spec, row_spec, row_spec, chunk_spec, g_spec, vec_spec, vec_spec, pl.BlockSpec((e, d), lambda i: (0, 0))],
        out_specs=[chunk_spec, pl.BlockSpec((tm, e), lambda i: (i, 0)), row_spec, row_spec, vec_spec, vec_spec, vec_spec],
        out_shape=[jax.ShapeDtypeStruct((nch, rows, LANE), F32), jax.ShapeDtypeStruct((rows, e), BF16),
                   jax.ShapeDtypeStruct((rows, d), F32), jax.ShapeDtypeStruct((rows, d), BF16)]
        + [jax.ShapeDtypeStruct((1, d), F32)] * 3,
    )(dxo, xin, br, y, ug, gt, lg, wout)


def _outproj_bwd_w(y, ug, dbr, gbuf, row0, name):
    nch, rows, _ = y.shape
    d = dbr.shape[1]
    e = nch * LANE
    es = e // N_CHIPS
    tm = _row_tile(rows, 512)
    assert gbuf.shape[2] == d and row0 % es == 0

    def body(y_ref, g_ref, dbr_ref, buf_ref, o_ref):
        @pl.when(pl.program_id(0) == 0)
        def _():
            o_ref[...] = jnp.zeros_like(o_ref)

        z = _gated(y_ref, g_ref, nch)
        o_ref[...] += _dot_tn(z, dbr_ref[...]).reshape(N_CHIPS, es, d)

    return _call(
        body, name=name, grid=(rows // tm,),
        in_specs=[pl.BlockSpec((nch, tm, LANE), lambda i: (0, i, 0)),
                  pl.BlockSpec((nch, tm, LANE), lambda i: (0, i, 0)),
                  pl.BlockSpec((tm, d), lambda i: (i, 0)),
                  ANY_SPEC],
        out_specs=pl.BlockSpec((N_CHIPS, es, d), lambda i: (0, row0 // es, 0)),
        out_shape=jax.ShapeDtypeStruct(gbuf.shape, F32),
        input_output_aliases={3: 0},
    )(y, ug, dbr, gbuf)


def _scan(a_ref, b_ref, h_ref, *, length, init, reverse, a_shift, store):
    nblk = length // SUB
    unroll = min(SCAN_UNROLL, nblk)
    assert nblk % unroll == 0
    row = lax.broadcasted_iota(jnp.int32, (SUB, LANE), 0)
    last = 0 if reverse else SUB - 1
    edges = [(row >= SUB - k) if reverse else (row < k) for k in (1, 2, 4)]

    def local_scan(a, b):
        for k, edge in zip((1, 2, 4), edges):
            sh = (SUB - k) if reverse else k
            b = b + a * jnp.where(edge, 0.0, pltpu.roll(b, sh, 0))
            a = a * jnp.where(edge, 1.0, pltpu.roll(a, sh, 0))
        return a, b

    def step(i, carry):
        base = pl.multiple_of(((nblk // unroll - 1 - i) if reverse else i) * (unroll * SUB), unroll * SUB)
        order = range(unroll - 1, -1, -1) if reverse else range(unroll)
        loaded = [(a_ref[pl.ds(PAD + base + j * SUB + a_shift, SUB), :], b_ref[pl.ds(PAD + base + j * SUB, SUB), :])
                  for j in order]
        scanned = [local_scan(a, b) for a, b in loaded]
        for j, (a, b) in zip(order, scanned):
            if store:
                h_ref[pl.ds(PAD + base + j * SUB, SUB), :] = b + a * carry
            a_l = jnp.broadcast_to(a[last:last + 1, :], (SUB, LANE))
            b_l = jnp.broadcast_to(b[last:last + 1, :], (SUB, LANE))
            carry = b_l + a_l * carry
        return carry

    carry = lax.fori_loop(0, nblk // unroll, step, jnp.broadcast_to(init, (SUB, LANE)))
    return carry[0:1, :]


def _conv_fwd(src_ref, upad, u_ref, cw, cb, length):
    zeros = jnp.zeros((PAD, LANE), F32)
    upad[pl.ds(0, PAD), :] = zeros
    upad[pl.ds(PAD + length, PAD), :] = zeros
    rt = _row_tile(length, ROW_TILE)

    def copy(i, c):
        t0 = pl.multiple_of(i * rt, rt)
        upad[pl.ds(PAD + t0, rt), :] = src_ref[pl.ds(t0, rt), :]
        return c

    lax.fori_loop(0, length // rt, copy, 0)

    def tile(i, c):
        t0 = pl.multiple_of(i * rt, rt)
        acc = jnp.zeros((rt, LANE), F32)
        for k in range(CONV_TAPS):
            acc = acc + upad[pl.ds(t0 + PAD - CONV_LEFT + k, rt), :] * cw[k:k + 1, :]
        u_ref[pl.ds(t0, rt), :] = acc + cb
        return c

    lax.fori_loop(0, length // rt, tile, 0)


def _gates_fwd(u_ref, a_ref, b_ref, wa, wx, ba, bx, ls, length, keep=None):
    rt = _row_tile(length, GATE_TILE)

    def tile(i, c):
        t0 = pl.multiple_of(i * rt, rt)
        ut = u_ref[pl.ds(t0, rt), :]
        ub = ut.astype(BF16)
        r = 0.5 * (jnp.tanh(_dot(ub, wa) + ba) + 1.0)
        ig = 0.5 * (jnp.tanh(_dot(ub, wx) + bx) + 1.0)
        if keep is not None:
            keep[0][pl.ds(t0, rt), :] = r
            keep[1][pl.ds(t0, rt), :] = ig
        la = (LRU_C * r) * ls
        a = jnp.exp(la)
        a_ref[pl.ds(PAD + t0, rt), :] = a
        b_ref[pl.ds(PAD + t0, rt), :] = jnp.sqrt(_one_minus_sq(la, a)) * (ig * ut)
        return c

    lax.fori_loop(0, length // rt, tile, 0, unroll=min(GATE_UNROLL, length // rt))


def _lru_specs():
    return [pl.BlockSpec((CONV_TAPS, LANE), lambda n: (0, n)),
            pl.BlockSpec((1, LANE), lambda n: (0, n)),
            pl.BlockSpec((2, None, LANE, LANE), lambda n: (0, n, 0, 0)),
            pl.BlockSpec((2, None, LANE, LANE), lambda n: (0, n, 0, 0)),
            pl.BlockSpec((2, LANE), lambda n: (0, n)),
            pl.BlockSpec((2, LANE), lambda n: (0, n)),
            pl.BlockSpec((2, LANE), lambda n: (0, n))]


def _rglru_fwd(ug, uc, conv_w, conv_b, wa, wx, ba, bx, lam):
    nb = uc.shape[0]
    s_len, t_len = ug.shape[1], uc.shape[1]

    def body(u0_ref, uc0_ref, cw_ref, cb_ref, wa_ref, wx_ref, ba_ref, bx_ref, lam_ref, y_ref,
             upad, ubuf, abuf, hbuf):
        cw, cb = cw_ref[...], cb_ref[...]
        lsig = _log_sigmoid(lam_ref[...])
        zero = jnp.zeros((1, LANE), F32)
        _conv_fwd(uc0_ref, upad, ubuf, cw, cb, t_len)
        h0 = []
        for dr in range(2):
            _gates_fwd(ubuf, abuf, hbuf, wa_ref[dr], wx_ref[dr], ba_ref[dr:dr + 1, :], bx_ref[dr:dr + 1, :],
                       lsig[dr:dr + 1, :], t_len)
            h0.append(_scan(abuf, hbuf, hbuf, length=t_len, init=zero, reverse=(dr == 1), a_shift=0, store=False))
        _conv_fwd(u0_ref, upad, ubuf, cw, cb, s_len)
        rt = _row_tile(s_len, ROW_TILE)
        for dr in range(2):
            _gates_fwd(ubuf, abuf, hbuf, wa_ref[dr], wx_ref[dr], ba_ref[dr:dr + 1, :], bx_ref[dr:dr + 1, :],
                       lsig[dr:dr + 1, :], s_len)
            _scan(abuf, hbuf, hbuf, length=s_len, init=h0[dr], reverse=(dr == 1), a_shift=0, store=True)

            def acc(i, c, dr=dr):
                t0 = pl.multiple_of(i * rt, rt)
                h = hbuf[pl.ds(PAD + t0, rt), :]
                if dr == 0:
                    upad[pl.ds(PAD + t0, rt), :] = h
                else:
                    y_ref[pl.ds(t0, rt), :] = (upad[pl.ds(PAD + t0, rt), :] + h).astype(y_ref.dtype)
                return c

            lax.fori_loop(0, s_len // rt, acc, 0)

    seq = pltpu.VMEM((s_len + 2 * PAD, LANE), F32)
    return _call(
        body, name="rglru_fwd", grid=(nb,),
        in_specs=[pl.BlockSpec((None, s_len, LANE), lambda n: (n, 0, 0)),
                  pl.BlockSpec((None, t_len, LANE), lambda n: (n, 0, 0))] + _lru_specs(),
        out_specs=pl.BlockSpec((None, s_len, LANE), lambda n: (n, 0, 0)),
        out_shape=jax.ShapeDtypeStruct((nb, s_len, LANE), BF16),
        scratch_shapes=[seq, pltpu.VMEM((s_len, LANE), F32), seq, seq],
    )(ug, uc, conv_w, conv_b, wa, wx, ba, bx, lam)


def _rglru_bwd(ug, uc, dy, conv_w, conv_b, wa, wx, ba, bx, lam):
    nb = uc.shape[0]
    e = nb * LANE
    s_len, t_len = ug.shape[1], uc.shape[1]

    def body(u0_ref, uc0_ref, dy_ref, cw_ref, cb_ref, wa_ref, wx_ref, ba_ref, bx_ref, lam_ref,
             du_ref, duc_ref, dcw_ref, dcb_ref, dwa_ref, dwx_ref, dba_ref, dbx_ref, dlam_ref,
             upad, ubuf, abuf, hbuf, lbuf, dubuf, rbuf, ibuf, cpad, cu, ca0, ch0, ca1, ch1, cr0, ci0, cr1, ci1):
        cw, cb = cw_ref[...], cb_ref[...]
        lam_v = lam_ref[...]
        lsig = _log_sigmoid(lam_v)
        zero = jnp.zeros((1, LANE), F32)
        zpad = jnp.zeros((PAD, LANE), F32)
        for ref in (dcw_ref, dcb_ref, dwa_ref, dwx_ref, dba_ref, dbx_ref, dlam_ref):
            ref[...] = jnp.zeros_like(ref)

        def params(dr):
            return (wa_ref[dr], wx_ref[dr], ba_ref[dr:dr + 1, :], bx_ref[dr:dr + 1, :], lsig[dr:dr + 1, :])

        def direction_bwd(dr, u_ref, a_ref, h_ref, l_ref, gates, dub, length, first):
            wa_d, wx_d, ba_d, bx_d, ls_d = params(dr)
            rt = _row_tile(length, GATE_BWD_TILE)
            prev = 1 if dr == 1 else -1

            def tile(i, c):
                t0 = pl.multiple_of(i * rt, rt)
                ut = u_ref[pl.ds(t0, rt), :]
                ub = ut.astype(BF16)
                r = gates[0][pl.ds(t0, rt), :]
                ig = gates[1][pl.ds(t0, rt), :]
                la = (LRU_C * r) * ls_d
                a = a_ref[pl.ds(PAD + t0, rt), :]
                q = _one_minus_sq(la, a)
                rs = lax.rsqrt(q)
                sq = q * rs
                lm = l_ref[pl.ds(PAD + t0, rt), :]
                da = lm * h_ref[pl.ds(PAD + t0 + prev, rt), :]
                dsq = lm * ig * ut
                dig = lm * sq * ut
                dla = da * a - dsq * (a * a) * rs
                dr_ = dla * (LRU_C * ls_d)
                dlam_ref[dr:dr + 1, :] += jnp.sum(dla * (LRU_C * r), axis=0, keepdims=True)
                dpr = dr_ * r * (1.0 - r)
                dpi = dig * ig * (1.0 - ig)
                dba_ref[dr:dr + 1, :] += jnp.sum(dpr, axis=0, keepdims=True)
                dbx_ref[dr:dr + 1, :] += jnp.sum(dpi, axis=0, keepdims=True)
                dprb, dpib = dpr.astype(BF16), dpi.astype(BF16)
                dwa_ref[dr] += _dot_tn(ub, dprb)
                dwx_ref[dr] += _dot_tn(ub, dpib)
                dut = lm * sq * ig + 2.0 * (_dot_nt(dprb, wa_d) + _dot_nt(dpib, wx_d))
                if first:
                    dub[pl.ds(PAD + t0, rt), :] = dut
                else:
                    dub[pl.ds(PAD + t0, rt), :] += dut
                return c

            lax.fori_loop(0, length // rt, tile, 0, unroll=min(GATE_UNROLL, length // rt))

        def conv_bwd(dub, src_pad, out_ref, length):
            rt = _row_tile(length, ROW_TILE)

            def tile(i, c):
                t0 = pl.multiple_of(i * rt, rt)
                dut = dub[pl.ds(PAD + t0, rt), :]
                dcb_ref[...] += jnp.sum(dut, axis=0, keepdims=True)
                acc = jnp.zeros((rt, LANE), F32)
                for k in range(CONV_TAPS):
                    sh = CONV_LEFT - k
                    acc = acc + dub[pl.ds(PAD + t0 + sh, rt), :] * cw[k:k + 1, :]
                    dcw_ref[k:k + 1, :] += jnp.sum(dut * src_pad[pl.ds(PAD + t0 - sh, rt), :], axis=0, keepdims=True)
                out_ref[pl.ds(t0, rt), :] = acc.astype(out_ref.dtype)
                return c

            lax.fori_loop(0, length // rt, tile, 0)

        _conv_fwd(uc0_ref, cpad, cu, cw, cb, t_len)
        cbufs = ((ca0, ch0), (ca1, ch1))
        cgates = ((cr0, ci0), (cr1, ci1))
        h0 = []
        for dr in range(2):
            ca, chh = cbufs[dr]
            _gates_fwd(cu, ca, chh, *params(dr), t_len, keep=cgates[dr])
            h0.append(_scan(ca, chh, chh, length=t_len, init=zero, reverse=(dr == 1), a_shift=0, store=True))
        _conv_fwd(u0_ref, upad, ubuf, cw, cb, s_len)
        rt = _row_tile(s_len, ROW_TILE)
        dh0 = []
        for dr in range(2):
            rev = dr == 1
            _gates_fwd(ubuf, abuf, hbuf, *params(dr), s_len, keep=(rbuf, ibuf))
            _scan(abuf, hbuf, hbuf, length=s_len, init=h0[dr], reverse=rev, a_shift=0, store=True)
            first_row = PAD + s_len if rev else PAD - 1
            hbuf[pl.ds(first_row, 1), :] = h0[dr]
            end_row = PAD - 1 if rev else PAD + s_len
            abuf[pl.ds(end_row, 1), :] = zero

            def copy(i, c):
                t0 = pl.multiple_of(i * rt, rt)
                lbuf[pl.ds(PAD + t0, rt), :] = dy_ref[pl.ds(t0, rt), :]
                return c

            lax.fori_loop(0, s_len // rt, copy, 0)
            _scan(abuf, lbuf, lbuf, length=s_len, init=zero, reverse=not rev, a_shift=(-1 if rev else 1), store=True)
            start = PAD + s_len - 1 if rev else PAD
            dh0.append(abuf[pl.ds(start, 1), :] * lbuf[pl.ds(start, 1), :])
            direction_bwd(dr, ubuf, abuf, hbuf, lbuf, (rbuf, ibuf), dubuf, s_len, first=(dr == 0))
        dubuf[pl.ds(0, PAD), :] = zpad
        dubuf[pl.ds(PAD + s_len, PAD), :] = zpad
        conv_bwd(dubuf, upad, du_ref, s_len)
        lc = lbuf
        duc_buf = dubuf
        for dr in range(2):
            rev = dr == 1
            ca, chh = cbufs[dr]
            first_row = PAD + t_len if rev else PAD - 1
            chh[pl.ds(first_row, 1), :] = zero
            end_row = PAD - 1 if rev else PAD + t_len
            ca[pl.ds(end_row, 1), :] = zero + 1.0
            rtc = _row_tile(t_len, ROW_TILE)

            def clear(i, c):
                t0 = pl.multiple_of(i * rtc, rtc)
                lc[pl.ds(PAD + t0, rtc), :] = jnp.zeros((rtc, LANE), F32)
                return c

            lax.fori_loop(0, t_len // rtc, clear, 0)
            _scan(ca, lc, lc, length=t_len, init=dh0[dr], reverse=not rev, a_shift=(-1 if rev else 1), store=True)
            direction_bwd(dr, cu, ca, chh, lc, cgates[dr], duc_buf, t_len, first=(dr == 0))
        duc_buf[pl.ds(0, PAD), :] = zpad
        duc_buf[pl.ds(PAD + t_len, PAD), :] = zpad
        conv_bwd(duc_buf, cpad, duc_ref, t_len)
        dlam_ref[...] = dlam_ref[...] * (1.0 - _sigmoid(lam_v))

    seq = pltpu.VMEM((s_len + 2 * PAD, LANE), F32)
    cseq = pltpu.VMEM((t_len + 2 * PAD, LANE), F32)
    flat = pltpu.VMEM((s_len, LANE), F32)
    cflat = pltpu.VMEM((t_len, LANE), F32)
    vec2 = pl.BlockSpec((2, LANE), lambda n: (0, n))
    wspec = pl.BlockSpec((2, None, LANE, LANE), lambda n: (0, n, 0, 0))
    return _call(
        body, name="rglru_bwd", grid=(nb,),
        in_specs=[pl.BlockSpec((None, s_len, LANE), lambda n: (n, 0, 0)),
                  pl.BlockSpec((None, t_len, LANE), lambda n: (n, 0, 0)),
                  pl.BlockSpec((None, s_len, LANE), lambda n: (n, 0, 0))] + _lru_specs(),
        out_specs=[pl.BlockSpec((s_len, LANE), lambda n: (0, n)),
                   pl.BlockSpec((t_len, LANE), lambda n: (0, n)),
                   pl.BlockSpec((CONV_TAPS, LANE), lambda n: (0, n)),
                   pl.BlockSpec((1, LANE), lambda n: (0, n)),
                   wspec, wspec, vec2, vec2, vec2],
        out_shape=[jax.ShapeDtypeStruct((s_len, e), BF16), jax.ShapeDtypeStruct((t_len, e), BF16),
                   jax.ShapeDtypeStruct((CONV_TAPS, e), F32), jax.ShapeDtypeStruct((1, e), F32),
                   jax.ShapeDtypeStruct((2, nb, LANE, LANE), F32), jax.ShapeDtypeStruct((2, nb, LANE, LANE), F32),
                   jax.ShapeDtypeStruct((2, e), F32), jax.ShapeDtypeStruct((2, e), F32), jax.ShapeDtypeStruct((2, e), F32)],
        scratch_shapes=[seq, flat, seq, seq, seq, seq, flat, flat,
                        cseq, cflat, cseq, cseq, cseq, cseq, cflat, cflat, cflat, cflat],
    )(ug, uc, dy, conv_w, conv_b, wa, wx, ba, bx, lam)


def _pool_windows(src_ref, out_ref, colbuf, rowbuf, half, transpose, s_len):
    gw = GRID_W
    lg = gw.bit_length() - 1
    n_rows = s_len // gw
    cp, rm = POOL_CPAD, 8 * gw
    stride = gw + 2 * cp
    rt = _row_tile(s_len, ROW_TILE)
    assert rt % gw == 0 and half <= cp
    gpt = rt // gw
    offs = range(-half, half)
    zmargin = jnp.zeros((cp, LANE), F32)

    def zcol(r, c):
        base = pl.multiple_of(r * stride, SUB)
        colbuf[pl.ds(base, cp), :] = zmargin
        colbuf[pl.ds(base + cp + gw, cp), :] = zmargin
        return c

    lax.fori_loop(0, n_rows, zcol, 0)

    def zrow(i, c):
        t0 = pl.multiple_of(i * gw, gw)
        rowbuf[pl.ds(t0, gw), :] = jnp.zeros((gw, LANE), F32)
        rowbuf[pl.ds(rm + s_len + t0, gw), :] = jnp.zeros((gw, LANE), F32)
        return c

    lax.fori_loop(0, rm // gw, zrow, 0)

    col = lax.broadcasted_iota(jnp.int32, (gw, LANE), 0)
    ccnt = (jnp.minimum(col + half, gw) - jnp.maximum(col - half, 0)).astype(F32)

    def row_counts(t0):
        row = (t0 + lax.broadcasted_iota(jnp.int32, (rt, LANE), 0)) >> lg
        return (jnp.minimum(row + half, n_rows) - jnp.maximum(row - half, 0)).astype(F32)

    def col_base(t0, g):
        return pl.multiple_of((t0 // gw) * stride, SUB) + g * stride + cp

    def col_sum(t0, g, sign):
        acc = jnp.zeros((gw, LANE), F32)
        for o in offs:
            acc = acc + colbuf[pl.ds(col_base(t0, g) + sign * o, gw), :]
        return acc

    def row_sum(t0, sign):
        acc = jnp.zeros((rt, LANE), F32)
        for o in offs:
            acc = acc + rowbuf[pl.ds(rm + t0 + sign * o * gw, rt), :]
        return acc

    n_tiles = s_len // rt
    assert rt >= half * gw

    def loop(fn, edges=False):
        def step(i, c):
            t0 = pl.multiple_of(i * rt, rt)
            fn(t0, False) if edges else fn(t0)
            return c
        if edges:
            fn(0, True)
            if n_tiles > 1:
                fn(s_len - rt, True)
            lax.fori_loop(1, n_tiles - 1, step, 0)
        else:
            lax.fori_loop(0, n_tiles, step, 0)

    inv_ccnt = 1.0 / ccnt

    def by_row_count(v, t0, edge):
        return v / row_counts(t0) if edge else v * (1.0 / (2 * half))

    if not transpose:
        def fill(t0):
            for g in range(gpt):
                colbuf[pl.ds(col_base(t0, g), gw), :] = src_ref[pl.ds(t0 + g * gw, gw), :]

        def cols(t0):
            for g in range(gpt):
                rowbuf[pl.ds(rm + t0 + g * gw, gw), :] = col_sum(t0, g, 1) * inv_ccnt

        def rows(t0, edge):
            mean = by_row_count(row_sum(t0, 1), t0, edge)
            out_ref[pl.ds(t0, rt), :] = (mean - src_ref[pl.ds(t0, rt), :]).astype(out_ref.dtype)

        loop(fill)
        loop(cols)
        loop(rows, edges=True)
    else:
        def fill(t0, edge):
            rowbuf[pl.ds(rm + t0, rt), :] = by_row_count(src_ref[pl.ds(t0, rt), :], t0, edge)

        def rows(t0):
            acc = row_sum(t0, -1)
            for g in range(gpt):
                colbuf[pl.ds(col_base(t0, g), gw), :] = acc[g * gw:(g + 1) * gw, :] * inv_ccnt

        def cols(t0):
            for g in range(gpt):
                rows_g = pl.ds(t0 + g * gw, gw)
                out_ref[rows_g, :] = (col_sum(t0, g, -1) - src_ref[rows_g, :]).astype(out_ref.dtype)

        loop(fill, edges=True)
        loop(rows)
        loop(cols)


def _pool_map(src, nb, transpose, out_chunk_major, name):
    s_len = src.shape[1]
    cpg = nb // len(POOL_WINDOWS)

    def body(src_ref, out_ref, colbuf, rowbuf):
        n = pl.program_id(0)
        for gi, w in enumerate(POOL_WINDOWS):
            @pl.when(n // cpg == gi)
            def _(w=w):
                _pool_windows(src_ref, out_ref, colbuf, rowbuf, w // 2, transpose, s_len)

    if out_chunk_major:
        out_spec = pl.BlockSpec((None, s_len, LANE), lambda n: (n, 0, 0))
        out_shape = jax.ShapeDtypeStruct((nb, s_len, LANE), BF16)
    else:
        out_spec = pl.BlockSpec((s_len, LANE), lambda n: (0, n))
        out_shape = jax.ShapeDtypeStruct((s_len, nb * LANE), BF16)
    return _call(
        body, name=name, grid=(nb,),
        in_specs=[pl.BlockSpec((None, s_len, LANE), lambda n: (n, 0, 0))],
        out_specs=out_spec, out_shape=out_shape,
        scratch_shapes=[pltpu.VMEM((s_len // GRID_W * (GRID_W + 2 * POOL_CPAD), LANE), F32),
                        pltpu.VMEM((s_len + 16 * GRID_W, LANE), F32)],
    )(src)


def _group_weight(w_ref):
    return jnp.concatenate([w_ref[k] for k in range(N_CHIPS)], axis=0)


def _pool_mm_fwd(dm, wp, scale):
    nb, rows, _ = dm.shape
    _, ng, pq, pg = wp.shape
    cpg = pg // LANE
    tm = _row_tile(rows, 2048)

    def body(d_ref, w_ref, s_ref, y_ref):
        _put_chunks(y_ref, _dot(_cat(d_ref, cpg), _group_weight(w_ref)) * s_ref[...], cpg)

    cspec = pl.BlockSpec((cpg, tm, LANE), lambda i, g: (g, i, 0))
    return _call(
        body, name="pool_mm_fwd", grid=(rows // tm, ng),
        in_specs=[cspec, pl.BlockSpec((N_CHIPS, None, pq, pg), lambda i, g: (0, g, 0, 0)),
                  pl.BlockSpec((1, pg), lambda i, g: (0, g))],
        out_specs=cspec, out_shape=jax.ShapeDtypeStruct((nb, rows, LANE), BF16),
    )(dm, wp, scale)


def _pool_mm_bwd(dy, dm, wp, scale, gbuf, row0):
    nb, rows, _ = dm.shape
    _, ng, pq, pg = wp.shape
    cpg = pg // LANE
    tm = _row_tile(rows, 1024)
    nt = rows // tm
    assert gbuf.shape[2] == 2 * pg and row0 % pq == 0

    def body(dy_ref, d_ref, w_ref, s_ref, buf_ref, dd_ref, dwp_ref, dsc_ref, acc):
        i = pl.program_id(1)

        @pl.when(i == 0)
        def _():
            acc[...] = jnp.zeros_like(acc)
            dsc_ref[...] = jnp.zeros_like(dsc_ref)

        dyv = _cat(dy_ref, cpg)
        dc = _cat(d_ref, cpg)
        w = _group_weight(w_ref)
        dsc_ref[...] += jnp.sum(dyv * _dot(dc, w), axis=0, keepdims=True)
        dyp = (dyv * s_ref[...]).astype(BF16)
        _put_chunks(dd_ref, _dot_nt(dyp, w), cpg)
        acc[...] += _dot_tn(dc, dyp)

        @pl.when(i == nt - 1)
        def _():
            dwp_ref[...] = acc[...].reshape(N_CHIPS, pq, pg)

    cspec = pl.BlockSpec((cpg, tm, LANE), lambda g, i: (g, i, 0))
    sspec = pl.BlockSpec((1, pg), lambda g, i: (0, g))
    return _call(
        body, name="pool_mm_bwd", grid=(ng, nt),
        in_specs=[cspec, cspec, pl.BlockSpec((N_CHIPS, None, pq, pg), lambda g, i: (0, g, 0, 0)), sspec, ANY_SPEC],
        out_specs=[cspec, pl.BlockSpec((N_CHIPS, pq, pg), lambda g, i: (0, row0 // pq + g // 2, g % 2)), sspec],
        out_shape=[jax.ShapeDtypeStruct((nb, rows, LANE), F32), jax.ShapeDtypeStruct(gbuf.shape, F32),
                   jax.ShapeDtypeStruct((1, ng * pg), F32)],
        scratch_shapes=[pltpu.VMEM((pg, pg), F32)],
        input_output_aliases={4: 1},
    )(dy, dm, wp, scale, gbuf)


def _adamw_math(w, g, m, v):
    nm = ADAM_B1 * m + (1.0 - ADAM_B1) * g
    nv = ADAM_B2 * v + (1.0 - ADAM_B2) * jnp.square(g)
    m_hat = nm / (1.0 - ADAM_B1 ** ADAM_STEP)
    v_hat = nv / (1.0 - ADAM_B2 ** ADAM_STEP)
    return -ADAM_LR * (m_hat / (jnp.sqrt(v_hat) + ADAM_EPS) + ADAM_WD * w), nm, nv


def _adamw_param(w3, m3, v3, gsrcs, pick, tm, name, after=None):
    n_blk, rows, cols = w3.shape
    ng = len(gsrcs)

    def body(*refs):
        w_ref, m_ref, v_ref = refs[:3]
        g_refs = refs[3:3 + ng]
        go_ref, d_ref, nm_ref, nv_ref = refs[-4:]
        g = pick(pl.program_id(0), [r[...] for r in g_refs])
        go_ref[...] = g
        d_ref[...], nm_ref[...], nv_ref[...] = _adamw_math(w_ref[...], g, m_ref[...], v_ref[...])

    spec = pl.BlockSpec((None, tm, cols), lambda n, i: (n, i, 0))
    extra = [] if after is None else [after]
    return _call(
        body, name=name, grid=(n_blk, rows // tm),
        in_specs=[spec] * 3 + [pl.BlockSpec(shape, imap) for _, shape, imap in gsrcs] + [ANY_SPEC] * len(extra),
        out_specs=[spec] * 4, out_shape=[jax.ShapeDtypeStruct(w3.shape, F32)] * 4,
    )(w3, m3, v3, *[a for a, _, _ in gsrcs], *extra)


def _adamw_small(quads):
    n = len(quads)

    def body(*refs):
        ins, outs = refs[:4 * n], refs[4 * n:]
        for k in range(n):
            w, g, m, v = (r[...] for r in ins[4 * k:4 * k + 4])
            outs[3 * k][...], outs[3 * k + 1][...], outs[3 * k + 2][...] = _adamw_math(w, g, m, v)

    flat = [a for q in quads for a in q]
    res = _call(body, name="adamw_small", grid=(1,),
                in_specs=[pl.BlockSpec(a.shape, lambda i: (0, 0)) for a in flat],
                out_specs=[pl.BlockSpec(q[0].shape, lambda i: (0, 0)) for q in quads for _ in range(3)],
                out_shape=[jax.ShapeDtypeStruct(q[0].shape, F32) for q in quads for _ in range(3)])(*flat)
    return [tuple(res[3 * k:3 * k + 3]) for k in range(n)]


def _place():
    return lax.axis_index("x"), lax.axis_index("y"), lax.axis_index("c")


def _other_chips(x, y):
    return [(1 - x, y), (x, 1 - y), (1 - x, 1 - y)]


def _own_slab(a, devices=False):
    x, y, c = _place()
    n, me = (N_DEV, 4 * x + 2 * y + c) if devices else (N_CHIPS, 2 * x + y)
    return lax.dynamic_update_slice(lax.empty((n,) + a.shape, a.dtype), a[None], (me, 0, 0))


def _gather_chips(arrays, name):
    n = len(arrays)
    halves = [a.shape[0] // 2 for a in arrays]
    for a, h in zip(arrays, halves):
        assert 2 * h == a.shape[0] and h % (32 // a.dtype.itemsize) == 0
    lands = [_own_slab(a) for a in arrays]

    def body(*refs):
        outs = refs[n:2 * n]
        send_sems, recv_sems = refs[2 * n:]
        x, y, c = _place()
        me = 2 * x + y
        chips = _other_chips(x, y)

        def mine(k):
            return pl.ds(c * halves[k], halves[k])

        def theirs(k):
            return pl.ds((1 - c) * halves[k], halves[k])

        def push(k, j, src, dst, to):
            return pltpu.make_async_remote_copy(src_ref=src, dst_ref=dst, send_sem=send_sems.at[6 * k + j],
                                                recv_sem=recv_sems.at[6 * k + j], device_id=to, device_id_type=MESH)

        started = []
        for j, (cx, cy) in enumerate(chips):
            for k in range(n):
                own = outs[k].at[me, mine(k)]
                cp = push(k, j, own, own, (cx, cy, c))
                cp.start()
                started.append(cp)
        for j, (cx, cy) in enumerate(chips):
            for k in range(n):
                slab = outs[k].at[2 * cx + cy, mine(k)]
                push(k, j, slab, slab, (x, y, c)).wait_recv()
                fwd = push(k, 3 + j, slab, slab, (x, y, 1 - c))
                fwd.start()
                started.append(fwd)
        for j, (cx, cy) in enumerate(chips):
            for k in range(n):
                slab = outs[k].at[2 * cx + cy, theirs(k)]
                push(k, 3 + j, slab, slab, (x, y, c)).wait_recv()
        for cp in started:
            cp.wait_send()

    return _call(
        body, name=name, in_specs=[ANY_SPEC] * n, out_specs=[ANY_SPEC] * n,
        out_shape=[jax.ShapeDtypeStruct(a.shape, a.dtype) for a in lands],
        input_output_aliases={k: k for k in range(n)},
        scratch_shapes=[pltpu.SemaphoreType.DMA((6 * n,)), pltpu.SemaphoreType.DMA((6 * n,))],
    )(*lands)


HBM_SPEC = pl.BlockSpec(memory_space=pltpu.HBM)
SEM_SPEC = pl.BlockSpec(memory_space=pltpu.SEMAPHORE)
SIDE_EFFECT = pltpu.SideEffectType.DATAFLOW_SIDE_EFFECTING


def _n_peers(kind):
    return N_DEV - 1 if kind == "devices" else N_CHIPS - 1


def _push_copies(src_refs, land_refs, send_sems, recv_sems, kind):
    x, y, c = _place()
    if kind == "devices":
        me = 4 * x + 2 * y + c
        peers = [((me + j) % N_DEV, None) for j in range(1, N_DEV)]
        peers = [((to // 4, (to // 2) % 2, to % 2), None) for to, _ in peers]
    else:
        me = 2 * x + y
        peers = [((cx, cy, c), 2 * cx + cy) for cx, cy in _other_chips(x, y)]
    n = len(peers)
    copies = []
    for j, (dev, slab) in enumerate(peers):
        for k, (src, land) in enumerate(zip(src_refs, land_refs)):
            copies.append(pltpu.make_async_remote_copy(
                src_ref=src.at[slab] if kind == "slab" else src, dst_ref=land.at[me], send_sem=send_sems.at[n * k + j],
                recv_sem=recv_sems.at[n * k + j], device_id=dev, device_id_type=MESH))
    return copies


def _push_start(srcs, lands, kind, after, name):
    n = len(srcs)

    def body(*refs):
        src_refs, land_refs = refs[:n], refs[n:2 * n]
        send_sems, recv_sems = refs[2 * n + 1], refs[2 * n + 2]
        token = refs[-1]
        for cp in _push_copies(src_refs, land_refs, send_sems, recv_sems, kind):
            cp.start()
        token[...] = jnp.zeros_like(token)

    bufs = [pltpu.with_memory_space_constraint(a, pltpu.HBM) for a in list(srcs) + list(lands)]
    res = _call(
        body, name=name,
        out_shape=[pltpu.SemaphoreType.DMA((_n_peers(kind) * n,)), pltpu.SemaphoreType.DMA((_n_peers(kind) * n,))]
        + [pltpu.HBM(a.shape, a.dtype) for a in bufs] + [jax.ShapeDtypeStruct((SUB, LANE), F32)],
        in_specs=[HBM_SPEC] * (2 * n) + [ANY_SPEC],
        out_specs=[SEM_SPEC, SEM_SPEC] + [HBM_SPEC] * (2 * n) + [pl.BlockSpec(memory_space=pltpu.VMEM)],
        input_output_aliases={i: 2 + i for i in range(2 * n)},
        compiler_params=pltpu.CompilerParams(has_side_effects=SIDE_EFFECT),
    )(*bufs, after)
    return res[0], res[1], list(res[2:2 + n]), list(res[2 + n:2 + 2 * n]), res[-1]


def _push_wait(send_sems, recv_sems, srcs, lands, kind, after, name):
    n = len(srcs)

    def body(*refs):
        src_refs, land_refs = refs[:n], refs[n:2 * n]
        send_sems, recv_sems = refs[2 * n], refs[2 * n + 1]
        for cp in _push_copies(src_refs, land_refs, send_sems, recv_sems, kind):
            cp.wait_send()
            cp.wait_recv()

    res = _call(
        body, name=name,
        out_shape=[pltpu.HBM(a.shape, a.dtype) for a in list(srcs) + list(lands)],
        in_specs=[HBM_SPEC] * (2 * n) + [SEM_SPEC, SEM_SPEC, ANY_SPEC],
        out_specs=[HBM_SPEC] * (2 * n),
        input_output_aliases={i: i for i in range(2 * n)},
        compiler_params=pltpu.CompilerParams(has_side_effects=SIDE_EFFECT),
    )(*srcs, *lands, send_sems, recv_sems, after)
    return list(res[n:])


def _sibling_swap(g):
    _, rows, w = g.shape
    half = rows // 2

    def body(g_ref, out_ref, send_sem, recv_sem):
        x, y, c = _place()
        cp = pltpu.make_async_remote_copy(src_ref=g_ref.at[:, pl.ds((1 - c) * half, half)], dst_ref=out_ref,
                                          send_sem=send_sem, recv_sem=recv_sem, device_id=(x, y, 1 - c), device_id_type=MESH)
        cp.start()
        cp.wait()

    return _call(body, name="rs_sibling_swap", in_specs=[ANY_SPEC], out_specs=ANY_SPEC,
                 out_shape=jax.ShapeDtypeStruct((N_CHIPS, half, w), F32),
                 scratch_shapes=[pltpu.SemaphoreType.DMA, pltpu.SemaphoreType.DMA])(g)


def _pair_add(g, got, place):
    _, rows, w = g.shape
    half = rows // 2
    tm = _row_tile(half, RS_TILE)
    nt = half // tm

    def body(p_ref, a_ref, b_ref, o_ref, own_ref):
        v = a_ref[...] + b_ref[...]
        o_ref[...] = v.astype(BF16)

        @pl.when(pl.program_id(1) == p_ref[1])
        def _():
            own_ref[...] = v

    return _call(
        body, name="rs_pair_add",
        grid_spec=pltpu.PrefetchScalarGridSpec(
            num_scalar_prefetch=1, grid=(nt, N_CHIPS),
            in_specs=[pl.BlockSpec((None, tm, w), lambda i, s, p: (s, p[0] * nt + i, 0)),
                      pl.BlockSpec((None, tm, w), lambda i, s, p: (s, i, 0))],
            out_specs=[pl.BlockSpec((None, tm, w), lambda i, s, p: (s, i, 0)),
                       pl.BlockSpec((tm, w), lambda i, s, p: (i, 0))]),
        out_shape=[jax.ShapeDtypeStruct((N_CHIPS, half, w), BF16), jax.ShapeDtypeStruct((half, w), F32)],
    )(place, g, got)


def _sum_chips(parts, own, place):
    _, half, w = parts.shape
    tm = _row_tile(half, RS_TILE)
    nt = half // tm

    def body(p_ref, parts_ref, own_ref, o_ref):
        me = p_ref[1]
        t = [jnp.where(me == q, own_ref[...], parts_ref[q].astype(F32)) for q in range(N_CHIPS)]
        o_ref[...] = (t[0] + t[1]) + (t[2] + t[3])

    return _call(
        body, name="rs_sum_chips",
        grid_spec=pltpu.PrefetchScalarGridSpec(
            num_scalar_prefetch=1, grid=(nt,),
            in_specs=[pl.BlockSpec((N_CHIPS, tm, w), lambda i, p: (0, i, 0)), pl.BlockSpec((tm, w), lambda i, p: (i, 0))],
            out_specs=pl.BlockSpec((tm, w), lambda i, p: (p[0] * nt + i, 0))),
        out_shape=jax.ShapeDtypeStruct((2 * half, w), F32),
    )(place, parts, own)


def _sibling_gather(red):
    rows, w = red.shape
    half = rows // 2

    def body(in_ref, out_ref, send_sem, recv_sem):
        x, y, c = _place()
        mine = out_ref.at[pl.ds(c * half, half)]
        cp = pltpu.make_async_remote_copy(src_ref=mine, dst_ref=mine, send_sem=send_sem, recv_sem=recv_sem,
                                          device_id=(x, y, 1 - c), device_id_type=MESH)
        cp.start()
        other = out_ref.at[pl.ds((1 - c) * half, half)]
        pltpu.make_async_remote_copy(src_ref=other, dst_ref=other, send_sem=send_sem, recv_sem=recv_sem,
                                     device_id=(x, y, c), device_id_type=MESH).wait_recv()
        cp.wait_send()

    return _call(body, name="rs_sibling_gather", in_specs=[ANY_SPEC], out_specs=ANY_SPEC,
                 out_shape=jax.ShapeDtypeStruct(red.shape, F32), input_output_aliases={0: 0},
                 scratch_shapes=[pltpu.SemaphoreType.DMA, pltpu.SemaphoreType.DMA])(red)


def _rs_begin(g, place, name):
    pair, own = _pair_add(g, _sibling_swap(g), place)
    send, recv, pair, parts, token = _push_start([pair], [jnp.zeros_like(pair)], "slab", own, name + "_start")
    return (send, recv, pair, parts, own), token


def _rs_end(state, place, after, name):
    send, recv, pair, parts, own = state
    (parts,) = _push_wait(send, recv, pair, parts, "slab", after, name + "_wait")
    return _sibling_gather(_sum_chips(parts, own, place))


WEIGHTS = ("c_ctx", "w_mod", "b_mod", "w_in", "w_out", "ln_g", "ln_b", "conv_w", "conv_b", "lru_wa", "lru_ba", "lru_wx",
           "lru_bx", "lru_lam", "pool_w", "pool_scale")
SMALL_GATHERED = ("conv_w", "lru_ba", "lru_bx", "lru_lam", "pool_scale")
SMALL_UPDATED = ("c_ctx", "b_mod", "ln_g", "ln_b", "conv_w", "conv_b", "lru_ba", "lru_bx", "lru_lam", "pool_scale")


def kernel(x, c, ctx, c_ctx, w_mod, b_mod, w_in, w_out, ln_g, ln_b, conv_w, conv_b, lru_wa, lru_ba, lru_wx, lru_bx, lru_lam, pool_w, pool_scale, loss_target, m_c_ctx, m_w_mod, m_b_mod, m_w_in, m_w_out, m_ln_g, m_ln_b, m_conv_w, m_conv_b, m_lru_wa, m_lru_ba, m_lru_wx, m_lru_bx, m_lru_lam, m_pool_w, m_pool_scale, v_c_ctx, v_w_mod, v_b_mod, v_w_in, v_w_out, v_ln_g, v_ln_b, v_conv_w, v_conv_b, v_lru_wa, v_lru_ba, v_lru_wx, v_lru_bx, v_lru_lam, v_pool_w, v_pool_scale):
    weights = dict(c_ctx=c_ctx, w_mod=w_mod, b_mod=b_mod, w_in=w_in, w_out=w_out, ln_g=ln_g, ln_b=ln_b, conv_w=conv_w,
                   conv_b=conv_b, lru_wa=lru_wa, lru_ba=lru_ba, lru_wx=lru_wx, lru_bx=lru_bx, lru_lam=lru_lam,
                   pool_w=pool_w, pool_scale=pool_scale)
    mom1 = dict(c_ctx=m_c_ctx, w_mod=m_w_mod, b_mod=m_b_mod, w_in=m_w_in, w_out=m_w_out, ln_g=m_ln_g, ln_b=m_ln_b,
                conv_w=m_conv_w, conv_b=m_conv_b, lru_wa=m_lru_wa, lru_ba=m_lru_ba, lru_wx=m_lru_wx, lru_bx=m_lru_bx,
                lru_lam=m_lru_lam, pool_w=m_pool_w, pool_scale=m_pool_scale)
    mom2 = dict(c_ctx=v_c_ctx, w_mod=v_w_mod, b_mod=v_b_mod, w_in=v_w_in, w_out=v_w_out, ln_g=v_ln_g, ln_b=v_ln_b,
                conv_w=v_conv_w, conv_b=v_conv_b, lru_wa=v_lru_wa, lru_ba=v_lru_ba, lru_wx=v_lru_wx, lru_bx=v_lru_bx,
                lru_lam=v_lru_lam, pool_w=v_pool_w, pool_scale=v_pool_scale)
    xs, cx, target = x[0], ctx[0], loss_target[0]
    s_len, d = xs.shape
    es = w_out.shape[1]
    e = es * N_CHIPS
    nb = e // LANE
    c3 = w_mod.shape[2]
    n4 = w_in.shape[2]
    pq, pg = pool_w.shape[2], pool_w.shape[3]
    ng = len(POOL_WINDOWS)
    width = n4
    assert width == d and 2 * pg == width and 2 * nb * LANE == N_CHIPS * width and d % (2 * N_CHIPS) == 0
    px, py, pc = _place()
    place = jnp.stack([pc, 2 * px + py]).astype(jnp.int32)
    cctx2 = c_ctx[None, :]

    eq = e // N_CHIPS
    small_rows = [(conv_w[0], 0), (lru_ba[0], CONV_TAPS), (lru_bx[0], CONV_TAPS + 2), (lru_lam[0], CONV_TAPS + 4),
                  (pool_scale, CONV_TAPS + 6)]
    small = _rows_kernel([(a, r, 0) for a, r in small_rows], 2 * SUB, eq, "pack_small_weights")
    wm_g, win0, sg = _gather_chips([w_mod.astype(BF16).reshape(DEPTH * d, c3), w_in[0].astype(BF16), small], "gather_weights0")
    later = [w_out[0].astype(BF16), w_in[1].astype(BF16), w_out[1].astype(BF16), pool_w.astype(BF16).reshape(ng * pq, pg)]
    w_send, w_recv, later, later_lands, w_token = _push_start(
        later, [_own_slab(a) for a in later], "same", sg, "gather_weights1_start")
    wm_all = wm_g.reshape(N_CHIPS, DEPTH, d, c3)
    full = {n: jnp.swapaxes(sg[:, r:r + a.shape[0]], 0, 1).reshape(a.shape[0], e)
            for n, (a, r) in zip(SMALL_GATHERED, small_rows)}
    wa_h, wx_h = (0.5 * lru_wa[0]).astype(BF16), (0.5 * lru_wx[0]).astype(BF16)
    lru_args = (full["conv_w"], conv_b, wa_h, wx_h, 0.5 * full["lru_ba"], 0.5 * full["lru_bx"], full["lru_lam"])
    scale_f = full["pool_scale"]

    mod = _mod_fwd(c + w_token[0:1, 0:1], cctx2, wm_all, b_mod[:, None, :])

    def mod_parts(l, row):
        v = mod[l, row]
        return v[None, :d], 1.0 + v[None, d:2 * d], v[None, 2 * d:]

    sh0, sc0, gt0 = mod_parts(0, 0)
    shc, scc, _ = mod_parts(0, 1)
    sh1, sc1, gt1 = mod_parts(1, 0)
    lg = [ln_g[l][None, :] for l in range(DEPTH)]
    lb = [ln_b[l][None, :] for l in range(DEPTH)]

    uu0, ug0 = _inproj_fwd(xs, sc0, sh0, win0, "inproj_fwd0")
    uc0 = _inproj_fwd(cx, scc, shc, win0[:2], "inproj_fwd_ctx")
    y0 = _rglru_fwd(uu0, uc0, *lru_args)
    wout0_g, win1, wout1_g, wp_g = _push_wait(w_send, w_recv, later, later_lands, "same", y0, "gather_weights1_wait")
    win = [win0, win1]
    wout = [wout0_g.reshape(e, d), wout1_g.reshape(e, d)]
    wp = wp_g.reshape(N_CHIPS, ng, pq, pg)
    br0, x1 = _outproj_fwd(y0, ug0, xs, gt0, wout[0], lg[0], lb[0], None, "outproj_fwd0")
    uu1, ug1 = _inproj_fwd(x1, sc1, sh1, win[1], "inproj_fwd1")
    d1 = _pool_map(uu1, nb, False, True, "pool_fwd")
    y1 = _pool_mm_fwd(d1, wp, scale_f)
    br1, dxo, loss_part = _outproj_fwd(y1, ug1, x1, gt1, wout[1], lg[1], lb[1], target, "outproj_fwd1")

    row_wout = d
    row_tail = d + es
    wq = 2 * (nb // N_CHIPS) * LANE * LANE // width
    whole = lambda r: (r + 2 * RS_TILE - 1) // (2 * RS_TILE) * (2 * RS_TILE)
    rows1 = whole(row_tail + pg // 2)
    rows0 = whole(row_tail + 2 * wq)
    fresh = lambda rows, used: (lax.empty if rows == used else jnp.zeros)((N_CHIPS, rows, width), F32)
    gbuf1 = fresh(rows1, row_tail + pg // 2)
    gbuf0 = fresh(rows0, row_tail + 2 * wq)

    dy1, dg1, dxres1, dbr1, dlg1, dlb1, dgt1 = _outproj_bwd(dxo, x1, br1, y1, ug1, gt1, lg[1], wout[1], "outproj_bwd1")
    gbuf1 = _outproj_bwd_w(y1, ug1, dbr1, gbuf1, row_wout, "outproj_bwd_w1")
    dd1, gbuf1, dscale = _pool_mm_bwd(dy1, d1, wp, scale_f, gbuf1, row_tail)
    du1 = _pool_map(dd1, nb, True, False, "pool_bwd")
    dx1, dsc1, dsh1 = _inproj_bwd_x([du1, dg1], x1, dxres1, sc1, win[1], "inproj_bwd_x1")
    gbuf1 = _inproj_bwd_w(x1, sc1, sh1, [du1, dg1], None, gbuf1, "inproj_bwd_w1")
    rs1, token1 = _rs_begin(gbuf1, place, "rs_exchange1")

    dy0, dg0, dxres0, dbr0, dlg0, dlb0, dgt0 = _outproj_bwd(dx1, xs, br0, y0, ug0, gt0 + token1[0:1, 0:1], lg[0], wout[0],
                                                            "outproj_bwd0")
    gbuf0 = _outproj_bwd_w(y0, ug0, dbr0, gbuf0, row_wout, "outproj_bwd_w0")
    du0, duc, dconv_w, dconv_b, dwa, dwx, dba, dbx, dlam = _rglru_bwd(uu0, uc0, dy0, *lru_args)
    dwin0c = _inproj_bwd_w(cx, scc, shc, [duc, jnp.zeros_like(duc)], None, None, "inproj_bwd_w_ctx")
    gbuf0 = _inproj_bwd_w(xs, sc0, sh0, [du0, dg0], dwin0c, gbuf0, "inproj_bwd_w0")

    def quarter(dw):
        t = dw.reshape(2, N_CHIPS, nb // N_CHIPS, LANE, LANE)
        return jnp.transpose(t, (1, 3, 0, 2, 4)).reshape(N_CHIPS, LANE, 2 * (nb // N_CHIPS) * LANE).reshape(N_CHIPS, wq, width)

    tail0 = jnp.concatenate([quarter(dwa), quarter(dwx)], axis=1)
    gbuf0 = lax.dynamic_update_slice(gbuf0, tail0, (0, row_tail, 0))
    red1 = _rs_end(rs1, place, gbuf0, "rs_exchange1")
    rs0, token0 = _rs_begin(gbuf0, place, "rs_exchange0")
    grad_x, dsc0, dsh0 = _inproj_bwd_x([du0, dg0], xs, dxres0, sc0 + token0[0:1, 0:1], win[0], "inproj_bwd_x0")
    dscc, dshc = _inproj_bwd_x([duc], cx, None, scc, win[0][:2], "inproj_bwd_x_ctx")

    k0 = VEC_KINDS
    vec = _rows_kernel(
        [(c, 0, 0), (loss_part, 0, d), (dsh0, 1, 0), (dsc0, 1, d), (dgt0, 1, 2 * d), (dshc, 2, 0), (dscc, 2, d),
         (dsh1, 3, 0), (dsc1, 3, d), (dgt1, 3, 2 * d),
         (dconv_b, k0, 0), (dlg0, k0, e), (dscale, k0 + 1, 0), (dlg1, k0 + 1, e), (dlb0, k0 + 2, 0), (dlb1, k0 + 2, d),
         (dconv_w, k0 + 3, 0), (dba, k0 + 7, 0), (dbx, k0 + 9, 0), (dlam, k0 + 11, 0)], VEC_ROWS, 3 * d, "pack_vec")
    v_send, v_recv, vec_l, vec_lands, v_token = _push_start([vec], [_own_slab(vec, devices=True)], "devices", vec,
                                                            "gather_devices_start")
    red0 = _rs_end(rs0, place, v_token, "rs_exchange0")
    quarters = red0[row_tail:row_tail + 2 * wq]
    q_send, q_recv, q_src, q_lands, q_token = _push_start([quarters], [_own_slab(quarters)], "same", red0,
                                                          "gather_replicated_start")

    tmw = _row_tile(d, 256)
    red_src = lambda red, r0, tm: (red, (tm, width), lambda n, i: (r0 // tm + i, 0))
    by_layer = lambda n, gs: jnp.where(n == 0, gs[0], gs[1])
    outs = {}
    outs["w_in"] = _adamw_param(w_in, m_w_in, v_w_in, [red_src(red0, 0, tmw), red_src(red1, 0, tmw)], by_layer, tmw, "adamw_w_in",
                                after=q_token)
    outs["w_out"] = _adamw_param(w_out, m_w_out, v_w_out, [red_src(red0, row_wout, tmw), red_src(red1, row_wout, tmw)],
                                 by_layer, tmw, "adamw_w_out")
    pw = [a.reshape(ng, pq, pg) for a in (pool_w, m_pool_w, v_pool_w)]
    outs["pool_w"] = [o.reshape(pool_w.shape) for o in _adamw_param(
        *pw, [(red1, (pq, pg), lambda n, i: (row_tail // pq + n // 2, n % 2))], lambda n, gs: gs[0], pq, "adamw_pool_w")]

    (gathered,) = _push_wait(v_send, v_recv, vec_l, vec_lands, "devices", outs["w_out"][1], "gather_devices_wait")
    gt_all = jnp.swapaxes(gathered, 0, 1)
    g_wmod = _mod_bwd_shard(gt_all, cctx2, place, c3)
    g_bmod, g_cctx, sq_err, g_small = _mod_bwd_rep(gt_all, cctx2, wm_all)
    loss = sq_err[0, 0] * (0.5 / d)
    outs["w_mod"] = _adamw_param(w_mod, m_w_mod, v_w_mod, [(g_wmod, (None, tmw, c3), lambda n, i: (n, i, 0))],
                                 lambda n, gs: gs[0], tmw, "adamw_w_mod")
    (rep,) = _push_wait(q_send, q_recv, q_src, q_lands, "same", outs["w_mod"][1], "gather_replicated_wait")
    bq = nb // N_CHIPS
    rep_src = lambda r0: (rep, (None, LANE, bq * LANE), lambda n, i: (n % N_CHIPS, r0 // LANE, n // N_CHIPS))
    stack = lambda n, gs: jnp.concatenate([gs[0][:, k * LANE:(k + 1) * LANE] for k in range(bq)], axis=0)
    for name, r0, trio in (("lru_wa", 0, (lru_wa, m_lru_wa, v_lru_wa)), ("lru_wx", wq, (lru_wx, m_lru_wx, v_lru_wx))):
        blocks = [a.reshape(2 * N_CHIPS, bq * LANE, LANE) for a in trio]
        outs[name] = [o.reshape(lru_wa.shape) for o in _adamw_param(*blocks, [rep_src(r0)], stack, bq * LANE, "adamw_" + name)]

    g_small = dict(g_small, c_ctx=g_cctx, b_mod=g_bmod)
    for n in SMALL_GATHERED:
        g_small[n] = lax.dynamic_slice_in_dim(g_small[n], place[1] * eq, eq, axis=1)
    as2d = lambda a: a.reshape(-1, a.shape[-1])
    quads = [(as2d(weights[n]), g_small[n], as2d(mom1[n]), as2d(mom2[n])) for n in SMALL_UPDATED]
    for n, (q, res) in zip(SMALL_UPDATED, zip(quads, _adamw_small(quads))):
        outs[n] = [a.reshape(weights[n].shape) for a in (q[1],) + res]

    result = [loss, grad_x[None]]
    for j in range(4):
        result += [outs[n][j] for n in WEIGHTS]
    return tuple(result)
```

```python
import jax
import jax.numpy as jnp
from jax import lax
from jax.experimental import pallas as pl
from jax.experimental.pallas import tpu as pltpu

F32 = jnp.float32
BF16 = jnp.bfloat16
LANE = 128
SUB = 8
GRID_W = 64
POOL_WINDOWS = (2, 4, 8, 16)
LRU_C = 8.0
DEPTH = 2
ALPHA = float((2 * DEPTH) ** 0.25)
LN_EPS = 1e-5
ADAM_LR, ADAM_B1, ADAM_B2, ADAM_EPS, ADAM_WD, ADAM_STEP = 0.001, 0.9, 0.999, 1e-08, 0.01, 10
N_CHIPS = 4
N_DEV = 8
MESH = pl.DeviceIdType.MESH
ROW_TILE = 512
GATE_TILE = 2048
GATE_BWD_TILE = 1024
GATE_UNROLL = 1
CONV_TAPS = 4
CONV_LEFT = 2
PAD = 8
SCAN_UNROLL = 32
RS_TILE = 448
LN_ROWS = 128
POOL_CPAD = 16
VEC_KINDS = 4


def _call(body, **kw):
    return pl.pallas_call(body, **kw)


def _dot(a, b):
    return jnp.dot(a, b, preferred_element_type=F32)


def _dot_nt(a, b):
    return lax.dot_general(a, b, (((1,), (1,)), ((), ())), preferred_element_type=F32)


def _dot_tn(a, b):
    return lax.dot_general(a, b, (((0,), (0,)), ((), ())), preferred_element_type=F32)


def _sigmoid(v):
    return 0.5 * (jnp.tanh(0.5 * v) + 1.0)


def _silu(v):
    return v * _sigmoid(v)


def _dsilu(v):
    s = _sigmoid(v)
    return s * (1.0 + v * (1.0 - s))


def _log_sigmoid(v):
    z = jnp.exp(-jnp.abs(v))
    return jnp.minimum(v, 0.0) - jnp.where(z < 1e-4, z * (1.0 - 0.5 * z), jnp.log(1.0 + z))


def _one_minus_sq(la, a):
    return -jnp.tanh(la) * (a * a + 1.0)


def _cat(ref, n):
    return jnp.concatenate([ref[k] for k in range(n)], axis=1)


def _put_chunks(ref, val, n, base=0):
    for k in range(n):
        ref[base + k] = val[:, k * LANE:(k + 1) * LANE].astype(ref.dtype)


def _row_tile(rows, want):
    t = min(rows, want)
    assert rows % t == 0
    return t


ANY_SPEC = pl.BlockSpec(memory_space=pl.ANY)


def _mod_fwd(c_all, cctx, wm, bm):
    nl, d, c3 = wm.shape

    def body(c_ref, cx_ref, w_ref, b_ref, o_ref):
        cc = jnp.concatenate([c_ref[...], cx_ref[...], jnp.zeros((SUB - 1, d), F32)], axis=0)
        o_ref[...] = _dot(_silu(cc).astype(BF16), w_ref[...]) + b_ref[...]

    return _call(
        body, name="mod_fwd", grid=(nl,),
        in_specs=[pl.BlockSpec((N_DEV, d), lambda l: (0, 0)),
                  pl.BlockSpec((1, d), lambda l: (0, 0)),
                  pl.BlockSpec((None, d, c3), lambda l: (l, 0, 0)),
                  pl.BlockSpec((None, 1, c3), lambda l: (l, 0, 0))],
        out_specs=pl.BlockSpec((None, 2 * SUB, c3), lambda l: (l, 0, 0)),
        out_shape=jax.ShapeDtypeStruct((nl, 2 * SUB, c3), F32),
    )(c_all, cctx, wm, bm)


def _rows_kernel(parts, rows, cols, name):
    def body(*refs):
        o_ref = refs[-1]
        o_ref[...] = jnp.zeros_like(o_ref)
        for ref, (a, r0, c0) in zip(refs[:-1], parts):
            for k in range(a.shape[0]):
                o_ref[r0 + k:r0 + k + 1, c0:c0 + a.shape[1]] = ref[k:k + 1, :]

    return _call(body, name=name, grid=(1,),
                 in_specs=[pl.BlockSpec(a.shape, lambda i: (0, 0)) for a, _, _ in parts],
                 out_specs=pl.BlockSpec((rows, cols), lambda i: (0, 0)),
                 out_shape=jax.ShapeDtypeStruct((rows, cols), F32))(*[a for a, _, _ in parts])


def _mod_bwd_shard(gt, cctx, place, c3):
    d = cctx.shape[1]

    def body(p_ref, cs_ref, dm_ref, dmx_ref, cx_ref, o_ref):
        l = pl.program_id(0)
        lhs = jnp.concatenate([_silu(cs_ref[...]), _silu(cx_ref[...]), jnp.zeros((7, d), F32)], axis=0).astype(BF16)
        dmx = jnp.where(l == 0, jnp.sum(dmx_ref[...], axis=0, keepdims=True), 0.0)
        rhs = jnp.concatenate([dm_ref[...], dmx, jnp.zeros((7, c3), F32)], axis=0).astype(BF16)
        o_ref[...] = _dot_tn(lhs, rhs)

    return _call(
        body, name="mod_bwd_shard",
        grid_spec=pltpu.PrefetchScalarGridSpec(
            num_scalar_prefetch=1, grid=(DEPTH,),
            in_specs=[pl.BlockSpec((None, N_DEV, d), lambda l, p: (0, 0, 0)),
                      pl.BlockSpec((None, N_DEV, c3), lambda l, p: (1 + 2 * l, 0, p[1])),
                      pl.BlockSpec((None, N_DEV, c3), lambda l, p: (2, 0, p[1])),
                      pl.BlockSpec((1, d), lambda l, p: (0, 0))],
            out_specs=pl.BlockSpec((None, d, c3), lambda l, p: (l, 0, 0))),
        out_shape=jax.ShapeDtypeStruct((DEPTH, d, c3), F32),
    )(place, gt, gt, gt, cctx)


def _small_layout(d, e):
    k = VEC_KINDS
    return {
        "conv_b": ((1, e), [(0, k, 0)]),
        "ln_g": ((2, d), [(0, k, e), (1, k + 1, e)]),
        "pool_scale": ((1, e), [(0, k + 1, 0)]),
        "ln_b": ((2, d), [(0, k + 2, 0), (1, k + 2, d)]),
        "conv_w": ((CONV_TAPS, e), [(t, k + 3 + t, 0) for t in range(CONV_TAPS)]),
        "lru_ba": ((2, e), [(j, k + 7 + j, 0) for j in range(2)]),
        "lru_bx": ((2, e), [(j, k + 9 + j, 0) for j in range(2)]),
        "lru_lam": ((2, e), [(j, k + 11 + j, 0) for j in range(2)]),
    }


VEC_ROWS = 24


def _mod_bwd_rep(gt, d):
    d3 = gt.shape[2]
    layout = _small_layout(d, d3 - d)
    names = list(layout)

    def body(g_ref, db_ref, loss_ref, *small_refs):
        loss_ref[...] = jnp.zeros_like(loss_ref) + jnp.sum(g_ref[0][:, d:d + LANE])
        dm0 = jnp.sum(g_ref[1], axis=0, keepdims=True)
        dmx = jnp.sum(g_ref[2], axis=0, keepdims=True)
        dm1 = jnp.sum(g_ref[3], axis=0, keepdims=True)
        db_ref[0:1, :] = dm0 + dmx
        db_ref[1:2, :] = dm1
        for ref, name in zip(small_refs, names):
            shape, places = layout[name]
            for arr_row, vec_row, col0 in places:
                total = jnp.sum(g_ref[vec_row], axis=0, keepdims=True)
                ref[arr_row:arr_row + 1, :] = total[:, col0:col0 + shape[1]]

    outs = _call(
        body, name="mod_bwd_rep", grid=(1,),
        in_specs=[pl.BlockSpec(gt.shape, lambda i: (0, 0, 0))],
        out_specs=[pl.BlockSpec((DEPTH, d3), lambda i: (0, 0)), pl.BlockSpec((1, LANE), lambda i: (0, 0))]
        + [pl.BlockSpec(layout[n][0], lambda i: (0, 0)) for n in names],
        out_shape=[jax.ShapeDtypeStruct((DEPTH, d3), F32), jax.ShapeDtypeStruct((1, LANE), F32)]
        + [jax.ShapeDtypeStruct(layout[n][0], F32) for n in names],
    )(gt)
    return outs[0], outs[1], dict(zip(names, outs[2:]))


def _cctx_partial(gt, wm0, place):
    d, c3 = wm0.shape

    def body(p_ref, dmx_ref, w_ref, o_ref):
        dmx = jnp.sum(dmx_ref[...], axis=0, keepdims=True)
        o_ref[...] = _dot_nt(jnp.broadcast_to(dmx, (2 * SUB, c3)).astype(BF16), w_ref[...])

    return _call(
        body, name="cctx_partial",
        grid_spec=pltpu.PrefetchScalarGridSpec(
            num_scalar_prefetch=1, grid=(1,),
            in_specs=[pl.BlockSpec((None, N_DEV, c3), lambda i, p: (2, 0, p[1])), pl.BlockSpec((d, c3), lambda i, p: (0, 0))],
            out_specs=pl.BlockSpec((2 * SUB, d), lambda i, p: (0, 0))),
        out_shape=jax.ShapeDtypeStruct((2 * SUB, d), F32),
    )(place, gt, wm0)


def _cctx_finish(parts, cctx):
    d = cctx.shape[1]

    def body(p_ref, cx_ref, o_ref):
        total = (p_ref[0, 0:1, :] + p_ref[1, 0:1, :]) + (p_ref[2, 0:1, :] + p_ref[3, 0:1, :])
        o_ref[...] = total * _dsilu(cx_ref[...])

    return _call(body, name="cctx_finish", grid=(1,),
                 in_specs=[pl.BlockSpec(parts.shape, lambda i: (0, 0, 0)), pl.BlockSpec((1, d), lambda i: (0, 0))],
                 out_specs=pl.BlockSpec((1, d), lambda i: (0, 0)),
                 out_shape=jax.ShapeDtypeStruct((1, d), F32))(parts, cctx)


def _inproj_fwd(xin, sc1, sh, w, name):
    rows, d = xin.shape
    ns, _, n4 = w.shape
    cpb = n4 // LANE
    tm = _row_tile(rows, 512)
    assert ns in (2, 4)

    def body(x_ref, sc_ref, sh_ref, w_ref, *o_refs):
        h = (x_ref[...] * sc_ref[...] + sh_ref[...]).astype(BF16)
        for s in range(ns):
            _put_chunks(o_refs[s // 2], _dot(h, w_ref[s]), cpb, base=(s % 2) * cpb)

    spec = pl.BlockSpec((2 * cpb, tm, LANE), lambda i: (0, i, 0))
    dtypes = (F32, BF16)[:ns // 2]
    res = _call(
        body, name=name, grid=(rows // tm,),
        in_specs=[pl.BlockSpec((tm, d), lambda i: (i, 0)),
                  pl.BlockSpec((1, d), lambda i: (0, 0)),
                  pl.BlockSpec((1, d), lambda i: (0, 0)),
                  pl.BlockSpec((ns, d, n4), lambda i: (0, 0, 0))],
        out_specs=[spec] * len(dtypes),
        out_shape=[jax.ShapeDtypeStruct((2 * cpb, rows, LANE), t) for t in dtypes],
    )(xin, sc1, sh, w)
    return res[0] if ns == 2 else tuple(res)


def _inproj_bwd_x(dparts, xin, dxres, sc1, w, name):
    rows, d = xin.shape
    npart = len(dparts)
    e = dparts[0].shape[1]
    ns, _, n4 = w.shape
    per = e // n4
    assert per * npart == ns
    tm = _row_tile(rows, 512)
    has_res = dxres is not None

    def body(*refs):
        dp = refs[:npart]
        x_ref, sc_ref, w_ref = refs[npart:npart + 3]
        rest = refs[npart + 3:]
        if has_res:
            res_ref, dx_ref, dsc_ref, dsh_ref = rest
        else:
            dsc_ref, dsh_ref = rest
        i = pl.program_id(0)
        dh = jnp.zeros((tm, d), F32)
        for p in range(npart):
            v = dp[p][...]
            for q in range(per):
                dh = dh + _dot_nt(v[:, q * n4:(q + 1) * n4], w_ref[p * per + q])

        @pl.when(i == 0)
        def _():
            dsc_ref[...] = jnp.zeros_like(dsc_ref)
            dsh_ref[...] = jnp.zeros_like(dsh_ref)

        dsc_ref[...] += jnp.sum(dh * x_ref[...], axis=0, keepdims=True)
        dsh_ref[...] += jnp.sum(dh, axis=0, keepdims=True)
        if has_res:
            dx_ref[...] = res_ref[...] + dh * sc_ref[...]

    row_spec = pl.BlockSpec((tm, d), lambda i: (i, 0))
    vec_spec = pl.BlockSpec((1, d), lambda i: (0, 0))
    in_specs = [pl.BlockSpec((tm, e), lambda i: (i, 0))] * npart + [row_spec, vec_spec,
                                                                     pl.BlockSpec((ns, d, n4), lambda i: (0, 0, 0))]
    args = list(dparts) + [xin, sc1, w]
    out_specs, out_shape = [vec_spec, vec_spec], [jax.ShapeDtypeStruct((1, d), F32)] * 2
    if has_res:
        in_specs.append(row_spec)
        args.append(dxres)
        out_specs = [row_spec] + out_specs
        out_shape = [jax.ShapeDtypeStruct((rows, d), F32)] + out_shape
    return _call(body, name=name, grid=(rows // tm,), in_specs=in_specs, out_specs=out_specs, out_shape=out_shape)(*args)


def _inproj_bwd_w(xin, sc1, sh, dparts, init, gbuf, name):
    rows, d = xin.shape
    npart = len(dparts)
    e = dparts[0].shape[1]
    n4 = e // 2
    ns = 2 * npart
    tm = _row_tile(rows, 1024)
    nt = rows // tm
    has_init = init is not None
    into = gbuf is not None
    assert not into or (ns == N_CHIPS and gbuf.shape[2] == n4)

    def body(*refs):
        x_ref, sc_ref, sh_ref = refs[:3]
        dp = refs[3:3 + npart]
        init_ref = refs[3 + npart] if has_init else None
        o_ref = refs[-1]
        s, i = pl.program_id(0), pl.program_id(1)
        h = (x_ref[...] * sc_ref[...] + sh_ref[...]).astype(BF16)

        @pl.when(i == 0)
        def _():
            o_ref[...] = init_ref[...] if has_init else jnp.zeros_like(o_ref)

        for p in range(npart):
            @pl.when(s // 2 == p)
            def _(p=p):
                o_ref[...] += _dot_tn(h, dp[p][...])

    in_specs = [pl.BlockSpec((tm, d), lambda s, i: (i, 0)),
                pl.BlockSpec((1, d), lambda s, i: (0, 0)),
                pl.BlockSpec((1, d), lambda s, i: (0, 0))]
    in_specs += [pl.BlockSpec((tm, n4), lambda s, i: (i, s % 2))] * npart
    args = [xin, sc1, sh] + list(dparts)
    o_spec = pl.BlockSpec((None, d, n4), lambda s, i: (s, 0, 0))
    if has_init:
        in_specs.append(o_spec)
        args.append(init)
    extra = {}
    if into:
        in_specs.append(ANY_SPEC)
        args.append(gbuf)
        extra = dict(input_output_aliases={len(args) - 1: 0})
    out_shape = jax.ShapeDtypeStruct(gbuf.shape if into else (ns, d, n4), F32)
    return _call(body, name=name, grid=(ns, nt), in_specs=in_specs, out_specs=o_spec, out_shape=out_shape, **extra)(*args)


def _gated(y_ref, g_ref, nch):
    return jnp.concatenate([(y_ref[k].astype(F32) * _silu(g_ref[k].astype(F32))).astype(BF16) for k in range(nch)], axis=1)


def _ln_stats(r):
    mu = jnp.mean(r, axis=-1, keepdims=True)
    var = jnp.mean(jnp.square(r - mu), axis=-1, keepdims=True)
    rstd = lax.rsqrt(var + LN_EPS)
    return (r - mu) * rstd, rstd


def _outproj_fwd(y, ug, xin, gt, wout, lg, lb, target, name):
    nch, rows, _ = y.shape
    e, d = wout.shape
    tm = _row_tile(rows, 512)
    with_loss = target is not None

    def body(*refs):
        y_ref, g_ref, x_ref, gt_ref, w_ref, lg_ref, lb_ref = refs[:7]
        if with_loss:
            t_ref, br_ref, dxo_ref, loss_ref = refs[7:]
        else:
            br_ref, xo_ref = refs[7:]
        z = _gated(y_ref, g_ref, nch)
        br_ref[...] = _dot(z, w_ref[...])
        if with_loss:
            @pl.when(pl.program_id(0) == 0)
            def _():
                loss_ref[...] = jnp.zeros_like(loss_ref)

        def norm(j, c):
            rows = pl.ds(pl.multiple_of(j * LN_ROWS, LN_ROWS), LN_ROWS)
            xhat, _ = _ln_stats(ALPHA * x_ref[rows, :] + gt_ref[...] * br_ref[rows, :])
            xo = xhat * lg_ref[...] + lb_ref[...]
            if with_loss:
                err = xo - t_ref[rows, :]
                dxo_ref[rows, :] = err * (1.0 / d)
                col = jnp.sum(err * err, axis=0, keepdims=True)
                loss_ref[...] += sum(col[:, k * LANE:(k + 1) * LANE] for k in range(d // LANE))
            else:
                xo_ref[rows, :] = xo
            return c

        lax.fori_loop(0, tm // LN_ROWS, norm, 0)

    chunk_spec = pl.BlockSpec((nch, tm, LANE), lambda i: (0, i, 0))
    g_spec = chunk_spec
    row_spec = pl.BlockSpec((tm, d), lambda i: (i, 0))
    vec_spec = pl.BlockSpec((1, d), lambda i: (0, 0))
    in_specs = [chunk_spec, g_spec, row_spec, vec_spec, pl.BlockSpec((e, d), lambda i: (0, 0)), vec_spec, vec_spec]
    args = [y, ug, xin, gt, wout, lg, lb]
    out_specs = [row_spec, row_spec]
    out_shape = [jax.ShapeDtypeStruct((rows, d), F32)] * 2
    if with_loss:
        in_specs.append(row_spec)
        args.append(target)
        out_specs.append(pl.BlockSpec((1, LANE), lambda i: (0, 0)))
        out_shape.append(jax.ShapeDtypeStruct((1, LANE), F32))
    return _call(body, name=name, grid=(rows // tm,), in_specs=in_specs, out_specs=out_specs, out_shape=out_shape)(*args)


def _outproj_bwd(dxo, xin, br, y, ug, gt, lg, wout, name):
    nch, rows, _ = y.shape
    e, d = wout.shape
    tm = _row_tile(rows, 256)

    def body(dxo_ref, x_ref, br_ref, y_ref, g_ref, gt_ref, lg_ref, w_ref,
             dy_ref, dg_ref, dxres_ref, dbr_ref, dlg_ref, dlb_ref, dgt_ref):
        @pl.when(pl.program_id(0) == 0)
        def _():
            dlg_ref[...] = jnp.zeros_like(dlg_ref)
            dlb_ref[...] = jnp.zeros_like(dlb_ref)
            dgt_ref[...] = jnp.zeros_like(dgt_ref)

        def norm_bwd(j, c):
            rows = pl.ds(pl.multiple_of(j * LN_ROWS, LN_ROWS), LN_ROWS)
            dxo_v = dxo_ref[rows, :]
            brv = br_ref[rows, :]
            xhat, rstd = _ln_stats(ALPHA * x_ref[rows, :] + gt_ref[...] * brv)
            dxh = dxo_v * lg_ref[...]
            dr = rstd * (dxh - jnp.mean(dxh, axis=-1, keepdims=True) - xhat * jnp.mean(dxh * xhat, axis=-1, keepdims=True))
            dlg_ref[...] += jnp.sum(dxo_v * xhat, axis=0, keepdims=True)
            dlb_ref[...] += jnp.sum(dxo_v, axis=0, keepdims=True)
            dgt_ref[...] += jnp.sum(dr * brv, axis=0, keepdims=True)
            dxres_ref[rows, :] = ALPHA * dr
            dbr_ref[rows, :] = (gt_ref[...] * dr).astype(BF16)
            return c

        lax.fori_loop(0, tm // LN_ROWS, norm_bwd, 0)
        dz = _dot_nt(dbr_ref[...], w_ref[...])
        for k in range(nch):
            dzk = dz[:, k * LANE:(k + 1) * LANE]
            gk = g_ref[k].astype(F32)
            dy_ref[k] = dzk * _silu(gk)
            dg_ref[:, k * LANE:(k + 1) * LANE] = (dzk * y_ref[k].astype(F32) * _dsilu(gk)).astype(BF16)

    chunk_spec = pl.BlockSpec((nch, tm, LANE), lambda i: (0, i, 0))
    g_spec = chunk_spec
    row_spec = pl.BlockSpec((tm, d), lambda i: (i, 0))
    vec_spec = pl.BlockSpec((1, d), lambda i: (0, 0))
    return _call(
        body, name=name, grid=(rows // tm,),
        in_specs=[row_spec, row_spec, row_spec, chunk_spec, g_spec, vec_spec, vec_spec, pl.BlockSpec((e, d), lambda i: (0, 0))],
        out_specs=[chunk_spec, pl.BlockSpec((tm, e), lambda i: (i, 0)), row_spec, row_spec, vec_spec, vec_spec, vec_spec],
        out_shape=[jax.ShapeDtypeStruct((nch, rows, LANE), F32), jax.ShapeDtypeStruct((rows, e), BF16),
                   jax.ShapeDtypeStruct((rows, d), F32), jax.ShapeDtypeStruct((rows, d), BF16)]
        + [jax.ShapeDtypeStruct((1, d), F32)] * 3,
    )(dxo, xin, br, y, ug, gt, lg, wout)


def _outproj_bwd_w(y, ug, dbr, gbuf, row0, name):
    nch, rows, _ = y.shape
    d = dbr.shape[1]
    e = nch * LANE
    es = e // N_CHIPS
    tm = _row_tile(rows, 1024)
    assert gbuf.shape[2] == d and row0 % es == 0

    def body(y_ref, g_ref, dbr_ref, buf_ref, o_ref):
        @pl.when(pl.program_id(0) == 0)
        def _():
            o_ref[...] = jnp.zeros_like(o_ref)

        z = _gated(y_ref, g_ref, nch)
        o_ref[...] += _dot_tn(z, dbr_ref[...]).reshape(N_CHIPS, es, d)

    return _call(
        body, name=name, grid=(rows // tm,),
        in_specs=[pl.BlockSpec((nch, tm, LANE), lambda i: (0, i, 0)),
                  pl.BlockSpec((nch, tm, LANE), lambda i: (0, i, 0)),
                  pl.BlockSpec((tm, d), lambda i: (i, 0)),
                  ANY_SPEC],
        out_specs=pl.BlockSpec((N_CHIPS, es, d), lambda i: (0, row0 // es, 0)),
        out_shape=jax.ShapeDtypeStruct(gbuf.shape, F32),
        input_output_aliases={3: 0},
    )(y, ug, dbr, gbuf)


def _scan(a_ref, b_ref, h_ref, *, length, init, reverse, a_shift, store):
    nblk = length // SUB
    unroll = min(SCAN_UNROLL, nblk)
    assert nblk % unroll == 0
    row = lax.broadcasted_iota(jnp.int32, (SUB, LANE), 0)
    last = 0 if reverse else SUB - 1
    edges = [(row >= SUB - k) if reverse else (row < k) for k in (1, 2, 4)]

    def local_scan(a, b):
        for k, edge in zip((1, 2, 4), edges):
            sh = (SUB - k) if reverse else k
            b = b + a * jnp.where(edge, 0.0, pltpu.roll(b, sh, 0))
            a = a * jnp.where(edge, 1.0, pltpu.roll(a, sh, 0))
        return a, b

    def step(i, carry):
        base = pl.multiple_of(((nblk // unroll - 1 - i) if reverse else i) * (unroll * SUB), unroll * SUB)
        order = range(unroll - 1, -1, -1) if reverse else range(unroll)
        loaded = [(a_ref[pl.ds(PAD + base + j * SUB + a_shift, SUB), :], b_ref[pl.ds(PAD + base + j * SUB, SUB), :])
                  for j in order]
        scanned = [local_scan(a, b) for a, b in loaded]
        for j, (a, b) in zip(order, scanned):
            if store:
                h_ref[pl.ds(PAD + base + j * SUB, SUB), :] = b + a * carry
            a_l = jnp.broadcast_to(a[last:last + 1, :], (SUB, LANE))
            b_l = jnp.broadcast_to(b[last:last + 1, :], (SUB, LANE))
            carry = b_l + a_l * carry
        return carry

    carry = lax.fori_loop(0, nblk // unroll, step, jnp.broadcast_to(init, (SUB, LANE)))
    return carry[0:1, :]


def _conv_fwd(src_ref, upad, u_ref, cw, cb, length):
    zeros = jnp.zeros((PAD, LANE), F32)
    upad[pl.ds(0, PAD), :] = zeros
    upad[pl.ds(PAD + length, PAD), :] = zeros
    rt = _row_tile(length, ROW_TILE)

    def copy(i, c):
        t0 = pl.multiple_of(i * rt, rt)
        upad[pl.ds(PAD + t0, rt), :] = src_ref[pl.ds(t0, rt), :]
        return c

    lax.fori_loop(0, length // rt, copy, 0)

    def tile(i, c):
        t0 = pl.multiple_of(i * rt, rt)
        acc = jnp.zeros((rt, LANE), F32)
        for k in range(CONV_TAPS):
            acc = acc + upad[pl.ds(t0 + PAD - CONV_LEFT + k, rt), :] * cw[k:k + 1, :]
        u_ref[pl.ds(t0, rt), :] = acc + cb
        return c

    lax.fori_loop(0, length // rt, tile, 0)


def _gates_fwd(u_ref, a_ref, b_ref, wa, wx, ba, bx, ls, length, keep=None):
    rt = _row_tile(length, GATE_TILE)

    def tile(i, c):
        t0 = pl.multiple_of(i * rt, rt)
        ut = u_ref[pl.ds(t0, rt), :]
        ub = ut.astype(BF16)
        r = 0.5 * (jnp.tanh(_dot(ub, wa) + ba) + 1.0)
        ig = 0.5 * (jnp.tanh(_dot(ub, wx) + bx) + 1.0)
        if keep is not None:
            keep[0][pl.ds(t0, rt), :] = r
            keep[1][pl.ds(t0, rt), :] = ig
        la = (LRU_C * r) * ls
        a = jnp.exp(la)
        a_ref[pl.ds(PAD + t0, rt), :] = a
        b_ref[pl.ds(PAD + t0, rt), :] = jnp.sqrt(_one_minus_sq(la, a)) * (ig * ut)
        return c

    lax.fori_loop(0, length // rt, tile, 0, unroll=min(GATE_UNROLL, length // rt))


def _lru_specs():
    return [pl.BlockSpec((CONV_TAPS, LANE), lambda n: (0, n)),
            pl.BlockSpec((1, LANE), lambda n: (0, n)),
            pl.BlockSpec((2, None, LANE, LANE), lambda n: (0, n, 0, 0)),
            pl.BlockSpec((2, None, LANE, LANE), lambda n: (0, n, 0, 0)),
            pl.BlockSpec((2, LANE), lambda n: (0, n)),
            pl.BlockSpec((2, LANE), lambda n: (0, n)),
            pl.BlockSpec((2, LANE), lambda n: (0, n))]


def _rglru_fwd(ug, uc, conv_w, conv_b, wa, wx, ba, bx, lam):
    nb = uc.shape[0]
    s_len, t_len = ug.shape[1], uc.shape[1]

    def body(u0_ref, uc0_ref, cw_ref, cb_ref, wa_ref, wx_ref, ba_ref, bx_ref, lam_ref, y_ref,
             upad, ubuf, abuf, hbuf):
        cw, cb = cw_ref[...], cb_ref[...]
        lsig = _log_sigmoid(lam_ref[...])
        zero = jnp.zeros((1, LANE), F32)
        _conv_fwd(uc0_ref, upad, ubuf, cw, cb, t_len)
        h0 = []
        for dr in range(2):
            _gates_fwd(ubuf, abuf, hbuf, wa_ref[dr], wx_ref[dr], ba_ref[dr:dr + 1, :], bx_ref[dr:dr + 1, :],
                       lsig[dr:dr + 1, :], t_len)
            h0.append(_scan(abuf, hbuf, hbuf, length=t_len, init=zero, reverse=(dr == 1), a_shift=0, store=False))
        _conv_fwd(u0_ref, upad, ubuf, cw, cb, s_len)
        rt = _row_tile(s_len, ROW_TILE)
        for dr in range(2):
            _gates_fwd(ubuf, abuf, hbuf, wa_ref[dr], wx_ref[dr], ba_ref[dr:dr + 1, :], bx_ref[dr:dr + 1, :],
                       lsig[dr:dr + 1, :], s_len)
            _scan(abuf, hbuf, hbuf, length=s_len, init=h0[dr], reverse=(dr == 1), a_shift=0, store=True)

            def acc(i, c, dr=dr):
                t0 = pl.multiple_of(i * rt, rt)
                h = hbuf[pl.ds(PAD + t0, rt), :]
                if dr == 0:
                    upad[pl.ds(PAD + t0, rt), :] = h
                else:
                    y_ref[pl.ds(t0, rt), :] = (upad[pl.ds(PAD + t0, rt), :] + h).astype(y_ref.dtype)
                return c

            lax.fori_loop(0, s_len // rt, acc, 0)

    seq = pltpu.VMEM((s_len + 2 * PAD, LANE), F32)
    return _call(
        body, name="rglru_fwd", grid=(nb,),
        in_specs=[pl.BlockSpec((None, s_len, LANE), lambda n: (n, 0, 0)),
                  pl.BlockSpec((None, t_len, LANE), lambda n: (n, 0, 0))] + _lru_specs(),
        out_specs=pl.BlockSpec((None, s_len, LANE), lambda n: (n, 0, 0)),
        out_shape=jax.ShapeDtypeStruct((nb, s_len, LANE), BF16),
        scratch_shapes=[seq, pltpu.VMEM((s_len, LANE), F32), seq, seq],
    )(ug, uc, conv_w, conv_b, wa, wx, ba, bx, lam)


def _rglru_bwd(ug, uc, dy, conv_w, conv_b, wa, wx, ba, bx, lam):
    nb = uc.shape[0]
    e = nb * LANE
    s_len, t_len = ug.shape[1], uc.shape[1]

    def body(u0_ref, uc0_ref, dy_ref, cw_ref, cb_ref, wa_ref, wx_ref, ba_ref, bx_ref, lam_ref,
             du_ref, duc_ref, dcw_ref, dcb_ref, dwa_ref, dwx_ref, dba_ref, dbx_ref, dlam_ref,
             upad, ubuf, abuf, hbuf, lbuf, dubuf, rbuf, ibuf, cpad, cu, ca0, ch0, ca1, ch1, cr0, ci0, cr1, ci1):
        cw, cb = cw_ref[...], cb_ref[...]
        lam_v = lam_ref[...]
        lsig = _log_sigmoid(lam_v)
        zero = jnp.zeros((1, LANE), F32)
        zpad = jnp.zeros((PAD, LANE), F32)
        for ref in (dcw_ref, dcb_ref, dwa_ref, dwx_ref, dba_ref, dbx_ref, dlam_ref):
            ref[...] = jnp.zeros_like(ref)

        def params(dr):
            return (wa_ref[dr], wx_ref[dr], ba_ref[dr:dr + 1, :], bx_ref[dr:dr + 1, :], lsig[dr:dr + 1, :])

        def direction_bwd(dr, u_ref, a_ref, h_ref, l_ref, gates, dub, length, first):
            wa_d, wx_d, ba_d, bx_d, ls_d = params(dr)
            rt = _row_tile(length, GATE_BWD_TILE)
            prev = 1 if dr == 1 else -1

            def tile(i, c):
                t0 = pl.multiple_of(i * rt, rt)
                ut = u_ref[pl.ds(t0, rt), :]
                ub = ut.astype(BF16)
                r = gates[0][pl.ds(t0, rt), :]
                ig = gates[1][pl.ds(t0, rt), :]
                la = (LRU_C * r) * ls_d
                a = a_ref[pl.ds(PAD + t0, rt), :]
                q = _one_minus_sq(la, a)
                rs = lax.rsqrt(q)
                sq = q * rs
                lm = l_ref[pl.ds(PAD + t0, rt), :]
                da = lm * h_ref[pl.ds(PAD + t0 + prev, rt), :]
                dsq = lm * ig * ut
                dig = lm * sq * ut
                dla = da * a - dsq * (a * a) * rs
                dr_ = dla * (LRU_C * ls_d)
                dlam_ref[dr:dr + 1, :] += jnp.sum(dla * (LRU_C * r), axis=0, keepdims=True)
                dpr = dr_ * r * (1.0 - r)
                dpi = dig * ig * (1.0 - ig)
                dba_ref[dr:dr + 1, :] += jnp.sum(dpr, axis=0, keepdims=True)
                dbx_ref[dr:dr + 1, :] += jnp.sum(dpi, axis=0, keepdims=True)
                dprb, dpib = dpr.astype(BF16), dpi.astype(BF16)
                dwa_ref[dr] += _dot_tn(ub, dprb)
                dwx_ref[dr] += _dot_tn(ub, dpib)
                dut = lm * sq * ig + 2.0 * (_dot_nt(dprb, wa_d) + _dot_nt(dpib, wx_d))
                if first:
                    dub[pl.ds(PAD + t0, rt), :] = dut
                else:
                    dub[pl.ds(PAD + t0, rt), :] += dut
                return c

            lax.fori_loop(0, length // rt, tile, 0, unroll=min(GATE_UNROLL, length // rt))

        def conv_bwd(dub, src_pad, out_ref, length):
            rt = _row_tile(length, ROW_TILE)

            def tile(i, c):
                t0 = pl.multiple_of(i * rt, rt)
                dut = dub[pl.ds(PAD + t0, rt), :]
                dcb_ref[...] += jnp.sum(dut, axis=0, keepdims=True)
                acc = jnp.zeros((rt, LANE), F32)
                for k in range(CONV_TAPS):
                    sh = CONV_LEFT - k
                    acc = acc + dub[pl.ds(PAD + t0 + sh, rt), :] * cw[k:k + 1, :]
                    dcw_ref[k:k + 1, :] += jnp.sum(dut * src_pad[pl.ds(PAD + t0 - sh, rt), :], axis=0, keepdims=True)
                out_ref[pl.ds(t0, rt), :] = acc.astype(out_ref.dtype)
                return c

            lax.fori_loop(0, length // rt, tile, 0)

        _conv_fwd(uc0_ref, cpad, cu, cw, cb, t_len)
        cbufs = ((ca0, ch0), (ca1, ch1))
        cgates = ((cr0, ci0), (cr1, ci1))
        h0 = []
        for dr in range(2):
            ca, chh = cbufs[dr]
            _gates_fwd(cu, ca, chh, *params(dr), t_len, keep=cgates[dr])
            h0.append(_scan(ca, chh, chh, length=t_len, init=zero, reverse=(dr == 1), a_shift=0, store=True))
        _conv_fwd(u0_ref, upad, ubuf, cw, cb, s_len)
        rt = _row_tile(s_len, ROW_TILE)
        dh0 = []
        for dr in range(2):
            rev = dr == 1
            _gates_fwd(ubuf, abuf, hbuf, *params(dr), s_len, keep=(rbuf, ibuf))
            _scan(abuf, hbuf, hbuf, length=s_len, init=h0[dr], reverse=rev, a_shift=0, store=True)
            first_row = PAD + s_len if rev else PAD - 1
            hbuf[pl.ds(first_row, 1), :] = h0[dr]
            end_row = PAD - 1 if rev else PAD + s_len
            abuf[pl.ds(end_row, 1), :] = zero

            def copy(i, c):
                t0 = pl.multiple_of(i * rt, rt)
                lbuf[pl.ds(PAD + t0, rt), :] = dy_ref[pl.ds(t0, rt), :]
                return c

            lax.fori_loop(0, s_len // rt, copy, 0)
            _scan(abuf, lbuf, lbuf, length=s_len, init=zero, reverse=not rev, a_shift=(-1 if rev else 1), store=True)
            start = PAD + s_len - 1 if rev else PAD
            dh0.append(abuf[pl.ds(start, 1), :] * lbuf[pl.ds(start, 1), :])
            direction_bwd(dr, ubuf, abuf, hbuf, lbuf, (rbuf, ibuf), dubuf, s_len, first=(dr == 0))
        dubuf[pl.ds(0, PAD), :] = zpad
        dubuf[pl.ds(PAD + s_len, PAD), :] = zpad
        conv_bwd(dubuf, upad, du_ref, s_len)
        lc = lbuf
        duc_buf = dubuf
        for dr in range(2):
            rev = dr == 1
            ca, chh = cbufs[dr]
            first_row = PAD + t_len if rev else PAD - 1
            chh[pl.ds(first_row, 1), :] = zero
            end_row = PAD - 1 if rev else PAD + t_len
            ca[pl.ds(end_row, 1), :] = zero + 1.0
            rtc = _row_tile(t_len, ROW_TILE)

            def clear(i, c):
                t0 = pl.multiple_of(i * rtc, rtc)
                lc[pl.ds(PAD + t0, rtc), :] = jnp.zeros((rtc, LANE), F32)
                return c

            lax.fori_loop(0, t_len // rtc, clear, 0)
            _scan(ca, lc, lc, length=t_len, init=dh0[dr], reverse=not rev, a_shift=(-1 if rev else 1), store=True)
            direction_bwd(dr, cu, ca, chh, lc, cgates[dr], duc_buf, t_len, first=(dr == 0))
        duc_buf[pl.ds(0, PAD), :] = zpad
        duc_buf[pl.ds(PAD + t_len, PAD), :] = zpad
        conv_bwd(duc_buf, cpad, duc_ref, t_len)
        dlam_ref[...] = dlam_ref[...] * (1.0 - _sigmoid(lam_v))

    seq = pltpu.VMEM((s_len + 2 * PAD, LANE), F32)
    cseq = pltpu.VMEM((t_len + 2 * PAD, LANE), F32)
    flat = pltpu.VMEM((s_len, LANE), F32)
    cflat = pltpu.VMEM((t_len, LANE), F32)
    vec2 = pl.BlockSpec((2, LANE), lambda n: (0, n))
    wspec = pl.BlockSpec((2, None, LANE, LANE), lambda n: (0, n, 0, 0))
    return _call(
        body, name="rglru_bwd", grid=(nb,),
        in_specs=[pl.BlockSpec((None, s_len, LANE), lambda n: (n, 0, 0)),
                  pl.BlockSpec((None, t_len, LANE), lambda n: (n, 0, 0)),
                  pl.BlockSpec((None, s_len, LANE), lambda n: (n, 0, 0))] + _lru_specs(),
        out_specs=[pl.BlockSpec((s_len, LANE), lambda n: (0, n)),
                   pl.BlockSpec((t_len, LANE), lambda n: (0, n)),
                   pl.BlockSpec((CONV_TAPS, LANE), lambda n: (0, n)),
                   pl.BlockSpec((1, LANE), lambda n: (0, n)),
                   wspec, wspec, vec2, vec2, vec2],
        out_shape=[jax.ShapeDtypeStruct((s_len, e), BF16), jax.ShapeDtypeStruct((t_len, e), BF16),
                   jax.ShapeDtypeStruct((CONV_TAPS, e), F32), jax.ShapeDtypeStruct((1, e), F32),
                   jax.ShapeDtypeStruct((2, nb, LANE, LANE), F32), jax.ShapeDtypeStruct((2, nb, LANE, LANE), F32),
                   jax.ShapeDtypeStruct((2, e), F32), jax.ShapeDtypeStruct((2, e), F32), jax.ShapeDtypeStruct((2, e), F32)],
        scratch_shapes=[seq, flat, seq, seq, seq, seq, flat, flat,
                        cseq, cflat, cseq, cseq, cseq, cseq, cflat, cflat, cflat, cflat],
    )(ug, uc, dy, conv_w, conv_b, wa, wx, ba, bx, lam)


def _pool_windows(src_ref, out_ref, colbuf, rowbuf, half, transpose, s_len):
    gw = GRID_W
    lg = gw.bit_length() - 1
    n_rows = s_len // gw
    cp, rm = POOL_CPAD, 8 * gw
    stride = gw + 2 * cp
    rt = _row_tile(s_len, ROW_TILE)
    assert rt % gw == 0 and half <= cp
    gpt = rt // gw
    offs = range(-half, half)
    zmargin = jnp.zeros((cp, LANE), F32)

    def zcol(r, c):
        base = pl.multiple_of(r * stride, SUB)
        colbuf[pl.ds(base, cp), :] = zmargin
        colbuf[pl.ds(base + cp + gw, cp), :] = zmargin
        return c

    lax.fori_loop(0, n_rows, zcol, 0)

    def zrow(i, c):
        t0 = pl.multiple_of(i * gw, gw)
        rowbuf[pl.ds(t0, gw), :] = jnp.zeros((gw, LANE), F32)
        rowbuf[pl.ds(rm + s_len + t0, gw), :] = jnp.zeros((gw, LANE), F32)
        return c

    lax.fori_loop(0, rm // gw, zrow, 0)

    col = lax.broadcasted_iota(jnp.int32, (gw, LANE), 0)
    ccnt = (jnp.minimum(col + half, gw) - jnp.maximum(col - half, 0)).astype(F32)

    def row_counts(t0):
        row = (t0 + lax.broadcasted_iota(jnp.int32, (rt, LANE), 0)) >> lg
        return (jnp.minimum(row + half, n_rows) - jnp.maximum(row - half, 0)).astype(F32)

    def col_base(t0, g):
        return pl.multiple_of((t0 // gw) * stride, SUB) + g * stride + cp

    def col_sum(t0, g, sign):
        acc = jnp.zeros((gw, LANE), F32)
        for o in offs:
            acc = acc + colbuf[pl.ds(col_base(t0, g) + sign * o, gw), :]
        return acc

    def row_sum(t0, sign):
        acc = jnp.zeros((rt, LANE), F32)
        for o in offs:
            acc = acc + rowbuf[pl.ds(rm + t0 + sign * o * gw, rt), :]
        return acc

    n_tiles = s_len // rt
    assert rt >= half * gw

    def loop(fn, edges=False):
        def step(i, c):
            t0 = pl.multiple_of(i * rt, rt)
            fn(t0, False) if edges else fn(t0)
            return c
        if edges:
            fn(0, True)
            if n_tiles > 1:
                fn(s_len - rt, True)
            lax.fori_loop(1, n_tiles - 1, step, 0)
        else:
            lax.fori_loop(0, n_tiles, step, 0)

    inv_ccnt = 1.0 / ccnt

    def by_row_count(v, t0, edge):
        return v / row_counts(t0) if edge else v * (1.0 / (2 * half))

    if not transpose:
        def fill(t0):
            for g in range(gpt):
                colbuf[pl.ds(col_base(t0, g), gw), :] = src_ref[pl.ds(t0 + g * gw, gw), :]

        def cols(t0):
            for g in range(gpt):
                rowbuf[pl.ds(rm + t0 + g * gw, gw), :] = col_sum(t0, g, 1) * inv_ccnt

        def rows(t0, edge):
            mean = by_row_count(row_sum(t0, 1), t0, edge)
            out_ref[pl.ds(t0, rt), :] = (mean - src_ref[pl.ds(t0, rt), :]).astype(out_ref.dtype)

        loop(fill)
        loop(cols)
        loop(rows, edges=True)
    else:
        def fill(t0, edge):
            rowbuf[pl.ds(rm + t0, rt), :] = by_row_count(src_ref[pl.ds(t0, rt), :], t0, edge)

        def rows(t0):
            acc = row_sum(t0, -1)
            for g in range(gpt):
                colbuf[pl.ds(col_base(t0, g), gw), :] = acc[g * gw:(g + 1) * gw, :] * inv_ccnt

        def cols(t0):
            for g in range(gpt):
                rows_g = pl.ds(t0 + g * gw, gw)
                out_ref[rows_g, :] = (col_sum(t0, g, -1) - src_ref[rows_g, :]).astype(out_ref.dtype)

        loop(fill, edges=True)
        loop(rows)
        loop(cols)


def _pool_map(src, nb, transpose, out_chunk_major, name):
    s_len = src.shape[1]
    cpg = nb // len(POOL_WINDOWS)

    def body(src_ref, out_ref, colbuf, rowbuf):
        n = pl.program_id(0)
        for gi, w in enumerate(POOL_WINDOWS):
            @pl.when(n // cpg == gi)
            def _(w=w):
                _pool_windows(src_ref, out_ref, colbuf, rowbuf, w // 2, transpose, s_len)

    if out_chunk_major:
        out_spec = pl.BlockSpec((None, s_len, LANE), lambda n: (n, 0, 0))
        out_shape = jax.ShapeDtypeStruct((nb, s_len, LANE), BF16)
    else:
        out_spec = pl.BlockSpec((s_len, LANE), lambda n: (0, n))
        out_shape = jax.ShapeDtypeStruct((s_len, nb * LANE), BF16)
    return _call(
        body, name=name, grid=(nb,),
        in_specs=[pl.BlockSpec((None, s_len, LANE), lambda n: (n, 0, 0))],
        out_specs=out_spec, out_shape=out_shape,
        scratch_shapes=[pltpu.VMEM((s_len // GRID_W * (GRID_W + 2 * POOL_CPAD), LANE), F32),
                        pltpu.VMEM((s_len + 16 * GRID_W, LANE), F32)],
    )(src)


def _group_weight(w_ref):
    return jnp.concatenate([w_ref[k] for k in range(N_CHIPS)], axis=0)


def _pool_mm_fwd(dm, wp, scale):
    nb, rows, _ = dm.shape
    _, ng, pq, pg = wp.shape
    cpg = pg // LANE
    tm = _row_tile(rows, 2048)

    def body(d_ref, w_ref, s_ref, y_ref):
        _put_chunks(y_ref, _dot(_cat(d_ref, cpg), _group_weight(w_ref)) * s_ref[...], cpg)

    cspec = pl.BlockSpec((cpg, tm, LANE), lambda i, g: (g, i, 0))
    return _call(
        body, name="pool_mm_fwd", grid=(rows // tm, ng),
        in_specs=[cspec, pl.BlockSpec((N_CHIPS, None, pq, pg), lambda i, g: (0, g, 0, 0)),
                  pl.BlockSpec((1, pg), lambda i, g: (0, g))],
        out_specs=cspec, out_shape=jax.ShapeDtypeStruct((nb, rows, LANE), BF16),
    )(dm, wp, scale)


def _pool_mm_bwd(dy, dm, wp, scale, gbuf, row0):
    nb, rows, _ = dm.shape
    _, ng, pq, pg = wp.shape
    cpg = pg // LANE
    tm = _row_tile(rows, 1024)
    nt = rows // tm
    assert gbuf.shape[2] == 2 * pg and row0 % pq == 0

    def body(dy_ref, d_ref, w_ref, s_ref, buf_ref, dd_ref, dwp_ref, dsc_ref, acc):
        i = pl.program_id(1)

        @pl.when(i == 0)
        def _():
            acc[...] = jnp.zeros_like(acc)
            dsc_ref[...] = jnp.zeros_like(dsc_ref)

        dyv = _cat(dy_ref, cpg)
        dc = _cat(d_ref, cpg)
        w = _group_weight(w_ref)
        dsc_ref[...] += jnp.sum(dyv * _dot(dc, w), axis=0, keepdims=True)
        dyp = (dyv * s_ref[...]).astype(BF16)
        _put_chunks(dd_ref, _dot_nt(dyp, w), cpg)
        acc[...] += _dot_tn(dc, dyp)

        @pl.when(i == nt - 1)
        def _():
            dwp_ref[...] = acc[...].reshape(N_CHIPS, pq, pg)

    cspec = pl.BlockSpec((cpg, tm, LANE), lambda g, i: (g, i, 0))
    sspec = pl.BlockSpec((1, pg), lambda g, i: (0, g))
    return _call(
        body, name="pool_mm_bwd", grid=(ng, nt),
        in_specs=[cspec, cspec, pl.BlockSpec((N_CHIPS, None, pq, pg), lambda g, i: (0, g, 0, 0)), sspec, ANY_SPEC],
        out_specs=[cspec, pl.BlockSpec((N_CHIPS, pq, pg), lambda g, i: (0, row0 // pq + g // 2, g % 2)), sspec],
        out_shape=[jax.ShapeDtypeStruct((nb, rows, LANE), F32), jax.ShapeDtypeStruct(gbuf.shape, F32),
                   jax.ShapeDtypeStruct((1, ng * pg), F32)],
        scratch_shapes=[pltpu.VMEM((pg, pg), F32)],
        input_output_aliases={4: 1},
    )(dy, dm, wp, scale, gbuf)


def _adamw_math(w, g, m, v):
    nm = ADAM_B1 * m + (1.0 - ADAM_B1) * g
    nv = ADAM_B2 * v + (1.0 - ADAM_B2) * jnp.square(g)
    m_hat = nm / (1.0 - ADAM_B1 ** ADAM_STEP)
    v_hat = nv / (1.0 - ADAM_B2 ** ADAM_STEP)
    return -ADAM_LR * (m_hat / (jnp.sqrt(v_hat) + ADAM_EPS) + ADAM_WD * w), nm, nv


def _adamw_param(w3, m3, v3, gsrcs, pick, tm, name, after=None):
    n_blk, rows, cols = w3.shape
    ng = len(gsrcs)

    def body(*refs):
        w_ref, m_ref, v_ref = refs[:3]
        g_refs = refs[3:3 + ng]
        go_ref, d_ref, nm_ref, nv_ref = refs[-4:]
        g = pick(pl.program_id(0), [r[...] for r in g_refs])
        go_ref[...] = g
        d_ref[...], nm_ref[...], nv_ref[...] = _adamw_math(w_ref[...], g, m_ref[...], v_ref[...])

    spec = pl.BlockSpec((None, tm, cols), lambda n, i: (n, i, 0))
    extra = [] if after is None else [after]
    return _call(
        body, name=name, grid=(n_blk, rows // tm),
        in_specs=[spec] * 3 + [pl.BlockSpec(shape, imap) for _, shape, imap in gsrcs] + [ANY_SPEC] * len(extra),
        out_specs=[spec] * 4, out_shape=[jax.ShapeDtypeStruct(w3.shape, F32)] * 4,
    )(w3, m3, v3, *[a for a, _, _ in gsrcs], *extra)


def _adamw_small(quads):
    n = len(quads)

    def body(*refs):
        ins, outs = refs[:4 * n], refs[4 * n:]
        for k in range(n):
            w, g, m, v = (r[...] for r in ins[4 * k:4 * k + 4])
            outs[3 * k][...], outs[3 * k + 1][...], outs[3 * k + 2][...] = _adamw_math(w, g, m, v)

    flat = [a for q in quads for a in q]
    res = _call(body, name="adamw_small", grid=(1,),
                in_specs=[pl.BlockSpec(a.shape, lambda i: (0, 0)) for a in flat],
                out_specs=[pl.BlockSpec(q[0].shape, lambda i: (0, 0)) for q in quads for _ in range(3)],
                out_shape=[jax.ShapeDtypeStruct(q[0].shape, F32) for q in quads for _ in range(3)])(*flat)
    return [tuple(res[3 * k:3 * k + 3]) for k in range(n)]


def _place():
    return lax.axis_index("x"), lax.axis_index("y"), lax.axis_index("c")


def _other_chips(x, y):
    return [(1 - x, y), (x, 1 - y), (1 - x, 1 - y)]


def _own_slab(a, devices=False):
    x, y, c = _place()
    n, me = (N_DEV, 4 * x + 2 * y + c) if devices else (N_CHIPS, 2 * x + y)
    return lax.dynamic_update_slice(lax.empty((n,) + a.shape, a.dtype), a[None], (me, 0, 0))


def _gather_chips(arrays, name):
    n = len(arrays)
    halves = [a.shape[0] // 2 for a in arrays]
    for a, h in zip(arrays, halves):
        assert 2 * h == a.shape[0] and h % (32 // a.dtype.itemsize) == 0
    lands = [_own_slab(a) for a in arrays]

    def body(*refs):
        outs = refs[n:2 * n]
        send_sems, recv_sems = refs[2 * n:]
        x, y, c = _place()
        me = 2 * x + y
        chips = _other_chips(x, y)

        def mine(k):
            return pl.ds(c * halves[k], halves[k])

        def theirs(k):
            return pl.ds((1 - c) * halves[k], halves[k])

        def push(k, j, src, dst, to):
            return pltpu.make_async_remote_copy(src_ref=src, dst_ref=dst, send_sem=send_sems.at[6 * k + j],
                                                recv_sem=recv_sems.at[6 * k + j], device_id=to, device_id_type=MESH)

        started = []
        for j, (cx, cy) in enumerate(chips):
            for k in range(n):
                own = outs[k].at[me, mine(k)]
                cp = push(k, j, own, own, (cx, cy, c))
                cp.start()
                started.append(cp)
        for j, (cx, cy) in enumerate(chips):
            for k in range(n):
                slab = outs[k].at[2 * cx + cy, mine(k)]
                push(k, j, slab, slab, (x, y, c)).wait_recv()
                fwd = push(k, 3 + j, slab, slab, (x, y, 1 - c))
                fwd.start()
                started.append(fwd)
        for j, (cx, cy) in enumerate(chips):
            for k in range(n):
                slab = outs[k].at[2 * cx + cy, theirs(k)]
                push(k, 3 + j, slab, slab, (x, y, c)).wait_recv()
        for cp in started:
            cp.wait_send()

    return _call(
        body, name=name, in_specs=[ANY_SPEC] * n, out_specs=[ANY_SPEC] * n,
        out_shape=[jax.ShapeDtypeStruct(a.shape, a.dtype) for a in lands],
        input_output_aliases={k: k for k in range(n)},
        scratch_shapes=[pltpu.SemaphoreType.DMA((6 * n,)), pltpu.SemaphoreType.DMA((6 * n,))],
    )(*lands)


HBM_SPEC = pl.BlockSpec(memory_space=pltpu.HBM)
SEM_SPEC = pl.BlockSpec(memory_space=pltpu.SEMAPHORE)
SIDE_EFFECT = pltpu.SideEffectType.DATAFLOW_SIDE_EFFECTING


def _n_peers(kind):
    return N_DEV - 1 if kind == "devices" else N_CHIPS - 1


def _push_copies(src_refs, land_refs, send_sems, recv_sems, kind):
    x, y, c = _place()
    if kind == "devices":
        me = 4 * x + 2 * y + c
        peers = [((me + j) % N_DEV, None) for j in range(1, N_DEV)]
        peers = [((to // 4, (to // 2) % 2, to % 2), None) for to, _ in peers]
    else:
        me = 2 * x + y
        peers = [((cx, cy, c), 2 * cx + cy) for cx, cy in _other_chips(x, y)]
    n = len(peers)
    copies = []
    for j, (dev, slab) in enumerate(peers):
        for k, (src, land) in enumerate(zip(src_refs, land_refs)):
            copies.append(pltpu.make_async_remote_copy(
                src_ref=src.at[slab] if kind == "slab" else src, dst_ref=land.at[me], send_sem=send_sems.at[n * k + j],
                recv_sem=recv_sems.at[n * k + j], device_id=dev, device_id_type=MESH))
    return copies


def _push_start(srcs, lands, kind, after, name):
    n = len(srcs)

    def body(*refs):
        src_refs, land_refs = refs[:n], refs[n:2 * n]
        send_sems, recv_sems = refs[2 * n + 1], refs[2 * n + 2]
        token = refs[-1]
        for cp in _push_copies(src_refs, land_refs, send_sems, recv_sems, kind):
            cp.start()
        token[...] = jnp.zeros_like(token)

    bufs = [pltpu.with_memory_space_constraint(a, pltpu.HBM) for a in list(srcs) + list(lands)]
    res = _call(
        body, name=name,
        out_shape=[pltpu.SemaphoreType.DMA((_n_peers(kind) * n,)), pltpu.SemaphoreType.DMA((_n_peers(kind) * n,))]
        + [pltpu.HBM(a.shape, a.dtype) for a in bufs] + [jax.ShapeDtypeStruct((SUB, LANE), F32)],
        in_specs=[HBM_SPEC] * (2 * n) + [ANY_SPEC],
        out_specs=[SEM_SPEC, SEM_SPEC] + [HBM_SPEC] * (2 * n) + [pl.BlockSpec(memory_space=pltpu.VMEM)],
        input_output_aliases={i: 2 + i for i in range(2 * n)},
        compiler_params=pltpu.CompilerParams(has_side_effects=SIDE_EFFECT),
    )(*bufs, after)
    return res[0], res[1], list(res[2:2 + n]), list(res[2 + n:2 + 2 * n]), res[-1]


def _push_wait(send_sems, recv_sems, srcs, lands, kind, after, name):
    n = len(srcs)

    def body(*refs):
        src_refs, land_refs = refs[:n], refs[n:2 * n]
        send_sems, recv_sems = refs[2 * n], refs[2 * n + 1]
        for cp in _push_copies(src_refs, land_refs, send_sems, recv_sems, kind):
            cp.wait_send()
            cp.wait_recv()

    res = _call(
        body, name=name,
        out_shape=[pltpu.HBM(a.shape, a.dtype) for a in list(srcs) + list(lands)],
        in_specs=[HBM_SPEC] * (2 * n) + [SEM_SPEC, SEM_SPEC, ANY_SPEC],
        out_specs=[HBM_SPEC] * (2 * n),
        input_output_aliases={i: i for i in range(2 * n)},
        compiler_params=pltpu.CompilerParams(has_side_effects=SIDE_EFFECT),
    )(*srcs, *lands, send_sems, recv_sems, after)
    return list(res[n:])


def _sibling_swap(g):
    _, rows, w = g.shape
    half = rows // 2

    def body(g_ref, out_ref, send_sem, recv_sem):
        x, y, c = _place()
        cp = pltpu.make_async_remote_copy(src_ref=g_ref.at[:, pl.ds((1 - c) * half, half)], dst_ref=out_ref,
                                          send_sem=send_sem, recv_sem=recv_sem, device_id=(x, y, 1 - c), device_id_type=MESH)
        cp.start()
        cp.wait()

    return _call(body, name="rs_sibling_swap", in_specs=[ANY_SPEC], out_specs=ANY_SPEC,
                 out_shape=jax.ShapeDtypeStruct((N_CHIPS, half, w), F32),
                 scratch_shapes=[pltpu.SemaphoreType.DMA, pltpu.SemaphoreType.DMA])(g)


def _pair_add(g, got, place):
    _, rows, w = g.shape
    half = rows // 2
    tm = _row_tile(half, RS_TILE)
    nt = half // tm

    def body(p_ref, a_ref, b_ref, o_ref, own_ref):
        v = a_ref[...] + b_ref[...]
        o_ref[...] = v.astype(BF16)

        @pl.when(pl.program_id(1) == p_ref[1])
        def _():
            own_ref[...] = v

    return _call(
        body, name="rs_pair_add",
        grid_spec=pltpu.PrefetchScalarGridSpec(
            num_scalar_prefetch=1, grid=(nt, N_CHIPS),
            in_specs=[pl.BlockSpec((None, tm, w), lambda i, s, p: (s, p[0] * nt + i, 0)),
                      pl.BlockSpec((None, tm, w), lambda i, s, p: (s, i, 0))],
            out_specs=[pl.BlockSpec((None, tm, w), lambda i, s, p: (s, i, 0)),
                       pl.BlockSpec((tm, w), lambda i, s, p: (i, 0))]),
        out_shape=[jax.ShapeDtypeStruct((N_CHIPS, half, w), BF16), jax.ShapeDtypeStruct((half, w), F32)],
    )(place, g, got)


def _sum_chips(parts, own, place):
    _, half, w = parts.shape
    tm = _row_tile(half, RS_TILE)
    nt = half // tm

    def body(p_ref, parts_ref, own_ref, o_ref):
        me = p_ref[1]
        t = [jnp.where(me == q, own_ref[...], parts_ref[q].astype(F32)) for q in range(N_CHIPS)]
        o_ref[...] = (t[0] + t[1]) + (t[2] + t[3])

    return _call(
        body, name="rs_sum_chips",
        grid_spec=pltpu.PrefetchScalarGridSpec(
            num_scalar_prefetch=1, grid=(nt,),
            in_specs=[pl.BlockSpec((N_CHIPS, tm, w), lambda i, p: (0, i, 0)), pl.BlockSpec((tm, w), lambda i, p: (i, 0))],
            out_specs=pl.BlockSpec((tm, w), lambda i, p: (p[0] * nt + i, 0))),
        out_shape=jax.ShapeDtypeStruct((2 * half, w), F32),
    )(place, parts, own)


def _sibling_gather(red):
    rows, w = red.shape
    half = rows // 2

    def body(in_ref, out_ref, send_sem, recv_sem):
        x, y, c = _place()
        mine = out_ref.at[pl.ds(c * half, half)]
        cp = pltpu.make_async_remote_copy(src_ref=mine, dst_ref=mine, send_sem=send_sem, recv_sem=recv_sem,
                                          device_id=(x, y, 1 - c), device_id_type=MESH)
        cp.start()
        other = out_ref.at[pl.ds((1 - c) * half, half)]
        pltpu.make_async_remote_copy(src_ref=other, dst_ref=other, send_sem=send_sem, recv_sem=recv_sem,
                                     device_id=(x, y, c), device_id_type=MESH).wait_recv()
        cp.wait_send()

    return _call(body, name="rs_sibling_gather", in_specs=[ANY_SPEC], out_specs=ANY_SPEC,
                 out_shape=jax.ShapeDtypeStruct(red.shape, F32), input_output_aliases={0: 0},
                 scratch_shapes=[pltpu.SemaphoreType.DMA, pltpu.SemaphoreType.DMA])(red)


def _rs_begin(g, place, name):
    pair, own = _pair_add(g, _sibling_swap(g), place)
    send, recv, pair, parts, token = _push_start([pair], [jnp.zeros_like(pair)], "slab", own, name + "_start")
    return (send, recv, pair, parts, own), token


def _rs_end(state, place, after, name):
    send, recv, pair, parts, own = state
    (parts,) = _push_wait(send, recv, pair, parts, "slab", after, name + "_wait")
    return _sibling_gather(_sum_chips(parts, own, place))


WEIGHTS = ("c_ctx", "w_mod", "b_mod", "w_in", "w_out", "ln_g", "ln_b", "conv_w", "conv_b", "lru_wa", "lru_ba", "lru_wx",
           "lru_bx", "lru_lam", "pool_w", "pool_scale")
SMALL_GATHERED = ("conv_w", "lru_ba", "lru_bx", "lru_lam", "pool_scale")
SMALL_UPDATED = ("c_ctx", "b_mod", "ln_g", "ln_b", "conv_w", "conv_b", "lru_ba", "lru_bx", "lru_lam", "pool_scale")


def kernel(x, c, ctx, c_ctx, w_mod, b_mod, w_in, w_out, ln_g, ln_b, conv_w, conv_b, lru_wa, lru_ba, lru_wx, lru_bx, lru_lam, pool_w, pool_scale, loss_target, m_c_ctx, m_w_mod, m_b_mod, m_w_in, m_w_out, m_ln_g, m_ln_b, m_conv_w, m_conv_b, m_lru_wa, m_lru_ba, m_lru_wx, m_lru_bx, m_lru_lam, m_pool_w, m_pool_scale, v_c_ctx, v_w_mod, v_b_mod, v_w_in, v_w_out, v_ln_g, v_ln_b, v_conv_w, v_conv_b, v_lru_wa, v_lru_ba, v_lru_wx, v_lru_bx, v_lru_lam, v_pool_w, v_pool_scale):
    weights = dict(c_ctx=c_ctx, w_mod=w_mod, b_mod=b_mod, w_in=w_in, w_out=w_out, ln_g=ln_g, ln_b=ln_b, conv_w=conv_w,
                   conv_b=conv_b, lru_wa=lru_wa, lru_ba=lru_ba, lru_wx=lru_wx, lru_bx=lru_bx, lru_lam=lru_lam,
                   pool_w=pool_w, pool_scale=pool_scale)
    mom1 = dict(c_ctx=m_c_ctx, w_mod=m_w_mod, b_mod=m_b_mod, w_in=m_w_in, w_out=m_w_out, ln_g=m_ln_g, ln_b=m_ln_b,
                conv_w=m_conv_w, conv_b=m_conv_b, lru_wa=m_lru_wa, lru_ba=m_lru_ba, lru_wx=m_lru_wx, lru_bx=m_lru_bx,
                lru_lam=m_lru_lam, pool_w=m_pool_w, pool_scale=m_pool_scale)
    mom2 = dict(c_ctx=v_c_ctx, w_mod=v_w_mod, b_mod=v_b_mod, w_in=v_w_in, w_out=v_w_out, ln_g=v_ln_g, ln_b=v_ln_b,
                conv_w=v_conv_w, conv_b=v_conv_b, lru_wa=v_lru_wa, lru_ba=v_lru_ba, lru_wx=v_lru_wx, lru_bx=v_lru_bx,
                lru_lam=v_lru_lam, pool_w=v_pool_w, pool_scale=v_pool_scale)
    xs, cx, target = x[0], ctx[0], loss_target[0]
    s_len, d = xs.shape
    es = w_out.shape[1]
    e = es * N_CHIPS
    nb = e // LANE
    c3 = w_mod.shape[2]
    n4 = w_in.shape[2]
    pq, pg = pool_w.shape[2], pool_w.shape[3]
    ng = len(POOL_WINDOWS)
    width = n4
    assert width == d and 2 * pg == width and 2 * nb * LANE == N_CHIPS * width and d % (2 * N_CHIPS) == 0
    px, py, pc = _place()
    place = jnp.stack([pc, 2 * px + py]).astype(jnp.int32)
    cctx2 = c_ctx[None, :]

    eq = e // N_CHIPS
    small_rows = [(conv_w[0], 0), (lru_ba[0], CONV_TAPS), (lru_bx[0], CONV_TAPS + 2), (lru_lam[0], CONV_TAPS + 4),
                  (pool_scale, CONV_TAPS + 6)]
    small = _rows_kernel([(a, r, 0) for a, r in small_rows], 2 * SUB, eq, "pack_small_weights")
    win0, sg = _gather_chips([w_in[0].astype(BF16), small], "gather_weights0")
    full = {n: jnp.swapaxes(sg[:, r:r + a.shape[0]], 0, 1).reshape(a.shape[0], e)
            for n, (a, r) in zip(SMALL_GATHERED, small_rows)}
    wa_h, wx_h = (0.5 * lru_wa[0]).astype(BF16), (0.5 * lru_wx[0]).astype(BF16)
    lru_args = (full["conv_w"], conv_b, wa_h, wx_h, 0.5 * full["lru_ba"], 0.5 * full["lru_bx"], full["lru_lam"])
    scale_f = full["pool_scale"]

    me8 = 4 * px + 2 * py + pc
    c_rows = _rows_kernel([(c, 0, 0)], SUB, d, "pack_c")
    c_send, c_recv, c_src, c_lands, c_token = _push_start([c_rows], [_own_slab(c_rows, devices=True)], "devices", sg,
                                                          "gather_c_start")
    (c_dev,) = _push_wait(c_send, c_recv, c_src, c_lands, "devices", c_token, "gather_c_wait")
    wm_mine = w_mod.astype(BF16)
    bm_mine = lax.dynamic_slice_in_dim(b_mod[:, None, :], place[1] * c3, c3, axis=2)
    (mod_g,) = _gather_chips([_mod_fwd(c_dev[:, 0, :], cctx2, wm_mine, bm_mine).reshape(DEPTH * 2 * SUB, c3)], "gather_mod")
    mod_all = jnp.transpose(mod_g.reshape(N_CHIPS, DEPTH, 2 * SUB, c3), (1, 2, 0, 3)).reshape(DEPTH, 2 * SUB, 3 * d)
    mod_mine = lax.dynamic_index_in_dim(mod_all, me8, axis=1, keepdims=False)
    later = [w_out[0].astype(BF16), w_in[1].astype(BF16), w_out[1].astype(BF16), pool_w.astype(BF16).reshape(ng * pq, pg)]
    w_send, w_recv, later, later_lands, w_token = _push_start(
        later, [_own_slab(a) for a in later], "same", mod_g, "gather_weights1_start")
    mod_mine = mod_mine + w_token[0:1, 0:1]

    def mod_parts(v):
        return v[None, :d], 1.0 + v[None, d:2 * d], v[None, 2 * d:]

    sh0, sc0, gt0 = mod_parts(mod_mine[0])
    shc, scc, _ = mod_parts(mod_all[0, N_DEV])
    sh1, sc1, gt1 = mod_parts(mod_mine[1])
    lg = [ln_g[l][None, :] for l in range(DEPTH)]
    lb = [ln_b[l][None, :] for l in range(DEPTH)]

    uu0, ug0 = _inproj_fwd(xs, sc0, sh0, win0, "inproj_fwd0")
    uc0 = _inproj_fwd(cx, scc, shc, win0[:2], "inproj_fwd_ctx")
    y0 = _rglru_fwd(uu0, uc0, *lru_args)
    wout0_g, win1, wout1_g, wp_g = _push_wait(w_send, w_recv, later, later_lands, "same", y0, "gather_weights1_wait")
    win = [win0, win1]
    wout = [wout0_g.reshape(e, d), wout1_g.reshape(e, d)]
    wp = wp_g.reshape(N_CHIPS, ng, pq, pg)
    br0, x1 = _outproj_fwd(y0, ug0, xs, gt0, wout[0], lg[0], lb[0], None, "outproj_fwd0")
    uu1, ug1 = _inproj_fwd(x1, sc1, sh1, win[1], "inproj_fwd1")
    d1 = _pool_map(uu1, nb, False, True, "pool_fwd")
    y1 = _pool_mm_fwd(d1, wp, scale_f)
    br1, dxo, loss_part = _outproj_fwd(y1, ug1, x1, gt1, wout[1], lg[1], lb[1], target, "outproj_fwd1")

    row_wout = d
    row_tail = d + es
    wq = 2 * (nb // N_CHIPS) * LANE * LANE // width
    whole = lambda r: (r + 2 * RS_TILE - 1) // (2 * RS_TILE) * (2 * RS_TILE)
    rows1 = whole(row_tail + pg // 2)
    rows0 = whole(row_tail + 2 * wq)
    fresh = lambda rows, used: (lax.empty if rows == used else jnp.zeros)((N_CHIPS, rows, width), F32)
    gbuf1 = fresh(rows1, row_tail + pg // 2)
    gbuf0 = fresh(rows0, row_tail + 2 * wq)

    dy1, dg1, dxres1, dbr1, dlg1, dlb1, dgt1 = _outproj_bwd(dxo, x1, br1, y1, ug1, gt1, lg[1], wout[1], "outproj_bwd1")
    gbuf1 = _outproj_bwd_w(y1, ug1, dbr1, gbuf1, row_wout, "outproj_bwd_w1")
    dd1, gbuf1, dscale = _pool_mm_bwd(dy1, d1, wp, scale_f, gbuf1, row_tail)
    du1 = _pool_map(dd1, nb, True, False, "pool_bwd")
    dx1, dsc1, dsh1 = _inproj_bwd_x([du1, dg1], x1, dxres1, sc1, win[1], "inproj_bwd_x1")
    gbuf1 = _inproj_bwd_w(x1, sc1, sh1, [du1, dg1], None, gbuf1, "inproj_bwd_w1")
    rs1, token1 = _rs_begin(gbuf1, place, "rs_exchange1")

    dy0, dg0, dxres0, dbr0, dlg0, dlb0, dgt0 = _outproj_bwd(dx1, xs, br0, y0, ug0, gt0 + token1[0:1, 0:1], lg[0], wout[0],
                                                            "outproj_bwd0")
    gbuf0 = _outproj_bwd_w(y0, ug0, dbr0, gbuf0, row_wout, "outproj_bwd_w0")
    du0, duc, dconv_w, dconv_b, dwa, dwx, dba, dbx, dlam = _rglru_bwd(uu0, uc0, dy0, *lru_args)
    dwin0c = _inproj_bwd_w(cx, scc, shc, [duc, jnp.zeros_like(duc)], None, None, "inproj_bwd_w_ctx")
    gbuf0 = _inproj_bwd_w(xs, sc0, sh0, [du0, dg0], dwin0c, gbuf0, "inproj_bwd_w0")

    def quarter(dw):
        t = dw.reshape(2, N_CHIPS, nb // N_CHIPS, LANE, LANE)
        return jnp.transpose(t, (1, 3, 0, 2, 4)).reshape(N_CHIPS, LANE, 2 * (nb // N_CHIPS) * LANE).reshape(N_CHIPS, wq, width)

    tail0 = jnp.concatenate([quarter(dwa), quarter(dwx)], axis=1)
    gbuf0 = lax.dynamic_update_slice(gbuf0, tail0, (0, row_tail, 0))
    red1 = _rs_end(rs1, place, gbuf0, "rs_exchange1")
    rs0, token0 = _rs_begin(gbuf0, place, "rs_exchange0")
    grad_x, dsc0, dsh0 = _inproj_bwd_x([du0, dg0], xs, dxres0, sc0 + token0[0:1, 0:1], win[0], "inproj_bwd_x0")
    dscc, dshc = _inproj_bwd_x([duc], cx, None, scc, win[0][:2], "inproj_bwd_x_ctx")

    k0 = VEC_KINDS
    vec = _rows_kernel(
        [(c, 0, 0), (loss_part, 0, d), (dsh0, 1, 0), (dsc0, 1, d), (dgt0, 1, 2 * d), (dshc, 2, 0), (dscc, 2, d),
         (dsh1, 3, 0), (dsc1, 3, d), (dgt1, 3, 2 * d),
         (dconv_b, k0, 0), (dlg0, k0, e), (dscale, k0 + 1, 0), (dlg1, k0 + 1, e), (dlb0, k0 + 2, 0), (dlb1, k0 + 2, d),
         (dconv_w, k0 + 3, 0), (dba, k0 + 7, 0), (dbx, k0 + 9, 0), (dlam, k0 + 11, 0)], VEC_ROWS, 3 * d, "pack_vec")
    v_send, v_recv, vec_l, vec_lands, v_token = _push_start([vec], [_own_slab(vec, devices=True)], "devices", vec,
                                                            "gather_devices_start")
    red0 = _rs_end(rs0, place, v_token, "rs_exchange0")
    quarters = red0[row_tail:row_tail + 2 * wq]
    q_send, q_recv, q_src, q_lands, q_token = _push_start([quarters], [_own_slab(quarters)], "same", red0,
                                                          "gather_replicated_start")

    tmw = _row_tile(d, 256)
    red_src = lambda red, r0, tm: (red, (tm, width), lambda n, i: (r0 // tm + i, 0))
    by_layer = lambda n, gs: jnp.where(n == 0, gs[0], gs[1])
    outs = {}
    outs["w_in"] = _adamw_param(w_in, m_w_in, v_w_in, [red_src(red0, 0, tmw), red_src(red1, 0, tmw)], by_layer, tmw, "adamw_w_in",
                                after=q_token)
    outs["w_out"] = _adamw_param(w_out, m_w_out, v_w_out, [red_src(red0, row_wout, tmw), red_src(red1, row_wout, tmw)],
                                 by_layer, tmw, "adamw_w_out")
    pw = [a.reshape(ng, pq, pg) for a in (pool_w, m_pool_w, v_pool_w)]
    outs["pool_w"] = [o.reshape(pool_w.shape) for o in _adamw_param(
        *pw, [(red1, (pq, pg), lambda n, i: (row_tail // pq + n // 2, n % 2))], lambda n, gs: gs[0], pq, "adamw_pool_w")]

    (gathered,) = _push_wait(v_send, v_recv, vec_l, vec_lands, "devices", outs["w_out"][1], "gather_devices_wait")
    gt_all = jnp.swapaxes(gathered, 0, 1)
    g_wmod = _mod_bwd_shard(gt_all, cctx2, place, c3)
    g_bmod, sq_err, g_small = _mod_bwd_rep(gt_all, d)
    loss = sq_err[0, 0] * (0.5 / d)
    cpart = _cctx_partial(gt_all, wm_mine[0], place)
    x_send, x_recv, x_src, x_lands, x_token = _push_start([cpart], [_own_slab(cpart)], "same", cpart, "gather_cctx_start")
    outs["w_mod"] = _adamw_param(w_mod, m_w_mod, v_w_mod, [(g_wmod, (None, tmw, c3), lambda n, i: (n, i, 0))],
                                 lambda n, gs: gs[0], tmw, "adamw_w_mod", after=x_token)
    (rep,) = _push_wait(q_send, q_recv, q_src, q_lands, "same", outs["w_mod"][1], "gather_replicated_wait")
    bq = nb // N_CHIPS
    rep_src = lambda r0: (rep, (None, LANE, bq * LANE), lambda n, i: (n % N_CHIPS, r0 // LANE, n // N_CHIPS))
    stack = lambda n, gs: jnp.concatenate([gs[0][:, k * LANE:(k + 1) * LANE] for k in range(bq)], axis=0)
    for name, r0, trio in (("lru_wa", 0, (lru_wa, m_lru_wa, v_lru_wa)), ("lru_wx", wq, (lru_wx, m_lru_wx, v_lru_wx))):
        blocks = [a.reshape(2 * N_CHIPS, bq * LANE, LANE) for a in trio]
        outs[name] = [o.reshape(lru_wa.shape) for o in _adamw_param(*blocks, [rep_src(r0)], stack, bq * LANE, "adamw_" + name)]

    (cparts,) = _push_wait(x_send, x_recv, x_src, x_lands, "same", outs["lru_wx"][1], "gather_cctx_wait")
    g_small = dict(g_small, c_ctx=_cctx_finish(cparts, cctx2), b_mod=g_bmod)
    for n in SMALL_GATHERED:
        g_small[n] = lax.dynamic_slice_in_dim(g_small[n], place[1] * eq, eq, axis=1)
    as2d = lambda a: a.reshape(-1, a.shape[-1])
    quads = [(as2d(weights[n]), g_small[n], as2d(mom1[n]), as2d(mom2[n])) for n in SMALL_UPDATED]
    for n, (q, res) in zip(SMALL_UPDATED, zip(quads, _adamw_small(quads))):
        outs[n] = [a.reshape(weights[n].shape) for a in (q[1],) + res]

    result = [loss, grad_x[None]]
    for j in range(4):
        result += [outs[n][j] for n in WEIGHTS]
    return tuple(result)
```

```python
import jax
import jax.numpy as jnp
from jax import lax
from jax.experimental import pallas as pl
from jax.experimental.pallas import tpu as pltpu

F32 = jnp.float32
BF16 = jnp.bfloat16
LANE = 128
SUB = 8
GRID_W = 64
POOL_WINDOWS = (2, 4, 8, 16)
LRU_C = 8.0
DEPTH = 2
ALPHA = float((2 * DEPTH) ** 0.25)
LN_EPS = 1e-5
ADAM_LR, ADAM_B1, ADAM_B2, ADAM_EPS, ADAM_WD, ADAM_STEP = 0.001, 0.9, 0.999, 1e-08, 0.01, 10
N_CHIPS = 4
N_DEV = 8
MESH = pl.DeviceIdType.MESH
ROW_TILE = 512
GATE_TILE = 2048
GATE_BWD_TILE = 1024
GATE_UNROLL = 1
CONV_TAPS = 4
CONV_LEFT = 2
PAD = 8
SCAN_UNROLL = 32
RS_TILE = 448
LN_ROWS = 128
POOL_CPAD = 16
VEC_KINDS = 4


def _call(body, **kw):
    return pl.pallas_call(body, **kw)


def _dot(a, b):
    return jnp.dot(a, b, preferred_element_type=F32)


def _dot_nt(a, b):
    return lax.dot_general(a, b, (((1,), (1,)), ((), ())), preferred_element_type=F32)


def _dot_tn(a, b):
    return lax.dot_general(a, b, (((0,), (0,)), ((), ())), preferred_element_type=F32)


def _sigmoid(v):
    return 0.5 * (jnp.tanh(0.5 * v) + 1.0)


def _silu(v):
    return v * _sigmoid(v)


def _dsilu(v):
    s = _sigmoid(v)
    return s * (1.0 + v * (1.0 - s))


def _log_sigmoid(v):
    z = jnp.exp(-jnp.abs(v))
    return jnp.minimum(v, 0.0) - jnp.where(z < 1e-4, z * (1.0 - 0.5 * z), jnp.log(1.0 + z))


def _one_minus_sq(la, a):
    return jnp.tanh(la) * (-1.0 - a * a)


def _cat(ref, n):
    return jnp.concatenate([ref[k] for k in range(n)], axis=1)


def _put_chunks(ref, val, n, base=0):
    for k in range(n):
        ref[base + k] = val[:, k * LANE:(k + 1) * LANE].astype(ref.dtype)


def _row_tile(rows, want):
    t = min(rows, want)
    assert rows % t == 0
    return t


ANY_SPEC = pl.BlockSpec(memory_space=pl.ANY)


def _mod_fwd(c_all, cctx, wm, bm):
    nl, d, c3 = wm.shape

    def body(c_ref, cx_ref, w_ref, b_ref, o_ref):
        cc = jnp.concatenate([c_ref[...], cx_ref[...], jnp.zeros((SUB - 1, d), F32)], axis=0)
        o_ref[...] = _dot(_silu(cc).astype(BF16), w_ref[...]) + b_ref[...]

    return _call(
        body, name="mod_fwd", grid=(nl,),
        in_specs=[pl.BlockSpec((N_DEV, d), lambda l: (0, 0)),
                  pl.BlockSpec((1, d), lambda l: (0, 0)),
                  pl.BlockSpec((None, d, c3), lambda l: (l, 0, 0)),
                  pl.BlockSpec((None, 1, c3), lambda l: (l, 0, 0))],
        out_specs=pl.BlockSpec((None, 2 * SUB, c3), lambda l: (l, 0, 0)),
        out_shape=jax.ShapeDtypeStruct((nl, 2 * SUB, c3), F32),
    )(c_all, cctx, wm, bm)


def _rows_kernel(parts, rows, cols, name):
    def body(*refs):
        o_ref = refs[-1]
        o_ref[...] = jnp.zeros_like(o_ref)
        for ref, (a, r0, c0) in zip(refs[:-1], parts):
            for k in range(a.shape[0]):
                o_ref[r0 + k:r0 + k + 1, c0:c0 + a.shape[1]] = ref[k:k + 1, :]

    return _call(body, name=name, grid=(1,),
                 in_specs=[pl.BlockSpec(a.shape, lambda i: (0, 0)) for a, _, _ in parts],
                 out_specs=pl.BlockSpec((rows, cols), lambda i: (0, 0)),
                 out_shape=jax.ShapeDtypeStruct((rows, cols), F32))(*[a for a, _, _ in parts])


def _mod_bwd_shard(gt, cctx, place, c3):
    d = cctx.shape[1]

    def body(p_ref, cs_ref, dm_ref, dmx_ref, cx_ref, o_ref):
        l = pl.program_id(0)
        lhs = jnp.concatenate([_silu(cs_ref[...]), _silu(cx_ref[...]), jnp.zeros((7, d), F32)], axis=0).astype(BF16)
        dmx = jnp.where(l == 0, jnp.sum(dmx_ref[...], axis=0, keepdims=True), 0.0)
        rhs = jnp.concatenate([dm_ref[...], dmx, jnp.zeros((7, c3), F32)], axis=0).astype(BF16)
        o_ref[...] = _dot_tn(lhs, rhs)

    return _call(
        body, name="mod_bwd_shard",
        grid_spec=pltpu.PrefetchScalarGridSpec(
            num_scalar_prefetch=1, grid=(DEPTH,),
            in_specs=[pl.BlockSpec((None, N_DEV, d), lambda l, p: (0, 0, 0)),
                      pl.BlockSpec((None, N_DEV, c3), lambda l, p: (1 + 2 * l, 0, p[1])),
                      pl.BlockSpec((None, N_DEV, c3), lambda l, p: (2, 0, p[1])),
                      pl.BlockSpec((1, d), lambda l, p: (0, 0))],
            out_specs=pl.BlockSpec((None, d, c3), lambda l, p: (l, 0, 0))),
        out_shape=jax.ShapeDtypeStruct((DEPTH, d, c3), F32),
    )(place, gt, gt, gt, cctx)


def _small_layout(d, e):
    k = VEC_KINDS
    return {
        "conv_b": ((1, e), [(0, k, 0)]),
        "ln_g": ((2, d), [(0, k, e), (1, k + 1, e)]),
        "pool_scale": ((1, e), [(0, k + 1, 0)]),
        "ln_b": ((2, d), [(0, k + 2, 0), (1, k + 2, d)]),
        "conv_w": ((CONV_TAPS, e), [(t, k + 3 + t, 0) for t in range(CONV_TAPS)]),
        "lru_ba": ((2, e), [(j, k + 7 + j, 0) for j in range(2)]),
        "lru_bx": ((2, e), [(j, k + 9 + j, 0) for j in range(2)]),
        "lru_lam": ((2, e), [(j, k + 11 + j, 0) for j in range(2)]),
    }


VEC_ROWS = 24


def _mod_bwd_rep(gt, d):
    d3 = gt.shape[2]
    layout = _small_layout(d, d3 - d)
    names = list(layout)

    def body(g_ref, db_ref, loss_ref, *small_refs):
        loss_ref[...] = jnp.zeros_like(loss_ref) + jnp.sum(g_ref[0][:, d:d + LANE])
        dm0 = jnp.sum(g_ref[1], axis=0, keepdims=True)
        dmx = jnp.sum(g_ref[2], axis=0, keepdims=True)
        dm1 = jnp.sum(g_ref[3], axis=0, keepdims=True)
        db_ref[0:1, :] = dm0 + dmx
        db_ref[1:2, :] = dm1
        for ref, name in zip(small_refs, names):
            shape, places = layout[name]
            for arr_row, vec_row, col0 in places:
                total = jnp.sum(g_ref[vec_row], axis=0, keepdims=True)
                ref[arr_row:arr_row + 1, :] = total[:, col0:col0 + shape[1]]

    outs = _call(
        body, name="mod_bwd_rep", grid=(1,),
        in_specs=[pl.BlockSpec(gt.shape, lambda i: (0, 0, 0))],
        out_specs=[pl.BlockSpec((DEPTH, d3), lambda i: (0, 0)), pl.BlockSpec((1, LANE), lambda i: (0, 0))]
        + [pl.BlockSpec(layout[n][0], lambda i: (0, 0)) for n in names],
        out_shape=[jax.ShapeDtypeStruct((DEPTH, d3), F32), jax.ShapeDtypeStruct((1, LANE), F32)]
        + [jax.ShapeDtypeStruct(layout[n][0], F32) for n in names],
    )(gt)
    return outs[0], outs[1], dict(zip(names, outs[2:]))


def _cctx_partial(gt, wm0, place):
    d, c3 = wm0.shape

    def body(p_ref, dmx_ref, w_ref, o_ref):
        dmx = jnp.sum(dmx_ref[...], axis=0, keepdims=True)
        o_ref[...] = _dot_nt(jnp.broadcast_to(dmx, (2 * SUB, c3)).astype(BF16), w_ref[...])

    return _call(
        body, name="cctx_partial",
        grid_spec=pltpu.PrefetchScalarGridSpec(
            num_scalar_prefetch=1, grid=(1,),
            in_specs=[pl.BlockSpec((None, N_DEV, c3), lambda i, p: (2, 0, p[1])), pl.BlockSpec((d, c3), lambda i, p: (0, 0))],
            out_specs=pl.BlockSpec((2 * SUB, d), lambda i, p: (0, 0))),
        out_shape=jax.ShapeDtypeStruct((2 * SUB, d), F32),
    )(place, gt, wm0)


def _cctx_finish(parts, cctx):
    d = cctx.shape[1]

    def body(p_ref, cx_ref, o_ref):
        total = (p_ref[0, 0:1, :] + p_ref[1, 0:1, :]) + (p_ref[2, 0:1, :] + p_ref[3, 0:1, :])
        o_ref[...] = total * _dsilu(cx_ref[...])

    return _call(body, name="cctx_finish", grid=(1,),
                 in_specs=[pl.BlockSpec(parts.shape, lambda i: (0, 0, 0)), pl.BlockSpec((1, d), lambda i: (0, 0))],
                 out_specs=pl.BlockSpec((1, d), lambda i: (0, 0)),
                 out_shape=jax.ShapeDtypeStruct((1, d), F32))(parts, cctx)


def _inproj_fwd(xin, sc1, sh, w, name):
    rows, d = xin.shape
    ns, _, n4 = w.shape
    cpb = n4 // LANE
    tm = _row_tile(rows, 512)
    assert ns in (2, 4)

    def body(x_ref, sc_ref, sh_ref, w_ref, *o_refs):
        h = (x_ref[...] * sc_ref[...] + sh_ref[...]).astype(BF16)
        for s in range(ns):
            _put_chunks(o_refs[s // 2], _dot(h, w_ref[s]), cpb, base=(s % 2) * cpb)

    spec = pl.BlockSpec((2 * cpb, tm, LANE), lambda i: (0, i, 0))
    dtypes = (F32, BF16)[:ns // 2]
    res = _call(
        body, name=name, grid=(rows // tm,),
        in_specs=[pl.BlockSpec((tm, d), lambda i: (i, 0)),
                  pl.BlockSpec((1, d), lambda i: (0, 0)),
                  pl.BlockSpec((1, d), lambda i: (0, 0)),
                  pl.BlockSpec((ns, d, n4), lambda i: (0, 0, 0))],
        out_specs=[spec] * len(dtypes),
        out_shape=[jax.ShapeDtypeStruct((2 * cpb, rows, LANE), t) for t in dtypes],
    )(xin, sc1, sh, w)
    return res[0] if ns == 2 else tuple(res)


def _inproj_bwd_x(dparts, xin, dxres, sc1, w, name):
    rows, d = xin.shape
    npart = len(dparts)
    e = dparts[0].shape[1]
    ns, _, n4 = w.shape
    per = e // n4
    assert per * npart == ns
    tm = _row_tile(rows, 512)
    has_res = dxres is not None

    def body(*refs):
        dp = refs[:npart]
        x_ref, sc_ref, w_ref = refs[npart:npart + 3]
        rest = refs[npart + 3:]
        if has_res:
            res_ref, dx_ref, dsc_ref, dsh_ref = rest
        else:
            dsc_ref, dsh_ref = rest
        i = pl.program_id(0)
        dh = jnp.zeros((tm, d), F32)
        for p in range(npart):
            v = dp[p][...]
            for q in range(per):
                dh = dh + _dot_nt(v[:, q * n4:(q + 1) * n4], w_ref[p * per + q])

        @pl.when(i == 0)
        def _():
            dsc_ref[...] = jnp.zeros_like(dsc_ref)
            dsh_ref[...] = jnp.zeros_like(dsh_ref)

        dsc_ref[...] += jnp.sum(dh * x_ref[...], axis=0, keepdims=True)
        dsh_ref[...] += jnp.sum(dh, axis=0, keepdims=True)
        if has_res:
            dx_ref[...] = res_ref[...] + dh * sc_ref[...]

    row_spec = pl.BlockSpec((tm, d), lambda i: (i, 0))
    vec_spec = pl.BlockSpec((1, d), lambda i: (0, 0))
    in_specs = [pl.BlockSpec((tm, e), lambda i: (i, 0))] * npart + [row_spec, vec_spec,
                                                                     pl.BlockSpec((ns, d, n4), lambda i: (0, 0, 0))]
    args = list(dparts) + [xin, sc1, w]
    out_specs, out_shape = [vec_spec, vec_spec], [jax.ShapeDtypeStruct((1, d), F32)] * 2
    if has_res:
        in_specs.append(row_spec)
        args.append(dxres)
        out_specs = [row_spec] + out_specs
        out_shape = [jax.ShapeDtypeStruct((rows, d), F32)] + out_shape
    return _call(body, name=name, grid=(rows // tm,), in_specs=in_specs, out_specs=out_specs, out_shape=out_shape)(*args)


def _inproj_bwd_w(xin, sc1, sh, dparts, init, gbuf, name):
    rows, d = xin.shape
    npart = len(dparts)
    e = dparts[0].shape[1]
    n4 = e // 2
    ns = 2 * npart
    tm = _row_tile(rows, 1024)
    nt = rows // tm
    has_init = init is not None
    into = gbuf is not None
    assert not into or (ns == N_CHIPS and gbuf.shape[2] == n4)

    def body(*refs):
        x_ref, sc_ref, sh_ref = refs[:3]
        dp = refs[3:3 + npart]
        init_ref = refs[3 + npart] if has_init else None
        o_ref = refs[-1]
        s, i = pl.program_id(0), pl.program_id(1)
        h = (x_ref[...] * sc_ref[...] + sh_ref[...]).astype(BF16)

        @pl.when(i == 0)
        def _():
            o_ref[...] = init_ref[...] if has_init else jnp.zeros_like(o_ref)

        for p in range(npart):
            @pl.when(s // 2 == p)
            def _(p=p):
                o_ref[...] += _dot_tn(h, dp[p][...])

    in_specs = [pl.BlockSpec((tm, d), lambda s, i: (i, 0)),
                pl.BlockSpec((1, d), lambda s, i: (0, 0)),
                pl.BlockSpec((1, d), lambda s, i: (0, 0))]
    in_specs += [pl.BlockSpec((tm, n4), lambda s, i: (i, s % 2))] * npart
    args = [xin, sc1, sh] + list(dparts)
    o_spec = pl.BlockSpec((None, d, n4), lambda s, i: (s, 0, 0))
    if has_init:
        in_specs.append(o_spec)
        args.append(init)
    extra = {}
    if into:
        in_specs.append(ANY_SPEC)
        args.append(gbuf)
        extra = dict(input_output_aliases={len(args) - 1: 0})
    out_shape = jax.ShapeDtypeStruct(gbuf.shape if into else (ns, d, n4), F32)
    return _call(body, name=name, grid=(ns, nt), in_specs=in_specs, out_specs=o_spec, out_shape=out_shape, **extra)(*args)


def _gated(y_ref, g_ref, nch):
    return jnp.concatenate([(y_ref[k].astype(F32) * _silu(g_ref[k].astype(F32))).astype(BF16) for k in range(nch)], axis=1)


def _ln_stats(r):
    mu = jnp.mean(r, axis=-1, keepdims=True)
    var = jnp.mean(jnp.square(r - mu), axis=-1, keepdims=True)
    rstd = lax.rsqrt(var + LN_EPS)
    return (r - mu) * rstd, rstd


def _outproj_fwd(y, ug, xin, gt, wout, lg, lb, target, name):
    nch, rows, _ = y.shape
    e, d = wout.shape
    tm = _row_tile(rows, 512)
    with_loss = target is not None

    def body(*refs):
        y_ref, g_ref, x_ref, gt_ref, w_ref, lg_ref, lb_ref = refs[:7]
        if with_loss:
            t_ref, br_ref, dxo_ref, loss_ref = refs[7:]
        else:
            br_ref, xo_ref = refs[7:]
        z = _gated(y_ref, g_ref, nch)
        br_ref[...] = _dot(z, w_ref[...])
        if with_loss:
            @pl.when(pl.program_id(0) == 0)
            def _():
                loss_ref[...] = jnp.zeros_like(loss_ref)

        def norm(j, c):
            rows = pl.ds(pl.multiple_of(j * LN_ROWS, LN_ROWS), LN_ROWS)
            xhat, _ = _ln_stats(ALPHA * x_ref[rows, :] + gt_ref[...] * br_ref[rows, :])
            xo = xhat * lg_ref[...] + lb_ref[...]
            if with_loss:
                err = xo - t_ref[rows, :]
                dxo_ref[rows, :] = err * (1.0 / d)
                col = jnp.sum(err * err, axis=0, keepdims=True)
                loss_ref[...] += sum(col[:, k * LANE:(k + 1) * LANE] for k in range(d // LANE))
            else:
                xo_ref[rows, :] = xo
            return c

        lax.fori_loop(0, tm // LN_ROWS, norm, 0)

    chunk_spec = pl.BlockSpec((nch, tm, LANE), lambda i: (0, i, 0))
    g_spec = chunk_spec
    row_spec = pl.BlockSpec((tm, d), lambda i: (i, 0))
    vec_spec = pl.BlockSpec((1, d), lambda i: (0, 0))
    in_specs = [chunk_spec, g_spec, row_spec, vec_spec, pl.BlockSpec((e, d), lambda i: (0, 0)), vec_spec, vec_spec]
    args = [y, ug, xin, gt, wout, lg, lb]
    out_specs = [row_spec, row_spec]
    out_shape = [jax.ShapeDtypeStruct((rows, d), F32)] * 2
    if with_loss:
        in_specs.append(row_spec)
        args.append(target)
        out_specs.append(pl.BlockSpec((1, LANE), lambda i: (0, 0)))
        out_shape.append(jax.ShapeDtypeStruct((1, LANE), F32))
    return _call(body, name=name, grid=(rows // tm,), in_specs=in_specs, out_specs=out_specs, out_shape=out_shape)(*args)


def _outproj_bwd(dxo, xin, br, y, ug, gt, lg, wout, name):
    nch, rows, _ = y.shape
    e, d = wout.shape
    tm = _row_tile(rows, 256)

    def body(dxo_ref, x_ref, br_ref, y_ref, g_ref, gt_ref, lg_ref, w_ref,
             dy_ref, dg_ref, dxres_ref, dbr_ref, dlg_ref, dlb_ref, dgt_ref):
        @pl.when(pl.program_id(0) == 0)
        def _():
            dlg_ref[...] = jnp.zeros_like(dlg_ref)
            dlb_ref[...] = jnp.zeros_like(dlb_ref)
            dgt_ref[...] = jnp.zeros_like(dgt_ref)

        def norm_bwd(j, c):
            rows = pl.ds(pl.multiple_of(j * LN_ROWS, LN_ROWS), LN_ROWS)
            dxo_v = dxo_ref[rows, :]
            brv = br_ref[rows, :]
            xhat, rstd = _ln_stats(ALPHA * x_ref[rows, :] + gt_ref[...] * brv)
            dxh = dxo_v * lg_ref[...]
            dr = rstd * (dxh - jnp.mean(dxh, axis=-1, keepdims=True) - xhat * jnp.mean(dxh * xhat, axis=-1, keepdims=True))
            dlg_ref[...] += jnp.sum(dxo_v * xhat, axis=0, keepdims=True)
            dlb_ref[...] += jnp.sum(dxo_v, axis=0, keepdims=True)
            dgt_ref[...] += jnp.sum(dr * brv, axis=0, keepdims=True)
            dxres_ref[rows, :] = ALPHA * dr
            dbr_ref[rows, :] = (gt_ref[...] * dr).astype(BF16)
            return c

        lax.fori_loop(0, tm // LN_ROWS, norm_bwd, 0)
        dz = _dot_nt(dbr_ref[...], w_ref[...])
        for k in range(nch):
            dzk = dz[:, k * LANE:(k + 1) * LANE]
            gk = g_ref[k].astype(F32)
            sk = _sigmoid(gk)
            dy_ref[k] = dzk * (gk * sk)
            dg_ref[:, k * LANE:(k + 1) * LANE] = (dzk * y_ref[k].astype(F32) * (sk * (1.0 + gk * (1.0 - sk)))).astype(BF16)

    chunk_spec = pl.BlockSpec((nch, tm, LANE), lambda i: (0, i, 0))
    g_spec = chunk_spec
    row_spec = pl.BlockSpec((tm, d), lambda i: (i, 0))
    vec_spec = pl.BlockSpec((1, d), lambda i: (0, 0))
    return _call(
        body, name=name, grid=(rows // tm,),
        in_specs=[row_spec, row_spec, row_spec, chunk_spec, g_spec, vec_spec, vec_spec, pl.BlockSpec((e, d), lambda i: (0, 0))],
        out_specs=[chunk_spec, pl.BlockSpec((tm, e), lambda i: (i, 0)), row_spec, row_spec, vec_spec, vec_spec, vec_spec],
        out_shape=[jax.ShapeDtypeStruct((nch, rows, LANE), F32), jax.ShapeDtypeStruct((rows, e), BF16),
                   jax.ShapeDtypeStruct((rows, d), F32), jax.ShapeDtypeStruct((rows, d), BF16)]
        + [jax.ShapeDtypeStruct((1, d), F32)] * 3,
    )(dxo, xin, br, y, ug, gt, lg, wout)


def _outproj_bwd_w(y, ug, dbr, gbuf, row0, name):
    nch, rows, _ = y.shape
    d = dbr.shape[1]
    e = nch * LANE
    es = e // N_CHIPS
    tm = _row_tile(rows, 1024)
    assert gbuf.shape[2] == d and row0 % es == 0

    def body(y_ref, g_ref, dbr_ref, buf_ref, o_ref):
        @pl.when(pl.program_id(0) == 0)
        def _():
            o_ref[...] = jnp.zeros_like(o_ref)

        z = _gated(y_ref, g_ref, nch)
        o_ref[...] += _dot_tn(z, dbr_ref[...]).reshape(N_CHIPS, es, d)

    return _call(
        body, name=name, grid=(rows // tm,),
        in_specs=[pl.BlockSpec((nch, tm, LANE), lambda i: (0, i, 0)),
                  pl.BlockSpec((nch, tm, LANE), lambda i: (0, i, 0)),
                  pl.BlockSpec((tm, d), lambda i: (i, 0)),
                  ANY_SPEC],
        out_specs=pl.BlockSpec((N_CHIPS, es, d), lambda i: (0, row0 // es, 0)),
        out_shape=jax.ShapeDtypeStruct(gbuf.shape, F32),
        input_output_aliases={3: 0},
    )(y, ug, dbr, gbuf)


def _scan(a_ref, b_ref, h_ref, *, length, init, reverse, a_shift, store):
    nblk = length // SUB
    unroll = min(SCAN_UNROLL, nblk)
    assert nblk % unroll == 0
    row = lax.broadcasted_iota(jnp.int32, (SUB, LANE), 0)
    last = 0 if reverse else SUB - 1
    edges = [(row >= SUB - k) if reverse else (row < k) for k in (1, 2, 4)]

    def local_scan(a, b):
        for k, edge in zip((1, 2, 4), edges):
            sh = (SUB - k) if reverse else k
            b = b + a * jnp.where(edge, 0.0, pltpu.roll(b, sh, 0))
            a = a * jnp.where(edge, 1.0, pltpu.roll(a, sh, 0))
        return a, b

    def step(i, carry):
        base = pl.multiple_of(((nblk // unroll - 1 - i) if reverse else i) * (unroll * SUB), unroll * SUB)
        order = range(unroll - 1, -1, -1) if reverse else range(unroll)
        loaded = [(a_ref[pl.ds(PAD + base + j * SUB + a_shift, SUB), :], b_ref[pl.ds(PAD + base + j * SUB, SUB), :])
                  for j in order]
        scanned = [local_scan(a, b) for a, b in loaded]
        for j, (a, b) in zip(order, scanned):
            if store:
                h_ref[pl.ds(PAD + base + j * SUB, SUB), :] = b + a * carry
            a_l = jnp.broadcast_to(a[last:last + 1, :], (SUB, LANE))
            b_l = jnp.broadcast_to(b[last:last + 1, :], (SUB, LANE))
            carry = b_l + a_l * carry
        return carry

    carry = lax.fori_loop(0, nblk // unroll, step, jnp.broadcast_to(init, (SUB, LANE)))
    return carry[0:1, :]


def _conv_fwd(src_ref, upad, u_ref, cw, cb, length):
    zeros = jnp.zeros((PAD, LANE), F32)
    upad[pl.ds(0, PAD), :] = zeros
    upad[pl.ds(PAD + length, PAD), :] = zeros
    rt = _row_tile(length, ROW_TILE)

    def copy(i, c):
        t0 = pl.multiple_of(i * rt, rt)
        upad[pl.ds(PAD + t0, rt), :] = src_ref[pl.ds(t0, rt), :]
        return c

    lax.fori_loop(0, length // rt, copy, 0)

    def tile(i, c):
        t0 = pl.multiple_of(i * rt, rt)
        acc = jnp.zeros((rt, LANE), F32)
        for k in range(CONV_TAPS):
            acc = acc + upad[pl.ds(t0 + PAD - CONV_LEFT + k, rt), :] * cw[k:k + 1, :]
        u_ref[pl.ds(t0, rt), :] = acc + cb
        return c

    lax.fori_loop(0, length // rt, tile, 0)


def _gates_fwd(u_ref, a_ref, b_ref, wa, wx, ba, bx, ls, length, keep=None):
    rt = _row_tile(length, GATE_TILE)
    ls_c = LRU_C * ls

    def tile(i, c):
        t0 = pl.multiple_of(i * rt, rt)
        ut = u_ref[pl.ds(t0, rt), :]
        ub = ut.astype(BF16)
        r = 0.5 * (jnp.tanh(_dot(ub, wa) + ba) + 1.0)
        ig = 0.5 * (jnp.tanh(_dot(ub, wx) + bx) + 1.0)
        if keep is not None:
            keep[0][pl.ds(t0, rt), :] = r
            keep[1][pl.ds(t0, rt), :] = ig
        la = r * ls_c
        a = jnp.exp(la)
        a_ref[pl.ds(PAD + t0, rt), :] = a
        q = _one_minus_sq(la, a)
        b_ref[pl.ds(PAD + t0, rt), :] = jnp.where(q > 0.0, q * lax.rsqrt(q), 0.0) * (ig * ut)
        return c

    lax.fori_loop(0, length // rt, tile, 0, unroll=min(GATE_UNROLL, length // rt))


def _lru_specs():
    return [pl.BlockSpec((CONV_TAPS, LANE), lambda n: (0, n)),
            pl.BlockSpec((1, LANE), lambda n: (0, n)),
            pl.BlockSpec((2, None, LANE, LANE), lambda n: (0, n, 0, 0)),
            pl.BlockSpec((2, None, LANE, LANE), lambda n: (0, n, 0, 0)),
            pl.BlockSpec((2, LANE), lambda n: (0, n)),
            pl.BlockSpec((2, LANE), lambda n: (0, n)),
            pl.BlockSpec((2, LANE), lambda n: (0, n))]


def _rglru_fwd(ug, uc, conv_w, conv_b, wa, wx, ba, bx, lam):
    nb = uc.shape[0]
    s_len, t_len = ug.shape[1], uc.shape[1]

    def body(u0_ref, uc0_ref, cw_ref, cb_ref, wa_ref, wx_ref, ba_ref, bx_ref, lam_ref, y_ref,
             upad, ubuf, abuf, hbuf):
        cw, cb = cw_ref[...], cb_ref[...]
        lsig = _log_sigmoid(lam_ref[...])
        zero = jnp.zeros((1, LANE), F32)
        _conv_fwd(uc0_ref, upad, ubuf, cw, cb, t_len)
        h0 = []
        for dr in range(2):
            _gates_fwd(ubuf, abuf, hbuf, wa_ref[dr], wx_ref[dr], ba_ref[dr:dr + 1, :], bx_ref[dr:dr + 1, :],
                       lsig[dr:dr + 1, :], t_len)
            h0.append(_scan(abuf, hbuf, hbuf, length=t_len, init=zero, reverse=(dr == 1), a_shift=0, store=False))
        _conv_fwd(u0_ref, upad, ubuf, cw, cb, s_len)
        rt = _row_tile(s_len, ROW_TILE)
        for dr in range(2):
            _gates_fwd(ubuf, abuf, hbuf, wa_ref[dr], wx_ref[dr], ba_ref[dr:dr + 1, :], bx_ref[dr:dr + 1, :],
                       lsig[dr:dr + 1, :], s_len)
            _scan(abuf, hbuf, hbuf, length=s_len, init=h0[dr], reverse=(dr == 1), a_shift=0, store=True)

            def acc(i, c, dr=dr):
                t0 = pl.multiple_of(i * rt, rt)
                h = hbuf[pl.ds(PAD + t0, rt), :]
                if dr == 0:
                    upad[pl.ds(PAD + t0, rt), :] = h
                else:
                    y_ref[pl.ds(t0, rt), :] = (upad[pl.ds(PAD + t0, rt), :] + h).astype(y_ref.dtype)
                return c

            lax.fori_loop(0, s_len // rt, acc, 0)

    seq = pltpu.VMEM((s_len + 2 * PAD, LANE), F32)
    return _call(
        body, name="rglru_fwd", grid=(nb,),
        in_specs=[pl.BlockSpec((None, s_len, LANE), lambda n: (n, 0, 0)),
                  pl.BlockSpec((None, t_len, LANE), lambda n: (n, 0, 0))] + _lru_specs(),
        out_specs=pl.BlockSpec((None, s_len, LANE), lambda n: (n, 0, 0)),
        out_shape=jax.ShapeDtypeStruct((nb, s_len, LANE), BF16),
        scratch_shapes=[seq, pltpu.VMEM((s_len, LANE), F32), seq, seq],
    )(ug, uc, conv_w, conv_b, wa, wx, ba, bx, lam)


def _rglru_bwd(ug, uc, dy, conv_w, conv_b, wa, wx, ba, bx, lam):
    nb = uc.shape[0]
    e = nb * LANE
    s_len, t_len = ug.shape[1], uc.shape[1]

    def body(u0_ref, uc0_ref, dy_ref, cw_ref, cb_ref, wa_ref, wx_ref, ba_ref, bx_ref, lam_ref,
             du_ref, duc_ref, dcw_ref, dcb_ref, dwa_ref, dwx_ref, dba_ref, dbx_ref, dlam_ref,
             upad, ubuf, abuf, hbuf, lbuf, dubuf, rbuf, ibuf, cpad, cu, ca0, ch0, ca1, ch1, cr0, ci0, cr1, ci1):
        cw, cb = cw_ref[...], cb_ref[...]
        lam_v = lam_ref[...]
        lsig = _log_sigmoid(lam_v)
        zero = jnp.zeros((1, LANE), F32)
        zpad = jnp.zeros((PAD, LANE), F32)
        for ref in (dcw_ref, dcb_ref, dwa_ref, dwx_ref, dba_ref, dbx_ref, dlam_ref):
            ref[...] = jnp.zeros_like(ref)

        def params(dr):
            return (wa_ref[dr], wx_ref[dr], ba_ref[dr:dr + 1, :], bx_ref[dr:dr + 1, :], lsig[dr:dr + 1, :])

        def direction_bwd(dr, u_ref, a_ref, h_ref, l_ref, gates, dub, length, first):
            wa_d, wx_d, ba_d, bx_d, ls_d = params(dr)
            rt = _row_tile(length, GATE_BWD_TILE)
            prev = 1 if dr == 1 else -1

            def tile(i, c):
                t0 = pl.multiple_of(i * rt, rt)
                ut = u_ref[pl.ds(t0, rt), :]
                ub = ut.astype(BF16)
                r = gates[0][pl.ds(t0, rt), :]
                ig = gates[1][pl.ds(t0, rt), :]
                la = r * (LRU_C * ls_d)
                a = a_ref[pl.ds(PAD + t0, rt), :]
                q = _one_minus_sq(la, a)
                rs = lax.rsqrt(q)
                sq = q * rs
                lm = l_ref[pl.ds(PAD + t0, rt), :]
                da = lm * h_ref[pl.ds(PAD + t0 + prev, rt), :]
                dsq = lm * ig * ut
                dig = lm * sq * ut
                dla = da * a - dsq * (a * a) * rs
                dr_ = dla * (LRU_C * ls_d)
                dlam_ref[dr:dr + 1, :] += jnp.sum(dla * (LRU_C * r), axis=0, keepdims=True)
                dpr = dr_ * r * (1.0 - r)
                dpi = dig * ig * (1.0 - ig)
                dba_ref[dr:dr + 1, :] += jnp.sum(dpr, axis=0, keepdims=True)
                dbx_ref[dr:dr + 1, :] += jnp.sum(dpi, axis=0, keepdims=True)
                dprb, dpib = dpr.astype(BF16), dpi.astype(BF16)
                dwa_ref[dr] += _dot_tn(ub, dprb)
                dwx_ref[dr] += _dot_tn(ub, dpib)
                dut = lm * sq * ig + 2.0 * (_dot_nt(dprb, wa_d) + _dot_nt(dpib, wx_d))
                if first:
                    dub[pl.ds(PAD + t0, rt), :] = dut
                else:
                    dub[pl.ds(PAD + t0, rt), :] += dut
                return c

            lax.fori_loop(0, length // rt, tile, 0, unroll=min(GATE_UNROLL, length // rt))

        def conv_bwd(dub, src_pad, out_ref, length):
            rt = _row_tile(length, ROW_TILE)

            def tile(i, c):
                t0 = pl.multiple_of(i * rt, rt)
                dut = dub[pl.ds(PAD + t0, rt), :]
                dcb_ref[...] += jnp.sum(dut, axis=0, keepdims=True)
                acc = jnp.zeros((rt, LANE), F32)
                for k in range(CONV_TAPS):
                    sh = CONV_LEFT - k
                    acc = acc + dub[pl.ds(PAD + t0 + sh, rt), :] * cw[k:k + 1, :]
                    dcw_ref[k:k + 1, :] += jnp.sum(dut * src_pad[pl.ds(PAD + t0 - sh, rt), :], axis=0, keepdims=True)
                out_ref[pl.ds(t0, rt), :] = acc.astype(out_ref.dtype)
                return c

            lax.fori_loop(0, length // rt, tile, 0)

        _conv_fwd(uc0_ref, cpad, cu, cw, cb, t_len)
        cbufs = ((ca0, ch0), (ca1, ch1))
        cgates = ((cr0, ci0), (cr1, ci1))
        h0 = []
        for dr in range(2):
            ca, chh = cbufs[dr]
            _gates_fwd(cu, ca, chh, *params(dr), t_len, keep=cgates[dr])
            h0.append(_scan(ca, chh, chh, length=t_len, init=zero, reverse=(dr == 1), a_shift=0, store=True))
        _conv_fwd(u0_ref, upad, ubuf, cw, cb, s_len)
        rt = _row_tile(s_len, ROW_TILE)
        dh0 = []
        for dr in range(2):
            rev = dr == 1
            _gates_fwd(ubuf, abuf, hbuf, *params(dr), s_len, keep=(rbuf, ibuf))
            _scan(abuf, hbuf, hbuf, length=s_len, init=h0[dr], reverse=rev, a_shift=0, store=True)
            first_row = PAD + s_len if rev else PAD - 1
            hbuf[pl.ds(first_row, 1), :] = h0[dr]
            end_row = PAD - 1 if rev else PAD + s_len
            abuf[pl.ds(end_row, 1), :] = zero

            def copy(i, c):
                t0 = pl.multiple_of(i * rt, rt)
                lbuf[pl.ds(PAD + t0, rt), :] = dy_ref[pl.ds(t0, rt), :]
                return c

            lax.fori_loop(0, s_len // rt, copy, 0)
            _scan(abuf, lbuf, lbuf, length=s_len, init=zero, reverse=not rev, a_shift=(-1 if rev else 1), store=True)
            start = PAD + s_len - 1 if rev else PAD
            dh0.append(abuf[pl.ds(start, 1), :] * lbuf[pl.ds(start, 1), :])
            direction_bwd(dr, ubuf, abuf, hbuf, lbuf, (rbuf, ibuf), dubuf, s_len, first=(dr == 0))
        dubuf[pl.ds(0, PAD), :] = zpad
        dubuf[pl.ds(PAD + s_len, PAD), :] = zpad
        conv_bwd(dubuf, upad, du_ref, s_len)
        lc = lbuf
        duc_buf = dubuf
        for dr in range(2):
            rev = dr == 1
            ca, chh = cbufs[dr]
            first_row = PAD + t_len if rev else PAD - 1
            chh[pl.ds(first_row, 1), :] = zero
            end_row = PAD - 1 if rev else PAD + t_len
            ca[pl.ds(end_row, 1), :] = zero + 1.0
            rtc = _row_tile(t_len, ROW_TILE)

            def clear(i, c):
                t0 = pl.multiple_of(i * rtc, rtc)
                lc[pl.ds(PAD + t0, rtc), :] = jnp.zeros((rtc, LANE), F32)
                return c

            lax.fori_loop(0, t_len // rtc, clear, 0)
            _scan(ca, lc, lc, length=t_len, init=dh0[dr], reverse=not rev, a_shift=(-1 if rev else 1), store=True)
            direction_bwd(dr, cu, ca, chh, lc, cgates[dr], duc_buf, t_len, first=(dr == 0))
        duc_buf[pl.ds(0, PAD), :] = zpad
        duc_buf[pl.ds(PAD + t_len, PAD), :] = zpad
        conv_bwd(duc_buf, cpad, duc_ref, t_len)
        dlam_ref[...] = dlam_ref[...] * (1.0 - _sigmoid(lam_v))

    seq = pltpu.VMEM((s_len + 2 * PAD, LANE), F32)
    cseq = pltpu.VMEM((t_len + 2 * PAD, LANE), F32)
    flat = pltpu.VMEM((s_len, LANE), F32)
    cflat = pltpu.VMEM((t_len, LANE), F32)
    vec2 = pl.BlockSpec((2, LANE), lambda n: (0, n))
    wspec = pl.BlockSpec((2, None, LANE, LANE), lambda n: (0, n, 0, 0))
    return _call(
        body, name="rglru_bwd", grid=(nb,),
        in_specs=[pl.BlockSpec((None, s_len, LANE), lambda n: (n, 0, 0)),
                  pl.BlockSpec((None, t_len, LANE), lambda n: (n, 0, 0)),
                  pl.BlockSpec((None, s_len, LANE), lambda n: (n, 0, 0))] + _lru_specs(),
        out_specs=[pl.BlockSpec((s_len, LANE), lambda n: (0, n)),
                   pl.BlockSpec((t_len, LANE), lambda n: (0, n)),
                   pl.BlockSpec((CONV_TAPS, LANE), lambda n: (0, n)),
                   pl.BlockSpec((1, LANE), lambda n: (0, n)),
                   wspec, wspec, vec2, vec2, vec2],
        out_shape=[jax.ShapeDtypeStruct((s_len, e), BF16), jax.ShapeDtypeStruct((t_len, e), BF16),
                   jax.ShapeDtypeStruct((CONV_TAPS, e), F32), jax.ShapeDtypeStruct((1, e), F32),
                   jax.ShapeDtypeStruct((2, nb, LANE, LANE), F32), jax.ShapeDtypeStruct((2, nb, LANE, LANE), F32),
                   jax.ShapeDtypeStruct((2, e), F32), jax.ShapeDtypeStruct((2, e), F32), jax.ShapeDtypeStruct((2, e), F32)],
        scratch_shapes=[seq, flat, seq, seq, seq, seq, flat, flat,
                        cseq, cflat, cseq, cseq, cseq, cseq, cflat, cflat, cflat, cflat],
    )(ug, uc, dy, conv_w, conv_b, wa, wx, ba, bx, lam)


def _pool_windows(src_ref, out_ref, colbuf, rowbuf, half, transpose, s_len):
    gw = GRID_W
    lg = gw.bit_length() - 1
    n_rows = s_len // gw
    cp, rm = POOL_CPAD, 8 * gw
    stride = gw + 2 * cp
    rt = _row_tile(s_len, ROW_TILE)
    assert rt % gw == 0 and half <= cp
    gpt = rt // gw
    offs = range(-half, half)
    zmargin = jnp.zeros((cp, LANE), F32)

    def zcol(r, c):
        base = pl.multiple_of(r * stride, SUB)
        colbuf[pl.ds(base, cp), :] = zmargin
        colbuf[pl.ds(base + cp + gw, cp), :] = zmargin
        return c

    lax.fori_loop(0, n_rows, zcol, 0)

    def zrow(i, c):
        t0 = pl.multiple_of(i * gw, gw)
        rowbuf[pl.ds(t0, gw), :] = jnp.zeros((gw, LANE), F32)
        rowbuf[pl.ds(rm + s_len + t0, gw), :] = jnp.zeros((gw, LANE), F32)
        return c

    lax.fori_loop(0, rm // gw, zrow, 0)

    col = lax.broadcasted_iota(jnp.int32, (gw, LANE), 0)
    ccnt = (jnp.minimum(col + half, gw) - jnp.maximum(col - half, 0)).astype(F32)

    def row_counts(t0):
        row = (t0 + lax.broadcasted_iota(jnp.int32, (rt, LANE), 0)) >> lg
        return (jnp.minimum(row + half, n_rows) - jnp.maximum(row - half, 0)).astype(F32)

    def col_base(t0, g):
        return pl.multiple_of((t0 // gw) * stride, SUB) + g * stride + cp

    def col_sum(t0, g, sign):
        acc = jnp.zeros((gw, LANE), F32)
        for o in offs:
            acc = acc + colbuf[pl.ds(col_base(t0, g) + sign * o, gw), :]
        return acc

    def row_sum(t0, sign):
        acc = jnp.zeros((rt, LANE), F32)
        for o in offs:
            acc = acc + rowbuf[pl.ds(rm + t0 + sign * o * gw, rt), :]
        return acc

    n_tiles = s_len // rt
    assert rt >= half * gw

    def loop(fn, edges=False):
        def step(i, c):
            t0 = pl.multiple_of(i * rt, rt)
            fn(t0, False) if edges else fn(t0)
            return c
        if edges:
            fn(0, True)
            if n_tiles > 1:
                fn(s_len - rt, True)
            lax.fori_loop(1, n_tiles - 1, step, 0)
        else:
            lax.fori_loop(0, n_tiles, step, 0)

    inv_ccnt = 1.0 / ccnt

    def by_row_count(v, t0, edge):
        return v / row_counts(t0) if edge else v * (1.0 / (2 * half))

    if not transpose:
        def fill(t0):
            for g in range(gpt):
                colbuf[pl.ds(col_base(t0, g), gw), :] = src_ref[pl.ds(t0 + g * gw, gw), :]

        def cols(t0):
            for g in range(gpt):
                rowbuf[pl.ds(rm + t0 + g * gw, gw), :] = col_sum(t0, g, 1) * inv_ccnt

        def rows(t0, edge):
            mean = by_row_count(row_sum(t0, 1), t0, edge)
            out_ref[pl.ds(t0, rt), :] = (mean - src_ref[pl.ds(t0, rt), :]).astype(out_ref.dtype)

        loop(fill)
        loop(cols)
        loop(rows, edges=True)
    else:
        def fill(t0, edge):
            rowbuf[pl.ds(rm + t0, rt), :] = by_row_count(src_ref[pl.ds(t0, rt), :], t0, edge)

        def rows(t0):
            acc = row_sum(t0, -1)
            for g in range(gpt):
                colbuf[pl.ds(col_base(t0, g), gw), :] = acc[g * gw:(g + 1) * gw, :] * inv_ccnt

        def cols(t0):
            for g in range(gpt):
                rows_g = pl.ds(t0 + g * gw, gw)
                out_ref[rows_g, :] = (col_sum(t0, g, -1) - src_ref[rows_g, :]).astype(out_ref.dtype)

        loop(fill, edges=True)
        loop(rows)
        loop(cols)


def _pool_map(src, nb, transpose, out_chunk_major, name):
    s_len = src.shape[1]
    cpg = nb // len(POOL_WINDOWS)

    def body(src_ref, out_ref, colbuf, rowbuf):
        n = pl.program_id(0)
        for gi, w in enumerate(POOL_WINDOWS):
            @pl.when(n // cpg == gi)
            def _(w=w):
                _pool_windows(src_ref, out_ref, colbuf, rowbuf, w // 2, transpose, s_len)

    if out_chunk_major:
        out_spec = pl.BlockSpec((None, s_len, LANE), lambda n: (n, 0, 0))
        out_shape = jax.ShapeDtypeStruct((nb, s_len, LANE), BF16)
    else:
        out_spec = pl.BlockSpec((s_len, LANE), lambda n: (0, n))
        out_shape = jax.ShapeDtypeStruct((s_len, nb * LANE), BF16)
    return _call(
        body, name=name, grid=(nb,),
        in_specs=[pl.BlockSpec((None, s_len, LANE), lambda n: (n, 0, 0))],
        out_specs=out_spec, out_shape=out_shape,
        scratch_shapes=[pltpu.VMEM((s_len // GRID_W * (GRID_W + 2 * POOL_CPAD), LANE), F32),
                        pltpu.VMEM((s_len + 16 * GRID_W, LANE), F32)],
    )(src)


def _group_weight(w_ref):
    return jnp.concatenate([w_ref[k] for k in range(N_CHIPS)], axis=0)


def _pool_mm_fwd(dm, wp, scale):
    nb, rows, _ = dm.shape
    _, ng, pq, pg = wp.shape
    cpg = pg // LANE
    tm = _row_tile(rows, 2048)

    def body(d_ref, w_ref, s_ref, y_ref):
        _put_chunks(y_ref, _dot(_cat(d_ref, cpg), _group_weight(w_ref)) * s_ref[...], cpg)

    cspec = pl.BlockSpec((cpg, tm, LANE), lambda i, g: (g, i, 0))
    return _call(
        body, name="pool_mm_fwd", grid=(rows // tm, ng),
        in_specs=[cspec, pl.BlockSpec((N_CHIPS, None, pq, pg), lambda i, g: (0, g, 0, 0)),
                  pl.BlockSpec((1, pg), lambda i, g: (0, g))],
        out_specs=cspec, out_shape=jax.ShapeDtypeStruct((nb, rows, LANE), BF16),
    )(dm, wp, scale)


def _pool_mm_bwd(dy, dm, wp, scale, gbuf, row0):
    nb, rows, _ = dm.shape
    _, ng, pq, pg = wp.shape
    cpg = pg // LANE
    tm = _row_tile(rows, 1024)
    nt = rows // tm
    assert gbuf.shape[2] == 2 * pg and row0 % pq == 0

    def body(dy_ref, d_ref, w_ref, s_ref, buf_ref, dd_ref, dwp_ref, dsc_ref, acc):
        i = pl.program_id(1)

        @pl.when(i == 0)
        def _():
            acc[...] = jnp.zeros_like(acc)
            dsc_ref[...] = jnp.zeros_like(dsc_ref)

        dyv = _cat(dy_ref, cpg)
        dc = _cat(d_ref, cpg)
        w = _group_weight(w_ref)
        dsc_ref[...] += jnp.sum(dyv * _dot(dc, w), axis=0, keepdims=True)
        dyp = (dyv * s_ref[...]).astype(BF16)
        _put_chunks(dd_ref, _dot_nt(dyp, w), cpg)
        acc[...] += _dot_tn(dc, dyp)

        @pl.when(i == nt - 1)
        def _():
            dwp_ref[...] = acc[...].reshape(N_CHIPS, pq, pg)

    cspec = pl.BlockSpec((cpg, tm, LANE), lambda g, i: (g, i, 0))
    sspec = pl.BlockSpec((1, pg), lambda g, i: (0, g))
    return _call(
        body, name="pool_mm_bwd", grid=(ng, nt),
        in_specs=[cspec, cspec, pl.BlockSpec((N_CHIPS, None, pq, pg), lambda g, i: (0, g, 0, 0)), sspec, ANY_SPEC],
        out_specs=[cspec, pl.BlockSpec((N_CHIPS, pq, pg), lambda g, i: (0, row0 // pq + g // 2, g % 2)), sspec],
        out_shape=[jax.ShapeDtypeStruct((nb, rows, LANE), F32), jax.ShapeDtypeStruct(gbuf.shape, F32),
                   jax.ShapeDtypeStruct((1, ng * pg), F32)],
        scratch_shapes=[pltpu.VMEM((pg, pg), F32)],
        input_output_aliases={4: 1},
    )(dy, dm, wp, scale, gbuf)


def _adamw_math(w, g, m, v):
    nm = ADAM_B1 * m + (1.0 - ADAM_B1) * g
    nv = ADAM_B2 * v + (1.0 - ADAM_B2) * jnp.square(g)
    m_hat = nm / (1.0 - ADAM_B1 ** ADAM_STEP)
    v_hat = nv / (1.0 - ADAM_B2 ** ADAM_STEP)
    return -ADAM_LR * (m_hat / (jnp.sqrt(v_hat) + ADAM_EPS) + ADAM_WD * w), nm, nv


def _adamw_param(w3, m3, v3, gsrcs, pick, tm, name, after=None):
    n_blk, rows, cols = w3.shape
    ng = len(gsrcs)

    def body(*refs):
        w_ref, m_ref, v_ref = refs[:3]
        g_refs = refs[3:3 + ng]
        go_ref, d_ref, nm_ref, nv_ref = refs[-4:]
        g = pick(pl.program_id(0), [r[...] for r in g_refs])
        go_ref[...] = g
        d_ref[...], nm_ref[...], nv_ref[...] = _adamw_math(w_ref[...], g, m_ref[...], v_ref[...])

    spec = pl.BlockSpec((None, tm, cols), lambda n, i: (n, i, 0))
    extra = [] if after is None else [after]
    return _call(
        body, name=name, grid=(n_blk, rows // tm),
        in_specs=[spec] * 3 + [pl.BlockSpec(shape, imap) for _, shape, imap in gsrcs] + [ANY_SPEC] * len(extra),
        out_specs=[spec] * 4, out_shape=[jax.ShapeDtypeStruct(w3.shape, F32)] * 4,
    )(w3, m3, v3, *[a for a, _, _ in gsrcs], *extra)


def _adamw_small(quads):
    n = len(quads)

    def body(*refs):
        ins, outs = refs[:4 * n], refs[4 * n:]
        for k in range(n):
            w, g, m, v = (r[...] for r in ins[4 * k:4 * k + 4])
            outs[3 * k][...], outs[3 * k + 1][...], outs[3 * k + 2][...] = _adamw_math(w, g, m, v)

    flat = [a for q in quads for a in q]
    res = _call(body, name="adamw_small", grid=(1,),
                in_specs=[pl.BlockSpec(a.shape, lambda i: (0, 0)) for a in flat],
                out_specs=[pl.BlockSpec(q[0].shape, lambda i: (0, 0)) for q in quads for _ in range(3)],
                out_shape=[jax.ShapeDtypeStruct(q[0].shape, F32) for q in quads for _ in range(3)])(*flat)
    return [tuple(res[3 * k:3 * k + 3]) for k in range(n)]


def _place():
    return lax.axis_index("x"), lax.axis_index("y"), lax.axis_index("c")


def _other_chips(x, y):
    return [(1 - x, y), (x, 1 - y), (1 - x, 1 - y)]


def _own_slab(a, devices=False):
    x, y, c = _place()
    n, me = (N_DEV, 4 * x + 2 * y + c) if devices else (N_CHIPS, 2 * x + y)
    return lax.dynamic_update_slice(lax.empty((n,) + a.shape, a.dtype), a[None], (me, 0, 0))


def _gather_chips(arrays, name):
    n = len(arrays)
    halves = [a.shape[0] // 2 for a in arrays]
    for a, h in zip(arrays, halves):
        assert 2 * h == a.shape[0] and h % (32 // a.dtype.itemsize) == 0
    lands = [_own_slab(a) for a in arrays]

    def body(*refs):
        outs = refs[n:2 * n]
        send_sems, recv_sems = refs[2 * n:]
        x, y, c = _place()
        me = 2 * x + y
        chips = _other_chips(x, y)

        def mine(k):
            return pl.ds(c * halves[k], halves[k])

        def theirs(k):
            return pl.ds((1 - c) * halves[k], halves[k])

        def push(k, j, src, dst, to):
            return pltpu.make_async_remote_copy(src_ref=src, dst_ref=dst, send_sem=send_sems.at[6 * k + j],
                                                recv_sem=recv_sems.at[6 * k + j], device_id=to, device_id_type=MESH)

        started = []
        for j, (cx, cy) in enumerate(chips):
            for k in range(n):
                own = outs[k].at[me, mine(k)]
                cp = push(k, j, own, own, (cx, cy, c))
                cp.start()
                started.append(cp)
        for j, (cx, cy) in enumerate(chips):
            for k in range(n):
                slab = outs[k].at[2 * cx + cy, mine(k)]
                push(k, j, slab, slab, (x, y, c)).wait_recv()
                fwd = push(k, 3 + j, slab, slab, (x, y, 1 - c))
                fwd.start()
                started.append(fwd)
        for j, (cx, cy) in enumerate(chips):
            for k in range(n):
                slab = outs[k].at[2 * cx + cy, theirs(k)]
                push(k, 3 + j, slab, slab, (x, y, c)).wait_recv()
        for cp in started:
            cp.wait_send()

    return _call(
        body, name=name, in_specs=[ANY_SPEC] * n, out_specs=[ANY_SPEC] * n,
        out_shape=[jax.ShapeDtypeStruct(a.shape, a.dtype) for a in lands],
        input_output_aliases={k: k for k in range(n)},
        scratch_shapes=[pltpu.SemaphoreType.DMA((6 * n,)), pltpu.SemaphoreType.DMA((6 * n,))],
    )(*lands)


HBM_SPEC = pl.BlockSpec(memory_space=pltpu.HBM)
SEM_SPEC = pl.BlockSpec(memory_space=pltpu.SEMAPHORE)
SIDE_EFFECT = pltpu.SideEffectType.DATAFLOW_SIDE_EFFECTING


def _n_peers(kind):
    return N_DEV - 1 if kind == "devices" else N_CHIPS - 1


def _push_copies(src_refs, land_refs, send_sems, recv_sems, kind):
    x, y, c = _place()
    if kind == "devices":
        me = 4 * x + 2 * y + c
        peers = [((me + j) % N_DEV, None) for j in range(1, N_DEV)]
        peers = [((to // 4, (to // 2) % 2, to % 2), None) for to, _ in peers]
    else:
        me = 2 * x + y
        peers = [((cx, cy, c), 2 * cx + cy) for cx, cy in _other_chips(x, y)]
    n = len(peers)
    copies = []
    for j, (dev, slab) in enumerate(peers):
        for k, (src, land) in enumerate(zip(src_refs, land_refs)):
            copies.append(pltpu.make_async_remote_copy(
                src_ref=src.at[slab] if kind == "slab" else src, dst_ref=land.at[me], send_sem=send_sems.at[n * k + j],
                recv_sem=recv_sems.at[n * k + j], device_id=dev, device_id_type=MESH))
    return copies


def _push_start(srcs, lands, kind, after, name):
    n = len(srcs)

    def body(*refs):
        src_refs, land_refs = refs[:n], refs[n:2 * n]
        send_sems, recv_sems = refs[2 * n + 1], refs[2 * n + 2]
        token = refs[-1]
        for cp in _push_copies(src_refs, land_refs, send_sems, recv_sems, kind):
            cp.start()
        token[...] = jnp.zeros_like(token)

    bufs = [pltpu.with_memory_space_constraint(a, pltpu.HBM) for a in list(srcs) + list(lands)]
    res = _call(
        body, name=name,
        out_shape=[pltpu.SemaphoreType.DMA((_n_peers(kind) * n,)), pltpu.SemaphoreType.DMA((_n_peers(kind) * n,))]
        + [pltpu.HBM(a.shape, a.dtype) for a in bufs] + [jax.ShapeDtypeStruct((SUB, LANE), F32)],
        in_specs=[HBM_SPEC] * (2 * n) + [ANY_SPEC],
        out_specs=[SEM_SPEC, SEM_SPEC] + [HBM_SPEC] * (2 * n) + [pl.BlockSpec(memory_space=pltpu.VMEM)],
        input_output_aliases={i: 2 + i for i in range(2 * n)},
        compiler_params=pltpu.CompilerParams(has_side_effects=SIDE_EFFECT),
    )(*bufs, after)
    return res[0], res[1], list(res[2:2 + n]), list(res[2 + n:2 + 2 * n]), res[-1]


def _push_wait(send_sems, recv_sems, srcs, lands, kind, after, name):
    n = len(srcs)

    def body(*refs):
        src_refs, land_refs = refs[:n], refs[n:2 * n]
        send_sems, recv_sems = refs[2 * n], refs[2 * n + 1]
        for cp in _push_copies(src_refs, land_refs, send_sems, recv_sems, kind):
            cp.wait_send()
            cp.wait_recv()

    res = _call(
        body, name=name,
        out_shape=[pltpu.HBM(a.shape, a.dtype) for a in list(srcs) + list(lands)],
        in_specs=[HBM_SPEC] * (2 * n) + [SEM_SPEC, SEM_SPEC, ANY_SPEC],
        out_specs=[HBM_SPEC] * (2 * n),
        input_output_aliases={i: i for i in range(2 * n)},
        compiler_params=pltpu.CompilerParams(has_side_effects=SIDE_EFFECT),
    )(*srcs, *lands, send_sems, recv_sems, after)
    return list(res[n:])


def _sibling_swap(g):
    _, rows, w = g.shape
    half = rows // 2

    def body(g_ref, out_ref, send_sem, recv_sem):
        x, y, c = _place()
        cp = pltpu.make_async_remote_copy(src_ref=g_ref.at[:, pl.ds((1 - c) * half, half)], dst_ref=out_ref,
                                          send_sem=send_sem, recv_sem=recv_sem, device_id=(x, y, 1 - c), device_id_type=MESH)
        cp.start()
        cp.wait()

    return _call(body, name="rs_sibling_swap", in_specs=[ANY_SPEC], out_specs=ANY_SPEC,
                 out_shape=jax.ShapeDtypeStruct((N_CHIPS, half, w), F32),
                 scratch_shapes=[pltpu.SemaphoreType.DMA, pltpu.SemaphoreType.DMA])(g)


def _pair_add(g, got, place):
    _, rows, w = g.shape
    half = rows // 2
    tm = _row_tile(half, RS_TILE)
    nt = half // tm

    def body(p_ref, a_ref, b_ref, o_ref, own_ref):
        v = a_ref[...] + b_ref[...]
        o_ref[...] = v.astype(BF16)

        @pl.when(pl.program_id(1) == p_ref[1])
        def _():
            own_ref[...] = v

    return _call(
        body, name="rs_pair_add",
        grid_spec=pltpu.PrefetchScalarGridSpec(
            num_scalar_prefetch=1, grid=(nt, N_CHIPS),
            in_specs=[pl.BlockSpec((None, tm, w), lambda i, s, p: (s, p[0] * nt + i, 0)),
                      pl.BlockSpec((None, tm, w), lambda i, s, p: (s, i, 0))],
            out_specs=[pl.BlockSpec((None, tm, w), lambda i, s, p: (s, i, 0)),
                       pl.BlockSpec((tm, w), lambda i, s, p: (i, 0))]),
        out_shape=[jax.ShapeDtypeStruct((N_CHIPS, half, w), BF16), jax.ShapeDtypeStruct((half, w), F32)],
    )(place, g, got)


def _sum_chips(parts, own, place):
    _, half, w = parts.shape
    tm = _row_tile(half, RS_TILE)
    nt = half // tm

    def body(p_ref, parts_ref, own_ref, o_ref):
        me = p_ref[1]
        t = [jnp.where(me == q, own_ref[...], parts_ref[q].astype(F32)) for q in range(N_CHIPS)]
        o_ref[...] = (t[0] + t[1]) + (t[2] + t[3])

    return _call(
        body, name="rs_sum_chips",
        grid_spec=pltpu.PrefetchScalarGridSpec(
            num_scalar_prefetch=1, grid=(nt,),
            in_specs=[pl.BlockSpec((N_CHIPS, tm, w), lambda i, p: (0, i, 0)), pl.BlockSpec((tm, w), lambda i, p: (i, 0))],
            out_specs=pl.BlockSpec((tm, w), lambda i, p: (p[0] * nt + i, 0))),
        out_shape=jax.ShapeDtypeStruct((2 * half, w), F32),
    )(place, parts, own)


def _sibling_gather(red):
    rows, w = red.shape
    half = rows // 2

    def body(in_ref, out_ref, send_sem, recv_sem):
        x, y, c = _place()
        mine = out_ref.at[pl.ds(c * half, half)]
        cp = pltpu.make_async_remote_copy(src_ref=mine, dst_ref=mine, send_sem=send_sem, recv_sem=recv_sem,
                                          device_id=(x, y, 1 - c), device_id_type=MESH)
        cp.start()
        other = out_ref.at[pl.ds((1 - c) * half, half)]
        pltpu.make_async_remote_copy(src_ref=other, dst_ref=other, send_sem=send_sem, recv_sem=recv_sem,
                                     device_id=(x, y, c), device_id_type=MESH).wait_recv()
        cp.wait_send()

    return _call(body, name="rs_sibling_gather", in_specs=[ANY_SPEC], out_specs=ANY_SPEC,
                 out_shape=jax.ShapeDtypeStruct(red.shape, F32), input_output_aliases={0: 0},
                 scratch_shapes=[pltpu.SemaphoreType.DMA, pltpu.SemaphoreType.DMA])(red)


def _rs_begin(g, place, name):
    pair, own = _pair_add(g, _sibling_swap(g), place)
    send, recv, pair, parts, token = _push_start([pair], [jnp.zeros_like(pair)], "slab", own, name + "_start")
    return (send, recv, pair, parts, own), token


def _rs_end(state, place, after, name):
    send, recv, pair, parts, own = state
    (parts,) = _push_wait(send, recv, pair, parts, "slab", after, name + "_wait")
    return _sibling_gather(_sum_chips(parts, own, place))


WEIGHTS = ("c_ctx", "w_mod", "b_mod", "w_in", "w_out", "ln_g", "ln_b", "conv_w", "conv_b", "lru_wa", "lru_ba", "lru_wx",
           "lru_bx", "lru_lam", "pool_w", "pool_scale")
SMALL_GATHERED = ("conv_w", "lru_ba", "lru_bx", "lru_lam", "pool_scale")
SMALL_UPDATED = ("c_ctx", "b_mod", "ln_g", "ln_b", "conv_w", "conv_b", "lru_ba", "lru_bx", "lru_lam", "pool_scale")


def kernel(x, c, ctx, c_ctx, w_mod, b_mod, w_in, w_out, ln_g, ln_b, conv_w, conv_b, lru_wa, lru_ba, lru_wx, lru_bx, lru_lam, pool_w, pool_scale, loss_target, m_c_ctx, m_w_mod, m_b_mod, m_w_in, m_w_out, m_ln_g, m_ln_b, m_conv_w, m_conv_b, m_lru_wa, m_lru_ba, m_lru_wx, m_lru_bx, m_lru_lam, m_pool_w, m_pool_scale, v_c_ctx, v_w_mod, v_b_mod, v_w_in, v_w_out, v_ln_g, v_ln_b, v_conv_w, v_conv_b, v_lru_wa, v_lru_ba, v_lru_wx, v_lru_bx, v_lru_lam, v_pool_w, v_pool_scale):
    weights = dict(c_ctx=c_ctx, w_mod=w_mod, b_mod=b_mod, w_in=w_in, w_out=w_out, ln_g=ln_g, ln_b=ln_b, conv_w=conv_w,
                   conv_b=conv_b, lru_wa=lru_wa, lru_ba=lru_ba, lru_wx=lru_wx, lru_bx=lru_bx, lru_lam=lru_lam,
                   pool_w=pool_w, pool_scale=pool_scale)
    mom1 = dict(c_ctx=m_c_ctx, w_mod=m_w_mod, b_mod=m_b_mod, w_in=m_w_in, w_out=m_w_out, ln_g=m_ln_g, ln_b=m_ln_b,
                conv_w=m_conv_w, conv_b=m_conv_b, lru_wa=m_lru_wa, lru_ba=m_lru_ba, lru_wx=m_lru_wx, lru_bx=m_lru_bx,
                lru_lam=m_lru_lam, pool_w=m_pool_w, pool_scale=m_pool_scale)
    mom2 = dict(c_ctx=v_c_ctx, w_mod=v_w_mod, b_mod=v_b_mod, w_in=v_w_in, w_out=v_w_out, ln_g=v_ln_g, ln_b=v_ln_b,
                conv_w=v_conv_w, conv_b=v_conv_b, lru_wa=v_lru_wa, lru_ba=v_lru_ba, lru_wx=v_lru_wx, lru_bx=v_lru_bx,
                lru_lam=v_lru_lam, pool_w=v_pool_w, pool_scale=v_pool_scale)
    xs, cx, target = x[0], ctx[0], loss_target[0]
    s_len, d = xs.shape
    es = w_out.shape[1]
    e = es * N_CHIPS
    nb = e // LANE
    c3 = w_mod.shape[2]
    n4 = w_in.shape[2]
    pq, pg = pool_w.shape[2], pool_w.shape[3]
    ng = len(POOL_WINDOWS)
    width = n4
    assert width == d and 2 * pg == width and 2 * nb * LANE == N_CHIPS * width and d % (2 * N_CHIPS) == 0
    px, py, pc = _place()
    place = jnp.stack([pc, 2 * px + py]).astype(jnp.int32)
    cctx2 = c_ctx[None, :]

    eq = e // N_CHIPS
    small_rows = [(conv_w[0], 0), (lru_ba[0], CONV_TAPS), (lru_bx[0], CONV_TAPS + 2), (lru_lam[0], CONV_TAPS + 4),
                  (pool_scale, CONV_TAPS + 6)]
    small = _rows_kernel([(a, r, 0) for a, r in small_rows], 2 * SUB, eq, "pack_small_weights")
    win0, sg = _gather_chips([w_in[0].astype(BF16), small], "gather_weights0")
    full = {n: jnp.swapaxes(sg[:, r:r + a.shape[0]], 0, 1).reshape(a.shape[0], e)
            for n, (a, r) in zip(SMALL_GATHERED, small_rows)}
    wa_h, wx_h = (0.5 * lru_wa[0]).astype(BF16), (0.5 * lru_wx[0]).astype(BF16)
    lru_args = (full["conv_w"], conv_b, wa_h, wx_h, 0.5 * full["lru_ba"], 0.5 * full["lru_bx"], full["lru_lam"])
    scale_f = full["pool_scale"]

    me8 = 4 * px + 2 * py + pc
    c_rows = _rows_kernel([(c, 0, 0)], SUB, d, "pack_c")
    c_send, c_recv, c_src, c_lands, c_token = _push_start([c_rows], [_own_slab(c_rows, devices=True)], "devices", sg,
                                                          "gather_c_start")
    (c_dev,) = _push_wait(c_send, c_recv, c_src, c_lands, "devices", c_token, "gather_c_wait")
    wm_mine = w_mod.astype(BF16)
    bm_mine = lax.dynamic_slice_in_dim(b_mod[:, None, :], place[1] * c3, c3, axis=2)
    (mod_g,) = _gather_chips([_mod_fwd(c_dev[:, 0, :], cctx2, wm_mine, bm_mine).reshape(DEPTH * 2 * SUB, c3)], "gather_mod")
    mod_all = jnp.transpose(mod_g.reshape(N_CHIPS, DEPTH, 2 * SUB, c3), (1, 2, 0, 3)).reshape(DEPTH, 2 * SUB, 3 * d)
    mod_mine = lax.dynamic_index_in_dim(mod_all, me8, axis=1, keepdims=False)
    later = [w_out[0].astype(BF16), w_in[1].astype(BF16), w_out[1].astype(BF16), pool_w.astype(BF16).reshape(ng * pq, pg)]
    w_send, w_recv, later, later_lands, w_token = _push_start(
        later, [_own_slab(a) for a in later], "same", mod_g, "gather_weights1_start")
    mod_mine = mod_mine + w_token[0:1, 0:1]

    def mod_parts(v):
        return v[None, :d], 1.0 + v[None, d:2 * d], v[None, 2 * d:]

    sh0, sc0, gt0 = mod_parts(mod_mine[0])
    shc, scc, _ = mod_parts(mod_all[0, N_DEV])
    sh1, sc1, gt1 = mod_parts(mod_mine[1])
    lg = [ln_g[l][None, :] for l in range(DEPTH)]
    lb = [ln_b[l][None, :] for l in range(DEPTH)]

    uu0, ug0 = _inproj_fwd(xs, sc0, sh0, win0, "inproj_fwd0")
    uc0 = _inproj_fwd(cx, scc, shc, win0[:2], "inproj_fwd_ctx")
    y0 = _rglru_fwd(uu0, uc0, *lru_args)
    wout0_g, win1, wout1_g, wp_g = _push_wait(w_send, w_recv, later, later_lands, "same", y0, "gather_weights1_wait")
    win = [win0, win1]
    wout = [wout0_g.reshape(e, d), wout1_g.reshape(e, d)]
    wp = wp_g.reshape(N_CHIPS, ng, pq, pg)
    br0, x1 = _outproj_fwd(y0, ug0, xs, gt0, wout[0], lg[0], lb[0], None, "outproj_fwd0")
    uu1, ug1 = _inproj_fwd(x1, sc1, sh1, win[1], "inproj_fwd1")
    d1 = _pool_map(uu1, nb, False, True, "pool_fwd")
    y1 = _pool_mm_fwd(d1, wp, scale_f)
    br1, dxo, loss_part = _outproj_fwd(y1, ug1, x1, gt1, wout[1], lg[1], lb[1], target, "outproj_fwd1")

    row_wout = d
    row_tail = d + es
    wq = 2 * (nb // N_CHIPS) * LANE * LANE // width
    whole = lambda r: (r + 2 * RS_TILE - 1) // (2 * RS_TILE) * (2 * RS_TILE)
    rows1 = whole(row_tail + pg // 2)
    rows0 = whole(row_tail + 2 * wq)
    fresh = lambda rows, used: (lax.empty if rows == used else jnp.zeros)((N_CHIPS, rows, width), F32)
    gbuf1 = fresh(rows1, row_tail + pg // 2)
    gbuf0 = fresh(rows0, row_tail + 2 * wq)

    dy1, dg1, dxres1, dbr1, dlg1, dlb1, dgt1 = _outproj_bwd(dxo, x1, br1, y1, ug1, gt1, lg[1], wout[1], "outproj_bwd1")
    gbuf1 = _outproj_bwd_w(y1, ug1, dbr1, gbuf1, row_wout, "outproj_bwd_w1")
    dd1, gbuf1, dscale = _pool_mm_bwd(dy1, d1, wp, scale_f, gbuf1, row_tail)
    du1 = _pool_map(dd1, nb, True, False, "pool_bwd")
    dx1, dsc1, dsh1 = _inproj_bwd_x([du1, dg1], x1, dxres1, sc1, win[1], "inproj_bwd_x1")
    gbuf1 = _inproj_bwd_w(x1, sc1, sh1, [du1, dg1], None, gbuf1, "inproj_bwd_w1")
    rs1, token1 = _rs_begin(gbuf1, place, "rs_exchange1")

    dy0, dg0, dxres0, dbr0, dlg0, dlb0, dgt0 = _outproj_bwd(dx1, xs, br0, y0, ug0, gt0 + token1[0:1, 0:1], lg[0], wout[0],
                                                            "outproj_bwd0")
    gbuf0 = _outproj_bwd_w(y0, ug0, dbr0, gbuf0, row_wout, "outproj_bwd_w0")
    du0, duc, dconv_w, dconv_b, dwa, dwx, dba, dbx, dlam = _rglru_bwd(uu0, uc0, dy0, *lru_args)
    dwin0c = _inproj_bwd_w(cx, scc, shc, [duc, jnp.zeros_like(duc)], None, None, "inproj_bwd_w_ctx")
    gbuf0 = _inproj_bwd_w(xs, sc0, sh0, [du0, dg0], dwin0c, gbuf0, "inproj_bwd_w0")

    def quarter(dw):
        t = dw.reshape(2, N_CHIPS, nb // N_CHIPS, LANE, LANE)
        return jnp.transpose(t, (1, 3, 0, 2, 4)).reshape(N_CHIPS, LANE, 2 * (nb // N_CHIPS) * LANE).reshape(N_CHIPS, wq, width)

    tail0 = jnp.concatenate([quarter(dwa), quarter(dwx)], axis=1)
    gbuf0 = lax.dynamic_update_slice(gbuf0, tail0, (0, row_tail, 0))
    red1 = _rs_end(rs1, place, gbuf0, "rs_exchange1")
    rs0, token0 = _rs_begin(gbuf0, place, "rs_exchange0")
    grad_x, dsc0, dsh0 = _inproj_bwd_x([du0, dg0], xs, dxres0, sc0 + token0[0:1, 0:1], win[0], "inproj_bwd_x0")
    dscc, dshc = _inproj_bwd_x([duc], cx, None, scc, win[0][:2], "inproj_bwd_x_ctx")

    k0 = VEC_KINDS
    vec = _rows_kernel(
        [(c, 0, 0), (loss_part, 0, d), (dsh0, 1, 0), (dsc0, 1, d), (dgt0, 1, 2 * d), (dshc, 2, 0), (dscc, 2, d),
         (dsh1, 3, 0), (dsc1, 3, d), (dgt1, 3, 2 * d),
         (dconv_b, k0, 0), (dlg0, k0, e), (dscale, k0 + 1, 0), (dlg1, k0 + 1, e), (dlb0, k0 + 2, 0), (dlb1, k0 + 2, d),
         (dconv_w, k0 + 3, 0), (dba, k0 + 7, 0), (dbx, k0 + 9, 0), (dlam, k0 + 11, 0)], VEC_ROWS, 3 * d, "pack_vec")
    v_send, v_recv, vec_l, vec_lands, v_token = _push_start([vec], [_own_slab(vec, devices=True)], "devices", vec,
                                                            "gather_devices_start")
    red0 = _rs_end(rs0, place, v_token, "rs_exchange0")
    quarters = red0[row_tail:row_tail + 2 * wq]
    q_send, q_recv, q_src, q_lands, q_token = _push_start([quarters], [_own_slab(quarters)], "same", red0,
                                                          "gather_replicated_start")

    tmw = _row_tile(d, 256)
    red_src = lambda red, r0, tm: (red, (tm, width), lambda n, i: (r0 // tm + i, 0))
    by_layer = lambda n, gs: jnp.where(n == 0, gs[0], gs[1])
    outs = {}
    outs["w_in"] = _adamw_param(w_in, m_w_in, v_w_in, [red_src(red0, 0, tmw), red_src(red1, 0, tmw)], by_layer, tmw, "adamw_w_in",
                                after=q_token)
    outs["w_out"] = _adamw_param(w_out, m_w_out, v_w_out, [red_src(red0, row_wout, tmw), red_src(red1, row_wout, tmw)],
                                 by_layer, tmw, "adamw_w_out")
    pw = [a.reshape(ng, pq, pg) for a in (pool_w, m_pool_w, v_pool_w)]
    outs["pool_w"] = [o.reshape(pool_w.shape) for o in _adamw_param(
        *pw, [(red1, (pq, pg), lambda n, i: (row_tail // pq + n // 2, n % 2))], lambda n, gs: gs[0], pq, "adamw_pool_w")]

    (gathered,) = _push_wait(v_send, v_recv, vec_l, vec_lands, "devices", outs["w_out"][1], "gather_devices_wait")
    gt_all = jnp.swapaxes(gathered, 0, 1)
    g_wmod = _mod_bwd_shard(gt_all, cctx2, place, c3)
    g_bmod, sq_err, g_small = _mod_bwd_rep(gt_all, d)
    loss = sq_err[0, 0] * (0.5 / d)
    cpart = _cctx_partial(gt_all, wm_mine[0], place)
    x_send, x_recv, x_src, x_lands, x_token = _push_start([cpart], [_own_slab(cpart)], "same", cpart, "gather_cctx_start")
    outs["w_mod"] = _adamw_param(w_mod, m_w_mod, v_w_mod, [(g_wmod, (None, tmw, c3), lambda n, i: (n, i, 0))],
                                 lambda n, gs: gs[0], tmw, "adamw_w_mod", after=x_token)
    (rep,) = _push_wait(q_send, q_recv, q_src, q_lands, "same", outs["w_mod"][1], "gather_replicated_wait")
    bq = nb // N_CHIPS
    rep_src = lambda r0: (rep, (None, LANE, bq * LANE), lambda n, i: (n % N_CHIPS, r0 // LANE, n // N_CHIPS))
    stack = lambda n, gs: jnp.concatenate([gs[0][:, k * LANE:(k + 1) * LANE] for k in range(bq)], axis=0)
    for name, r0, trio in (("lru_wa", 0, (lru_wa, m_lru_wa, v_lru_wa)), ("lru_wx", wq, (lru_wx, m_lru_wx, v_lru_wx))):
        blocks = [a.reshape(2 * N_CHIPS, bq * LANE, LANE) for a in trio]
        outs[name] = [o.reshape(lru_wa.shape) for o in _adamw_param(*blocks, [rep_src(r0)], stack, bq * LANE, "adamw_" + name)]

    (cparts,) = _push_wait(x_send, x_recv, x_src, x_lands, "same", outs["lru_wx"][1], "gather_cctx_wait")
    g_small = dict(g_small, c_ctx=_cctx_finish(cparts, cctx2), b_mod=g_bmod)
    for n in SMALL_GATHERED:
        g_small[n] = lax.dynamic_slice_in_dim(g_small[n], place[1] * eq, eq, axis=1)
    as2d = lambda a: a.reshape(-1, a.shape[-1])
    quads = [(as2d(weights[n]), g_small[n], as2d(mom1[n]), as2d(mom2[n])) for n in SMALL_UPDATED]
    for n, (q, res) in zip(SMALL_UPDATED, zip(quads, _adamw_small(quads))):
        outs[n] = [a.reshape(weights[n].shape) for a in (q[1],) + res]

    result = [loss, grad_x[None]]
    for j in range(4):
        result += [outs[n][j] for n in WEIGHTS]
    return tuple(result)
```

```python
import jax
import jax.numpy as jnp
from jax import lax
from jax.experimental import pallas as pl
from jax.experimental.pallas import tpu as pltpu

F32 = jnp.float32
BF16 = jnp.bfloat16
LANE = 128
SUB = 8
GRID_W = 64
POOL_WINDOWS = (2, 4, 8, 16)
LRU_C = 8.0
DEPTH = 2
ALPHA = float((2 * DEPTH) ** 0.25)
LN_EPS = 1e-5
ADAM_LR, ADAM_B1, ADAM_B2, ADAM_EPS, ADAM_WD, ADAM_STEP = 0.001, 0.9, 0.999, 1e-08, 0.01, 10
N_CHIPS = 4
N_DEV = 8
MESH = pl.DeviceIdType.MESH
ROW_TILE = 512
GATE_TILE = 2048
GATE_BWD_TILE = 1024
GATE_UNROLL = 1
CONV_TAPS = 4
CONV_LEFT = 2
PAD = 8
SCAN_UNROLL = 32
RS_TILE = 448
LN_ROWS = 128
POOL_CPAD = 16
VEC_KINDS = 4


def _call(body, **kw):
    return pl.pallas_call(body, **kw)


def _dot(a, b):
    return jnp.dot(a, b, preferred_element_type=F32)


def _dot_nt(a, b):
    return lax.dot_general(a, b, (((1,), (1,)), ((), ())), preferred_element_type=F32)


def _dot_tn(a, b):
    return lax.dot_general(a, b, (((0,), (0,)), ((), ())), preferred_element_type=F32)


def _sigmoid(v):
    return 0.5 * (jnp.tanh(0.5 * v) + 1.0)


def _silu(v):
    return v * _sigmoid(v)


def _dsilu(v):
    s = _sigmoid(v)
    return s * (1.0 + v * (1.0 - s))


def _log_sigmoid(v):
    z = jnp.exp(-jnp.abs(v))
    return jnp.minimum(v, 0.0) - jnp.where(z < 1e-4, z * (1.0 - 0.5 * z), jnp.log(1.0 + z))


def _one_minus_sq(la, a):
    return jnp.tanh(la) * (-1.0 - a * a)


def _cat(ref, n):
    return jnp.concatenate([ref[k] for k in range(n)], axis=1)


def _put_chunks(ref, val, n, base=0):
    for k in range(n):
        ref[base + k] = val[:, k * LANE:(k + 1) * LANE].astype(ref.dtype)


def _row_tile(rows, want):
    t = min(rows, want)
    assert rows % t == 0
    return t


ANY_SPEC = pl.BlockSpec(memory_space=pl.ANY)


def _mod_fwd(c_all, cctx, wm, bm):
    nl, d, c3 = wm.shape

    def body(c_ref, cx_ref, w_ref, b_ref, o_ref):
        cc = jnp.concatenate([c_ref[...], cx_ref[...], jnp.zeros((SUB - 1, d), F32)], axis=0)
        o_ref[...] = _dot(_silu(cc).astype(BF16), w_ref[...]) + b_ref[...]

    return _call(
        body, name="mod_fwd", grid=(nl,),
        in_specs=[pl.BlockSpec((N_DEV, d), lambda l: (0, 0)),
                  pl.BlockSpec((1, d), lambda l: (0, 0)),
                  pl.BlockSpec((None, d, c3), lambda l: (l, 0, 0)),
                  pl.BlockSpec((None, 1, c3), lambda l: (l, 0, 0))],
        out_specs=pl.BlockSpec((None, 2 * SUB, c3), lambda l: (l, 0, 0)),
        out_shape=jax.ShapeDtypeStruct((nl, 2 * SUB, c3), F32),
    )(c_all, cctx, wm, bm)


def _rows_kernel(parts, rows, cols, name):
    def body(*refs):
        o_ref = refs[-1]
        o_ref[...] = jnp.zeros_like(o_ref)
        for ref, (a, r0, c0) in zip(refs[:-1], parts):
            for k in range(a.shape[0]):
                o_ref[r0 + k:r0 + k + 1, c0:c0 + a.shape[1]] = ref[k:k + 1, :]

    return _call(body, name=name, grid=(1,),
                 in_specs=[pl.BlockSpec(a.shape, lambda i: (0, 0)) for a, _, _ in parts],
                 out_specs=pl.BlockSpec((rows, cols), lambda i: (0, 0)),
                 out_shape=jax.ShapeDtypeStruct((rows, cols), F32))(*[a for a, _, _ in parts])


def _mod_bwd_shard(gt, cctx, place, c3):
    d = cctx.shape[1]

    def body(p_ref, cs_ref, dm_ref, dmx_ref, cx_ref, o_ref):
        l = pl.program_id(0)
        lhs = jnp.concatenate([_silu(cs_ref[...]), _silu(cx_ref[...]), jnp.zeros((7, d), F32)], axis=0).astype(BF16)
        dmx = jnp.where(l == 0, jnp.sum(dmx_ref[...], axis=0, keepdims=True), 0.0)
        rhs = jnp.concatenate([dm_ref[...], dmx, jnp.zeros((7, c3), F32)], axis=0).astype(BF16)
        o_ref[...] = _dot_tn(lhs, rhs)

    return _call(
        body, name="mod_bwd_shard",
        grid_spec=pltpu.PrefetchScalarGridSpec(
            num_scalar_prefetch=1, grid=(DEPTH,),
            in_specs=[pl.BlockSpec((None, N_DEV, d), lambda l, p: (0, 0, 0)),
                      pl.BlockSpec((None, N_DEV, c3), lambda l, p: (1 + 2 * l, 0, p[1])),
                      pl.BlockSpec((None, N_DEV, c3), lambda l, p: (2, 0, p[1])),
                      pl.BlockSpec((1, d), lambda l, p: (0, 0))],
            out_specs=pl.BlockSpec((None, d, c3), lambda l, p: (l, 0, 0))),
        out_shape=jax.ShapeDtypeStruct((DEPTH, d, c3), F32),
    )(place, gt, gt, gt, cctx)


def _small_layout(d, e):
    k = VEC_KINDS
    return {
        "conv_b": ((1, e), [(0, k, 0)]),
        "ln_g": ((2, d), [(0, k, e), (1, k + 1, e)]),
        "pool_scale": ((1, e), [(0, k + 1, 0)]),
        "ln_b": ((2, d), [(0, k + 2, 0), (1, k + 2, d)]),
        "conv_w": ((CONV_TAPS, e), [(t, k + 3 + t, 0) for t in range(CONV_TAPS)]),
        "lru_ba": ((2, e), [(j, k + 7 + j, 0) for j in range(2)]),
        "lru_bx": ((2, e), [(j, k + 9 + j, 0) for j in range(2)]),
        "lru_lam": ((2, e), [(j, k + 11 + j, 0) for j in range(2)]),
    }


VEC_ROWS = 24


def _mod_bwd_rep(gt, d):
    d3 = gt.shape[2]
    layout = _small_layout(d, d3 - d)
    names = list(layout)

    def body(g_ref, db_ref, loss_ref, *small_refs):
        loss_ref[...] = jnp.zeros_like(loss_ref) + jnp.sum(g_ref[0][:, d:d + LANE])
        dm0 = jnp.sum(g_ref[1], axis=0, keepdims=True)
        dmx = jnp.sum(g_ref[2], axis=0, keepdims=True)
        dm1 = jnp.sum(g_ref[3], axis=0, keepdims=True)
        db_ref[0:1, :] = dm0 + dmx
        db_ref[1:2, :] = dm1
        for ref, name in zip(small_refs, names):
            shape, places = layout[name]
            for arr_row, vec_row, col0 in places:
                total = jnp.sum(g_ref[vec_row], axis=0, keepdims=True)
                ref[arr_row:arr_row + 1, :] = total[:, col0:col0 + shape[1]]

    outs = _call(
        body, name="mod_bwd_rep", grid=(1,),
        in_specs=[pl.BlockSpec(gt.shape, lambda i: (0, 0, 0))],
        out_specs=[pl.BlockSpec((DEPTH, d3), lambda i: (0, 0)), pl.BlockSpec((1, LANE), lambda i: (0, 0))]
        + [pl.BlockSpec(layout[n][0], lambda i: (0, 0)) for n in names],
        out_shape=[jax.ShapeDtypeStruct((DEPTH, d3), F32), jax.ShapeDtypeStruct((1, LANE), F32)]
        + [jax.ShapeDtypeStruct(layout[n][0], F32) for n in names],
    )(gt)
    return outs[0], outs[1], dict(zip(names, outs[2:]))


def _cctx_partial(gt, wm0, place):
    d, c3 = wm0.shape

    def body(p_ref, dmx_ref, w_ref, o_ref):
        dmx = jnp.sum(dmx_ref[...], axis=0, keepdims=True)
        o_ref[...] = _dot_nt(jnp.broadcast_to(dmx, (2 * SUB, c3)).astype(BF16), w_ref[...])

    return _call(
        body, name="cctx_partial",
        grid_spec=pltpu.PrefetchScalarGridSpec(
            num_scalar_prefetch=1, grid=(1,),
            in_specs=[pl.BlockSpec((None, N_DEV, c3), lambda i, p: (2, 0, p[1])), pl.BlockSpec((d, c3), lambda i, p: (0, 0))],
            out_specs=pl.BlockSpec((2 * SUB, d), lambda i, p: (0, 0))),
        out_shape=jax.ShapeDtypeStruct((2 * SUB, d), F32),
    )(place, gt, wm0)


def _cctx_finish(parts, cctx):
    d = cctx.shape[1]

    def body(p_ref, cx_ref, o_ref):
        total = (p_ref[0, 0:1, :] + p_ref[1, 0:1, :]) + (p_ref[2, 0:1, :] + p_ref[3, 0:1, :])
        o_ref[...] = total * _dsilu(cx_ref[...])

    return _call(body, name="cctx_finish", grid=(1,),
                 in_specs=[pl.BlockSpec(parts.shape, lambda i: (0, 0, 0)), pl.BlockSpec((1, d), lambda i: (0, 0))],
                 out_specs=pl.BlockSpec((1, d), lambda i: (0, 0)),
                 out_shape=jax.ShapeDtypeStruct((1, d), F32))(parts, cctx)


def _inproj_fwd(xin, sc1, sh, w, name):
    rows, d = xin.shape
    ns, _, n4 = w.shape
    cpb = n4 // LANE
    tm = _row_tile(rows, 512)
    assert ns in (2, 4)

    def body(x_ref, sc_ref, sh_ref, w_ref, *o_refs):
        h = (x_ref[...] * sc_ref[...] + sh_ref[...]).astype(BF16)
        for s in range(ns):
            _put_chunks(o_refs[s // 2], _dot(h, w_ref[s]), cpb, base=(s % 2) * cpb)

    spec = pl.BlockSpec((2 * cpb, tm, LANE), lambda i: (0, i, 0))
    dtypes = (F32, BF16)[:ns // 2]
    res = _call(
        body, name=name, grid=(rows // tm,),
        in_specs=[pl.BlockSpec((tm, d), lambda i: (i, 0)),
                  pl.BlockSpec((1, d), lambda i: (0, 0)),
                  pl.BlockSpec((1, d), lambda i: (0, 0)),
                  pl.BlockSpec((ns, d, n4), lambda i: (0, 0, 0))],
        out_specs=[spec] * len(dtypes),
        out_shape=[jax.ShapeDtypeStruct((2 * cpb, rows, LANE), t) for t in dtypes],
    )(xin, sc1, sh, w)
    return res[0] if ns == 2 else tuple(res)


def _inproj_bwd_x(dparts, xin, dxres, sc1, w, name):
    rows, d = xin.shape
    npart = len(dparts)
    e = dparts[0].shape[1]
    ns, _, n4 = w.shape
    per = e // n4
    assert per * npart == ns
    tm = _row_tile(rows, 512)
    has_res = dxres is not None

    def body(*refs):
        dp = refs[:npart]
        x_ref, sc_ref, w_ref = refs[npart:npart + 3]
        rest = refs[npart + 3:]
        if has_res:
            res_ref, dx_ref, dsc_ref, dsh_ref = rest
        else:
            dsc_ref, dsh_ref = rest
        i = pl.program_id(0)
        dh = jnp.zeros((tm, d), F32)
        for p in range(npart):
            v = dp[p][...]
            for q in range(per):
                dh = dh + _dot_nt(v[:, q * n4:(q + 1) * n4], w_ref[p * per + q])

        @pl.when(i == 0)
        def _():
            dsc_ref[...] = jnp.zeros_like(dsc_ref)
            dsh_ref[...] = jnp.zeros_like(dsh_ref)

        dsc_ref[...] += jnp.sum(dh * x_ref[...], axis=0, keepdims=True)
        dsh_ref[...] += jnp.sum(dh, axis=0, keepdims=True)
        if has_res:
            dx_ref[...] = res_ref[...] + dh * sc_ref[...]

    row_spec = pl.BlockSpec((tm, d), lambda i: (i, 0))
    vec_spec = pl.BlockSpec((1, d), lambda i: (0, 0))
    in_specs = [pl.BlockSpec((tm, e), lambda i: (i, 0))] * npart + [row_spec, vec_spec,
                                                                     pl.BlockSpec((ns, d, n4), lambda i: (0, 0, 0))]
    args = list(dparts) + [xin, sc1, w]
    out_specs, out_shape = [vec_spec, vec_spec], [jax.ShapeDtypeStruct((1, d), F32)] * 2
    if has_res:
        in_specs.append(row_spec)
        args.append(dxres)
        out_specs = [row_spec] + out_specs
        out_shape = [jax.ShapeDtypeStruct((rows, d), F32)] + out_shape
    return _call(body, name=name, grid=(rows // tm,), in_specs=in_specs, out_specs=out_specs, out_shape=out_shape)(*args)


def _inproj_bwd_w(xin, sc1, sh, dparts, init, gbuf, name):
    rows, d = xin.shape
    npart = len(dparts)
    e = dparts[0].shape[1]
    n4 = e // 2
    ns = 2 * npart
    tm = _row_tile(rows, 1024)
    nt = rows // tm
    has_init = init is not None
    into = gbuf is not None
    assert not into or (ns == N_CHIPS and gbuf.shape[2] == n4)

    def body(*refs):
        x_ref, sc_ref, sh_ref = refs[:3]
        dp = refs[3:3 + npart]
        init_ref = refs[3 + npart] if has_init else None
        o_ref = refs[-1]
        s, i = pl.program_id(0), pl.program_id(1)
        h = (x_ref[...] * sc_ref[...] + sh_ref[...]).astype(BF16)

        @pl.when(i == 0)
        def _():
            o_ref[...] = init_ref[...] if has_init else jnp.zeros_like(o_ref)

        for p in range(npart):
            @pl.when(s // 2 == p)
            def _(p=p):
                o_ref[...] += _dot_tn(h, dp[p][...])

    in_specs = [pl.BlockSpec((tm, d), lambda s, i: (i, 0)),
                pl.BlockSpec((1, d), lambda s, i: (0, 0)),
                pl.BlockSpec((1, d), lambda s, i: (0, 0))]
    in_specs += [pl.BlockSpec((tm, n4), lambda s, i: (i, s % 2))] * npart
    args = [xin, sc1, sh] + list(dparts)
    o_spec = pl.BlockSpec((None, d, n4), lambda s, i: (s, 0, 0))
    if has_init:
        in_specs.append(o_spec)
        args.append(init)
    extra = {}
    if into:
        in_specs.append(ANY_SPEC)
        args.append(gbuf)
        extra = dict(input_output_aliases={len(args) - 1: 0})
    out_shape = jax.ShapeDtypeStruct(gbuf.shape if into else (ns, d, n4), F32)
    return _call(body, name=name, grid=(ns, nt), in_specs=in_specs, out_specs=o_spec, out_shape=out_shape, **extra)(*args)


def _gated(y_ref, g_ref, nch):
    return jnp.concatenate([(y_ref[k].astype(F32) * _silu(g_ref[k].astype(F32))).astype(BF16) for k in range(nch)], axis=1)


def _ln_stats(r):
    mu = jnp.mean(r, axis=-1, keepdims=True)
    var = jnp.mean(jnp.square(r - mu), axis=-1, keepdims=True)
    rstd = lax.rsqrt(var + LN_EPS)
    return (r - mu) * rstd, rstd


def _outproj_fwd(y, ug, xin, gt, wout, lg, lb, target, name):
    nch, rows, _ = y.shape
    e, d = wout.shape
    tm = _row_tile(rows, 512)
    with_loss = target is not None

    def body(*refs):
        y_ref, g_ref, x_ref, gt_ref, w_ref, lg_ref, lb_ref = refs[:7]
        if with_loss:
            t_ref, br_ref, dxo_ref, loss_ref = refs[7:]
        else:
            br_ref, xo_ref = refs[7:]
        z = _gated(y_ref, g_ref, nch)
        br_ref[...] = _dot(z, w_ref[...])
        if with_loss:
            @pl.when(pl.program_id(0) == 0)
            def _():
                loss_ref[...] = jnp.zeros_like(loss_ref)

        def norm(j, c):
            rows = pl.ds(pl.multiple_of(j * LN_ROWS, LN_ROWS), LN_ROWS)
            xhat, _ = _ln_stats(ALPHA * x_ref[rows, :] + gt_ref[...] * br_ref[rows, :])
            xo = xhat * lg_ref[...] + lb_ref[...]
            if with_loss:
                err = xo - t_ref[rows, :]
                dxo_ref[rows, :] = err * (1.0 / d)
                col = jnp.sum(err * err, axis=0, keepdims=True)
                loss_ref[...] += sum(col[:, k * LANE:(k + 1) * LANE] for k in range(d // LANE))
            else:
                xo_ref[rows, :] = xo
            return c

        lax.fori_loop(0, tm // LN_ROWS, norm, 0)

    chunk_spec = pl.BlockSpec((nch, tm, LANE), lambda i: (0, i, 0))
    g_spec = chunk_spec
    row_spec = pl.BlockSpec((tm, d), lambda i: (i, 0))
    vec_spec = pl.BlockSpec((1, d), lambda i: (0, 0))
    in_specs = [chunk_spec, g_spec, row_spec, vec_spec, pl.BlockSpec((e, d), lambda i: (0, 0)), vec_spec, vec_spec]
    args = [y, ug, xin, gt, wout, lg, lb]
    out_specs = [row_spec, row_spec]
    out_shape = [jax.ShapeDtypeStruct((rows, d), F32)] * 2
    if with_loss:
        in_specs.append(row_spec)
        args.append(target)
        out_specs.append(pl.BlockSpec((1, LANE), lambda i: (0, 0)))
        out_shape.append(jax.ShapeDtypeStruct((1, LANE), F32))
    return _call(body, name=name, grid=(rows // tm,), in_specs=in_specs, out_specs=out_specs, out_shape=out_shape)(*args)


def _outproj_bwd(dxo, xin, br, y, ug, gt, lg, wout, name, dy_dtype=F32):
    nch, rows, _ = y.shape
    e, d = wout.shape
    tm = _row_tile(rows, 256)

    def body(dxo_ref, x_ref, br_ref, y_ref, g_ref, gt_ref, lg_ref, w_ref,
             dy_ref, dg_ref, dxres_ref, dbr_ref, dlg_ref, dlb_ref, dgt_ref):
        @pl.when(pl.program_id(0) == 0)
        def _():
            dlg_ref[...] = jnp.zeros_like(dlg_ref)
            dlb_ref[...] = jnp.zeros_like(dlb_ref)
            dgt_ref[...] = jnp.zeros_like(dgt_ref)

        def norm_bwd(j, c):
            rows = pl.ds(pl.multiple_of(j * LN_ROWS, LN_ROWS), LN_ROWS)
            dxo_v = dxo_ref[rows, :]
            brv = br_ref[rows, :]
            xhat, rstd = _ln_stats(ALPHA * x_ref[rows, :] + gt_ref[...] * brv)
            dxh = dxo_v * lg_ref[...]
            dr = rstd * (dxh - jnp.mean(dxh, axis=-1, keepdims=True) - xhat * jnp.mean(dxh * xhat, axis=-1, keepdims=True))
            dlg_ref[...] += jnp.sum(dxo_v * xhat, axis=0, keepdims=True)
            dlb_ref[...] += jnp.sum(dxo_v, axis=0, keepdims=True)
            dgt_ref[...] += jnp.sum(dr * brv, axis=0, keepdims=True)
            dxres_ref[rows, :] = ALPHA * dr
            dbr_ref[rows, :] = (gt_ref[...] * dr).astype(BF16)
            return c

        lax.fori_loop(0, tm // LN_ROWS, norm_bwd, 0)
        dz = _dot_nt(dbr_ref[...], w_ref[...])
        for k in range(nch):
            dzk = dz[:, k * LANE:(k + 1) * LANE]
            gk = g_ref[k].astype(F32)
            sk = _sigmoid(gk)
            dy_ref[k] = (dzk * (gk * sk)).astype(dy_ref.dtype)
            dg_ref[:, k * LANE:(k + 1) * LANE] = (dzk * y_ref[k].astype(F32) * (sk * (1.0 + gk * (1.0 - sk)))).astype(BF16)

    chunk_spec = pl.BlockSpec((nch, tm, LANE), lambda i: (0, i, 0))
    g_spec = chunk_spec
    row_spec = pl.BlockSpec((tm, d), lambda i: (i, 0))
    vec_spec = pl.BlockSpec((1, d), lambda i: (0, 0))
    return _call(
        body, name=name, grid=(rows // tm,),
        in_specs=[row_spec, row_spec, row_spec, chunk_spec, g_spec, vec_spec, vec_spec, pl.BlockSpec((e, d), lambda i: (0, 0))],
        out_specs=[chunk_spec, pl.BlockSpec((tm, e), lambda i: (i, 0)), row_spec, row_spec, vec_spec, vec_spec, vec_spec],
        out_shape=[jax.ShapeDtypeStruct((nch, rows, LANE), dy_dtype), jax.ShapeDtypeStruct((rows, e), BF16),
                   jax.ShapeDtypeStruct((rows, d), F32), jax.ShapeDtypeStruct((rows, d), BF16)]
        + [jax.ShapeDtypeStruct((1, d), F32)] * 3,
    )(dxo, xin, br, y, ug, gt, lg, wout)


def _outproj_bwd_w(y, ug, dbr, gbuf, row0, name):
    nch, rows, _ = y.shape
    d = dbr.shape[1]
    e = nch * LANE
    es = e // N_CHIPS
    tm = _row_tile(rows, 1024)
    assert gbuf.shape[2] == d and row0 % es == 0

    def body(y_ref, g_ref, dbr_ref, buf_ref, o_ref):
        @pl.when(pl.program_id(0) == 0)
        def _():
            o_ref[...] = jnp.zeros_like(o_ref)

        z = _gated(y_ref, g_ref, nch)
        o_ref[...] += _dot_tn(z, dbr_ref[...]).reshape(N_CHIPS, es, d)

    return _call(
        body, name=name, grid=(rows // tm,),
        in_specs=[pl.BlockSpec((nch, tm, LANE), lambda i: (0, i, 0)),
                  pl.BlockSpec((nch, tm, LANE), lambda i: (0, i, 0)),
                  pl.BlockSpec((tm, d), lambda i: (i, 0)),
                  ANY_SPEC],
        out_specs=pl.BlockSpec((N_CHIPS, es, d), lambda i: (0, row0 // es, 0)),
        out_shape=jax.ShapeDtypeStruct(gbuf.shape, F32),
        input_output_aliases={3: 0},
    )(y, ug, dbr, gbuf)


def _scan(a_ref, b_ref, h_ref, *, length, init, reverse, a_shift, store):
    nblk = length // SUB
    unroll = min(SCAN_UNROLL, nblk)
    assert nblk % unroll == 0
    row = lax.broadcasted_iota(jnp.int32, (SUB, LANE), 0)
    last = 0 if reverse else SUB - 1
    edges = [(row >= SUB - k) if reverse else (row < k) for k in (1, 2, 4)]

    def local_scan(a, b):
        for k, edge in zip((1, 2, 4), edges):
            sh = (SUB - k) if reverse else k
            b = b + a * jnp.where(edge, 0.0, pltpu.roll(b, sh, 0))
            a = a * jnp.where(edge, 1.0, pltpu.roll(a, sh, 0))
        return a, b

    def step(i, carry):
        base = pl.multiple_of(((nblk // unroll - 1 - i) if reverse else i) * (unroll * SUB), unroll * SUB)
        order = range(unroll - 1, -1, -1) if reverse else range(unroll)
        loaded = [(a_ref[pl.ds(PAD + base + j * SUB + a_shift, SUB), :], b_ref[pl.ds(PAD + base + j * SUB, SUB), :])
                  for j in order]
        scanned = [local_scan(a, b) for a, b in loaded]
        for j, (a, b) in zip(order, scanned):
            if store:
                h_ref[pl.ds(PAD + base + j * SUB, SUB), :] = b + a * carry
            a_l = jnp.broadcast_to(a[last:last + 1, :], (SUB, LANE))
            b_l = jnp.broadcast_to(b[last:last + 1, :], (SUB, LANE))
            carry = b_l + a_l * carry
        return carry

    carry = lax.fori_loop(0, nblk // unroll, step, jnp.broadcast_to(init, (SUB, LANE)))
    return carry[0:1, :]


def _conv_fwd(src_ref, upad, u_ref, cw, cb, length):
    zeros = jnp.zeros((PAD, LANE), F32)
    upad[pl.ds(0, PAD), :] = zeros
    upad[pl.ds(PAD + length, PAD), :] = zeros
    rt = _row_tile(length, ROW_TILE)

    def copy(i, c):
        t0 = pl.multiple_of(i * rt, rt)
        upad[pl.ds(PAD + t0, rt), :] = src_ref[pl.ds(t0, rt), :]
        return c

    lax.fori_loop(0, length // rt, copy, 0)

    def tile(i, c):
        t0 = pl.multiple_of(i * rt, rt)
        acc = jnp.zeros((rt, LANE), F32)
        for k in range(CONV_TAPS):
            acc = acc + upad[pl.ds(t0 + PAD - CONV_LEFT + k, rt), :] * cw[k:k + 1, :]
        u_ref[pl.ds(t0, rt), :] = acc + cb
        return c

    lax.fori_loop(0, length // rt, tile, 0)


def _gates_fwd(u_ref, a_ref, b_ref, wa, wx, ba, bx, ls, length, keep=None):
    rt = _row_tile(length, GATE_TILE)
    ls_c = LRU_C * ls

    def tile(i, c):
        t0 = pl.multiple_of(i * rt, rt)
        ut = u_ref[pl.ds(t0, rt), :]
        ub = ut.astype(BF16)
        r = 0.5 * (jnp.tanh(_dot(ub, wa) + ba) + 1.0)
        ig = 0.5 * (jnp.tanh(_dot(ub, wx) + bx) + 1.0)
        if keep is not None:
            keep[0][pl.ds(t0, rt), :] = r
            keep[1][pl.ds(t0, rt), :] = ig
        la = r * ls_c
        a = jnp.exp(la)
        a_ref[pl.ds(PAD + t0, rt), :] = a
        q = _one_minus_sq(la, a)
        b_ref[pl.ds(PAD + t0, rt), :] = jnp.where(q > 0.0, q * lax.rsqrt(q), 0.0) * (ig * ut)
        return c

    lax.fori_loop(0, length // rt, tile, 0, unroll=min(GATE_UNROLL, length // rt))


def _lru_specs():
    return [pl.BlockSpec((CONV_TAPS, LANE), lambda n: (0, n)),
            pl.BlockSpec((1, LANE), lambda n: (0, n)),
            pl.BlockSpec((2, None, LANE, LANE), lambda n: (0, n, 0, 0)),
            pl.BlockSpec((2, None, LANE, LANE), lambda n: (0, n, 0, 0)),
            pl.BlockSpec((2, LANE), lambda n: (0, n)),
            pl.BlockSpec((2, LANE), lambda n: (0, n)),
            pl.BlockSpec((2, LANE), lambda n: (0, n))]


def _rglru_fwd(ug, uc, conv_w, conv_b, wa, wx, ba, bx, lam):
    nb = uc.shape[0]
    s_len, t_len = ug.shape[1], uc.shape[1]

    def body(u0_ref, uc0_ref, cw_ref, cb_ref, wa_ref, wx_ref, ba_ref, bx_ref, lam_ref, y_ref,
             upad, ubuf, abuf, hbuf):
        cw, cb = cw_ref[...], cb_ref[...]
        lsig = _log_sigmoid(lam_ref[...])
        zero = jnp.zeros((1, LANE), F32)
        _conv_fwd(uc0_ref, upad, ubuf, cw, cb, t_len)
        h0 = []
        for dr in range(2):
            _gates_fwd(ubuf, abuf, hbuf, wa_ref[dr], wx_ref[dr], ba_ref[dr:dr + 1, :], bx_ref[dr:dr + 1, :],
                       lsig[dr:dr + 1, :], t_len)
            h0.append(_scan(abuf, hbuf, hbuf, length=t_len, init=zero, reverse=(dr == 1), a_shift=0, store=False))
        _conv_fwd(u0_ref, upad, ubuf, cw, cb, s_len)
        rt = _row_tile(s_len, ROW_TILE)
        for dr in range(2):
            _gates_fwd(ubuf, abuf, hbuf, wa_ref[dr], wx_ref[dr], ba_ref[dr:dr + 1, :], bx_ref[dr:dr + 1, :],
                       lsig[dr:dr + 1, :], s_len)
            _scan(abuf, hbuf, hbuf, length=s_len, init=h0[dr], reverse=(dr == 1), a_shift=0, store=True)

            def acc(i, c, dr=dr):
                t0 = pl.multiple_of(i * rt, rt)
                h = hbuf[pl.ds(PAD + t0, rt), :]
                if dr == 0:
                    upad[pl.ds(PAD + t0, rt), :] = h
                else:
                    y_ref[pl.ds(t0, rt), :] = (upad[pl.ds(PAD + t0, rt), :] + h).astype(y_ref.dtype)
                return c

            lax.fori_loop(0, s_len // rt, acc, 0)

    seq = pltpu.VMEM((s_len + 2 * PAD, LANE), F32)
    return _call(
        body, name="rglru_fwd", grid=(nb,),
        in_specs=[pl.BlockSpec((None, s_len, LANE), lambda n: (n, 0, 0)),
                  pl.BlockSpec((None, t_len, LANE), lambda n: (n, 0, 0))] + _lru_specs(),
        out_specs=pl.BlockSpec((None, s_len, LANE), lambda n: (n, 0, 0)),
        out_shape=jax.ShapeDtypeStruct((nb, s_len, LANE), BF16),
        scratch_shapes=[seq, pltpu.VMEM((s_len, LANE), F32), seq, seq],
    )(ug, uc, conv_w, conv_b, wa, wx, ba, bx, lam)


def _rglru_bwd(ug, uc, dy, conv_w, conv_b, wa, wx, ba, bx, lam):
    nb = uc.shape[0]
    e = nb * LANE
    s_len, t_len = ug.shape[1], uc.shape[1]

    def body(u0_ref, uc0_ref, dy_ref, cw_ref, cb_ref, wa_ref, wx_ref, ba_ref, bx_ref, lam_ref,
             du_ref, duc_ref, dcw_ref, dcb_ref, dwa_ref, dwx_ref, dba_ref, dbx_ref, dlam_ref,
             upad, ubuf, abuf, hbuf, lbuf, dubuf, rbuf, ibuf, cpad, cu, ca0, ch0, ca1, ch1, cr0, ci0, cr1, ci1):
        cw, cb = cw_ref[...], cb_ref[...]
        lam_v = lam_ref[...]
        lsig = _log_sigmoid(lam_v)
        zero = jnp.zeros((1, LANE), F32)
        zpad = jnp.zeros((PAD, LANE), F32)
        for ref in (dcw_ref, dcb_ref, dwa_ref, dwx_ref, dba_ref, dbx_ref, dlam_ref):
            ref[...] = jnp.zeros_like(ref)

        def params(dr):
            return (wa_ref[dr], wx_ref[dr], ba_ref[dr:dr + 1, :], bx_ref[dr:dr + 1, :], lsig[dr:dr + 1, :])

        def direction_bwd(dr, u_ref, a_ref, h_ref, l_ref, gates, dub, length, first):
            wa_d, wx_d, ba_d, bx_d, ls_d = params(dr)
            rt = _row_tile(length, GATE_BWD_TILE)
            prev = 1 if dr == 1 else -1

            def tile(i, c):
                t0 = pl.multiple_of(i * rt, rt)
                ut = u_ref[pl.ds(t0, rt), :]
                ub = ut.astype(BF16)
                r = gates[0][pl.ds(t0, rt), :]
                ig = gates[1][pl.ds(t0, rt), :]
                la = r * (LRU_C * ls_d)
                a = a_ref[pl.ds(PAD + t0, rt), :]
                q = _one_minus_sq(la, a)
                rs = lax.rsqrt(q)
                sq = q * rs
                lm = l_ref[pl.ds(PAD + t0, rt), :]
                da = lm * h_ref[pl.ds(PAD + t0 + prev, rt), :]
                dsq = lm * ig * ut
                dig = lm * sq * ut
                dla = da * a - dsq * (a * a) * rs
                dr_ = dla * (LRU_C * ls_d)
                dlam_ref[dr:dr + 1, :] += jnp.sum(dla * (LRU_C * r), axis=0, keepdims=True)
                dpr = dr_ * r * (1.0 - r)
                dpi = dig * ig * (1.0 - ig)
                dba_ref[dr:dr + 1, :] += jnp.sum(dpr, axis=0, keepdims=True)
                dbx_ref[dr:dr + 1, :] += jnp.sum(dpi, axis=0, keepdims=True)
                dprb, dpib = dpr.astype(BF16), dpi.astype(BF16)
                dwa_ref[dr] += _dot_tn(ub, dprb)
                dwx_ref[dr] += _dot_tn(ub, dpib)
                dut = lm * sq * ig + 2.0 * (_dot_nt(dprb, wa_d) + _dot_nt(dpib, wx_d))
                if first:
                    dub[pl.ds(PAD + t0, rt), :] = dut
                else:
                    dub[pl.ds(PAD + t0, rt), :] += dut
                return c

            lax.fori_loop(0, length // rt, tile, 0, unroll=min(GATE_UNROLL, length // rt))

        def conv_bwd(dub, src_pad, out_ref, length):
            rt = _row_tile(length, ROW_TILE)

            def tile(i, c):
                t0 = pl.multiple_of(i * rt, rt)
                dut = dub[pl.ds(PAD + t0, rt), :]
                dcb_ref[...] += jnp.sum(dut, axis=0, keepdims=True)
                acc = jnp.zeros((rt, LANE), F32)
                for k in range(CONV_TAPS):
                    sh = CONV_LEFT - k
                    acc = acc + dub[pl.ds(PAD + t0 + sh, rt), :] * cw[k:k + 1, :]
                    dcw_ref[k:k + 1, :] += jnp.sum(dut * src_pad[pl.ds(PAD + t0 - sh, rt), :], axis=0, keepdims=True)
                out_ref[pl.ds(t0, rt), :] = acc.astype(out_ref.dtype)
                return c

            lax.fori_loop(0, length // rt, tile, 0)

        _conv_fwd(uc0_ref, cpad, cu, cw, cb, t_len)
        cbufs = ((ca0, ch0), (ca1, ch1))
        cgates = ((cr0, ci0), (cr1, ci1))
        h0 = []
        for dr in range(2):
            ca, chh = cbufs[dr]
            _gates_fwd(cu, ca, chh, *params(dr), t_len, keep=cgates[dr])
            h0.append(_scan(ca, chh, chh, length=t_len, init=zero, reverse=(dr == 1), a_shift=0, store=True))
        _conv_fwd(u0_ref, upad, ubuf, cw, cb, s_len)
        rt = _row_tile(s_len, ROW_TILE)
        dh0 = []
        for dr in range(2):
            rev = dr == 1
            _gates_fwd(ubuf, abuf, hbuf, *params(dr), s_len, keep=(rbuf, ibuf))
            _scan(abuf, hbuf, hbuf, length=s_len, init=h0[dr], reverse=rev, a_shift=0, store=True)
            first_row = PAD + s_len if rev else PAD - 1
            hbuf[pl.ds(first_row, 1), :] = h0[dr]
            end_row = PAD - 1 if rev else PAD + s_len
            abuf[pl.ds(end_row, 1), :] = zero

            def copy(i, c):
                t0 = pl.multiple_of(i * rt, rt)
                lbuf[pl.ds(PAD + t0, rt), :] = dy_ref[pl.ds(t0, rt), :]
                return c

            lax.fori_loop(0, s_len // rt, copy, 0)
            _scan(abuf, lbuf, lbuf, length=s_len, init=zero, reverse=not rev, a_shift=(-1 if rev else 1), store=True)
            start = PAD + s_len - 1 if rev else PAD
            dh0.append(abuf[pl.ds(start, 1), :] * lbuf[pl.ds(start, 1), :])
            direction_bwd(dr, ubuf, abuf, hbuf, lbuf, (rbuf, ibuf), dubuf, s_len, first=(dr == 0))
        dubuf[pl.ds(0, PAD), :] = zpad
        dubuf[pl.ds(PAD + s_len, PAD), :] = zpad
        conv_bwd(dubuf, upad, du_ref, s_len)
        lc = lbuf
        duc_buf = dubuf
        for dr in range(2):
            rev = dr == 1
            ca, chh = cbufs[dr]
            first_row = PAD + t_len if rev else PAD - 1
            chh[pl.ds(first_row, 1), :] = zero
            end_row = PAD - 1 if rev else PAD + t_len
            ca[pl.ds(end_row, 1), :] = zero + 1.0
            rtc = _row_tile(t_len, ROW_TILE)

            def clear(i, c):
                t0 = pl.multiple_of(i * rtc, rtc)
                lc[pl.ds(PAD + t0, rtc), :] = jnp.zeros((rtc, LANE), F32)
                return c

            lax.fori_loop(0, t_len // rtc, clear, 0)
            _scan(ca, lc, lc, length=t_len, init=dh0[dr], reverse=not rev, a_shift=(-1 if rev else 1), store=True)
            direction_bwd(dr, cu, ca, chh, lc, cgates[dr], duc_buf, t_len, first=(dr == 0))
        duc_buf[pl.ds(0, PAD), :] = zpad
        duc_buf[pl.ds(PAD + t_len, PAD), :] = zpad
        conv_bwd(duc_buf, cpad, duc_ref, t_len)
        dlam_ref[...] = dlam_ref[...] * (1.0 - _sigmoid(lam_v))

    seq = pltpu.VMEM((s_len + 2 * PAD, LANE), F32)
    cseq = pltpu.VMEM((t_len + 2 * PAD, LANE), F32)
    flat = pltpu.VMEM((s_len, LANE), F32)
    cflat = pltpu.VMEM((t_len, LANE), F32)
    vec2 = pl.BlockSpec((2, LANE), lambda n: (0, n))
    wspec = pl.BlockSpec((2, None, LANE, LANE), lambda n: (0, n, 0, 0))
    return _call(
        body, name="rglru_bwd", grid=(nb,),
        in_specs=[pl.BlockSpec((None, s_len, LANE), lambda n: (n, 0, 0)),
                  pl.BlockSpec((None, t_len, LANE), lambda n: (n, 0, 0)),
                  pl.BlockSpec((None, s_len, LANE), lambda n: (n, 0, 0))] + _lru_specs(),
        out_specs=[pl.BlockSpec((s_len, LANE), lambda n: (0, n)),
                   pl.BlockSpec((t_len, LANE), lambda n: (0, n)),
                   pl.BlockSpec((CONV_TAPS, LANE), lambda n: (0, n)),
                   pl.BlockSpec((1, LANE), lambda n: (0, n)),
                   wspec, wspec, vec2, vec2, vec2],
        out_shape=[jax.ShapeDtypeStruct((s_len, e), BF16), jax.ShapeDtypeStruct((t_len, e), BF16),
                   jax.ShapeDtypeStruct((CONV_TAPS, e), F32), jax.ShapeDtypeStruct((1, e), F32),
                   jax.ShapeDtypeStruct((2, nb, LANE, LANE), F32), jax.ShapeDtypeStruct((2, nb, LANE, LANE), F32),
                   jax.ShapeDtypeStruct((2, e), F32), jax.ShapeDtypeStruct((2, e), F32), jax.ShapeDtypeStruct((2, e), F32)],
        scratch_shapes=[seq, flat, seq, seq, seq, seq, flat, flat,
                        cseq, cflat, cseq, cseq, cseq, cseq, cflat, cflat, cflat, cflat],
    )(ug, uc, dy, conv_w, conv_b, wa, wx, ba, bx, lam)


def _pool_windows(src_ref, out_ref, colbuf, rowbuf, half, transpose, s_len):
    gw = GRID_W
    lg = gw.bit_length() - 1
    n_rows = s_len // gw
    cp, rm = POOL_CPAD, 8 * gw
    stride = gw + 2 * cp
    rt = _row_tile(s_len, ROW_TILE)
    assert rt % gw == 0 and half <= cp
    gpt = rt // gw
    offs = range(-half, half)
    zmargin = jnp.zeros((cp, LANE), F32)

    def zcol(r, c):
        base = pl.multiple_of(r * stride, SUB)
        colbuf[pl.ds(base, cp), :] = zmargin
        colbuf[pl.ds(base + cp + gw, cp), :] = zmargin
        return c

    lax.fori_loop(0, n_rows, zcol, 0)

    def zrow(i, c):
        t0 = pl.multiple_of(i * gw, gw)
        rowbuf[pl.ds(t0, gw), :] = jnp.zeros((gw, LANE), F32)
        rowbuf[pl.ds(rm + s_len + t0, gw), :] = jnp.zeros((gw, LANE), F32)
        return c

    lax.fori_loop(0, rm // gw, zrow, 0)

    col = lax.broadcasted_iota(jnp.int32, (gw, LANE), 0)
    ccnt = (jnp.minimum(col + half, gw) - jnp.maximum(col - half, 0)).astype(F32)

    def row_counts(t0):
        row = (t0 + lax.broadcasted_iota(jnp.int32, (rt, LANE), 0)) >> lg
        return (jnp.minimum(row + half, n_rows) - jnp.maximum(row - half, 0)).astype(F32)

    def col_base(t0, g):
        return pl.multiple_of((t0 // gw) * stride, SUB) + g * stride + cp

    def col_sum(t0, g, sign):
        acc = jnp.zeros((gw, LANE), F32)
        for o in offs:
            acc = acc + colbuf[pl.ds(col_base(t0, g) + sign * o, gw), :]
        return acc

    def row_sum(t0, sign):
        acc = jnp.zeros((rt, LANE), F32)
        for o in offs:
            acc = acc + rowbuf[pl.ds(rm + t0 + sign * o * gw, rt), :]
        return acc

    n_tiles = s_len // rt
    assert rt >= half * gw

    def loop(fn, edges=False):
        def step(i, c):
            t0 = pl.multiple_of(i * rt, rt)
            fn(t0, False) if edges else fn(t0)
            return c
        if edges:
            fn(0, True)
            if n_tiles > 1:
                fn(s_len - rt, True)
            lax.fori_loop(1, n_tiles - 1, step, 0)
        else:
            lax.fori_loop(0, n_tiles, step, 0)

    inv_ccnt = 1.0 / ccnt

    def by_row_count(v, t0, edge):
        return v / row_counts(t0) if edge else v * (1.0 / (2 * half))

    if not transpose:
        def fill(t0):
            for g in range(gpt):
                colbuf[pl.ds(col_base(t0, g), gw), :] = src_ref[pl.ds(t0 + g * gw, gw), :]

        def cols(t0):
            for g in range(gpt):
                rowbuf[pl.ds(rm + t0 + g * gw, gw), :] = col_sum(t0, g, 1) * inv_ccnt

        def rows(t0, edge):
            mean = by_row_count(row_sum(t0, 1), t0, edge)
            out_ref[pl.ds(t0, rt), :] = (mean - src_ref[pl.ds(t0, rt), :]).astype(out_ref.dtype)

        loop(fill)
        loop(cols)
        loop(rows, edges=True)
    else:
        def fill(t0, edge):
            rowbuf[pl.ds(rm + t0, rt), :] = by_row_count(src_ref[pl.ds(t0, rt), :], t0, edge)

        def rows(t0):
            acc = row_sum(t0, -1)
            for g in range(gpt):
                colbuf[pl.ds(col_base(t0, g), gw), :] = acc[g * gw:(g + 1) * gw, :] * inv_ccnt

        def cols(t0):
            for g in range(gpt):
                rows_g = pl.ds(t0 + g * gw, gw)
                out_ref[rows_g, :] = (col_sum(t0, g, -1) - src_ref[rows_g, :]).astype(out_ref.dtype)

        loop(fill, edges=True)
        loop(rows)
        loop(cols)


def _pool_map(src, nb, transpose, out_chunk_major, name):
    s_len = src.shape[1]
    cpg = nb // len(POOL_WINDOWS)

    def body(src_ref, out_ref, colbuf, rowbuf):
        n = pl.program_id(0)
        for gi, w in enumerate(POOL_WINDOWS):
            @pl.when(n // cpg == gi)
            def _(w=w):
                _pool_windows(src_ref, out_ref, colbuf, rowbuf, w // 2, transpose, s_len)

    if out_chunk_major:
        out_spec = pl.BlockSpec((None, s_len, LANE), lambda n: (n, 0, 0))
        out_shape = jax.ShapeDtypeStruct((nb, s_len, LANE), BF16)
    else:
        out_spec = pl.BlockSpec((s_len, LANE), lambda n: (0, n))
        out_shape = jax.ShapeDtypeStruct((s_len, nb * LANE), BF16)
    return _call(
        body, name=name, grid=(nb,),
        in_specs=[pl.BlockSpec((None, s_len, LANE), lambda n: (n, 0, 0))],
        out_specs=out_spec, out_shape=out_shape,
        scratch_shapes=[pltpu.VMEM((s_len // GRID_W * (GRID_W + 2 * POOL_CPAD), LANE), F32),
                        pltpu.VMEM((s_len + 16 * GRID_W, LANE), F32)],
    )(src)


def _group_weight(w_ref):
    return jnp.concatenate([w_ref[k] for k in range(N_CHIPS)], axis=0)


def _pool_mm_fwd(dm, wp, scale):
    nb, rows, _ = dm.shape
    _, ng, pq, pg = wp.shape
    cpg = pg // LANE
    tm = _row_tile(rows, 2048)

    def body(d_ref, w_ref, s_ref, y_ref):
        _put_chunks(y_ref, _dot(_cat(d_ref, cpg), _group_weight(w_ref)) * s_ref[...], cpg)

    cspec = pl.BlockSpec((cpg, tm, LANE), lambda i, g: (g, i, 0))
    return _call(
        body, name="pool_mm_fwd", grid=(rows // tm, ng),
        in_specs=[cspec, pl.BlockSpec((N_CHIPS, None, pq, pg), lambda i, g: (0, g, 0, 0)),
                  pl.BlockSpec((1, pg), lambda i, g: (0, g))],
        out_specs=cspec, out_shape=jax.ShapeDtypeStruct((nb, rows, LANE), BF16),
    )(dm, wp, scale)


def _pool_mm_bwd(dy, dm, wp, scale, gbuf, row0):
    nb, rows, _ = dm.shape
    _, ng, pq, pg = wp.shape
    cpg = pg // LANE
    tm = _row_tile(rows, 1024)
    nt = rows // tm
    assert gbuf.shape[2] == 2 * pg and row0 % pq == 0

    def body(dy_ref, d_ref, w_ref, s_ref, buf_ref, dd_ref, dwp_ref, dsc_ref, acc):
        i = pl.program_id(1)

        @pl.when(i == 0)
        def _():
            acc[...] = jnp.zeros_like(acc)
            dsc_ref[...] = jnp.zeros_like(dsc_ref)

        dyv = _cat(dy_ref, cpg).astype(F32)
        dc = _cat(d_ref, cpg)
        w = _group_weight(w_ref)
        dsc_ref[...] += jnp.sum(dyv * _dot(dc, w), axis=0, keepdims=True)
        dyp = (dyv * s_ref[...]).astype(BF16)
        _put_chunks(dd_ref, _dot_nt(dyp, w), cpg)
        acc[...] += _dot_tn(dc, dyp)

        @pl.when(i == nt - 1)
        def _():
            dwp_ref[...] = acc[...].reshape(N_CHIPS, pq, pg)

    cspec = pl.BlockSpec((cpg, tm, LANE), lambda g, i: (g, i, 0))
    sspec = pl.BlockSpec((1, pg), lambda g, i: (0, g))
    return _call(
        body, name="pool_mm_bwd", grid=(ng, nt),
        in_specs=[cspec, cspec, pl.BlockSpec((N_CHIPS, None, pq, pg), lambda g, i: (0, g, 0, 0)), sspec, ANY_SPEC],
        out_specs=[cspec, pl.BlockSpec((N_CHIPS, pq, pg), lambda g, i: (0, row0 // pq + g // 2, g % 2)), sspec],
        out_shape=[jax.ShapeDtypeStruct((nb, rows, LANE), F32), jax.ShapeDtypeStruct(gbuf.shape, F32),
                   jax.ShapeDtypeStruct((1, ng * pg), F32)],
        scratch_shapes=[pltpu.VMEM((pg, pg), F32)],
        input_output_aliases={4: 1},
    )(dy, dm, wp, scale, gbuf)


def _adamw_math(w, g, m, v):
    nm = ADAM_B1 * m + (1.0 - ADAM_B1) * g
    nv = ADAM_B2 * v + (1.0 - ADAM_B2) * jnp.square(g)
    m_hat = nm / (1.0 - ADAM_B1 ** ADAM_STEP)
    v_hat = nv / (1.0 - ADAM_B2 ** ADAM_STEP)
    return -ADAM_LR * (m_hat / (jnp.sqrt(v_hat) + ADAM_EPS) + ADAM_WD * w), nm, nv


def _adamw_param(w3, m3, v3, gsrcs, pick, tm, name, after=None):
    n_blk, rows, cols = w3.shape
    ng = len(gsrcs)

    def body(*refs):
        w_ref, m_ref, v_ref = refs[:3]
        g_refs = refs[3:3 + ng]
        go_ref, d_ref, nm_ref, nv_ref = refs[-4:]
        g = pick(pl.program_id(0), [r[...] for r in g_refs])
        go_ref[...] = g
        d_ref[...], nm_ref[...], nv_ref[...] = _adamw_math(w_ref[...], g, m_ref[...], v_ref[...])

    spec = pl.BlockSpec((None, tm, cols), lambda n, i: (n, i, 0))
    extra = [] if after is None else [after]
    return _call(
        body, name=name, grid=(n_blk, rows // tm),
        in_specs=[spec] * 3 + [pl.BlockSpec(shape, imap) for _, shape, imap in gsrcs] + [ANY_SPEC] * len(extra),
        out_specs=[spec] * 4, out_shape=[jax.ShapeDtypeStruct(w3.shape, F32)] * 4,
    )(w3, m3, v3, *[a for a, _, _ in gsrcs], *extra)


def _adamw_small(quads):
    n = len(quads)

    def body(*refs):
        ins, outs = refs[:4 * n], refs[4 * n:]
        for k in range(n):
            w, g, m, v = (r[...] for r in ins[4 * k:4 * k + 4])
            outs[3 * k][...], outs[3 * k + 1][...], outs[3 * k + 2][...] = _adamw_math(w, g, m, v)

    flat = [a for q in quads for a in q]
    res = _call(body, name="adamw_small", grid=(1,),
                in_specs=[pl.BlockSpec(a.shape, lambda i: (0, 0)) for a in flat],
                out_specs=[pl.BlockSpec(q[0].shape, lambda i: (0, 0)) for q in quads for _ in range(3)],
                out_shape=[jax.ShapeDtypeStruct(q[0].shape, F32) for q in quads for _ in range(3)])(*flat)
    return [tuple(res[3 * k:3 * k + 3]) for k in range(n)]


def _place():
    return lax.axis_index("x"), lax.axis_index("y"), lax.axis_index("c")


def _other_chips(x, y):
    return [(1 - x, y), (x, 1 - y), (1 - x, 1 - y)]


def _own_slab(a, devices=False):
    x, y, c = _place()
    n, me = (N_DEV, 4 * x + 2 * y + c) if devices else (N_CHIPS, 2 * x + y)
    return lax.dynamic_update_slice(lax.empty((n,) + a.shape, a.dtype), a[None], (me, 0, 0))


def _gather_chips(arrays, name):
    n = len(arrays)
    halves = [a.shape[0] // 2 for a in arrays]
    for a, h in zip(arrays, halves):
        assert 2 * h == a.shape[0] and h % (32 // a.dtype.itemsize) == 0
    lands = [_own_slab(a) for a in arrays]

    def body(*refs):
        outs = refs[n:2 * n]
        send_sems, recv_sems = refs[2 * n:]
        x, y, c = _place()
        me = 2 * x + y
        chips = _other_chips(x, y)

        def mine(k):
            return pl.ds(c * halves[k], halves[k])

        def theirs(k):
            return pl.ds((1 - c) * halves[k], halves[k])

        def push(k, j, src, dst, to):
            return pltpu.make_async_remote_copy(src_ref=src, dst_ref=dst, send_sem=send_sems.at[6 * k + j],
                                                recv_sem=recv_sems.at[6 * k + j], device_id=to, device_id_type=MESH)

        started = []
        for j, (cx, cy) in enumerate(chips):
            for k in range(n):
                own = outs[k].at[me, mine(k)]
                cp = push(k, j, own, own, (cx, cy, c))
                cp.start()
                started.append(cp)
        for j, (cx, cy) in enumerate(chips):
            for k in range(n):
                slab = outs[k].at[2 * cx + cy, mine(k)]
                push(k, j, slab, slab, (x, y, c)).wait_recv()
                fwd = push(k, 3 + j, slab, slab, (x, y, 1 - c))
                fwd.start()
                started.append(fwd)
        for j, (cx, cy) in enumerate(chips):
            for k in range(n):
                slab = outs[k].at[2 * cx + cy, theirs(k)]
                push(k, 3 + j, slab, slab, (x, y, c)).wait_recv()
        for cp in started:
            cp.wait_send()

    return _call(
        body, name=name, in_specs=[ANY_SPEC] * n, out_specs=[ANY_SPEC] * n,
        out_shape=[jax.ShapeDtypeStruct(a.shape, a.dtype) for a in lands],
        input_output_aliases={k: k for k in range(n)},
        scratch_shapes=[pltpu.SemaphoreType.DMA((6 * n,)), pltpu.SemaphoreType.DMA((6 * n,))],
    )(*lands)


HBM_SPEC = pl.BlockSpec(memory_space=pltpu.HBM)
SEM_SPEC = pl.BlockSpec(memory_space=pltpu.SEMAPHORE)
SIDE_EFFECT = pltpu.SideEffectType.DATAFLOW_SIDE_EFFECTING


def _n_peers(kind):
    return N_DEV - 1 if kind == "devices" else N_CHIPS - 1


def _push_copies(src_refs, land_refs, send_sems, recv_sems, kind):
    x, y, c = _place()
    if kind == "devices":
        me = 4 * x + 2 * y + c
        peers = [((me + j) % N_DEV, None) for j in range(1, N_DEV)]
        peers = [((to // 4, (to // 2) % 2, to % 2), None) for to, _ in peers]
    else:
        me = 2 * x + y
        peers = [((cx, cy, c), 2 * cx + cy) for cx, cy in _other_chips(x, y)]
    n = len(peers)
    copies = []
    for j, (dev, slab) in enumerate(peers):
        for k, (src, land) in enumerate(zip(src_refs, land_refs)):
            copies.append(pltpu.make_async_remote_copy(
                src_ref=src.at[slab] if kind == "slab" else src, dst_ref=land.at[me], send_sem=send_sems.at[n * k + j],
                recv_sem=recv_sems.at[n * k + j], device_id=dev, device_id_type=MESH))
    return copies


def _push_start(srcs, lands, kind, after, name):
    n = len(srcs)

    def body(*refs):
        src_refs, land_refs = refs[:n], refs[n:2 * n]
        send_sems, recv_sems = refs[2 * n + 1], refs[2 * n + 2]
        token = refs[-1]
        for cp in _push_copies(src_refs, land_refs, send_sems, recv_sems, kind):
            cp.start()
        token[...] = jnp.zeros_like(token)

    bufs = [pltpu.with_memory_space_constraint(a, pltpu.HBM) for a in list(srcs) + list(lands)]
    res = _call(
        body, name=name,
        out_shape=[pltpu.SemaphoreType.DMA((_n_peers(kind) * n,)), pltpu.SemaphoreType.DMA((_n_peers(kind) * n,))]
        + [pltpu.HBM(a.shape, a.dtype) for a in bufs] + [jax.ShapeDtypeStruct((SUB, LANE), F32)],
        in_specs=[HBM_SPEC] * (2 * n) + [ANY_SPEC],
        out_specs=[SEM_SPEC, SEM_SPEC] + [HBM_SPEC] * (2 * n) + [pl.BlockSpec(memory_space=pltpu.VMEM)],
        input_output_aliases={i: 2 + i for i in range(2 * n)},
        compiler_params=pltpu.CompilerParams(has_side_effects=SIDE_EFFECT),
    )(*bufs, after)
    return res[0], res[1], list(res[2:2 + n]), list(res[2 + n:2 + 2 * n]), res[-1]


def _push_wait(send_sems, recv_sems, srcs, lands, kind, after, name):
    n = len(srcs)

    def body(*refs):
        src_refs, land_refs = refs[:n], refs[n:2 * n]
        send_sems, recv_sems = refs[2 * n], refs[2 * n + 1]
        for cp in _push_copies(src_refs, land_refs, send_sems, recv_sems, kind):
            cp.wait_send()
            cp.wait_recv()

    res = _call(
        body, name=name,
        out_shape=[pltpu.HBM(a.shape, a.dtype) for a in list(srcs) + list(lands)],
        in_specs=[HBM_SPEC] * (2 * n) + [SEM_SPEC, SEM_SPEC, ANY_SPEC],
        out_specs=[HBM_SPEC] * (2 * n),
        input_output_aliases={i: i for i in range(2 * n)},
        compiler_params=pltpu.CompilerParams(has_side_effects=SIDE_EFFECT),
    )(*srcs, *lands, send_sems, recv_sems, after)
    return list(res[n:])


def _sibling_swap(g):
    _, rows, w = g.shape
    half = rows // 2

    def body(g_ref, out_ref, send_sem, recv_sem):
        x, y, c = _place()
        cp = pltpu.make_async_remote_copy(src_ref=g_ref.at[:, pl.ds((1 - c) * half, half)], dst_ref=out_ref,
                                          send_sem=send_sem, recv_sem=recv_sem, device_id=(x, y, 1 - c), device_id_type=MESH)
        cp.start()
        cp.wait()

    return _call(body, name="rs_sibling_swap", in_specs=[ANY_SPEC], out_specs=ANY_SPEC,
                 out_shape=jax.ShapeDtypeStruct((N_CHIPS, half, w), F32),
                 scratch_shapes=[pltpu.SemaphoreType.DMA, pltpu.SemaphoreType.DMA])(g)


def _pair_add(g, got, place):
    _, rows, w = g.shape
    half = rows // 2
    tm = _row_tile(half, RS_TILE)
    nt = half // tm

    def body(p_ref, a_ref, b_ref, o_ref, own_ref):
        v = a_ref[...] + b_ref[...]
        o_ref[...] = v.astype(BF16)

        @pl.when(pl.program_id(1) == p_ref[1])
        def _():
            own_ref[...] = v

    return _call(
        body, name="rs_pair_add",
        grid_spec=pltpu.PrefetchScalarGridSpec(
            num_scalar_prefetch=1, grid=(nt, N_CHIPS),
            in_specs=[pl.BlockSpec((None, tm, w), lambda i, s, p: (s, p[0] * nt + i, 0)),
                      pl.BlockSpec((None, tm, w), lambda i, s, p: (s, i, 0))],
            out_specs=[pl.BlockSpec((None, tm, w), lambda i, s, p: (s, i, 0)),
                       pl.BlockSpec((tm, w), lambda i, s, p: (i, 0))]),
        out_shape=[jax.ShapeDtypeStruct((N_CHIPS, half, w), BF16), jax.ShapeDtypeStruct((half, w), F32)],
    )(place, g, got)


def _sum_chips(parts, own, place):
    _, half, w = parts.shape
    tm = _row_tile(half, RS_TILE)
    nt = half // tm

    def body(p_ref, parts_ref, own_ref, o_ref):
        me = p_ref[1]
        t = [jnp.where(me == q, own_ref[...], parts_ref[q].astype(F32)) for q in range(N_CHIPS)]
        o_ref[...] = (t[0] + t[1]) + (t[2] + t[3])

    return _call(
        body, name="rs_sum_chips",
        grid_spec=pltpu.PrefetchScalarGridSpec(
            num_scalar_prefetch=1, grid=(nt,),
            in_specs=[pl.BlockSpec((N_CHIPS, tm, w), lambda i, p: (0, i, 0)), pl.BlockSpec((tm, w), lambda i, p: (i, 0))],
            out_specs=pl.BlockSpec((tm, w), lambda i, p: (p[0] * nt + i, 0))),
        out_shape=jax.ShapeDtypeStruct((2 * half, w), F32),
    )(place, parts, own)


def _sibling_gather(red):
    rows, w = red.shape
    half = rows // 2

    def body(in_ref, out_ref, send_sem, recv_sem):
        x, y, c = _place()
        mine = out_ref.at[pl.ds(c * half, half)]
        cp = pltpu.make_async_remote_copy(src_ref=mine, dst_ref=mine, send_sem=send_sem, recv_sem=recv_sem,
                                          device_id=(x, y, 1 - c), device_id_type=MESH)
        cp.start()
        other = out_ref.at[pl.ds((1 - c) * half, half)]
        pltpu.make_async_remote_copy(src_ref=other, dst_ref=other, send_sem=send_sem, recv_sem=recv_sem,
                                     device_id=(x, y, c), device_id_type=MESH).wait_recv()
        cp.wait_send()

    return _call(body, name="rs_sibling_gather", in_specs=[ANY_SPEC], out_specs=ANY_SPEC,
                 out_shape=jax.ShapeDtypeStruct(red.shape, F32), input_output_aliases={0: 0},
                 scratch_shapes=[pltpu.SemaphoreType.DMA, pltpu.SemaphoreType.DMA])(red)


def _rs_begin(g, place, name):
    pair, own = _pair_add(g, _sibling_swap(g), place)
    send, recv, pair, parts, token = _push_start([pair], [jnp.zeros_like(pair)], "slab", own, name + "_start")
    return (send, recv, pair, parts, own), token


def _rs_end(state, place, after, name):
    send, recv, pair, parts, own = state
    (parts,) = _push_wait(send, recv, pair, parts, "slab", after, name + "_wait")
    return _sibling_gather(_sum_chips(parts, own, place))


WEIGHTS = ("c_ctx", "w_mod", "b_mod", "w_in", "w_out", "ln_g", "ln_b", "conv_w", "conv_b", "lru_wa", "lru_ba", "lru_wx",
           "lru_bx", "lru_lam", "pool_w", "pool_scale")
SMALL_GATHERED = ("conv_w", "lru_ba", "lru_bx", "lru_lam", "pool_scale")
SMALL_UPDATED = ("c_ctx", "b_mod", "ln_g", "ln_b", "conv_w", "conv_b", "lru_ba", "lru_bx", "lru_lam", "pool_scale")


def kernel(x, c, ctx, c_ctx, w_mod, b_mod, w_in, w_out, ln_g, ln_b, conv_w, conv_b, lru_wa, lru_ba, lru_wx, lru_bx, lru_lam, pool_w, pool_scale, loss_target, m_c_ctx, m_w_mod, m_b_mod, m_w_in, m_w_out, m_ln_g, m_ln_b, m_conv_w, m_conv_b, m_lru_wa, m_lru_ba, m_lru_wx, m_lru_bx, m_lru_lam, m_pool_w, m_pool_scale, v_c_ctx, v_w_mod, v_b_mod, v_w_in, v_w_out, v_ln_g, v_ln_b, v_conv_w, v_conv_b, v_lru_wa, v_lru_ba, v_lru_wx, v_lru_bx, v_lru_lam, v_pool_w, v_pool_scale):
    weights = dict(c_ctx=c_ctx, w_mod=w_mod, b_mod=b_mod, w_in=w_in, w_out=w_out, ln_g=ln_g, ln_b=ln_b, conv_w=conv_w,
                   conv_b=conv_b, lru_wa=lru_wa, lru_ba=lru_ba, lru_wx=lru_wx, lru_bx=lru_bx, lru_lam=lru_lam,
                   pool_w=pool_w, pool_scale=pool_scale)
    mom1 = dict(c_ctx=m_c_ctx, w_mod=m_w_mod, b_mod=m_b_mod, w_in=m_w_in, w_out=m_w_out, ln_g=m_ln_g, ln_b=m_ln_b,
                conv_w=m_conv_w, conv_b=m_conv_b, lru_wa=m_lru_wa, lru_ba=m_lru_ba, lru_wx=m_lru_wx, lru_bx=m_lru_bx,
                lru_lam=m_lru_lam, pool_w=m_pool_w, pool_scale=m_pool_scale)
    mom2 = dict(c_ctx=v_c_ctx, w_mod=v_w_mod, b_mod=v_b_mod, w_in=v_w_in, w_out=v_w_out, ln_g=v_ln_g, ln_b=v_ln_b,
                conv_w=v_conv_w, conv_b=v_conv_b, lru_wa=v_lru_wa, lru_ba=v_lru_ba, lru_wx=v_lru_wx, lru_bx=v_lru_bx,
                lru_lam=v_lru_lam, pool_w=v_pool_w, pool_scale=v_pool_scale)
    xs, cx, target = x[0], ctx[0], loss_target[0]
    s_len, d = xs.shape
    es = w_out.shape[1]
    e = es * N_CHIPS
    nb = e // LANE
    c3 = w_mod.shape[2]
    n4 = w_in.shape[2]
    pq, pg = pool_w.shape[2], pool_w.shape[3]
    ng = len(POOL_WINDOWS)
    width = n4
    assert width == d and 2 * pg == width and 2 * nb * LANE == N_CHIPS * width and d % (2 * N_CHIPS) == 0
    px, py, pc = _place()
    place = jnp.stack([pc, 2 * px + py]).astype(jnp.int32)
    cctx2 = c_ctx[None, :]

    eq = e // N_CHIPS
    small_rows = [(conv_w[0], 0), (lru_ba[0], CONV_TAPS), (lru_bx[0], CONV_TAPS + 2), (lru_lam[0], CONV_TAPS + 4),
                  (pool_scale, CONV_TAPS + 6)]
    small = _rows_kernel([(a, r, 0) for a, r in small_rows], 2 * SUB, eq, "pack_small_weights")
    win0, sg = _gather_chips([w_in[0].astype(BF16), small], "gather_weights0")
    full = {n: jnp.swapaxes(sg[:, r:r + a.shape[0]], 0, 1).reshape(a.shape[0], e)
            for n, (a, r) in zip(SMALL_GATHERED, small_rows)}
    wa_h, wx_h = (0.5 * lru_wa[0]).astype(BF16), (0.5 * lru_wx[0]).astype(BF16)
    lru_args = (full["conv_w"], conv_b, wa_h, wx_h, 0.5 * full["lru_ba"], 0.5 * full["lru_bx"], full["lru_lam"])
    scale_f = full["pool_scale"]

    me8 = 4 * px + 2 * py + pc
    c_rows = _rows_kernel([(c, 0, 0)], SUB, d, "pack_c")
    c_send, c_recv, c_src, c_lands, c_token = _push_start([c_rows], [_own_slab(c_rows, devices=True)], "devices", sg,
                                                          "gather_c_start")
    (c_dev,) = _push_wait(c_send, c_recv, c_src, c_lands, "devices", c_token, "gather_c_wait")
    wm_mine = w_mod.astype(BF16)
    bm_mine = lax.dynamic_slice_in_dim(b_mod[:, None, :], place[1] * c3, c3, axis=2)
    (mod_g,) = _gather_chips([_mod_fwd(c_dev[:, 0, :], cctx2, wm_mine, bm_mine).reshape(DEPTH * 2 * SUB, c3)], "gather_mod")
    mod_all = jnp.transpose(mod_g.reshape(N_CHIPS, DEPTH, 2 * SUB, c3), (1, 2, 0, 3)).reshape(DEPTH, 2 * SUB, 3 * d)
    mod_mine = lax.dynamic_index_in_dim(mod_all, me8, axis=1, keepdims=False)
    later = [w_out[0].astype(BF16), w_in[1].astype(BF16), w_out[1].astype(BF16), pool_w.astype(BF16).reshape(ng * pq, pg)]
    w_send, w_recv, later, later_lands, w_token = _push_start(
        later, [_own_slab(a) for a in later], "same", mod_g, "gather_weights1_start")
    mod_mine = mod_mine + w_token[0:1, 0:1]

    def mod_parts(v):
        return v[None, :d], 1.0 + v[None, d:2 * d], v[None, 2 * d:]

    sh0, sc0, gt0 = mod_parts(mod_mine[0])
    shc, scc, _ = mod_parts(mod_all[0, N_DEV])
    sh1, sc1, gt1 = mod_parts(mod_mine[1])
    lg = [ln_g[l][None, :] for l in range(DEPTH)]
    lb = [ln_b[l][None, :] for l in range(DEPTH)]

    uu0, ug0 = _inproj_fwd(xs, sc0, sh0, win0, "inproj_fwd0")
    uc0 = _inproj_fwd(cx, scc, shc, win0[:2], "inproj_fwd_ctx")
    y0 = _rglru_fwd(uu0, uc0, *lru_args)
    wout0_g, win1, wout1_g, wp_g = _push_wait(w_send, w_recv, later, later_lands, "same", y0, "gather_weights1_wait")
    win = [win0, win1]
    wout = [wout0_g.reshape(e, d), wout1_g.reshape(e, d)]
    wp = wp_g.reshape(N_CHIPS, ng, pq, pg)
    br0, x1 = _outproj_fwd(y0, ug0, xs, gt0, wout[0], lg[0], lb[0], None, "outproj_fwd0")
    uu1, ug1 = _inproj_fwd(x1, sc1, sh1, win[1], "inproj_fwd1")
    d1 = _pool_map(uu1, nb, False, True, "pool_fwd")
    y1 = _pool_mm_fwd(d1, wp, scale_f)
    br1, dxo, loss_part = _outproj_fwd(y1, ug1, x1, gt1, wout[1], lg[1], lb[1], target, "outproj_fwd1")

    row_wout = d
    row_tail = d + es
    wq = 2 * (nb // N_CHIPS) * LANE * LANE // width
    whole = lambda r: (r + 2 * RS_TILE - 1) // (2 * RS_TILE) * (2 * RS_TILE)
    rows1 = whole(row_tail + pg // 2)
    rows0 = whole(row_tail + 2 * wq)
    fresh = lambda rows, used: (lax.empty if rows == used else jnp.zeros)((N_CHIPS, rows, width), F32)
    gbuf1 = fresh(rows1, row_tail + pg // 2)
    gbuf0 = fresh(rows0, row_tail + 2 * wq)

    dy1, dg1, dxres1, dbr1, dlg1, dlb1, dgt1 = _outproj_bwd(dxo, x1, br1, y1, ug1, gt1, lg[1], wout[1], "outproj_bwd1",
                                                            dy_dtype=BF16)
    gbuf1 = _outproj_bwd_w(y1, ug1, dbr1, gbuf1, row_wout, "outproj_bwd_w1")
    dd1, gbuf1, dscale = _pool_mm_bwd(dy1, d1, wp, scale_f, gbuf1, row_tail)
    du1 = _pool_map(dd1, nb, True, False, "pool_bwd")
    dx1, dsc1, dsh1 = _inproj_bwd_x([du1, dg1], x1, dxres1, sc1, win[1], "inproj_bwd_x1")
    gbuf1 = _inproj_bwd_w(x1, sc1, sh1, [du1, dg1], None, gbuf1, "inproj_bwd_w1")
    rs1, token1 = _rs_begin(gbuf1, place, "rs_exchange1")

    dy0, dg0, dxres0, dbr0, dlg0, dlb0, dgt0 = _outproj_bwd(dx1, xs, br0, y0, ug0, gt0 + token1[0:1, 0:1], lg[0], wout[0],
                                                            "outproj_bwd0")
    gbuf0 = _outproj_bwd_w(y0, ug0, dbr0, gbuf0, row_wout, "outproj_bwd_w0")
    du0, duc, dconv_w, dconv_b, dwa, dwx, dba, dbx, dlam = _rglru_bwd(uu0, uc0, dy0, *lru_args)
    dwin0c = _inproj_bwd_w(cx, scc, shc, [duc, jnp.zeros_like(duc)], None, None, "inproj_bwd_w_ctx")
    gbuf0 = _inproj_bwd_w(xs, sc0, sh0, [du0, dg0], dwin0c, gbuf0, "inproj_bwd_w0")

    def quarter(dw):
        t = dw.reshape(2, N_CHIPS, nb // N_CHIPS, LANE, LANE)
        return jnp.transpose(t, (1, 3, 0, 2, 4)).reshape(N_CHIPS, LANE, 2 * (nb // N_CHIPS) * LANE).reshape(N_CHIPS, wq, width)

    tail0 = jnp.concatenate([quarter(dwa), quarter(dwx)], axis=1)
    gbuf0 = lax.dynamic_update_slice(gbuf0, tail0, (0, row_tail, 0))
    red1 = _rs_end(rs1, place, gbuf0, "rs_exchange1")
    rs0, token0 = _rs_begin(gbuf0, place, "rs_exchange0")
    grad_x, dsc0, dsh0 = _inproj_bwd_x([du0, dg0], xs, dxres0, sc0 + token0[0:1, 0:1], win[0], "inproj_bwd_x0")
    dscc, dshc = _inproj_bwd_x([duc], cx, None, scc, win[0][:2], "inproj_bwd_x_ctx")

    k0 = VEC_KINDS
    vec = _rows_kernel(
        [(c, 0, 0), (loss_part, 0, d), (dsh0, 1, 0), (dsc0, 1, d), (dgt0, 1, 2 * d), (dshc, 2, 0), (dscc, 2, d),
         (dsh1, 3, 0), (dsc1, 3, d), (dgt1, 3, 2 * d),
         (dconv_b, k0, 0), (dlg0, k0, e), (dscale, k0 + 1, 0), (dlg1, k0 + 1, e), (dlb0, k0 + 2, 0), (dlb1, k0 + 2, d),
         (dconv_w, k0 + 3, 0), (dba, k0 + 7, 0), (dbx, k0 + 9, 0), (dlam, k0 + 11, 0)], VEC_ROWS, 3 * d, "pack_vec")
    v_send, v_recv, vec_l, vec_lands, v_token = _push_start([vec], [_own_slab(vec, devices=True)], "devices", vec,
                                                            "gather_devices_start")
    red0 = _rs_end(rs0, place, v_token, "rs_exchange0")
    quarters = red0[row_tail:row_tail + 2 * wq]
    q_send, q_recv, q_src, q_lands, q_token = _push_start([quarters], [_own_slab(quarters)], "same", red0,
                                                          "gather_replicated_start")

    tmw = _row_tile(d, 256)
    red_src = lambda red, r0, tm: (red, (tm, width), lambda n, i: (r0 // tm + i, 0))
    by_layer = lambda n, gs: jnp.where(n == 0, gs[0], gs[1])
    outs = {}
    outs["w_in"] = _adamw_param(w_in, m_w_in, v_w_in, [red_src(red0, 0, tmw), red_src(red1, 0, tmw)], by_layer, tmw, "adamw_w_in",
                                after=q_token)
    outs["w_out"] = _adamw_param(w_out, m_w_out, v_w_out, [red_src(red0, row_wout, tmw), red_src(red1, row_wout, tmw)],
                                 by_layer, tmw, "adamw_w_out")
    pw = [a.reshape(ng, pq, pg) for a in (pool_w, m_pool_w, v_pool_w)]
    outs["pool_w"] = [o.reshape(pool_w.shape) for o in _adamw_param(
        *pw, [(red1, (pq, pg), lambda n, i: (row_tail // pq + n // 2, n % 2))], lambda n, gs: gs[0], pq, "adamw_pool_w")]

    (gathered,) = _push_wait(v_send, v_recv, vec_l, vec_lands, "devices", outs["w_out"][1], "gather_devices_wait")
    gt_all = jnp.swapaxes(gathered, 0, 1)
    g_wmod = _mod_bwd_shard(gt_all, cctx2, place, c3)
    g_bmod, sq_err, g_small = _mod_bwd_rep(gt_all, d)
    loss = sq_err[0, 0] * (0.5 / d)
    cpart = _cctx_partial(gt_all, wm_mine[0], place)
    x_send, x_recv, x_src, x_lands, x_token = _push_start([cpart], [_own_slab(cpart)], "same", cpart, "gather_cctx_start")
    outs["w_mod"] = _adamw_param(w_mod, m_w_mod, v_w_mod, [(g_wmod, (None, tmw, c3), lambda n, i: (n, i, 0))],
                                 lambda n, gs: gs[0], tmw, "adamw_w_mod", after=x_token)
    (rep,) = _push_wait(q_send, q_recv, q_src, q_lands, "same", outs["w_mod"][1], "gather_replicated_wait")
    bq = nb // N_CHIPS
    rep_src = lambda r0: (rep, (None, LANE, bq * LANE), lambda n, i: (n % N_CHIPS, r0 // LANE, n // N_CHIPS))
    stack = lambda n, gs: jnp.concatenate([gs[0][:, k * LANE:(k + 1) * LANE] for k in range(bq)], axis=0)
    for name, r0, trio in (("lru_wa", 0, (lru_wa, m_lru_wa, v_lru_wa)), ("lru_wx", wq, (lru_wx, m_lru_wx, v_lru_wx))):
        blocks = [a.reshape(2 * N_CHIPS, bq * LANE, LANE) for a in trio]
        outs[name] = [o.reshape(lru_wa.shape) for o in _adamw_param(*blocks, [rep_src(r0)], stack, bq * LANE, "adamw_" + name)]

    (cparts,) = _push_wait(x_send, x_recv, x_src, x_lands, "same", outs["lru_wx"][1], "gather_cctx_wait")
    g_small = dict(g_small, c_ctx=_cctx_finish(cparts, cctx2), b_mod=g_bmod)
    for n in SMALL_GATHERED:
        g_small[n] = lax.dynamic_slice_in_dim(g_small[n], place[1] * eq, eq, axis=1)
    as2d = lambda a: a.reshape(-1, a.shape[-1])
    quads = [(as2d(weights[n]), g_small[n], as2d(mom1[n]), as2d(mom2[n])) for n in SMALL_UPDATED]
    for n, (q, res) in zip(SMALL_UPDATED, zip(quads, _adamw_small(quads))):
        outs[n] = [a.reshape(weights[n].shape) for a in (q[1],) + res]

    result = [loss, grad_x[None]]
    for j in range(4):
        result += [outs[n][j] for n in WEIGHTS]
    return tuple(result)
```

```python
import jax
import jax.numpy as jnp
from jax import lax
from jax.experimental import pallas as pl
from jax.experimental.pallas import tpu as pltpu

F32 = jnp.float32
BF16 = jnp.bfloat16
LANE = 128
SUB = 8
GRID_W = 64
POOL_WINDOWS = (2, 4, 8, 16)
LRU_C = 8.0
DEPTH = 2
ALPHA = float((2 * DEPTH) ** 0.25)
LN_EPS = 1e-5
ADAM_LR, ADAM_B1, ADAM_B2, ADAM_EPS, ADAM_WD, ADAM_STEP = 0.001, 0.9, 0.999, 1e-08, 0.01, 10
N_CHIPS = 4
N_DEV = 8
MESH = pl.DeviceIdType.MESH
ROW_TILE = 512
GATE_TILE = 2048
GATE_BWD_TILE = 2048
GATE_UNROLL = 1
CONV_TAPS = 4
CONV_LEFT = 2
PAD = 8
SCAN_UNROLL = 32
RS_TILE = 448
LN_ROWS = 128
POOL_CPAD = 16
VEC_KINDS = 4


def _call(body, **kw):
    return pl.pallas_call(body, **kw)


def _dot(a, b):
    return jnp.dot(a, b, preferred_element_type=F32)


def _dot_nt(a, b):
    return lax.dot_general(a, b, (((1,), (1,)), ((), ())), preferred_element_type=F32)


def _dot_tn(a, b):
    return lax.dot_general(a, b, (((0,), (0,)), ((), ())), preferred_element_type=F32)


def _sigmoid(v):
    return 0.5 * (jnp.tanh(0.5 * v) + 1.0)


def _silu(v):
    return v * _sigmoid(v)


def _dsilu(v):
    s = _sigmoid(v)
    return s * (1.0 + v * (1.0 - s))


def _log_sigmoid(v):
    z = jnp.exp(-jnp.abs(v))
    return jnp.minimum(v, 0.0) - jnp.where(z < 1e-4, z * (1.0 - 0.5 * z), jnp.log(1.0 + z))


def _one_minus_sq(la, a):
    return jnp.tanh(la) * (-1.0 - a * a)


def _cat(ref, n):
    return jnp.concatenate([ref[k] for k in range(n)], axis=1)


def _put_chunks(ref, val, n, base=0):
    for k in range(n):
        ref[base + k] = val[:, k * LANE:(k + 1) * LANE].astype(ref.dtype)


def _row_tile(rows, want):
    t = min(rows, want)
    assert rows % t == 0
    return t


ANY_SPEC = pl.BlockSpec(memory_space=pl.ANY)


def _mod_fwd(c_all, cctx, wm, bm):
    nl, d, c3 = wm.shape

    def body(c_ref, cx_ref, w_ref, b_ref, o_ref):
        cc = jnp.concatenate([c_ref[...], cx_ref[...], jnp.zeros((SUB - 1, d), F32)], axis=0)
        o_ref[...] = _dot(_silu(cc).astype(BF16), w_ref[...]) + b_ref[...]

    return _call(
        body, name="mod_fwd", grid=(nl,),
        in_specs=[pl.BlockSpec((N_DEV, d), lambda l: (0, 0)),
                  pl.BlockSpec((1, d), lambda l: (0, 0)),
                  pl.BlockSpec((None, d, c3), lambda l: (l, 0, 0)),
                  pl.BlockSpec((None, 1, c3), lambda l: (l, 0, 0))],
        out_specs=pl.BlockSpec((None, 2 * SUB, c3), lambda l: (l, 0, 0)),
        out_shape=jax.ShapeDtypeStruct((nl, 2 * SUB, c3), F32),
    )(c_all, cctx, wm, bm)


def _rows_kernel(parts, rows, cols, name):
    def body(*refs):
        o_ref = refs[-1]
        o_ref[...] = jnp.zeros_like(o_ref)
        for ref, (a, r0, c0) in zip(refs[:-1], parts):
            for k in range(a.shape[0]):
                o_ref[r0 + k:r0 + k + 1, c0:c0 + a.shape[1]] = ref[k:k + 1, :]

    return _call(body, name=name, grid=(1,),
                 in_specs=[pl.BlockSpec(a.shape, lambda i: (0, 0)) for a, _, _ in parts],
                 out_specs=pl.BlockSpec((rows, cols), lambda i: (0, 0)),
                 out_shape=jax.ShapeDtypeStruct((rows, cols), F32))(*[a for a, _, _ in parts])


def _mod_bwd_shard(gt, cctx, place, c3):
    d = cctx.shape[1]

    def body(p_ref, cs_ref, dm_ref, dmx_ref, cx_ref, o_ref):
        l = pl.program_id(0)
        lhs = jnp.concatenate([_silu(cs_ref[...]), _silu(cx_ref[...]), jnp.zeros((7, d), F32)], axis=0).astype(BF16)
        dmx = jnp.where(l == 0, jnp.sum(dmx_ref[...], axis=0, keepdims=True), 0.0)
        rhs = jnp.concatenate([dm_ref[...], dmx, jnp.zeros((7, c3), F32)], axis=0).astype(BF16)
        o_ref[...] = _dot_tn(lhs, rhs)

    return _call(
        body, name="mod_bwd_shard",
        grid_spec=pltpu.PrefetchScalarGridSpec(
            num_scalar_prefetch=1, grid=(DEPTH,),
            in_specs=[pl.BlockSpec((None, N_DEV, d), lambda l, p: (0, 0, 0)),
                      pl.BlockSpec((None, N_DEV, c3), lambda l, p: (1 + 2 * l, 0, p[1])),
                      pl.BlockSpec((None, N_DEV, c3), lambda l, p: (2, 0, p[1])),
                      pl.BlockSpec((1, d), lambda l, p: (0, 0))],
            out_specs=pl.BlockSpec((None, d, c3), lambda l, p: (l, 0, 0))),
        out_shape=jax.ShapeDtypeStruct((DEPTH, d, c3), F32),
    )(place, gt, gt, gt, cctx)


def _small_layout(d, e):
    k = VEC_KINDS
    return {
        "conv_b": ((1, e), [(0, k, 0)]),
        "ln_g": ((2, d), [(0, k, e), (1, k + 1, e)]),
        "pool_scale": ((1, e), [(0, k + 1, 0)]),
        "ln_b": ((2, d), [(0, k + 2, 0), (1, k + 2, d)]),
        "conv_w": ((CONV_TAPS, e), [(t, k + 3 + t, 0) for t in range(CONV_TAPS)]),
        "lru_ba": ((2, e), [(j, k + 7 + j, 0) for j in range(2)]),
        "lru_bx": ((2, e), [(j, k + 9 + j, 0) for j in range(2)]),
        "lru_lam": ((2, e), [(j, k + 11 + j, 0) for j in range(2)]),
    }


VEC_ROWS = 24


def _mod_bwd_rep(gt, d):
    d3 = gt.shape[2]
    layout = _small_layout(d, d3 - d)
    names = list(layout)

    def body(g_ref, db_ref, loss_ref, *small_refs):
        loss_ref[...] = jnp.zeros_like(loss_ref) + jnp.sum(g_ref[0][:, d:d + LANE])
        dm0 = jnp.sum(g_ref[1], axis=0, keepdims=True)
        dmx = jnp.sum(g_ref[2], axis=0, keepdims=True)
        dm1 = jnp.sum(g_ref[3], axis=0, keepdims=True)
        db_ref[0:1, :] = dm0 + dmx
        db_ref[1:2, :] = dm1
        for ref, name in zip(small_refs, names):
            shape, places = layout[name]
            for arr_row, vec_row, col0 in places:
                total = jnp.sum(g_ref[vec_row], axis=0, keepdims=True)
                ref[arr_row:arr_row + 1, :] = total[:, col0:col0 + shape[1]]

    outs = _call(
        body, name="mod_bwd_rep", grid=(1,),
        in_specs=[pl.BlockSpec(gt.shape, lambda i: (0, 0, 0))],
        out_specs=[pl.BlockSpec((DEPTH, d3), lambda i: (0, 0)), pl.BlockSpec((1, LANE), lambda i: (0, 0))]
        + [pl.BlockSpec(layout[n][0], lambda i: (0, 0)) for n in names],
        out_shape=[jax.ShapeDtypeStruct((DEPTH, d3), F32), jax.ShapeDtypeStruct((1, LANE), F32)]
        + [jax.ShapeDtypeStruct(layout[n][0], F32) for n in names],
    )(gt)
    return outs[0], outs[1], dict(zip(names, outs[2:]))


def _cctx_partial(gt, wm0, place):
    d, c3 = wm0.shape

    def body(p_ref, dmx_ref, w_ref, o_ref):
        dmx = jnp.sum(dmx_ref[...], axis=0, keepdims=True)
        o_ref[...] = _dot_nt(jnp.broadcast_to(dmx, (2 * SUB, c3)).astype(BF16), w_ref[...])

    return _call(
        body, name="cctx_partial",
        grid_spec=pltpu.PrefetchScalarGridSpec(
            num_scalar_prefetch=1, grid=(1,),
            in_specs=[pl.BlockSpec((None, N_DEV, c3), lambda i, p: (2, 0, p[1])), pl.BlockSpec((d, c3), lambda i, p: (0, 0))],
            out_specs=pl.BlockSpec((2 * SUB, d), lambda i, p: (0, 0))),
        out_shape=jax.ShapeDtypeStruct((2 * SUB, d), F32),
    )(place, gt, wm0)


def _cctx_finish(parts, cctx):
    d = cctx.shape[1]

    def body(p_ref, cx_ref, o_ref):
        total = (p_ref[0, 0:1, :] + p_ref[1, 0:1, :]) + (p_ref[2, 0:1, :] + p_ref[3, 0:1, :])
        o_ref[...] = total * _dsilu(cx_ref[...])

    return _call(body, name="cctx_finish", grid=(1,),
                 in_specs=[pl.BlockSpec(parts.shape, lambda i: (0, 0, 0)), pl.BlockSpec((1, d), lambda i: (0, 0))],
                 out_specs=pl.BlockSpec((1, d), lambda i: (0, 0)),
                 out_shape=jax.ShapeDtypeStruct((1, d), F32))(parts, cctx)


def _inproj_fwd(xin, sc1, sh, w, name):
    rows, d = xin.shape
    ns, _, n4 = w.shape
    cpb = n4 // LANE
    tm = _row_tile(rows, 512)
    assert ns in (2, 4)

    def body(x_ref, sc_ref, sh_ref, w_ref, *o_refs):
        h = (x_ref[...] * sc_ref[...] + sh_ref[...]).astype(BF16)
        for s in range(ns):
            _put_chunks(o_refs[s // 2], _dot(h, w_ref[s]), cpb, base=(s % 2) * cpb)

    spec = pl.BlockSpec((2 * cpb, tm, LANE), lambda i: (0, i, 0))
    dtypes = (F32, BF16)[:ns // 2]
    res = _call(
        body, name=name, grid=(rows // tm,),
        in_specs=[pl.BlockSpec((tm, d), lambda i: (i, 0)),
                  pl.BlockSpec((1, d), lambda i: (0, 0)),
                  pl.BlockSpec((1, d), lambda i: (0, 0)),
                  pl.BlockSpec((ns, d, n4), lambda i: (0, 0, 0))],
        out_specs=[spec] * len(dtypes),
        out_shape=[jax.ShapeDtypeStruct((2 * cpb, rows, LANE), t) for t in dtypes],
    )(xin, sc1, sh, w)
    return res[0] if ns == 2 else tuple(res)


def _inproj_bwd_x(dparts, xin, dxres, sc1, w, name):
    rows, d = xin.shape
    npart = len(dparts)
    e = dparts[0].shape[1]
    ns, _, n4 = w.shape
    per = e // n4
    assert per * npart == ns
    tm = _row_tile(rows, 512)
    has_res = dxres is not None

    def body(*refs):
        dp = refs[:npart]
        x_ref, sc_ref, w_ref = refs[npart:npart + 3]
        rest = refs[npart + 3:]
        if has_res:
            res_ref, dx_ref, dsc_ref, dsh_ref = rest
        else:
            dsc_ref, dsh_ref = rest
        i = pl.program_id(0)
        dh = jnp.zeros((tm, d), F32)
        for p in range(npart):
            v = dp[p][...]
            for q in range(per):
                dh = dh + _dot_nt(v[:, q * n4:(q + 1) * n4], w_ref[p * per + q])

        @pl.when(i == 0)
        def _():
            dsc_ref[...] = jnp.zeros_like(dsc_ref)
            dsh_ref[...] = jnp.zeros_like(dsh_ref)

        dsc_ref[...] += jnp.sum(dh * x_ref[...], axis=0, keepdims=True)
        dsh_ref[...] += jnp.sum(dh, axis=0, keepdims=True)
        if has_res:
            dx_ref[...] = res_ref[...] + dh * sc_ref[...]

    row_spec = pl.BlockSpec((tm, d), lambda i: (i, 0))
    vec_spec = pl.BlockSpec((1, d), lambda i: (0, 0))
    in_specs = [pl.BlockSpec((tm, e), lambda i: (i, 0))] * npart + [row_spec, vec_spec,
                                                                     pl.BlockSpec((ns, d, n4), lambda i: (0, 0, 0))]
    args = list(dparts) + [xin, sc1, w]
    out_specs, out_shape = [vec_spec, vec_spec], [jax.ShapeDtypeStruct((1, d), F32)] * 2
    if has_res:
        in_specs.append(row_spec)
        args.append(dxres)
        out_specs = [row_spec] + out_specs
        out_shape = [jax.ShapeDtypeStruct((rows, d), F32)] + out_shape
    return _call(body, name=name, grid=(rows // tm,), in_specs=in_specs, out_specs=out_specs, out_shape=out_shape)(*args)


def _inproj_bwd_w(xin, sc1, sh, dparts, init, gbuf, name):
    rows, d = xin.shape
    npart = len(dparts)
    e = dparts[0].shape[1]
    n4 = e // 2
    ns = 2 * npart
    tm = _row_tile(rows, 1024)
    nt = rows // tm
    has_init = init is not None
    into = gbuf is not None
    assert not into or (ns == N_CHIPS and gbuf.shape[2] == n4)

    def body(*refs):
        x_ref, sc_ref, sh_ref = refs[:3]
        dp = refs[3:3 + npart]
        init_ref = refs[3 + npart] if has_init else None
        o_ref = refs[-1]
        s, i = pl.program_id(0), pl.program_id(1)
        h = (x_ref[...] * sc_ref[...] + sh_ref[...]).astype(BF16)

        @pl.when(i == 0)
        def _():
            o_ref[...] = init_ref[...] if has_init else jnp.zeros_like(o_ref)

        for p in range(npart):
            @pl.when(s // 2 == p)
            def _(p=p):
                o_ref[...] += _dot_tn(h, dp[p][...])

    in_specs = [pl.BlockSpec((tm, d), lambda s, i: (i, 0)),
                pl.BlockSpec((1, d), lambda s, i: (0, 0)),
                pl.BlockSpec((1, d), lambda s, i: (0, 0))]
    in_specs += [pl.BlockSpec((tm, n4), lambda s, i: (i, s % 2))] * npart
    args = [xin, sc1, sh] + list(dparts)
    o_spec = pl.BlockSpec((None, d, n4), lambda s, i: (s, 0, 0))
    if has_init:
        in_specs.append(o_spec)
        args.append(init)
    extra = {}
    if into:
        in_specs.append(ANY_SPEC)
        args.append(gbuf)
        extra = dict(input_output_aliases={len(args) - 1: 0})
    out_shape = jax.ShapeDtypeStruct(gbuf.shape if into else (ns, d, n4), F32)
    return _call(body, name=name, grid=(ns, nt), in_specs=in_specs, out_specs=o_spec, out_shape=out_shape, **extra)(*args)


def _gated(y_ref, g_ref, nch):
    return jnp.concatenate([(y_ref[k].astype(F32) * _silu(g_ref[k].astype(F32))).astype(BF16) for k in range(nch)], axis=1)


def _ln_stats(r):
    mu = jnp.mean(r, axis=-1, keepdims=True)
    var = jnp.mean(jnp.square(r - mu), axis=-1, keepdims=True)
    rstd = lax.rsqrt(var + LN_EPS)
    return (r - mu) * rstd, rstd


def _outproj_fwd(y, ug, xin, gt, wout, lg, lb, target, name):
    nch, rows, _ = y.shape
    e, d = wout.shape
    tm = _row_tile(rows, 512)
    with_loss = target is not None

    def body(*refs):
        y_ref, g_ref, x_ref, gt_ref, w_ref, lg_ref, lb_ref = refs[:7]
        if with_loss:
            t_ref, br_ref, dxo_ref, loss_ref = refs[7:]
        else:
            br_ref, xo_ref = refs[7:]
        z = _gated(y_ref, g_ref, nch)
        br_ref[...] = _dot(z, w_ref[...])
        if with_loss:
            @pl.when(pl.program_id(0) == 0)
            def _():
                loss_ref[...] = jnp.zeros_like(loss_ref)

        def norm(j, c):
            rows = pl.ds(pl.multiple_of(j * LN_ROWS, LN_ROWS), LN_ROWS)
            xhat, _ = _ln_stats(ALPHA * x_ref[rows, :] + gt_ref[...] * br_ref[rows, :])
            xo = xhat * lg_ref[...] + lb_ref[...]
            if with_loss:
                err = xo - t_ref[rows, :]
                dxo_ref[rows, :] = err * (1.0 / d)
                col = jnp.sum(err * err, axis=0, keepdims=True)
                loss_ref[...] += sum(col[:, k * LANE:(k + 1) * LANE] for k in range(d // LANE))
            else:
                xo_ref[rows, :] = xo
            return c

        lax.fori_loop(0, tm // LN_ROWS, norm, 0)

    chunk_spec = pl.BlockSpec((nch, tm, LANE), lambda i: (0, i, 0))
    g_spec = chunk_spec
    row_spec = pl.BlockSpec((tm, d), lambda i: (i, 0))
    vec_spec = pl.BlockSpec((1, d), lambda i: (0, 0))
    in_specs = [chunk_spec, g_spec, row_spec, vec_spec, pl.BlockSpec((e, d), lambda i: (0, 0)), vec_spec, vec_spec]
    args = [y, ug, xin, gt, wout, lg, lb]
    out_specs = [row_spec, row_spec]
    out_shape = [jax.ShapeDtypeStruct((rows, d), F32)] * 2
    if with_loss:
        in_specs.append(row_spec)
        args.append(target)
        out_specs.append(pl.BlockSpec((1, LANE), lambda i: (0, 0)))
        out_shape.append(jax.ShapeDtypeStruct((1, LANE), F32))
    return _call(body, name=name, grid=(rows // tm,), in_specs=in_specs, out_specs=out_specs, out_shape=out_shape)(*args)


def _outproj_bwd(dxo, xin, br, y, ug, gt, lg, wout, name, dy_dtype=F32):
    nch, rows, _ = y.shape
    e, d = wout.shape
    tm = _row_tile(rows, 256)

    def body(dxo_ref, x_ref, br_ref, y_ref, g_ref, gt_ref, lg_ref, w_ref,
             dy_ref, dg_ref, dxres_ref, dbr_ref, dlg_ref, dlb_ref, dgt_ref):
        @pl.when(pl.program_id(0) == 0)
        def _():
            dlg_ref[...] = jnp.zeros_like(dlg_ref)
            dlb_ref[...] = jnp.zeros_like(dlb_ref)
            dgt_ref[...] = jnp.zeros_like(dgt_ref)

        def norm_bwd(j, c):
            rows = pl.ds(pl.multiple_of(j * LN_ROWS, LN_ROWS), LN_ROWS)
            dxo_v = dxo_ref[rows, :]
            brv = br_ref[rows, :]
            xhat, rstd = _ln_stats(ALPHA * x_ref[rows, :] + gt_ref[...] * brv)
            dxh = dxo_v * lg_ref[...]
            dr = rstd * (dxh - jnp.mean(dxh, axis=-1, keepdims=True) - xhat * jnp.mean(dxh * xhat, axis=-1, keepdims=True))
            dlg_ref[...] += jnp.sum(dxo_v * xhat, axis=0, keepdims=True)
            dlb_ref[...] += jnp.sum(dxo_v, axis=0, keepdims=True)
            dgt_ref[...] += jnp.sum(dr * brv, axis=0, keepdims=True)
            dxres_ref[rows, :] = ALPHA * dr
            dbr_ref[rows, :] = (gt_ref[...] * dr).astype(BF16)
            return c

        lax.fori_loop(0, tm // LN_ROWS, norm_bwd, 0)
        dz = _dot_nt(dbr_ref[...], w_ref[...])
        for k in range(nch):
            dzk = dz[:, k * LANE:(k + 1) * LANE]
            gk = g_ref[k].astype(F32)
            sk = _sigmoid(gk)
            dy_ref[k] = (dzk * (gk * sk)).astype(dy_ref.dtype)
            dg_ref[:, k * LANE:(k + 1) * LANE] = (dzk * y_ref[k].astype(F32) * (sk * (1.0 + gk * (1.0 - sk)))).astype(BF16)

    chunk_spec = pl.BlockSpec((nch, tm, LANE), lambda i: (0, i, 0))
    g_spec = chunk_spec
    row_spec = pl.BlockSpec((tm, d), lambda i: (i, 0))
    vec_spec = pl.BlockSpec((1, d), lambda i: (0, 0))
    return _call(
        body, name=name, grid=(rows // tm,),
        in_specs=[row_spec, row_spec, row_spec, chunk_spec, g_spec, vec_spec, vec_spec, pl.BlockSpec((e, d), lambda i: (0, 0))],
        out_specs=[chunk_spec, pl.BlockSpec((tm, e), lambda i: (i, 0)), row_spec, row_spec, vec_spec, vec_spec, vec_spec],
        out_shape=[jax.ShapeDtypeStruct((nch, rows, LANE), dy_dtype), jax.ShapeDtypeStruct((rows, e), BF16),
                   jax.ShapeDtypeStruct((rows, d), F32), jax.ShapeDtypeStruct((rows, d), BF16)]
        + [jax.ShapeDtypeStruct((1, d), F32)] * 3,
    )(dxo, xin, br, y, ug, gt, lg, wout)


def _outproj_bwd_w(y, ug, dbr, gbuf, row0, name):
    nch, rows, _ = y.shape
    d = dbr.shape[1]
    e = nch * LANE
    es = e // N_CHIPS
    tm = _row_tile(rows, 1024)
    assert gbuf.shape[2] == d and row0 % es == 0

    def body(y_ref, g_ref, dbr_ref, buf_ref, o_ref):
        @pl.when(pl.program_id(0) == 0)
        def _():
            o_ref[...] = jnp.zeros_like(o_ref)

        z = _gated(y_ref, g_ref, nch)
        o_ref[...] += _dot_tn(z, dbr_ref[...]).reshape(N_CHIPS, es, d)

    return _call(
        body, name=name, grid=(rows // tm,),
        in_specs=[pl.BlockSpec((nch, tm, LANE), lambda i: (0, i, 0)),
                  pl.BlockSpec((nch, tm, LANE), lambda i: (0, i, 0)),
                  pl.BlockSpec((tm, d), lambda i: (i, 0)),
                  ANY_SPEC],
        out_specs=pl.BlockSpec((N_CHIPS, es, d), lambda i: (0, row0 // es, 0)),
        out_shape=jax.ShapeDtypeStruct(gbuf.shape, F32),
        input_output_aliases={3: 0},
    )(y, ug, dbr, gbuf)


def _scan(a_ref, b_ref, h_ref, *, length, init, reverse, a_shift, store):
    nblk = length // SUB
    unroll = min(SCAN_UNROLL, nblk)
    assert nblk % unroll == 0
    row = lax.broadcasted_iota(jnp.int32, (SUB, LANE), 0)
    last = 0 if reverse else SUB - 1
    edges = [(row >= SUB - k) if reverse else (row < k) for k in (1, 2, 4)]

    def local_scan(a, b):
        for k, edge in zip((1, 2, 4), edges):
            sh = (SUB - k) if reverse else k
            b = b + a * jnp.where(edge, 0.0, pltpu.roll(b, sh, 0))
            a = a * jnp.where(edge, 1.0, pltpu.roll(a, sh, 0))
        return a, b

    def step(i, carry):
        base = pl.multiple_of(((nblk // unroll - 1 - i) if reverse else i) * (unroll * SUB), unroll * SUB)
        order = range(unroll - 1, -1, -1) if reverse else range(unroll)
        loaded = [(a_ref[pl.ds(PAD + base + j * SUB + a_shift, SUB), :], b_ref[pl.ds(PAD + base + j * SUB, SUB), :])
                  for j in order]
        scanned = [local_scan(a, b) for a, b in loaded]
        for j, (a, b) in zip(order, scanned):
            if store:
                h_ref[pl.ds(PAD + base + j * SUB, SUB), :] = b + a * carry
            a_l = jnp.broadcast_to(a[last:last + 1, :], (SUB, LANE))
            b_l = jnp.broadcast_to(b[last:last + 1, :], (SUB, LANE))
            carry = b_l + a_l * carry
        return carry

    carry = lax.fori_loop(0, nblk // unroll, step, jnp.broadcast_to(init, (SUB, LANE)))
    return carry[0:1, :]


def _conv_fwd(src_ref, upad, u_ref, cw, cb, length):
    zeros = jnp.zeros((PAD, LANE), F32)
    upad[pl.ds(0, PAD), :] = zeros
    upad[pl.ds(PAD + length, PAD), :] = zeros
    rt = _row_tile(length, ROW_TILE)

    def copy(i, c):
        t0 = pl.multiple_of(i * rt, rt)
        upad[pl.ds(PAD + t0, rt), :] = src_ref[pl.ds(t0, rt), :]
        return c

    lax.fori_loop(0, length // rt, copy, 0)

    def tile(i, c):
        t0 = pl.multiple_of(i * rt, rt)
        acc = jnp.zeros((rt, LANE), F32)
        for k in range(CONV_TAPS):
            acc = acc + upad[pl.ds(t0 + PAD - CONV_LEFT + k, rt), :] * cw[k:k + 1, :]
        u_ref[pl.ds(t0, rt), :] = acc + cb
        return c

    lax.fori_loop(0, length // rt, tile, 0)


def _gates_fwd(u_ref, a_ref, b_ref, wa, wx, ba, bx, ls, length, keep=None):
    rt = _row_tile(length, GATE_TILE)
    ls_c = LRU_C * ls

    def tile(i, c):
        t0 = pl.multiple_of(i * rt, rt)
        ut = u_ref[pl.ds(t0, rt), :]
        ub = ut.astype(BF16)
        r = 0.5 * (jnp.tanh(_dot(ub, wa) + ba) + 1.0)
        ig = 0.5 * (jnp.tanh(_dot(ub, wx) + bx) + 1.0)
        if keep is not None:
            keep[0][pl.ds(t0, rt), :] = r
            keep[1][pl.ds(t0, rt), :] = ig
        la = r * ls_c
        a = jnp.exp(la)
        a_ref[pl.ds(PAD + t0, rt), :] = a
        q = _one_minus_sq(la, a)
        b_ref[pl.ds(PAD + t0, rt), :] = jnp.where(q > 0.0, q * lax.rsqrt(q), 0.0) * (ig * ut)
        return c

    lax.fori_loop(0, length // rt, tile, 0, unroll=min(GATE_UNROLL, length // rt))


def _lru_specs():
    return [pl.BlockSpec((CONV_TAPS, LANE), lambda n: (0, n)),
            pl.BlockSpec((1, LANE), lambda n: (0, n)),
            pl.BlockSpec((2, None, LANE, LANE), lambda n: (0, n, 0, 0)),
            pl.BlockSpec((2, None, LANE, LANE), lambda n: (0, n, 0, 0)),
            pl.BlockSpec((2, LANE), lambda n: (0, n)),
            pl.BlockSpec((2, LANE), lambda n: (0, n)),
            pl.BlockSpec((2, LANE), lambda n: (0, n))]


def _rglru_fwd(ug, uc, conv_w, conv_b, wa, wx, ba, bx, lam):
    nb = uc.shape[0]
    s_len, t_len = ug.shape[1], uc.shape[1]

    def body(u0_ref, uc0_ref, cw_ref, cb_ref, wa_ref, wx_ref, ba_ref, bx_ref, lam_ref, y_ref,
             upad, ubuf, abuf, hbuf):
        cw, cb = cw_ref[...], cb_ref[...]
        lsig = _log_sigmoid(lam_ref[...])
        zero = jnp.zeros((1, LANE), F32)
        _conv_fwd(uc0_ref, upad, ubuf, cw, cb, t_len)
        h0 = []
        for dr in range(2):
            _gates_fwd(ubuf, abuf, hbuf, wa_ref[dr], wx_ref[dr], ba_ref[dr:dr + 1, :], bx_ref[dr:dr + 1, :],
                       lsig[dr:dr + 1, :], t_len)
            h0.append(_scan(abuf, hbuf, hbuf, length=t_len, init=zero, reverse=(dr == 1), a_shift=0, store=False))
        _conv_fwd(u0_ref, upad, ubuf, cw, cb, s_len)
        rt = _row_tile(s_len, ROW_TILE)
        for dr in range(2):
            _gates_fwd(ubuf, abuf, hbuf, wa_ref[dr], wx_ref[dr], ba_ref[dr:dr + 1, :], bx_ref[dr:dr + 1, :],
                       lsig[dr:dr + 1, :], s_len)
            _scan(abuf, hbuf, hbuf, length=s_len, init=h0[dr], reverse=(dr == 1), a_shift=0, store=True)

            def acc(i, c, dr=dr):
                t0 = pl.multiple_of(i * rt, rt)
                h = hbuf[pl.ds(PAD + t0, rt), :]
                if dr == 0:
                    upad[pl.ds(PAD + t0, rt), :] = h
                else:
                    y_ref[pl.ds(t0, rt), :] = (upad[pl.ds(PAD + t0, rt), :] + h).astype(y_ref.dtype)
                return c

            lax.fori_loop(0, s_len // rt, acc, 0)

    seq = pltpu.VMEM((s_len + 2 * PAD, LANE), F32)
    return _call(
        body, name="rglru_fwd", grid=(nb,),
        in_specs=[pl.BlockSpec((None, s_len, LANE), lambda n: (n, 0, 0)),
                  pl.BlockSpec((None, t_len, LANE), lambda n: (n, 0, 0))] + _lru_specs(),
        out_specs=pl.BlockSpec((None, s_len, LANE), lambda n: (n, 0, 0)),
        out_shape=jax.ShapeDtypeStruct((nb, s_len, LANE), BF16),
        scratch_shapes=[seq, pltpu.VMEM((s_len, LANE), F32), seq, seq],
    )(ug, uc, conv_w, conv_b, wa, wx, ba, bx, lam)


def _rglru_bwd(ug, uc, dy, conv_w, conv_b, wa, wx, ba, bx, lam):
    nb = uc.shape[0]
    e = nb * LANE
    s_len, t_len = ug.shape[1], uc.shape[1]

    def body(u0_ref, uc0_ref, dy_ref, cw_ref, cb_ref, wa_ref, wx_ref, ba_ref, bx_ref, lam_ref,
             du_ref, duc_ref, dcw_ref, dcb_ref, dwa_ref, dwx_ref, dba_ref, dbx_ref, dlam_ref,
             upad, ubuf, abuf, hbuf, lbuf, dubuf, rbuf, ibuf, cpad, cu, ca0, ch0, ca1, ch1, cr0, ci0, cr1, ci1):
        cw, cb = cw_ref[...], cb_ref[...]
        lam_v = lam_ref[...]
        lsig = _log_sigmoid(lam_v)
        zero = jnp.zeros((1, LANE), F32)
        zpad = jnp.zeros((PAD, LANE), F32)
        for ref in (dcw_ref, dcb_ref, dwa_ref, dwx_ref, dba_ref, dbx_ref, dlam_ref):
            ref[...] = jnp.zeros_like(ref)

        def params(dr):
            return (wa_ref[dr], wx_ref[dr], ba_ref[dr:dr + 1, :], bx_ref[dr:dr + 1, :], lsig[dr:dr + 1, :])

        def direction_bwd(dr, u_ref, a_ref, h_ref, l_ref, gates, dub, length, first):
            wa_d, wx_d, ba_d, bx_d, ls_d = params(dr)
            rt = _row_tile(length, GATE_BWD_TILE)
            prev = 1 if dr == 1 else -1

            def tile(i, c):
                t0 = pl.multiple_of(i * rt, rt)
                ut = u_ref[pl.ds(t0, rt), :]
                ub = ut.astype(BF16)
                r = gates[0][pl.ds(t0, rt), :]
                ig = gates[1][pl.ds(t0, rt), :]
                la = r * (LRU_C * ls_d)
                a = a_ref[pl.ds(PAD + t0, rt), :]
                q = _one_minus_sq(la, a)
                rs = lax.rsqrt(q)
                sq = q * rs
                lm = l_ref[pl.ds(PAD + t0, rt), :]
                da = lm * h_ref[pl.ds(PAD + t0 + prev, rt), :]
                dsq = lm * ig * ut
                dig = lm * sq * ut
                dla = da * a - dsq * (a * a) * rs
                dr_ = dla * (LRU_C * ls_d)
                dlam_ref[dr:dr + 1, :] += jnp.sum(dla * (LRU_C * r), axis=0, keepdims=True)
                dpr = dr_ * r * (1.0 - r)
                dpi = dig * ig * (1.0 - ig)
                dba_ref[dr:dr + 1, :] += jnp.sum(dpr, axis=0, keepdims=True)
                dbx_ref[dr:dr + 1, :] += jnp.sum(dpi, axis=0, keepdims=True)
                dprb, dpib = dpr.astype(BF16), dpi.astype(BF16)
                dwa_ref[dr] += _dot_tn(ub, dprb)
                dwx_ref[dr] += _dot_tn(ub, dpib)
                dut = lm * sq * ig + 2.0 * (_dot_nt(dprb, wa_d) + _dot_nt(dpib, wx_d))
                if first:
                    dub[pl.ds(PAD + t0, rt), :] = dut
                else:
                    dub[pl.ds(PAD + t0, rt), :] += dut
                return c

            lax.fori_loop(0, length // rt, tile, 0, unroll=min(GATE_UNROLL, length // rt))

        def conv_bwd(dub, src_pad, out_ref, length):
            rt = _row_tile(length, ROW_TILE)

            def tile(i, c):
                t0 = pl.multiple_of(i * rt, rt)
                dut = dub[pl.ds(PAD + t0, rt), :]
                dcb_ref[...] += jnp.sum(dut, axis=0, keepdims=True)
                acc = jnp.zeros((rt, LANE), F32)
                for k in range(CONV_TAPS):
                    sh = CONV_LEFT - k
                    acc = acc + dub[pl.ds(PAD + t0 + sh, rt), :] * cw[k:k + 1, :]
                    dcw_ref[k:k + 1, :] += jnp.sum(dut * src_pad[pl.ds(PAD + t0 - sh, rt), :], axis=0, keepdims=True)
                out_ref[pl.ds(t0, rt), :] = acc.astype(out_ref.dtype)
                return c

            lax.fori_loop(0, length // rt, tile, 0)

        _conv_fwd(uc0_ref, cpad, cu, cw, cb, t_len)
        cbufs = ((ca0, ch0), (ca1, ch1))
        cgates = ((cr0, ci0), (cr1, ci1))
        h0 = []
        for dr in range(2):
            ca, chh = cbufs[dr]
            _gates_fwd(cu, ca, chh, *params(dr), t_len, keep=cgates[dr])
            h0.append(_scan(ca, chh, chh, length=t_len, init=zero, reverse=(dr == 1), a_shift=0, store=True))
        _conv_fwd(u0_ref, upad, ubuf, cw, cb, s_len)
        rt = _row_tile(s_len, ROW_TILE)
        dh0 = []
        for dr in range(2):
            rev = dr == 1
            _gates_fwd(ubuf, abuf, hbuf, *params(dr), s_len, keep=(rbuf, ibuf))
            _scan(abuf, hbuf, hbuf, length=s_len, init=h0[dr], reverse=rev, a_shift=0, store=True)
            first_row = PAD + s_len if rev else PAD - 1
            hbuf[pl.ds(first_row, 1), :] = h0[dr]
            end_row = PAD - 1 if rev else PAD + s_len
            abuf[pl.ds(end_row, 1), :] = zero

            def copy(i, c):
                t0 = pl.multiple_of(i * rt, rt)
                lbuf[pl.ds(PAD + t0, rt), :] = dy_ref[pl.ds(t0, rt), :]
                return c

            lax.fori_loop(0, s_len // rt, copy, 0)
            _scan(abuf, lbuf, lbuf, length=s_len, init=zero, reverse=not rev, a_shift=(-1 if rev else 1), store=True)
            start = PAD + s_len - 1 if rev else PAD
            dh0.append(abuf[pl.ds(start, 1), :] * lbuf[pl.ds(start, 1), :])
            direction_bwd(dr, ubuf, abuf, hbuf, lbuf, (rbuf, ibuf), dubuf, s_len, first=(dr == 0))
        dubuf[pl.ds(0, PAD), :] = zpad
        dubuf[pl.ds(PAD + s_len, PAD), :] = zpad
        conv_bwd(dubuf, upad, du_ref, s_len)
        lc = lbuf
        duc_buf = dubuf
        for dr in range(2):
            rev = dr == 1
            ca, chh = cbufs[dr]
            first_row = PAD + t_len if rev else PAD - 1
            chh[pl.ds(first_row, 1), :] = zero
            end_row = PAD - 1 if rev else PAD + t_len
            ca[pl.ds(end_row, 1), :] = zero + 1.0
            rtc = _row_tile(t_len, ROW_TILE)

            def clear(i, c):
                t0 = pl.multiple_of(i * rtc, rtc)
                lc[pl.ds(PAD + t0, rtc), :] = jnp.zeros((rtc, LANE), F32)
                return c

            lax.fori_loop(0, t_len // rtc, clear, 0)
            _scan(ca, lc, lc, length=t_len, init=dh0[dr], reverse=not rev, a_shift=(-1 if rev else 1), store=True)
            direction_bwd(dr, cu, ca, chh, lc, cgates[dr], duc_buf, t_len, first=(dr == 0))
        duc_buf[pl.ds(0, PAD), :] = zpad
        duc_buf[pl.ds(PAD + t_len, PAD), :] = zpad
        conv_bwd(duc_buf, cpad, duc_ref, t_len)
        dlam_ref[...] = dlam_ref[...] * (1.0 - _sigmoid(lam_v))

    seq = pltpu.VMEM((s_len + 2 * PAD, LANE), F32)
    cseq = pltpu.VMEM((t_len + 2 * PAD, LANE), F32)
    flat = pltpu.VMEM((s_len, LANE), F32)
    cflat = pltpu.VMEM((t_len, LANE), F32)
    vec2 = pl.BlockSpec((2, LANE), lambda n: (0, n))
    wspec = pl.BlockSpec((2, None, LANE, LANE), lambda n: (0, n, 0, 0))
    return _call(
        body, name="rglru_bwd", grid=(nb,),
        in_specs=[pl.BlockSpec((None, s_len, LANE), lambda n: (n, 0, 0)),
                  pl.BlockSpec((None, t_len, LANE), lambda n: (n, 0, 0)),
                  pl.BlockSpec((None, s_len, LANE), lambda n: (n, 0, 0))] + _lru_specs(),
        out_specs=[pl.BlockSpec((s_len, LANE), lambda n: (0, n)),
                   pl.BlockSpec((t_len, LANE), lambda n: (0, n)),
                   pl.BlockSpec((CONV_TAPS, LANE), lambda n: (0, n)),
                   pl.BlockSpec((1, LANE), lambda n: (0, n)),
                   wspec, wspec, vec2, vec2, vec2],
        out_shape=[jax.ShapeDtypeStruct((s_len, e), BF16), jax.ShapeDtypeStruct((t_len, e), BF16),
                   jax.ShapeDtypeStruct((CONV_TAPS, e), F32), jax.ShapeDtypeStruct((1, e), F32),
                   jax.ShapeDtypeStruct((2, nb, LANE, LANE), F32), jax.ShapeDtypeStruct((2, nb, LANE, LANE), F32),
                   jax.ShapeDtypeStruct((2, e), F32), jax.ShapeDtypeStruct((2, e), F32), jax.ShapeDtypeStruct((2, e), F32)],
        scratch_shapes=[seq, flat, seq, seq, seq, seq, flat, flat,
                        cseq, cflat, cseq, cseq, cseq, cseq, cflat, cflat, cflat, cflat],
    )(ug, uc, dy, conv_w, conv_b, wa, wx, ba, bx, lam)


def _pool_windows(src_ref, out_ref, colbuf, rowbuf, half, transpose, s_len):
    gw = GRID_W
    lg = gw.bit_length() - 1
    n_rows = s_len // gw
    cp, rm = POOL_CPAD, 8 * gw
    stride = gw + 2 * cp
    rt = _row_tile(s_len, ROW_TILE)
    assert rt % gw == 0 and half <= cp
    gpt = rt // gw
    offs = range(-half, half)
    zmargin = jnp.zeros((cp, LANE), F32)

    def zcol(r, c):
        base = pl.multiple_of(r * stride, SUB)
        colbuf[pl.ds(base, cp), :] = zmargin
        colbuf[pl.ds(base + cp + gw, cp), :] = zmargin
        return c

    lax.fori_loop(0, n_rows, zcol, 0)

    def zrow(i, c):
        t0 = pl.multiple_of(i * gw, gw)
        rowbuf[pl.ds(t0, gw), :] = jnp.zeros((gw, LANE), F32)
        rowbuf[pl.ds(rm + s_len + t0, gw), :] = jnp.zeros((gw, LANE), F32)
        return c

    lax.fori_loop(0, rm // gw, zrow, 0)

    col = lax.broadcasted_iota(jnp.int32, (gw, LANE), 0)
    ccnt = (jnp.minimum(col + half, gw) - jnp.maximum(col - half, 0)).astype(F32)

    def row_counts(t0):
        row = (t0 + lax.broadcasted_iota(jnp.int32, (rt, LANE), 0)) >> lg
        return (jnp.minimum(row + half, n_rows) - jnp.maximum(row - half, 0)).astype(F32)

    def col_base(t0, g):
        return pl.multiple_of((t0 // gw) * stride, SUB) + g * stride + cp

    def col_sum(t0, g, sign):
        acc = jnp.zeros((gw, LANE), F32)
        for o in offs:
            acc = acc + colbuf[pl.ds(col_base(t0, g) + sign * o, gw), :]
        return acc

    def row_sum(t0, sign):
        acc = jnp.zeros((rt, LANE), F32)
        for o in offs:
            acc = acc + rowbuf[pl.ds(rm + t0 + sign * o * gw, rt), :]
        return acc

    n_tiles = s_len // rt
    assert rt >= half * gw

    def loop(fn, edges=False):
        def step(i, c):
            t0 = pl.multiple_of(i * rt, rt)
            fn(t0, False) if edges else fn(t0)
            return c
        if edges:
            fn(0, True)
            if n_tiles > 1:
                fn(s_len - rt, True)
            lax.fori_loop(1, n_tiles - 1, step, 0)
        else:
            lax.fori_loop(0, n_tiles, step, 0)

    inv_ccnt = 1.0 / ccnt

    def by_row_count(v, t0, edge):
        return v / row_counts(t0) if edge else v * (1.0 / (2 * half))

    if not transpose:
        def fill(t0):
            for g in range(gpt):
                colbuf[pl.ds(col_base(t0, g), gw), :] = src_ref[pl.ds(t0 + g * gw, gw), :]

        def cols(t0):
            for g in range(gpt):
                rowbuf[pl.ds(rm + t0 + g * gw, gw), :] = col_sum(t0, g, 1) * inv_ccnt

        def rows(t0, edge):
            mean = by_row_count(row_sum(t0, 1), t0, edge)
            out_ref[pl.ds(t0, rt), :] = (mean - src_ref[pl.ds(t0, rt), :]).astype(out_ref.dtype)

        loop(fill)
        loop(cols)
        loop(rows, edges=True)
    else:
        def fill(t0, edge):
            rowbuf[pl.ds(rm + t0, rt), :] = by_row_count(src_ref[pl.ds(t0, rt), :], t0, edge)

        def rows(t0):
            acc = row_sum(t0, -1)
            for g in range(gpt):
                colbuf[pl.ds(col_base(t0, g), gw), :] = acc[g * gw:(g + 1) * gw, :] * inv_ccnt

        def cols(t0):
            for g in range(gpt):
                rows_g = pl.ds(t0 + g * gw, gw)
                out_ref[rows_g, :] = (col_sum(t0, g, -1) - src_ref[rows_g, :]).astype(out_ref.dtype)

        loop(fill, edges=True)
        loop(rows)
        loop(cols)


def _pool_map(src, nb, transpose, out_chunk_major, name):
    s_len = src.shape[1]
    cpg = nb // len(POOL_WINDOWS)

    def body(src_ref, out_ref, colbuf, rowbuf):
        n = pl.program_id(0)
        for gi, w in enumerate(POOL_WINDOWS):
            @pl.when(n // cpg == gi)
            def _(w=w):
                _pool_windows(src_ref, out_ref, colbuf, rowbuf, w // 2, transpose, s_len)

    if out_chunk_major:
        out_spec = pl.BlockSpec((None, s_len, LANE), lambda n: (n, 0, 0))
        out_shape = jax.ShapeDtypeStruct((nb, s_len, LANE), BF16)
    else:
        out_spec = pl.BlockSpec((s_len, LANE), lambda n: (0, n))
        out_shape = jax.ShapeDtypeStruct((s_len, nb * LANE), BF16)
    return _call(
        body, name=name, grid=(nb,),
        in_specs=[pl.BlockSpec((None, s_len, LANE), lambda n: (n, 0, 0))],
        out_specs=out_spec, out_shape=out_shape,
        scratch_shapes=[pltpu.VMEM((s_len // GRID_W * (GRID_W + 2 * POOL_CPAD), LANE), F32),
                        pltpu.VMEM((s_len + 16 * GRID_W, LANE), F32)],
    )(src)


def _group_weight(w_ref):
    return jnp.concatenate([w_ref[k] for k in range(N_CHIPS)], axis=0)


def _pool_mm_fwd(dm, wp, scale):
    nb, rows, _ = dm.shape
    _, ng, pq, pg = wp.shape
    cpg = pg // LANE
    tm = _row_tile(rows, 2048)

    def body(d_ref, w_ref, s_ref, y_ref):
        _put_chunks(y_ref, _dot(_cat(d_ref, cpg), _group_weight(w_ref)) * s_ref[...], cpg)

    cspec = pl.BlockSpec((cpg, tm, LANE), lambda i, g: (g, i, 0))
    return _call(
        body, name="pool_mm_fwd", grid=(rows // tm, ng),
        in_specs=[cspec, pl.BlockSpec((N_CHIPS, None, pq, pg), lambda i, g: (0, g, 0, 0)),
                  pl.BlockSpec((1, pg), lambda i, g: (0, g))],
        out_specs=cspec, out_shape=jax.ShapeDtypeStruct((nb, rows, LANE), BF16),
    )(dm, wp, scale)


def _pool_mm_bwd(dy, dm, wp, scale, gbuf, row0):
    nb, rows, _ = dm.shape
    _, ng, pq, pg = wp.shape
    cpg = pg // LANE
    tm = _row_tile(rows, 1024)
    nt = rows // tm
    assert gbuf.shape[2] == 2 * pg and row0 % pq == 0

    def body(dy_ref, d_ref, w_ref, s_ref, buf_ref, dd_ref, dwp_ref, dsc_ref, acc):
        i = pl.program_id(1)

        @pl.when(i == 0)
        def _():
            acc[...] = jnp.zeros_like(acc)
            dsc_ref[...] = jnp.zeros_like(dsc_ref)

        dyv = _cat(dy_ref, cpg).astype(F32)
        dc = _cat(d_ref, cpg)
        w = _group_weight(w_ref)
        dsc_ref[...] += jnp.sum(dyv * _dot(dc, w), axis=0, keepdims=True)
        dyp = (dyv * s_ref[...]).astype(BF16)
        _put_chunks(dd_ref, _dot_nt(dyp, w), cpg)
        acc[...] += _dot_tn(dc, dyp)

        @pl.when(i == nt - 1)
        def _():
            dwp_ref[...] = acc[...].reshape(N_CHIPS, pq, pg)

    cspec = pl.BlockSpec((cpg, tm, LANE), lambda g, i: (g, i, 0))
    sspec = pl.BlockSpec((1, pg), lambda g, i: (0, g))
    return _call(
        body, name="pool_mm_bwd", grid=(ng, nt),
        in_specs=[cspec, cspec, pl.BlockSpec((N_CHIPS, None, pq, pg), lambda g, i: (0, g, 0, 0)), sspec, ANY_SPEC],
        out_specs=[cspec, pl.BlockSpec((N_CHIPS, pq, pg), lambda g, i: (0, row0 // pq + g // 2, g % 2)), sspec],
        out_shape=[jax.ShapeDtypeStruct((nb, rows, LANE), F32), jax.ShapeDtypeStruct(gbuf.shape, F32),
                   jax.ShapeDtypeStruct((1, ng * pg), F32)],
        scratch_shapes=[pltpu.VMEM((pg, pg), F32)],
        input_output_aliases={4: 1},
    )(dy, dm, wp, scale, gbuf)


def _adamw_math(w, g, m, v):
    nm = ADAM_B1 * m + (1.0 - ADAM_B1) * g
    nv = ADAM_B2 * v + (1.0 - ADAM_B2) * jnp.square(g)
    m_hat = nm / (1.0 - ADAM_B1 ** ADAM_STEP)
    v_hat = nv / (1.0 - ADAM_B2 ** ADAM_STEP)
    return -ADAM_LR * (m_hat / (jnp.sqrt(v_hat) + ADAM_EPS) + ADAM_WD * w), nm, nv


def _adamw_param(w3, m3, v3, gsrcs, pick, tm, name, after=None):
    n_blk, rows, cols = w3.shape
    ng = len(gsrcs)

    def body(*refs):
        w_ref, m_ref, v_ref = refs[:3]
        g_refs = refs[3:3 + ng]
        go_ref, d_ref, nm_ref, nv_ref = refs[-4:]
        g = pick(pl.program_id(0), [r[...] for r in g_refs])
        go_ref[...] = g
        d_ref[...], nm_ref[...], nv_ref[...] = _adamw_math(w_ref[...], g, m_ref[...], v_ref[...])

    spec = pl.BlockSpec((None, tm, cols), lambda n, i: (n, i, 0))
    extra = [] if after is None else [after]
    return _call(
        body, name=name, grid=(n_blk, rows // tm),
        in_specs=[spec] * 3 + [pl.BlockSpec(shape, imap) for _, shape, imap in gsrcs] + [ANY_SPEC] * len(extra),
        out_specs=[spec] * 4, out_shape=[jax.ShapeDtypeStruct(w3.shape, F32)] * 4,
    )(w3, m3, v3, *[a for a, _, _ in gsrcs], *extra)


def _adamw_small(quads):
    n = len(quads)

    def body(*refs):
        ins, outs = refs[:4 * n], refs[4 * n:]
        for k in range(n):
            w, g, m, v = (r[...] for r in ins[4 * k:4 * k + 4])
            outs[3 * k][...], outs[3 * k + 1][...], outs[3 * k + 2][...] = _adamw_math(w, g, m, v)

    flat = [a for q in quads for a in q]
    res = _call(body, name="adamw_small", grid=(1,),
                in_specs=[pl.BlockSpec(a.shape, lambda i: (0, 0)) for a in flat],
                out_specs=[pl.BlockSpec(q[0].shape, lambda i: (0, 0)) for q in quads for _ in range(3)],
                out_shape=[jax.ShapeDtypeStruct(q[0].shape, F32) for q in quads for _ in range(3)])(*flat)
    return [tuple(res[3 * k:3 * k + 3]) for k in range(n)]


def _place():
    return lax.axis_index("x"), lax.axis_index("y"), lax.axis_index("c")


def _other_chips(x, y):
    return [(1 - x, y), (x, 1 - y), (1 - x, 1 - y)]


def _own_slab(a, devices=False):
    x, y, c = _place()
    n, me = (N_DEV, 4 * x + 2 * y + c) if devices else (N_CHIPS, 2 * x + y)
    return lax.dynamic_update_slice(lax.empty((n,) + a.shape, a.dtype), a[None], (me, 0, 0))


def _gather_chips(arrays, name):
    n = len(arrays)
    halves = [a.shape[0] // 2 for a in arrays]
    for a, h in zip(arrays, halves):
        assert 2 * h == a.shape[0] and h % (32 // a.dtype.itemsize) == 0
    lands = [_own_slab(a) for a in arrays]

    def body(*refs):
        outs = refs[n:2 * n]
        send_sems, recv_sems = refs[2 * n:]
        x, y, c = _place()
        me = 2 * x + y
        chips = _other_chips(x, y)

        def mine(k):
            return pl.ds(c * halves[k], halves[k])

        def theirs(k):
            return pl.ds((1 - c) * halves[k], halves[k])

        def push(k, j, src, dst, to):
            return pltpu.make_async_remote_copy(src_ref=src, dst_ref=dst, send_sem=send_sems.at[6 * k + j],
                                                recv_sem=recv_sems.at[6 * k + j], device_id=to, device_id_type=MESH)

        started = []
        for j, (cx, cy) in enumerate(chips):
            for k in range(n):
                own = outs[k].at[me, mine(k)]
                cp = push(k, j, own, own, (cx, cy, c))
                cp.start()
                started.append(cp)
        for j, (cx, cy) in enumerate(chips):
            for k in range(n):
                slab = outs[k].at[2 * cx + cy, mine(k)]
                push(k, j, slab, slab, (x, y, c)).wait_recv()
                fwd = push(k, 3 + j, slab, slab, (x, y, 1 - c))
                fwd.start()
                started.append(fwd)
        for j, (cx, cy) in enumerate(chips):
            for k in range(n):
                slab = outs[k].at[2 * cx + cy, theirs(k)]
                push(k, 3 + j, slab, slab, (x, y, c)).wait_recv()
        for cp in started:
            cp.wait_send()

    return _call(
        body, name=name, in_specs=[ANY_SPEC] * n, out_specs=[ANY_SPEC] * n,
        out_shape=[jax.ShapeDtypeStruct(a.shape, a.dtype) for a in lands],
        input_output_aliases={k: k for k in range(n)},
        scratch_shapes=[pltpu.SemaphoreType.DMA((6 * n,)), pltpu.SemaphoreType.DMA((6 * n,))],
    )(*lands)


HBM_SPEC = pl.BlockSpec(memory_space=pltpu.HBM)
SEM_SPEC = pl.BlockSpec(memory_space=pltpu.SEMAPHORE)
SIDE_EFFECT = pltpu.SideEffectType.DATAFLOW_SIDE_EFFECTING


def _n_peers(kind):
    return N_DEV - 1 if kind == "devices" else N_CHIPS - 1


def _push_copies(src_refs, land_refs, send_sems, recv_sems, kind):
    x, y, c = _place()
    if kind == "devices":
        me = 4 * x + 2 * y + c
        peers = [((me + j) % N_DEV, None) for j in range(1, N_DEV)]
        peers = [((to // 4, (to // 2) % 2, to % 2), None) for to, _ in peers]
    else:
        me = 2 * x + y
        peers = [((cx, cy, c), 2 * cx + cy) for cx, cy in _other_chips(x, y)]
    n = len(peers)
    copies = []
    for j, (dev, slab) in enumerate(peers):
        for k, (src, land) in enumerate(zip(src_refs, land_refs)):
            copies.append(pltpu.make_async_remote_copy(
                src_ref=src.at[slab] if kind == "slab" else src, dst_ref=land.at[me], send_sem=send_sems.at[n * k + j],
                recv_sem=recv_sems.at[n * k + j], device_id=dev, device_id_type=MESH))
    return copies


def _push_start(srcs, lands, kind, after, name):
    n = len(srcs)

    def body(*refs):
        src_refs, land_refs = refs[:n], refs[n:2 * n]
        send_sems, recv_sems = refs[2 * n + 1], refs[2 * n + 2]
        token = refs[-1]
        for cp in _push_copies(src_refs, land_refs, send_sems, recv_sems, kind):
            cp.start()
        token[...] = jnp.zeros_like(token)

    bufs = [pltpu.with_memory_space_constraint(a, pltpu.HBM) for a in list(srcs) + list(lands)]
    res = _call(
        body, name=name,
        out_shape=[pltpu.SemaphoreType.DMA((_n_peers(kind) * n,)), pltpu.SemaphoreType.DMA((_n_peers(kind) * n,))]
        + [pltpu.HBM(a.shape, a.dtype) for a in bufs] + [jax.ShapeDtypeStruct((SUB, LANE), F32)],
        in_specs=[HBM_SPEC] * (2 * n) + [ANY_SPEC],
        out_specs=[SEM_SPEC, SEM_SPEC] + [HBM_SPEC] * (2 * n) + [pl.BlockSpec(memory_space=pltpu.VMEM)],
        input_output_aliases={i: 2 + i for i in range(2 * n)},
        compiler_params=pltpu.CompilerParams(has_side_effects=SIDE_EFFECT),
    )(*bufs, after)
    return res[0], res[1], list(res[2:2 + n]), list(res[2 + n:2 + 2 * n]), res[-1]


def _push_wait(send_sems, recv_sems, srcs, lands, kind, after, name):
    n = len(srcs)

    def body(*refs):
        src_refs, land_refs = refs[:n], refs[n:2 * n]
        send_sems, recv_sems = refs[2 * n], refs[2 * n + 1]
        for cp in _push_copies(src_refs, land_refs, send_sems, recv_sems, kind):
            cp.wait_send()
            cp.wait_recv()

    res = _call(
        body, name=name,
        out_shape=[pltpu.HBM(a.shape, a.dtype) for a in list(srcs) + list(lands)],
        in_specs=[HBM_SPEC] * (2 * n) + [SEM_SPEC, SEM_SPEC, ANY_SPEC],
        out_specs=[HBM_SPEC] * (2 * n),
        input_output_aliases={i: i for i in range(2 * n)},
        compiler_params=pltpu.CompilerParams(has_side_effects=SIDE_EFFECT),
    )(*srcs, *lands, send_sems, recv_sems, after)
    return list(res[n:])


def _sibling_swap(g):
    _, rows, w = g.shape
    half = rows // 2

    def body(g_ref, out_ref, send_sem, recv_sem):
        x, y, c = _place()
        cp = pltpu.make_async_remote_copy(src_ref=g_ref.at[:, pl.ds((1 - c) * half, half)], dst_ref=out_ref,
                                          send_sem=send_sem, recv_sem=recv_sem, device_id=(x, y, 1 - c), device_id_type=MESH)
        cp.start()
        cp.wait()

    return _call(body, name="rs_sibling_swap", in_specs=[ANY_SPEC], out_specs=ANY_SPEC,
                 out_shape=jax.ShapeDtypeStruct((N_CHIPS, half, w), F32),
                 scratch_shapes=[pltpu.SemaphoreType.DMA, pltpu.SemaphoreType.DMA])(g)


def _pair_add(g, got, place):
    _, rows, w = g.shape
    half = rows // 2
    tm = _row_tile(half, RS_TILE)
    nt = half // tm

    def body(p_ref, a_ref, b_ref, o_ref, own_ref):
        v = a_ref[...] + b_ref[...]
        o_ref[...] = v.astype(BF16)

        @pl.when(pl.program_id(1) == p_ref[1])
        def _():
            own_ref[...] = v

    return _call(
        body, name="rs_pair_add",
        grid_spec=pltpu.PrefetchScalarGridSpec(
            num_scalar_prefetch=1, grid=(nt, N_CHIPS),
            in_specs=[pl.BlockSpec((None, tm, w), lambda i, s, p: (s, p[0] * nt + i, 0)),
                      pl.BlockSpec((None, tm, w), lambda i, s, p: (s, i, 0))],
            out_specs=[pl.BlockSpec((None, tm, w), lambda i, s, p: (s, i, 0)),
                       pl.BlockSpec((tm, w), lambda i, s, p: (i, 0))]),
        out_shape=[jax.ShapeDtypeStruct((N_CHIPS, half, w), BF16), jax.ShapeDtypeStruct((half, w), F32)],
    )(place, g, got)


def _sum_chips(parts, own, place):
    _, half, w = parts.shape
    tm = _row_tile(half, RS_TILE)
    nt = half // tm

    def body(p_ref, parts_ref, own_ref, o_ref):
        me = p_ref[1]
        t = [jnp.where(me == q, own_ref[...], parts_ref[q].astype(F32)) for q in range(N_CHIPS)]
        o_ref[...] = (t[0] + t[1]) + (t[2] + t[3])

    return _call(
        body, name="rs_sum_chips",
        grid_spec=pltpu.PrefetchScalarGridSpec(
            num_scalar_prefetch=1, grid=(nt,),
            in_specs=[pl.BlockSpec((N_CHIPS, tm, w), lambda i, p: (0, i, 0)), pl.BlockSpec((tm, w), lambda i, p: (i, 0))],
            out_specs=pl.BlockSpec((tm, w), lambda i, p: (p[0] * nt + i, 0))),
        out_shape=jax.ShapeDtypeStruct((2 * half, w), F32),
    )(place, parts, own)


def _sibling_gather(red):
    rows, w = red.shape
    half = rows // 2

    def body(in_ref, out_ref, send_sem, recv_sem):
        x, y, c = _place()
        mine = out_ref.at[pl.ds(c * half, half)]
        cp = pltpu.make_async_remote_copy(src_ref=mine, dst_ref=mine, send_sem=send_sem, recv_sem=recv_sem,
                                          device_id=(x, y, 1 - c), device_id_type=MESH)
        cp.start()
        other = out_ref.at[pl.ds((1 - c) * half, half)]
        pltpu.make_async_remote_copy(src_ref=other, dst_ref=other, send_sem=send_sem, recv_sem=recv_sem,
                                     device_id=(x, y, c), device_id_type=MESH).wait_recv()
        cp.wait_send()

    return _call(body, name="rs_sibling_gather", in_specs=[ANY_SPEC], out_specs=ANY_SPEC,
                 out_shape=jax.ShapeDtypeStruct(red.shape, F32), input_output_aliases={0: 0},
                 scratch_shapes=[pltpu.SemaphoreType.DMA, pltpu.SemaphoreType.DMA])(red)


def _rs_begin(g, place, name):
    pair, own = _pair_add(g, _sibling_swap(g), place)
    send, recv, pair, parts, token = _push_start([pair], [jnp.zeros_like(pair)], "slab", own, name + "_start")
    return (send, recv, pair, parts, own), token


def _rs_end(state, place, after, name):
    send, recv, pair, parts, own = state
    (parts,) = _push_wait(send, recv, pair, parts, "slab", after, name + "_wait")
    return _sibling_gather(_sum_chips(parts, own, place))


WEIGHTS = ("c_ctx", "w_mod", "b_mod", "w_in", "w_out", "ln_g", "ln_b", "conv_w", "conv_b", "lru_wa", "lru_ba", "lru_wx",
           "lru_bx", "lru_lam", "pool_w", "pool_scale")
SMALL_GATHERED = ("conv_w", "lru_ba", "lru_bx", "lru_lam", "pool_scale")
SMALL_UPDATED = ("c_ctx", "b_mod", "ln_g", "ln_b", "conv_w", "conv_b", "lru_ba", "lru_bx", "lru_lam", "pool_scale")


def kernel(x, c, ctx, c_ctx, w_mod, b_mod, w_in, w_out, ln_g, ln_b, conv_w, conv_b, lru_wa, lru_ba, lru_wx, lru_bx, lru_lam, pool_w, pool_scale, loss_target, m_c_ctx, m_w_mod, m_b_mod, m_w_in, m_w_out, m_ln_g, m_ln_b, m_conv_w, m_conv_b, m_lru_wa, m_lru_ba, m_lru_wx, m_lru_bx, m_lru_lam, m_pool_w, m_pool_scale, v_c_ctx, v_w_mod, v_b_mod, v_w_in, v_w_out, v_ln_g, v_ln_b, v_conv_w, v_conv_b, v_lru_wa, v_lru_ba, v_lru_wx, v_lru_bx, v_lru_lam, v_pool_w, v_pool_scale):
    weights = dict(c_ctx=c_ctx, w_mod=w_mod, b_mod=b_mod, w_in=w_in, w_out=w_out, ln_g=ln_g, ln_b=ln_b, conv_w=conv_w,
                   conv_b=conv_b, lru_wa=lru_wa, lru_ba=lru_ba, lru_wx=lru_wx, lru_bx=lru_bx, lru_lam=lru_lam,
                   pool_w=pool_w, pool_scale=pool_scale)
    mom1 = dict(c_ctx=m_c_ctx, w_mod=m_w_mod, b_mod=m_b_mod, w_in=m_w_in, w_out=m_w_out, ln_g=m_ln_g, ln_b=m_ln_b,
                conv_w=m_conv_w, conv_b=m_conv_b, lru_wa=m_lru_wa, lru_ba=m_lru_ba, lru_wx=m_lru_wx, lru_bx=m_lru_bx,
                lru_lam=m_lru_lam, pool_w=m_pool_w, pool_scale=m_pool_scale)
    mom2 = dict(c_ctx=v_c_ctx, w_mod=v_w_mod, b_mod=v_b_mod, w_in=v_w_in, w_out=v_w_out, ln_g=v_ln_g, ln_b=v_ln_b,
                conv_w=v_conv_w, conv_b=v_conv_b, lru_wa=v_lru_wa, lru_ba=v_lru_ba, lru_wx=v_lru_wx, lru_bx=v_lru_bx,
                lru_lam=v_lru_lam, pool_w=v_pool_w, pool_scale=v_pool_scale)
    xs, cx, target = x[0], ctx[0], loss_target[0]
    s_len, d = xs.shape
    es = w_out.shape[1]
    e = es * N_CHIPS
    nb = e // LANE
    c3 = w_mod.shape[2]
    n4 = w_in.shape[2]
    pq, pg = pool_w.shape[2], pool_w.shape[3]
    ng = len(POOL_WINDOWS)
    width = n4
    assert width == d and 2 * pg == width and 2 * nb * LANE == N_CHIPS * width and d % (2 * N_CHIPS) == 0
    px, py, pc = _place()
    place = jnp.stack([pc, 2 * px + py]).astype(jnp.int32)
    cctx2 = c_ctx[None, :]

    eq = e // N_CHIPS
    small_rows = [(conv_w[0], 0), (lru_ba[0], CONV_TAPS), (lru_bx[0], CONV_TAPS + 2), (lru_lam[0], CONV_TAPS + 4),
                  (pool_scale, CONV_TAPS + 6)]
    small = _rows_kernel([(a, r, 0) for a, r in small_rows], 2 * SUB, eq, "pack_small_weights")
    win0, sg = _gather_chips([w_in[0].astype(BF16), small], "gather_weights0")
    full = {n: jnp.swapaxes(sg[:, r:r + a.shape[0]], 0, 1).reshape(a.shape[0], e)
            for n, (a, r) in zip(SMALL_GATHERED, small_rows)}
    wa_h, wx_h = (0.5 * lru_wa[0]).astype(BF16), (0.5 * lru_wx[0]).astype(BF16)
    lru_args = (full["conv_w"], conv_b, wa_h, wx_h, 0.5 * full["lru_ba"], 0.5 * full["lru_bx"], full["lru_lam"])
    scale_f = full["pool_scale"]

    me8 = 4 * px + 2 * py + pc
    c_rows = _rows_kernel([(c, 0, 0)], SUB, d, "pack_c")
    c_send, c_recv, c_src, c_lands, c_token = _push_start([c_rows], [_own_slab(c_rows, devices=True)], "devices", sg,
                                                          "gather_c_start")
    (c_dev,) = _push_wait(c_send, c_recv, c_src, c_lands, "devices", c_token, "gather_c_wait")
    wm_mine = w_mod.astype(BF16)
    bm_mine = lax.dynamic_slice_in_dim(b_mod[:, None, :], place[1] * c3, c3, axis=2)
    (mod_g,) = _gather_chips([_mod_fwd(c_dev[:, 0, :], cctx2, wm_mine, bm_mine).reshape(DEPTH * 2 * SUB, c3)], "gather_mod")
    mod_all = jnp.transpose(mod_g.reshape(N_CHIPS, DEPTH, 2 * SUB, c3), (1, 2, 0, 3)).reshape(DEPTH, 2 * SUB, 3 * d)
    mod_mine = lax.dynamic_index_in_dim(mod_all, me8, axis=1, keepdims=False)
    later = [w_out[0].astype(BF16), w_in[1].astype(BF16), w_out[1].astype(BF16), pool_w.astype(BF16).reshape(ng * pq, pg)]
    w_send, w_recv, later, later_lands, w_token = _push_start(
        later, [_own_slab(a) for a in later], "same", mod_g, "gather_weights1_start")
    mod_mine = mod_mine + w_token[0:1, 0:1]

    def mod_parts(v):
        return v[None, :d], 1.0 + v[None, d:2 * d], v[None, 2 * d:]

    sh0, sc0, gt0 = mod_parts(mod_mine[0])
    shc, scc, _ = mod_parts(mod_all[0, N_DEV])
    sh1, sc1, gt1 = mod_parts(mod_mine[1])
    lg = [ln_g[l][None, :] for l in range(DEPTH)]
    lb = [ln_b[l][None, :] for l in range(DEPTH)]

    uu0, ug0 = _inproj_fwd(xs, sc0, sh0, win0, "inproj_fwd0")
    uc0 = _inproj_fwd(cx, scc, shc, win0[:2], "inproj_fwd_ctx")
    y0 = _rglru_fwd(uu0, uc0, *lru_args)
    wout0_g, win1, wout1_g, wp_g = _push_wait(w_send, w_recv, later, later_lands, "same", y0, "gather_weights1_wait")
    win = [win0, win1]
    wout = [wout0_g.reshape(e, d), wout1_g.reshape(e, d)]
    wp = wp_g.reshape(N_CHIPS, ng, pq, pg)
    br0, x1 = _outproj_fwd(y0, ug0, xs, gt0, wout[0], lg[0], lb[0], None, "outproj_fwd0")
    uu1, ug1 = _inproj_fwd(x1, sc1, sh1, win[1], "inproj_fwd1")
    d1 = _pool_map(uu1, nb, False, True, "pool_fwd")
    y1 = _pool_mm_fwd(d1, wp, scale_f)
    br1, dxo, loss_part = _outproj_fwd(y1, ug1, x1, gt1, wout[1], lg[1], lb[1], target, "outproj_fwd1")

    row_wout = d
    row_tail = d + es
    wq = 2 * (nb // N_CHIPS) * LANE * LANE // width
    whole = lambda r: (r + 2 * RS_TILE - 1) // (2 * RS_TILE) * (2 * RS_TILE)
    rows1 = whole(row_tail + pg // 2)
    rows0 = whole(row_tail + 2 * wq)
    fresh = lambda rows, used: (lax.empty if rows == used else jnp.zeros)((N_CHIPS, rows, width), F32)
    gbuf1 = fresh(rows1, row_tail + pg // 2)
    gbuf0 = fresh(rows0, row_tail + 2 * wq)

    dy1, dg1, dxres1, dbr1, dlg1, dlb1, dgt1 = _outproj_bwd(dxo, x1, br1, y1, ug1, gt1, lg[1], wout[1], "outproj_bwd1",
                                                            dy_dtype=BF16)
    gbuf1 = _outproj_bwd_w(y1, ug1, dbr1, gbuf1, row_wout, "outproj_bwd_w1")
    dd1, gbuf1, dscale = _pool_mm_bwd(dy1, d1, wp, scale_f, gbuf1, row_tail)
    du1 = _pool_map(dd1, nb, True, False, "pool_bwd")
    dx1, dsc1, dsh1 = _inproj_bwd_x([du1, dg1], x1, dxres1, sc1, win[1], "inproj_bwd_x1")
    gbuf1 = _inproj_bwd_w(x1, sc1, sh1, [du1, dg1], None, gbuf1, "inproj_bwd_w1")
    rs1, token1 = _rs_begin(gbuf1, place, "rs_exchange1")

    dy0, dg0, dxres0, dbr0, dlg0, dlb0, dgt0 = _outproj_bwd(dx1, xs, br0, y0, ug0, gt0 + token1[0:1, 0:1], lg[0], wout[0],
                                                            "outproj_bwd0")
    gbuf0 = _outproj_bwd_w(y0, ug0, dbr0, gbuf0, row_wout, "outproj_bwd_w0")
    du0, duc, dconv_w, dconv_b, dwa, dwx, dba, dbx, dlam = _rglru_bwd(uu0, uc0, dy0, *lru_args)
    dwin0c = _inproj_bwd_w(cx, scc, shc, [duc, jnp.zeros_like(duc)], None, None, "inproj_bwd_w_ctx")
    gbuf0 = _inproj_bwd_w(xs, sc0, sh0, [du0, dg0], dwin0c, gbuf0, "inproj_bwd_w0")

    def quarter(dw):
        t = dw.reshape(2, N_CHIPS, nb // N_CHIPS, LANE, LANE)
        return jnp.transpose(t, (1, 3, 0, 2, 4)).reshape(N_CHIPS, LANE, 2 * (nb // N_CHIPS) * LANE).reshape(N_CHIPS, wq, width)

    tail0 = jnp.concatenate([quarter(dwa), quarter(dwx)], axis=1)
    gbuf0 = lax.dynamic_update_slice(gbuf0, tail0, (0, row_tail, 0))
    red1 = _rs_end(rs1, place, gbuf0, "rs_exchange1")
    rs0, token0 = _rs_begin(gbuf0, place, "rs_exchange0")
    grad_x, dsc0, dsh0 = _inproj_bwd_x([du0, dg0], xs, dxres0, sc0 + token0[0:1, 0:1], win[0], "inproj_bwd_x0")
    dscc, dshc = _inproj_bwd_x([duc], cx, None, scc, win[0][:2], "inproj_bwd_x_ctx")

    k0 = VEC_KINDS
    vec = _rows_kernel(
        [(c, 0, 0), (loss_part, 0, d), (dsh0, 1, 0), (dsc0, 1, d), (dgt0, 1, 2 * d), (dshc, 2, 0), (dscc, 2, d),
         (dsh1, 3, 0), (dsc1, 3, d), (dgt1, 3, 2 * d),
         (dconv_b, k0, 0), (dlg0, k0, e), (dscale, k0 + 1, 0), (dlg1, k0 + 1, e), (dlb0, k0 + 2, 0), (dlb1, k0 + 2, d),
         (dconv_w, k0 + 3, 0), (dba, k0 + 7, 0), (dbx, k0 + 9, 0), (dlam, k0 + 11, 0)], VEC_ROWS, 3 * d, "pack_vec")
    v_send, v_recv, vec_l, vec_lands, v_token = _push_start([vec], [_own_slab(vec, devices=True)], "devices", vec,
                                                            "gather_devices_start")
    red0 = _rs_end(rs0, place, v_token, "rs_exchange0")
    quarters = red0[row_tail:row_tail + 2 * wq]
    q_send, q_recv, q_src, q_lands, q_token = _push_start([quarters], [_own_slab(quarters)], "same", red0,
                                                          "gather_replicated_start")

    tmw = _row_tile(d, 256)
    red_src = lambda red, r0, tm: (red, (tm, width), lambda n, i: (r0 // tm + i, 0))
    by_layer = lambda n, gs: jnp.where(n == 0, gs[0], gs[1])
    outs = {}
    outs["w_in"] = _adamw_param(w_in, m_w_in, v_w_in, [red_src(red0, 0, tmw), red_src(red1, 0, tmw)], by_layer, tmw, "adamw_w_in",
                                after=q_token)
    outs["w_out"] = _adamw_param(w_out, m_w_out, v_w_out, [red_src(red0, row_wout, tmw), red_src(red1, row_wout, tmw)],
                                 by_layer, tmw, "adamw_w_out")
    pw = [a.reshape(ng, pq, pg) for a in (pool_w, m_pool_w, v_pool_w)]
    outs["pool_w"] = [o.reshape(pool_w.shape) for o in _adamw_param(
        *pw, [(red1, (pq, pg), lambda n, i: (row_tail // pq + n // 2, n % 2))], lambda n, gs: gs[0], pq, "adamw_pool_w")]

    (gathered,) = _push_wait(v_send, v_recv, vec_l, vec_lands, "devices", outs["w_out"][1], "gather_devices_wait")
    gt_all = jnp.swapaxes(gathered, 0, 1)
    g_wmod = _mod_bwd_shard(gt_all, cctx2, place, c3)
    g_bmod, sq_err, g_small = _mod_bwd_rep(gt_all, d)
    loss = sq_err[0, 0] * (0.5 / d)
    cpart = _cctx_partial(gt_all, wm_mine[0], place)
    x_send, x_recv, x_src, x_lands, x_token = _push_start([cpart], [_own_slab(cpart)], "same", cpart, "gather_cctx_start")
    outs["w_mod"] = _adamw_param(w_mod, m_w_mod, v_w_mod, [(g_wmod, (None, tmw, c3), lambda n, i: (n, i, 0))],
                                 lambda n, gs: gs[0], tmw, "adamw_w_mod", after=x_token)
    (rep,) = _push_wait(q_send, q_recv, q_src, q_lands, "same", outs["w_mod"][1], "gather_replicated_wait")
    bq = nb // N_CHIPS
    rep_src = lambda r0: (rep, (None, LANE, bq * LANE), lambda n, i: (n % N_CHIPS, r0 // LANE, n // N_CHIPS))
    stack = lambda n, gs: jnp.concatenate([gs[0][:, k * LANE:(k + 1) * LANE] for k in range(bq)], axis=0)
    for name, r0, trio in (("lru_wa", 0, (lru_wa, m_lru_wa, v_lru_wa)), ("lru_wx", wq, (lru_wx, m_lru_wx, v_lru_wx))):
        blocks = [a.reshape(2 * N_CHIPS, bq * LANE, LANE) for a in trio]
        outs[name] = [o.reshape(lru_wa.shape) for o in _adamw_param(*blocks, [rep_src(r0)], stack, bq * LANE, "adamw_" + name)]

    (cparts,) = _push_wait(x_send, x_recv, x_src, x_lands, "same", outs["lru_wx"][1], "gather_cctx_wait")
    g_small = dict(g_small, c_ctx=_cctx_finish(cparts, cctx2), b_mod=g_bmod)
    for n in SMALL_GATHERED:
        g_small[n] = lax.dynamic_slice_in_dim(g_small[n], place[1] * eq, eq, axis=1)
    as2d = lambda a: a.reshape(-1, a.shape[-1])
    quads = [(as2d(weights[n]), g_small[n], as2d(mom1[n]), as2d(mom2[n])) for n in SMALL_UPDATED]
    for n, (q, res) in zip(SMALL_UPDATED, zip(quads, _adamw_small(quads))):
        outs[n] = [a.reshape(weights[n].shape) for a in (q[1],) + res]

    result = [loss, grad_x[None]]
    for j in range(4):
        result += [outs[n][j] for n in WEIGHTS]
    return tuple(result)
```

```python
import jax
import jax.numpy as jnp
from jax import lax
from jax.experimental import pallas as pl
from jax.experimental.pallas import tpu as pltpu

F32 = jnp.float32
BF16 = jnp.bfloat16
LANE = 128
SUB = 8
GRID_W = 64
POOL_WINDOWS = (2, 4, 8, 16)
LRU_C = 8.0
DEPTH = 2
ALPHA = float((2 * DEPTH) ** 0.25)
LN_EPS = 1e-5
ADAM_LR, ADAM_B1, ADAM_B2, ADAM_EPS, ADAM_WD, ADAM_STEP = 0.001, 0.9, 0.999, 1e-08, 0.01, 10
N_CHIPS = 4
N_DEV = 8
MESH = pl.DeviceIdType.MESH
ROW_TILE = 512
GATE_TILE = 4096
GATE_BWD_TILE = 2048
GATE_UNROLL = 1
CONV_TAPS = 4
CONV_LEFT = 2
PAD = 8
SCAN_UNROLL = 32
RS_TILE = 448
LN_ROWS = 128
POOL_CPAD = 16
VEC_KINDS = 4


def _call(body, **kw):
    return pl.pallas_call(body, **kw)


def _dot(a, b):
    return jnp.dot(a, b, preferred_element_type=F32)


def _dot_nt(a, b):
    return lax.dot_general(a, b, (((1,), (1,)), ((), ())), preferred_element_type=F32)


def _dot_tn(a, b):
    return lax.dot_general(a, b, (((0,), (0,)), ((), ())), preferred_element_type=F32)


def _sigmoid(v):
    return 0.5 * (jnp.tanh(0.5 * v) + 1.0)


def _silu(v):
    return v * _sigmoid(v)


def _dsilu(v):
    s = _sigmoid(v)
    return s * (1.0 + v * (1.0 - s))


def _log_sigmoid(v):
    z = jnp.exp(-jnp.abs(v))
    return jnp.minimum(v, 0.0) - jnp.where(z < 1e-4, z * (1.0 - 0.5 * z), jnp.log(1.0 + z))


def _one_minus_sq(la, a):
    return jnp.tanh(la) * (-1.0 - a * a)


def _cat(ref, n):
    return jnp.concatenate([ref[k] for k in range(n)], axis=1)


def _put_chunks(ref, val, n, base=0):
    for k in range(n):
        ref[base + k] = val[:, k * LANE:(k + 1) * LANE].astype(ref.dtype)


def _row_tile(rows, want):
    t = min(rows, want)
    assert rows % t == 0
    return t


ANY_SPEC = pl.BlockSpec(memory_space=pl.ANY)


def _mod_fwd(c_all, cctx, wm, bm):
    nl, d, c3 = wm.shape

    def body(c_ref, cx_ref, w_ref, b_ref, o_ref):
        cc = jnp.concatenate([c_ref[...], cx_ref[...], jnp.zeros((SUB - 1, d), F32)], axis=0)
        o_ref[...] = _dot(_silu(cc).astype(BF16), w_ref[...]) + b_ref[...]

    return _call(
        body, name="mod_fwd", grid=(nl,),
        in_specs=[pl.BlockSpec((N_DEV, d), lambda l: (0, 0)),
                  pl.BlockSpec((1, d), lambda l: (0, 0)),
                  pl.BlockSpec((None, d, c3), lambda l: (l, 0, 0)),
                  pl.BlockSpec((None, 1, c3), lambda l: (l, 0, 0))],
        out_specs=pl.BlockSpec((None, 2 * SUB, c3), lambda l: (l, 0, 0)),
        out_shape=jax.ShapeDtypeStruct((nl, 2 * SUB, c3), F32),
    )(c_all, cctx, wm, bm)


def _rows_kernel(parts, rows, cols, name):
    def body(*refs):
        o_ref = refs[-1]
        o_ref[...] = jnp.zeros_like(o_ref)
        for ref, (a, r0, c0) in zip(refs[:-1], parts):
            for k in range(a.shape[0]):
                o_ref[r0 + k:r0 + k + 1, c0:c0 + a.shape[1]] = ref[k:k + 1, :]

    return _call(body, name=name, grid=(1,),
                 in_specs=[pl.BlockSpec(a.shape, lambda i: (0, 0)) for a, _, _ in parts],
                 out_specs=pl.BlockSpec((rows, cols), lambda i: (0, 0)),
                 out_shape=jax.ShapeDtypeStruct((rows, cols), F32))(*[a for a, _, _ in parts])


def _mod_bwd_shard(gt, cctx, place, c3):
    d = cctx.shape[1]

    def body(p_ref, cs_ref, dm_ref, dmx_ref, cx_ref, o_ref):
        l = pl.program_id(0)
        lhs = jnp.concatenate([_silu(cs_ref[...]), _silu(cx_ref[...]), jnp.zeros((7, d), F32)], axis=0).astype(BF16)
        dmx = jnp.where(l == 0, jnp.sum(dmx_ref[...], axis=0, keepdims=True), 0.0)
        rhs = jnp.concatenate([dm_ref[...], dmx, jnp.zeros((7, c3), F32)], axis=0).astype(BF16)
        o_ref[...] = _dot_tn(lhs, rhs)

    return _call(
        body, name="mod_bwd_shard",
        grid_spec=pltpu.PrefetchScalarGridSpec(
            num_scalar_prefetch=1, grid=(DEPTH,),
            in_specs=[pl.BlockSpec((None, N_DEV, d), lambda l, p: (0, 0, 0)),
                      pl.BlockSpec((None, N_DEV, c3), lambda l, p: (1 + 2 * l, 0, p[1])),
                      pl.BlockSpec((None, N_DEV, c3), lambda l, p: (2, 0, p[1])),
                      pl.BlockSpec((1, d), lambda l, p: (0, 0))],
            out_specs=pl.BlockSpec((None, d, c3), lambda l, p: (l, 0, 0))),
        out_shape=jax.ShapeDtypeStruct((DEPTH, d, c3), F32),
    )(place, gt, gt, gt, cctx)


def _small_layout(d, e):
    k = VEC_KINDS
    return {
        "conv_b": ((1, e), [(0, k, 0)]),
        "ln_g": ((2, d), [(0, k, e), (1, k + 1, e)]),
        "pool_scale": ((1, e), [(0, k + 1, 0)]),
        "ln_b": ((2, d), [(0, k + 2, 0), (1, k + 2, d)]),
        "conv_w": ((CONV_TAPS, e), [(t, k + 3 + t, 0) for t in range(CONV_TAPS)]),
        "lru_ba": ((2, e), [(j, k + 7 + j, 0) for j in range(2)]),
        "lru_bx": ((2, e), [(j, k + 9 + j, 0) for j in range(2)]),
        "lru_lam": ((2, e), [(j, k + 11 + j, 0) for j in range(2)]),
    }


VEC_ROWS = 24


def _mod_bwd_rep(gt, d):
    d3 = gt.shape[2]
    layout = _small_layout(d, d3 - d)
    names = list(layout)

    def body(g_ref, db_ref, loss_ref, *small_refs):
        loss_ref[...] = jnp.zeros_like(loss_ref) + jnp.sum(g_ref[0][:, d:d + LANE])
        dm0 = jnp.sum(g_ref[1], axis=0, keepdims=True)
        dmx = jnp.sum(g_ref[2], axis=0, keepdims=True)
        dm1 = jnp.sum(g_ref[3], axis=0, keepdims=True)
        db_ref[0:1, :] = dm0 + dmx
        db_ref[1:2, :] = dm1
        for ref, name in zip(small_refs, names):
            shape, places = layout[name]
            for arr_row, vec_row, col0 in places:
                total = jnp.sum(g_ref[vec_row], axis=0, keepdims=True)
                ref[arr_row:arr_row + 1, :] = total[:, col0:col0 + shape[1]]

    outs = _call(
        body, name="mod_bwd_rep", grid=(1,),
        in_specs=[pl.BlockSpec(gt.shape, lambda i: (0, 0, 0))],
        out_specs=[pl.BlockSpec((DEPTH, d3), lambda i: (0, 0)), pl.BlockSpec((1, LANE), lambda i: (0, 0))]
        + [pl.BlockSpec(layout[n][0], lambda i: (0, 0)) for n in names],
        out_shape=[jax.ShapeDtypeStruct((DEPTH, d3), F32), jax.ShapeDtypeStruct((1, LANE), F32)]
        + [jax.ShapeDtypeStruct(layout[n][0], F32) for n in names],
    )(gt)
    return outs[0], outs[1], dict(zip(names, outs[2:]))


def _cctx_partial(gt, wm0, place):
    d, c3 = wm0.shape

    def body(p_ref, dmx_ref, w_ref, o_ref):
        dmx = jnp.sum(dmx_ref[...], axis=0, keepdims=True)
        o_ref[...] = _dot_nt(jnp.broadcast_to(dmx, (2 * SUB, c3)).astype(BF16), w_ref[...])

    return _call(
        body, name="cctx_partial",
        grid_spec=pltpu.PrefetchScalarGridSpec(
            num_scalar_prefetch=1, grid=(1,),
            in_specs=[pl.BlockSpec((None, N_DEV, c3), lambda i, p: (2, 0, p[1])), pl.BlockSpec((d, c3), lambda i, p: (0, 0))],
            out_specs=pl.BlockSpec((2 * SUB, d), lambda i, p: (0, 0))),
        out_shape=jax.ShapeDtypeStruct((2 * SUB, d), F32),
    )(place, gt, wm0)


def _cctx_finish(parts, cctx):
    d = cctx.shape[1]

    def body(p_ref, cx_ref, o_ref):
        total = (p_ref[0, 0:1, :] + p_ref[1, 0:1, :]) + (p_ref[2, 0:1, :] + p_ref[3, 0:1, :])
        o_ref[...] = total * _dsilu(cx_ref[...])

    return _call(body, name="cctx_finish", grid=(1,),
                 in_specs=[pl.BlockSpec(parts.shape, lambda i: (0, 0, 0)), pl.BlockSpec((1, d), lambda i: (0, 0))],
                 out_specs=pl.BlockSpec((1, d), lambda i: (0, 0)),
                 out_shape=jax.ShapeDtypeStruct((1, d), F32))(parts, cctx)


def _inproj_fwd(xin, sc1, sh, w, name):
    rows, d = xin.shape
    ns, _, n4 = w.shape
    cpb = n4 // LANE
    tm = _row_tile(rows, 512)
    assert ns in (2, 4)

    def body(x_ref, sc_ref, sh_ref, w_ref, *o_refs):
        h = (x_ref[...] * sc_ref[...] + sh_ref[...]).astype(BF16)
        for s in range(ns):
            _put_chunks(o_refs[s // 2], _dot(h, w_ref[s]), cpb, base=(s % 2) * cpb)

    spec = pl.BlockSpec((2 * cpb, tm, LANE), lambda i: (0, i, 0))
    dtypes = (F32, BF16)[:ns // 2]
    res = _call(
        body, name=name, grid=(rows // tm,),
        in_specs=[pl.BlockSpec((tm, d), lambda i: (i, 0)),
                  pl.BlockSpec((1, d), lambda i: (0, 0)),
                  pl.BlockSpec((1, d), lambda i: (0, 0)),
                  pl.BlockSpec((ns, d, n4), lambda i: (0, 0, 0))],
        out_specs=[spec] * len(dtypes),
        out_shape=[jax.ShapeDtypeStruct((2 * cpb, rows, LANE), t) for t in dtypes],
    )(xin, sc1, sh, w)
    return res[0] if ns == 2 else tuple(res)


def _inproj_bwd_x(dparts, xin, dxres, sc1, w, name):
    rows, d = xin.shape
    npart = len(dparts)
    e = dparts[0].shape[1]
    ns, _, n4 = w.shape
    per = e // n4
    assert per * npart == ns
    tm = _row_tile(rows, 512)
    has_res = dxres is not None

    def body(*refs):
        dp = refs[:npart]
        x_ref, sc_ref, w_ref = refs[npart:npart + 3]
        rest = refs[npart + 3:]
        if has_res:
            res_ref, dx_ref, dsc_ref, dsh_ref = rest
        else:
            dsc_ref, dsh_ref = rest
        i = pl.program_id(0)
        dh = jnp.zeros((tm, d), F32)
        for p in range(npart):
            v = dp[p][...]
            for q in range(per):
                dh = dh + _dot_nt(v[:, q * n4:(q + 1) * n4], w_ref[p * per + q])

        @pl.when(i == 0)
        def _():
            dsc_ref[...] = jnp.zeros_like(dsc_ref)
            dsh_ref[...] = jnp.zeros_like(dsh_ref)

        dsc_ref[...] += jnp.sum(dh * x_ref[...], axis=0, keepdims=True)
        dsh_ref[...] += jnp.sum(dh, axis=0, keepdims=True)
        if has_res:
            dx_ref[...] = res_ref[...] + dh * sc_ref[...]

    row_spec = pl.BlockSpec((tm, d), lambda i: (i, 0))
    vec_spec = pl.BlockSpec((1, d), lambda i: (0, 0))
    in_specs = [pl.BlockSpec((tm, e), lambda i: (i, 0))] * npart + [row_spec, vec_spec,
                                                                     pl.BlockSpec((ns, d, n4), lambda i: (0, 0, 0))]
    args = list(dparts) + [xin, sc1, w]
    out_specs, out_shape = [vec_spec, vec_spec], [jax.ShapeDtypeStruct((1, d), F32)] * 2
    if has_res:
        in_specs.append(row_spec)
        args.append(dxres)
        out_specs = [row_spec] + out_specs
        out_shape = [jax.ShapeDtypeStruct((rows, d), F32)] + out_shape
    return _call(body, name=name, grid=(rows // tm,), in_specs=in_specs, out_specs=out_specs, out_shape=out_shape)(*args)


def _inproj_bwd_w(xin, sc1, sh, dparts, init, gbuf, name):
    rows, d = xin.shape
    npart = len(dparts)
    e = dparts[0].shape[1]
    n4 = e // 2
    ns = 2 * npart
    tm = _row_tile(rows, 1024)
    nt = rows // tm
    has_init = init is not None
    into = gbuf is not None
    assert not into or (ns == N_CHIPS and gbuf.shape[2] == n4)

    def body(*refs):
        x_ref, sc_ref, sh_ref = refs[:3]
        dp = refs[3:3 + npart]
        init_ref = refs[3 + npart] if has_init else None
        o_ref = refs[-1]
        s, i = pl.program_id(0), pl.program_id(1)
        h = (x_ref[...] * sc_ref[...] + sh_ref[...]).astype(BF16)

        @pl.when(i == 0)
        def _():
            o_ref[...] = init_ref[...] if has_init else jnp.zeros_like(o_ref)

        for p in range(npart):
            @pl.when(s // 2 == p)
            def _(p=p):
                o_ref[...] += _dot_tn(h, dp[p][...])

    in_specs = [pl.BlockSpec((tm, d), lambda s, i: (i, 0)),
                pl.BlockSpec((1, d), lambda s, i: (0, 0)),
                pl.BlockSpec((1, d), lambda s, i: (0, 0))]
    in_specs += [pl.BlockSpec((tm, n4), lambda s, i: (i, s % 2))] * npart
    args = [xin, sc1, sh] + list(dparts)
    o_spec = pl.BlockSpec((None, d, n4), lambda s, i: (s, 0, 0))
    if has_init:
        in_specs.append(o_spec)
        args.append(init)
    extra = {}
    if into:
        in_specs.append(ANY_SPEC)
        args.append(gbuf)
        extra = dict(input_output_aliases={len(args) - 1: 0})
    out_shape = jax.ShapeDtypeStruct(gbuf.shape if into else (ns, d, n4), F32)
    return _call(body, name=name, grid=(ns, nt), in_specs=in_specs, out_specs=o_spec, out_shape=out_shape, **extra)(*args)


def _gated(y_ref, g_ref, nch):
    return jnp.concatenate([(y_ref[k].astype(F32) * _silu(g_ref[k].astype(F32))).astype(BF16) for k in range(nch)], axis=1)


def _ln_stats(r):
    mu = jnp.mean(r, axis=-1, keepdims=True)
    var = jnp.mean(jnp.square(r - mu), axis=-1, keepdims=True)
    rstd = lax.rsqrt(var + LN_EPS)
    return (r - mu) * rstd, rstd


def _outproj_fwd(y, ug, xin, gt, wout, lg, lb, target, name):
    nch, rows, _ = y.shape
    e, d = wout.shape
    tm = _row_tile(rows, 512)
    with_loss = target is not None

    def body(*refs):
        y_ref, g_ref, x_ref, gt_ref, w_ref, lg_ref, lb_ref = refs[:7]
        if with_loss:
            t_ref, br_ref, dxo_ref, loss_ref = refs[7:]
        else:
            br_ref, xo_ref = refs[7:]
        z = _gated(y_ref, g_ref, nch)
        br_ref[...] = _dot(z, w_ref[...])
        if with_loss:
            @pl.when(pl.program_id(0) == 0)
            def _():
                loss_ref[...] = jnp.zeros_like(loss_ref)

        def norm(j, c):
            rows = pl.ds(pl.multiple_of(j * LN_ROWS, LN_ROWS), LN_ROWS)
            xhat, _ = _ln_stats(ALPHA * x_ref[rows, :] + gt_ref[...] * br_ref[rows, :])
            xo = xhat * lg_ref[...] + lb_ref[...]
            if with_loss:
                err = xo - t_ref[rows, :]
                dxo_ref[rows, :] = err * (1.0 / d)
                col = jnp.sum(err * err, axis=0, keepdims=True)
                loss_ref[...] += sum(col[:, k * LANE:(k + 1) * LANE] for k in range(d // LANE))
            else:
                xo_ref[rows, :] = xo
            return c

        lax.fori_loop(0, tm // LN_ROWS, norm, 0)

    chunk_spec = pl.BlockSpec((nch, tm, LANE), lambda i: (0, i, 0))
    g_spec = chunk_spec
    row_spec = pl.BlockSpec((tm, d), lambda i: (i, 0))
    vec_spec = pl.BlockSpec((1, d), lambda i: (0, 0))
    in_specs = [chunk_spec, g_spec, row_spec, vec_spec, pl.BlockSpec((e, d), lambda i: (0, 0)), vec_spec, vec_spec]
    args = [y, ug, xin, gt, wout, lg, lb]
    out_specs = [row_spec, row_spec]
    out_shape = [jax.ShapeDtypeStruct((rows, d), F32)] * 2
    if with_loss:
        in_specs.append(row_spec)
        args.append(target)
        out_specs.append(pl.BlockSpec((1, LANE), lambda i: (0, 0)))
        out_shape.append(jax.ShapeDtypeStruct((1, LANE), F32))
    return _call(body, name=name, grid=(rows // tm,), in_specs=in_specs, out_specs=out_specs, out_shape=out_shape)(*args)


def _outproj_bwd(dxo, xin, br, y, ug, gt, lg, wout, name, dy_dtype=F32):
    nch, rows, _ = y.shape
    e, d = wout.shape
    tm = _row_tile(rows, 256)

    def body(dxo_ref, x_ref, br_ref, y_ref, g_ref, gt_ref, lg_ref, w_ref,
             dy_ref, dg_ref, dxres_ref, dbr_ref, dlg_ref, dlb_ref, dgt_ref):
        @pl.when(pl.program_id(0) == 0)
        def _():
            dlg_ref[...] = jnp.zeros_like(dlg_ref)
            dlb_ref[...] = jnp.zeros_like(dlb_ref)
            dgt_ref[...] = jnp.zeros_like(dgt_ref)

        def norm_bwd(j, c):
            rows = pl.ds(pl.multiple_of(j * LN_ROWS, LN_ROWS), LN_ROWS)
            dxo_v = dxo_ref[rows, :]
            brv = br_ref[rows, :]
            xhat, rstd = _ln_stats(ALPHA * x_ref[rows, :] + gt_ref[...] * brv)
            dxh = dxo_v * lg_ref[...]
            dr = rstd * (dxh - jnp.mean(dxh, axis=-1, keepdims=True) - xhat * jnp.mean(dxh * xhat, axis=-1, keepdims=True))
            dlg_ref[...] += jnp.sum(dxo_v * xhat, axis=0, keepdims=True)
            dlb_ref[...] += jnp.sum(dxo_v, axis=0, keepdims=True)
            dgt_ref[...] += jnp.sum(dr * brv, axis=0, keepdims=True)
            dxres_ref[rows, :] = ALPHA * dr
            dbr_ref[rows, :] = (gt_ref[...] * dr).astype(BF16)
            return c

        lax.fori_loop(0, tm // LN_ROWS, norm_bwd, 0)
        dz = _dot_nt(dbr_ref[...], w_ref[...])
        for k in range(nch):
            dzk = dz[:, k * LANE:(k + 1) * LANE]
            gk = g_ref[k].astype(F32)
            sk = _sigmoid(gk)
            dy_ref[k] = (dzk * (gk * sk)).astype(dy_ref.dtype)
            dg_ref[:, k * LANE:(k + 1) * LANE] = (dzk * y_ref[k].astype(F32) * (sk * (1.0 + gk * (1.0 - sk)))).astype(BF16)

    chunk_spec = pl.BlockSpec((nch, tm, LANE), lambda i: (0, i, 0))
    g_spec = chunk_spec
    row_spec = pl.BlockSpec((tm, d), lambda i: (i, 0))
    vec_spec = pl.BlockSpec((1, d), lambda i: (0, 0))
    return _call(
        body, name=name, grid=(rows // tm,),
        in_specs=[row_spec, row_spec, row_spec, chunk_spec, g_spec, vec_spec, vec_spec, pl.BlockSpec((e, d), lambda i: (0, 0))],
        out_specs=[chunk_spec, pl.BlockSpec((tm, e), lambda i: (i, 0)), row_spec, row_spec, vec_spec, vec_spec, vec_spec],
        out_shape=[jax.ShapeDtypeStruct((nch, rows, LANE), dy_dtype), jax.ShapeDtypeStruct((rows, e), BF16),
                   jax.ShapeDtypeStruct((rows, d), F32), jax.ShapeDtypeStruct((rows, d), BF16)]
        + [jax.ShapeDtypeStruct((1, d), F32)] * 3,
    )(dxo, xin, br, y, ug, gt, lg, wout)


def _outproj_bwd_w(y, ug, dbr, gbuf, row0, name):
    nch, rows, _ = y.shape
    d = dbr.shape[1]
    e = nch * LANE
    es = e // N_CHIPS
    tm = _row_tile(rows, 1024)
    assert gbuf.shape[2] == d and row0 % es == 0

    def body(y_ref, g_ref, dbr_ref, buf_ref, o_ref):
        @pl.when(pl.program_id(0) == 0)
        def _():
            o_ref[...] = jnp.zeros_like(o_ref)

        z = _gated(y_ref, g_ref, nch)
        o_ref[...] += _dot_tn(z, dbr_ref[...]).reshape(N_CHIPS, es, d)

    return _call(
        body, name=name, grid=(rows // tm,),
        in_specs=[pl.BlockSpec((nch, tm, LANE), lambda i: (0, i, 0)),
                  pl.BlockSpec((nch, tm, LANE), lambda i: (0, i, 0)),
                  pl.BlockSpec((tm, d), lambda i: (i, 0)),
                  ANY_SPEC],
        out_specs=pl.BlockSpec((N_CHIPS, es, d), lambda i: (0, row0 // es, 0)),
        out_shape=jax.ShapeDtypeStruct(gbuf.shape, F32),
        input_output_aliases={3: 0},
    )(y, ug, dbr, gbuf)


def _scan(a_ref, b_ref, h_ref, *, length, init, reverse, a_shift, store):
    nblk = length // SUB
    unroll = min(SCAN_UNROLL, nblk)
    assert nblk % unroll == 0
    row = lax.broadcasted_iota(jnp.int32, (SUB, LANE), 0)
    last = 0 if reverse else SUB - 1
    edges = [(row >= SUB - k) if reverse else (row < k) for k in (1, 2, 4)]

    def local_scan(a, b):
        for k, edge in zip((1, 2, 4), edges):
            sh = (SUB - k) if reverse else k
            b = b + a * jnp.where(edge, 0.0, pltpu.roll(b, sh, 0))
            a = a * jnp.where(edge, 1.0, pltpu.roll(a, sh, 0))
        return a, b

    def step(i, carry):
        base = pl.multiple_of(((nblk // unroll - 1 - i) if reverse else i) * (unroll * SUB), unroll * SUB)
        order = range(unroll - 1, -1, -1) if reverse else range(unroll)
        loaded = [(a_ref[pl.ds(PAD + base + j * SUB + a_shift, SUB), :], b_ref[pl.ds(PAD + base + j * SUB, SUB), :])
                  for j in order]
        scanned = [local_scan(a, b) for a, b in loaded]
        for j, (a, b) in zip(order, scanned):
            if store:
                h_ref[pl.ds(PAD + base + j * SUB, SUB), :] = b + a * carry
            a_l = jnp.broadcast_to(a[last:last + 1, :], (SUB, LANE))
            b_l = jnp.broadcast_to(b[last:last + 1, :], (SUB, LANE))
            carry = b_l + a_l * carry
        return carry

    carry = lax.fori_loop(0, nblk // unroll, step, jnp.broadcast_to(init, (SUB, LANE)))
    return carry[0:1, :]


def _conv_fwd(src_ref, upad, u_ref, cw, cb, length):
    zeros = jnp.zeros((PAD, LANE), F32)
    upad[pl.ds(0, PAD), :] = zeros
    upad[pl.ds(PAD + length, PAD), :] = zeros
    rt = _row_tile(length, ROW_TILE)

    def copy(i, c):
        t0 = pl.multiple_of(i * rt, rt)
        upad[pl.ds(PAD + t0, rt), :] = src_ref[pl.ds(t0, rt), :]
        return c

    lax.fori_loop(0, length // rt, copy, 0)

    def tile(i, c):
        t0 = pl.multiple_of(i * rt, rt)
        acc = jnp.zeros((rt, LANE), F32)
        for k in range(CONV_TAPS):
            acc = acc + upad[pl.ds(t0 + PAD - CONV_LEFT + k, rt), :] * cw[k:k + 1, :]
        u_ref[pl.ds(t0, rt), :] = acc + cb
        return c

    lax.fori_loop(0, length // rt, tile, 0)


def _gates_fwd(u_ref, a_ref, b_ref, wa, wx, ba, bx, ls, length, keep=None):
    rt = _row_tile(length, GATE_TILE)
    ls_c = LRU_C * ls

    def tile(i, c):
        t0 = pl.multiple_of(i * rt, rt)
        ut = u_ref[pl.ds(t0, rt), :]
        ub = ut.astype(BF16)
        r = 0.5 * (jnp.tanh(_dot(ub, wa) + ba) + 1.0)
        ig = 0.5 * (jnp.tanh(_dot(ub, wx) + bx) + 1.0)
        if keep is not None:
            keep[0][pl.ds(t0, rt), :] = r
            keep[1][pl.ds(t0, rt), :] = ig
        la = r * ls_c
        a = jnp.exp(la)
        a_ref[pl.ds(PAD + t0, rt), :] = a
        q = _one_minus_sq(la, a)
        b_ref[pl.ds(PAD + t0, rt), :] = jnp.where(q > 0.0, q * lax.rsqrt(q), 0.0) * (ig * ut)
        return c

    lax.fori_loop(0, length // rt, tile, 0, unroll=min(GATE_UNROLL, length // rt))


def _lru_specs():
    return [pl.BlockSpec((CONV_TAPS, LANE), lambda n: (0, n)),
            pl.BlockSpec((1, LANE), lambda n: (0, n)),
            pl.BlockSpec((2, None, LANE, LANE), lambda n: (0, n, 0, 0)),
            pl.BlockSpec((2, None, LANE, LANE), lambda n: (0, n, 0, 0)),
            pl.BlockSpec((2, LANE), lambda n: (0, n)),
            pl.BlockSpec((2, LANE), lambda n: (0, n)),
            pl.BlockSpec((2, LANE), lambda n: (0, n))]


def _rglru_fwd(ug, uc, conv_w, conv_b, wa, wx, ba, bx, lam):
    nb = uc.shape[0]
    s_len, t_len = ug.shape[1], uc.shape[1]

    def body(u0_ref, uc0_ref, cw_ref, cb_ref, wa_ref, wx_ref, ba_ref, bx_ref, lam_ref, y_ref,
             upad, ubuf, abuf, hbuf):
        cw, cb = cw_ref[...], cb_ref[...]
        lsig = _log_sigmoid(lam_ref[...])
        zero = jnp.zeros((1, LANE), F32)
        _conv_fwd(uc0_ref, upad, ubuf, cw, cb, t_len)
        h0 = []
        for dr in range(2):
            _gates_fwd(ubuf, abuf, hbuf, wa_ref[dr], wx_ref[dr], ba_ref[dr:dr + 1, :], bx_ref[dr:dr + 1, :],
                       lsig[dr:dr + 1, :], t_len)
            h0.append(_scan(abuf, hbuf, hbuf, length=t_len, init=zero, reverse=(dr == 1), a_shift=0, store=False))
        _conv_fwd(u0_ref, upad, ubuf, cw, cb, s_len)
        rt = _row_tile(s_len, ROW_TILE)
        for dr in range(2):
            _gates_fwd(ubuf, abuf, hbuf, wa_ref[dr], wx_ref[dr], ba_ref[dr:dr + 1, :], bx_ref[dr:dr + 1, :],
                       lsig[dr:dr + 1, :], s_len)
            _scan(abuf, hbuf, hbuf, length=s_len, init=h0[dr], reverse=(dr == 1), a_shift=0, store=True)

            def acc(i, c, dr=dr):
                t0 = pl.multiple_of(i * rt, rt)
                h = hbuf[pl.ds(PAD + t0, rt), :]
                if dr == 0:
                    upad[pl.ds(PAD + t0, rt), :] = h
                else:
                    y_ref[pl.ds(t0, rt), :] = (upad[pl.ds(PAD + t0, rt), :] + h).astype(y_ref.dtype)
                return c

            lax.fori_loop(0, s_len // rt, acc, 0)

    seq = pltpu.VMEM((s_len + 2 * PAD, LANE), F32)
    return _call(
        body, name="rglru_fwd", grid=(nb,),
        in_specs=[pl.BlockSpec((None, s_len, LANE), lambda n: (n, 0, 0)),
                  pl.BlockSpec((None, t_len, LANE), lambda n: (n, 0, 0))] + _lru_specs(),
        out_specs=pl.BlockSpec((None, s_len, LANE), lambda n: (n, 0, 0)),
        out_shape=jax.ShapeDtypeStruct((nb, s_len, LANE), BF16),
        scratch_shapes=[seq, pltpu.VMEM((s_len, LANE), F32), seq, seq],
    )(ug, uc, conv_w, conv_b, wa, wx, ba, bx, lam)


def _rglru_bwd(ug, uc, dy, conv_w, conv_b, wa, wx, ba, bx, lam):
    nb = uc.shape[0]
    e = nb * LANE
    s_len, t_len = ug.shape[1], uc.shape[1]

    def body(u0_ref, uc0_ref, dy_ref, cw_ref, cb_ref, wa_ref, wx_ref, ba_ref, bx_ref, lam_ref,
             du_ref, duc_ref, dcw_ref, dcb_ref, dwa_ref, dwx_ref, dba_ref, dbx_ref, dlam_ref,
             upad, ubuf, abuf, hbuf, lbuf, dubuf, rbuf, ibuf, cpad, cu, ca0, ch0, ca1, ch1, cr0, ci0, cr1, ci1):
        cw, cb = cw_ref[...], cb_ref[...]
        lam_v = lam_ref[...]
        lsig = _log_sigmoid(lam_v)
        zero = jnp.zeros((1, LANE), F32)
        zpad = jnp.zeros((PAD, LANE), F32)
        for ref in (dcw_ref, dcb_ref, dwa_ref, dwx_ref, dba_ref, dbx_ref, dlam_ref):
            ref[...] = jnp.zeros_like(ref)

        def params(dr):
            return (wa_ref[dr], wx_ref[dr], ba_ref[dr:dr + 1, :], bx_ref[dr:dr + 1, :], lsig[dr:dr + 1, :])

        def direction_bwd(dr, u_ref, a_ref, h_ref, l_ref, gates, dub, length, first):
            wa_d, wx_d, ba_d, bx_d, ls_d = params(dr)
            rt = _row_tile(length, GATE_BWD_TILE)
            prev = 1 if dr == 1 else -1

            def tile(i, c):
                t0 = pl.multiple_of(i * rt, rt)
                ut = u_ref[pl.ds(t0, rt), :]
                ub = ut.astype(BF16)
                r = gates[0][pl.ds(t0, rt), :]
                ig = gates[1][pl.ds(t0, rt), :]
                la = r * (LRU_C * ls_d)
                a = a_ref[pl.ds(PAD + t0, rt), :]
                q = _one_minus_sq(la, a)
                rs = lax.rsqrt(q)
                sq = q * rs
                lm = l_ref[pl.ds(PAD + t0, rt), :]
                da = lm * h_ref[pl.ds(PAD + t0 + prev, rt), :]
                dsq = lm * ig * ut
                dig = lm * sq * ut
                dla = da * a - dsq * (a * a) * rs
                dr_ = dla * (LRU_C * ls_d)
                dlam_ref[dr:dr + 1, :] += jnp.sum(dla * (LRU_C * r), axis=0, keepdims=True)
                dpr = dr_ * r * (1.0 - r)
                dpi = dig * ig * (1.0 - ig)
                dba_ref[dr:dr + 1, :] += jnp.sum(dpr, axis=0, keepdims=True)
                dbx_ref[dr:dr + 1, :] += jnp.sum(dpi, axis=0, keepdims=True)
                dprb, dpib = dpr.astype(BF16), dpi.astype(BF16)
                dwa_ref[dr] += _dot_tn(ub, dprb)
                dwx_ref[dr] += _dot_tn(ub, dpib)
                dut = lm * sq * ig + 2.0 * (_dot_nt(dprb, wa_d) + _dot_nt(dpib, wx_d))
                if first:
                    dub[pl.ds(PAD + t0, rt), :] = dut
                else:
                    dub[pl.ds(PAD + t0, rt), :] += dut
                return c

            lax.fori_loop(0, length // rt, tile, 0, unroll=min(GATE_UNROLL, length // rt))

        def conv_bwd(dub, src_pad, out_ref, length):
            rt = _row_tile(length, ROW_TILE)

            def tile(i, c):
                t0 = pl.multiple_of(i * rt, rt)
                dut = dub[pl.ds(PAD + t0, rt), :]
                dcb_ref[...] += jnp.sum(dut, axis=0, keepdims=True)
                acc = jnp.zeros((rt, LANE), F32)
                for k in range(CONV_TAPS):
                    sh = CONV_LEFT - k
                    acc = acc + dub[pl.ds(PAD + t0 + sh, rt), :] * cw[k:k + 1, :]
                    dcw_ref[k:k + 1, :] += jnp.sum(dut * src_pad[pl.ds(PAD + t0 - sh, rt), :], axis=0, keepdims=True)
                out_ref[pl.ds(t0, rt), :] = acc.astype(out_ref.dtype)
                return c

            lax.fori_loop(0, length // rt, tile, 0)

        _conv_fwd(uc0_ref, cpad, cu, cw, cb, t_len)
        cbufs = ((ca0, ch0), (ca1, ch1))
        cgates = ((cr0, ci0), (cr1, ci1))
        h0 = []
        for dr in range(2):
            ca, chh = cbufs[dr]
            _gates_fwd(cu, ca, chh, *params(dr), t_len, keep=cgates[dr])
            h0.append(_scan(ca, chh, chh, length=t_len, init=zero, reverse=(dr == 1), a_shift=0, store=True))
        _conv_fwd(u0_ref, upad, ubuf, cw, cb, s_len)
        rt = _row_tile(s_len, ROW_TILE)
        dh0 = []
        for dr in range(2):
            rev = dr == 1
            _gates_fwd(ubuf, abuf, hbuf, *params(dr), s_len, keep=(rbuf, ibuf))
            _scan(abuf, hbuf, hbuf, length=s_len, init=h0[dr], reverse=rev, a_shift=0, store=True)
            first_row = PAD + s_len if rev else PAD - 1
            hbuf[pl.ds(first_row, 1), :] = h0[dr]
            end_row = PAD - 1 if rev else PAD + s_len
            abuf[pl.ds(end_row, 1), :] = zero

            def copy(i, c):
                t0 = pl.multiple_of(i * rt, rt)
                lbuf[pl.ds(PAD + t0, rt), :] = dy_ref[pl.ds(t0, rt), :]
                return c

            lax.fori_loop(0, s_len // rt, copy, 0)
            _scan(abuf, lbuf, lbuf, length=s_len, init=zero, reverse=not rev, a_shift=(-1 if rev else 1), store=True)
            start = PAD + s_len - 1 if rev else PAD
            dh0.append(abuf[pl.ds(start, 1), :] * lbuf[pl.ds(start, 1), :])
            direction_bwd(dr, ubuf, abuf, hbuf, lbuf, (rbuf, ibuf), dubuf, s_len, first=(dr == 0))
        dubuf[pl.ds(0, PAD), :] = zpad
        dubuf[pl.ds(PAD + s_len, PAD), :] = zpad
        conv_bwd(dubuf, upad, du_ref, s_len)
        lc = lbuf
        duc_buf = dubuf
        for dr in range(2):
            rev = dr == 1
            ca, chh = cbufs[dr]
            first_row = PAD + t_len if rev else PAD - 1
            chh[pl.ds(first_row, 1), :] = zero
            end_row = PAD - 1 if rev else PAD + t_len
            ca[pl.ds(end_row, 1), :] = zero + 1.0
            rtc = _row_tile(t_len, ROW_TILE)

            def clear(i, c):
                t0 = pl.multiple_of(i * rtc, rtc)
                lc[pl.ds(PAD + t0, rtc), :] = jnp.zeros((rtc, LANE), F32)
                return c

            lax.fori_loop(0, t_len // rtc, clear, 0)
            _scan(ca, lc, lc, length=t_len, init=dh0[dr], reverse=not rev, a_shift=(-1 if rev else 1), store=True)
            direction_bwd(dr, cu, ca, chh, lc, cgates[dr], duc_buf, t_len, first=(dr == 0))
        duc_buf[pl.ds(0, PAD), :] = zpad
        duc_buf[pl.ds(PAD + t_len, PAD), :] = zpad
        conv_bwd(duc_buf, cpad, duc_ref, t_len)
        dlam_ref[...] = dlam_ref[...] * (1.0 - _sigmoid(lam_v))

    seq = pltpu.VMEM((s_len + 2 * PAD, LANE), F32)
    cseq = pltpu.VMEM((t_len + 2 * PAD, LANE), F32)
    flat = pltpu.VMEM((s_len, LANE), F32)
    cflat = pltpu.VMEM((t_len, LANE), F32)
    vec2 = pl.BlockSpec((2, LANE), lambda n: (0, n))
    wspec = pl.BlockSpec((2, None, LANE, LANE), lambda n: (0, n, 0, 0))
    return _call(
        body, name="rglru_bwd", grid=(nb,),
        in_specs=[pl.BlockSpec((None, s_len, LANE), lambda n: (n, 0, 0)),
                  pl.BlockSpec((None, t_len, LANE), lambda n: (n, 0, 0)),
                  pl.BlockSpec((None, s_len, LANE), lambda n: (n, 0, 0))] + _lru_specs(),
        out_specs=[pl.BlockSpec((s_len, LANE), lambda n: (0, n)),
                   pl.BlockSpec((t_len, LANE), lambda n: (0, n)),
                   pl.BlockSpec((CONV_TAPS, LANE), lambda n: (0, n)),
                   pl.BlockSpec((1, LANE), lambda n: (0, n)),
                   wspec, wspec, vec2, vec2, vec2],
        out_shape=[jax.ShapeDtypeStruct((s_len, e), BF16), jax.ShapeDtypeStruct((t_len, e), BF16),
                   jax.ShapeDtypeStruct((CONV_TAPS, e), F32), jax.ShapeDtypeStruct((1, e), F32),
                   jax.ShapeDtypeStruct((2, nb, LANE, LANE), F32), jax.ShapeDtypeStruct((2, nb, LANE, LANE), F32),
                   jax.ShapeDtypeStruct((2, e), F32), jax.ShapeDtypeStruct((2, e), F32), jax.ShapeDtypeStruct((2, e), F32)],
        scratch_shapes=[seq, flat, seq, seq, seq, seq, flat, flat,
                        cseq, cflat, cseq, cseq, cseq, cseq, cflat, cflat, cflat, cflat],
    )(ug, uc, dy, conv_w, conv_b, wa, wx, ba, bx, lam)


def _pool_windows(src_ref, out_ref, colbuf, rowbuf, half, transpose, s_len):
    gw = GRID_W
    lg = gw.bit_length() - 1
    n_rows = s_len // gw
    cp, rm = POOL_CPAD, 8 * gw
    stride = gw + 2 * cp
    rt = _row_tile(s_len, ROW_TILE)
    assert rt % gw == 0 and half <= cp
    gpt = rt // gw
    offs = range(-half, half)
    zmargin = jnp.zeros((cp, LANE), F32)

    def zcol(r, c):
        base = pl.multiple_of(r * stride, SUB)
        colbuf[pl.ds(base, cp), :] = zmargin
        colbuf[pl.ds(base + cp + gw, cp), :] = zmargin
        return c

    lax.fori_loop(0, n_rows, zcol, 0)

    def zrow(i, c):
        t0 = pl.multiple_of(i * gw, gw)
        rowbuf[pl.ds(t0, gw), :] = jnp.zeros((gw, LANE), F32)
        rowbuf[pl.ds(rm + s_len + t0, gw), :] = jnp.zeros((gw, LANE), F32)
        return c

    lax.fori_loop(0, rm // gw, zrow, 0)

    col = lax.broadcasted_iota(jnp.int32, (gw, LANE), 0)
    ccnt = (jnp.minimum(col + half, gw) - jnp.maximum(col - half, 0)).astype(F32)

    def row_counts(t0):
        row = (t0 + lax.broadcasted_iota(jnp.int32, (rt, LANE), 0)) >> lg
        return (jnp.minimum(row + half, n_rows) - jnp.maximum(row - half, 0)).astype(F32)

    def col_base(t0, g):
        return pl.multiple_of((t0 // gw) * stride, SUB) + g * stride + cp

    def col_sum(t0, g, sign):
        acc = jnp.zeros((gw, LANE), F32)
        for o in offs:
            acc = acc + colbuf[pl.ds(col_base(t0, g) + sign * o, gw), :]
        return acc

    def row_sum(t0, sign):
        acc = jnp.zeros((rt, LANE), F32)
        for o in offs:
            acc = acc + rowbuf[pl.ds(rm + t0 + sign * o * gw, rt), :]
        return acc

    n_tiles = s_len // rt
    assert rt >= half * gw

    def loop(fn, edges=False):
        def step(i, c):
            t0 = pl.multiple_of(i * rt, rt)
            fn(t0, False) if edges else fn(t0)
            return c
        if edges:
            fn(0, True)
            if n_tiles > 1:
                fn(s_len - rt, True)
            lax.fori_loop(1, n_tiles - 1, step, 0)
        else:
            lax.fori_loop(0, n_tiles, step, 0)

    inv_ccnt = 1.0 / ccnt

    def by_row_count(v, t0, edge):
        return v / row_counts(t0) if edge else v * (1.0 / (2 * half))

    if not transpose:
        def fill(t0):
            for g in range(gpt):
                colbuf[pl.ds(col_base(t0, g), gw), :] = src_ref[pl.ds(t0 + g * gw, gw), :]

        def cols(t0):
            for g in range(gpt):
                rowbuf[pl.ds(rm + t0 + g * gw, gw), :] = col_sum(t0, g, 1) * inv_ccnt

        def rows(t0, edge):
            mean = by_row_count(row_sum(t0, 1), t0, edge)
            out_ref[pl.ds(t0, rt), :] = (mean - src_ref[pl.ds(t0, rt), :]).astype(out_ref.dtype)

        loop(fill)
        loop(cols)
        loop(rows, edges=True)
    else:
        def fill(t0, edge):
            rowbuf[pl.ds(rm + t0, rt), :] = by_row_count(src_ref[pl.ds(t0, rt), :], t0, edge)

        def rows(t0):
            acc = row_sum(t0, -1)
            for g in range(gpt):
                colbuf[pl.ds(col_base(t0, g), gw), :] = acc[g * gw:(g + 1) * gw, :] * inv_ccnt

        def cols(t0):
            for g in range(gpt):
                rows_g = pl.ds(t0 + g * gw, gw)
                out_ref[rows_g, :] = (col_sum(t0, g, -1) - src_ref[rows_g, :]).astype(out_ref.dtype)

        loop(fill, edges=True)
        loop(rows)
        loop(cols)


def _pool_map(src, nb, transpose, out_chunk_major, name):
    s_len = src.shape[1]
    cpg = nb // len(POOL_WINDOWS)

    def body(src_ref, out_ref, colbuf, rowbuf):
        n = pl.program_id(0)
        for gi, w in enumerate(POOL_WINDOWS):
            @pl.when(n // cpg == gi)
            def _(w=w):
                _pool_windows(src_ref, out_ref, colbuf, rowbuf, w // 2, transpose, s_len)

    if out_chunk_major:
        out_spec = pl.BlockSpec((None, s_len, LANE), lambda n: (n, 0, 0))
        out_shape = jax.ShapeDtypeStruct((nb, s_len, LANE), BF16)
    else:
        out_spec = pl.BlockSpec((s_len, LANE), lambda n: (0, n))
        out_shape = jax.ShapeDtypeStruct((s_len, nb * LANE), BF16)
    return _call(
        body, name=name, grid=(nb,),
        in_specs=[pl.BlockSpec((None, s_len, LANE), lambda n: (n, 0, 0))],
        out_specs=out_spec, out_shape=out_shape,
        scratch_shapes=[pltpu.VMEM((s_len // GRID_W * (GRID_W + 2 * POOL_CPAD), LANE), F32),
                        pltpu.VMEM((s_len + 16 * GRID_W, LANE), F32)],
    )(src)


def _group_weight(w_ref):
    return jnp.concatenate([w_ref[k] for k in range(N_CHIPS)], axis=0)


def _pool_mm_fwd(dm, wp, scale):
    nb, rows, _ = dm.shape
    _, ng, pq, pg = wp.shape
    cpg = pg // LANE
    tm = _row_tile(rows, 2048)

    def body(d_ref, w_ref, s_ref, y_ref):
        _put_chunks(y_ref, _dot(_cat(d_ref, cpg), _group_weight(w_ref)) * s_ref[...], cpg)

    cspec = pl.BlockSpec((cpg, tm, LANE), lambda i, g: (g, i, 0))
    return _call(
        body, name="pool_mm_fwd", grid=(rows // tm, ng),
        in_specs=[cspec, pl.BlockSpec((N_CHIPS, None, pq, pg), lambda i, g: (0, g, 0, 0)),
                  pl.BlockSpec((1, pg), lambda i, g: (0, g))],
        out_specs=cspec, out_shape=jax.ShapeDtypeStruct((nb, rows, LANE), BF16),
    )(dm, wp, scale)


def _pool_mm_bwd(dy, dm, wp, scale, gbuf, row0):
    nb, rows, _ = dm.shape
    _, ng, pq, pg = wp.shape
    cpg = pg // LANE
    tm = _row_tile(rows, 1024)
    nt = rows // tm
    assert gbuf.shape[2] == 2 * pg and row0 % pq == 0

    def body(dy_ref, d_ref, w_ref, s_ref, buf_ref, dd_ref, dwp_ref, dsc_ref, acc):
        i = pl.program_id(1)

        @pl.when(i == 0)
        def _():
            acc[...] = jnp.zeros_like(acc)
            dsc_ref[...] = jnp.zeros_like(dsc_ref)

        dyv = _cat(dy_ref, cpg).astype(F32)
        dc = _cat(d_ref, cpg)
        w = _group_weight(w_ref)
        dsc_ref[...] += jnp.sum(dyv * _dot(dc, w), axis=0, keepdims=True)
        dyp = (dyv * s_ref[...]).astype(BF16)
        _put_chunks(dd_ref, _dot_nt(dyp, w), cpg)
        acc[...] += _dot_tn(dc, dyp)

        @pl.when(i == nt - 1)
        def _():
            dwp_ref[...] = acc[...].reshape(N_CHIPS, pq, pg)

    cspec = pl.BlockSpec((cpg, tm, LANE), lambda g, i: (g, i, 0))
    sspec = pl.BlockSpec((1, pg), lambda g, i: (0, g))
    return _call(
        body, name="pool_mm_bwd", grid=(ng, nt),
        in_specs=[cspec, cspec, pl.BlockSpec((N_CHIPS, None, pq, pg), lambda g, i: (0, g, 0, 0)), sspec, ANY_SPEC],
        out_specs=[cspec, pl.BlockSpec((N_CHIPS, pq, pg), lambda g, i: (0, row0 // pq + g // 2, g % 2)), sspec],
        out_shape=[jax.ShapeDtypeStruct((nb, rows, LANE), F32), jax.ShapeDtypeStruct(gbuf.shape, F32),
                   jax.ShapeDtypeStruct((1, ng * pg), F32)],
        scratch_shapes=[pltpu.VMEM((pg, pg), F32)],
        input_output_aliases={4: 1},
    )(dy, dm, wp, scale, gbuf)


def _adamw_math(w, g, m, v):
    nm = ADAM_B1 * m + (1.0 - ADAM_B1) * g
    nv = ADAM_B2 * v + (1.0 - ADAM_B2) * jnp.square(g)
    m_hat = nm / (1.0 - ADAM_B1 ** ADAM_STEP)
    v_hat = nv / (1.0 - ADAM_B2 ** ADAM_STEP)
    return -ADAM_LR * (m_hat / (jnp.sqrt(v_hat) + ADAM_EPS) + ADAM_WD * w), nm, nv


def _adamw_param(w3, m3, v3, gsrcs, pick, tm, name, after=None):
    n_blk, rows, cols = w3.shape
    ng = len(gsrcs)

    def body(*refs):
        w_ref, m_ref, v_ref = refs[:3]
        g_refs = refs[3:3 + ng]
        go_ref, d_ref, nm_ref, nv_ref = refs[-4:]
        g = pick(pl.program_id(0), [r[...] for r in g_refs])
        go_ref[...] = g
        d_ref[...], nm_ref[...], nv_ref[...] = _adamw_math(w_ref[...], g, m_ref[...], v_ref[...])

    spec = pl.BlockSpec((None, tm, cols), lambda n, i: (n, i, 0))
    extra = [] if after is None else [after]
    return _call(
        body, name=name, grid=(n_blk, rows // tm),
        in_specs=[spec] * 3 + [pl.BlockSpec(shape, imap) for _, shape, imap in gsrcs] + [ANY_SPEC] * len(extra),
        out_specs=[spec] * 4, out_shape=[jax.ShapeDtypeStruct(w3.shape, F32)] * 4,
    )(w3, m3, v3, *[a for a, _, _ in gsrcs], *extra)


def _adamw_small(quads):
    n = len(quads)

    def body(*refs):
        ins, outs = refs[:4 * n], refs[4 * n:]
        for k in range(n):
            w, g, m, v = (r[...] for r in ins[4 * k:4 * k + 4])
            outs[3 * k][...], outs[3 * k + 1][...], outs[3 * k + 2][...] = _adamw_math(w, g, m, v)

    flat = [a for q in quads for a in q]
    res = _call(body, name="adamw_small", grid=(1,),
                in_specs=[pl.BlockSpec(a.shape, lambda i: (0, 0)) for a in flat],
                out_specs=[pl.BlockSpec(q[0].shape, lambda i: (0, 0)) for q in quads for _ in range(3)],
                out_shape=[jax.ShapeDtypeStruct(q[0].shape, F32) for q in quads for _ in range(3)])(*flat)
    return [tuple(res[3 * k:3 * k + 3]) for k in range(n)]


def _place():
    return lax.axis_index("x"), lax.axis_index("y"), lax.axis_index("c")


def _other_chips(x, y):
    return [(1 - x, y), (x, 1 - y), (1 - x, 1 - y)]


def _own_slab(a, devices=False):
    x, y, c = _place()
    n, me = (N_DEV, 4 * x + 2 * y + c) if devices else (N_CHIPS, 2 * x + y)
    return lax.dynamic_update_slice(lax.empty((n,) + a.shape, a.dtype), a[None], (me, 0, 0))


def _gather_chips(arrays, name):
    n = len(arrays)
    halves = [a.shape[0] // 2 for a in arrays]
    for a, h in zip(arrays, halves):
        assert 2 * h == a.shape[0] and h % (32 // a.dtype.itemsize) == 0
    lands = [_own_slab(a) for a in arrays]

    def body(*refs):
        outs = refs[n:2 * n]
        send_sems, recv_sems = refs[2 * n:]
        x, y, c = _place()
        me = 2 * x + y
        chips = _other_chips(x, y)

        def mine(k):
            return pl.ds(c * halves[k], halves[k])

        def theirs(k):
            return pl.ds((1 - c) * halves[k], halves[k])

        def push(k, j, src, dst, to):
            return pltpu.make_async_remote_copy(src_ref=src, dst_ref=dst, send_sem=send_sems.at[6 * k + j],
                                                recv_sem=recv_sems.at[6 * k + j], device_id=to, device_id_type=MESH)

        started = []
        for j, (cx, cy) in enumerate(chips):
            for k in range(n):
                own = outs[k].at[me, mine(k)]
                cp = push(k, j, own, own, (cx, cy, c))
                cp.start()
                started.append(cp)
        for j, (cx, cy) in enumerate(chips):
            for k in range(n):
                slab = outs[k].at[2 * cx + cy, mine(k)]
                push(k, j, slab, slab, (x, y, c)).wait_recv()
                fwd = push(k, 3 + j, slab, slab, (x, y, 1 - c))
                fwd.start()
                started.append(fwd)
        for j, (cx, cy) in enumerate(chips):
            for k in range(n):
                slab = outs[k].at[2 * cx + cy, theirs(k)]
                push(k, 3 + j, slab, slab, (x, y, c)).wait_recv()
        for cp in started:
            cp.wait_send()

    return _call(
        body, name=name, in_specs=[ANY_SPEC] * n, out_specs=[ANY_SPEC] * n,
        out_shape=[jax.ShapeDtypeStruct(a.shape, a.dtype) for a in lands],
        input_output_aliases={k: k for k in range(n)},
        scratch_shapes=[pltpu.SemaphoreType.DMA((6 * n,)), pltpu.SemaphoreType.DMA((6 * n,))],
    )(*lands)


HBM_SPEC = pl.BlockSpec(memory_space=pltpu.HBM)
SEM_SPEC = pl.BlockSpec(memory_space=pltpu.SEMAPHORE)
SIDE_EFFECT = pltpu.SideEffectType.DATAFLOW_SIDE_EFFECTING


def _n_peers(kind):
    return N_DEV - 1 if kind == "devices" else N_CHIPS - 1


def _push_copies(src_refs, land_refs, send_sems, recv_sems, kind):
    x, y, c = _place()
    if kind == "devices":
        me = 4 * x + 2 * y + c
        peers = [((me + j) % N_DEV, None) for j in range(1, N_DEV)]
        peers = [((to // 4, (to // 2) % 2, to % 2), None) for to, _ in peers]
    else:
        me = 2 * x + y
        peers = [((cx, cy, c), 2 * cx + cy) for cx, cy in _other_chips(x, y)]
    n = len(peers)
    copies = []
    for j, (dev, slab) in enumerate(peers):
        for k, (src, land) in enumerate(zip(src_refs, land_refs)):
            copies.append(pltpu.make_async_remote_copy(
                src_ref=src.at[slab] if kind == "slab" else src, dst_ref=land.at[me], send_sem=send_sems.at[n * k + j],
                recv_sem=recv_sems.at[n * k + j], device_id=dev, device_id_type=MESH))
    return copies


def _push_start(srcs, lands, kind, after, name):
    n = len(srcs)

    def body(*refs):
        src_refs, land_refs = refs[:n], refs[n:2 * n]
        send_sems, recv_sems = refs[2 * n + 1], refs[2 * n + 2]
        token = refs[-1]
        for cp in _push_copies(src_refs, land_refs, send_sems, recv_sems, kind):
            cp.start()
        token[...] = jnp.zeros_like(token)

    bufs = [pltpu.with_memory_space_constraint(a, pltpu.HBM) for a in list(srcs) + list(lands)]
    res = _call(
        body, name=name,
        out_shape=[pltpu.SemaphoreType.DMA((_n_peers(kind) * n,)), pltpu.SemaphoreType.DMA((_n_peers(kind) * n,))]
        + [pltpu.HBM(a.shape, a.dtype) for a in bufs] + [jax.ShapeDtypeStruct((SUB, LANE), F32)],
        in_specs=[HBM_SPEC] * (2 * n) + [ANY_SPEC],
        out_specs=[SEM_SPEC, SEM_SPEC] + [HBM_SPEC] * (2 * n) + [pl.BlockSpec(memory_space=pltpu.VMEM)],
        input_output_aliases={i: 2 + i for i in range(2 * n)},
        compiler_params=pltpu.CompilerParams(has_side_effects=SIDE_EFFECT),
    )(*bufs, after)
    return res[0], res[1], list(res[2:2 + n]), list(res[2 + n:2 + 2 * n]), res[-1]


def _push_wait(send_sems, recv_sems, srcs, lands, kind, after, name):
    n = len(srcs)

    def body(*refs):
        src_refs, land_refs = refs[:n], refs[n:2 * n]
        send_sems, recv_sems = refs[2 * n], refs[2 * n + 1]
        for cp in _push_copies(src_refs, land_refs, send_sems, recv_sems, kind):
            cp.wait_send()
            cp.wait_recv()

    res = _call(
        body, name=name,
        out_shape=[pltpu.HBM(a.shape, a.dtype) for a in list(srcs) + list(lands)],
        in_specs=[HBM_SPEC] * (2 * n) + [SEM_SPEC, SEM_SPEC, ANY_SPEC],
        out_specs=[HBM_SPEC] * (2 * n),
        input_output_aliases={i: i for i in range(2 * n)},
        compiler_params=pltpu.CompilerParams(has_side_effects=SIDE_EFFECT),
    )(*srcs, *lands, send_sems, recv_sems, after)
    return list(res[n:])


def _sibling_swap(g):
    _, rows, w = g.shape
    half = rows // 2

    def body(g_ref, out_ref, send_sem, recv_sem):
        x, y, c = _place()
        cp = pltpu.make_async_remote_copy(src_ref=g_ref.at[:, pl.ds((1 - c) * half, half)], dst_ref=out_ref,
                                          send_sem=send_sem, recv_sem=recv_sem, device_id=(x, y, 1 - c), device_id_type=MESH)
        cp.start()
        cp.wait()

    return _call(body, name="rs_sibling_swap", in_specs=[ANY_SPEC], out_specs=ANY_SPEC,
                 out_shape=jax.ShapeDtypeStruct((N_CHIPS, half, w), F32),
                 scratch_shapes=[pltpu.SemaphoreType.DMA, pltpu.SemaphoreType.DMA])(g)


def _pair_add(g, got, place):
    _, rows, w = g.shape
    half = rows // 2
    tm = _row_tile(half, RS_TILE)
    nt = half // tm

    def body(p_ref, a_ref, b_ref, o_ref, own_ref):
        v = a_ref[...] + b_ref[...]
        o_ref[...] = v.astype(BF16)

        @pl.when(pl.program_id(1) == p_ref[1])
        def _():
            own_ref[...] = v

    return _call(
        body, name="rs_pair_add",
        grid_spec=pltpu.PrefetchScalarGridSpec(
            num_scalar_prefetch=1, grid=(nt, N_CHIPS),
            in_specs=[pl.BlockSpec((None, tm, w), lambda i, s, p: (s, p[0] * nt + i, 0)),
                      pl.BlockSpec((None, tm, w), lambda i, s, p: (s, i, 0))],
            out_specs=[pl.BlockSpec((None, tm, w), lambda i, s, p: (s, i, 0)),
                       pl.BlockSpec((tm, w), lambda i, s, p: (i, 0))]),
        out_shape=[jax.ShapeDtypeStruct((N_CHIPS, half, w), BF16), jax.ShapeDtypeStruct((half, w), F32)],
    )(place, g, got)


def _sum_chips(parts, own, place):
    _, half, w = parts.shape
    tm = _row_tile(half, RS_TILE)
    nt = half // tm

    def body(p_ref, parts_ref, own_ref, o_ref):
        me = p_ref[1]
        t = [jnp.where(me == q, own_ref[...], parts_ref[q].astype(F32)) for q in range(N_CHIPS)]
        o_ref[...] = (t[0] + t[1]) + (t[2] + t[3])

    return _call(
        body, name="rs_sum_chips",
        grid_spec=pltpu.PrefetchScalarGridSpec(
            num_scalar_prefetch=1, grid=(nt,),
            in_specs=[pl.BlockSpec((N_CHIPS, tm, w), lambda i, p: (0, i, 0)), pl.BlockSpec((tm, w), lambda i, p: (i, 0))],
            out_specs=pl.BlockSpec((tm, w), lambda i, p: (p[0] * nt + i, 0))),
        out_shape=jax.ShapeDtypeStruct((2 * half, w), F32),
    )(place, parts, own)


def _sibling_gather(red):
    rows, w = red.shape
    half = rows // 2

    def body(in_ref, out_ref, send_sem, recv_sem):
        x, y, c = _place()
        mine = out_ref.at[pl.ds(c * half, half)]
        cp = pltpu.make_async_remote_copy(src_ref=mine, dst_ref=mine, send_sem=send_sem, recv_sem=recv_sem,
                                          device_id=(x, y, 1 - c), device_id_type=MESH)
        cp.start()
        other = out_ref.at[pl.ds((1 - c) * half, half)]
        pltpu.make_async_remote_copy(src_ref=other, dst_ref=other, send_sem=send_sem, recv_sem=recv_sem,
                                     device_id=(x, y, c), device_id_type=MESH).wait_recv()
        cp.wait_send()

    return _call(body, name="rs_sibling_gather", in_specs=[ANY_SPEC], out_specs=ANY_SPEC,
                 out_shape=jax.ShapeDtypeStruct(red.shape, F32), input_output_aliases={0: 0},
                 scratch_shapes=[pltpu.SemaphoreType.DMA, pltpu.SemaphoreType.DMA])(red)


def _rs_begin(g, place, name):
    pair, own = _pair_add(g, _sibling_swap(g), place)
    send, recv, pair, parts, token = _push_start([pair], [jnp.zeros_like(pair)], "slab", own, name + "_start")
    return (send, recv, pair, parts, own), token


def _rs_end(state, place, after, name):
    send, recv, pair, parts, own = state
    (parts,) = _push_wait(send, recv, pair, parts, "slab", after, name + "_wait")
    return _sibling_gather(_sum_chips(parts, own, place))


WEIGHTS = ("c_ctx", "w_mod", "b_mod", "w_in", "w_out", "ln_g", "ln_b", "conv_w", "conv_b", "lru_wa", "lru_ba", "lru_wx",
           "lru_bx", "lru_lam", "pool_w", "pool_scale")
SMALL_GATHERED = ("conv_w", "lru_ba", "lru_bx", "lru_lam", "pool_scale")
SMALL_UPDATED = ("c_ctx", "b_mod", "ln_g", "ln_b", "conv_w", "conv_b", "lru_ba", "lru_bx", "lru_lam", "pool_scale")


def kernel(x, c, ctx, c_ctx, w_mod, b_mod, w_in, w_out, ln_g, ln_b, conv_w, conv_b, lru_wa, lru_ba, lru_wx, lru_bx, lru_lam, pool_w, pool_scale, loss_target, m_c_ctx, m_w_mod, m_b_mod, m_w_in, m_w_out, m_ln_g, m_ln_b, m_conv_w, m_conv_b, m_lru_wa, m_lru_ba, m_lru_wx, m_lru_bx, m_lru_lam, m_pool_w, m_pool_scale, v_c_ctx, v_w_mod, v_b_mod, v_w_in, v_w_out, v_ln_g, v_ln_b, v_conv_w, v_conv_b, v_lru_wa, v_lru_ba, v_lru_wx, v_lru_bx, v_lru_lam, v_pool_w, v_pool_scale):
    weights = dict(c_ctx=c_ctx, w_mod=w_mod, b_mod=b_mod, w_in=w_in, w_out=w_out, ln_g=ln_g, ln_b=ln_b, conv_w=conv_w,
                   conv_b=conv_b, lru_wa=lru_wa, lru_ba=lru_ba, lru_wx=lru_wx, lru_bx=lru_bx, lru_lam=lru_lam,
                   pool_w=pool_w, pool_scale=pool_scale)
    mom1 = dict(c_ctx=m_c_ctx, w_mod=m_w_mod, b_mod=m_b_mod, w_in=m_w_in, w_out=m_w_out, ln_g=m_ln_g, ln_b=m_ln_b,
                conv_w=m_conv_w, conv_b=m_conv_b, lru_wa=m_lru_wa, lru_ba=m_lru_ba, lru_wx=m_lru_wx, lru_bx=m_lru_bx,
                lru_lam=m_lru_lam, pool_w=m_pool_w, pool_scale=m_pool_scale)
    mom2 = dict(c_ctx=v_c_ctx, w_mod=v_w_mod, b_mod=v_b_mod, w_in=v_w_in, w_out=v_w_out, ln_g=v_ln_g, ln_b=v_ln_b,
                conv_w=v_conv_w, conv_b=v_conv_b, lru_wa=v_lru_wa, lru_ba=v_lru_ba, lru_wx=v_lru_wx, lru_bx=v_lru_bx,
                lru_lam=v_lru_lam, pool_w=v_pool_w, pool_scale=v_pool_scale)
    xs, cx, target = x[0], ctx[0], loss_target[0]
    s_len, d = xs.shape
    es = w_out.shape[1]
    e = es * N_CHIPS
    nb = e // LANE
    c3 = w_mod.shape[2]
    n4 = w_in.shape[2]
    pq, pg = pool_w.shape[2], pool_w.shape[3]
    ng = len(POOL_WINDOWS)
    width = n4
    assert width == d and 2 * pg == width and 2 * nb * LANE == N_CHIPS * width and d % (2 * N_CHIPS) == 0
    px, py, pc = _place()
    place = jnp.stack([pc, 2 * px + py]).astype(jnp.int32)
    cctx2 = c_ctx[None, :]

    eq = e // N_CHIPS
    small_rows = [(conv_w[0], 0), (lru_ba[0], CONV_TAPS), (lru_bx[0], CONV_TAPS + 2), (lru_lam[0], CONV_TAPS + 4),
                  (pool_scale, CONV_TAPS + 6)]
    small = _rows_kernel([(a, r, 0) for a, r in small_rows], 2 * SUB, eq, "pack_small_weights")
    win0, sg = _gather_chips([w_in[0].astype(BF16), small], "gather_weights0")
    full = {n: jnp.swapaxes(sg[:, r:r + a.shape[0]], 0, 1).reshape(a.shape[0], e)
            for n, (a, r) in zip(SMALL_GATHERED, small_rows)}
    wa_h, wx_h = (0.5 * lru_wa[0]).astype(BF16), (0.5 * lru_wx[0]).astype(BF16)
    lru_args = (full["conv_w"], conv_b, wa_h, wx_h, 0.5 * full["lru_ba"], 0.5 * full["lru_bx"], full["lru_lam"])
    scale_f = full["pool_scale"]

    me8 = 4 * px + 2 * py + pc
    c_rows = _rows_kernel([(c, 0, 0)], SUB, d, "pack_c")
    c_send, c_recv, c_src, c_lands, c_token = _push_start([c_rows], [_own_slab(c_rows, devices=True)], "devices", sg,
                                                          "gather_c_start")
    (c_dev,) = _push_wait(c_send, c_recv, c_src, c_lands, "devices", c_token, "gather_c_wait")
    wm_mine = w_mod.astype(BF16)
    bm_mine = lax.dynamic_slice_in_dim(b_mod[:, None, :], place[1] * c3, c3, axis=2)
    (mod_g,) = _gather_chips([_mod_fwd(c_dev[:, 0, :], cctx2, wm_mine, bm_mine).reshape(DEPTH * 2 * SUB, c3)], "gather_mod")
    mod_all = jnp.transpose(mod_g.reshape(N_CHIPS, DEPTH, 2 * SUB, c3), (1, 2, 0, 3)).reshape(DEPTH, 2 * SUB, 3 * d)
    mod_mine = lax.dynamic_index_in_dim(mod_all, me8, axis=1, keepdims=False)
    later = [w_out[0].astype(BF16), w_in[1].astype(BF16), w_out[1].astype(BF16), pool_w.astype(BF16).reshape(ng * pq, pg)]
    w_send, w_recv, later, later_lands, w_token = _push_start(
        later, [_own_slab(a) for a in later], "same", mod_g, "gather_weights1_start")
    mod_mine = mod_mine + w_token[0:1, 0:1]

    def mod_parts(v):
        return v[None, :d], 1.0 + v[None, d:2 * d], v[None, 2 * d:]

    sh0, sc0, gt0 = mod_parts(mod_mine[0])
    shc, scc, _ = mod_parts(mod_all[0, N_DEV])
    sh1, sc1, gt1 = mod_parts(mod_mine[1])
    lg = [ln_g[l][None, :] for l in range(DEPTH)]
    lb = [ln_b[l][None, :] for l in range(DEPTH)]

    uu0, ug0 = _inproj_fwd(xs, sc0, sh0, win0, "inproj_fwd0")
    uc0 = _inproj_fwd(cx, scc, shc, win0[:2], "inproj_fwd_ctx")
    y0 = _rglru_fwd(uu0, uc0, *lru_args)
    wout0_g, win1, wout1_g, wp_g = _push_wait(w_send, w_recv, later, later_lands, "same", y0, "gather_weights1_wait")
    win = [win0, win1]
    wout = [wout0_g.reshape(e, d), wout1_g.reshape(e, d)]
    wp = wp_g.reshape(N_CHIPS, ng, pq, pg)
    br0, x1 = _outproj_fwd(y0, ug0, xs, gt0, wout[0], lg[0], lb[0], None, "outproj_fwd0")
    uu1, ug1 = _inproj_fwd(x1, sc1, sh1, win[1], "inproj_fwd1")
    d1 = _pool_map(uu1, nb, False, True, "pool_fwd")
    y1 = _pool_mm_fwd(d1, wp, scale_f)
    br1, dxo, loss_part = _outproj_fwd(y1, ug1, x1, gt1, wout[1], lg[1], lb[1], target, "outproj_fwd1")

    row_wout = d
    row_tail = d + es
    wq = 2 * (nb // N_CHIPS) * LANE * LANE // width
    whole = lambda r: (r + 2 * RS_TILE - 1) // (2 * RS_TILE) * (2 * RS_TILE)
    rows1 = whole(row_tail + pg // 2)
    rows0 = whole(row_tail + 2 * wq)
    fresh = lambda rows, used: (lax.empty if rows == used else jnp.zeros)((N_CHIPS, rows, width), F32)
    gbuf1 = fresh(rows1, row_tail + pg // 2)
    gbuf0 = fresh(rows0, row_tail + 2 * wq)

    dy1, dg1, dxres1, dbr1, dlg1, dlb1, dgt1 = _outproj_bwd(dxo, x1, br1, y1, ug1, gt1, lg[1], wout[1], "outproj_bwd1",
                                                            dy_dtype=BF16)
    gbuf1 = _outproj_bwd_w(y1, ug1, dbr1, gbuf1, row_wout, "outproj_bwd_w1")
    dd1, gbuf1, dscale = _pool_mm_bwd(dy1, d1, wp, scale_f, gbuf1, row_tail)
    du1 = _pool_map(dd1, nb, True, False, "pool_bwd")
    dx1, dsc1, dsh1 = _inproj_bwd_x([du1, dg1], x1, dxres1, sc1, win[1], "inproj_bwd_x1")
    gbuf1 = _inproj_bwd_w(x1, sc1, sh1, [du1, dg1], None, gbuf1, "inproj_bwd_w1")
    rs1, token1 = _rs_begin(gbuf1, place, "rs_exchange1")

    dy0, dg0, dxres0, dbr0, dlg0, dlb0, dgt0 = _outproj_bwd(dx1, xs, br0, y0, ug0, gt0 + token1[0:1, 0:1], lg[0], wout[0],
                                                            "outproj_bwd0")
    gbuf0 = _outproj_bwd_w(y0, ug0, dbr0, gbuf0, row_wout, "outproj_bwd_w0")
    du0, duc, dconv_w, dconv_b, dwa, dwx, dba, dbx, dlam = _rglru_bwd(uu0, uc0, dy0, *lru_args)
    dwin0c = _inproj_bwd_w(cx, scc, shc, [duc, jnp.zeros_like(duc)], None, None, "inproj_bwd_w_ctx")
    gbuf0 = _inproj_bwd_w(xs, sc0, sh0, [du0, dg0], dwin0c, gbuf0, "inproj_bwd_w0")

    def quarter(dw):
        t = dw.reshape(2, N_CHIPS, nb // N_CHIPS, LANE, LANE)
        return jnp.transpose(t, (1, 3, 0, 2, 4)).reshape(N_CHIPS, LANE, 2 * (nb // N_CHIPS) * LANE).reshape(N_CHIPS, wq, width)

    tail0 = jnp.concatenate([quarter(dwa), quarter(dwx)], axis=1)
    gbuf0 = lax.dynamic_update_slice(gbuf0, tail0, (0, row_tail, 0))
    red1 = _rs_end(rs1, place, gbuf0, "rs_exchange1")
    rs0, token0 = _rs_begin(gbuf0, place, "rs_exchange0")
    grad_x, dsc0, dsh0 = _inproj_bwd_x([du0, dg0], xs, dxres0, sc0 + token0[0:1, 0:1], win[0], "inproj_bwd_x0")
    dscc, dshc = _inproj_bwd_x([duc], cx, None, scc, win[0][:2], "inproj_bwd_x_ctx")

    k0 = VEC_KINDS
    vec = _rows_kernel(
        [(c, 0, 0), (loss_part, 0, d), (dsh0, 1, 0), (dsc0, 1, d), (dgt0, 1, 2 * d), (dshc, 2, 0), (dscc, 2, d),
         (dsh1, 3, 0), (dsc1, 3, d), (dgt1, 3, 2 * d),
         (dconv_b, k0, 0), (dlg0, k0, e), (dscale, k0 + 1, 0), (dlg1, k0 + 1, e), (dlb0, k0 + 2, 0), (dlb1, k0 + 2, d),
         (dconv_w, k0 + 3, 0), (dba, k0 + 7, 0), (dbx, k0 + 9, 0), (dlam, k0 + 11, 0)], VEC_ROWS, 3 * d, "pack_vec")
    v_send, v_recv, vec_l, vec_lands, v_token = _push_start([vec], [_own_slab(vec, devices=True)], "devices", vec,
                                                            "gather_devices_start")
    red0 = _rs_end(rs0, place, v_token, "rs_exchange0")
    quarters = red0[row_tail:row_tail + 2 * wq]
    q_send, q_recv, q_src, q_lands, q_token = _push_start([quarters], [_own_slab(quarters)], "same", red0,
                                                          "gather_replicated_start")

    tmw = _row_tile(d, 256)
    red_src = lambda red, r0, tm: (red, (tm, width), lambda n, i: (r0 // tm + i, 0))
    by_layer = lambda n, gs: jnp.where(n == 0, gs[0], gs[1])
    outs = {}
    outs["w_in"] = _adamw_param(w_in, m_w_in, v_w_in, [red_src(red0, 0, tmw), red_src(red1, 0, tmw)], by_layer, tmw, "adamw_w_in",
                                after=q_token)
    outs["w_out"] = _adamw_param(w_out, m_w_out, v_w_out, [red_src(red0, row_wout, tmw), red_src(red1, row_wout, tmw)],
                                 by_layer, tmw, "adamw_w_out")
    pw = [a.reshape(ng, pq, pg) for a in (pool_w, m_pool_w, v_pool_w)]
    outs["pool_w"] = [o.reshape(pool_w.shape) for o in _adamw_param(
        *pw, [(red1, (pq, pg), lambda n, i: (row_tail // pq + n // 2, n % 2))], lambda n, gs: gs[0], pq, "adamw_pool_w")]

    (gathered,) = _push_wait(v_send, v_recv, vec_l, vec_lands, "devices", outs["w_out"][1], "gather_devices_wait")
    gt_all = jnp.swapaxes(gathered, 0, 1)
    g_wmod = _mod_bwd_shard(gt_all, cctx2, place, c3)
    g_bmod, sq_err, g_small = _mod_bwd_rep(gt_all, d)
    loss = sq_err[0, 0] * (0.5 / d)
    cpart = _cctx_partial(gt_all, wm_mine[0], place)
    x_send, x_recv, x_src, x_lands, x_token = _push_start([cpart], [_own_slab(cpart)], "same", cpart, "gather_cctx_start")
    outs["w_mod"] = _adamw_param(w_mod, m_w_mod, v_w_mod, [(g_wmod, (None, tmw, c3), lambda n, i: (n, i, 0))],
                                 lambda n, gs: gs[0], tmw, "adamw_w_mod", after=x_token)
    (rep,) = _push_wait(q_send, q_recv, q_src, q_lands, "same", outs["w_mod"][1], "gather_replicated_wait")
    bq = nb // N_CHIPS
    rep_src = lambda r0: (rep, (None, LANE, bq * LANE), lambda n, i: (n % N_CHIPS, r0 // LANE, n // N_CHIPS))
    stack = lambda n, gs: jnp.concatenate([gs[0][:, k * LANE:(k + 1) * LANE] for k in range(bq)], axis=0)
    for name, r0, trio in (("lru_wa", 0, (lru_wa, m_lru_wa, v_lru_wa)), ("lru_wx", wq, (lru_wx, m_lru_wx, v_lru_wx))):
        blocks = [a.reshape(2 * N_CHIPS, bq * LANE, LANE) for a in trio]
        outs[name] = [o.reshape(lru_wa.shape) for o in _adamw_param(*blocks, [rep_src(r0)], stack, bq * LANE, "adamw_" + name)]

    (cparts,) = _push_wait(x_send, x_recv, x_src, x_lands, "same", outs["lru_wx"][1], "gather_cctx_wait")
    g_small = dict(g_small, c_ctx=_cctx_finish(cparts, cctx2), b_mod=g_bmod)
    for n in SMALL_GATHERED:
        g_small[n] = lax.dynamic_slice_in_dim(g_small[n], place[1] * eq, eq, axis=1)
    as2d = lambda a: a.reshape(-1, a.shape[-1])
    quads = [(as2d(weights[n]), g_small[n], as2d(mom1[n]), as2d(mom2[n])) for n in SMALL_UPDATED]
    for n, (q, res) in zip(SMALL_UPDATED, zip(quads, _adamw_small(quads))):
        outs[n] = [a.reshape(weights[n].shape) for a in (q[1],) + res]

    result = [loss, grad_x[None]]
    for j in range(4):
        result += [outs[n][j] for n in WEIGHTS]
    return tuple(result)
```

```python
import jax
import jax.numpy as jnp
from jax import lax
from jax.experimental import pallas as pl
from jax.experimental.pallas import tpu as pltpu

F32 = jnp.float32
BF16 = jnp.bfloat16
LANE = 128
SUB = 8
GRID_W = 64
POOL_WINDOWS = (2, 4, 8, 16)
LRU_C = 8.0
DEPTH = 2
ALPHA = float((2 * DEPTH) ** 0.25)
LN_EPS = 1e-5
ADAM_LR, ADAM_B1, ADAM_B2, ADAM_EPS, ADAM_WD, ADAM_STEP = 0.001, 0.9, 0.999, 1e-08, 0.01, 10
N_CHIPS = 4
N_DEV = 8
MESH = pl.DeviceIdType.MESH
ROW_TILE = 512
GATE_TILE = 8192
GATE_BWD_TILE = 2048
GATE_UNROLL = 1
CONV_TAPS = 4
CONV_LEFT = 2
PAD = 8
SCAN_UNROLL = 32
RS_TILE = 448
LN_ROWS = 128
POOL_CPAD = 16
VEC_KINDS = 4


def _call(body, **kw):
    return pl.pallas_call(body, **kw)


def _dot(a, b):
    return jnp.dot(a, b, preferred_element_type=F32)


def _dot_nt(a, b):
    return lax.dot_general(a, b, (((1,), (1,)), ((), ())), preferred_element_type=F32)


def _dot_tn(a, b):
    return lax.dot_general(a, b, (((0,), (0,)), ((), ())), preferred_element_type=F32)


def _sigmoid(v):
    return 0.5 * (jnp.tanh(0.5 * v) + 1.0)


def _silu(v):
    return v * _sigmoid(v)


def _dsilu(v):
    s = _sigmoid(v)
    return s * (1.0 + v * (1.0 - s))


def _log_sigmoid(v):
    z = jnp.exp(-jnp.abs(v))
    return jnp.minimum(v, 0.0) - jnp.where(z < 1e-4, z * (1.0 - 0.5 * z), jnp.log(1.0 + z))


def _one_minus_sq(la, a):
    return jnp.tanh(la) * (-1.0 - a * a)


def _cat(ref, n):
    return jnp.concatenate([ref[k] for k in range(n)], axis=1)


def _put_chunks(ref, val, n, base=0):
    for k in range(n):
        ref[base + k] = val[:, k * LANE:(k + 1) * LANE].astype(ref.dtype)


def _row_tile(rows, want):
    t = min(rows, want)
    assert rows % t == 0
    return t


ANY_SPEC = pl.BlockSpec(memory_space=pl.ANY)


def _mod_fwd(c_all, cctx, wm, bm):
    nl, d, c3 = wm.shape

    def body(c_ref, cx_ref, w_ref, b_ref, o_ref):
        cc = jnp.concatenate([c_ref[...], cx_ref[...], jnp.zeros((SUB - 1, d), F32)], axis=0)
        o_ref[...] = _dot(_silu(cc).astype(BF16), w_ref[...]) + b_ref[...]

    return _call(
        body, name="mod_fwd", grid=(nl,),
        in_specs=[pl.BlockSpec((N_DEV, d), lambda l: (0, 0)),
                  pl.BlockSpec((1, d), lambda l: (0, 0)),
                  pl.BlockSpec((None, d, c3), lambda l: (l, 0, 0)),
                  pl.BlockSpec((None, 1, c3), lambda l: (l, 0, 0))],
        out_specs=pl.BlockSpec((None, 2 * SUB, c3), lambda l: (l, 0, 0)),
        out_shape=jax.ShapeDtypeStruct((nl, 2 * SUB, c3), F32),
    )(c_all, cctx, wm, bm)


def _rows_kernel(parts, rows, cols, name):
    def body(*refs):
        o_ref = refs[-1]
        o_ref[...] = jnp.zeros_like(o_ref)
        for ref, (a, r0, c0) in zip(refs[:-1], parts):
            for k in range(a.shape[0]):
                o_ref[r0 + k:r0 + k + 1, c0:c0 + a.shape[1]] = ref[k:k + 1, :]

    return _call(body, name=name, grid=(1,),
                 in_specs=[pl.BlockSpec(a.shape, lambda i: (0, 0)) for a, _, _ in parts],
                 out_specs=pl.BlockSpec((rows, cols), lambda i: (0, 0)),
                 out_shape=jax.ShapeDtypeStruct((rows, cols), F32))(*[a for a, _, _ in parts])


def _mod_bwd_shard(gt, cctx, place, c3):
    d = cctx.shape[1]

    def body(p_ref, cs_ref, dm_ref, dmx_ref, cx_ref, o_ref):
        l = pl.program_id(0)
        lhs = jnp.concatenate([_silu(cs_ref[...]), _silu(cx_ref[...]), jnp.zeros((7, d), F32)], axis=0).astype(BF16)
        dmx = jnp.where(l == 0, jnp.sum(dmx_ref[...], axis=0, keepdims=True), 0.0)
        rhs = jnp.concatenate([dm_ref[...], dmx, jnp.zeros((7, c3), F32)], axis=0).astype(BF16)
        o_ref[...] = _dot_tn(lhs, rhs)

    return _call(
        body, name="mod_bwd_shard",
        grid_spec=pltpu.PrefetchScalarGridSpec(
            num_scalar_prefetch=1, grid=(DEPTH,),
            in_specs=[pl.BlockSpec((None, N_DEV, d), lambda l, p: (0, 0, 0)),
                      pl.BlockSpec((None, N_DEV, c3), lambda l, p: (1 + 2 * l, 0, p[1])),
                      pl.BlockSpec((None, N_DEV, c3), lambda l, p: (2, 0, p[1])),
                      pl.BlockSpec((1, d), lambda l, p: (0, 0))],
            out_specs=pl.BlockSpec((None, d, c3), lambda l, p: (l, 0, 0))),
        out_shape=jax.ShapeDtypeStruct((DEPTH, d, c3), F32),
    )(place, gt, gt, gt, cctx)


def _small_layout(d, e):
    k = VEC_KINDS
    return {
        "conv_b": ((1, e), [(0, k, 0)]),
        "ln_g": ((2, d), [(0, k, e), (1, k + 1, e)]),
        "pool_scale": ((1, e), [(0, k + 1, 0)]),
        "ln_b": ((2, d), [(0, k + 2, 0), (1, k + 2, d)]),
        "conv_w": ((CONV_TAPS, e), [(t, k + 3 + t, 0) for t in range(CONV_TAPS)]),
        "lru_ba": ((2, e), [(j, k + 7 + j, 0) for j in range(2)]),
        "lru_bx": ((2, e), [(j, k + 9 + j, 0) for j in range(2)]),
        "lru_lam": ((2, e), [(j, k + 11 + j, 0) for j in range(2)]),
    }


VEC_ROWS = 24


def _mod_bwd_rep(gt, d):
    d3 = gt.shape[2]
    layout = _small_layout(d, d3 - d)
    names = list(layout)

    def body(g_ref, db_ref, loss_ref, *small_refs):
        loss_ref[...] = jnp.zeros_like(loss_ref) + jnp.sum(g_ref[0][:, d:d + LANE])
        dm0 = jnp.sum(g_ref[1], axis=0, keepdims=True)
        dmx = jnp.sum(g_ref[2], axis=0, keepdims=True)
        dm1 = jnp.sum(g_ref[3], axis=0, keepdims=True)
        db_ref[0:1, :] = dm0 + dmx
        db_ref[1:2, :] = dm1
        for ref, name in zip(small_refs, names):
            shape, places = layout[name]
            for arr_row, vec_row, col0 in places:
                total = jnp.sum(g_ref[vec_row], axis=0, keepdims=True)
                ref[arr_row:arr_row + 1, :] = total[:, col0:col0 + shape[1]]

    outs = _call(
        body, name="mod_bwd_rep", grid=(1,),
        in_specs=[pl.BlockSpec(gt.shape, lambda i: (0, 0, 0))],
        out_specs=[pl.BlockSpec((DEPTH, d3), lambda i: (0, 0)), pl.BlockSpec((1, LANE), lambda i: (0, 0))]
        + [pl.BlockSpec(layout[n][0], lambda i: (0, 0)) for n in names],
        out_shape=[jax.ShapeDtypeStruct((DEPTH, d3), F32), jax.ShapeDtypeStruct((1, LANE), F32)]
        + [jax.ShapeDtypeStruct(layout[n][0], F32) for n in names],
    )(gt)
    return outs[0], outs[1], dict(zip(names, outs[2:]))


def _cctx_partial(gt, wm0, place):
    d, c3 = wm0.shape

    def body(p_ref, dmx_ref, w_ref, o_ref):
        dmx = jnp.sum(dmx_ref[...], axis=0, keepdims=True)
        o_ref[...] = _dot_nt(jnp.broadcast_to(dmx, (2 * SUB, c3)).astype(BF16), w_ref[...])

    return _call(
        body, name="cctx_partial",
        grid_spec=pltpu.PrefetchScalarGridSpec(
            num_scalar_prefetch=1, grid=(1,),
            in_specs=[pl.BlockSpec((None, N_DEV, c3), lambda i, p: (2, 0, p[1])), pl.BlockSpec((d, c3), lambda i, p: (0, 0))],
            out_specs=pl.BlockSpec((2 * SUB, d), lambda i, p: (0, 0))),
        out_shape=jax.ShapeDtypeStruct((2 * SUB, d), F32),
    )(place, gt, wm0)


def _cctx_finish(parts, cctx):
    d = cctx.shape[1]

    def body(p_ref, cx_ref, o_ref):
        total = (p_ref[0, 0:1, :] + p_ref[1, 0:1, :]) + (p_ref[2, 0:1, :] + p_ref[3, 0:1, :])
        o_ref[...] = total * _dsilu(cx_ref[...])

    return _call(body, name="cctx_finish", grid=(1,),
                 in_specs=[pl.BlockSpec(parts.shape, lambda i: (0, 0, 0)), pl.BlockSpec((1, d), lambda i: (0, 0))],
                 out_specs=pl.BlockSpec((1, d), lambda i: (0, 0)),
                 out_shape=jax.ShapeDtypeStruct((1, d), F32))(parts, cctx)


def _inproj_fwd(xin, sc1, sh, w, name):
    rows, d = xin.shape
    ns, _, n4 = w.shape
    cpb = n4 // LANE
    tm = _row_tile(rows, 512)
    assert ns in (2, 4)

    def body(x_ref, sc_ref, sh_ref, w_ref, *o_refs):
        h = (x_ref[...] * sc_ref[...] + sh_ref[...]).astype(BF16)
        for s in range(ns):
            _put_chunks(o_refs[s // 2], _dot(h, w_ref[s]), cpb, base=(s % 2) * cpb)

    spec = pl.BlockSpec((2 * cpb, tm, LANE), lambda i: (0, i, 0))
    dtypes = (F32, BF16)[:ns // 2]
    res = _call(
        body, name=name, grid=(rows // tm,),
        in_specs=[pl.BlockSpec((tm, d), lambda i: (i, 0)),
                  pl.BlockSpec((1, d), lambda i: (0, 0)),
                  pl.BlockSpec((1, d), lambda i: (0, 0)),
                  pl.BlockSpec((ns, d, n4), lambda i: (0, 0, 0))],
        out_specs=[spec] * len(dtypes),
        out_shape=[jax.ShapeDtypeStruct((2 * cpb, rows, LANE), t) for t in dtypes],
    )(xin, sc1, sh, w)
    return res[0] if ns == 2 else tuple(res)


def _inproj_bwd_x(dparts, xin, dxres, sc1, w, name):
    rows, d = xin.shape
    npart = len(dparts)
    e = dparts[0].shape[1]
    ns, _, n4 = w.shape
    per = e // n4
    assert per * npart == ns
    tm = _row_tile(rows, 512)
    has_res = dxres is not None

    def body(*refs):
        dp = refs[:npart]
        x_ref, sc_ref, w_ref = refs[npart:npart + 3]
        rest = refs[npart + 3:]
        if has_res:
            res_ref, dx_ref, dsc_ref, dsh_ref = rest
        else:
            dsc_ref, dsh_ref = rest
        i = pl.program_id(0)
        dh = jnp.zeros((tm, d), F32)
        for p in range(npart):
            v = dp[p][...]
            for q in range(per):
                dh = dh + _dot_nt(v[:, q * n4:(q + 1) * n4], w_ref[p * per + q])

        @pl.when(i == 0)
        def _():
            dsc_ref[...] = jnp.zeros_like(dsc_ref)
            dsh_ref[...] = jnp.zeros_like(dsh_ref)

        dsc_ref[...] += jnp.sum(dh * x_ref[...], axis=0, keepdims=True)
        dsh_ref[...] += jnp.sum(dh, axis=0, keepdims=True)
        if has_res:
            dx_ref[...] = res_ref[...] + dh * sc_ref[...]

    row_spec = pl.BlockSpec((tm, d), lambda i: (i, 0))
    vec_spec = pl.BlockSpec((1, d), lambda i: (0, 0))
    in_specs = [pl.BlockSpec((tm, e), lambda i: (i, 0))] * npart + [row_spec, vec_spec,
                                                                     pl.BlockSpec((ns, d, n4), lambda i: (0, 0, 0))]
    args = list(dparts) + [xin, sc1, w]
    out_specs, out_shape = [vec_spec, vec_spec], [jax.ShapeDtypeStruct((1, d), F32)] * 2
    if has_res:
        in_specs.append(row_spec)
        args.append(dxres)
        out_specs = [row_spec] + out_specs
        out_shape = [jax.ShapeDtypeStruct((rows, d), F32)] + out_shape
    return _call(body, name=name, grid=(rows // tm,), in_specs=in_specs, out_specs=out_specs, out_shape=out_shape)(*args)


def _inproj_bwd_w(xin, sc1, sh, dparts, init, gbuf, name):
    rows, d = xin.shape
    npart = len(dparts)
    e = dparts[0].shape[1]
    n4 = e // 2
    ns = 2 * npart
    tm = _row_tile(rows, 1024)
    nt = rows // tm
    has_init = init is not None
    into = gbuf is not None
    assert not into or (ns == N_CHIPS and gbuf.shape[2] == n4)

    def body(*refs):
        x_ref, sc_ref, sh_ref = refs[:3]
        dp = refs[3:3 + npart]
        init_ref = refs[3 + npart] if has_init else None
        o_ref = refs[-1]
        s, i = pl.program_id(0), pl.program_id(1)
        h = (x_ref[...] * sc_ref[...] + sh_ref[...]).astype(BF16)

        @pl.when(i == 0)
        def _():
            o_ref[...] = init_ref[...] if has_init else jnp.zeros_like(o_ref)

        for p in range(npart):
            @pl.when(s // 2 == p)
            def _(p=p):
                o_ref[...] += _dot_tn(h, dp[p][...])

    in_specs = [pl.BlockSpec((tm, d), lambda s, i: (i, 0)),
                pl.BlockSpec((1, d), lambda s, i: (0, 0)),
                pl.BlockSpec((1, d), lambda s, i: (0, 0))]
    in_specs += [pl.BlockSpec((tm, n4), lambda s, i: (i, s % 2))] * npart
    args = [xin, sc1, sh] + list(dparts)
    o_spec = pl.BlockSpec((None, d, n4), lambda s, i: (s, 0, 0))
    if has_init:
        in_specs.append(o_spec)
        args.append(init)
    extra = {}
    if into:
        in_specs.append(ANY_SPEC)
        args.append(gbuf)
        extra = dict(input_output_aliases={len(args) - 1: 0})
    out_shape = jax.ShapeDtypeStruct(gbuf.shape if into else (ns, d, n4), F32)
    return _call(body, name=name, grid=(ns, nt), in_specs=in_specs, out_specs=o_spec, out_shape=out_shape, **extra)(*args)


def _gated(y_ref, g_ref, nch):
    return jnp.concatenate([(y_ref[k].astype(F32) * _silu(g_ref[k].astype(F32))).astype(BF16) for k in range(nch)], axis=1)


def _ln_stats(r):
    mu = jnp.mean(r, axis=-1, keepdims=True)
    var = jnp.mean(jnp.square(r - mu), axis=-1, keepdims=True)
    rstd = lax.rsqrt(var + LN_EPS)
    return (r - mu) * rstd, rstd


def _outproj_fwd(y, ug, xin, gt, wout, lg, lb, target, name):
    nch, rows, _ = y.shape
    e, d = wout.shape
    tm = _row_tile(rows, 512)
    with_loss = target is not None

    def body(*refs):
        y_ref, g_ref, x_ref, gt_ref, w_ref, lg_ref, lb_ref = refs[:7]
        if with_loss:
            t_ref, br_ref, dxo_ref, loss_ref = refs[7:]
        else:
            br_ref, xo_ref = refs[7:]
        z = _gated(y_ref, g_ref, nch)
        br_ref[...] = _dot(z, w_ref[...])
        if with_loss:
            @pl.when(pl.program_id(0) == 0)
            def _():
                loss_ref[...] = jnp.zeros_like(loss_ref)

        def norm(j, c):
            rows = pl.ds(pl.multiple_of(j * LN_ROWS, LN_ROWS), LN_ROWS)
            xhat, _ = _ln_stats(ALPHA * x_ref[rows, :] + gt_ref[...] * br_ref[rows, :])
            xo = xhat * lg_ref[...] + lb_ref[...]
            if with_loss:
                err = xo - t_ref[rows, :]
                dxo_ref[rows, :] = err * (1.0 / d)
                col = jnp.sum(err * err, axis=0, keepdims=True)
                loss_ref[...] += sum(col[:, k * LANE:(k + 1) * LANE] for k in range(d // LANE))
            else:
                xo_ref[rows, :] = xo
            return c

        lax.fori_loop(0, tm // LN_ROWS, norm, 0)

    chunk_spec = pl.BlockSpec((nch, tm, LANE), lambda i: (0, i, 0))
    g_spec = chunk_spec
    row_spec = pl.BlockSpec((tm, d), lambda i: (i, 0))
    vec_spec = pl.BlockSpec((1, d), lambda i: (0, 0))
    in_specs = [chunk_spec, g_spec, row_spec, vec_spec, pl.BlockSpec((e, d), lambda i: (0, 0)), vec_spec, vec_spec]
    args = [y, ug, xin, gt, wout, lg, lb]
    out_specs = [row_spec, row_spec]
    out_shape = [jax.ShapeDtypeStruct((rows, d), F32)] * 2
    if with_loss:
        in_specs.append(row_spec)
        args.append(target)
        out_specs.append(pl.BlockSpec((1, LANE), lambda i: (0, 0)))
        out_shape.append(jax.ShapeDtypeStruct((1, LANE), F32))
    return _call(body, name=name, grid=(rows // tm,), in_specs=in_specs, out_specs=out_specs, out_shape=out_shape)(*args)


def _outproj_bwd(dxo, xin, br, y, ug, gt, lg, wout, name, dy_dtype=F32):
    nch, rows, _ = y.shape
    e, d = wout.shape
    tm = _row_tile(rows, 256)

    def body(dxo_ref, x_ref, br_ref, y_ref, g_ref, gt_ref, lg_ref, w_ref,
             dy_ref, dg_ref, dxres_ref, dbr_ref, dlg_ref, dlb_ref, dgt_ref):
        @pl.when(pl.program_id(0) == 0)
        def _():
            dlg_ref[...] = jnp.zeros_like(dlg_ref)
            dlb_ref[...] = jnp.zeros_like(dlb_ref)
            dgt_ref[...] = jnp.zeros_like(dgt_ref)

        def norm_bwd(j, c):
            rows = pl.ds(pl.multiple_of(j * LN_ROWS, LN_ROWS), LN_ROWS)
            dxo_v = dxo_ref[rows, :]
            brv = br_ref[rows, :]
            xhat, rstd = _ln_stats(ALPHA * x_ref[rows, :] + gt_ref[...] * brv)
            dxh = dxo_v * lg_ref[...]
            dr = rstd * (dxh - jnp.mean(dxh, axis=-1, keepdims=True) - xhat * jnp.mean(dxh * xhat, axis=-1, keepdims=True))
            dlg_ref[...] += jnp.sum(dxo_v * xhat, axis=0, keepdims=True)
            dlb_ref[...] += jnp.sum(dxo_v, axis=0, keepdims=True)
            dgt_ref[...] += jnp.sum(dr * brv, axis=0, keepdims=True)
            dxres_ref[rows, :] = ALPHA * dr
            dbr_ref[rows, :] = (gt_ref[...] * dr).astype(BF16)
            return c

        lax.fori_loop(0, tm // LN_ROWS, norm_bwd, 0)
        dz = _dot_nt(dbr_ref[...], w_ref[...])
        for k in range(nch):
            dzk = dz[:, k * LANE:(k + 1) * LANE]
            gk = g_ref[k].astype(F32)
            sk = _sigmoid(gk)
            dy_ref[k] = (dzk * (gk * sk)).astype(dy_ref.dtype)
            dg_ref[:, k * LANE:(k + 1) * LANE] = (dzk * y_ref[k].astype(F32) * (sk * (1.0 + gk * (1.0 - sk)))).astype(BF16)

    chunk_spec = pl.BlockSpec((nch, tm, LANE), lambda i: (0, i, 0))
    g_spec = chunk_spec
    row_spec = pl.BlockSpec((tm, d), lambda i: (i, 0))
    vec_spec = pl.BlockSpec((1, d), lambda i: (0, 0))
    return _call(
        body, name=name, grid=(rows // tm,),
        in_specs=[row_spec, row_spec, row_spec, chunk_spec, g_spec, vec_spec, vec_spec, pl.BlockSpec((e, d), lambda i: (0, 0))],
        out_specs=[chunk_spec, pl.BlockSpec((tm, e), lambda i: (i, 0)), row_spec, row_spec, vec_spec, vec_spec, vec_spec],
        out_shape=[jax.ShapeDtypeStruct((nch, rows, LANE), dy_dtype), jax.ShapeDtypeStruct((rows, e), BF16),
                   jax.ShapeDtypeStruct((rows, d), F32), jax.ShapeDtypeStruct((rows, d), BF16)]
        + [jax.ShapeDtypeStruct((1, d), F32)] * 3,
    )(dxo, xin, br, y, ug, gt, lg, wout)


def _outproj_bwd_w(y, ug, dbr, gbuf, row0, name):
    nch, rows, _ = y.shape
    d = dbr.shape[1]
    e = nch * LANE
    es = e // N_CHIPS
    tm = _row_tile(rows, 1024)
    assert gbuf.shape[2] == d and row0 % es == 0

    def body(y_ref, g_ref, dbr_ref, buf_ref, o_ref):
        @pl.when(pl.program_id(0) == 0)
        def _():
            o_ref[...] = jnp.zeros_like(o_ref)

        z = _gated(y_ref, g_ref, nch)
        o_ref[...] += _dot_tn(z, dbr_ref[...]).reshape(N_CHIPS, es, d)

    return _call(
        body, name=name, grid=(rows // tm,),
        in_specs=[pl.BlockSpec((nch, tm, LANE), lambda i: (0, i, 0)),
                  pl.BlockSpec((nch, tm, LANE), lambda i: (0, i, 0)),
                  pl.BlockSpec((tm, d), lambda i: (i, 0)),
                  ANY_SPEC],
        out_specs=pl.BlockSpec((N_CHIPS, es, d), lambda i: (0, row0 // es, 0)),
        out_shape=jax.ShapeDtypeStruct(gbuf.shape, F32),
        input_output_aliases={3: 0},
    )(y, ug, dbr, gbuf)


def _scan(a_ref, b_ref, h_ref, *, length, init, reverse, a_shift, store):
    nblk = length // SUB
    unroll = min(SCAN_UNROLL, nblk)
    assert nblk % unroll == 0
    row = lax.broadcasted_iota(jnp.int32, (SUB, LANE), 0)
    last = 0 if reverse else SUB - 1
    edges = [(row >= SUB - k) if reverse else (row < k) for k in (1, 2, 4)]

    def local_scan(a, b):
        for k, edge in zip((1, 2, 4), edges):
            sh = (SUB - k) if reverse else k
            b = b + a * jnp.where(edge, 0.0, pltpu.roll(b, sh, 0))
            a = a * jnp.where(edge, 1.0, pltpu.roll(a, sh, 0))
        return a, b

    def step(i, carry):
        base = pl.multiple_of(((nblk // unroll - 1 - i) if reverse else i) * (unroll * SUB), unroll * SUB)
        order = range(unroll - 1, -1, -1) if reverse else range(unroll)
        loaded = [(a_ref[pl.ds(PAD + base + j * SUB + a_shift, SUB), :], b_ref[pl.ds(PAD + base + j * SUB, SUB), :])
                  for j in order]
        scanned = [local_scan(a, b) for a, b in loaded]
        for j, (a, b) in zip(order, scanned):
            if store:
                h_ref[pl.ds(PAD + base + j * SUB, SUB), :] = b + a * carry
            a_l = jnp.broadcast_to(a[last:last + 1, :], (SUB, LANE))
            b_l = jnp.broadcast_to(b[last:last + 1, :], (SUB, LANE))
            carry = b_l + a_l * carry
        return carry

    carry = lax.fori_loop(0, nblk // unroll, step, jnp.broadcast_to(init, (SUB, LANE)))
    return carry[0:1, :]


def _conv_fwd(src_ref, upad, u_ref, cw, cb, length):
    zeros = jnp.zeros((PAD, LANE), F32)
    upad[pl.ds(0, PAD), :] = zeros
    upad[pl.ds(PAD + length, PAD), :] = zeros
    rt = _row_tile(length, ROW_TILE)

    def copy(i, c):
        t0 = pl.multiple_of(i * rt, rt)
        upad[pl.ds(PAD + t0, rt), :] = src_ref[pl.ds(t0, rt), :]
        return c

    lax.fori_loop(0, length // rt, copy, 0)

    def tile(i, c):
        t0 = pl.multiple_of(i * rt, rt)
        acc = jnp.zeros((rt, LANE), F32)
        for k in range(CONV_TAPS):
            acc = acc + upad[pl.ds(t0 + PAD - CONV_LEFT + k, rt), :] * cw[k:k + 1, :]
        u_ref[pl.ds(t0, rt), :] = acc + cb
        return c

    lax.fori_loop(0, length // rt, tile, 0)


def _gates_fwd(u_ref, a_ref, b_ref, wa, wx, ba, bx, ls, length, keep=None):
    rt = _row_tile(length, GATE_TILE)
    ls_c = LRU_C * ls

    def tile(i, c):
        t0 = pl.multiple_of(i * rt, rt)
        ut = u_ref[pl.ds(t0, rt), :]
        ub = ut.astype(BF16)
        r = 0.5 * (jnp.tanh(_dot(ub, wa) + ba) + 1.0)
        ig = 0.5 * (jnp.tanh(_dot(ub, wx) + bx) + 1.0)
        if keep is not None:
            keep[0][pl.ds(t0, rt), :] = r
            keep[1][pl.ds(t0, rt), :] = ig
        la = r * ls_c
        a = jnp.exp(la)
        a_ref[pl.ds(PAD + t0, rt), :] = a
        q = _one_minus_sq(la, a)
        b_ref[pl.ds(PAD + t0, rt), :] = jnp.where(q > 0.0, q * lax.rsqrt(q), 0.0) * (ig * ut)
        return c

    lax.fori_loop(0, length // rt, tile, 0, unroll=min(GATE_UNROLL, length // rt))


def _lru_specs():
    return [pl.BlockSpec((CONV_TAPS, LANE), lambda n: (0, n)),
            pl.BlockSpec((1, LANE), lambda n: (0, n)),
            pl.BlockSpec((2, None, LANE, LANE), lambda n: (0, n, 0, 0)),
            pl.BlockSpec((2, None, LANE, LANE), lambda n: (0, n, 0, 0)),
            pl.BlockSpec((2, LANE), lambda n: (0, n)),
            pl.BlockSpec((2, LANE), lambda n: (0, n)),
            pl.BlockSpec((2, LANE), lambda n: (0, n))]


def _rglru_fwd(ug, uc, conv_w, conv_b, wa, wx, ba, bx, lam):
    nb = uc.shape[0]
    s_len, t_len = ug.shape[1], uc.shape[1]

    def body(u0_ref, uc0_ref, cw_ref, cb_ref, wa_ref, wx_ref, ba_ref, bx_ref, lam_ref, y_ref,
             upad, ubuf, abuf, hbuf):
        cw, cb = cw_ref[...], cb_ref[...]
        lsig = _log_sigmoid(lam_ref[...])
        zero = jnp.zeros((1, LANE), F32)
        _conv_fwd(uc0_ref, upad, ubuf, cw, cb, t_len)
        h0 = []
        for dr in range(2):
            _gates_fwd(ubuf, abuf, hbuf, wa_ref[dr], wx_ref[dr], ba_ref[dr:dr + 1, :], bx_ref[dr:dr + 1, :],
                       lsig[dr:dr + 1, :], t_len)
            h0.append(_scan(abuf, hbuf, hbuf, length=t_len, init=zero, reverse=(dr == 1), a_shift=0, store=False))
        _conv_fwd(u0_ref, upad, ubuf, cw, cb, s_len)
        rt = _row_tile(s_len, ROW_TILE)
        for dr in range(2):
            _gates_fwd(ubuf, abuf, hbuf, wa_ref[dr], wx_ref[dr], ba_ref[dr:dr + 1, :], bx_ref[dr:dr + 1, :],
                       lsig[dr:dr + 1, :], s_len)
            _scan(abuf, hbuf, hbuf, length=s_len, init=h0[dr], reverse=(dr == 1), a_shift=0, store=True)

            def acc(i, c, dr=dr):
                t0 = pl.multiple_of(i * rt, rt)
                h = hbuf[pl.ds(PAD + t0, rt), :]
                if dr == 0:
                    upad[pl.ds(PAD + t0, rt), :] = h
                else:
                    y_ref[pl.ds(t0, rt), :] = (upad[pl.ds(PAD + t0, rt), :] + h).astype(y_ref.dtype)
                return c

            lax.fori_loop(0, s_len // rt, acc, 0)

    seq = pltpu.VMEM((s_len + 2 * PAD, LANE), F32)
    return _call(
        body, name="rglru_fwd", grid=(nb,),
        in_specs=[pl.BlockSpec((None, s_len, LANE), lambda n: (n, 0, 0)),
                  pl.BlockSpec((None, t_len, LANE), lambda n: (n, 0, 0))] + _lru_specs(),
        out_specs=pl.BlockSpec((None, s_len, LANE), lambda n: (n, 0, 0)),
        out_shape=jax.ShapeDtypeStruct((nb, s_len, LANE), BF16),
        scratch_shapes=[seq, pltpu.VMEM((s_len, LANE), F32), seq, seq],
    )(ug, uc, conv_w, conv_b, wa, wx, ba, bx, lam)


def _rglru_bwd(ug, uc, dy, conv_w, conv_b, wa, wx, ba, bx, lam):
    nb = uc.shape[0]
    e = nb * LANE
    s_len, t_len = ug.shape[1], uc.shape[1]

    def body(u0_ref, uc0_ref, dy_ref, cw_ref, cb_ref, wa_ref, wx_ref, ba_ref, bx_ref, lam_ref,
             du_ref, duc_ref, dcw_ref, dcb_ref, dwa_ref, dwx_ref, dba_ref, dbx_ref, dlam_ref,
             upad, ubuf, abuf, hbuf, lbuf, dubuf, rbuf, ibuf, cpad, cu, ca0, ch0, ca1, ch1, cr0, ci0, cr1, ci1):
        cw, cb = cw_ref[...], cb_ref[...]
        lam_v = lam_ref[...]
        lsig = _log_sigmoid(lam_v)
        zero = jnp.zeros((1, LANE), F32)
        zpad = jnp.zeros((PAD, LANE), F32)
        for ref in (dcw_ref, dcb_ref, dwa_ref, dwx_ref, dba_ref, dbx_ref, dlam_ref):
            ref[...] = jnp.zeros_like(ref)

        def params(dr):
            return (wa_ref[dr], wx_ref[dr], ba_ref[dr:dr + 1, :], bx_ref[dr:dr + 1, :], lsig[dr:dr + 1, :])

        def direction_bwd(dr, u_ref, a_ref, h_ref, l_ref, gates, dub, length, first):
            wa_d, wx_d, ba_d, bx_d, ls_d = params(dr)
            rt = _row_tile(length, GATE_BWD_TILE)
            prev = 1 if dr == 1 else -1

            def tile(i, c):
                t0 = pl.multiple_of(i * rt, rt)
                ut = u_ref[pl.ds(t0, rt), :]
                ub = ut.astype(BF16)
                r = gates[0][pl.ds(t0, rt), :]
                ig = gates[1][pl.ds(t0, rt), :]
                la = r * (LRU_C * ls_d)
                a = a_ref[pl.ds(PAD + t0, rt), :]
                q = _one_minus_sq(la, a)
                rs = lax.rsqrt(q)
                sq = q * rs
                lm = l_ref[pl.ds(PAD + t0, rt), :]
                da = lm * h_ref[pl.ds(PAD + t0 + prev, rt), :]
                dsq = lm * ig * ut
                dig = lm * sq * ut
                dla = da * a - dsq * (a * a) * rs
                dr_ = dla * (LRU_C * ls_d)
                dlam_ref[dr:dr + 1, :] += jnp.sum(dla * (LRU_C * r), axis=0, keepdims=True)
                dpr = dr_ * r * (1.0 - r)
                dpi = dig * ig * (1.0 - ig)
                dba_ref[dr:dr + 1, :] += jnp.sum(dpr, axis=0, keepdims=True)
                dbx_ref[dr:dr + 1, :] += jnp.sum(dpi, axis=0, keepdims=True)
                dprb, dpib = dpr.astype(BF16), dpi.astype(BF16)
                dwa_ref[dr] += _dot_tn(ub, dprb)
                dwx_ref[dr] += _dot_tn(ub, dpib)
                dut = lm * sq * ig + 2.0 * (_dot_nt(dprb, wa_d) + _dot_nt(dpib, wx_d))
                if first:
                    dub[pl.ds(PAD + t0, rt), :] = dut
                else:
                    dub[pl.ds(PAD + t0, rt), :] += dut
                return c

            lax.fori_loop(0, length // rt, tile, 0, unroll=min(GATE_UNROLL, length // rt))

        def conv_bwd(dub, src_pad, out_ref, length):
            rt = _row_tile(length, ROW_TILE)

            def tile(i, c):
                t0 = pl.multiple_of(i * rt, rt)
                dut = dub[pl.ds(PAD + t0, rt), :]
                dcb_ref[...] += jnp.sum(dut, axis=0, keepdims=True)
                acc = jnp.zeros((rt, LANE), F32)
                for k in range(CONV_TAPS):
                    sh = CONV_LEFT - k
                    acc = acc + dub[pl.ds(PAD + t0 + sh, rt), :] * cw[k:k + 1, :]
                    dcw_ref[k:k + 1, :] += jnp.sum(dut * src_pad[pl.ds(PAD + t0 - sh, rt), :], axis=0, keepdims=True)
                out_ref[pl.ds(t0, rt), :] = acc.astype(out_ref.dtype)
                return c

            lax.fori_loop(0, length // rt, tile, 0)

        _conv_fwd(uc0_ref, cpad, cu, cw, cb, t_len)
        cbufs = ((ca0, ch0), (ca1, ch1))
        cgates = ((cr0, ci0), (cr1, ci1))
        h0 = []
        for dr in range(2):
            ca, chh = cbufs[dr]
            _gates_fwd(cu, ca, chh, *params(dr), t_len, keep=cgates[dr])
            h0.append(_scan(ca, chh, chh, length=t_len, init=zero, reverse=(dr == 1), a_shift=0, store=True))
        _conv_fwd(u0_ref, upad, ubuf, cw, cb, s_len)
        rt = _row_tile(s_len, ROW_TILE)
        dh0 = []
        for dr in range(2):
            rev = dr == 1
            _gates_fwd(ubuf, abuf, hbuf, *params(dr), s_len, keep=(rbuf, ibuf))
            _scan(abuf, hbuf, hbuf, length=s_len, init=h0[dr], reverse=rev, a_shift=0, store=True)
            first_row = PAD + s_len if rev else PAD - 1
            hbuf[pl.ds(first_row, 1), :] = h0[dr]
            end_row = PAD - 1 if rev else PAD + s_len
            abuf[pl.ds(end_row, 1), :] = zero

            def copy(i, c):
                t0 = pl.multiple_of(i * rt, rt)
                lbuf[pl.ds(PAD + t0, rt), :] = dy_ref[pl.ds(t0, rt), :]
                return c

            lax.fori_loop(0, s_len // rt, copy, 0)
            _scan(abuf, lbuf, lbuf, length=s_len, init=zero, reverse=not rev, a_shift=(-1 if rev else 1), store=True)
            start = PAD + s_len - 1 if rev else PAD
            dh0.append(abuf[pl.ds(start, 1), :] * lbuf[pl.ds(start, 1), :])
            direction_bwd(dr, ubuf, abuf, hbuf, lbuf, (rbuf, ibuf), dubuf, s_len, first=(dr == 0))
        dubuf[pl.ds(0, PAD), :] = zpad
        dubuf[pl.ds(PAD + s_len, PAD), :] = zpad
        conv_bwd(dubuf, upad, du_ref, s_len)
        lc = lbuf
        duc_buf = dubuf
        for dr in range(2):
            rev = dr == 1
            ca, chh = cbufs[dr]
            first_row = PAD + t_len if rev else PAD - 1
            chh[pl.ds(first_row, 1), :] = zero
            end_row = PAD - 1 if rev else PAD + t_len
            ca[pl.ds(end_row, 1), :] = zero + 1.0
            rtc = _row_tile(t_len, ROW_TILE)

            def clear(i, c):
                t0 = pl.multiple_of(i * rtc, rtc)
                lc[pl.ds(PAD + t0, rtc), :] = jnp.zeros((rtc, LANE), F32)
                return c

            lax.fori_loop(0, t_len // rtc, clear, 0)
            _scan(ca, lc, lc, length=t_len, init=dh0[dr], reverse=not rev, a_shift=(-1 if rev else 1), store=True)
            direction_bwd(dr, cu, ca, chh, lc, cgates[dr], duc_buf, t_len, first=(dr == 0))
        duc_buf[pl.ds(0, PAD), :] = zpad
        duc_buf[pl.ds(PAD + t_len, PAD), :] = zpad
        conv_bwd(duc_buf, cpad, duc_ref, t_len)
        dlam_ref[...] = dlam_ref[...] * (1.0 - _sigmoid(lam_v))

    seq = pltpu.VMEM((s_len + 2 * PAD, LANE), F32)
    cseq = pltpu.VMEM((t_len + 2 * PAD, LANE), F32)
    flat = pltpu.VMEM((s_len, LANE), F32)
    cflat = pltpu.VMEM((t_len, LANE), F32)
    vec2 = pl.BlockSpec((2, LANE), lambda n: (0, n))
    wspec = pl.BlockSpec((2, None, LANE, LANE), lambda n: (0, n, 0, 0))
    return _call(
        body, name="rglru_bwd", grid=(nb,),
        in_specs=[pl.BlockSpec((None, s_len, LANE), lambda n: (n, 0, 0)),
                  pl.BlockSpec((None, t_len, LANE), lambda n: (n, 0, 0)),
                  pl.BlockSpec((None, s_len, LANE), lambda n: (n, 0, 0))] + _lru_specs(),
        out_specs=[pl.BlockSpec((s_len, LANE), lambda n: (0, n)),
                   pl.BlockSpec((t_len, LANE), lambda n: (0, n)),
                   pl.BlockSpec((CONV_TAPS, LANE), lambda n: (0, n)),
                   pl.BlockSpec((1, LANE), lambda n: (0, n)),
                   wspec, wspec, vec2, vec2, vec2],
        out_shape=[jax.ShapeDtypeStruct((s_len, e), BF16), jax.ShapeDtypeStruct((t_len, e), BF16),
                   jax.ShapeDtypeStruct((CONV_TAPS, e), F32), jax.ShapeDtypeStruct((1, e), F32),
                   jax.ShapeDtypeStruct((2, nb, LANE, LANE), F32), jax.ShapeDtypeStruct((2, nb, LANE, LANE), F32),
                   jax.ShapeDtypeStruct((2, e), F32), jax.ShapeDtypeStruct((2, e), F32), jax.ShapeDtypeStruct((2, e), F32)],
        scratch_shapes=[seq, flat, seq, seq, seq, seq, flat, flat,
                        cseq, cflat, cseq, cseq, cseq, cseq, cflat, cflat, cflat, cflat],
    )(ug, uc, dy, conv_w, conv_b, wa, wx, ba, bx, lam)


def _pool_windows(src_ref, out_ref, colbuf, rowbuf, half, transpose, s_len):
    gw = GRID_W
    lg = gw.bit_length() - 1
    n_rows = s_len // gw
    cp, rm = POOL_CPAD, 8 * gw
    stride = gw + 2 * cp
    rt = _row_tile(s_len, ROW_TILE)
    assert rt % gw == 0 and half <= cp
    gpt = rt // gw
    offs = range(-half, half)
    zmargin = jnp.zeros((cp, LANE), F32)

    def zcol(r, c):
        base = pl.multiple_of(r * stride, SUB)
        colbuf[pl.ds(base, cp), :] = zmargin
        colbuf[pl.ds(base + cp + gw, cp), :] = zmargin
        return c

    lax.fori_loop(0, n_rows, zcol, 0)

    def zrow(i, c):
        t0 = pl.multiple_of(i * gw, gw)
        rowbuf[pl.ds(t0, gw), :] = jnp.zeros((gw, LANE), F32)
        rowbuf[pl.ds(rm + s_len + t0, gw), :] = jnp.zeros((gw, LANE), F32)
        return c

    lax.fori_loop(0, rm // gw, zrow, 0)

    col = lax.broadcasted_iota(jnp.int32, (gw, LANE), 0)
    ccnt = (jnp.minimum(col + half, gw) - jnp.maximum(col - half, 0)).astype(F32)

    def row_counts(t0):
        row = (t0 + lax.broadcasted_iota(jnp.int32, (rt, LANE), 0)) >> lg
        return (jnp.minimum(row + half, n_rows) - jnp.maximum(row - half, 0)).astype(F32)

    def col_base(t0, g):
        return pl.multiple_of((t0 // gw) * stride, SUB) + g * stride + cp

    def col_sum(t0, g, sign):
        acc = jnp.zeros((gw, LANE), F32)
        for o in offs:
            acc = acc + colbuf[pl.ds(col_base(t0, g) + sign * o, gw), :]
        return acc

    def row_sum(t0, sign):
        acc = jnp.zeros((rt, LANE), F32)
        for o in offs:
            acc = acc + rowbuf[pl.ds(rm + t0 + sign * o * gw, rt), :]
        return acc

    n_tiles = s_len // rt
    assert rt >= half * gw

    def loop(fn, edges=False):
        def step(i, c):
            t0 = pl.multiple_of(i * rt, rt)
            fn(t0, False) if edges else fn(t0)
            return c
        if edges:
            fn(0, True)
            if n_tiles > 1:
                fn(s_len - rt, True)
            lax.fori_loop(1, n_tiles - 1, step, 0)
        else:
            lax.fori_loop(0, n_tiles, step, 0)

    inv_ccnt = 1.0 / ccnt

    def by_row_count(v, t0, edge):
        return v / row_counts(t0) if edge else v * (1.0 / (2 * half))

    if not transpose:
        def fill(t0):
            for g in range(gpt):
                colbuf[pl.ds(col_base(t0, g), gw), :] = src_ref[pl.ds(t0 + g * gw, gw), :]

        def cols(t0):
            for g in range(gpt):
                rowbuf[pl.ds(rm + t0 + g * gw, gw), :] = col_sum(t0, g, 1) * inv_ccnt

        def rows(t0, edge):
            mean = by_row_count(row_sum(t0, 1), t0, edge)
            out_ref[pl.ds(t0, rt), :] = (mean - src_ref[pl.ds(t0, rt), :]).astype(out_ref.dtype)

        loop(fill)
        loop(cols)
        loop(rows, edges=True)
    else:
        def fill(t0, edge):
            rowbuf[pl.ds(rm + t0, rt), :] = by_row_count(src_ref[pl.ds(t0, rt), :], t0, edge)

        def rows(t0):
            acc = row_sum(t0, -1)
            for g in range(gpt):
                colbuf[pl.ds(col_base(t0, g), gw), :] = acc[g * gw:(g + 1) * gw, :] * inv_ccnt

        def cols(t0):
            for g in range(gpt):
                rows_g = pl.ds(t0 + g * gw, gw)
                out_ref[rows_g, :] = (col_sum(t0, g, -1) - src_ref[rows_g, :]).astype(out_ref.dtype)

        loop(fill, edges=True)
        loop(rows)
        loop(cols)


def _pool_map(src, nb, transpose, out_chunk_major, name):
    s_len = src.shape[1]
    cpg = nb // len(POOL_WINDOWS)

    def body(src_ref, out_ref, colbuf, rowbuf):
        n = pl.program_id(0)
        for gi, w in enumerate(POOL_WINDOWS):
            @pl.when(n // cpg == gi)
            def _(w=w):
                _pool_windows(src_ref, out_ref, colbuf, rowbuf, w // 2, transpose, s_len)

    if out_chunk_major:
        out_spec = pl.BlockSpec((None, s_len, LANE), lambda n: (n, 0, 0))
        out_shape = jax.ShapeDtypeStruct((nb, s_len, LANE), BF16)
    else:
        out_spec = pl.BlockSpec((s_len, LANE), lambda n: (0, n))
        out_shape = jax.ShapeDtypeStruct((s_len, nb * LANE), BF16)
    return _call(
        body, name=name, grid=(nb,),
        in_specs=[pl.BlockSpec((None, s_len, LANE), lambda n: (n, 0, 0))],
        out_specs=out_spec, out_shape=out_shape,
        scratch_shapes=[pltpu.VMEM((s_len // GRID_W * (GRID_W + 2 * POOL_CPAD), LANE), F32),
                        pltpu.VMEM((s_len + 16 * GRID_W, LANE), F32)],
    )(src)


def _group_weight(w_ref):
    return jnp.concatenate([w_ref[k] for k in range(N_CHIPS)], axis=0)


def _pool_mm_fwd(dm, wp, scale):
    nb, rows, _ = dm.shape
    _, ng, pq, pg = wp.shape
    cpg = pg // LANE
    tm = _row_tile(rows, 2048)

    def body(d_ref, w_ref, s_ref, y_ref):
        _put_chunks(y_ref, _dot(_cat(d_ref, cpg), _group_weight(w_ref)) * s_ref[...], cpg)

    cspec = pl.BlockSpec((cpg, tm, LANE), lambda i, g: (g, i, 0))
    return _call(
        body, name="pool_mm_fwd", grid=(rows // tm, ng),
        in_specs=[cspec, pl.BlockSpec((N_CHIPS, None, pq, pg), lambda i, g: (0, g, 0, 0)),
                  pl.BlockSpec((1, pg), lambda i, g: (0, g))],
        out_specs=cspec, out_shape=jax.ShapeDtypeStruct((nb, rows, LANE), BF16),
    )(dm, wp, scale)


def _pool_mm_bwd(dy, dm, wp, scale, gbuf, row0):
    nb, rows, _ = dm.shape
    _, ng, pq, pg = wp.shape
    cpg = pg // LANE
    tm = _row_tile(rows, 1024)
    nt = rows // tm
    assert gbuf.shape[2] == 2 * pg and row0 % pq == 0

    def body(dy_ref, d_ref, w_ref, s_ref, buf_ref, dd_ref, dwp_ref, dsc_ref, acc):
        i = pl.program_id(1)

        @pl.when(i == 0)
        def _():
            acc[...] = jnp.zeros_like(acc)
            dsc_ref[...] = jnp.zeros_like(dsc_ref)

        dyv = _cat(dy_ref, cpg).astype(F32)
        dc = _cat(d_ref, cpg)
        w = _group_weight(w_ref)
        dsc_ref[...] += jnp.sum(dyv * _dot(dc, w), axis=0, keepdims=True)
        dyp = (dyv * s_ref[...]).astype(BF16)
        _put_chunks(dd_ref, _dot_nt(dyp, w), cpg)
        acc[...] += _dot_tn(dc, dyp)

        @pl.when(i == nt - 1)
        def _():
            dwp_ref[...] = acc[...].reshape(N_CHIPS, pq, pg)

    cspec = pl.BlockSpec((cpg, tm, LANE), lambda g, i: (g, i, 0))
    sspec = pl.BlockSpec((1, pg), lambda g, i: (0, g))
    return _call(
        body, name="pool_mm_bwd", grid=(ng, nt),
        in_specs=[cspec, cspec, pl.BlockSpec((N_CHIPS, None, pq, pg), lambda g, i: (0, g, 0, 0)), sspec, ANY_SPEC],
        out_specs=[cspec, pl.BlockSpec((N_CHIPS, pq, pg), lambda g, i: (0, row0 // pq + g // 2, g % 2)), sspec],
        out_shape=[jax.ShapeDtypeStruct((nb, rows, LANE), F32), jax.ShapeDtypeStruct(gbuf.shape, F32),
                   jax.ShapeDtypeStruct((1, ng * pg), F32)],
        scratch_shapes=[pltpu.VMEM((pg, pg), F32)],
        input_output_aliases={4: 1},
    )(dy, dm, wp, scale, gbuf)


def _adamw_math(w, g, m, v):
    nm = ADAM_B1 * m + (1.0 - ADAM_B1) * g
    nv = ADAM_B2 * v + (1.0 - ADAM_B2) * jnp.square(g)
    m_hat = nm / (1.0 - ADAM_B1 ** ADAM_STEP)
    v_hat = nv / (1.0 - ADAM_B2 ** ADAM_STEP)
    return -ADAM_LR * (m_hat / (jnp.sqrt(v_hat) + ADAM_EPS) + ADAM_WD * w), nm, nv


def _adamw_param(w3, m3, v3, gsrcs, pick, tm, name, after=None):
    n_blk, rows, cols = w3.shape
    ng = len(gsrcs)

    def body(*refs):
        w_ref, m_ref, v_ref = refs[:3]
        g_refs = refs[3:3 + ng]
        go_ref, d_ref, nm_ref, nv_ref = refs[-4:]
        g = pick(pl.program_id(0), [r[...] for r in g_refs])
        go_ref[...] = g
        d_ref[...], nm_ref[...], nv_ref[...] = _adamw_math(w_ref[...], g, m_ref[...], v_ref[...])

    spec = pl.BlockSpec((None, tm, cols), lambda n, i: (n, i, 0))
    extra = [] if after is None else [after]
    return _call(
        body, name=name, grid=(n_blk, rows // tm),
        in_specs=[spec] * 3 + [pl.BlockSpec(shape, imap) for _, shape, imap in gsrcs] + [ANY_SPEC] * len(extra),
        out_specs=[spec] * 4, out_shape=[jax.ShapeDtypeStruct(w3.shape, F32)] * 4,
    )(w3, m3, v3, *[a for a, _, _ in gsrcs], *extra)


def _adamw_small(quads):
    n = len(quads)

    def body(*refs):
        ins, outs = refs[:4 * n], refs[4 * n:]
        for k in range(n):
            w, g, m, v = (r[...] for r in ins[4 * k:4 * k + 4])
            outs[3 * k][...], outs[3 * k + 1][...], outs[3 * k + 2][...] = _adamw_math(w, g, m, v)

    flat = [a for q in quads for a in q]
    res = _call(body, name="adamw_small", grid=(1,),
                in_specs=[pl.BlockSpec(a.shape, lambda i: (0, 0)) for a in flat],
                out_specs=[pl.BlockSpec(q[0].shape, lambda i: (0, 0)) for q in quads for _ in range(3)],
                out_shape=[jax.ShapeDtypeStruct(q[0].shape, F32) for q in quads for _ in range(3)])(*flat)
    return [tuple(res[3 * k:3 * k + 3]) for k in range(n)]


def _place():
    return lax.axis_index("x"), lax.axis_index("y"), lax.axis_index("c")


def _other_chips(x, y):
    return [(1 - x, y), (x, 1 - y), (1 - x, 1 - y)]


def _own_slab(a, devices=False):
    x, y, c = _place()
    n, me = (N_DEV, 4 * x + 2 * y + c) if devices else (N_CHIPS, 2 * x + y)
    return lax.dynamic_update_slice(lax.empty((n,) + a.shape, a.dtype), a[None], (me, 0, 0))


def _gather_chips(arrays, name):
    n = len(arrays)
    halves = [a.shape[0] // 2 for a in arrays]
    for a, h in zip(arrays, halves):
        assert 2 * h == a.shape[0] and h % (32 // a.dtype.itemsize) == 0
    lands = [_own_slab(a) for a in arrays]

    def body(*refs):
        outs = refs[n:2 * n]
        send_sems, recv_sems = refs[2 * n:]
        x, y, c = _place()
        me = 2 * x + y
        chips = _other_chips(x, y)

        def mine(k):
            return pl.ds(c * halves[k], halves[k])

        def theirs(k):
            return pl.ds((1 - c) * halves[k], halves[k])

        def push(k, j, src, dst, to):
            return pltpu.make_async_remote_copy(src_ref=src, dst_ref=dst, send_sem=send_sems.at[6 * k + j],
                                                recv_sem=recv_sems.at[6 * k + j], device_id=to, device_id_type=MESH)

        started = []
        for j, (cx, cy) in enumerate(chips):
            for k in range(n):
                own = outs[k].at[me, mine(k)]
                cp = push(k, j, own, own, (cx, cy, c))
                cp.start()
                started.append(cp)
        for j, (cx, cy) in enumerate(chips):
            for k in range(n):
                slab = outs[k].at[2 * cx + cy, mine(k)]
                push(k, j, slab, slab, (x, y, c)).wait_recv()
                fwd = push(k, 3 + j, slab, slab, (x, y, 1 - c))
                fwd.start()
                started.append(fwd)
        for j, (cx, cy) in enumerate(chips):
            for k in range(n):
                slab = outs[k].at[2 * cx + cy, theirs(k)]
                push(k, 3 + j, slab, slab, (x, y, c)).wait_recv()
        for cp in started:
            cp.wait_send()

    return _call(
        body, name=name, in_specs=[ANY_SPEC] * n, out_specs=[ANY_SPEC] * n,
        out_shape=[jax.ShapeDtypeStruct(a.shape, a.dtype) for a in lands],
        input_output_aliases={k: k for k in range(n)},
        scratch_shapes=[pltpu.SemaphoreType.DMA((6 * n,)), pltpu.SemaphoreType.DMA((6 * n,))],
    )(*lands)


HBM_SPEC = pl.BlockSpec(memory_space=pltpu.HBM)
SEM_SPEC = pl.BlockSpec(memory_space=pltpu.SEMAPHORE)
SIDE_EFFECT = pltpu.SideEffectType.DATAFLOW_SIDE_EFFECTING


def _n_peers(kind):
    return N_DEV - 1 if kind == "devices" else N_CHIPS - 1


def _push_copies(src_refs, land_refs, send_sems, recv_sems, kind):
    x, y, c = _place()
    if kind == "devices":
        me = 4 * x + 2 * y + c
        peers = [((me + j) % N_DEV, None) for j in range(1, N_DEV)]
        peers = [((to // 4, (to // 2) % 2, to % 2), None) for to, _ in peers]
    else:
        me = 2 * x + y
        peers = [((cx, cy, c), 2 * cx + cy) for cx, cy in _other_chips(x, y)]
    n = len(peers)
    copies = []
    for j, (dev, slab) in enumerate(peers):
        for k, (src, land) in enumerate(zip(src_refs, land_refs)):
            copies.append(pltpu.make_async_remote_copy(
                src_ref=src.at[slab] if kind == "slab" else src, dst_ref=land.at[me], send_sem=send_sems.at[n * k + j],
                recv_sem=recv_sems.at[n * k + j], device_id=dev, device_id_type=MESH))
    return copies


def _push_start(srcs, lands, kind, after, name):
    n = len(srcs)

    def body(*refs):
        src_refs, land_refs = refs[:n], refs[n:2 * n]
        send_sems, recv_sems = refs[2 * n + 1], refs[2 * n + 2]
        token = refs[-1]
        for cp in _push_copies(src_refs, land_refs, send_sems, recv_sems, kind):
            cp.start()
        token[...] = jnp.zeros_like(token)

    bufs = [pltpu.with_memory_space_constraint(a, pltpu.HBM) for a in list(srcs) + list(lands)]
    res = _call(
        body, name=name,
        out_shape=[pltpu.SemaphoreType.DMA((_n_peers(kind) * n,)), pltpu.SemaphoreType.DMA((_n_peers(kind) * n,))]
        + [pltpu.HBM(a.shape, a.dtype) for a in bufs] + [jax.ShapeDtypeStruct((SUB, LANE), F32)],
        in_specs=[HBM_SPEC] * (2 * n) + [ANY_SPEC],
        out_specs=[SEM_SPEC, SEM_SPEC] + [HBM_SPEC] * (2 * n) + [pl.BlockSpec(memory_space=pltpu.VMEM)],
        input_output_aliases={i: 2 + i for i in range(2 * n)},
        compiler_params=pltpu.CompilerParams(has_side_effects=SIDE_EFFECT),
    )(*bufs, after)
    return res[0], res[1], list(res[2:2 + n]), list(res[2 + n:2 + 2 * n]), res[-1]


def _push_wait(send_sems, recv_sems, srcs, lands, kind, after, name):
    n = len(srcs)

    def body(*refs):
        src_refs, land_refs = refs[:n], refs[n:2 * n]
        send_sems, recv_sems = refs[2 * n], refs[2 * n + 1]
        for cp in _push_copies(src_refs, land_refs, send_sems, recv_sems, kind):
            cp.wait_send()
            cp.wait_recv()

    res = _call(
        body, name=name,
        out_shape=[pltpu.HBM(a.shape, a.dtype) for a in list(srcs) + list(lands)],
        in_specs=[HBM_SPEC] * (2 * n) + [SEM_SPEC, SEM_SPEC, ANY_SPEC],
        out_specs=[HBM_SPEC] * (2 * n),
        input_output_aliases={i: i for i in range(2 * n)},
        compiler_params=pltpu.CompilerParams(has_side_effects=SIDE_EFFECT),
    )(*srcs, *lands, send_sems, recv_sems, after)
    return list(res[n:])


def _sibling_swap(g):
    _, rows, w = g.shape
    half = rows // 2

    def body(g_ref, out_ref, send_sem, recv_sem):
        x, y, c = _place()
        cp = pltpu.make_async_remote_copy(src_ref=g_ref.at[:, pl.ds((1 - c) * half, half)], dst_ref=out_ref,
                                          send_sem=send_sem, recv_sem=recv_sem, device_id=(x, y, 1 - c), device_id_type=MESH)
        cp.start()
        cp.wait()

    return _call(body, name="rs_sibling_swap", in_specs=[ANY_SPEC], out_specs=ANY_SPEC,
                 out_shape=jax.ShapeDtypeStruct((N_CHIPS, half, w), F32),
                 scratch_shapes=[pltpu.SemaphoreType.DMA, pltpu.SemaphoreType.DMA])(g)


def _pair_add(g, got, place):
    _, rows, w = g.shape
    half = rows // 2
    tm = _row_tile(half, RS_TILE)
    nt = half // tm

    def body(p_ref, a_ref, b_ref, o_ref, own_ref):
        v = a_ref[...] + b_ref[...]
        o_ref[...] = v.astype(BF16)

        @pl.when(pl.program_id(1) == p_ref[1])
        def _():
            own_ref[...] = v

    return _call(
        body, name="rs_pair_add",
        grid_spec=pltpu.PrefetchScalarGridSpec(
            num_scalar_prefetch=1, grid=(nt, N_CHIPS),
            in_specs=[pl.BlockSpec((None, tm, w), lambda i, s, p: (s, p[0] * nt + i, 0)),
                      pl.BlockSpec((None, tm, w), lambda i, s, p: (s, i, 0))],
            out_specs=[pl.BlockSpec((None, tm, w), lambda i, s, p: (s, i, 0)),
                       pl.BlockSpec((tm, w), lambda i, s, p: (i, 0))]),
        out_shape=[jax.ShapeDtypeStruct((N_CHIPS, half, w), BF16), jax.ShapeDtypeStruct((half, w), F32)],
    )(place, g, got)


def _sum_chips(parts, own, place):
    _, half, w = parts.shape
    tm = _row_tile(half, RS_TILE)
    nt = half // tm

    def body(p_ref, parts_ref, own_ref, o_ref):
        me = p_ref[1]
        t = [jnp.where(me == q, own_ref[...], parts_ref[q].astype(F32)) for q in range(N_CHIPS)]
        o_ref[...] = (t[0] + t[1]) + (t[2] + t[3])

    return _call(
        body, name="rs_sum_chips",
        grid_spec=pltpu.PrefetchScalarGridSpec(
            num_scalar_prefetch=1, grid=(nt,),
            in_specs=[pl.BlockSpec((N_CHIPS, tm, w), lambda i, p: (0, i, 0)), pl.BlockSpec((tm, w), lambda i, p: (i, 0))],
            out_specs=pl.BlockSpec((tm, w), lambda i, p: (p[0] * nt + i, 0))),
        out_shape=jax.ShapeDtypeStruct((2 * half, w), F32),
    )(place, parts, own)


def _sibling_gather(red):
    rows, w = red.shape
    half = rows // 2

    def body(in_ref, out_ref, send_sem, recv_sem):
        x, y, c = _place()
        mine = out_ref.at[pl.ds(c * half, half)]
        cp = pltpu.make_async_remote_copy(src_ref=mine, dst_ref=mine, send_sem=send_sem, recv_sem=recv_sem,
                                          device_id=(x, y, 1 - c), device_id_type=MESH)
        cp.start()
        other = out_ref.at[pl.ds((1 - c) * half, half)]
        pltpu.make_async_remote_copy(src_ref=other, dst_ref=other, send_sem=send_sem, recv_sem=recv_sem,
                                     device_id=(x, y, c), device_id_type=MESH).wait_recv()
        cp.wait_send()

    return _call(body, name="rs_sibling_gather", in_specs=[ANY_SPEC], out_specs=ANY_SPEC,
                 out_shape=jax.ShapeDtypeStruct(red.shape, F32), input_output_aliases={0: 0},
                 scratch_shapes=[pltpu.SemaphoreType.DMA, pltpu.SemaphoreType.DMA])(red)


def _rs_begin(g, place, name):
    pair, own = _pair_add(g, _sibling_swap(g), place)
    send, recv, pair, parts, token = _push_start([pair], [jnp.zeros_like(pair)], "slab", own, name + "_start")
    return (send, recv, pair, parts, own), token


def _rs_end(state, place, after, name):
    send, recv, pair, parts, own = state
    (parts,) = _push_wait(send, recv, pair, parts, "slab", after, name + "_wait")
    return _sibling_gather(_sum_chips(parts, own, place))


WEIGHTS = ("c_ctx", "w_mod", "b_mod", "w_in", "w_out", "ln_g", "ln_b", "conv_w", "conv_b", "lru_wa", "lru_ba", "lru_wx",
           "lru_bx", "lru_lam", "pool_w", "pool_scale")
SMALL_GATHERED = ("conv_w", "lru_ba", "lru_bx", "lru_lam", "pool_scale")
SMALL_UPDATED = ("c_ctx", "b_mod", "ln_g", "ln_b", "conv_w", "conv_b", "lru_ba", "lru_bx", "lru_lam", "pool_scale")


def kernel(x, c, ctx, c_ctx, w_mod, b_mod, w_in, w_out, ln_g, ln_b, conv_w, conv_b, lru_wa, lru_ba, lru_wx, lru_bx, lru_lam, pool_w, pool_scale, loss_target, m_c_ctx, m_w_mod, m_b_mod, m_w_in, m_w_out, m_ln_g, m_ln_b, m_conv_w, m_conv_b, m_lru_wa, m_lru_ba, m_lru_wx, m_lru_bx, m_lru_lam, m_pool_w, m_pool_scale, v_c_ctx, v_w_mod, v_b_mod, v_w_in, v_w_out, v_ln_g, v_ln_b, v_conv_w, v_conv_b, v_lru_wa, v_lru_ba, v_lru_wx, v_lru_bx, v_lru_lam, v_pool_w, v_pool_scale):
    weights = dict(c_ctx=c_ctx, w_mod=w_mod, b_mod=b_mod, w_in=w_in, w_out=w_out, ln_g=ln_g, ln_b=ln_b, conv_w=conv_w,
                   conv_b=conv_b, lru_wa=lru_wa, lru_ba=lru_ba, lru_wx=lru_wx, lru_bx=lru_bx, lru_lam=lru_lam,
                   pool_w=pool_w, pool_scale=pool_scale)
    mom1 = dict(c_ctx=m_c_ctx, w_mod=m_w_mod, b_mod=m_b_mod, w_in=m_w_in, w_out=m_w_out, ln_g=m_ln_g, ln_b=m_ln_b,
                conv_w=m_conv_w, conv_b=m_conv_b, lru_wa=m_lru_wa, lru_ba=m_lru_ba, lru_wx=m_lru_wx, lru_bx=m_lru_bx,
                lru_lam=m_lru_lam, pool_w=m_pool_w, pool_scale=m_pool_scale)
    mom2 = dict(c_ctx=v_c_ctx, w_mod=v_w_mod, b_mod=v_b_mod, w_in=v_w_in, w_out=v_w_out, ln_g=v_ln_g, ln_b=v_ln_b,
                conv_w=v_conv_w, conv_b=v_conv_b, lru_wa=v_lru_wa, lru_ba=v_lru_ba, lru_wx=v_lru_wx, lru_bx=v_lru_bx,
                lru_lam=v_lru_lam, pool_w=v_pool_w, pool_scale=v_pool_scale)
    xs, cx, target = x[0], ctx[0], loss_target[0]
    s_len, d = xs.shape
    es = w_out.shape[1]
    e = es * N_CHIPS
    nb = e // LANE
    c3 = w_mod.shape[2]
    n4 = w_in.shape[2]
    pq, pg = pool_w.shape[2], pool_w.shape[3]
    ng = len(POOL_WINDOWS)
    width = n4
    assert width == d and 2 * pg == width and 2 * nb * LANE == N_CHIPS * width and d % (2 * N_CHIPS) == 0
    px, py, pc = _place()
    place = jnp.stack([pc, 2 * px + py]).astype(jnp.int32)
    cctx2 = c_ctx[None, :]

    eq = e // N_CHIPS
    small_rows = [(conv_w[0], 0), (lru_ba[0], CONV_TAPS), (lru_bx[0], CONV_TAPS + 2), (lru_lam[0], CONV_TAPS + 4),
                  (pool_scale, CONV_TAPS + 6)]
    small = _rows_kernel([(a, r, 0) for a, r in small_rows], 2 * SUB, eq, "pack_small_weights")
    win0, sg = _gather_chips([w_in[0].astype(BF16), small], "gather_weights0")
    full = {n: jnp.swapaxes(sg[:, r:r + a.shape[0]], 0, 1).reshape(a.shape[0], e)
            for n, (a, r) in zip(SMALL_GATHERED, small_rows)}
    wa_h, wx_h = (0.5 * lru_wa[0]).astype(BF16), (0.5 * lru_wx[0]).astype(BF16)
    lru_args = (full["conv_w"], conv_b, wa_h, wx_h, 0.5 * full["lru_ba"], 0.5 * full["lru_bx"], full["lru_lam"])
    scale_f = full["pool_scale"]

    me8 = 4 * px + 2 * py + pc
    c_rows = _rows_kernel([(c, 0, 0)], SUB, d, "pack_c")
    c_send, c_recv, c_src, c_lands, c_token = _push_start([c_rows], [_own_slab(c_rows, devices=True)], "devices", sg,
                                                          "gather_c_start")
    (c_dev,) = _push_wait(c_send, c_recv, c_src, c_lands, "devices", c_token, "gather_c_wait")
    wm_mine = w_mod.astype(BF16)
    bm_mine = lax.dynamic_slice_in_dim(b_mod[:, None, :], place[1] * c3, c3, axis=2)
    (mod_g,) = _gather_chips([_mod_fwd(c_dev[:, 0, :], cctx2, wm_mine, bm_mine).reshape(DEPTH * 2 * SUB, c3)], "gather_mod")
    mod_all = jnp.transpose(mod_g.reshape(N_CHIPS, DEPTH, 2 * SUB, c3), (1, 2, 0, 3)).reshape(DEPTH, 2 * SUB, 3 * d)
    mod_mine = lax.dynamic_index_in_dim(mod_all, me8, axis=1, keepdims=False)
    later = [w_out[0].astype(BF16), w_in[1].astype(BF16), w_out[1].astype(BF16), pool_w.astype(BF16).reshape(ng * pq, pg)]
    w_send, w_recv, later, later_lands, w_token = _push_start(
        later, [_own_slab(a) for a in later], "same", mod_g, "gather_weights1_start")
    mod_mine = mod_mine + w_token[0:1, 0:1]

    def mod_parts(v):
        return v[None, :d], 1.0 + v[None, d:2 * d], v[None, 2 * d:]

    sh0, sc0, gt0 = mod_parts(mod_mine[0])
    shc, scc, _ = mod_parts(mod_all[0, N_DEV])
    sh1, sc1, gt1 = mod_parts(mod_mine[1])
    lg = [ln_g[l][None, :] for l in range(DEPTH)]
    lb = [ln_b[l][None, :] for l in range(DEPTH)]

    uu0, ug0 = _inproj_fwd(xs, sc0, sh0, win0, "inproj_fwd0")
    uc0 = _inproj_fwd(cx, scc, shc, win0[:2], "inproj_fwd_ctx")
    y0 = _rglru_fwd(uu0, uc0, *lru_args)
    wout0_g, win1, wout1_g, wp_g = _push_wait(w_send, w_recv, later, later_lands, "same", y0, "gather_weights1_wait")
    win = [win0, win1]
    wout = [wout0_g.reshape(e, d), wout1_g.reshape(e, d)]
    wp = wp_g.reshape(N_CHIPS, ng, pq, pg)
    br0, x1 = _outproj_fwd(y0, ug0, xs, gt0, wout[0], lg[0], lb[0], None, "outproj_fwd0")
    uu1, ug1 = _inproj_fwd(x1, sc1, sh1, win[1], "inproj_fwd1")
    d1 = _pool_map(uu1, nb, False, True, "pool_fwd")
    y1 = _pool_mm_fwd(d1, wp, scale_f)
    br1, dxo, loss_part = _outproj_fwd(y1, ug1, x1, gt1, wout[1], lg[1], lb[1], target, "outproj_fwd1")

    row_wout = d
    row_tail = d + es
    wq = 2 * (nb // N_CHIPS) * LANE * LANE // width
    whole = lambda r: (r + 2 * RS_TILE - 1) // (2 * RS_TILE) * (2 * RS_TILE)
    rows1 = whole(row_tail + pg // 2)
    rows0 = whole(row_tail + 2 * wq)
    fresh = lambda rows, used: (lax.empty if rows == used else jnp.zeros)((N_CHIPS, rows, width), F32)
    gbuf1 = fresh(rows1, row_tail + pg // 2)
    gbuf0 = fresh(rows0, row_tail + 2 * wq)

    dy1, dg1, dxres1, dbr1, dlg1, dlb1, dgt1 = _outproj_bwd(dxo, x1, br1, y1, ug1, gt1, lg[1], wout[1], "outproj_bwd1",
                                                            dy_dtype=BF16)
    gbuf1 = _outproj_bwd_w(y1, ug1, dbr1, gbuf1, row_wout, "outproj_bwd_w1")
    dd1, gbuf1, dscale = _pool_mm_bwd(dy1, d1, wp, scale_f, gbuf1, row_tail)
    du1 = _pool_map(dd1, nb, True, False, "pool_bwd")
    dx1, dsc1, dsh1 = _inproj_bwd_x([du1, dg1], x1, dxres1, sc1, win[1], "inproj_bwd_x1")
    gbuf1 = _inproj_bwd_w(x1, sc1, sh1, [du1, dg1], None, gbuf1, "inproj_bwd_w1")
    rs1, token1 = _rs_begin(gbuf1, place, "rs_exchange1")

    dy0, dg0, dxres0, dbr0, dlg0, dlb0, dgt0 = _outproj_bwd(dx1, xs, br0, y0, ug0, gt0 + token1[0:1, 0:1], lg[0], wout[0],
                                                            "outproj_bwd0")
    gbuf0 = _outproj_bwd_w(y0, ug0, dbr0, gbuf0, row_wout, "outproj_bwd_w0")
    du0, duc, dconv_w, dconv_b, dwa, dwx, dba, dbx, dlam = _rglru_bwd(uu0, uc0, dy0, *lru_args)
    dwin0c = _inproj_bwd_w(cx, scc, shc, [duc, jnp.zeros_like(duc)], None, None, "inproj_bwd_w_ctx")
    gbuf0 = _inproj_bwd_w(xs, sc0, sh0, [du0, dg0], dwin0c, gbuf0, "inproj_bwd_w0")

    def quarter(dw):
        t = dw.reshape(2, N_CHIPS, nb // N_CHIPS, LANE, LANE)
        return jnp.transpose(t, (1, 3, 0, 2, 4)).reshape(N_CHIPS, LANE, 2 * (nb // N_CHIPS) * LANE).reshape(N_CHIPS, wq, width)

    tail0 = jnp.concatenate([quarter(dwa), quarter(dwx)], axis=1)
    gbuf0 = lax.dynamic_update_slice(gbuf0, tail0, (0, row_tail, 0))
    red1 = _rs_end(rs1, place, gbuf0, "rs_exchange1")
    rs0, token0 = _rs_begin(gbuf0, place, "rs_exchange0")
    grad_x, dsc0, dsh0 = _inproj_bwd_x([du0, dg0], xs, dxres0, sc0 + token0[0:1, 0:1], win[0], "inproj_bwd_x0")
    dscc, dshc = _inproj_bwd_x([duc], cx, None, scc, win[0][:2], "inproj_bwd_x_ctx")

    k0 = VEC_KINDS
    vec = _rows_kernel(
        [(c, 0, 0), (loss_part, 0, d), (dsh0, 1, 0), (dsc0, 1, d), (dgt0, 1, 2 * d), (dshc, 2, 0), (dscc, 2, d),
         (dsh1, 3, 0), (dsc1, 3, d), (dgt1, 3, 2 * d),
         (dconv_b, k0, 0), (dlg0, k0, e), (dscale, k0 + 1, 0), (dlg1, k0 + 1, e), (dlb0, k0 + 2, 0), (dlb1, k0 + 2, d),
         (dconv_w, k0 + 3, 0), (dba, k0 + 7, 0), (dbx, k0 + 9, 0), (dlam, k0 + 11, 0)], VEC_ROWS, 3 * d, "pack_vec")
    v_send, v_recv, vec_l, vec_lands, v_token = _push_start([vec], [_own_slab(vec, devices=True)], "devices", vec,
                                                            "gather_devices_start")
    red0 = _rs_end(rs0, place, v_token, "rs_exchange0")
    quarters = red0[row_tail:row_tail + 2 * wq]
    q_send, q_recv, q_src, q_lands, q_token = _push_start([quarters], [_own_slab(quarters)], "same", red0,
                                                          "gather_replicated_start")

    tmw = _row_tile(d, 256)
    red_src = lambda red, r0, tm: (red, (tm, width), lambda n, i: (r0 // tm + i, 0))
    by_layer = lambda n, gs: jnp.where(n == 0, gs[0], gs[1])
    outs = {}
    outs["w_in"] = _adamw_param(w_in, m_w_in, v_w_in, [red_src(red0, 0, tmw), red_src(red1, 0, tmw)], by_layer, tmw, "adamw_w_in",
                                after=q_token)
    outs["w_out"] = _adamw_param(w_out, m_w_out, v_w_out, [red_src(red0, row_wout, tmw), red_src(red1, row_wout, tmw)],
                                 by_layer, tmw, "adamw_w_out")
    pw = [a.reshape(ng, pq, pg) for a in (pool_w, m_pool_w, v_pool_w)]
    outs["pool_w"] = [o.reshape(pool_w.shape) for o in _adamw_param(
        *pw, [(red1, (pq, pg), lambda n, i: (row_tail // pq + n // 2, n % 2))], lambda n, gs: gs[0], pq, "adamw_pool_w")]

    (gathered,) = _push_wait(v_send, v_recv, vec_l, vec_lands, "devices", outs["w_out"][1], "gather_devices_wait")
    gt_all = jnp.swapaxes(gathered, 0, 1)
    g_wmod = _mod_bwd_shard(gt_all, cctx2, place, c3)
    g_bmod, sq_err, g_small = _mod_bwd_rep(gt_all, d)
    loss = sq_err[0, 0] * (0.5 / d)
    cpart = _cctx_partial(gt_all, wm_mine[0], place)
    x_send, x_recv, x_src, x_lands, x_token = _push_start([cpart], [_own_slab(cpart)], "same", cpart, "gather_cctx_start")
    outs["w_mod"] = _adamw_param(w_mod, m_w_mod, v_w_mod, [(g_wmod, (None, tmw, c3), lambda n, i: (n, i, 0))],
                                 lambda n, gs: gs[0], tmw, "adamw_w_mod", after=x_token)
    (rep,) = _push_wait(q_send, q_recv, q_src, q_lands, "same", outs["w_mod"][1], "gather_replicated_wait")
    bq = nb // N_CHIPS
    rep_src = lambda r0: (rep, (None, LANE, bq * LANE), lambda n, i: (n % N_CHIPS, r0 // LANE, n // N_CHIPS))
    stack = lambda n, gs: jnp.concatenate([gs[0][:, k * LANE:(k + 1) * LANE] for k in range(bq)], axis=0)
    for name, r0, trio in (("lru_wa", 0, (lru_wa, m_lru_wa, v_lru_wa)), ("lru_wx", wq, (lru_wx, m_lru_wx, v_lru_wx))):
        blocks = [a.reshape(2 * N_CHIPS, bq * LANE, LANE) for a in trio]
        outs[name] = [o.reshape(lru_wa.shape) for o in _adamw_param(*blocks, [rep_src(r0)], stack, bq * LANE, "adamw_" + name)]

    (cparts,) = _push_wait(x_send, x_recv, x_src, x_lands, "same", outs["lru_wx"][1], "gather_cctx_wait")
    g_small = dict(g_small, c_ctx=_cctx_finish(cparts, cctx2), b_mod=g_bmod)
    for n in SMALL_GATHERED:
        g_small[n] = lax.dynamic_slice_in_dim(g_small[n], place[1] * eq, eq, axis=1)
    as2d = lambda a: a.reshape(-1, a.shape[-1])
    quads = [(as2d(weights[n]), g_small[n], as2d(mom1[n]), as2d(mom2[n])) for n in SMALL_UPDATED]
    for n, (q, res) in zip(SMALL_UPDATED, zip(quads, _adamw_small(quads))):
        outs[n] = [a.reshape(weights[n].shape) for a in (q[1],) + res]

    result = [loss, grad_x[None]]
    for j in range(4):
        result += [outs[n][j] for n in WEIGHTS]
    return tuple(result)
```

```python
import jax
import jax.numpy as jnp
from jax import lax
from jax.experimental import pallas as pl
from jax.experimental.pallas import tpu as pltpu

F32 = jnp.float32
BF16 = jnp.bfloat16
LANE = 128
SUB = 8
GRID_W = 64
POOL_WINDOWS = (2, 4, 8, 16)
LRU_C = 8.0
DEPTH = 2
ALPHA = float((2 * DEPTH) ** 0.25)
LN_EPS = 1e-5
ADAM_LR, ADAM_B1, ADAM_B2, ADAM_EPS, ADAM_WD, ADAM_STEP = 0.001, 0.9, 0.999, 1e-08, 0.01, 10
N_CHIPS = 4
N_DEV = 8
MESH = pl.DeviceIdType.MESH
ROW_TILE = 1024
GATE_TILE = 8192
GATE_BWD_TILE = 2048
GATE_UNROLL = 1
CONV_TAPS = 4
CONV_LEFT = 2
PAD = 8
SCAN_UNROLL = 32
RS_TILE = 448
LN_ROWS = 128
POOL_CPAD = 16
VEC_KINDS = 4


def _call(body, **kw):
    return pl.pallas_call(body, **kw)


def _dot(a, b):
    return jnp.dot(a, b, preferred_element_type=F32)


def _dot_nt(a, b):
    return lax.dot_general(a, b, (((1,), (1,)), ((), ())), preferred_element_type=F32)


def _dot_tn(a, b):
    return lax.dot_general(a, b, (((0,), (0,)), ((), ())), preferred_element_type=F32)


def _sigmoid(v):
    return 0.5 * (jnp.tanh(0.5 * v) + 1.0)


def _silu(v):
    return v * _sigmoid(v)


def _dsilu(v):
    s = _sigmoid(v)
    return s * (1.0 + v * (1.0 - s))


def _log_sigmoid(v):
    z = jnp.exp(-jnp.abs(v))
    return jnp.minimum(v, 0.0) - jnp.where(z < 1e-4, z * (1.0 - 0.5 * z), jnp.log(1.0 + z))


def _one_minus_sq(la, a):
    return jnp.tanh(la) * (-1.0 - a * a)


def _cat(ref, n):
    return jnp.concatenate([ref[k] for k in range(n)], axis=1)


def _put_chunks(ref, val, n, base=0):
    for k in range(n):
        ref[base + k] = val[:, k * LANE:(k + 1) * LANE].astype(ref.dtype)


def _row_tile(rows, want):
    t = min(rows, want)
    assert rows % t == 0
    return t


ANY_SPEC = pl.BlockSpec(memory_space=pl.ANY)


def _mod_fwd(c_all, cctx, wm, bm):
    nl, d, c3 = wm.shape

    def body(c_ref, cx_ref, w_ref, b_ref, o_ref):
        cc = jnp.concatenate([c_ref[...], cx_ref[...], jnp.zeros((SUB - 1, d), F32)], axis=0)
        o_ref[...] = _dot(_silu(cc).astype(BF16), w_ref[...]) + b_ref[...]

    return _call(
        body, name="mod_fwd", grid=(nl,),
        in_specs=[pl.BlockSpec((N_DEV, d), lambda l: (0, 0)),
                  pl.BlockSpec((1, d), lambda l: (0, 0)),
                  pl.BlockSpec((None, d, c3), lambda l: (l, 0, 0)),
                  pl.BlockSpec((None, 1, c3), lambda l: (l, 0, 0))],
        out_specs=pl.BlockSpec((None, 2 * SUB, c3), lambda l: (l, 0, 0)),
        out_shape=jax.ShapeDtypeStruct((nl, 2 * SUB, c3), F32),
    )(c_all, cctx, wm, bm)


def _rows_kernel(parts, rows, cols, name):
    def body(*refs):
        o_ref = refs[-1]
        o_ref[...] = jnp.zeros_like(o_ref)
        for ref, (a, r0, c0) in zip(refs[:-1], parts):
            for k in range(a.shape[0]):
                o_ref[r0 + k:r0 + k + 1, c0:c0 + a.shape[1]] = ref[k:k + 1, :]

    return _call(body, name=name, grid=(1,),
                 in_specs=[pl.BlockSpec(a.shape, lambda i: (0, 0)) for a, _, _ in parts],
                 out_specs=pl.BlockSpec((rows, cols), lambda i: (0, 0)),
                 out_shape=jax.ShapeDtypeStruct((rows, cols), F32))(*[a for a, _, _ in parts])


def _mod_bwd_shard(gt, cctx, place, c3):
    d = cctx.shape[1]

    def body(p_ref, cs_ref, dm_ref, dmx_ref, cx_ref, o_ref):
        l = pl.program_id(0)
        lhs = jnp.concatenate([_silu(cs_ref[...]), _silu(cx_ref[...]), jnp.zeros((7, d), F32)], axis=0).astype(BF16)
        dmx = jnp.where(l == 0, jnp.sum(dmx_ref[...], axis=0, keepdims=True), 0.0)
        rhs = jnp.concatenate([dm_ref[...], dmx, jnp.zeros((7, c3), F32)], axis=0).astype(BF16)
        o_ref[...] = _dot_tn(lhs, rhs)

    return _call(
        body, name="mod_bwd_shard",
        grid_spec=pltpu.PrefetchScalarGridSpec(
            num_scalar_prefetch=1, grid=(DEPTH,),
            in_specs=[pl.BlockSpec((None, N_DEV, d), lambda l, p: (0, 0, 0)),
                      pl.BlockSpec((None, N_DEV, c3), lambda l, p: (1 + 2 * l, 0, p[1])),
                      pl.BlockSpec((None, N_DEV, c3), lambda l, p: (2, 0, p[1])),
                      pl.BlockSpec((1, d), lambda l, p: (0, 0))],
            out_specs=pl.BlockSpec((None, d, c3), lambda l, p: (l, 0, 0))),
        out_shape=jax.ShapeDtypeStruct((DEPTH, d, c3), F32),
    )(place, gt, gt, gt, cctx)


def _small_layout(d, e):
    k = VEC_KINDS
    return {
        "conv_b": ((1, e), [(0, k, 0)]),
        "ln_g": ((2, d), [(0, k, e), (1, k + 1, e)]),
        "pool_scale": ((1, e), [(0, k + 1, 0)]),
        "ln_b": ((2, d), [(0, k + 2, 0), (1, k + 2, d)]),
        "conv_w": ((CONV_TAPS, e), [(t, k + 3 + t, 0) for t in range(CONV_TAPS)]),
        "lru_ba": ((2, e), [(j, k + 7 + j, 0) for j in range(2)]),
        "lru_bx": ((2, e), [(j, k + 9 + j, 0) for j in range(2)]),
        "lru_lam": ((2, e), [(j, k + 11 + j, 0) for j in range(2)]),
    }


VEC_ROWS = 24


def _mod_bwd_rep(gt, d):
    d3 = gt.shape[2]
    layout = _small_layout(d, d3 - d)
    names = list(layout)

    def body(g_ref, db_ref, loss_ref, *small_refs):
        loss_ref[...] = jnp.zeros_like(loss_ref) + jnp.sum(g_ref[0][:, d:d + LANE])
        dm0 = jnp.sum(g_ref[1], axis=0, keepdims=True)
        dmx = jnp.sum(g_ref[2], axis=0, keepdims=True)
        dm1 = jnp.sum(g_ref[3], axis=0, keepdims=True)
        db_ref[0:1, :] = dm0 + dmx
        db_ref[1:2, :] = dm1
        for ref, name in zip(small_refs, names):
            shape, places = layout[name]
            for arr_row, vec_row, col0 in places:
                total = jnp.sum(g_ref[vec_row], axis=0, keepdims=True)
                ref[arr_row:arr_row + 1, :] = total[:, col0:col0 + shape[1]]

    outs = _call(
        body, name="mod_bwd_rep", grid=(1,),
        in_specs=[pl.BlockSpec(gt.shape, lambda i: (0, 0, 0))],
        out_specs=[pl.BlockSpec((DEPTH, d3), lambda i: (0, 0)), pl.BlockSpec((1, LANE), lambda i: (0, 0))]
        + [pl.BlockSpec(layout[n][0], lambda i: (0, 0)) for n in names],
        out_shape=[jax.ShapeDtypeStruct((DEPTH, d3), F32), jax.ShapeDtypeStruct((1, LANE), F32)]
        + [jax.ShapeDtypeStruct(layout[n][0], F32) for n in names],
    )(gt)
    return outs[0], outs[1], dict(zip(names, outs[2:]))


def _cctx_partial(gt, wm0, place):
    d, c3 = wm0.shape

    def body(p_ref, dmx_ref, w_ref, o_ref):
        dmx = jnp.sum(dmx_ref[...], axis=0, keepdims=True)
        o_ref[...] = _dot_nt(jnp.broadcast_to(dmx, (2 * SUB, c3)).astype(BF16), w_ref[...])

    return _call(
        body, name="cctx_partial",
        grid_spec=pltpu.PrefetchScalarGridSpec(
            num_scalar_prefetch=1, grid=(1,),
            in_specs=[pl.BlockSpec((None, N_DEV, c3), lambda i, p: (2, 0, p[1])), pl.BlockSpec((d, c3), lambda i, p: (0, 0))],
            out_specs=pl.BlockSpec((2 * SUB, d), lambda i, p: (0, 0))),
        out_shape=jax.ShapeDtypeStruct((2 * SUB, d), F32),
    )(place, gt, wm0)


def _cctx_finish(parts, cctx):
    d = cctx.shape[1]

    def body(p_ref, cx_ref, o_ref):
        total = (p_ref[0, 0:1, :] + p_ref[1, 0:1, :]) + (p_ref[2, 0:1, :] + p_ref[3, 0:1, :])
        o_ref[...] = total * _dsilu(cx_ref[...])

    return _call(body, name="cctx_finish", grid=(1,),
                 in_specs=[pl.BlockSpec(parts.shape, lambda i: (0, 0, 0)), pl.BlockSpec((1, d), lambda i: (0, 0))],
                 out_specs=pl.BlockSpec((1, d), lambda i: (0, 0)),
                 out_shape=jax.ShapeDtypeStruct((1, d), F32))(parts, cctx)


def _inproj_fwd(xin, sc1, sh, w, name):
    rows, d = xin.shape
    ns, _, n4 = w.shape
    cpb = n4 // LANE
    tm = _row_tile(rows, 512)
    assert ns in (2, 4)

    def body(x_ref, sc_ref, sh_ref, w_ref, *o_refs):
        h = (x_ref[...] * sc_ref[...] + sh_ref[...]).astype(BF16)
        for s in range(ns):
            _put_chunks(o_refs[s // 2], _dot(h, w_ref[s]), cpb, base=(s % 2) * cpb)

    spec = pl.BlockSpec((2 * cpb, tm, LANE), lambda i: (0, i, 0))
    dtypes = (F32, BF16)[:ns // 2]
    res = _call(
        body, name=name, grid=(rows // tm,),
        in_specs=[pl.BlockSpec((tm, d), lambda i: (i, 0)),
                  pl.BlockSpec((1, d), lambda i: (0, 0)),
                  pl.BlockSpec((1, d), lambda i: (0, 0)),
                  pl.BlockSpec((ns, d, n4), lambda i: (0, 0, 0))],
        out_specs=[spec] * len(dtypes),
        out_shape=[jax.ShapeDtypeStruct((2 * cpb, rows, LANE), t) for t in dtypes],
    )(xin, sc1, sh, w)
    return res[0] if ns == 2 else tuple(res)


def _inproj_bwd_x(dparts, xin, dxres, sc1, w, name):
    rows, d = xin.shape
    npart = len(dparts)
    e = dparts[0].shape[1]
    ns, _, n4 = w.shape
    per = e // n4
    assert per * npart == ns
    tm = _row_tile(rows, 512)
    has_res = dxres is not None

    def body(*refs):
        dp = refs[:npart]
        x_ref, sc_ref, w_ref = refs[npart:npart + 3]
        rest = refs[npart + 3:]
        if has_res:
            res_ref, dx_ref, dsc_ref, dsh_ref = rest
        else:
            dsc_ref, dsh_ref = rest
        i = pl.program_id(0)
        dh = jnp.zeros((tm, d), F32)
        for p in range(npart):
            v = dp[p][...]
            for q in range(per):
                dh = dh + _dot_nt(v[:, q * n4:(q + 1) * n4], w_ref[p * per + q])

        @pl.when(i == 0)
        def _():
            dsc_ref[...] = jnp.zeros_like(dsc_ref)
            dsh_ref[...] = jnp.zeros_like(dsh_ref)

        dsc_ref[...] += jnp.sum(dh * x_ref[...], axis=0, keepdims=True)
        dsh_ref[...] += jnp.sum(dh, axis=0, keepdims=True)
        if has_res:
            dx_ref[...] = res_ref[...] + dh * sc_ref[...]

    row_spec = pl.BlockSpec((tm, d), lambda i: (i, 0))
    vec_spec = pl.BlockSpec((1, d), lambda i: (0, 0))
    in_specs = [pl.BlockSpec((tm, e), lambda i: (i, 0))] * npart + [row_spec, vec_spec,
                                                                     pl.BlockSpec((ns, d, n4), lambda i: (0, 0, 0))]
    args = list(dparts) + [xin, sc1, w]
    out_specs, out_shape = [vec_spec, vec_spec], [jax.ShapeDtypeStruct((1, d), F32)] * 2
    if has_res:
        in_specs.append(row_spec)
        args.append(dxres)
        out_specs = [row_spec] + out_specs
        out_shape = [jax.ShapeDtypeStruct((rows, d), F32)] + out_shape
    return _call(body, name=name, grid=(rows // tm,), in_specs=in_specs, out_specs=out_specs, out_shape=out_shape)(*args)


def _inproj_bwd_w(xin, sc1, sh, dparts, init, gbuf, name):
    rows, d = xin.shape
    npart = len(dparts)
    e = dparts[0].shape[1]
    n4 = e // 2
    ns = 2 * npart
    tm = _row_tile(rows, 1024)
    nt = rows // tm
    has_init = init is not None
    into = gbuf is not None
    assert not into or (ns == N_CHIPS and gbuf.shape[2] == n4)

    def body(*refs):
        x_ref, sc_ref, sh_ref = refs[:3]
        dp = refs[3:3 + npart]
        init_ref = refs[3 + npart] if has_init else None
        o_ref = refs[-1]
        s, i = pl.program_id(0), pl.program_id(1)
        h = (x_ref[...] * sc_ref[...] + sh_ref[...]).astype(BF16)

        @pl.when(i == 0)
        def _():
            o_ref[...] = init_ref[...] if has_init else jnp.zeros_like(o_ref)

        for p in range(npart):
            @pl.when(s // 2 == p)
            def _(p=p):
                o_ref[...] += _dot_tn(h, dp[p][...])

    in_specs = [pl.BlockSpec((tm, d), lambda s, i: (i, 0)),
                pl.BlockSpec((1, d), lambda s, i: (0, 0)),
                pl.BlockSpec((1, d), lambda s, i: (0, 0))]
    in_specs += [pl.BlockSpec((tm, n4), lambda s, i: (i, s % 2))] * npart
    args = [xin, sc1, sh] + list(dparts)
    o_spec = pl.BlockSpec((None, d, n4), lambda s, i: (s, 0, 0))
    if has_init:
        in_specs.append(o_spec)
        args.append(init)
    extra = {}
    if into:
        in_specs.append(ANY_SPEC)
        args.append(gbuf)
        extra = dict(input_output_aliases={len(args) - 1: 0})
    out_shape = jax.ShapeDtypeStruct(gbuf.shape if into else (ns, d, n4), F32)
    return _call(body, name=name, grid=(ns, nt), in_specs=in_specs, out_specs=o_spec, out_shape=out_shape, **extra)(*args)


def _gated(y_ref, g_ref, nch):
    return jnp.concatenate([(y_ref[k].astype(F32) * _silu(g_ref[k].astype(F32))).astype(BF16) for k in range(nch)], axis=1)


def _ln_stats(r):
    mu = jnp.mean(r, axis=-1, keepdims=True)
    var = jnp.mean(jnp.square(r - mu), axis=-1, keepdims=True)
    rstd = lax.rsqrt(var + LN_EPS)
    return (r - mu) * rstd, rstd


def _outproj_fwd(y, ug, xin, gt, wout, lg, lb, target, name):
    nch, rows, _ = y.shape
    e, d = wout.shape
    tm = _row_tile(rows, 512)
    with_loss = target is not None

    def body(*refs):
        y_ref, g_ref, x_ref, gt_ref, w_ref, lg_ref, lb_ref = refs[:7]
        if with_loss:
            t_ref, br_ref, dxo_ref, loss_ref = refs[7:]
        else:
            br_ref, xo_ref = refs[7:]
        z = _gated(y_ref, g_ref, nch)
        br_ref[...] = _dot(z, w_ref[...])
        if with_loss:
            @pl.when(pl.program_id(0) == 0)
            def _():
                loss_ref[...] = jnp.zeros_like(loss_ref)

        def norm(j, c):
            rows = pl.ds(pl.multiple_of(j * LN_ROWS, LN_ROWS), LN_ROWS)
            xhat, _ = _ln_stats(ALPHA * x_ref[rows, :] + gt_ref[...] * br_ref[rows, :])
            xo = xhat * lg_ref[...] + lb_ref[...]
            if with_loss:
                err = xo - t_ref[rows, :]
                dxo_ref[rows, :] = err * (1.0 / d)
                col = jnp.sum(err * err, axis=0, keepdims=True)
                loss_ref[...] += sum(col[:, k * LANE:(k + 1) * LANE] for k in range(d // LANE))
            else:
                xo_ref[rows, :] = xo
            return c

        lax.fori_loop(0, tm // LN_ROWS, norm, 0)

    chunk_spec = pl.BlockSpec((nch, tm, LANE), lambda i: (0, i, 0))
    g_spec = chunk_spec
    row_spec = pl.BlockSpec((tm, d), lambda i: (i, 0))
    vec_spec = pl.BlockSpec((1, d), lambda i: (0, 0))
    in_specs = [chunk_spec, g_spec, row_spec, vec_spec, pl.BlockSpec((e, d), lambda i: (0, 0)), vec_spec, vec_spec]
    args = [y, ug, xin, gt, wout, lg, lb]
    out_specs = [row_spec, row_spec]
    out_shape = [jax.ShapeDtypeStruct((rows, d), F32)] * 2
    if with_loss:
        in_specs.append(row_spec)
        args.append(target)
        out_specs.append(pl.BlockSpec((1, LANE), lambda i: (0, 0)))
        out_shape.append(jax.ShapeDtypeStruct((1, LANE), F32))
    return _call(body, name=name, grid=(rows // tm,), in_specs=in_specs, out_specs=out_specs, out_shape=out_shape)(*args)


def _outproj_bwd(dxo, xin, br, y, ug, gt, lg, wout, name, dy_dtype=F32):
    nch, rows, _ = y.shape
    e, d = wout.shape
    tm = _row_tile(rows, 256)

    def body(dxo_ref, x_ref, br_ref, y_ref, g_ref, gt_ref, lg_ref, w_ref,
             dy_ref, dg_ref, dxres_ref, dbr_ref, dlg_ref, dlb_ref, dgt_ref):
        @pl.when(pl.program_id(0) == 0)
        def _():
            dlg_ref[...] = jnp.zeros_like(dlg_ref)
            dlb_ref[...] = jnp.zeros_like(dlb_ref)
            dgt_ref[...] = jnp.zeros_like(dgt_ref)

        def norm_bwd(j, c):
            rows = pl.ds(pl.multiple_of(j * LN_ROWS, LN_ROWS), LN_ROWS)
            dxo_v = dxo_ref[rows, :]
            brv = br_ref[rows, :]
            xhat, rstd = _ln_stats(ALPHA * x_ref[rows, :] + gt_ref[...] * brv)
            dxh = dxo_v * lg_ref[...]
            dr = rstd * (dxh - jnp.mean(dxh, axis=-1, keepdims=True) - xhat * jnp.mean(dxh * xhat, axis=-1, keepdims=True))
            dlg_ref[...] += jnp.sum(dxo_v * xhat, axis=0, keepdims=True)
            dlb_ref[...] += jnp.sum(dxo_v, axis=0, keepdims=True)
            dgt_ref[...] += jnp.sum(dr * brv, axis=0, keepdims=True)
            dxres_ref[rows, :] = ALPHA * dr
            dbr_ref[rows, :] = (gt_ref[...] * dr).astype(BF16)
            return c

        lax.fori_loop(0, tm // LN_ROWS, norm_bwd, 0)
        dz = _dot_nt(dbr_ref[...], w_ref[...])
        for k in range(nch):
            dzk = dz[:, k * LANE:(k + 1) * LANE]
            gk = g_ref[k].astype(F32)
            sk = _sigmoid(gk)
            dy_ref[k] = (dzk * (gk * sk)).astype(dy_ref.dtype)
            dg_ref[:, k * LANE:(k + 1) * LANE] = (dzk * y_ref[k].astype(F32) * (sk * (1.0 + gk * (1.0 - sk)))).astype(BF16)

    chunk_spec = pl.BlockSpec((nch, tm, LANE), lambda i: (0, i, 0))
    g_spec = chunk_spec
    row_spec = pl.BlockSpec((tm, d), lambda i: (i, 0))
    vec_spec = pl.BlockSpec((1, d), lambda i: (0, 0))
    return _call(
        body, name=name, grid=(rows // tm,),
        in_specs=[row_spec, row_spec, row_spec, chunk_spec, g_spec, vec_spec, vec_spec, pl.BlockSpec((e, d), lambda i: (0, 0))],
        out_specs=[chunk_spec, pl.BlockSpec((tm, e), lambda i: (i, 0)), row_spec, row_spec, vec_spec, vec_spec, vec_spec],
        out_shape=[jax.ShapeDtypeStruct((nch, rows, LANE), dy_dtype), jax.ShapeDtypeStruct((rows, e), BF16),
                   jax.ShapeDtypeStruct((rows, d), F32), jax.ShapeDtypeStruct((rows, d), BF16)]
        + [jax.ShapeDtypeStruct((1, d), F32)] * 3,
    )(dxo, xin, br, y, ug, gt, lg, wout)


def _outproj_bwd_w(y, ug, dbr, gbuf, row0, name):
    nch, rows, _ = y.shape
    d = dbr.shape[1]
    e = nch * LANE
    es = e // N_CHIPS
    tm = _row_tile(rows, 1024)
    assert gbuf.shape[2] == d and row0 % es == 0

    def body(y_ref, g_ref, dbr_ref, buf_ref, o_ref):
        @pl.when(pl.program_id(0) == 0)
        def _():
            o_ref[...] = jnp.zeros_like(o_ref)

        z = _gated(y_ref, g_ref, nch)
        o_ref[...] += _dot_tn(z, dbr_ref[...]).reshape(N_CHIPS, es, d)

    return _call(
        body, name=name, grid=(rows // tm,),
        in_specs=[pl.BlockSpec((nch, tm, LANE), lambda i: (0, i, 0)),
                  pl.BlockSpec((nch, tm, LANE), lambda i: (0, i, 0)),
                  pl.BlockSpec((tm, d), lambda i: (i, 0)),
                  ANY_SPEC],
        out_specs=pl.BlockSpec((N_CHIPS, es, d), lambda i: (0, row0 // es, 0)),
        out_shape=jax.ShapeDtypeStruct(gbuf.shape, F32),
        input_output_aliases={3: 0},
    )(y, ug, dbr, gbuf)


def _scan(a_ref, b_ref, h_ref, *, length, init, reverse, a_shift, store):
    nblk = length // SUB
    unroll = min(SCAN_UNROLL, nblk)
    assert nblk % unroll == 0
    row = lax.broadcasted_iota(jnp.int32, (SUB, LANE), 0)
    last = 0 if reverse else SUB - 1
    edges = [(row >= SUB - k) if reverse else (row < k) for k in (1, 2, 4)]

    def local_scan(a, b):
        for k, edge in zip((1, 2, 4), edges):
            sh = (SUB - k) if reverse else k
            b = b + a * jnp.where(edge, 0.0, pltpu.roll(b, sh, 0))
            a = a * jnp.where(edge, 1.0, pltpu.roll(a, sh, 0))
        return a, b

    def step(i, carry):
        base = pl.multiple_of(((nblk // unroll - 1 - i) if reverse else i) * (unroll * SUB), unroll * SUB)
        order = range(unroll - 1, -1, -1) if reverse else range(unroll)
        loaded = [(a_ref[pl.ds(PAD + base + j * SUB + a_shift, SUB), :], b_ref[pl.ds(PAD + base + j * SUB, SUB), :])
                  for j in order]
        scanned = [local_scan(a, b) for a, b in loaded]
        for j, (a, b) in zip(order, scanned):
            if store:
                h_ref[pl.ds(PAD + base + j * SUB, SUB), :] = b + a * carry
            a_l = jnp.broadcast_to(a[last:last + 1, :], (SUB, LANE))
            b_l = jnp.broadcast_to(b[last:last + 1, :], (SUB, LANE))
            carry = b_l + a_l * carry
        return carry

    carry = lax.fori_loop(0, nblk // unroll, step, jnp.broadcast_to(init, (SUB, LANE)))
    return carry[0:1, :]


def _conv_fwd(src_ref, upad, u_ref, cw, cb, length):
    zeros = jnp.zeros((PAD, LANE), F32)
    upad[pl.ds(0, PAD), :] = zeros
    upad[pl.ds(PAD + length, PAD), :] = zeros
    rt = _row_tile(length, ROW_TILE)

    def copy(i, c):
        t0 = pl.multiple_of(i * rt, rt)
        upad[pl.ds(PAD + t0, rt), :] = src_ref[pl.ds(t0, rt), :]
        return c

    lax.fori_loop(0, length // rt, copy, 0)

    def tile(i, c):
        t0 = pl.multiple_of(i * rt, rt)
        acc = jnp.zeros((rt, LANE), F32)
        for k in range(CONV_TAPS):
            acc = acc + upad[pl.ds(t0 + PAD - CONV_LEFT + k, rt), :] * cw[k:k + 1, :]
        u_ref[pl.ds(t0, rt), :] = acc + cb
        return c

    lax.fori_loop(0, length // rt, tile, 0)


def _gates_fwd(u_ref, a_ref, b_ref, wa, wx, ba, bx, ls, length, keep=None):
    rt = _row_tile(length, GATE_TILE)
    ls_c = LRU_C * ls

    def tile(i, c):
        t0 = pl.multiple_of(i * rt, rt)
        ut = u_ref[pl.ds(t0, rt), :]
        ub = ut.astype(BF16)
        r = 0.5 * (jnp.tanh(_dot(ub, wa) + ba) + 1.0)
        ig = 0.5 * (jnp.tanh(_dot(ub, wx) + bx) + 1.0)
        if keep is not None:
            keep[0][pl.ds(t0, rt), :] = r
            keep[1][pl.ds(t0, rt), :] = ig
        la = r * ls_c
        a = jnp.exp(la)
        a_ref[pl.ds(PAD + t0, rt), :] = a
        q = _one_minus_sq(la, a)
        b_ref[pl.ds(PAD + t0, rt), :] = jnp.where(q > 0.0, q * lax.rsqrt(q), 0.0) * (ig * ut)
        return c

    lax.fori_loop(0, length // rt, tile, 0, unroll=min(GATE_UNROLL, length // rt))


def _lru_specs():
    return [pl.BlockSpec((CONV_TAPS, LANE), lambda n: (0, n)),
            pl.BlockSpec((1, LANE), lambda n: (0, n)),
            pl.BlockSpec((2, None, LANE, LANE), lambda n: (0, n, 0, 0)),
            pl.BlockSpec((2, None, LANE, LANE), lambda n: (0, n, 0, 0)),
            pl.BlockSpec((2, LANE), lambda n: (0, n)),
            pl.BlockSpec((2, LANE), lambda n: (0, n)),
            pl.BlockSpec((2, LANE), lambda n: (0, n))]


def _rglru_fwd(ug, uc, conv_w, conv_b, wa, wx, ba, bx, lam):
    nb = uc.shape[0]
    s_len, t_len = ug.shape[1], uc.shape[1]

    def body(u0_ref, uc0_ref, cw_ref, cb_ref, wa_ref, wx_ref, ba_ref, bx_ref, lam_ref, y_ref,
             upad, ubuf, abuf, hbuf):
        cw, cb = cw_ref[...], cb_ref[...]
        lsig = _log_sigmoid(lam_ref[...])
        zero = jnp.zeros((1, LANE), F32)
        _conv_fwd(uc0_ref, upad, ubuf, cw, cb, t_len)
        h0 = []
        for dr in range(2):
            _gates_fwd(ubuf, abuf, hbuf, wa_ref[dr], wx_ref[dr], ba_ref[dr:dr + 1, :], bx_ref[dr:dr + 1, :],
                       lsig[dr:dr + 1, :], t_len)
            h0.append(_scan(abuf, hbuf, hbuf, length=t_len, init=zero, reverse=(dr == 1), a_shift=0, store=False))
        _conv_fwd(u0_ref, upad, ubuf, cw, cb, s_len)
        rt = _row_tile(s_len, ROW_TILE)
        for dr in range(2):
            _gates_fwd(ubuf, abuf, hbuf, wa_ref[dr], wx_ref[dr], ba_ref[dr:dr + 1, :], bx_ref[dr:dr + 1, :],
                       lsig[dr:dr + 1, :], s_len)
            _scan(abuf, hbuf, hbuf, length=s_len, init=h0[dr], reverse=(dr == 1), a_shift=0, store=True)

            def acc(i, c, dr=dr):
                t0 = pl.multiple_of(i * rt, rt)
                h = hbuf[pl.ds(PAD + t0, rt), :]
                if dr == 0:
                    upad[pl.ds(PAD + t0, rt), :] = h
                else:
                    y_ref[pl.ds(t0, rt), :] = (upad[pl.ds(PAD + t0, rt), :] + h).astype(y_ref.dtype)
                return c

            lax.fori_loop(0, s_len // rt, acc, 0)

    seq = pltpu.VMEM((s_len + 2 * PAD, LANE), F32)
    return _call(
        body, name="rglru_fwd", grid=(nb,),
        in_specs=[pl.BlockSpec((None, s_len, LANE), lambda n: (n, 0, 0)),
                  pl.BlockSpec((None, t_len, LANE), lambda n: (n, 0, 0))] + _lru_specs(),
        out_specs=pl.BlockSpec((None, s_len, LANE), lambda n: (n, 0, 0)),
        out_shape=jax.ShapeDtypeStruct((nb, s_len, LANE), BF16),
        scratch_shapes=[seq, pltpu.VMEM((s_len, LANE), F32), seq, seq],
    )(ug, uc, conv_w, conv_b, wa, wx, ba, bx, lam)


def _rglru_bwd(ug, uc, dy, conv_w, conv_b, wa, wx, ba, bx, lam):
    nb = uc.shape[0]
    e = nb * LANE
    s_len, t_len = ug.shape[1], uc.shape[1]

    def body(u0_ref, uc0_ref, dy_ref, cw_ref, cb_ref, wa_ref, wx_ref, ba_ref, bx_ref, lam_ref,
             du_ref, duc_ref, dcw_ref, dcb_ref, dwa_ref, dwx_ref, dba_ref, dbx_ref, dlam_ref,
             upad, ubuf, abuf, hbuf, lbuf, dubuf, rbuf, ibuf, cpad, cu, ca0, ch0, ca1, ch1, cr0, ci0, cr1, ci1):
        cw, cb = cw_ref[...], cb_ref[...]
        lam_v = lam_ref[...]
        lsig = _log_sigmoid(lam_v)
        zero = jnp.zeros((1, LANE), F32)
        zpad = jnp.zeros((PAD, LANE), F32)
        for ref in (dcw_ref, dcb_ref, dwa_ref, dwx_ref, dba_ref, dbx_ref, dlam_ref):
            ref[...] = jnp.zeros_like(ref)

        def params(dr):
            return (wa_ref[dr], wx_ref[dr], ba_ref[dr:dr + 1, :], bx_ref[dr:dr + 1, :], lsig[dr:dr + 1, :])

        def direction_bwd(dr, u_ref, a_ref, h_ref, l_ref, gates, dub, length, first):
            wa_d, wx_d, ba_d, bx_d, ls_d = params(dr)
            rt = _row_tile(length, GATE_BWD_TILE)
            prev = 1 if dr == 1 else -1

            def tile(i, c):
                t0 = pl.multiple_of(i * rt, rt)
                ut = u_ref[pl.ds(t0, rt), :]
                ub = ut.astype(BF16)
                r = gates[0][pl.ds(t0, rt), :]
                ig = gates[1][pl.ds(t0, rt), :]
                la = r * (LRU_C * ls_d)
                a = a_ref[pl.ds(PAD + t0, rt), :]
                q = _one_minus_sq(la, a)
                rs = lax.rsqrt(q)
                sq = q * rs
                lm = l_ref[pl.ds(PAD + t0, rt), :]
                da = lm * h_ref[pl.ds(PAD + t0 + prev, rt), :]
                dsq = lm * ig * ut
                dig = lm * sq * ut
                dla = da * a - dsq * (a * a) * rs
                dr_ = dla * (LRU_C * ls_d)
                dlam_ref[dr:dr + 1, :] += jnp.sum(dla * (LRU_C * r), axis=0, keepdims=True)
                dpr = dr_ * r * (1.0 - r)
                dpi = dig * ig * (1.0 - ig)
                dba_ref[dr:dr + 1, :] += jnp.sum(dpr, axis=0, keepdims=True)
                dbx_ref[dr:dr + 1, :] += jnp.sum(dpi, axis=0, keepdims=True)
                dprb, dpib = dpr.astype(BF16), dpi.astype(BF16)
                dwa_ref[dr] += _dot_tn(ub, dprb)
                dwx_ref[dr] += _dot_tn(ub, dpib)
                dut = lm * sq * ig + 2.0 * (_dot_nt(dprb, wa_d) + _dot_nt(dpib, wx_d))
                if first:
                    dub[pl.ds(PAD + t0, rt), :] = dut
                else:
                    dub[pl.ds(PAD + t0, rt), :] += dut
                return c

            lax.fori_loop(0, length // rt, tile, 0, unroll=min(GATE_UNROLL, length // rt))

        def conv_bwd(dub, src_pad, out_ref, length):
            rt = _row_tile(length, ROW_TILE)

            def tile(i, c):
                t0 = pl.multiple_of(i * rt, rt)
                dut = dub[pl.ds(PAD + t0, rt), :]
                dcb_ref[...] += jnp.sum(dut, axis=0, keepdims=True)
                acc = jnp.zeros((rt, LANE), F32)
                for k in range(CONV_TAPS):
                    sh = CONV_LEFT - k
                    acc = acc + dub[pl.ds(PAD + t0 + sh, rt), :] * cw[k:k + 1, :]
                    dcw_ref[k:k + 1, :] += jnp.sum(dut * src_pad[pl.ds(PAD + t0 - sh, rt), :], axis=0, keepdims=True)
                out_ref[pl.ds(t0, rt), :] = acc.astype(out_ref.dtype)
                return c

            lax.fori_loop(0, length // rt, tile, 0)

        _conv_fwd(uc0_ref, cpad, cu, cw, cb, t_len)
        cbufs = ((ca0, ch0), (ca1, ch1))
        cgates = ((cr0, ci0), (cr1, ci1))
        h0 = []
        for dr in range(2):
            ca, chh = cbufs[dr]
            _gates_fwd(cu, ca, chh, *params(dr), t_len, keep=cgates[dr])
            h0.append(_scan(ca, chh, chh, length=t_len, init=zero, reverse=(dr == 1), a_shift=0, store=True))
        _conv_fwd(u0_ref, upad, ubuf, cw, cb, s_len)
        rt = _row_tile(s_len, ROW_TILE)
        dh0 = []
        for dr in range(2):
            rev = dr == 1
            _gates_fwd(ubuf, abuf, hbuf, *params(dr), s_len, keep=(rbuf, ibuf))
            _scan(abuf, hbuf, hbuf, length=s_len, init=h0[dr], reverse=rev, a_shift=0, store=True)
            first_row = PAD + s_len if rev else PAD - 1
            hbuf[pl.ds(first_row, 1), :] = h0[dr]
            end_row = PAD - 1 if rev else PAD + s_len
            abuf[pl.ds(end_row, 1), :] = zero

            def copy(i, c):
                t0 = pl.multiple_of(i * rt, rt)
                lbuf[pl.ds(PAD + t0, rt), :] = dy_ref[pl.ds(t0, rt), :]
                return c

            lax.fori_loop(0, s_len // rt, copy, 0)
            _scan(abuf, lbuf, lbuf, length=s_len, init=zero, reverse=not rev, a_shift=(-1 if rev else 1), store=True)
            start = PAD + s_len - 1 if rev else PAD
            dh0.append(abuf[pl.ds(start, 1), :] * lbuf[pl.ds(start, 1), :])
            direction_bwd(dr, ubuf, abuf, hbuf, lbuf, (rbuf, ibuf), dubuf, s_len, first=(dr == 0))
        dubuf[pl.ds(0, PAD), :] = zpad
        dubuf[pl.ds(PAD + s_len, PAD), :] = zpad
        conv_bwd(dubuf, upad, du_ref, s_len)
        lc = lbuf
        duc_buf = dubuf
        for dr in range(2):
            rev = dr == 1
            ca, chh = cbufs[dr]
            first_row = PAD + t_len if rev else PAD - 1
            chh[pl.ds(first_row, 1), :] = zero
            end_row = PAD - 1 if rev else PAD + t_len
            ca[pl.ds(end_row, 1), :] = zero + 1.0
            rtc = _row_tile(t_len, ROW_TILE)

            def clear(i, c):
                t0 = pl.multiple_of(i * rtc, rtc)
                lc[pl.ds(PAD + t0, rtc), :] = jnp.zeros((rtc, LANE), F32)
                return c

            lax.fori_loop(0, t_len // rtc, clear, 0)
            _scan(ca, lc, lc, length=t_len, init=dh0[dr], reverse=not rev, a_shift=(-1 if rev else 1), store=True)
            direction_bwd(dr, cu, ca, chh, lc, cgates[dr], duc_buf, t_len, first=(dr == 0))
        duc_buf[pl.ds(0, PAD), :] = zpad
        duc_buf[pl.ds(PAD + t_len, PAD), :] = zpad
        conv_bwd(duc_buf, cpad, duc_ref, t_len)
        dlam_ref[...] = dlam_ref[...] * (1.0 - _sigmoid(lam_v))

    seq = pltpu.VMEM((s_len + 2 * PAD, LANE), F32)
    cseq = pltpu.VMEM((t_len + 2 * PAD, LANE), F32)
    flat = pltpu.VMEM((s_len, LANE), F32)
    cflat = pltpu.VMEM((t_len, LANE), F32)
    vec2 = pl.BlockSpec((2, LANE), lambda n: (0, n))
    wspec = pl.BlockSpec((2, None, LANE, LANE), lambda n: (0, n, 0, 0))
    return _call(
        body, name="rglru_bwd", grid=(nb,),
        in_specs=[pl.BlockSpec((None, s_len, LANE), lambda n: (n, 0, 0)),
                  pl.BlockSpec((None, t_len, LANE), lambda n: (n, 0, 0)),
                  pl.BlockSpec((None, s_len, LANE), lambda n: (n, 0, 0))] + _lru_specs(),
        out_specs=[pl.BlockSpec((s_len, LANE), lambda n: (0, n)),
                   pl.BlockSpec((t_len, LANE), lambda n: (0, n)),
                   pl.BlockSpec((CONV_TAPS, LANE), lambda n: (0, n)),
                   pl.BlockSpec((1, LANE), lambda n: (0, n)),
                   wspec, wspec, vec2, vec2, vec2],
        out_shape=[jax.ShapeDtypeStruct((s_len, e), BF16), jax.ShapeDtypeStruct((t_len, e), BF16),
                   jax.ShapeDtypeStruct((CONV_TAPS, e), F32), jax.ShapeDtypeStruct((1, e), F32),
                   jax.ShapeDtypeStruct((2, nb, LANE, LANE), F32), jax.ShapeDtypeStruct((2, nb, LANE, LANE), F32),
                   jax.ShapeDtypeStruct((2, e), F32), jax.ShapeDtypeStruct((2, e), F32), jax.ShapeDtypeStruct((2, e), F32)],
        scratch_shapes=[seq, flat, seq, seq, seq, seq, flat, flat,
                        cseq, cflat, cseq, cseq, cseq, cseq, cflat, cflat, cflat, cflat],
    )(ug, uc, dy, conv_w, conv_b, wa, wx, ba, bx, lam)


def _pool_windows(src_ref, out_ref, colbuf, rowbuf, half, transpose, s_len):
    gw = GRID_W
    lg = gw.bit_length() - 1
    n_rows = s_len // gw
    cp, rm = POOL_CPAD, 8 * gw
    stride = gw + 2 * cp
    rt = _row_tile(s_len, ROW_TILE)
    assert rt % gw == 0 and half <= cp
    gpt = rt // gw
    offs = range(-half, half)
    zmargin = jnp.zeros((cp, LANE), F32)

    def zcol(r, c):
        base = pl.multiple_of(r * stride, SUB)
        colbuf[pl.ds(base, cp), :] = zmargin
        colbuf[pl.ds(base + cp + gw, cp), :] = zmargin
        return c

    lax.fori_loop(0, n_rows, zcol, 0)

    def zrow(i, c):
        t0 = pl.multiple_of(i * gw, gw)
        rowbuf[pl.ds(t0, gw), :] = jnp.zeros((gw, LANE), F32)
        rowbuf[pl.ds(rm + s_len + t0, gw), :] = jnp.zeros((gw, LANE), F32)
        return c

    lax.fori_loop(0, rm // gw, zrow, 0)

    col = lax.broadcasted_iota(jnp.int32, (gw, LANE), 0)
    ccnt = (jnp.minimum(col + half, gw) - jnp.maximum(col - half, 0)).astype(F32)

    def row_counts(t0):
        row = (t0 + lax.broadcasted_iota(jnp.int32, (rt, LANE), 0)) >> lg
        return (jnp.minimum(row + half, n_rows) - jnp.maximum(row - half, 0)).astype(F32)

    def col_base(t0, g):
        return pl.multiple_of((t0 // gw) * stride, SUB) + g * stride + cp

    def col_sum(t0, g, sign):
        acc = jnp.zeros((gw, LANE), F32)
        for o in offs:
            acc = acc + colbuf[pl.ds(col_base(t0, g) + sign * o, gw), :]
        return acc

    def row_sum(t0, sign):
        acc = jnp.zeros((rt, LANE), F32)
        for o in offs:
            acc = acc + rowbuf[pl.ds(rm + t0 + sign * o * gw, rt), :]
        return acc

    n_tiles = s_len // rt
    assert rt >= half * gw

    def loop(fn, edges=False):
        def step(i, c):
            t0 = pl.multiple_of(i * rt, rt)
            fn(t0, False) if edges else fn(t0)
            return c
        if edges:
            fn(0, True)
            if n_tiles > 1:
                fn(s_len - rt, True)
            lax.fori_loop(1, n_tiles - 1, step, 0)
        else:
            lax.fori_loop(0, n_tiles, step, 0)

    inv_ccnt = 1.0 / ccnt

    def by_row_count(v, t0, edge):
        return v / row_counts(t0) if edge else v * (1.0 / (2 * half))

    if not transpose:
        def fill(t0):
            for g in range(gpt):
                colbuf[pl.ds(col_base(t0, g), gw), :] = src_ref[pl.ds(t0 + g * gw, gw), :]

        def cols(t0):
            for g in range(gpt):
                rowbuf[pl.ds(rm + t0 + g * gw, gw), :] = col_sum(t0, g, 1) * inv_ccnt

        def rows(t0, edge):
            mean = by_row_count(row_sum(t0, 1), t0, edge)
            out_ref[pl.ds(t0, rt), :] = (mean - src_ref[pl.ds(t0, rt), :]).astype(out_ref.dtype)

        loop(fill)
        loop(cols)
        loop(rows, edges=True)
    else:
        def fill(t0, edge):
            rowbuf[pl.ds(rm + t0, rt), :] = by_row_count(src_ref[pl.ds(t0, rt), :], t0, edge)

        def rows(t0):
            acc = row_sum(t0, -1)
            for g in range(gpt):
                colbuf[pl.ds(col_base(t0, g), gw), :] = acc[g * gw:(g + 1) * gw, :] * inv_ccnt

        def cols(t0):
            for g in range(gpt):
                rows_g = pl.ds(t0 + g * gw, gw)
                out_ref[rows_g, :] = (col_sum(t0, g, -1) - src_ref[rows_g, :]).astype(out_ref.dtype)

        loop(fill, edges=True)
        loop(rows)
        loop(cols)


def _pool_map(src, nb, transpose, out_chunk_major, name):
    s_len = src.shape[1]
    cpg = nb // len(POOL_WINDOWS)

    def body(src_ref, out_ref, colbuf, rowbuf):
        n = pl.program_id(0)
        for gi, w in enumerate(POOL_WINDOWS):
            @pl.when(n // cpg == gi)
            def _(w=w):
                _pool_windows(src_ref, out_ref, colbuf, rowbuf, w // 2, transpose, s_len)

    if out_chunk_major:
        out_spec = pl.BlockSpec((None, s_len, LANE), lambda n: (n, 0, 0))
        out_shape = jax.ShapeDtypeStruct((nb, s_len, LANE), BF16)
    else:
        out_spec = pl.BlockSpec((s_len, LANE), lambda n: (0, n))
        out_shape = jax.ShapeDtypeStruct((s_len, nb * LANE), BF16)
    return _call(
        body, name=name, grid=(nb,),
        in_specs=[pl.BlockSpec((None, s_len, LANE), lambda n: (n, 0, 0))],
        out_specs=out_spec, out_shape=out_shape,
        scratch_shapes=[pltpu.VMEM((s_len // GRID_W * (GRID_W + 2 * POOL_CPAD), LANE), F32),
                        pltpu.VMEM((s_len + 16 * GRID_W, LANE), F32)],
    )(src)


def _group_weight(w_ref):
    return jnp.concatenate([w_ref[k] for k in range(N_CHIPS)], axis=0)


def _pool_mm_fwd(dm, wp, scale):
    nb, rows, _ = dm.shape
    _, ng, pq, pg = wp.shape
    cpg = pg // LANE
    tm = _row_tile(rows, 2048)

    def body(d_ref, w_ref, s_ref, y_ref):
        _put_chunks(y_ref, _dot(_cat(d_ref, cpg), _group_weight(w_ref)) * s_ref[...], cpg)

    cspec = pl.BlockSpec((cpg, tm, LANE), lambda i, g: (g, i, 0))
    return _call(
        body, name="pool_mm_fwd", grid=(rows // tm, ng),
        in_specs=[cspec, pl.BlockSpec((N_CHIPS, None, pq, pg), lambda i, g: (0, g, 0, 0)),
                  pl.BlockSpec((1, pg), lambda i, g: (0, g))],
        out_specs=cspec, out_shape=jax.ShapeDtypeStruct((nb, rows, LANE), BF16),
    )(dm, wp, scale)


def _pool_mm_bwd(dy, dm, wp, scale, gbuf, row0):
    nb, rows, _ = dm.shape
    _, ng, pq, pg = wp.shape
    cpg = pg // LANE
    tm = _row_tile(rows, 1024)
    nt = rows // tm
    assert gbuf.shape[2] == 2 * pg and row0 % pq == 0

    def body(dy_ref, d_ref, w_ref, s_ref, buf_ref, dd_ref, dwp_ref, dsc_ref, acc):
        i = pl.program_id(1)

        @pl.when(i == 0)
        def _():
            acc[...] = jnp.zeros_like(acc)
            dsc_ref[...] = jnp.zeros_like(dsc_ref)

        dyv = _cat(dy_ref, cpg).astype(F32)
        dc = _cat(d_ref, cpg)
        w = _group_weight(w_ref)
        dsc_ref[...] += jnp.sum(dyv * _dot(dc, w), axis=0, keepdims=True)
        dyp = (dyv * s_ref[...]).astype(BF16)
        _put_chunks(dd_ref, _dot_nt(dyp, w), cpg)
        acc[...] += _dot_tn(dc, dyp)

        @pl.when(i == nt - 1)
        def _():
            dwp_ref[...] = acc[...].reshape(N_CHIPS, pq, pg)

    cspec = pl.BlockSpec((cpg, tm, LANE), lambda g, i: (g, i, 0))
    sspec = pl.BlockSpec((1, pg), lambda g, i: (0, g))
    return _call(
        body, name="pool_mm_bwd", grid=(ng, nt),
        in_specs=[cspec, cspec, pl.BlockSpec((N_CHIPS, None, pq, pg), lambda g, i: (0, g, 0, 0)), sspec, ANY_SPEC],
        out_specs=[cspec, pl.BlockSpec((N_CHIPS, pq, pg), lambda g, i: (0, row0 // pq + g // 2, g % 2)), sspec],
        out_shape=[jax.ShapeDtypeStruct((nb, rows, LANE), F32), jax.ShapeDtypeStruct(gbuf.shape, F32),
                   jax.ShapeDtypeStruct((1, ng * pg), F32)],
        scratch_shapes=[pltpu.VMEM((pg, pg), F32)],
        input_output_aliases={4: 1},
    )(dy, dm, wp, scale, gbuf)


def _adamw_math(w, g, m, v):
    nm = ADAM_B1 * m + (1.0 - ADAM_B1) * g
    nv = ADAM_B2 * v + (1.0 - ADAM_B2) * jnp.square(g)
    m_hat = nm / (1.0 - ADAM_B1 ** ADAM_STEP)
    v_hat = nv / (1.0 - ADAM_B2 ** ADAM_STEP)
    return -ADAM_LR * (m_hat / (jnp.sqrt(v_hat) + ADAM_EPS) + ADAM_WD * w), nm, nv


def _adamw_param(w3, m3, v3, gsrcs, pick, tm, name, after=None):
    n_blk, rows, cols = w3.shape
    ng = len(gsrcs)

    def body(*refs):
        w_ref, m_ref, v_ref = refs[:3]
        g_refs = refs[3:3 + ng]
        go_ref, d_ref, nm_ref, nv_ref = refs[-4:]
        g = pick(pl.program_id(0), [r[...] for r in g_refs])
        go_ref[...] = g
        d_ref[...], nm_ref[...], nv_ref[...] = _adamw_math(w_ref[...], g, m_ref[...], v_ref[...])

    spec = pl.BlockSpec((None, tm, cols), lambda n, i: (n, i, 0))
    extra = [] if after is None else [after]
    return _call(
        body, name=name, grid=(n_blk, rows // tm),
        in_specs=[spec] * 3 + [pl.BlockSpec(shape, imap) for _, shape, imap in gsrcs] + [ANY_SPEC] * len(extra),
        out_specs=[spec] * 4, out_shape=[jax.ShapeDtypeStruct(w3.shape, F32)] * 4,
    )(w3, m3, v3, *[a for a, _, _ in gsrcs], *extra)


def _adamw_small(quads):
    n = len(quads)

    def body(*refs):
        ins, outs = refs[:4 * n], refs[4 * n:]
        for k in range(n):
            w, g, m, v = (r[...] for r in ins[4 * k:4 * k + 4])
            outs[3 * k][...], outs[3 * k + 1][...], outs[3 * k + 2][...] = _adamw_math(w, g, m, v)

    flat = [a for q in quads for a in q]
    res = _call(body, name="adamw_small", grid=(1,),
                in_specs=[pl.BlockSpec(a.shape, lambda i: (0, 0)) for a in flat],
                out_specs=[pl.BlockSpec(q[0].shape, lambda i: (0, 0)) for q in quads for _ in range(3)],
                out_shape=[jax.ShapeDtypeStruct(q[0].shape, F32) for q in quads for _ in range(3)])(*flat)
    return [tuple(res[3 * k:3 * k + 3]) for k in range(n)]


def _place():
    return lax.axis_index("x"), lax.axis_index("y"), lax.axis_index("c")


def _other_chips(x, y):
    return [(1 - x, y), (x, 1 - y), (1 - x, 1 - y)]


def _own_slab(a, devices=False):
    x, y, c = _place()
    n, me = (N_DEV, 4 * x + 2 * y + c) if devices else (N_CHIPS, 2 * x + y)
    return lax.dynamic_update_slice(lax.empty((n,) + a.shape, a.dtype), a[None], (me, 0, 0))


def _gather_chips(arrays, name):
    n = len(arrays)
    halves = [a.shape[0] // 2 for a in arrays]
    for a, h in zip(arrays, halves):
        assert 2 * h == a.shape[0] and h % (32 // a.dtype.itemsize) == 0
    lands = [_own_slab(a) for a in arrays]

    def body(*refs):
        outs = refs[n:2 * n]
        send_sems, recv_sems = refs[2 * n:]
        x, y, c = _place()
        me = 2 * x + y
        chips = _other_chips(x, y)

        def mine(k):
            return pl.ds(c * halves[k], halves[k])

        def theirs(k):
            return pl.ds((1 - c) * halves[k], halves[k])

        def push(k, j, src, dst, to):
            return pltpu.make_async_remote_copy(src_ref=src, dst_ref=dst, send_sem=send_sems.at[6 * k + j],
                                                recv_sem=recv_sems.at[6 * k + j], device_id=to, device_id_type=MESH)

        started = []
        for j, (cx, cy) in enumerate(chips):
            for k in range(n):
                own = outs[k].at[me, mine(k)]
                cp = push(k, j, own, own, (cx, cy, c))
                cp.start()
                started.append(cp)
        for j, (cx, cy) in enumerate(chips):
            for k in range(n):
                slab = outs[k].at[2 * cx + cy, mine(k)]
                push(k, j, slab, slab, (x, y, c)).wait_recv()
                fwd = push(k, 3 + j, slab, slab, (x, y, 1 - c))
                fwd.start()
                started.append(fwd)
        for j, (cx, cy) in enumerate(chips):
            for k in range(n):
                slab = outs[k].at[2 * cx + cy, theirs(k)]
                push(k, 3 + j, slab, slab, (x, y, c)).wait_recv()
        for cp in started:
            cp.wait_send()

    return _call(
        body, name=name, in_specs=[ANY_SPEC] * n, out_specs=[ANY_SPEC] * n,
        out_shape=[jax.ShapeDtypeStruct(a.shape, a.dtype) for a in lands],
        input_output_aliases={k: k for k in range(n)},
        scratch_shapes=[pltpu.SemaphoreType.DMA((6 * n,)), pltpu.SemaphoreType.DMA((6 * n,))],
    )(*lands)


HBM_SPEC = pl.BlockSpec(memory_space=pltpu.HBM)
SEM_SPEC = pl.BlockSpec(memory_space=pltpu.SEMAPHORE)
SIDE_EFFECT = pltpu.SideEffectType.DATAFLOW_SIDE_EFFECTING


def _n_peers(kind):
    return N_DEV - 1 if kind == "devices" else N_CHIPS - 1


def _push_copies(src_refs, land_refs, send_sems, recv_sems, kind):
    x, y, c = _place()
    if kind == "devices":
        me = 4 * x + 2 * y + c
        peers = [((me + j) % N_DEV, None) for j in range(1, N_DEV)]
        peers = [((to // 4, (to // 2) % 2, to % 2), None) for to, _ in peers]
    else:
        me = 2 * x + y
        peers = [((cx, cy, c), 2 * cx + cy) for cx, cy in _other_chips(x, y)]
    n = len(peers)
    copies = []
    for j, (dev, slab) in enumerate(peers):
        for k, (src, land) in enumerate(zip(src_refs, land_refs)):
            copies.append(pltpu.make_async_remote_copy(
                src_ref=src.at[slab] if kind == "slab" else src, dst_ref=land.at[me], send_sem=send_sems.at[n * k + j],
                recv_sem=recv_sems.at[n * k + j], device_id=dev, device_id_type=MESH))
    return copies


def _push_start(srcs, lands, kind, after, name):
    n = len(srcs)

    def body(*refs):
        src_refs, land_refs = refs[:n], refs[n:2 * n]
        send_sems, recv_sems = refs[2 * n + 1], refs[2 * n + 2]
        token = refs[-1]
        for cp in _push_copies(src_refs, land_refs, send_sems, recv_sems, kind):
            cp.start()
        token[...] = jnp.zeros_like(token)

    bufs = [pltpu.with_memory_space_constraint(a, pltpu.HBM) for a in list(srcs) + list(lands)]
    res = _call(
        body, name=name,
        out_shape=[pltpu.SemaphoreType.DMA((_n_peers(kind) * n,)), pltpu.SemaphoreType.DMA((_n_peers(kind) * n,))]
        + [pltpu.HBM(a.shape, a.dtype) for a in bufs] + [jax.ShapeDtypeStruct((SUB, LANE), F32)],
        in_specs=[HBM_SPEC] * (2 * n) + [ANY_SPEC],
        out_specs=[SEM_SPEC, SEM_SPEC] + [HBM_SPEC] * (2 * n) + [pl.BlockSpec(memory_space=pltpu.VMEM)],
        input_output_aliases={i: 2 + i for i in range(2 * n)},
        compiler_params=pltpu.CompilerParams(has_side_effects=SIDE_EFFECT),
    )(*bufs, after)
    return res[0], res[1], list(res[2:2 + n]), list(res[2 + n:2 + 2 * n]), res[-1]


def _push_wait(send_sems, recv_sems, srcs, lands, kind, after, name):
    n = len(srcs)

    def body(*refs):
        src_refs, land_refs = refs[:n], refs[n:2 * n]
        send_sems, recv_sems = refs[2 * n], refs[2 * n + 1]
        for cp in _push_copies(src_refs, land_refs, send_sems, recv_sems, kind):
            cp.wait_send()
            cp.wait_recv()

    res = _call(
        body, name=name,
        out_shape=[pltpu.HBM(a.shape, a.dtype) for a in list(srcs) + list(lands)],
        in_specs=[HBM_SPEC] * (2 * n) + [SEM_SPEC, SEM_SPEC, ANY_SPEC],
        out_specs=[HBM_SPEC] * (2 * n),
        input_output_aliases={i: i for i in range(2 * n)},
        compiler_params=pltpu.CompilerParams(has_side_effects=SIDE_EFFECT),
    )(*srcs, *lands, send_sems, recv_sems, after)
    return list(res[n:])


def _sibling_swap(g):
    _, rows, w = g.shape
    half = rows // 2

    def body(g_ref, out_ref, send_sem, recv_sem):
        x, y, c = _place()
        cp = pltpu.make_async_remote_copy(src_ref=g_ref.at[:, pl.ds((1 - c) * half, half)], dst_ref=out_ref,
                                          send_sem=send_sem, recv_sem=recv_sem, device_id=(x, y, 1 - c), device_id_type=MESH)
        cp.start()
        cp.wait()

    return _call(body, name="rs_sibling_swap", in_specs=[ANY_SPEC], out_specs=ANY_SPEC,
                 out_shape=jax.ShapeDtypeStruct((N_CHIPS, half, w), F32),
                 scratch_shapes=[pltpu.SemaphoreType.DMA, pltpu.SemaphoreType.DMA])(g)


def _pair_add(g, got, place):
    _, rows, w = g.shape
    half = rows // 2
    tm = _row_tile(half, RS_TILE)
    nt = half // tm

    def body(p_ref, a_ref, b_ref, o_ref, own_ref):
        v = a_ref[...] + b_ref[...]
        o_ref[...] = v.astype(BF16)

        @pl.when(pl.program_id(1) == p_ref[1])
        def _():
            own_ref[...] = v

    return _call(
        body, name="rs_pair_add",
        grid_spec=pltpu.PrefetchScalarGridSpec(
            num_scalar_prefetch=1, grid=(nt, N_CHIPS),
            in_specs=[pl.BlockSpec((None, tm, w), lambda i, s, p: (s, p[0] * nt + i, 0)),
                      pl.BlockSpec((None, tm, w), lambda i, s, p: (s, i, 0))],
            out_specs=[pl.BlockSpec((None, tm, w), lambda i, s, p: (s, i, 0)),
                       pl.BlockSpec((tm, w), lambda i, s, p: (i, 0))]),
        out_shape=[jax.ShapeDtypeStruct((N_CHIPS, half, w), BF16), jax.ShapeDtypeStruct((half, w), F32)],
    )(place, g, got)


def _sum_chips(parts, own, place):
    _, half, w = parts.shape
    tm = _row_tile(half, RS_TILE)
    nt = half // tm

    def body(p_ref, parts_ref, own_ref, o_ref):
        me = p_ref[1]
        t = [jnp.where(me == q, own_ref[...], parts_ref[q].astype(F32)) for q in range(N_CHIPS)]
        o_ref[...] = (t[0] + t[1]) + (t[2] + t[3])

    return _call(
        body, name="rs_sum_chips",
        grid_spec=pltpu.PrefetchScalarGridSpec(
            num_scalar_prefetch=1, grid=(nt,),
            in_specs=[pl.BlockSpec((N_CHIPS, tm, w), lambda i, p: (0, i, 0)), pl.BlockSpec((tm, w), lambda i, p: (i, 0))],
            out_specs=pl.BlockSpec((tm, w), lambda i, p: (p[0] * nt + i, 0))),
        out_shape=jax.ShapeDtypeStruct((2 * half, w), F32),
    )(place, parts, own)


def _sibling_gather(red):
    rows, w = red.shape
    half = rows // 2

    def body(in_ref, out_ref, send_sem, recv_sem):
        x, y, c = _place()
        mine = out_ref.at[pl.ds(c * half, half)]
        cp = pltpu.make_async_remote_copy(src_ref=mine, dst_ref=mine, send_sem=send_sem, recv_sem=recv_sem,
                                          device_id=(x, y, 1 - c), device_id_type=MESH)
        cp.start()
        other = out_ref.at[pl.ds((1 - c) * half, half)]
        pltpu.make_async_remote_copy(src_ref=other, dst_ref=other, send_sem=send_sem, recv_sem=recv_sem,
                                     device_id=(x, y, c), device_id_type=MESH).wait_recv()
        cp.wait_send()

    return _call(body, name="rs_sibling_gather", in_specs=[ANY_SPEC], out_specs=ANY_SPEC,
                 out_shape=jax.ShapeDtypeStruct(red.shape, F32), input_output_aliases={0: 0},
                 scratch_shapes=[pltpu.SemaphoreType.DMA, pltpu.SemaphoreType.DMA])(red)


def _rs_begin(g, place, name):
    pair, own = _pair_add(g, _sibling_swap(g), place)
    send, recv, pair, parts, token = _push_start([pair], [jnp.zeros_like(pair)], "slab", own, name + "_start")
    return (send, recv, pair, parts, own), token


def _rs_end(state, place, after, name):
    send, recv, pair, parts, own = state
    (parts,) = _push_wait(send, recv, pair, parts, "slab", after, name + "_wait")
    return _sibling_gather(_sum_chips(parts, own, place))


WEIGHTS = ("c_ctx", "w_mod", "b_mod", "w_in", "w_out", "ln_g", "ln_b", "conv_w", "conv_b", "lru_wa", "lru_ba", "lru_wx",
           "lru_bx", "lru_lam", "pool_w", "pool_scale")
SMALL_GATHERED = ("conv_w", "lru_ba", "lru_bx", "lru_lam", "pool_scale")
SMALL_UPDATED = ("c_ctx", "b_mod", "ln_g", "ln_b", "conv_w", "conv_b", "lru_ba", "lru_bx", "lru_lam", "pool_scale")


def kernel(x, c, ctx, c_ctx, w_mod, b_mod, w_in, w_out, ln_g, ln_b, conv_w, conv_b, lru_wa, lru_ba, lru_wx, lru_bx, lru_lam, pool_w, pool_scale, loss_target, m_c_ctx, m_w_mod, m_b_mod, m_w_in, m_w_out, m_ln_g, m_ln_b, m_conv_w, m_conv_b, m_lru_wa, m_lru_ba, m_lru_wx, m_lru_bx, m_lru_lam, m_pool_w, m_pool_scale, v_c_ctx, v_w_mod, v_b_mod, v_w_in, v_w_out, v_ln_g, v_ln_b, v_conv_w, v_conv_b, v_lru_wa, v_lru_ba, v_lru_wx, v_lru_bx, v_lru_lam, v_pool_w, v_pool_scale):
    weights = dict(c_ctx=c_ctx, w_mod=w_mod, b_mod=b_mod, w_in=w_in, w_out=w_out, ln_g=ln_g, ln_b=ln_b, conv_w=conv_w,
                   conv_b=conv_b, lru_wa=lru_wa, lru_ba=lru_ba, lru_wx=lru_wx, lru_bx=lru_bx, lru_lam=lru_lam,
                   pool_w=pool_w, pool_scale=pool_scale)
    mom1 = dict(c_ctx=m_c_ctx, w_mod=m_w_mod, b_mod=m_b_mod, w_in=m_w_in, w_out=m_w_out, ln_g=m_ln_g, ln_b=m_ln_b,
                conv_w=m_conv_w, conv_b=m_conv_b, lru_wa=m_lru_wa, lru_ba=m_lru_ba, lru_wx=m_lru_wx, lru_bx=m_lru_bx,
                lru_lam=m_lru_lam, pool_w=m_pool_w, pool_scale=m_pool_scale)
    mom2 = dict(c_ctx=v_c_ctx, w_mod=v_w_mod, b_mod=v_b_mod, w_in=v_w_in, w_out=v_w_out, ln_g=v_ln_g, ln_b=v_ln_b,
                conv_w=v_conv_w, conv_b=v_conv_b, lru_wa=v_lru_wa, lru_ba=v_lru_ba, lru_wx=v_lru_wx, lru_bx=v_lru_bx,
                lru_lam=v_lru_lam, pool_w=v_pool_w, pool_scale=v_pool_scale)
    xs, cx, target = x[0], ctx[0], loss_target[0]
    s_len, d = xs.shape
    es = w_out.shape[1]
    e = es * N_CHIPS
    nb = e // LANE
    c3 = w_mod.shape[2]
    n4 = w_in.shape[2]
    pq, pg = pool_w.shape[2], pool_w.shape[3]
    ng = len(POOL_WINDOWS)
    width = n4
    assert width == d and 2 * pg == width and 2 * nb * LANE == N_CHIPS * width and d % (2 * N_CHIPS) == 0
    px, py, pc = _place()
    place = jnp.stack([pc, 2 * px + py]).astype(jnp.int32)
    cctx2 = c_ctx[None, :]

    eq = e // N_CHIPS
    small_rows = [(conv_w[0], 0), (lru_ba[0], CONV_TAPS), (lru_bx[0], CONV_TAPS + 2), (lru_lam[0], CONV_TAPS + 4),
                  (pool_scale, CONV_TAPS + 6)]
    small = _rows_kernel([(a, r, 0) for a, r in small_rows], 2 * SUB, eq, "pack_small_weights")
    win0, sg = _gather_chips([w_in[0].astype(BF16), small], "gather_weights0")
    full = {n: jnp.swapaxes(sg[:, r:r + a.shape[0]], 0, 1).reshape(a.shape[0], e)
            for n, (a, r) in zip(SMALL_GATHERED, small_rows)}
    wa_h, wx_h = (0.5 * lru_wa[0]).astype(BF16), (0.5 * lru_wx[0]).astype(BF16)
    lru_args = (full["conv_w"], conv_b, wa_h, wx_h, 0.5 * full["lru_ba"], 0.5 * full["lru_bx"], full["lru_lam"])
    scale_f = full["pool_scale"]

    me8 = 4 * px + 2 * py + pc
    c_rows = _rows_kernel([(c, 0, 0)], SUB, d, "pack_c")
    c_send, c_recv, c_src, c_lands, c_token = _push_start([c_rows], [_own_slab(c_rows, devices=True)], "devices", sg,
                                                          "gather_c_start")
    (c_dev,) = _push_wait(c_send, c_recv, c_src, c_lands, "devices", c_token, "gather_c_wait")
    wm_mine = w_mod.astype(BF16)
    bm_mine = lax.dynamic_slice_in_dim(b_mod[:, None, :], place[1] * c3, c3, axis=2)
    (mod_g,) = _gather_chips([_mod_fwd(c_dev[:, 0, :], cctx2, wm_mine, bm_mine).reshape(DEPTH * 2 * SUB, c3)], "gather_mod")
    mod_all = jnp.transpose(mod_g.reshape(N_CHIPS, DEPTH, 2 * SUB, c3), (1, 2, 0, 3)).reshape(DEPTH, 2 * SUB, 3 * d)
    mod_mine = lax.dynamic_index_in_dim(mod_all, me8, axis=1, keepdims=False)
    later = [w_out[0].astype(BF16), w_in[1].astype(BF16), w_out[1].astype(BF16), pool_w.astype(BF16).reshape(ng * pq, pg)]
    w_send, w_recv, later, later_lands, w_token = _push_start(
        later, [_own_slab(a) for a in later], "same", mod_g, "gather_weights1_start")
    mod_mine = mod_mine + w_token[0:1, 0:1]

    def mod_parts(v):
        return v[None, :d], 1.0 + v[None, d:2 * d], v[None, 2 * d:]

    sh0, sc0, gt0 = mod_parts(mod_mine[0])
    shc, scc, _ = mod_parts(mod_all[0, N_DEV])
    sh1, sc1, gt1 = mod_parts(mod_mine[1])
    lg = [ln_g[l][None, :] for l in range(DEPTH)]
    lb = [ln_b[l][None, :] for l in range(DEPTH)]

    uu0, ug0 = _inproj_fwd(xs, sc0, sh0, win0, "inproj_fwd0")
    uc0 = _inproj_fwd(cx, scc, shc, win0[:2], "inproj_fwd_ctx")
    y0 = _rglru_fwd(uu0, uc0, *lru_args)
    wout0_g, win1, wout1_g, wp_g = _push_wait(w_send, w_recv, later, later_lands, "same", y0, "gather_weights1_wait")
    win = [win0, win1]
    wout = [wout0_g.reshape(e, d), wout1_g.reshape(e, d)]
    wp = wp_g.reshape(N_CHIPS, ng, pq, pg)
    br0, x1 = _outproj_fwd(y0, ug0, xs, gt0, wout[0], lg[0], lb[0], None, "outproj_fwd0")
    uu1, ug1 = _inproj_fwd(x1, sc1, sh1, win[1], "inproj_fwd1")
    d1 = _pool_map(uu1, nb, False, True, "pool_fwd")
    y1 = _pool_mm_fwd(d1, wp, scale_f)
    br1, dxo, loss_part = _outproj_fwd(y1, ug1, x1, gt1, wout[1], lg[1], lb[1], target, "outproj_fwd1")

    row_wout = d
    row_tail = d + es
    wq = 2 * (nb // N_CHIPS) * LANE * LANE // width
    whole = lambda r: (r + 2 * RS_TILE - 1) // (2 * RS_TILE) * (2 * RS_TILE)
    rows1 = whole(row_tail + pg // 2)
    rows0 = whole(row_tail + 2 * wq)
    fresh = lambda rows, used: (lax.empty if rows == used else jnp.zeros)((N_CHIPS, rows, width), F32)
    gbuf1 = fresh(rows1, row_tail + pg // 2)
    gbuf0 = fresh(rows0, row_tail + 2 * wq)

    dy1, dg1, dxres1, dbr1, dlg1, dlb1, dgt1 = _outproj_bwd(dxo, x1, br1, y1, ug1, gt1, lg[1], wout[1], "outproj_bwd1",
                                                            dy_dtype=BF16)
    gbuf1 = _outproj_bwd_w(y1, ug1, dbr1, gbuf1, row_wout, "outproj_bwd_w1")
    dd1, gbuf1, dscale = _pool_mm_bwd(dy1, d1, wp, scale_f, gbuf1, row_tail)
    du1 = _pool_map(dd1, nb, True, False, "pool_bwd")
    dx1, dsc1, dsh1 = _inproj_bwd_x([du1, dg1], x1, dxres1, sc1, win[1], "inproj_bwd_x1")
    gbuf1 = _inproj_bwd_w(x1, sc1, sh1, [du1, dg1], None, gbuf1, "inproj_bwd_w1")
    rs1, token1 = _rs_begin(gbuf1, place, "rs_exchange1")

    dy0, dg0, dxres0, dbr0, dlg0, dlb0, dgt0 = _outproj_bwd(dx1, xs, br0, y0, ug0, gt0 + token1[0:1, 0:1], lg[0], wout[0],
                                                            "outproj_bwd0")
    gbuf0 = _outproj_bwd_w(y0, ug0, dbr0, gbuf0, row_wout, "outproj_bwd_w0")
    du0, duc, dconv_w, dconv_b, dwa, dwx, dba, dbx, dlam = _rglru_bwd(uu0, uc0, dy0, *lru_args)
    dwin0c = _inproj_bwd_w(cx, scc, shc, [duc, jnp.zeros_like(duc)], None, None, "inproj_bwd_w_ctx")
    gbuf0 = _inproj_bwd_w(xs, sc0, sh0, [du0, dg0], dwin0c, gbuf0, "inproj_bwd_w0")

    def quarter(dw):
        t = dw.reshape(2, N_CHIPS, nb // N_CHIPS, LANE, LANE)
        return jnp.transpose(t, (1, 3, 0, 2, 4)).reshape(N_CHIPS, LANE, 2 * (nb // N_CHIPS) * LANE).reshape(N_CHIPS, wq, width)

    tail0 = jnp.concatenate([quarter(dwa), quarter(dwx)], axis=1)
    gbuf0 = lax.dynamic_update_slice(gbuf0, tail0, (0, row_tail, 0))
    red1 = _rs_end(rs1, place, gbuf0, "rs_exchange1")
    rs0, token0 = _rs_begin(gbuf0, place, "rs_exchange0")
    grad_x, dsc0, dsh0 = _inproj_bwd_x([du0, dg0], xs, dxres0, sc0 + token0[0:1, 0:1], win[0], "inproj_bwd_x0")
    dscc, dshc = _inproj_bwd_x([duc], cx, None, scc, win[0][:2], "inproj_bwd_x_ctx")

    k0 = VEC_KINDS
    vec = _rows_kernel(
        [(c, 0, 0), (loss_part, 0, d), (dsh0, 1, 0), (dsc0, 1, d), (dgt0, 1, 2 * d), (dshc, 2, 0), (dscc, 2, d),
         (dsh1, 3, 0), (dsc1, 3, d), (dgt1, 3, 2 * d),
         (dconv_b, k0, 0), (dlg0, k0, e), (dscale, k0 + 1, 0), (dlg1, k0 + 1, e), (dlb0, k0 + 2, 0), (dlb1, k0 + 2, d),
         (dconv_w, k0 + 3, 0), (dba, k0 + 7, 0), (dbx, k0 + 9, 0), (dlam, k0 + 11, 0)], VEC_ROWS, 3 * d, "pack_vec")
    v_send, v_recv, vec_l, vec_lands, v_token = _push_start([vec], [_own_slab(vec, devices=True)], "devices", vec,
                                                            "gather_devices_start")
    red0 = _rs_end(rs0, place, v_token, "rs_exchange0")
    quarters = red0[row_tail:row_tail + 2 * wq]
    q_send, q_recv, q_src, q_lands, q_token = _push_start([quarters], [_own_slab(quarters)], "same", red0,
                                                          "gather_replicated_start")

    tmw = _row_tile(d, 256)
    red_src = lambda red, r0, tm: (red, (tm, width), lambda n, i: (r0 // tm + i, 0))
    by_layer = lambda n, gs: jnp.where(n == 0, gs[0], gs[1])
    outs = {}
    outs["w_in"] = _adamw_param(w_in, m_w_in, v_w_in, [red_src(red0, 0, tmw), red_src(red1, 0, tmw)], by_layer, tmw, "adamw_w_in",
                                after=q_token)
    outs["w_out"] = _adamw_param(w_out, m_w_out, v_w_out, [red_src(red0, row_wout, tmw), red_src(red1, row_wout, tmw)],
                                 by_layer, tmw, "adamw_w_out")
    pw = [a.reshape(ng, pq, pg) for a in (pool_w, m_pool_w, v_pool_w)]
    outs["pool_w"] = [o.reshape(pool_w.shape) for o in _adamw_param(
        *pw, [(red1, (pq, pg), lambda n, i: (row_tail // pq + n // 2, n % 2))], lambda n, gs: gs[0], pq, "adamw_pool_w")]

    (gathered,) = _push_wait(v_send, v_recv, vec_l, vec_lands, "devices", outs["w_out"][1], "gather_devices_wait")
    gt_all = jnp.swapaxes(gathered, 0, 1)
    g_wmod = _mod_bwd_shard(gt_all, cctx2, place, c3)
    g_bmod, sq_err, g_small = _mod_bwd_rep(gt_all, d)
    loss = sq_err[0, 0] * (0.5 / d)
    cpart = _cctx_partial(gt_all, wm_mine[0], place)
    x_send, x_recv, x_src, x_lands, x_token = _push_start([cpart], [_own_slab(cpart)], "same", cpart, "gather_cctx_start")
    outs["w_mod"] = _adamw_param(w_mod, m_w_mod, v_w_mod, [(g_wmod, (None, tmw, c3), lambda n, i: (n, i, 0))],
                                 lambda n, gs: gs[0], tmw, "adamw_w_mod", after=x_token)
    (rep,) = _push_wait(q_send, q_recv, q_src, q_lands, "same", outs["w_mod"][1], "gather_replicated_wait")
    bq = nb // N_CHIPS
    rep_src = lambda r0: (rep, (None, LANE, bq * LANE), lambda n, i: (n % N_CHIPS, r0 // LANE, n // N_CHIPS))
    stack = lambda n, gs: jnp.concatenate([gs[0][:, k * LANE:(k + 1) * LANE] for k in range(bq)], axis=0)
    for name, r0, trio in (("lru_wa", 0, (lru_wa, m_lru_wa, v_lru_wa)), ("lru_wx", wq, (lru_wx, m_lru_wx, v_lru_wx))):
        blocks = [a.reshape(2 * N_CHIPS, bq * LANE, LANE) for a in trio]
        outs[name] = [o.reshape(lru_wa.shape) for o in _adamw_param(*blocks, [rep_src(r0)], stack, bq * LANE, "adamw_" + name)]

    (cparts,) = _push_wait(x_send, x_recv, x_src, x_lands, "same", outs["lru_wx"][1], "gather_cctx_wait")
    g_small = dict(g_small, c_ctx=_cctx_finish(cparts, cctx2), b_mod=g_bmod)
    for n in SMALL_GATHERED:
        g_small[n] = lax.dynamic_slice_in_dim(g_small[n], place[1] * eq, eq, axis=1)
    as2d = lambda a: a.reshape(-1, a.shape[-1])
    quads = [(as2d(weights[n]), g_small[n], as2d(mom1[n]), as2d(mom2[n])) for n in SMALL_UPDATED]
    for n, (q, res) in zip(SMALL_UPDATED, zip(quads, _adamw_small(quads))):
        outs[n] = [a.reshape(weights[n].shape) for a in (q[1],) + res]

    result = [loss, grad_x[None]]
    for j in range(4):
        result += [outs[n][j] for n in WEIGHTS]
    return tuple(result)
```
